```python
import math
import jax, jax.numpy as jnp
from jax import lax
import numpy as np

D_MODEL = 1024
BATCH = 16
SEQ = 4096
DEPTH = 1

D_MIX = D_MODEL
D_SSM = D_MIX // 2
SSM_GROUP = 16
N_SSM_GROUPS = D_SSM // SSM_GROUP
SSM_STATE = 64
D_ATTN = D_MIX - D_SSM
N_HEADS = 8
QK_NOPE = 64
QK_ROPE = 32
V_HEAD = D_ATTN // N_HEADS
Q_LORA = 384
KV_LORA = 256
IN_COLS = D_SSM + Q_LORA + KV_LORA + QK_ROPE
D_FF = 4 * D_MODEL
ROPE_BASE = 10000.0
Q_BLOCK = 128
EPS = 1e-6
DT_MIN = 1e-3
DT_MAX = 1e-1
N_MOD = 6

kernel_name = "hymba_s5_mla_adaln_block"


def rmsnorm(x, g):
    xf = x.astype(jnp.float32)
    y = xf * lax.rsqrt(jnp.mean(xf * xf, axis=-1, keepdims=True) + EPS)
    return (y * g.astype(jnp.float32)).astype(x.dtype)


def rope_tables(positions):
    inv_freq = ROPE_BASE ** (-jnp.arange(0, QK_ROPE, 2, dtype=jnp.float32) / QK_ROPE)
    ang = positions.astype(jnp.float32)[..., None] * inv_freq
    return jnp.cos(ang), jnp.sin(ang)


def apply_rope(x, cos, sin):
    xf = x.astype(jnp.float32)
    x1, x2 = jnp.split(xf, 2, axis=-1)
    out = jnp.concatenate([x1 * cos - x2 * sin, x1 * sin + x2 * cos], axis=-1)
    return out.astype(x.dtype)


def s5_mixer(u, lam_re, lam_im, b_re, b_im, c_re, c_im, d, log_dt, w_glu):
    f32 = jnp.float32
    bsz, seq, _ = u.shape
    uf = u.astype(f32).reshape(bsz, seq, N_SSM_GROUPS, SSM_GROUP)
    lam = lax.complex(lam_re.astype(f32), lam_im.astype(f32))
    dt = jnp.exp(log_dt.astype(f32))[:, None]
    lam_bar = jnp.exp(lam * dt)
    b = lax.complex(b_re.astype(f32), b_im.astype(f32))
    b_bar = ((lam_bar - 1.0) / lam)[..., None] * b
    bu = jnp.einsum("bsgh,gph->bsgp", uf, b_bar)
    a = jnp.broadcast_to(lam_bar, (1, seq) + lam_bar.shape)

    def combine(left, right):
        a_l, b_l = left
        a_r, b_r = right
        return a_r * a_l, a_r * b_l + b_r

    _, states = lax.associative_scan(combine, (a, bu), axis=1)
    y = (jnp.einsum("bsgp,ghp->bsgh", jnp.real(states), c_re.astype(f32))
         - jnp.einsum("bsgp,ghp->bsgh", jnp.imag(states), c_im.astype(f32))
         + d.astype(f32) * uf)
    y = jax.nn.gelu(y).reshape(bsz, seq, D_SSM).astype(u.dtype)
    z = y @ w_glu
    return z[..., :D_SSM] * jax.nn.sigmoid(z[..., D_SSM:])


def causal_block_attention(q_nope, q_rope, k_nope, k_rope, v):
    bsz, seq = q_nope.shape[:2]
    n_blocks = seq // Q_BLOCK
    scale = (QK_NOPE + QK_ROPE) ** -0.5
    key_pos = jnp.arange(seq)

    def one_block(i):
        start = i * Q_BLOCK
        qn = lax.dynamic_slice_in_dim(q_nope, start, Q_BLOCK, axis=1)
        qr = lax.dynamic_slice_in_dim(q_rope, start, Q_BLOCK, axis=1)
        s = (jnp.einsum("bqhd,bkhd->bhqk", qn, k_nope)
             + jnp.einsum("bqhr,bkr->bhqk", qr, k_rope)).astype(jnp.float32) * scale
        q_pos = start + jnp.arange(Q_BLOCK)
        mask = key_pos[None, :] <= q_pos[:, None]
        s = jnp.where(mask, s, -jnp.inf)
        p = jax.nn.softmax(s, axis=-1).astype(v.dtype)
        return jnp.einsum("bhqk,bkhd->bqhd", p, v)

    out = lax.map(one_block, jnp.arange(n_blocks))
    return out.transpose(1, 0, 2, 3, 4).reshape(bsz, seq, N_HEADS, V_HEAD)


def mla_mixer(q_lat, kv_lat, k_rope, cos, sin, q_norm_g, w_uq, kv_norm_g, w_ukv):
    bsz, seq, _ = q_lat.shape
    q = (rmsnorm(q_lat, q_norm_g) @ w_uq).reshape(bsz, seq, N_HEADS, QK_NOPE + QK_ROPE)
    kv = (rmsnorm(kv_lat, kv_norm_g) @ w_ukv).reshape(bsz, seq, N_HEADS, QK_NOPE + V_HEAD)
    q_nope, q_rope = q[..., :QK_NOPE], q[..., QK_NOPE:]
    k_nope, v = kv[..., :QK_NOPE], kv[..., QK_NOPE:]
    q_rope = apply_rope(q_rope, cos[:, :, None, :], sin[:, :, None, :])
    k_rope = apply_rope(k_rope, cos, sin)
    out = causal_block_attention(q_nope, q_rope, k_nope, k_rope, v)
    return out.reshape(bsz, seq, D_ATTN)


def _fwd_setup_inputs(seed: int = 0) -> dict:
    key = jax.random.key(seed)
    ks = jax.random.split(key, 32)
    f32 = jnp.float32
    L = DEPTH

    def nrm(k, shape, scale):
        return jax.random.normal(k, shape, f32) * scale

    def gain(k, shape):
        return 1.0 + 0.02 * jax.random.normal(k, shape, f32)

    x = jax.random.normal(ks[0], (BATCH, SEQ, D_MODEL), f32)
    c = jax.random.normal(ks[1], (BATCH, D_MODEL), f32)
    offset = jax.random.randint(ks[2], (BATCH, 1), 0, 2048, dtype=jnp.int32)
    positions = offset + jnp.arange(SEQ, dtype=jnp.int32)[None, :]

    n_idx = jnp.arange(SSM_STATE, dtype=f32)
    lam_re = -0.5 * jnp.exp(0.01 * jax.random.normal(ks[3], (L, N_SSM_GROUPS, SSM_STATE), f32))
    lam_im = math.pi * n_idx + 0.01 * jax.random.normal(ks[4], (L, N_SSM_GROUPS, SSM_STATE), f32)
    log_dt = jax.random.uniform(ks[5], (L, N_SSM_GROUPS), f32, math.log(DT_MIN), math.log(DT_MAX))

    return {
        "x": x,
        "c": c,
        "positions": positions,
        "ada_w": nrm(ks[6], (L, D_MODEL, N_MOD * D_MODEL), 0.5 * D_MODEL ** -0.5),
        "ada_b": nrm(ks[7], (L, N_MOD * D_MODEL), 0.02),
        "norm1_g": gain(ks[8], (L, D_MODEL)),
        "w_in": nrm(ks[9], (L, D_MODEL, IN_COLS), D_MODEL ** -0.5),
        "ssm_lambda_re": lam_re,
        "ssm_lambda_im": lam_im,
        "ssm_b_re": nrm(ks[10], (L, N_SSM_GROUPS, SSM_STATE, SSM_GROUP), (2 * SSM_GROUP) ** -0.5),
        "ssm_b_im": nrm(ks[11], (L, N_SSM_GROUPS, SSM_STATE, SSM_GROUP), (2 * SSM_GROUP) ** -0.5),
        "ssm_c_re": nrm(ks[12], (L, N_SSM_GROUPS, SSM_GROUP, SSM_STATE), (2 * SSM_STATE) ** -0.5),
        "ssm_c_im": nrm(ks[13], (L, N_SSM_GROUPS, SSM_GROUP, SSM_STATE), (2 * SSM_STATE) ** -0.5),
        "ssm_d": nrm(ks[14], (L, N_SSM_GROUPS, SSM_GROUP), 1.0),
        "ssm_log_dt": log_dt,
        "w_glu": nrm(ks[15], (L, D_SSM, 2 * D_SSM), D_SSM ** -0.5),
        "q_norm_g": gain(ks[16], (L, Q_LORA)),
        "w_uq": nrm(ks[17], (L, Q_LORA, N_HEADS * (QK_NOPE + QK_ROPE)), Q_LORA ** -0.5),
        "kv_norm_g": gain(ks[18], (L, KV_LORA)),
        "w_ukv": nrm(ks[19], (L, KV_LORA, N_HEADS * (QK_NOPE + V_HEAD)), KV_LORA ** -0.5),
        "ssm_out_g": gain(ks[20], (L, D_SSM)),
        "attn_out_g": gain(ks[21], (L, D_ATTN)),
        "w_out": nrm(ks[22], (L, D_MIX, D_MODEL), D_MIX ** -0.5),
        "norm2_g": gain(ks[23], (L, D_MODEL)),
        "w_ff1": nrm(ks[24], (L, D_MODEL, D_FF), D_MODEL ** -0.5),
        "w_ff2": nrm(ks[25], (L, D_FF, D_MODEL), D_FF ** -0.5),
        "final_ada_w": nrm(ks[26], (D_MODEL, 2 * D_MODEL), 0.5 * D_MODEL ** -0.5),
        "final_ada_b": nrm(ks[27], (2 * D_MODEL,), 0.02),
        "final_norm_g": gain(ks[28], (D_MODEL,)),
    }


def _fwd_reference(x, c, positions, ada_w, ada_b, norm1_g, w_in, ssm_lambda_re, ssm_lambda_im,
              ssm_b_re, ssm_b_im, ssm_c_re, ssm_c_im, ssm_d, ssm_log_dt, w_glu,
              q_norm_g, w_uq, kv_norm_g, w_ukv, ssm_out_g, attn_out_g, w_out,
              norm2_g, w_ff1, w_ff2, final_ada_w, final_ada_b, final_norm_g):
    cond = jax.nn.silu(c)
    cos, sin = rope_tables(positions)
    s1 = D_SSM
    s2 = s1 + Q_LORA
    s3 = s2 + KV_LORA
    for l in range(DEPTH):
        mod = (cond @ ada_w[l] + ada_b[l])[:, None, :]
        shift1, scale1, gate1, shift2, scale2, gate2 = jnp.split(mod, N_MOD, axis=-1)

        h = rmsnorm(x, norm1_g[l]) * (1.0 + scale1) + shift1
        proj = h @ w_in[l]
        u = proj[..., :s1]
        q_lat = proj[..., s1:s2]
        kv_lat = proj[..., s2:s3]
        k_rope = proj[..., s3:]
        y_ssm = s5_mixer(u, ssm_lambda_re[l], ssm_lambda_im[l], ssm_b_re[l], ssm_b_im[l],
                         ssm_c_re[l], ssm_c_im[l], ssm_d[l], ssm_log_dt[l], w_glu[l])
        y_attn = mla_mixer(q_lat, kv_lat, k_rope, cos, sin, q_norm_g[l], w_uq[l],
                           kv_norm_g[l], w_ukv[l])
        y = jnp.concatenate([rmsnorm(y_ssm, ssm_out_g[l]), rmsnorm(y_attn, attn_out_g[l])], axis=-1)
        x = x + gate1 * (y @ w_out[l])

        h = rmsnorm(x, norm2_g[l]) * (1.0 + scale2) + shift2
        ff = jnp.square(jax.nn.relu(h @ w_ff1[l])) @ w_ff2[l]
        x = x + gate2 * ff

    fmod = (cond @ final_ada_w + final_ada_b)[:, None, :]
    fshift, fscale = jnp.split(fmod, 2, axis=-1)
    return rmsnorm(x, final_norm_g) * (1.0 + fscale) + fshift


import jax as _jax
import jax.numpy as _jnp

TWIN_FORMAT = 'train_step'
FWD_PARAMS = ['x', 'c', 'positions', 'ada_w', 'ada_b', 'norm1_g', 'w_in', 'ssm_lambda_re', 'ssm_lambda_im', 'ssm_b_re', 'ssm_b_im', 'ssm_c_re', 'ssm_c_im', 'ssm_d', 'ssm_log_dt', 'w_glu', 'q_norm_g', 'w_uq', 'kv_norm_g', 'w_ukv', 'ssm_out_g', 'attn_out_g', 'w_out', 'norm2_g', 'w_ff1', 'w_ff2', 'final_ada_w', 'final_ada_b', 'final_norm_g']
TWIN_WEIGHTS = ['ada_w', 'ada_b', 'norm1_g', 'w_in', 'ssm_lambda_re', 'ssm_lambda_im', 'ssm_b_re', 'ssm_b_im', 'ssm_c_re', 'ssm_c_im', 'ssm_d', 'ssm_log_dt', 'w_glu', 'q_norm_g', 'w_uq', 'kv_norm_g', 'w_ukv', 'ssm_out_g', 'attn_out_g', 'w_out', 'norm2_g', 'w_ff1', 'w_ff2', 'final_ada_w', 'final_ada_b', 'final_norm_g']
TWIN_DIFF_INPUT = 'x'
TWIN_INPUTS = ['x', 'c', 'positions', 'ada_w', 'ada_b', 'norm1_g', 'w_in', 'ssm_lambda_re', 'ssm_lambda_im', 'ssm_b_re', 'ssm_b_im', 'ssm_c_re', 'ssm_c_im', 'ssm_d', 'ssm_log_dt', 'w_glu', 'q_norm_g', 'w_uq', 'kv_norm_g', 'w_ukv', 'ssm_out_g', 'attn_out_g', 'w_out', 'norm2_g', 'w_ff1', 'w_ff2', 'final_ada_w', 'final_ada_b', 'final_norm_g', 'loss_target', 'm_ada_w', 'm_ada_b', 'm_norm1_g', 'm_w_in', 'm_ssm_lambda_re', 'm_ssm_lambda_im', 'm_ssm_b_re', 'm_ssm_b_im', 'm_ssm_c_re', 'm_ssm_c_im', 'm_ssm_d', 'm_ssm_log_dt', 'm_w_glu', 'm_q_norm_g', 'm_w_uq', 'm_kv_norm_g', 'm_w_ukv', 'm_ssm_out_g', 'm_attn_out_g', 'm_w_out', 'm_norm2_g', 'm_w_ff1', 'm_w_ff2', 'm_final_ada_w', 'm_final_ada_b', 'm_final_norm_g', 'v_ada_w', 'v_ada_b', 'v_norm1_g', 'v_w_in', 'v_ssm_lambda_re', 'v_ssm_lambda_im', 'v_ssm_b_re', 'v_ssm_b_im', 'v_ssm_c_re', 'v_ssm_c_im', 'v_ssm_d', 'v_ssm_log_dt', 'v_w_glu', 'v_q_norm_g', 'v_w_uq', 'v_kv_norm_g', 'v_w_ukv', 'v_ssm_out_g', 'v_attn_out_g', 'v_w_out', 'v_norm2_g', 'v_w_ff1', 'v_w_ff2', 'v_final_ada_w', 'v_final_ada_b', 'v_final_norm_g']
TWIN_OUTPUTS = ['loss', 'grad_x', 'grad_ada_w', 'grad_ada_b', 'grad_norm1_g', 'grad_w_in', 'grad_ssm_lambda_re', 'grad_ssm_lambda_im', 'grad_ssm_b_re', 'grad_ssm_b_im', 'grad_ssm_c_re', 'grad_ssm_c_im', 'grad_ssm_d', 'grad_ssm_log_dt', 'grad_w_glu', 'grad_q_norm_g', 'grad_w_uq', 'grad_kv_norm_g', 'grad_w_ukv', 'grad_ssm_out_g', 'grad_attn_out_g', 'grad_w_out', 'grad_norm2_g', 'grad_w_ff1', 'grad_w_ff2', 'grad_final_ada_w', 'grad_final_ada_b', 'grad_final_norm_g', 'delta_ada_w', 'delta_ada_b', 'delta_norm1_g', 'delta_w_in', 'delta_ssm_lambda_re', 'delta_ssm_lambda_im', 'delta_ssm_b_re', 'delta_ssm_b_im', 'delta_ssm_c_re', 'delta_ssm_c_im', 'delta_ssm_d', 'delta_ssm_log_dt', 'delta_w_glu', 'delta_q_norm_g', 'delta_w_uq', 'delta_kv_norm_g', 'delta_w_ukv', 'delta_ssm_out_g', 'delta_attn_out_g', 'delta_w_out', 'delta_norm2_g', 'delta_w_ff1', 'delta_w_ff2', 'delta_final_ada_w', 'delta_final_ada_b', 'delta_final_norm_g', 'new_m_ada_w', 'new_m_ada_b', 'new_m_norm1_g', 'new_m_w_in', 'new_m_ssm_lambda_re', 'new_m_ssm_lambda_im', 'new_m_ssm_b_re', 'new_m_ssm_b_im', 'new_m_ssm_c_re', 'new_m_ssm_c_im', 'new_m_ssm_d', 'new_m_ssm_log_dt', 'new_m_w_glu', 'new_m_q_norm_g', 'new_m_w_uq', 'new_m_kv_norm_g', 'new_m_w_ukv', 'new_m_ssm_out_g', 'new_m_attn_out_g', 'new_m_w_out', 'new_m_norm2_g', 'new_m_w_ff1', 'new_m_w_ff2', 'new_m_final_ada_w', 'new_m_final_ada_b', 'new_m_final_norm_g', 'new_v_ada_w', 'new_v_ada_b', 'new_v_norm1_g', 'new_v_w_in', 'new_v_ssm_lambda_re', 'new_v_ssm_lambda_im', 'new_v_ssm_b_re', 'new_v_ssm_b_im', 'new_v_ssm_c_re', 'new_v_ssm_c_im', 'new_v_ssm_d', 'new_v_ssm_log_dt', 'new_v_w_glu', 'new_v_q_norm_g', 'new_v_w_uq', 'new_v_kv_norm_g', 'new_v_w_ukv', 'new_v_ssm_out_g', 'new_v_attn_out_g', 'new_v_w_out', 'new_v_norm2_g', 'new_v_w_ff1', 'new_v_w_ff2', 'new_v_final_ada_w', 'new_v_final_ada_b', 'new_v_final_norm_g']
TWIN_LEAF_KINDS = {'loss': 'loss', 'grad_x': 'grad_x', 'grad_ada_w': 'grad_w', 'grad_ada_b': 'grad_w', 'grad_norm1_g': 'grad_w', 'grad_w_in': 'grad_w', 'grad_ssm_lambda_re': 'grad_w', 'grad_ssm_lambda_im': 'grad_w', 'grad_ssm_b_re': 'grad_w', 'grad_ssm_b_im': 'grad_w', 'grad_ssm_c_re': 'grad_w', 'grad_ssm_c_im': 'grad_w', 'grad_ssm_d': 'grad_w', 'grad_ssm_log_dt': 'grad_w', 'grad_w_glu': 'grad_w', 'grad_q_norm_g': 'grad_w', 'grad_w_uq': 'grad_w', 'grad_kv_norm_g': 'grad_w', 'grad_w_ukv': 'grad_w', 'grad_ssm_out_g': 'grad_w', 'grad_attn_out_g': 'grad_w', 'grad_w_out': 'grad_w', 'grad_norm2_g': 'grad_w', 'grad_w_ff1': 'grad_w', 'grad_w_ff2': 'grad_w', 'grad_final_ada_w': 'grad_w', 'grad_final_ada_b': 'grad_w', 'grad_final_norm_g': 'grad_w', 'delta_ada_w': 'delta_w', 'delta_ada_b': 'delta_w', 'delta_norm1_g': 'delta_w', 'delta_w_in': 'delta_w', 'delta_ssm_lambda_re': 'delta_w', 'delta_ssm_lambda_im': 'delta_w', 'delta_ssm_b_re': 'delta_w', 'delta_ssm_b_im': 'delta_w', 'delta_ssm_c_re': 'delta_w', 'delta_ssm_c_im': 'delta_w', 'delta_ssm_d': 'delta_w', 'delta_ssm_log_dt': 'delta_w', 'delta_w_glu': 'delta_w', 'delta_q_norm_g': 'delta_w', 'delta_w_uq': 'delta_w', 'delta_kv_norm_g': 'delta_w', 'delta_w_ukv': 'delta_w', 'delta_ssm_out_g': 'delta_w', 'delta_attn_out_g': 'delta_w', 'delta_w_out': 'delta_w', 'delta_norm2_g': 'delta_w', 'delta_w_ff1': 'delta_w', 'delta_w_ff2': 'delta_w', 'delta_final_ada_w': 'delta_w', 'delta_final_ada_b': 'delta_w', 'delta_final_norm_g': 'delta_w', 'new_m_ada_w': 'new_m', 'new_m_ada_b': 'new_m', 'new_m_norm1_g': 'new_m', 'new_m_w_in': 'new_m', 'new_m_ssm_lambda_re': 'new_m', 'new_m_ssm_lambda_im': 'new_m', 'new_m_ssm_b_re': 'new_m', 'new_m_ssm_b_im': 'new_m', 'new_m_ssm_c_re': 'new_m', 'new_m_ssm_c_im': 'new_m', 'new_m_ssm_d': 'new_m', 'new_m_ssm_log_dt': 'new_m', 'new_m_w_glu': 'new_m', 'new_m_q_norm_g': 'new_m', 'new_m_w_uq': 'new_m', 'new_m_kv_norm_g': 'new_m', 'new_m_w_ukv': 'new_m', 'new_m_ssm_out_g': 'new_m', 'new_m_attn_out_g': 'new_m', 'new_m_w_out': 'new_m', 'new_m_norm2_g': 'new_m', 'new_m_w_ff1': 'new_m', 'new_m_w_ff2': 'new_m', 'new_m_final_ada_w': 'new_m', 'new_m_final_ada_b': 'new_m', 'new_m_final_norm_g': 'new_m', 'new_v_ada_w': 'new_v', 'new_v_ada_b': 'new_v', 'new_v_norm1_g': 'new_v', 'new_v_w_in': 'new_v', 'new_v_ssm_lambda_re': 'new_v', 'new_v_ssm_lambda_im': 'new_v', 'new_v_ssm_b_re': 'new_v', 'new_v_ssm_b_im': 'new_v', 'new_v_ssm_c_re': 'new_v', 'new_v_ssm_c_im': 'new_v', 'new_v_ssm_d': 'new_v', 'new_v_ssm_log_dt': 'new_v', 'new_v_w_glu': 'new_v', 'new_v_q_norm_g': 'new_v', 'new_v_w_uq': 'new_v', 'new_v_kv_norm_g': 'new_v', 'new_v_w_ukv': 'new_v', 'new_v_ssm_out_g': 'new_v', 'new_v_attn_out_g': 'new_v', 'new_v_w_out': 'new_v', 'new_v_norm2_g': 'new_v', 'new_v_w_ff1': 'new_v', 'new_v_w_ff2': 'new_v', 'new_v_final_ada_w': 'new_v', 'new_v_final_ada_b': 'new_v', 'new_v_final_norm_g': 'new_v'}


def _forward(args):
    return _fwd_reference(*[args[k] for k in FWD_PARAMS])


def _output_shape():
    out = _jax.eval_shape(lambda: _forward(_fwd_setup_inputs(0)))
    return out.shape, out.dtype

N_MICROBATCH = 1
ADAM_LR = 0.001
ADAM_B1 = 0.9
ADAM_B2 = 0.999
ADAM_EPS = 1e-08
ADAM_WD = 0.01
ADAM_STEP = 10
PER_EXAMPLE_BATCH_AXIS = {'x': 0, 'c': 0, 'positions': 0, 'loss_target': 0}
SHARED_INPUTS = []
_WEIGHT_DTYPES = {'ada_w': _jnp.float32, 'ada_b': _jnp.float32, 'norm1_g': _jnp.float32, 'w_in': _jnp.float32, 'ssm_lambda_re': _jnp.float32, 'ssm_lambda_im': _jnp.float32, 'ssm_b_re': _jnp.float32, 'ssm_b_im': _jnp.float32, 'ssm_c_re': _jnp.float32, 'ssm_c_im': _jnp.float32, 'ssm_d': _jnp.float32, 'ssm_log_dt': _jnp.float32, 'w_glu': _jnp.float32, 'q_norm_g': _jnp.float32, 'w_uq': _jnp.float32, 'kv_norm_g': _jnp.float32, 'w_ukv': _jnp.float32, 'ssm_out_g': _jnp.float32, 'attn_out_g': _jnp.float32, 'w_out': _jnp.float32, 'norm2_g': _jnp.float32, 'w_ff1': _jnp.float32, 'w_ff2': _jnp.float32, 'final_ada_w': _jnp.float32, 'final_ada_b': _jnp.float32, 'final_norm_g': _jnp.float32}
MOMENT_SCALE = {'ada_w': 4.899679e+00, 'ada_b': 8.453076e+00, 'norm1_g': 1.550110e-01, 'w_in': 2.366989e+00, 'ssm_lambda_re': 7.078116e-02, 'ssm_lambda_im': 7.392924e-02, 'ssm_b_re': 6.247529e-02, 'ssm_b_im': 5.123699e-02, 'ssm_c_re': 1.084620e-01, 'ssm_c_im': 1.095495e-01, 'ssm_d': 1.896911e+00, 'ssm_log_dt': 3.924075e+00, 'w_glu': 1.398538e+00, 'q_norm_g': 8.904907e-02, 'w_uq': 7.471435e-02, 'kv_norm_g': 4.871279e+00, 'w_ukv': 2.441080e+00, 'ssm_out_g': 1.898795e+00, 'attn_out_g': 3.462677e+00, 'w_out': 2.922653e+00, 'norm2_g': 5.399730e-01, 'w_ff1': 5.154925e-01, 'w_ff2': 2.376715e+00, 'final_ada_w': 1.875229e+01, 'final_ada_b': 4.824083e+01, 'final_norm_g': 7.606223e+01}


def _to_microbatches(a, axis):
    t = _jnp.moveaxis(a, axis, 0)
    t = t.reshape((N_MICROBATCH, t.shape[0] // N_MICROBATCH) + t.shape[1:])
    return _jnp.moveaxis(t, 1, axis + 1)


def setup_inputs(seed: int = 0) -> dict:
    inp = _fwd_setup_inputs(seed)
    key = _jax.random.fold_in(_jax.random.key(seed), 7919)
    shape, _ = _output_shape()
    out = dict(inp)
    out["loss_target"] = _jax.random.normal(_jax.random.fold_in(key, 0), shape, _jnp.float32)
    for i, name in enumerate(TWIN_WEIGHTS):
        w = inp[name].astype(_jnp.float32)
        if MOMENT_SCALE is None:
            s = _jnp.sqrt(_jnp.mean(_jnp.square(w)) + 1e-30)
        else:
            s = MOMENT_SCALE[name]
        km, kv = _jax.random.split(_jax.random.fold_in(key, i + 1))
        out[name] = w
        out["m_" + name] = s * _jax.random.normal(km, w.shape, _jnp.float32)
        out["v_" + name] = (s * s) * _jax.random.uniform(kv, w.shape, _jnp.float32, 0.5, 1.5)
    if N_MICROBATCH > 1:
        for name, axis in PER_EXAMPLE_BATCH_AXIS.items():
            out[name] = _to_microbatches(out[name], axis)
    return {'x': out['x'], 'c': out['c'], 'positions': out['positions'], 'ada_w': out['ada_w'], 'ada_b': out['ada_b'], 'norm1_g': out['norm1_g'], 'w_in': out['w_in'], 'ssm_lambda_re': out['ssm_lambda_re'], 'ssm_lambda_im': out['ssm_lambda_im'], 'ssm_b_re': out['ssm_b_re'], 'ssm_b_im': out['ssm_b_im'], 'ssm_c_re': out['ssm_c_re'], 'ssm_c_im': out['ssm_c_im'], 'ssm_d': out['ssm_d'], 'ssm_log_dt': out['ssm_log_dt'], 'w_glu': out['w_glu'], 'q_norm_g': out['q_norm_g'], 'w_uq': out['w_uq'], 'kv_norm_g': out['kv_norm_g'], 'w_ukv': out['w_ukv'], 'ssm_out_g': out['ssm_out_g'], 'attn_out_g': out['attn_out_g'], 'w_out': out['w_out'], 'norm2_g': out['norm2_g'], 'w_ff1': out['w_ff1'], 'w_ff2': out['w_ff2'], 'final_ada_w': out['final_ada_w'], 'final_ada_b': out['final_ada_b'], 'final_norm_g': out['final_norm_g'], 'loss_target': out['loss_target'], 'm_ada_w': out['m_ada_w'], 'm_ada_b': out['m_ada_b'], 'm_norm1_g': out['m_norm1_g'], 'm_w_in': out['m_w_in'], 'm_ssm_lambda_re': out['m_ssm_lambda_re'], 'm_ssm_lambda_im': out['m_ssm_lambda_im'], 'm_ssm_b_re': out['m_ssm_b_re'], 'm_ssm_b_im': out['m_ssm_b_im'], 'm_ssm_c_re': out['m_ssm_c_re'], 'm_ssm_c_im': out['m_ssm_c_im'], 'm_ssm_d': out['m_ssm_d'], 'm_ssm_log_dt': out['m_ssm_log_dt'], 'm_w_glu': out['m_w_glu'], 'm_q_norm_g': out['m_q_norm_g'], 'm_w_uq': out['m_w_uq'], 'm_kv_norm_g': out['m_kv_norm_g'], 'm_w_ukv': out['m_w_ukv'], 'm_ssm_out_g': out['m_ssm_out_g'], 'm_attn_out_g': out['m_attn_out_g'], 'm_w_out': out['m_w_out'], 'm_norm2_g': out['m_norm2_g'], 'm_w_ff1': out['m_w_ff1'], 'm_w_ff2': out['m_w_ff2'], 'm_final_ada_w': out['m_final_ada_w'], 'm_final_ada_b': out['m_final_ada_b'], 'm_final_norm_g': out['m_final_norm_g'], 'v_ada_w': out['v_ada_w'], 'v_ada_b': out['v_ada_b'], 'v_norm1_g': out['v_norm1_g'], 'v_w_in': out['v_w_in'], 'v_ssm_lambda_re': out['v_ssm_lambda_re'], 'v_ssm_lambda_im': out['v_ssm_lambda_im'], 'v_ssm_b_re': out['v_ssm_b_re'], 'v_ssm_b_im': out['v_ssm_b_im'], 'v_ssm_c_re': out['v_ssm_c_re'], 'v_ssm_c_im': out['v_ssm_c_im'], 'v_ssm_d': out['v_ssm_d'], 'v_ssm_log_dt': out['v_ssm_log_dt'], 'v_w_glu': out['v_w_glu'], 'v_q_norm_g': out['v_q_norm_g'], 'v_w_uq': out['v_w_uq'], 'v_kv_norm_g': out['v_kv_norm_g'], 'v_w_ukv': out['v_w_ukv'], 'v_ssm_out_g': out['v_ssm_out_g'], 'v_attn_out_g': out['v_attn_out_g'], 'v_w_out': out['v_w_out'], 'v_norm2_g': out['v_norm2_g'], 'v_w_ff1': out['v_w_ff1'], 'v_w_ff2': out['v_w_ff2'], 'v_final_ada_w': out['v_final_ada_w'], 'v_final_ada_b': out['v_final_ada_b'], 'v_final_norm_g': out['v_final_norm_g']}


def _loss(weights, diff, rest, loss_target):
    with _jax.named_scope("forward"):
        args = {**rest, TWIN_DIFF_INPUT: diff, **{k: w.astype(_WEIGHT_DTYPES[k]) for k, w in weights.items()}}
        y = _forward(args)
    with _jax.named_scope("loss_head"):
        err = _jnp.square(y.astype(_jnp.float32) - loss_target)
        return 0.5 * _jnp.sum(_jnp.mean(err, axis=-1)) if err.ndim else 0.5 * err


def _adamw(w, g, m, v):
    m = ADAM_B1 * m + (1.0 - ADAM_B1) * g
    v = ADAM_B2 * v + (1.0 - ADAM_B2) * _jnp.square(g)
    m_hat = m / (1.0 - ADAM_B1 ** ADAM_STEP)
    v_hat = v / (1.0 - ADAM_B2 ** ADAM_STEP)
    delta = -ADAM_LR * (m_hat / (_jnp.sqrt(v_hat) + ADAM_EPS) + ADAM_WD * w)
    return delta, m, v


def reference(x, c, positions, ada_w, ada_b, norm1_g, w_in, ssm_lambda_re, ssm_lambda_im, ssm_b_re, ssm_b_im, ssm_c_re, ssm_c_im, ssm_d, ssm_log_dt, w_glu, q_norm_g, w_uq, kv_norm_g, w_ukv, ssm_out_g, attn_out_g, w_out, norm2_g, w_ff1, w_ff2, final_ada_w, final_ada_b, final_norm_g, loss_target, m_ada_w, m_ada_b, m_norm1_g, m_w_in, m_ssm_lambda_re, m_ssm_lambda_im, m_ssm_b_re, m_ssm_b_im, m_ssm_c_re, m_ssm_c_im, m_ssm_d, m_ssm_log_dt, m_w_glu, m_q_norm_g, m_w_uq, m_kv_norm_g, m_w_ukv, m_ssm_out_g, m_attn_out_g, m_w_out, m_norm2_g, m_w_ff1, m_w_ff2, m_final_ada_w, m_final_ada_b, m_final_norm_g, v_ada_w, v_ada_b, v_norm1_g, v_w_in, v_ssm_lambda_re, v_ssm_lambda_im, v_ssm_b_re, v_ssm_b_im, v_ssm_c_re, v_ssm_c_im, v_ssm_d, v_ssm_log_dt, v_w_glu, v_q_norm_g, v_w_uq, v_kv_norm_g, v_w_ukv, v_ssm_out_g, v_attn_out_g, v_w_out, v_norm2_g, v_w_ff1, v_w_ff2, v_final_ada_w, v_final_ada_b, v_final_norm_g):
    given = dict(x=x, c=c, positions=positions, ada_w=ada_w, ada_b=ada_b, norm1_g=norm1_g, w_in=w_in, ssm_lambda_re=ssm_lambda_re, ssm_lambda_im=ssm_lambda_im, ssm_b_re=ssm_b_re, ssm_b_im=ssm_b_im, ssm_c_re=ssm_c_re, ssm_c_im=ssm_c_im, ssm_d=ssm_d, ssm_log_dt=ssm_log_dt, w_glu=w_glu, q_norm_g=q_norm_g, w_uq=w_uq, kv_norm_g=kv_norm_g, w_ukv=w_ukv, ssm_out_g=ssm_out_g, attn_out_g=attn_out_g, w_out=w_out, norm2_g=norm2_g, w_ff1=w_ff1, w_ff2=w_ff2, final_ada_w=final_ada_w, final_ada_b=final_ada_b, final_norm_g=final_norm_g, loss_target=loss_target, m_ada_w=m_ada_w, m_ada_b=m_ada_b, m_norm1_g=m_norm1_g, m_w_in=m_w_in, m_ssm_lambda_re=m_ssm_lambda_re, m_ssm_lambda_im=m_ssm_lambda_im, m_ssm_b_re=m_ssm_b_re, m_ssm_b_im=m_ssm_b_im, m_ssm_c_re=m_ssm_c_re, m_ssm_c_im=m_ssm_c_im, m_ssm_d=m_ssm_d, m_ssm_log_dt=m_ssm_log_dt, m_w_glu=m_w_glu, m_q_norm_g=m_q_norm_g, m_w_uq=m_w_uq, m_kv_norm_g=m_kv_norm_g, m_w_ukv=m_w_ukv, m_ssm_out_g=m_ssm_out_g, m_attn_out_g=m_attn_out_g, m_w_out=m_w_out, m_norm2_g=m_norm2_g, m_w_ff1=m_w_ff1, m_w_ff2=m_w_ff2, m_final_ada_w=m_final_ada_w, m_final_ada_b=m_final_ada_b, m_final_norm_g=m_final_norm_g, v_ada_w=v_ada_w, v_ada_b=v_ada_b, v_norm1_g=v_norm1_g, v_w_in=v_w_in, v_ssm_lambda_re=v_ssm_lambda_re, v_ssm_lambda_im=v_ssm_lambda_im, v_ssm_b_re=v_ssm_b_re, v_ssm_b_im=v_ssm_b_im, v_ssm_c_re=v_ssm_c_re, v_ssm_c_im=v_ssm_c_im, v_ssm_d=v_ssm_d, v_ssm_log_dt=v_ssm_log_dt, v_w_glu=v_w_glu, v_q_norm_g=v_q_norm_g, v_w_uq=v_w_uq, v_kv_norm_g=v_kv_norm_g, v_w_ukv=v_w_ukv, v_ssm_out_g=v_ssm_out_g, v_attn_out_g=v_attn_out_g, v_w_out=v_w_out, v_norm2_g=v_norm2_g, v_w_ff1=v_w_ff1, v_w_ff2=v_w_ff2, v_final_ada_w=v_final_ada_w, v_final_ada_b=v_final_ada_b, v_final_norm_g=v_final_norm_g)
    weights = {n: given[n] for n in TWIN_WEIGHTS}
    shared = {n: given[n] for n in SHARED_INPUTS}
    per_example = {n: given[n] for n in ['x', 'c', 'positions']}
    grad_fn = _jax.value_and_grad(_loss, argnums=(0, 1))

    def one_microbatch(ex, loss_target):
        ex = dict(ex)
        diff = ex.pop(TWIN_DIFF_INPUT)
        return grad_fn(weights, diff, {**shared, **ex}, loss_target)

    if N_MICROBATCH == 1:
        loss, (grad_w, grad_x) = one_microbatch(per_example, given["loss_target"])
    else:
        def body(carry, xs):
            loss_sum, grad_sum = carry
            l_k, (gw_k, gx_k) = one_microbatch(xs[0], xs[1])
            with _jax.named_scope("update"):
                return (loss_sum + l_k, _jax.tree.map(_jnp.add, grad_sum, gw_k)), gx_k

        init = (_jnp.zeros((), _jnp.float32), _jax.tree.map(_jnp.zeros_like, weights))
        (loss, grad_w), grad_x = _jax.lax.scan(body, init, (per_example, given["loss_target"]))
    with _jax.named_scope("update"):
        delta_w, new_m, new_v = {}, {}, {}
        for n in TWIN_WEIGHTS:
            delta_w[n], new_m[n], new_v[n] = _adamw(weights[n], grad_w[n], given["m_" + n], given["v_" + n])
    return (loss, grad_x, *[grad_w[n] for n in TWIN_WEIGHTS], *[delta_w[n] for n in TWIN_WEIGHTS],
            *[new_m[n] for n in TWIN_WEIGHTS], *[new_v[n] for n in TWIN_WEIGHTS])
```

```python
import functools
import math

import numpy as np
import jax
import jax.numpy as jnp
from jax import lax
from jax.experimental import pallas as pl
from jax.experimental.pallas import tpu as pltpu

F32 = jnp.float32
BF16 = jnp.bfloat16

N_DEV = 8
EPS = 1e-6
D_SSM = 512
N_GROUPS = 32
GROUP = 16
N_STATE = 64
N_ST = N_GROUPS * N_STATE
Q_LORA = 384
KV_LORA = 256
QK_NOPE = 64
QK_ROPE = 32
V_HEAD = 64
N_HEADS = 8
HEAD_PAD = 128
QK_W = N_HEADS * HEAD_PAD
V_W = N_HEADS * V_HEAD
IN_COLS = D_SSM + Q_LORA + KV_LORA + QK_ROPE
IN_PAD = D_SSM + Q_LORA + KV_LORA + HEAD_PAD
ROPE_BASE = 10000.0
ATTN_SCALE = (QK_NOPE + QK_ROPE) ** -0.5
NEG = -1e30

ADAM_LR = 0.001
ADAM_B1 = 0.9
ADAM_B2 = 0.999
ADAM_EPS = 1e-08
ADAM_WD = 0.01
ADAM_STEP = 10

VMEM_LIMIT = 56 * 1024 * 1024
MESH_ID = pl.DeviceIdType.MESH


def _dot(a, b):
    return jnp.dot(a, b, preferred_element_type=F32)


def _dot_nt(a, b):
    return lax.dot_general(a, b, (((1,), (1,)), ((), ())), preferred_element_type=F32)


def _dot_tn(a, b):
    return lax.dot_general(a, b, (((0,), (0,)), ((), ())), preferred_element_type=F32)


def _rms_stats(x):
    r = lax.rsqrt(jnp.mean(x * x, axis=-1, keepdims=True) + EPS)
    return r, x * r


def _rms_bwd(dy, xh, r):
    return r * (dy - xh * jnp.mean(dy * xh, axis=-1, keepdims=True))


def _sigmoid(x):
    return 1.0 / (1.0 + jnp.exp(-x))


_GELU_C = math.sqrt(2.0 / math.pi)


def _gelu(x):
    t = jnp.tanh(_GELU_C * (x + 0.044715 * x * x * x))
    return 0.5 * x * (1.0 + t)


def _gelu_grad(x):
    t = jnp.tanh(_GELU_C * (x + 0.044715 * x * x * x))
    return 0.5 * (1.0 + t) + 0.5 * x * (1.0 - t * t) * _GELU_C * (1.0 + 3.0 * 0.044715 * x * x)


def _colsum(a):
    return jnp.sum(a, axis=0, keepdims=True)


def _params(sem=None):
    return pltpu.CompilerParams(dimension_semantics=sem, vmem_limit_bytes=VMEM_LIMIT)


def _mesh_pos():
    x, y, c = lax.axis_index("x"), lax.axis_index("y"), lax.axis_index("c")
    return x, y, c, 4 * x + 2 * y + c


def _peer(x, y, c, k):
    px = 1 - x if k & 4 else x
    py = 1 - y if k & 2 else y
    pc = 1 - c if k & 1 else c
    return (px, py, pc), 4 * px + 2 * py + pc


def all_gather(name, arrays):
    n = len(arrays)

    def body(*refs):
        ins, outs = refs[:n], refs[n:2 * n]
        send_sems, recv_sems, loc_sems = refs[2 * n:]
        x, y, c, me = _mesh_pos()
        local, sent = [], []
        for i in range(n):
            cp = pltpu.make_async_copy(ins[i], outs[i].at[me], loc_sems.at[i])
            cp.start()
            local.append(cp)
            for k in range(1, N_DEV):
                peer, _ = _peer(x, y, c, k)
                rc = pltpu.make_async_remote_copy(
                    src_ref=ins[i], dst_ref=outs[i].at[me], send_sem=send_sems.at[i, k - 1],
                    recv_sem=recv_sems.at[i, k - 1], device_id=peer, device_id_type=MESH_ID)
                rc.start()
                sent.append(rc)
        for i in range(n):
            for k in range(1, N_DEV):
                peer, pidx = _peer(x, y, c, k)
                pltpu.make_async_remote_copy(
                    src_ref=ins[i], dst_ref=outs[i].at[pidx], send_sem=send_sems.at[i, k - 1],
                    recv_sem=recv_sems.at[i, k - 1], device_id=peer, device_id_type=MESH_ID).wait_recv()
        for rc in sent:
            rc.wait_send()
        for cp in local:
            cp.wait()

    any_spec = pl.BlockSpec(memory_space=pl.ANY)
    return pl.pallas_call(
        body, name=name,
        out_shape=[jax.ShapeDtypeStruct((N_DEV,) + a.shape, a.dtype) for a in arrays],
        in_specs=[any_spec] * n, out_specs=[any_spec] * n,
        scratch_shapes=[pltpu.SemaphoreType.DMA((n, N_DEV - 1)), pltpu.SemaphoreType.DMA((n, N_DEV - 1)),
                        pltpu.SemaphoreType.DMA((n,))],
    )(*arrays)


def all_to_all(name, arrays, col_widths):
    n = len(arrays)
    shard_shapes = [a.shape[1:] if w is None else (a.shape[0], w) for a, w in zip(arrays, col_widths)]

    def body(*refs):
        ins, outs = refs[:n], refs[n:2 * n]
        send_sems, recv_sems, loc_sems = refs[2 * n:]
        x, y, c, me = _mesh_pos()

        def src(i, j):
            w = col_widths[i]
            if w is None:
                return ins[i].at[j]
            return ins[i].at[:, pl.ds(pl.multiple_of(j * w, 128), w)]

        local, sent = [], []
        for i in range(n):
            cp = pltpu.make_async_copy(src(i, me), outs[i].at[me], loc_sems.at[i])
            cp.start()
            local.append(cp)
            for k in range(1, N_DEV):
                peer, pidx = _peer(x, y, c, k)
                rc = pltpu.make_async_remote_copy(
                    src_ref=src(i, pidx), dst_ref=outs[i].at[me], send_sem=send_sems.at[i, k - 1],
                    recv_sem=recv_sems.at[i, k - 1], device_id=peer, device_id_type=MESH_ID)
                rc.start()
                sent.append(rc)
        for i in range(n):
            for k in range(1, N_DEV):
                peer, pidx = _peer(x, y, c, k)
                pltpu.make_async_remote_copy(
                    src_ref=src(i, pidx), dst_ref=outs[i].at[pidx], send_sem=send_sems.at[i, k - 1],
                    recv_sem=recv_sems.at[i, k - 1], device_id=peer, device_id_type=MESH_ID).wait_recv()
        for rc in sent:
            rc.wait_send()
        for cp in local:
            cp.wait()

    any_spec = pl.BlockSpec(memory_space=pl.ANY)
    return pl.pallas_call(
        body, name=name,
        out_shape=[jax.ShapeDtypeStruct((N_DEV,) + tuple(s), a.dtype) for a, s in zip(arrays, shard_shapes)],
        in_specs=[any_spec] * n, out_specs=[any_spec] * n,
        scratch_shapes=[pltpu.SemaphoreType.DMA((n, N_DEV - 1)), pltpu.SemaphoreType.DMA((n, N_DEV - 1)),
                        pltpu.SemaphoreType.DMA((n,))],
    )(*arrays)


def row_call(name, body, B, S, tm, rows, bvecs, vecs, res, row_outs, bacc, gacc):
    ns = S // tm
    T = B * S
    n_r, n_b, n_v, n_w = len(rows), len(bvecs), len(vecs), len(res)
    n_ro, n_ba, n_ga = len(row_outs), len(bacc), len(gacc)

    def kern(*refs):
        i = 0
        r_refs = refs[i:i + n_r]; i += n_r
        b_refs = refs[i:i + n_b]; i += n_b
        v_refs = refs[i:i + n_v]; i += n_v
        w_hbm = refs[i:i + n_w]; i += n_w
        ro_refs = refs[i:i + n_ro]; i += n_ro
        ba_refs = refs[i:i + n_ba]; i += n_ba
        ga_refs = refs[i:i + n_ga]; i += n_ga
        w_vmem = refs[i:i + n_w]
        b = pl.program_id(0)
        s = pl.program_id(1)
        first = jnp.logical_and(b == 0, s == 0)

        if n_w:
            @pl.when(first)
            def _load_weights():
                for h, v in zip(w_hbm, w_vmem):
                    pltpu.sync_copy(h, v)

        ro, ba, ga = body([r[...] for r in r_refs], [r[0] for r in b_refs], [r[...] for r in v_refs],
                          list(w_vmem))
        for ref, val in zip(ro_refs, ro):
            ref[...] = val.astype(ref.dtype)

        def accumulate(ref, val, is_first, idx):
            @pl.when(is_first)
            def _set():
                ref[idx] = val

            @pl.when(jnp.logical_not(is_first))
            def _add():
                ref[idx] = ref[idx] + val

        for ref, val in zip(ba_refs, ba):
            accumulate(ref, val, s == 0, 0)
        for ref, val in zip(ga_refs, ga):
            accumulate(ref, val, first, Ellipsis)

    in_specs = ([pl.BlockSpec((tm, r.shape[1]), lambda b, s: (b * ns + s, 0)) for r in rows]
                + [pl.BlockSpec((1, 1, v.shape[2]), lambda b, s: (b, 0, 0)) for v in bvecs]
                + [pl.BlockSpec(v.shape, lambda b, s, nd=v.ndim: (0,) * nd) for v in vecs]
                + [pl.BlockSpec(memory_space=pl.ANY) for _ in res])
    out_shape = ([jax.ShapeDtypeStruct((T, w), dt) for w, dt in row_outs]
                 + [jax.ShapeDtypeStruct((B, 1, w), F32) for w in bacc]
                 + [jax.ShapeDtypeStruct(tuple(sh), F32) for sh in gacc])
    out_specs = ([pl.BlockSpec((tm, w), lambda b, s: (b * ns + s, 0)) for w, _ in row_outs]
                 + [pl.BlockSpec((1, 1, w), lambda b, s: (b, 0, 0)) for w in bacc]
                 + [pl.BlockSpec(tuple(sh), lambda b, s, nd=len(sh): (0,) * nd) for sh in gacc])
    outs = pl.pallas_call(
        kern, name=name, grid=(B, ns), in_specs=in_specs, out_specs=out_specs, out_shape=out_shape,
        scratch_shapes=[pltpu.VMEM(w.shape, w.dtype) for w in res],
        compiler_params=_params(("arbitrary", "arbitrary")),
    )(*rows, *bvecs, *vecs, *res)
    return outs[:n_ro], outs[n_ro:n_ro + n_ba], outs[n_ro + n_ba:]


def matmul_tn(name, a, b):
    T, K = a.shape
    N = b.shape[1]
    tk = min(K, 512)
    tn = 512 if N % 512 == 0 else 256
    tt = min(T, 512)
    nt = T // tt

    def kern(a_ref, b_ref, o_ref):
        t = pl.program_id(2)
        part = _dot_tn(a_ref[...], b_ref[...])

        @pl.when(t == 0)
        def _set():
            o_ref[...] = part

        @pl.when(t > 0)
        def _add():
            o_ref[...] = o_ref[...] + part

    return pl.pallas_call(
        kern, name=name, grid=(K // tk, N // tn, nt),
        in_specs=[pl.BlockSpec((tt, tk), lambda i, j, t: (t, i)), pl.BlockSpec((tt, tn), lambda i, j, t: (t, j))],
        out_specs=pl.BlockSpec((tk, tn), lambda i, j, t: (i, j)),
        out_shape=jax.ShapeDtypeStruct((K, N), F32),
        compiler_params=_params(("parallel", "parallel", "arbitrary")),
    )(a, b)


def _rope_consts():
    inv_freq = ROPE_BASE ** (-jnp.arange(0, QK_ROPE, 2, dtype=F32) / QK_ROPE)
    zeros64 = jnp.zeros((64,), F32)
    zeros32 = jnp.zeros((32,), F32)
    zeros16 = jnp.zeros((16,), F32)
    ones16 = jnp.ones((16,), F32)
    invf = jnp.concatenate([zeros64, inv_freq, inv_freq, zeros32])[None, :]
    m_lo = jnp.concatenate([zeros64, ones16, zeros16, zeros32])[None, :]
    m_hi = jnp.concatenate([zeros64, zeros16, ones16, zeros32])[None, :]
    return invf, m_lo, m_hi


def _rope_tables(pos, invf, m_lo, m_hi):
    ang = pos * invf
    return jnp.cos(ang), jnp.sin(ang) * m_lo, jnp.sin(ang) * m_hi


def _rope_tile(t, cs, s_lo, s_hi, sign):
    up = pltpu.roll(t, HEAD_PAD - 16, axis=1)
    down = pltpu.roll(t, 16, axis=1)
    return t * cs + sign * (down * s_hi - up * s_lo)


def attention_fwd(q, k, v, B, S, tq):
    nq = S // tq

    def kern(q_ref, k_ref, v_ref, o_ref, lse_ref, m_scr, l_scr, acc_scr):
        qi, ki = pl.program_id(1), pl.program_id(2)

        @pl.when(ki == 0)
        def _init():
            m_scr[...] = jnp.full(m_scr.shape, NEG, F32)
            l_scr[...] = jnp.zeros(l_scr.shape, F32)
            acc_scr[...] = jnp.zeros(acc_scr.shape, F32)

        @pl.when(ki <= qi)
        def _block():
            row = lax.broadcasted_iota(jnp.int32, (tq, tq), 0)
            col = lax.broadcasted_iota(jnp.int32, (tq, tq), 1)
            keep = jnp.logical_or(ki < qi, col <= row)
            for h in range(N_HEADS):
                qh = q_ref[:, h * HEAD_PAD:(h + 1) * HEAD_PAD]
                kh = k_ref[:, h * HEAD_PAD:(h + 1) * HEAD_PAD]
                vh = v_ref[:, h * V_HEAD:(h + 1) * V_HEAD]
                s = jnp.where(keep, _dot_nt(qh, kh) * ATTN_SCALE, NEG)
                m_prev = m_scr[h]
                m_new = jnp.maximum(m_prev, jnp.max(s, axis=-1, keepdims=True))
                alpha = jnp.exp(m_prev - m_new)
                p = jnp.exp(s - m_new)
                l_scr[h] = alpha * l_scr[h] + jnp.sum(p, axis=-1, keepdims=True)
                acc_scr[h] = alpha * acc_scr[h] + _dot(p.astype(BF16), vh)
                m_scr[h] = m_new

        @pl.when(ki == qi)
        def _finish():
            lane = lax.broadcasted_iota(jnp.int32, (tq, HEAD_PAD), 1)
            lse = jnp.zeros((tq, HEAD_PAD), F32)
            for h in range(N_HEADS):
                o_ref[:, h * V_HEAD:(h + 1) * V_HEAD] = acc_scr[h] / l_scr[h]
                lse = jnp.where(lane == h, m_scr[h] + jnp.log(l_scr[h]), lse)
            lse_ref[...] = lse

    T = B * S
    return pl.pallas_call(
        kern, name="attention_fwd", grid=(B, nq, nq),
        in_specs=[pl.BlockSpec((tq, QK_W), lambda b, i, j: (b * nq + i, 0)),
                  pl.BlockSpec((tq, QK_W), lambda b, i, j: (b * nq + jnp.minimum(i, j), 0)),
                  pl.BlockSpec((tq, V_W), lambda b, i, j: (b * nq + jnp.minimum(i, j), 0))],
        out_specs=[pl.BlockSpec((tq, V_W), lambda b, i, j: (b * nq + i, 0)),
                   pl.BlockSpec((tq, HEAD_PAD), lambda b, i, j: (b * nq + i, 0))],
        out_shape=[jax.ShapeDtypeStruct((T, V_W), F32), jax.ShapeDtypeStruct((T, HEAD_PAD), F32)],
        scratch_shapes=[pltpu.VMEM((N_HEADS, tq, 1), F32), pltpu.VMEM((N_HEADS, tq, 1), F32),
                        pltpu.VMEM((N_HEADS, tq, V_HEAD), F32)],
        compiler_params=_params(("parallel", "parallel", "arbitrary")),
    )(q, k, v)


def attention_bwd(q, k, v, o, do, lse, B, S, tq):
    nq = S // tq

    def kern(q_ref, k_ref, v_ref, o_ref, do_ref, lse_ref, dq_hbm, dk_ref, dv_ref, dq_acc, dk_acc, dv_acc):
        b, kj, qi = pl.program_id(0), pl.program_id(1), pl.program_id(2)

        @pl.when(jnp.logical_and(kj == 0, qi == 0))
        def _zero_dq():
            dq_acc[...] = jnp.zeros(dq_acc.shape, F32)

        @pl.when(qi == 0)
        def _zero_dkv():
            dk_acc[...] = jnp.zeros(dk_acc.shape, F32)
            dv_acc[...] = jnp.zeros(dv_acc.shape, F32)

        @pl.when(qi >= kj)
        def _block():
            row = lax.broadcasted_iota(jnp.int32, (tq, tq), 0)
            col = lax.broadcasted_iota(jnp.int32, (tq, tq), 1)
            keep = jnp.logical_or(kj < qi, col <= row)
            lane = lax.broadcasted_iota(jnp.int32, (tq, HEAD_PAD), 1)
            lse_all = lse_ref[...]
            q_rows = pl.ds(pl.multiple_of(qi * tq, tq), tq)
            for h in range(N_HEADS):
                hq = slice(h * HEAD_PAD, (h + 1) * HEAD_PAD)
                hv = slice(h * V_HEAD, (h + 1) * V_HEAD)
                qh, kh, vh = q_ref[:, hq], k_ref[:, hq], v_ref[:, hv]
                oh, doh = o_ref[:, hv], do_ref[:, hv]
                lse_h = jnp.sum(jnp.where(lane == h, lse_all, 0.0), axis=-1, keepdims=True)
                delta = jnp.sum(oh * doh, axis=-1, keepdims=True)
                s = jnp.where(keep, _dot_nt(qh, kh) * ATTN_SCALE, NEG)
                p = jnp.exp(s - lse_h)
                dob = doh.astype(BF16)
                dv_acc[:, hv] = dv_acc[:, hv] + _dot_tn(p.astype(BF16), dob)
                dp = _dot_nt(dob, vh)
                ds = (p * (dp - delta) * ATTN_SCALE).astype(BF16)
                dk_acc[:, hq] = dk_acc[:, hq] + _dot_tn(ds, qh)
                dq_acc[q_rows, hq] = dq_acc[q_rows, hq] + _dot(ds, kh)

        @pl.when(qi == nq - 1)
        def _write_dkv():
            dk_ref[...] = dk_acc[...]
            dv_ref[...] = dv_acc[...]

        @pl.when(jnp.logical_and(kj == nq - 1, qi == nq - 1))
        def _write_dq():
            pltpu.sync_copy(dq_acc, dq_hbm.at[b])

    T = B * S
    qrow = lambda b, j, i: (b * nq + jnp.maximum(i, j), 0)
    krow = lambda b, j, i: (b * nq + j, 0)
    return pl.pallas_call(
        kern, name="attention_bwd", grid=(B, nq, nq),
        in_specs=[pl.BlockSpec((tq, QK_W), qrow), pl.BlockSpec((tq, QK_W), krow), pl.BlockSpec((tq, V_W), krow),
                  pl.BlockSpec((tq, V_W), qrow), pl.BlockSpec((tq, V_W), qrow), pl.BlockSpec((tq, HEAD_PAD), qrow)],
        out_specs=[pl.BlockSpec(memory_space=pl.ANY), pl.BlockSpec((tq, QK_W), krow), pl.BlockSpec((tq, V_W), krow)],
        out_shape=[jax.ShapeDtypeStruct((B, S, QK_W), F32), jax.ShapeDtypeStruct((T, QK_W), F32),
                   jax.ShapeDtypeStruct((T, V_W), F32)],
        scratch_shapes=[pltpu.VMEM((S, QK_W), F32), pltpu.VMEM((tq, QK_W), F32), pltpu.VMEM((tq, V_W), F32)],
        compiler_params=_params(("arbitrary", "arbitrary", "arbitrary")),
    )(q, k, v, o, do, lse)


GB = 8
N_GB = N_GROUPS // GB
GB_U = GB * GROUP
GB_ST = GB * N_STATE


def _gb_u(j):
    return slice(j * GB_U, (j + 1) * GB_U)


def _gb_s(j):
    return slice(j * GB_ST, (j + 1) * GB_ST)


def ssm_param_fn(lr, li, ld, br, bi):
    dt = jnp.exp(ld)
    er = jnp.exp(lr * dt)
    ar = er * jnp.cos(li * dt)
    ai = er * jnp.sin(li * dt)
    nr = ar - 1.0
    den = lr * lr + li * li
    cr = (nr * lr + ai * li) / den
    ci = (ai * lr - nr * li) / den
    return ar, ai, cr * br - ci * bi, cr * bi + ci * br


def ssm_params_fwd(reps):
    def kern(lr, li, ld, br, bi, ar_o, ai_o, bbr_o, bbi_o):
        ar, ai, bbr, bbi = ssm_param_fn(lr[...], li[...], ld[...], br[...], bi[...])
        ar_o[...] = ar
        ai_o[...] = ai
        bbr_o[...] = bbr
        bbi_o[...] = bbi

    shp = jax.ShapeDtypeStruct(reps[0].shape, F32)
    return pl.pallas_call(kern, name="ssm_params_fwd", out_shape=[shp] * 4)(*reps)


def ssm_params_bwd(reps, cots):
    rows = reps[0].shape[0]

    def kern(lr, li, ld, br, bi, d_ar, d_ai, d_bbr, d_bbi, dlr_o, dli_o, dld_o, dbr_o, dbi_o):
        _, vjp = jax.vjp(ssm_param_fn, lr[...], li[...], ld[...], br[...], bi[...])
        dlr, dli, dld, dbr, dbi = vjp((d_ar[...], d_ai[...], d_bbr[...], d_bbi[...]))
        dlr_o[...] = jnp.sum(dlr.reshape(N_GROUPS, GROUP, N_STATE), axis=1)
        dli_o[...] = jnp.sum(dli.reshape(N_GROUPS, GROUP, N_STATE), axis=1)
        dld_o[...] = jnp.sum(jnp.sum(dld.reshape(N_GROUPS, GROUP, N_STATE), axis=1), axis=-1, keepdims=True)
        dbr_o[...] = dbr
        dbi_o[...] = dbi

    g = jax.ShapeDtypeStruct((N_GROUPS, N_STATE), F32)
    full = jax.ShapeDtypeStruct((rows, N_STATE), F32)
    return pl.pallas_call(
        kern, name="ssm_params_bwd",
        out_shape=[g, g, jax.ShapeDtypeStruct((N_GROUPS, 1), F32), full, full])(*reps, *cots)


def ssm_fwd(proj, b_re, b_im, c_re, c_im, d_vec, a_re, a_im, B, S, ts):
    ns = S // ts
    T = B * S

    def kern(u_ref, bre_ref, bim_ref, cre_ref, cim_ref, d_ref, are_ref, aim_ref,
             sre_ref, sim_ref, y_ref, car_re, car_im, bu_re, bu_im):
        s = pl.program_id(1)

        @pl.when(s == 0)
        def _reset():
            car_re[...] = jnp.zeros(car_re.shape, F32)
            car_im[...] = jnp.zeros(car_im.shape, F32)

        u = u_ref[...]
        ub = u.astype(BF16)
        for j in range(N_GB):
            uj, bj, sj = ub[:, _gb_u(j)], _gb_u(j), _gb_s(j)
            bu_re[:, sj] = _dot(uj, bre_ref[bj, :])
            bu_im[:, sj] = _dot(uj, bim_ref[bj, :])
        ar, ai = are_ref[...], aim_ref[...]

        def step(t, carry):
            xr, xi = carry
            row = pl.ds(t, 1)
            nr = ar * xr - ai * xi + bu_re[row, :]
            ni = ar * xi + ai * xr + bu_im[row, :]
            sre_ref[row, :] = nr
            sim_ref[row, :] = ni
            return nr, ni

        xr, xi = lax.fori_loop(0, ts, step, (car_re[0:1, :], car_im[0:1, :]))
        car_re[0:1, :] = xr
        car_im[0:1, :] = xi
        y = [_dot(sre_ref[:, _gb_s(j)].astype(BF16), cre_ref[_gb_s(j), :])
             - _dot(sim_ref[:, _gb_s(j)].astype(BF16), cim_ref[_gb_s(j), :]) for j in range(N_GB)]
        y_ref[...] = jnp.concatenate(y, axis=1) + d_ref[...] * u

    row = lambda b, s: (b * ns + s, 0)
    whole = lambda b, s: (0, 0)
    return pl.pallas_call(
        kern, name="ssm_fwd", grid=(B, ns),
        in_specs=[pl.BlockSpec((ts, D_SSM), row),
                  pl.BlockSpec((D_SSM, GB_ST), whole), pl.BlockSpec((D_SSM, GB_ST), whole),
                  pl.BlockSpec((N_ST, GB_U), whole), pl.BlockSpec((N_ST, GB_U), whole),
                  pl.BlockSpec((1, D_SSM), whole), pl.BlockSpec((1, N_ST), whole), pl.BlockSpec((1, N_ST), whole)],
        out_specs=[pl.BlockSpec((ts, N_ST), row), pl.BlockSpec((ts, N_ST), row), pl.BlockSpec((ts, D_SSM), row)],
        out_shape=[jax.ShapeDtypeStruct((T, N_ST), F32), jax.ShapeDtypeStruct((T, N_ST), F32),
                   jax.ShapeDtypeStruct((T, D_SSM), F32)],
        scratch_shapes=[pltpu.VMEM((8, N_ST), F32), pltpu.VMEM((8, N_ST), F32),
                        pltpu.VMEM((ts, N_ST), F32), pltpu.VMEM((ts, N_ST), F32)],
        compiler_params=_params(("arbitrary", "arbitrary")),
    )(proj, b_re, b_im, c_re, c_im, d_vec, a_re, a_im)


def ssm_bwd(dy, proj, s_re, s_im, b_re, b_im, c_re, c_im, d_vec, a_re, a_im, B, S, ts):
    ns = S // ts
    T = B * S
    t8 = ts // 8

    def kern(dy_ref, u_ref, sre_ref, sim_ref, pre_ref, pim_ref, bre_ref, bim_ref, cre_ref, cim_ref,
             d_ref, are_ref, aim_ref,
             du_ref, dcre_ref, dcim_ref, dbre_ref, dbim_ref, dare_ref, daim_ref, dd_ref,
             car_re, car_im, g_re, g_im):
        b, s = pl.program_id(0), pl.program_id(1)
        first = jnp.logical_and(b == 0, s == 0)

        @pl.when(s == 0)
        def _reset():
            car_re[...] = jnp.zeros(car_re.shape, F32)
            car_im[...] = jnp.zeros(car_im.shape, F32)

        dyv = dy_ref[...]
        dyb = dyv.astype(BF16)
        u = u_ref[...]
        for j in range(N_GB):
            g_re[:, _gb_s(j)] = _dot_nt(dyb[:, _gb_u(j)], cre_ref[_gb_s(j), :])
            g_im[:, _gb_s(j)] = -_dot_nt(dyb[:, _gb_u(j)], cim_ref[_gb_s(j), :])
        ar, ai = are_ref[...], aim_ref[...]

        def step(i, carry):
            gr_next, gi_next = carry
            row = pl.ds(ts - 1 - i, 1)
            gr = g_re[row, :] + ar * gr_next + ai * gi_next
            gi = g_im[row, :] + ar * gi_next - ai * gr_next
            g_re[row, :] = gr
            g_im[row, :] = gi
            return gr, gi

        gr0, gi0 = lax.fori_loop(0, ts, step, (car_re[0:1, :], car_im[0:1, :]))
        car_re[0:1, :] = gr0
        car_im[0:1, :] = gi0

        gr, gi = g_re[...], g_im[...]
        sr, si = sre_ref[...], sim_ref[...]
        has_prev = (s < ns - 1).astype(F32)
        row_id = lax.broadcasted_iota(jnp.int32, (ts, N_ST), 0)
        spr = jnp.where(row_id == 0, pre_ref[7:8, :] * has_prev, pltpu.roll(sr, 1, axis=0))
        spi = jnp.where(row_id == 0, pim_ref[7:8, :] * has_prev, pltpu.roll(si, 1, axis=0))
        grb, gib, ub = gr.astype(BF16), gi.astype(BF16), u.astype(BF16)
        srb, sib = sr.astype(BF16), si.astype(BF16)
        cat0 = lambda f: jnp.concatenate([f(j) for j in range(N_GB)], axis=0)
        cat1 = lambda f: jnp.concatenate([f(j) for j in range(N_GB)], axis=1)
        parts = [
            (dcre_ref, cat0(lambda j: _dot_tn(srb[:, _gb_s(j)], dyb[:, _gb_u(j)]))),
            (dcim_ref, cat0(lambda j: -_dot_tn(sib[:, _gb_s(j)], dyb[:, _gb_u(j)]))),
            (dbre_ref, cat0(lambda j: _dot_tn(ub[:, _gb_u(j)], grb[:, _gb_s(j)]))),
            (dbim_ref, cat0(lambda j: _dot_tn(ub[:, _gb_u(j)], gib[:, _gb_s(j)]))),
            (dare_ref, _colsum(gr * spr + gi * spi)),
            (daim_ref, _colsum(gi * spr - gr * spi)),
            (dd_ref, _colsum(dyv * u)),
        ]
        du = cat1(lambda j: _dot_nt(grb[:, _gb_s(j)], bre_ref[_gb_u(j), :])
                  + _dot_nt(gib[:, _gb_s(j)], bim_ref[_gb_u(j), :]))
        du_ref[...] = du + d_ref[...] * dyv

        @pl.when(first)
        def _set():
            for ref, val in parts:
                ref[...] = val

        @pl.when(jnp.logical_not(first))
        def _add():
            for ref, val in parts:
                ref[...] = ref[...] + val

    row = lambda b, s: (b * ns + (ns - 1 - s), 0)
    prev = lambda b, s: (jnp.maximum((b * ns + (ns - 1 - s)) * t8 - 1, 0), 0)
    whole = lambda b, s: (0, 0)
    acc_shapes = [(N_ST, GB_U), (N_ST, GB_U), (D_SSM, GB_ST), (D_SSM, GB_ST), (1, N_ST), (1, N_ST), (1, D_SSM)]
    return pl.pallas_call(
        kern, name="ssm_bwd", grid=(B, ns),
        in_specs=[pl.BlockSpec((ts, D_SSM), row), pl.BlockSpec((ts, D_SSM), row),
                  pl.BlockSpec((ts, N_ST), row), pl.BlockSpec((ts, N_ST), row),
                  pl.BlockSpec((8, N_ST), prev), pl.BlockSpec((8, N_ST), prev),
                  pl.BlockSpec((D_SSM, GB_ST), whole), pl.BlockSpec((D_SSM, GB_ST), whole),
                  pl.BlockSpec((N_ST, GB_U), whole), pl.BlockSpec((N_ST, GB_U), whole),
                  pl.BlockSpec((1, D_SSM), whole), pl.BlockSpec((1, N_ST), whole), pl.BlockSpec((1, N_ST), whole)],
        out_specs=[pl.BlockSpec((ts, D_SSM), row)] + [pl.BlockSpec(sh, whole) for sh in acc_shapes],
        out_shape=[jax.ShapeDtypeStruct((T, D_SSM), F32)] + [jax.ShapeDtypeStruct(sh, F32) for sh in acc_shapes],
        scratch_shapes=[pltpu.VMEM((8, N_ST), F32), pltpu.VMEM((8, N_ST), F32),
                        pltpu.VMEM((ts, N_ST), F32), pltpu.VMEM((ts, N_ST), F32)],
        compiler_params=_params(("arbitrary", "arbitrary")),
    )(dy, proj, s_re, s_im, s_re, s_im, b_re, b_im, c_re, c_im, d_vec, a_re, a_im)


def modulation_fwd(c_all, ada_w, ada_b, fada_w, fada_b):
    def kern(c_ref, w_ref, b_ref, fw_ref, fb_ref, cond_ref, mod_ref, fmod_ref):
        cv = c_ref[...]
        cond = (cv * _sigmoid(cv)).astype(BF16)
        cond_ref[...] = cond
        mod_ref[...] = _dot(cond, w_ref[...].astype(BF16)) + b_ref[...]
        fmod_ref[...] = _dot(cond, fw_ref[...].astype(BF16)) + fb_ref[...]

    n = c_all.shape[0]
    return pl.pallas_call(
        kern, name="modulation_fwd",
        out_shape=[jax.ShapeDtypeStruct(c_all.shape, BF16), jax.ShapeDtypeStruct((n, ada_w.shape[1]), F32),
                   jax.ShapeDtypeStruct((n, fada_w.shape[1]), F32)],
        compiler_params=_params(),
    )(c_all, ada_w, ada_b, fada_w, fada_b)


def modulation_bwd(cond_all, dmod, dfmod, dmod_all):
    def kern(c_ref, dm_ref, dfm_ref, all_ref, gw_ref, gfw_ref, gb_ref):
        cond = c_ref[...]
        gw_ref[...] = _dot_tn(cond, dm_ref[...].astype(BF16))
        gfw_ref[...] = _dot_tn(cond, dfm_ref[...].astype(BF16))
        gb_ref[...] = _colsum(all_ref[...])

    d = cond_all.shape[1]
    return pl.pallas_call(
        kern, name="modulation_bwd",
        out_shape=[jax.ShapeDtypeStruct((d, dmod.shape[1]), F32), jax.ShapeDtypeStruct((d, dfmod.shape[1]), F32),
                   jax.ShapeDtypeStruct((1, dmod_all.shape[1]), F32)],
        compiler_params=_params(),
    )(cond_all, dmod, dfmod, dmod_all)


def adamw(name, parts, w, m, v):
    n, R, C = parts.shape
    tr = R
    while n * tr * C * 4 > 4 * 1024 * 1024 and tr % 16 == 0:
        tr //= 2

    def kern(p_ref, w_ref, m_ref, v_ref, g_o, d_o, m_o, v_o):
        g = p_ref[0]
        for i in range(1, n):
            g = g + p_ref[i]
        m_new = ADAM_B1 * m_ref[...] + (1.0 - ADAM_B1) * g
        v_new = ADAM_B2 * v_ref[...] + (1.0 - ADAM_B2) * (g * g)
        m_hat = m_new / (1.0 - ADAM_B1 ** ADAM_STEP)
        v_hat = v_new / (1.0 - ADAM_B2 ** ADAM_STEP)
        g_o[...] = g
        d_o[...] = -ADAM_LR * (m_hat / (jnp.sqrt(v_hat) + ADAM_EPS) + ADAM_WD * w_ref[...])
        m_o[...] = m_new
        v_o[...] = v_new

    blk = pl.BlockSpec((tr, C), lambda i: (i, 0))
    shp = jax.ShapeDtypeStruct((R, C), F32)
    return pl.pallas_call(
        kern, name=name, grid=(R // tr,),
        in_specs=[pl.BlockSpec((n, tr, C), lambda i: (0, i, 0)), blk, blk, blk],
        out_specs=[blk] * 4, out_shape=[shp] * 4,
        compiler_params=_params(("parallel",)),
    )(parts, w, m, v)


def _cols_from_gathered(g):
    return jnp.transpose(g, (1, 0, 2)).reshape(g.shape[1], N_DEV * g.shape[2])


def _cols_to_shards(w):
    k, n8 = w.shape
    return jnp.transpose(w.reshape(k, N_DEV, n8 // N_DEV), (1, 0, 2))


def _pad_w_in(w):
    z = functools.partial(jnp.zeros, dtype=w.dtype)
    k = w.shape[0]
    cut = D_SSM + Q_LORA + KV_LORA
    return jnp.concatenate([w[:, :cut], z((k, 64)), w[:, cut:], z((k, 32))], axis=1)


def _unpad_w_in(w):
    cut = D_SSM + Q_LORA + KV_LORA
    return jnp.concatenate([w[:, :cut], w[:, cut + 64:cut + 96]], axis=1)


def _pad_w_uq(w):
    k = w.shape[0]
    w3 = w.reshape(k, N_HEADS, QK_NOPE + QK_ROPE)
    return jnp.pad(w3, ((0, 0), (0, 0), (0, HEAD_PAD - QK_NOPE - QK_ROPE))).reshape(k, QK_W)


def _unpad_w_uq(w):
    k = w.shape[0]
    return w.reshape(k, N_HEADS, HEAD_PAD)[:, :, :QK_NOPE + QK_ROPE].reshape(k, N_HEADS * (QK_NOPE + QK_ROPE))


def _pad_w_ukv(w):
    k = w.shape[0]
    w3 = w.reshape(k, N_HEADS, QK_NOPE + V_HEAD)
    kpart = jnp.pad(w3[:, :, :QK_NOPE], ((0, 0), (0, 0), (0, HEAD_PAD - QK_NOPE))).reshape(k, QK_W)
    vpart = w3[:, :, QK_NOPE:].reshape(k, V_W)
    return jnp.concatenate([kpart, vpart], axis=1)


def _unpad_w_ukv(w):
    k = w.shape[0]
    kpart = w[:, :QK_W].reshape(k, N_HEADS, HEAD_PAD)[:, :, :QK_NOPE]
    vpart = w[:, QK_W:].reshape(k, N_HEADS, V_HEAD)
    return jnp.concatenate([kpart, vpart], axis=2).reshape(k, N_HEADS * (QK_NOPE + V_HEAD))


def _block_diag(blocks):
    g, r, c = blocks.shape
    b4 = blocks.reshape(N_GB, GB, r, c)
    keep = jnp.eye(GB, dtype=bool)[None, :, None, :, None]
    return jnp.where(keep, b4[:, :, :, None, :], 0).reshape(g * r, GB * c)


def _block_diag_take(stacked, r, c):
    d5 = stacked.reshape(N_GB, GB, r, GB, c)
    idx = jnp.arange(GB)
    return jnp.transpose(d5[:, idx, :, idx, :], (1, 0, 2, 3)).reshape(N_GROUPS, r, c)


SMALL = ["norm1_g", "ssm_lambda_re", "ssm_lambda_im", "ssm_b_re", "ssm_b_im", "ssm_c_re", "ssm_c_im",
         "ssm_d", "ssm_log_dt", "q_norm_g", "kv_norm_g", "ssm_out_g", "attn_out_g", "norm2_g", "final_norm_g"]
BIASES = ["ada_b", "final_ada_b"]
SMALL_COLS = 1024


def _pack_small(values):
    flat = jnp.concatenate([v.reshape(-1) for v in values])
    rows = -(-flat.shape[0] // (8 * SMALL_COLS)) * 8
    return jnp.pad(flat, (0, rows * SMALL_COLS - flat.shape[0])).reshape(rows, SMALL_COLS)


def _unpack_small(packed, like):
    flat = packed.reshape(-1)
    out, off = [], 0
    for ref in like:
        n = int(np.prod(ref.shape))
        out.append(flat[off:off + n].reshape(ref.shape))
        off += n
    return out


def kernel(x, c, positions, ada_w, ada_b, norm1_g, w_in, ssm_lambda_re, ssm_lambda_im, ssm_b_re, ssm_b_im, ssm_c_re, ssm_c_im, ssm_d, ssm_log_dt, w_glu, q_norm_g, w_uq, kv_norm_g, w_ukv, ssm_out_g, attn_out_g, w_out, norm2_g, w_ff1, w_ff2, final_ada_w, final_ada_b, final_norm_g, loss_target, m_ada_w, m_ada_b, m_norm1_g, m_w_in, m_ssm_lambda_re, m_ssm_lambda_im, m_ssm_b_re, m_ssm_b_im, m_ssm_c_re, m_ssm_c_im, m_ssm_d, m_ssm_log_dt, m_w_glu, m_q_norm_g, m_w_uq, m_kv_norm_g, m_w_ukv, m_ssm_out_g, m_attn_out_g, m_w_out, m_norm2_g, m_w_ff1, m_w_ff2, m_final_ada_w, m_final_ada_b, m_final_norm_g, v_ada_w, v_ada_b, v_norm1_g, v_w_in, v_ssm_lambda_re, v_ssm_lambda_im, v_ssm_b_re, v_ssm_b_im, v_ssm_c_re, v_ssm_c_im, v_ssm_d, v_ssm_log_dt, v_w_glu, v_q_norm_g, v_w_uq, v_kv_norm_g, v_w_ukv, v_ssm_out_g, v_attn_out_g, v_w_out, v_norm2_g, v_w_ff1, v_w_ff2, v_final_ada_w, v_final_ada_b, v_final_norm_g):
    given = dict(locals())
    B, S, D = x.shape
    T = B * S
    tm = min(256, S)
    me = 4 * lax.axis_index("x") + 2 * lax.axis_index("y") + lax.axis_index("c")

    big = ["w_in", "w_glu", "w_uq", "w_ukv", "w_out", "w_ff1", "w_ff2"]
    shards = {n: given[n][0] for n in big}
    gathered = all_gather("gather_weights", [c] + [shards[n].astype(BF16) for n in big])
    c_all = gathered[0].reshape(N_DEV * B, D)
    gw = dict(zip(big, gathered[1:]))
    w_in_p = _pad_w_in(_cols_from_gathered(gw["w_in"]))
    w_glu_f = _cols_from_gathered(gw["w_glu"])
    w_uq_p = _pad_w_uq(_cols_from_gathered(gw["w_uq"]))
    w_ukv_p = _pad_w_ukv(_cols_from_gathered(gw["w_ukv"]))
    w_out_f = gw["w_out"].reshape(-1, D)
    w_ff1_f = _cols_from_gathered(gw["w_ff1"])
    w_ff2_f = gw["w_ff2"].reshape(-1, D)

    n_mod = ada_w.shape[2]
    n_fmod = final_ada_w.shape[1]
    ada_b_sh = lax.dynamic_slice_in_dim(ada_b, me * n_mod, n_mod, axis=1)
    fada_b_sh = lax.dynamic_slice_in_dim(final_ada_b[None, :], me * n_fmod, n_fmod, axis=1)
    cond_all, mod_sh, fmod_sh = modulation_fwd(c_all, ada_w[0], ada_b_sh, final_ada_w, fada_b_sh)
    mod_g, fmod_g = all_gather("gather_modulation", [mod_sh, fmod_sh])
    mod_mine = lax.dynamic_slice_in_dim(_cols_from_gathered(mod_g), me * B, B, axis=0)
    fmod_mine = lax.dynamic_slice_in_dim(_cols_from_gathered(fmod_g), me * B, B, axis=0)
    shift1, scale1, gate1, shift2, scale2, gate2 = [mod_mine[:, None, i * D:(i + 1) * D] for i in range(6)]
    fshift, fscale = fmod_mine[:, None, :D], fmod_mine[:, None, D:]

    x2d = x.reshape(T, D)
    tgt2d = loss_target.reshape(T, D)
    pos = positions.astype(F32).reshape(T, 1)
    invf, m_lo, m_hi = _rope_consts()
    rope_mask = m_lo + m_hi

    def fwd_in(rows, bv, vecs, res):
        (xv,), (sc, sh), (g1,), (w,) = rows, bv, vecs, res
        r, xh = _rms_stats(xv)
        h = xh * g1 * (1.0 + sc) + sh
        return [_dot(h.astype(BF16), w[...])], [], []

    (proj,), _, _ = row_call("fwd_in", fwd_in, B, S, tm, [x2d], [scale1, shift1], [norm1_g], [w_in_p],
                             [(IN_PAD, F32)], [], [])

    def fwd_qkv(rows, bv, vecs, res):
        (pr, ps), (gq, gkv, fr, lo, hi), (wq, wkv) = rows, vecs, res
        cs, s_lo, s_hi = _rope_tables(ps, fr, lo, hi)
        _, qh = _rms_stats(pr[:, D_SSM:D_SSM + Q_LORA])
        _, kvh = _rms_stats(pr[:, D_SSM + Q_LORA:D_SSM + Q_LORA + KV_LORA])
        qf = _dot((qh * gq).astype(BF16), wq[...])
        kvf = _dot((kvh * gkv).astype(BF16), wkv[...])
        k_rope = _rope_tile(pr[:, IN_PAD - HEAD_PAD:], cs, s_lo, s_hi, 1.0)
        q_t, k_t = [], []
        for h in range(N_HEADS):
            hs = slice(h * HEAD_PAD, (h + 1) * HEAD_PAD)
            q_t.append(_rope_tile(qf[:, hs], cs, s_lo, s_hi, 1.0))
            k_t.append(kvf[:, hs] + k_rope)
        return [jnp.concatenate(q_t, axis=1), jnp.concatenate(k_t, axis=1), kvf[:, QK_W:]], [], []

    (q_r, k_r, v_r), _, _ = row_call(
        "fwd_qkv", fwd_qkv, B, S, tm, [proj, pos], [], [q_norm_g, kv_norm_g, invf, m_lo, m_hi],
        [w_uq_p, w_ukv_p], [(QK_W, BF16), (QK_W, BF16), (V_W, BF16)], [], [])
    y_attn, lse = attention_fwd(q_r, k_r, v_r, B, S, tm)

    rep = lambda a: jnp.repeat(a, GROUP, axis=0)
    lam_rep = [rep(ssm_lambda_re[0]), rep(ssm_lambda_im[0]),
               rep(jnp.broadcast_to(ssm_log_dt[0][:, None], (N_GROUPS, N_STATE)))]
    bt = lambda a: jnp.transpose(a, (0, 2, 1)).reshape(D_SSM, N_STATE)
    reps = lam_rep + [bt(ssm_b_re[0]), bt(ssm_b_im[0])]
    a_re_rep, a_im_rep, bb_re, bb_im = ssm_params_fwd(reps)
    a_re = a_re_rep.reshape(N_GROUPS, GROUP, N_STATE)[:, 0, :].reshape(1, N_ST)
    a_im = a_im_rep.reshape(N_GROUPS, GROUP, N_STATE)[:, 0, :].reshape(1, N_ST)
    bbd_re = _block_diag(bb_re.reshape(N_GROUPS, GROUP, N_STATE)).astype(BF16)
    bbd_im = _block_diag(bb_im.reshape(N_GROUPS, GROUP, N_STATE)).astype(BF16)
    cbd_re = _block_diag(jnp.transpose(ssm_c_re[0], (0, 2, 1))).astype(BF16)
    cbd_im = _block_diag(jnp.transpose(ssm_c_im[0], (0, 2, 1))).astype(BF16)
    d_vec = ssm_d.reshape(1, D_SSM)
    st_re, st_im, y_pre = ssm_fwd(proj, bbd_re, bbd_im, cbd_re, cbd_im, d_vec, a_re, a_im, B, S, tm)

    def fwd_mix(rows, bv, vecs, res):
        (yp, ya, xv), (g1v,), (gso, gao), (wg, wo) = rows, bv, vecs, res
        z = _dot(_gelu(yp).astype(BF16), wg[...])
        y_ssm = z[:, :D_SSM] * _sigmoid(z[:, D_SSM:])
        _, sh = _rms_stats(y_ssm)
        _, ah = _rms_stats(ya)
        o = _dot((sh * gso).astype(BF16), wo[0:D_SSM, :]) + _dot((ah * gao).astype(BF16), wo[D_SSM:, :])
        return [z, o, xv + g1v * o], [], []

    (z, o, x2), _, _ = row_call("fwd_mix", fwd_mix, B, S, tm, [y_pre, y_attn, x2d], [gate1],
                                [ssm_out_g, attn_out_g], [w_glu_f, w_out_f],
                                [(2 * D_SSM, F32), (D, F32), (D, F32)], [], [])

    def final_out(x3, gf, fsc, fsh):
        r3, xh3 = _rms_stats(x3)
        n3 = xh3 * gf
        return r3, xh3, n3, n3 * (1.0 + fsc) + fsh

    def mlp_hidden(xv, g2, sc, sh, w1):
        r2, xh2 = _rms_stats(xv)
        n2 = xh2 * g2
        h2 = n2 * (1.0 + sc) + sh
        return r2, xh2, n2, h2, _dot(h2.astype(BF16), w1[...])

    def fwd_mlp(rows, bv, vecs, res):
        (xv, tg), (sc, sh, g2v, fsc, fsh), (g2, gf), (w1, w2) = rows, bv, vecs, res
        _, _, _, _, a = mlp_hidden(xv, g2, sc, sh, w1)
        ra = jnp.maximum(a, 0.0)
        ff = _dot((ra * ra).astype(BF16), w2[...])
        x3 = xv + g2v * ff
        _, _, _, out = final_out(x3, gf, fsc, fsh)
        err = out - tg
        loss = 0.5 * jnp.sum(jnp.mean(err * err, axis=-1, keepdims=True), axis=0, keepdims=True)
        return [ff, x3], [], [jnp.broadcast_to(loss, (1, 128))]

    fnorm_g = final_norm_g[None, :]
    (ff, x3), _, (loss_acc,) = row_call(
        "fwd_mlp", fwd_mlp, B, S, tm, [x2, tgt2d], [scale2, shift2, gate2, fscale, fshift], [norm2_g, fnorm_g],
        [w_ff1_f, w_ff2_f], [(D, F32), (D, F32)], [], [(1, 128)])
    loss = lax.psum(loss_acc[0, 0], ("x", "y", "c"))

    def bwd_mlp(rows, bv, vecs, res):
        (x3v, tg, x2v, ffv), (sc, sh, g2v, fsc, fsh), (g2, gf), (w1, w2) = rows, bv, vecs, res
        r3, xh3, n3, out = final_out(x3v, gf, fsc, fsh)
        dout = (out - tg) * (1.0 / D)
        dn3 = dout * (1.0 + fsc)
        dx3 = _rms_bwd(dn3 * gf, xh3, r3)
        dff = dx3 * g2v
        r2, xh2, n2, h2, a = mlp_hidden(x2v, g2, sc, sh, w1)
        ra = jnp.maximum(a, 0.0)
        dffb = dff.astype(BF16)
        da = _dot_nt(dffb, w2[...]) * (2.0 * ra)
        dab = da.astype(BF16)
        dh2 = _dot_nt(dab, w1[...])
        dn2 = dh2 * (1.0 + sc)
        dx2 = dx3 + _rms_bwd(dn2 * g2, xh2, r2)
        return ([dx2, h2, ra * ra, dab, dffb],
                [_colsum(dout), _colsum(dout * n3), _colsum(dx3 * ffv), _colsum(dh2), _colsum(dh2 * n2)],
                [_colsum(dn3 * xh3), _colsum(dn2 * xh2)])

    ((dx2, h2b, fb, dab, dffb), (d_fshift, d_fscale, d_gate2, d_shift2, d_scale2), (d_gf, d_g2)) = row_call(
        "bwd_mlp", bwd_mlp, B, S, tm, [x3, tgt2d, x2, ff], [scale2, shift2, gate2, fscale, fshift],
        [norm2_g, fnorm_g], [w_ff1_f, w_ff2_f],
        [(D, F32), (D, BF16), (4 * D, BF16), (4 * D, BF16), (D, BF16)], [D] * 5, [(1, D), (1, D)])
    g_w_ff1 = matmul_tn("grad_w_ff1", h2b, dab)
    g_w_ff2 = matmul_tn("grad_w_ff2", fb, dffb)

    def bwd_mix(rows, bv, vecs, res):
        (dxv, ov, zv, ya, yp), (g1v,), (gso, gao), (wo, wg) = rows, bv, vecs, res
        do = (dxv * g1v).astype(BF16)
        dyn = _dot_nt(do, wo[...])
        z1, sg = zv[:, :D_SSM], _sigmoid(zv[:, D_SSM:])
        y_ssm = z1 * sg
        rs, sh = _rms_stats(y_ssm)
        ra, ah = _rms_stats(ya)
        dyn_s, dyn_a = dyn[:, :D_SSM], dyn[:, D_SSM:]
        dy_ssm = _rms_bwd(dyn_s * gso, sh, rs)
        dy_attn = _rms_bwd(dyn_a * gao, ah, ra)
        dz = jnp.concatenate([dy_ssm * sg, dy_ssm * z1 * sg * (1.0 - sg)], axis=1).astype(BF16)
        dy_pre = _dot_nt(dz, wg[...]) * _gelu_grad(yp)
        yn = jnp.concatenate([sh * gso, ah * gao], axis=1)
        return ([do, yn, dz, _gelu(yp), dy_attn, dy_pre], [_colsum(dxv * ov)],
                [_colsum(dyn_s * sh), _colsum(dyn_a * ah)])

    ((dob, ynb, dzb, ygb, dy_attn, dy_pre), (d_gate1,), (d_gso, d_gao)) = row_call(
        "bwd_mix", bwd_mix, B, S, tm, [dx2, o, z, y_attn, y_pre], [gate1], [ssm_out_g, attn_out_g],
        [w_out_f, w_glu_f], [(D, BF16), (D, BF16), (2 * D_SSM, BF16), (D_SSM, BF16), (V_W, F32), (D_SSM, F32)],
        [D], [(1, D_SSM), (1, V_W)])
    g_w_out = matmul_tn("grad_w_out", ynb, dob)
    g_w_glu = matmul_tn("grad_w_glu", ygb, dzb)

    dq3, dk_r, dv_r = attention_bwd(q_r, k_r, v_r, y_attn, dy_attn, lse, B, S, tm)
    dq_r = dq3.reshape(T, QK_W)

    def bwd_qkv(rows, bv, vecs, res):
        (dqv, dkv, dvv, pr, ps), (gq, gkv, fr, lo, hi, rmask), (wq, wkv) = rows, vecs, res
        cs, s_lo, s_hi = _rope_tables(ps, fr, lo, hi)
        dq_t = []
        dk_rope = jnp.zeros((dkv.shape[0], HEAD_PAD), F32)
        for h in range(N_HEADS):
            hs = slice(h * HEAD_PAD, (h + 1) * HEAD_PAD)
            dq_t.append(_rope_tile(dqv[:, hs], cs, s_lo, s_hi, -1.0))
            dk_rope = dk_rope + dkv[:, hs]
        dk_rope = _rope_tile(dk_rope * rmask, cs, s_lo, s_hi, -1.0)
        dqb = jnp.concatenate(dq_t, axis=1).astype(BF16)
        dkvb = jnp.concatenate([dkv, dvv], axis=1).astype(BF16)
        rq, qh = _rms_stats(pr[:, D_SSM:D_SSM + Q_LORA])
        rk, kvh = _rms_stats(pr[:, D_SSM + Q_LORA:D_SSM + Q_LORA + KV_LORA])
        dqn = _dot_nt(dqb, wq[...])
        dkvn = _dot_nt(dkvb, wkv[...])
        dpa = jnp.concatenate([_rms_bwd(dqn * gq, qh, rq), _rms_bwd(dkvn * gkv, kvh, rk), dk_rope], axis=1)
        return [dpa, qh * gq, kvh * gkv, dqb, dkvb], [], [_colsum(dqn * qh), _colsum(dkvn * kvh)]

    ((dpa, qnb, kvnb, dqb, dkvb), _, (d_gq, d_gkv)) = row_call(
        "bwd_qkv", bwd_qkv, B, S, tm, [dq_r, dk_r, dv_r, proj, pos], [],
        [q_norm_g, kv_norm_g, invf, m_lo, m_hi, rope_mask], [w_uq_p, w_ukv_p],
        [(Q_LORA + KV_LORA + HEAD_PAD, F32), (Q_LORA, BF16), (KV_LORA, BF16), (QK_W, BF16), (QK_W + V_W, BF16)],
        [], [(1, Q_LORA), (1, KV_LORA)])
    g_w_uq = _unpad_w_uq(matmul_tn("grad_w_uq", qnb, dqb))
    g_w_ukv = _unpad_w_ukv(matmul_tn("grad_w_ukv", kvnb, dkvb))

    du, dc_re_d, dc_im_d, db_re_d, db_im_d, da_re, da_im, d_d = ssm_bwd(
        dy_pre, proj, st_re, st_im, bbd_re, bbd_im, cbd_re, cbd_im, d_vec, a_re, a_im, B, S, tm)
    place = lambda a: jnp.zeros((N_GROUPS, GROUP, N_STATE), F32).at[:, 0, :].set(
        a.reshape(N_GROUPS, N_STATE)).reshape(D_SSM, N_STATE)
    cots = [place(da_re), place(da_im),
            _block_diag_take(db_re_d, GROUP, N_STATE).reshape(D_SSM, N_STATE),
            _block_diag_take(db_im_d, GROUP, N_STATE).reshape(D_SSM, N_STATE)]
    g_lam_re, g_lam_im, g_log_dt, g_bt_re, g_bt_im = ssm_params_bwd(reps, cots)
    unbt = lambda a: jnp.transpose(a.reshape(N_GROUPS, GROUP, N_STATE), (0, 2, 1))
    g_c_re = jnp.transpose(_block_diag_take(dc_re_d, N_STATE, GROUP), (0, 2, 1))
    g_c_im = jnp.transpose(_block_diag_take(dc_im_d, N_STATE, GROUP), (0, 2, 1))

    def bwd_in(rows, bv, vecs, res):
        (duv, dpav, xv, dx2v), (sc, sh), (g1,), (w,) = rows, bv, vecs, res
        dproj = jnp.concatenate([duv, dpav], axis=1).astype(BF16)
        dh = _dot_nt(dproj, w[...])
        r, xh = _rms_stats(xv)
        n1 = xh * g1
        dn1 = dh * (1.0 + sc)
        dx = dx2v + _rms_bwd(dn1 * g1, xh, r)
        return [dx, n1 * (1.0 + sc) + sh, dproj], [_colsum(dh), _colsum(dh * n1)], [_colsum(dn1 * xh)]

    ((grad_x, h1b, dprojb), (d_shift1, d_scale1), (d_g1,)) = row_call(
        "bwd_in", bwd_in, B, S, tm, [du, dpa, x2d, dx2], [scale1, shift1], [norm1_g], [w_in_p],
        [(D, F32), (D, BF16), (IN_PAD, BF16)], [D, D], [(1, D)])
    g_w_in = _unpad_w_in(matmul_tn("grad_w_in", h1b, dprojb))

    dmod = jnp.concatenate([d_shift1, d_scale1, d_gate1, d_shift2, d_scale2, d_gate2, d_fshift, d_fscale],
                           axis=2).reshape(B, 8 * D)
    small_part = {
        "norm1_g": d_g1, "ssm_lambda_re": g_lam_re,
        "ssm_lambda_im": g_lam_im, "ssm_b_re": unbt(g_bt_re), "ssm_b_im": unbt(g_bt_im), "ssm_c_re": g_c_re,
        "ssm_c_im": g_c_im, "ssm_d": d_d, "ssm_log_dt": g_log_dt, "q_norm_g": d_gq, "kv_norm_g": d_gkv,
        "ssm_out_g": d_gso, "attn_out_g": d_gao, "norm2_g": d_g2, "final_norm_g": d_gf}
    small_packed = _pack_small([small_part[n] for n in SMALL])
    dmod_g, small_g = all_gather("gather_small_grads", [dmod, small_packed])
    dmod_all = dmod_g.reshape(N_DEV * B, 8 * D)
    dm_sh = lax.dynamic_slice_in_dim(dmod_all[:, :6 * D], me * n_mod, n_mod, axis=1)
    dfm_sh = lax.dynamic_slice_in_dim(dmod_all[:, 6 * D:], me * n_fmod, n_fmod, axis=1)
    g_ada_w, g_fada_w, g_biases = modulation_bwd(cond_all, dm_sh, dfm_sh, dmod_all)

    exchanged = all_to_all(
        "exchange_weight_grads",
        [_cols_to_shards(g_w_in), g_w_glu, _cols_to_shards(g_w_uq), g_w_ukv, g_w_out.reshape(N_DEV, -1, D),
         g_w_ff1, g_w_ff2.reshape(N_DEV, -1, D)],
        [None, w_glu.shape[2], None, w_ukv.shape[2], None, w_ff1.shape[2], None])
    parts = dict(zip(big, exchanged))
    parts["ada_w"] = g_ada_w[None]
    parts["final_ada_w"] = g_fada_w[None]

    res_g, res_d, res_m, res_v = {}, {}, {}, {}
    for n in ["ada_w"] + big + ["final_ada_w"]:
        w_full = given[n]
        w2 = w_full.reshape(w_full.shape[-2], w_full.shape[-1])
        out = adamw("adamw_" + n, parts[n], w2, given["m_" + n].reshape(w2.shape), given["v_" + n].reshape(w2.shape))
        res_g[n], res_d[n], res_m[n], res_v[n] = [a.reshape(w_full.shape) for a in out]
    small_w = [given[n] for n in SMALL]
    out = adamw("adamw_small", small_g, _pack_small(small_w), _pack_small([given["m_" + n] for n in SMALL]),
                _pack_small([given["v_" + n] for n in SMALL]))
    for res, packed in zip((res_g, res_d, res_m, res_v), out):
        for n, val in zip(SMALL, _unpack_small(packed, small_w)):
            res[n] = val
    bias_w = [given[n] for n in BIASES]
    out = adamw("adamw_biases", g_biases.reshape(1, 8, D), _pack_small(bias_w),
                _pack_small([given["m_" + n] for n in BIASES]), _pack_small([given["v_" + n] for n in BIASES]))
    for res, packed in zip((res_g, res_d, res_m, res_v), out):
        for n, val in zip(BIASES, _unpack_small(packed, bias_w)):
            res[n] = val

    order = ["ada_w", "ada_b", "norm1_g", "w_in", "ssm_lambda_re", "ssm_lambda_im", "ssm_b_re", "ssm_b_im",
             "ssm_c_re", "ssm_c_im", "ssm_d", "ssm_log_dt", "w_glu", "q_norm_g", "w_uq", "kv_norm_g", "w_ukv",
             "ssm_out_g", "attn_out_g", "w_out", "norm2_g", "w_ff1", "w_ff2", "final_ada_w", "final_ada_b",
             "final_norm_g"]
    return (loss, grad_x.reshape(B, S, D), *[res_g[n] for n in order], *[res_d[n] for n in order],
            *[res_m[n] for n in order], *[res_v[n] for n in order])
```

```python
import functools
import math

import numpy as np
import jax
import jax.numpy as jnp
from jax import lax
from jax.experimental import pallas as pl
from jax.experimental.pallas import tpu as pltpu

F32 = jnp.float32
BF16 = jnp.bfloat16

N_DEV = 8
EPS = 1e-6
D_SSM = 512
N_GROUPS = 32
GROUP = 16
N_STATE = 64
N_ST = N_GROUPS * N_STATE
Q_LORA = 384
KV_LORA = 256
QK_NOPE = 64
QK_ROPE = 32
V_HEAD = 64
N_HEADS = 8
HEAD_PAD = 128
QK_W = N_HEADS * HEAD_PAD
V_W = N_HEADS * V_HEAD
IN_COLS = D_SSM + Q_LORA + KV_LORA + QK_ROPE
IN_PAD = D_SSM + Q_LORA + KV_LORA + HEAD_PAD
ROPE_BASE = 10000.0
ATTN_SCALE = (QK_NOPE + QK_ROPE) ** -0.5
NEG = -1e30

ADAM_LR = 0.001
ADAM_B1 = 0.9
ADAM_B2 = 0.999
ADAM_EPS = 1e-08
ADAM_WD = 0.01
ADAM_STEP = 10

VMEM_LIMIT = 56 * 1024 * 1024
MESH_ID = pl.DeviceIdType.MESH


def _dot(a, b):
    return jnp.dot(a, b, preferred_element_type=F32)


def _dot_nt(a, b):
    return lax.dot_general(a, b, (((1,), (1,)), ((), ())), preferred_element_type=F32)


def _dot_tn(a, b):
    return lax.dot_general(a, b, (((0,), (0,)), ((), ())), preferred_element_type=F32)


def _rms_stats(x):
    r = lax.rsqrt(jnp.mean(x * x, axis=-1, keepdims=True) + EPS)
    return r, x * r


def _rms_bwd(dy, xh, r):
    return r * (dy - xh * jnp.mean(dy * xh, axis=-1, keepdims=True))


def _sigmoid(x):
    return 1.0 / (1.0 + jnp.exp(-x))


_GELU_C = math.sqrt(2.0 / math.pi)


def _gelu(x):
    t = jnp.tanh(_GELU_C * (x + 0.044715 * x * x * x))
    return 0.5 * x * (1.0 + t)


def _gelu_grad(x):
    t = jnp.tanh(_GELU_C * (x + 0.044715 * x * x * x))
    return 0.5 * (1.0 + t) + 0.5 * x * (1.0 - t * t) * _GELU_C * (1.0 + 3.0 * 0.044715 * x * x)


def _colsum(a):
    return jnp.sum(a, axis=0, keepdims=True)


def _params(sem=None):
    return pltpu.CompilerParams(dimension_semantics=sem, vmem_limit_bytes=VMEM_LIMIT)


def _mesh_pos():
    x, y, c = lax.axis_index("x"), lax.axis_index("y"), lax.axis_index("c")
    return x, y, c, 4 * x + 2 * y + c


def _peer(x, y, c, k):
    px = 1 - x if k & 4 else x
    py = 1 - y if k & 2 else y
    pc = 1 - c if k & 1 else c
    return (px, py, pc), 4 * px + 2 * py + pc


def all_gather(name, arrays):
    n = len(arrays)

    def body(*refs):
        ins, outs = refs[:n], refs[n:2 * n]
        send_sems, recv_sems, loc_sems = refs[2 * n:]
        x, y, c, me = _mesh_pos()
        local, sent = [], []
        for i in range(n):
            cp = pltpu.make_async_copy(ins[i], outs[i].at[me], loc_sems.at[i])
            cp.start()
            local.append(cp)
            for k in range(1, N_DEV):
                peer, _ = _peer(x, y, c, k)
                rc = pltpu.make_async_remote_copy(
                    src_ref=ins[i], dst_ref=outs[i].at[me], send_sem=send_sems.at[i, k - 1],
                    recv_sem=recv_sems.at[i, k - 1], device_id=peer, device_id_type=MESH_ID)
                rc.start()
                sent.append(rc)
        for i in range(n):
            for k in range(1, N_DEV):
                peer, pidx = _peer(x, y, c, k)
                pltpu.make_async_remote_copy(
                    src_ref=ins[i], dst_ref=outs[i].at[pidx], send_sem=send_sems.at[i, k - 1],
                    recv_sem=recv_sems.at[i, k - 1], device_id=peer, device_id_type=MESH_ID).wait_recv()
        for rc in sent:
            rc.wait_send()
        for cp in local:
            cp.wait()

    any_spec = pl.BlockSpec(memory_space=pl.ANY)
    return pl.pallas_call(
        body, name=name,
        out_shape=[jax.ShapeDtypeStruct((N_DEV,) + a.shape, a.dtype) for a in arrays],
        in_specs=[any_spec] * n, out_specs=[any_spec] * n,
        scratch_shapes=[pltpu.SemaphoreType.DMA((n, N_DEV - 1)), pltpu.SemaphoreType.DMA((n, N_DEV - 1)),
                        pltpu.SemaphoreType.DMA((n,))],
    )(*arrays)


def all_to_all(name, arrays, col_widths):
    n = len(arrays)
    shard_shapes = [a.shape[1:] if w is None else (a.shape[0], w) for a, w in zip(arrays, col_widths)]

    def body(*refs):
        ins, outs = refs[:n], refs[n:2 * n]
        send_sems, recv_sems, loc_sems = refs[2 * n:]
        x, y, c, me = _mesh_pos()

        def src(i, j):
            w = col_widths[i]
            if w is None:
                return ins[i].at[j]
            return ins[i].at[:, pl.ds(pl.multiple_of(j * w, 128), w)]

        local, sent = [], []
        for i in range(n):
            cp = pltpu.make_async_copy(src(i, me), outs[i].at[me], loc_sems.at[i])
            cp.start()
            local.append(cp)
            for k in range(1, N_DEV):
                peer, pidx = _peer(x, y, c, k)
                rc = pltpu.make_async_remote_copy(
                    src_ref=src(i, pidx), dst_ref=outs[i].at[me], send_sem=send_sems.at[i, k - 1],
                    recv_sem=recv_sems.at[i, k - 1], device_id=peer, device_id_type=MESH_ID)
                rc.start()
                sent.append(rc)
        for i in range(n):
            for k in range(1, N_DEV):
                peer, pidx = _peer(x, y, c, k)
                pltpu.make_async_remote_copy(
                    src_ref=src(i, pidx), dst_ref=outs[i].at[pidx], send_sem=send_sems.at[i, k - 1],
                    recv_sem=recv_sems.at[i, k - 1], device_id=peer, device_id_type=MESH_ID).wait_recv()
        for rc in sent:
            rc.wait_send()
        for cp in local:
            cp.wait()

    any_spec = pl.BlockSpec(memory_space=pl.ANY)
    return pl.pallas_call(
        body, name=name,
        out_shape=[jax.ShapeDtypeStruct((N_DEV,) + tuple(s), a.dtype) for a, s in zip(arrays, shard_shapes)],
        in_specs=[any_spec] * n, out_specs=[any_spec] * n,
        scratch_shapes=[pltpu.SemaphoreType.DMA((n, N_DEV - 1)), pltpu.SemaphoreType.DMA((n, N_DEV - 1)),
                        pltpu.SemaphoreType.DMA((n,))],
    )(*arrays)


def row_call(name, body, B, S, tm, rows, bvecs, vecs, res, row_outs, bacc, gacc):
    ns = S // tm
    T = B * S
    n_r, n_b, n_v, n_w = len(rows), len(bvecs), len(vecs), len(res)
    n_ro, n_ba, n_ga = len(row_outs), len(bacc), len(gacc)

    def kern(*refs):
        i = 0
        r_refs = refs[i:i + n_r]; i += n_r
        b_refs = refs[i:i + n_b]; i += n_b
        v_refs = refs[i:i + n_v]; i += n_v
        w_hbm = refs[i:i + n_w]; i += n_w
        ro_refs = refs[i:i + n_ro]; i += n_ro
        ba_refs = refs[i:i + n_ba]; i += n_ba
        ga_refs = refs[i:i + n_ga]; i += n_ga
        w_vmem = refs[i:i + n_w]
        b = pl.program_id(0)
        s = pl.program_id(1)
        first = jnp.logical_and(b == 0, s == 0)

        if n_w:
            @pl.when(first)
            def _load_weights():
                for h, v in zip(w_hbm, w_vmem):
                    pltpu.sync_copy(h, v)

        ro, ba, ga = body([r[...] for r in r_refs], [r[0] for r in b_refs], [r[...] for r in v_refs],
                          list(w_vmem))
        for ref, val in zip(ro_refs, ro):
            ref[...] = val.astype(ref.dtype)

        def accumulate(ref, val, is_first, idx):
            @pl.when(is_first)
            def _set():
                ref[idx] = val

            @pl.when(jnp.logical_not(is_first))
            def _add():
                ref[idx] = ref[idx] + val

        for ref, val in zip(ba_refs, ba):
            accumulate(ref, val, s == 0, 0)
        for ref, val in zip(ga_refs, ga):
            accumulate(ref, val, first, Ellipsis)

    row_arrays = [r[0] if isinstance(r, tuple) else r for r in rows]
    row_in_specs = [pl.BlockSpec(r[1], r[2]) if isinstance(r, tuple)
                    else pl.BlockSpec((tm, r.shape[1]), lambda b, s: (b * ns + s, 0)) for r in rows]
    ro_shapes = [jax.ShapeDtypeStruct(tuple(o[0]), o[1]) if len(o) == 4 else jax.ShapeDtypeStruct((T, o[0]), o[1])
                 for o in row_outs]
    ro_specs = [pl.BlockSpec(o[2], o[3]) if len(o) == 4 else pl.BlockSpec((tm, o[0]), lambda b, s: (b * ns + s, 0))
                for o in row_outs]
    in_specs = (row_in_specs
                + [pl.BlockSpec((1, 1, v.shape[2]), lambda b, s: (b, 0, 0)) for v in bvecs]
                + [pl.BlockSpec(v.shape, lambda b, s, nd=v.ndim: (0,) * nd) for v in vecs]
                + [pl.BlockSpec(memory_space=pl.ANY) for _ in res])
    out_shape = (ro_shapes
                 + [jax.ShapeDtypeStruct((B, 1, w), F32) for w in bacc]
                 + [jax.ShapeDtypeStruct(tuple(sh), F32) for sh in gacc])
    out_specs = (ro_specs
                 + [pl.BlockSpec((1, 1, w), lambda b, s: (b, 0, 0)) for w in bacc]
                 + [pl.BlockSpec(tuple(sh), lambda b, s, nd=len(sh): (0,) * nd) for sh in gacc])
    outs = pl.pallas_call(
        kern, name=name, grid=(B, ns), in_specs=in_specs, out_specs=out_specs, out_shape=out_shape,
        scratch_shapes=[pltpu.VMEM(w.shape, w.dtype) for w in res],
        compiler_params=_params(("arbitrary", "arbitrary")),
    )(*row_arrays, *bvecs, *vecs, *res)
    return outs[:n_ro], outs[n_ro:n_ro + n_ba], outs[n_ro + n_ba:]


def matmul_tn(name, a, b):
    T, K = a.shape
    N = b.shape[1]
    tk = min(K, 512)
    tn = 512 if N % 512 == 0 else 256
    tt = min(T, 512)
    nt = T // tt

    def kern(a_ref, b_ref, o_ref):
        t = pl.program_id(2)
        part = _dot_tn(a_ref[...], b_ref[...])

        @pl.when(t == 0)
        def _set():
            o_ref[...] = part

        @pl.when(t > 0)
        def _add():
            o_ref[...] = o_ref[...] + part

    return pl.pallas_call(
        kern, name=name, grid=(K // tk, N // tn, nt),
        in_specs=[pl.BlockSpec((tt, tk), lambda i, j, t: (t, i)), pl.BlockSpec((tt, tn), lambda i, j, t: (t, j))],
        out_specs=pl.BlockSpec((tk, tn), lambda i, j, t: (i, j)),
        out_shape=jax.ShapeDtypeStruct((K, N), F32),
        compiler_params=_params(("parallel", "parallel", "arbitrary")),
    )(a, b)


def _rope_consts():
    inv_freq = ROPE_BASE ** (-jnp.arange(0, QK_ROPE, 2, dtype=F32) / QK_ROPE)
    zeros64 = jnp.zeros((64,), F32)
    zeros32 = jnp.zeros((32,), F32)
    zeros16 = jnp.zeros((16,), F32)
    ones16 = jnp.ones((16,), F32)
    invf = jnp.concatenate([zeros64, inv_freq, inv_freq, zeros32])[None, :]
    m_lo = jnp.concatenate([zeros64, ones16, zeros16, zeros32])[None, :]
    m_hi = jnp.concatenate([zeros64, zeros16, ones16, zeros32])[None, :]
    return invf, m_lo, m_hi


def _rope_tables(pos, invf, m_lo, m_hi):
    ang = pos * invf
    return jnp.cos(ang), jnp.sin(ang) * m_lo, jnp.sin(ang) * m_hi


def _rope_tile(t, cs, s_lo, s_hi, sign):
    up = pltpu.roll(t, HEAD_PAD - 16, axis=1)
    down = pltpu.roll(t, 16, axis=1)
    return t * cs + sign * (down * s_hi - up * s_lo)


def _heads(ref, width):
    return jnp.stack([ref[:, h * width:(h + 1) * width] for h in range(N_HEADS)], axis=0)


def _bmm(a, b, ca, cb):
    return lax.dot_general(a, b, (((ca,), (cb,)), ((0,), (0,))), preferred_element_type=F32)


def _scores_t(k_ref, q_ref, masked, tq):
    st = _bmm(_heads(k_ref, HEAD_PAD), _heads(q_ref, HEAD_PAD), 2, 2) * ATTN_SCALE
    if masked:
        key = lax.broadcasted_iota(jnp.int32, (tq, tq), 0)
        qry = lax.broadcasted_iota(jnp.int32, (tq, tq), 1)
        st = jnp.where((key <= qry)[None], st, NEG)
    return st


def attention_fwd(q, k, vt, B, S, tq):
    nq = S // tq

    def kern(q_ref, k_ref, vt_ref, o_ref, lse_ref, m_scr, l_scr, acc_scr):
        qi, ki = pl.program_id(1), pl.program_id(2)

        @pl.when(ki == 0)
        def _init():
            m_scr[...] = jnp.full(m_scr.shape, NEG, F32)
            l_scr[...] = jnp.zeros(l_scr.shape, F32)
            acc_scr[...] = jnp.zeros(acc_scr.shape, F32)

        def block(masked):
            st = _scores_t(k_ref, q_ref, masked, tq)
            m_prev = m_scr[...]
            m_new = jnp.maximum(m_prev, jnp.max(st, axis=1, keepdims=True))
            alpha = jnp.exp(m_prev - m_new)
            p = jnp.exp(st - m_new)
            l_scr[...] = alpha * l_scr[...] + jnp.sum(p, axis=1, keepdims=True)
            vt3 = vt_ref[...].reshape(N_HEADS, V_HEAD, tq)
            acc_scr[...] = alpha * acc_scr[...] + _bmm(vt3, p.astype(BF16), 2, 1)
            m_scr[...] = m_new

        @pl.when(ki < qi)
        def _full():
            block(False)

        @pl.when(ki == qi)
        def _diagonal():
            block(True)
            ot = acc_scr[...] / l_scr[...]
            for h in range(N_HEADS):
                o_ref[:, h * V_HEAD:(h + 1) * V_HEAD] = ot[h].T
            lse_ref[...] = m_scr[...] + jnp.log(l_scr[...])

    T = B * S
    return pl.pallas_call(
        kern, name="attention_fwd", grid=(B, nq, nq),
        in_specs=[pl.BlockSpec((tq, QK_W), lambda b, i, j: (b * nq + i, 0)),
                  pl.BlockSpec((tq, QK_W), lambda b, i, j: (b * nq + jnp.minimum(i, j), 0)),
                  pl.BlockSpec((V_W, tq), lambda b, i, j: (0, b * nq + jnp.minimum(i, j)))],
        out_specs=[pl.BlockSpec((tq, V_W), lambda b, i, j: (b * nq + i, 0)),
                   pl.BlockSpec((N_HEADS, 1, tq), lambda b, i, j: (0, 0, b * nq + i))],
        out_shape=[jax.ShapeDtypeStruct((T, V_W), F32), jax.ShapeDtypeStruct((N_HEADS, 1, T), F32)],
        scratch_shapes=[pltpu.VMEM((N_HEADS, 1, tq), F32), pltpu.VMEM((N_HEADS, 1, tq), F32),
                        pltpu.VMEM((N_HEADS, V_HEAD, tq), F32)],
        compiler_params=_params(("parallel", "parallel", "arbitrary")),
    )(q, k, vt)


def attention_bwd(q, k, kt, v, do, lse, delta, B, S, tq):
    nq = S // tq

    def kern(q_ref, k_ref, kt_ref, v_ref, do_ref, lse_ref, delta_ref, dq_hbm, dk_ref, dv_ref,
             dq_acc, dk_acc, dv_acc):
        b, kj, qi = pl.program_id(0), pl.program_id(1), pl.program_id(2)

        @pl.when(jnp.logical_and(kj == 0, qi == 0))
        def _zero_dq():
            dq_acc[...] = jnp.zeros(dq_acc.shape, F32)

        @pl.when(qi == 0)
        def _zero_dkv():
            dk_acc[...] = jnp.zeros(dk_acc.shape, F32)
            dv_acc[...] = jnp.zeros(dv_acc.shape, F32)

        def block(masked):
            pt = jnp.exp(_scores_t(k_ref, q_ref, masked, tq) - lse_ref[...])
            do3 = _heads(do_ref, V_HEAD).astype(BF16)
            dv_acc[...] = dv_acc[...] + _bmm(pt.astype(BF16), do3, 2, 1)
            dpt = _bmm(_heads(v_ref, V_HEAD), do3, 2, 2)
            dst = (pt * (dpt - delta_ref[...]) * ATTN_SCALE).astype(BF16)
            dk_acc[...] = dk_acc[...] + _bmm(dst, _heads(q_ref, HEAD_PAD), 2, 1)
            q_cols = pl.ds(pl.multiple_of(qi * tq, tq), tq)
            kt3 = kt_ref[...].reshape(N_HEADS, HEAD_PAD, tq)
            dq_acc[:, :, q_cols] = dq_acc[:, :, q_cols] + _bmm(kt3, dst, 2, 1)

        @pl.when(qi > kj)
        def _full():
            block(False)

        @pl.when(qi == kj)
        def _diagonal():
            block(True)

        @pl.when(qi == nq - 1)
        def _write_dkv():
            for h in range(N_HEADS):
                dk_ref[:, h * HEAD_PAD:(h + 1) * HEAD_PAD] = dk_acc[h]
                dv_ref[:, h * V_HEAD:(h + 1) * V_HEAD] = dv_acc[h]

        @pl.when(jnp.logical_and(kj == nq - 1, qi == nq - 1))
        def _write_dq():
            pltpu.sync_copy(dq_acc, dq_hbm.at[b])

    T = B * S
    qrow = lambda b, j, i: (b * nq + jnp.maximum(i, j), 0)
    qcol3 = lambda b, j, i: (0, 0, b * nq + jnp.maximum(i, j))
    krow = lambda b, j, i: (b * nq + j, 0)
    return pl.pallas_call(
        kern, name="attention_bwd", grid=(B, nq, nq),
        in_specs=[pl.BlockSpec((tq, QK_W), qrow), pl.BlockSpec((tq, QK_W), krow),
                  pl.BlockSpec((QK_W, tq), lambda b, j, i: (0, b * nq + j)), pl.BlockSpec((tq, V_W), krow),
                  pl.BlockSpec((tq, V_W), qrow), pl.BlockSpec((N_HEADS, 1, tq), qcol3),
                  pl.BlockSpec((N_HEADS, 1, tq), qcol3)],
        out_specs=[pl.BlockSpec(memory_space=pl.ANY), pl.BlockSpec((tq, QK_W), krow), pl.BlockSpec((tq, V_W), krow)],
        out_shape=[jax.ShapeDtypeStruct((B, N_HEADS, HEAD_PAD, S), F32), jax.ShapeDtypeStruct((T, QK_W), F32),
                   jax.ShapeDtypeStruct((T, V_W), F32)],
        scratch_shapes=[pltpu.VMEM((N_HEADS, HEAD_PAD, S), F32), pltpu.VMEM((N_HEADS, tq, HEAD_PAD), F32),
                        pltpu.VMEM((N_HEADS, tq, V_HEAD), F32)],
        compiler_params=_params(("arbitrary", "arbitrary", "arbitrary")),
    )(q, k, kt, v, do, lse, delta)


GB = 8
N_GB = N_GROUPS // GB
GB_U = GB * GROUP
GB_ST = GB * N_STATE


def _gb_u(j):
    return slice(j * GB_U, (j + 1) * GB_U)


def _gb_s(j):
    return slice(j * GB_ST, (j + 1) * GB_ST)


def ssm_param_fn(lr, li, ld, br, bi):
    dt = jnp.exp(ld)
    er = jnp.exp(lr * dt)
    ar = er * jnp.cos(li * dt)
    ai = er * jnp.sin(li * dt)
    nr = ar - 1.0
    den = lr * lr + li * li
    cr = (nr * lr + ai * li) / den
    ci = (ai * lr - nr * li) / den
    return ar, ai, cr * br - ci * bi, cr * bi + ci * br


def ssm_params_fwd(reps):
    def kern(lr, li, ld, br, bi, ar_o, ai_o, bbr_o, bbi_o):
        ar, ai, bbr, bbi = ssm_param_fn(lr[...], li[...], ld[...], br[...], bi[...])
        ar_o[...] = ar
        ai_o[...] = ai
        bbr_o[...] = bbr
        bbi_o[...] = bbi

    shp = jax.ShapeDtypeStruct(reps[0].shape, F32)
    return pl.pallas_call(kern, name="ssm_params_fwd", out_shape=[shp] * 4)(*reps)


def ssm_params_bwd(reps, cots):
    rows = reps[0].shape[0]

    def kern(lr, li, ld, br, bi, d_ar, d_ai, d_bbr, d_bbi, dlr_o, dli_o, dld_o, dbr_o, dbi_o):
        _, vjp = jax.vjp(ssm_param_fn, lr[...], li[...], ld[...], br[...], bi[...])
        dlr, dli, dld, dbr, dbi = vjp((d_ar[...], d_ai[...], d_bbr[...], d_bbi[...]))
        dlr_o[...] = jnp.sum(dlr.reshape(N_GROUPS, GROUP, N_STATE), axis=1)
        dli_o[...] = jnp.sum(dli.reshape(N_GROUPS, GROUP, N_STATE), axis=1)
        dld_o[...] = jnp.sum(jnp.sum(dld.reshape(N_GROUPS, GROUP, N_STATE), axis=1), axis=-1, keepdims=True)
        dbr_o[...] = dbr
        dbi_o[...] = dbi

    g = jax.ShapeDtypeStruct((N_GROUPS, N_STATE), F32)
    full = jax.ShapeDtypeStruct((rows, N_STATE), F32)
    return pl.pallas_call(
        kern, name="ssm_params_bwd",
        out_shape=[g, g, jax.ShapeDtypeStruct((N_GROUPS, 1), F32), full, full])(*reps, *cots)


def ssm_fwd(proj, b_re, b_im, c_re, c_im, d_vec, a_re, a_im, B, S, ts):
    ns = S // ts
    T = B * S

    def kern(u_ref, bre_ref, bim_ref, cre_ref, cim_ref, d_ref, are_ref, aim_ref,
             sre_ref, sim_ref, y_ref, car_re, car_im, bu_re, bu_im):
        s = pl.program_id(1)

        @pl.when(s == 0)
        def _reset():
            car_re[...] = jnp.zeros(car_re.shape, F32)
            car_im[...] = jnp.zeros(car_im.shape, F32)

        u = u_ref[...]
        ub = u.astype(BF16)
        for j in range(N_GB):
            uj, bj, sj = ub[:, _gb_u(j)], _gb_u(j), _gb_s(j)
            bu_re[:, sj] = _dot(uj, bre_ref[bj, :])
            bu_im[:, sj] = _dot(uj, bim_ref[bj, :])
        ar, ai = are_ref[...], aim_ref[...]

        def step(t, carry):
            xr, xi = carry
            row = pl.ds(t, 1)
            nr = ar * xr - ai * xi + bu_re[row, :]
            ni = ar * xi + ai * xr + bu_im[row, :]
            sre_ref[row, :] = nr
            sim_ref[row, :] = ni
            return nr, ni

        xr, xi = lax.fori_loop(0, ts, step, (car_re[0:1, :], car_im[0:1, :]))
        car_re[0:1, :] = xr
        car_im[0:1, :] = xi
        y = [_dot(sre_ref[:, _gb_s(j)].astype(BF16), cre_ref[_gb_s(j), :])
             - _dot(sim_ref[:, _gb_s(j)].astype(BF16), cim_ref[_gb_s(j), :]) for j in range(N_GB)]
        y_ref[...] = jnp.concatenate(y, axis=1) + d_ref[...] * u

    row = lambda b, s: (b * ns + s, 0)
    whole = lambda b, s: (0, 0)
    return pl.pallas_call(
        kern, name="ssm_fwd", grid=(B, ns),
        in_specs=[pl.BlockSpec((ts, D_SSM), row),
                  pl.BlockSpec((D_SSM, GB_ST), whole), pl.BlockSpec((D_SSM, GB_ST), whole),
                  pl.BlockSpec((N_ST, GB_U), whole), pl.BlockSpec((N_ST, GB_U), whole),
                  pl.BlockSpec((1, D_SSM), whole), pl.BlockSpec((1, N_ST), whole), pl.BlockSpec((1, N_ST), whole)],
        out_specs=[pl.BlockSpec((ts, N_ST), row), pl.BlockSpec((ts, N_ST), row), pl.BlockSpec((ts, D_SSM), row)],
        out_shape=[jax.ShapeDtypeStruct((T, N_ST), F32), jax.ShapeDtypeStruct((T, N_ST), F32),
                   jax.ShapeDtypeStruct((T, D_SSM), F32)],
        scratch_shapes=[pltpu.VMEM((8, N_ST), F32), pltpu.VMEM((8, N_ST), F32),
                        pltpu.VMEM((ts, N_ST), F32), pltpu.VMEM((ts, N_ST), F32)],
        compiler_params=_params(("arbitrary", "arbitrary")),
    )(proj, b_re, b_im, c_re, c_im, d_vec, a_re, a_im)


def ssm_bwd(dy, proj, s_re, s_im, b_re, b_im, c_re, c_im, d_vec, a_re, a_im, B, S, ts):
    ns = S // ts
    T = B * S
    t8 = ts // 8

    def kern(dy_ref, u_ref, sre_ref, sim_ref, pre_ref, pim_ref, bre_ref, bim_ref, cre_ref, cim_ref,
             d_ref, are_ref, aim_ref,
             du_ref, dcre_ref, dcim_ref, dbre_ref, dbim_ref, dare_ref, daim_ref, dd_ref,
             car_re, car_im, g_re, g_im):
        b, s = pl.program_id(0), pl.program_id(1)
        first = jnp.logical_and(b == 0, s == 0)

        @pl.when(s == 0)
        def _reset():
            car_re[...] = jnp.zeros(car_re.shape, F32)
            car_im[...] = jnp.zeros(car_im.shape, F32)

        dyv = dy_ref[...]
        dyb = dyv.astype(BF16)
        u = u_ref[...]
        for j in range(N_GB):
            g_re[:, _gb_s(j)] = _dot_nt(dyb[:, _gb_u(j)], cre_ref[_gb_s(j), :])
            g_im[:, _gb_s(j)] = -_dot_nt(dyb[:, _gb_u(j)], cim_ref[_gb_s(j), :])
        ar, ai = are_ref[...], aim_ref[...]

        def step(i, carry):
            gr_next, gi_next = carry
            row = pl.ds(ts - 1 - i, 1)
            gr = g_re[row, :] + ar * gr_next + ai * gi_next
            gi = g_im[row, :] + ar * gi_next - ai * gr_next
            g_re[row, :] = gr
            g_im[row, :] = gi
            return gr, gi

        gr0, gi0 = lax.fori_loop(0, ts, step, (car_re[0:1, :], car_im[0:1, :]))
        car_re[0:1, :] = gr0
        car_im[0:1, :] = gi0

        gr, gi = g_re[...], g_im[...]
        sr, si = sre_ref[...], sim_ref[...]
        has_prev = (s < ns - 1).astype(F32)
        row_id = lax.broadcasted_iota(jnp.int32, (ts, N_ST), 0)
        spr = jnp.where(row_id == 0, pre_ref[7:8, :] * has_prev, pltpu.roll(sr, 1, axis=0))
        spi = jnp.where(row_id == 0, pim_ref[7:8, :] * has_prev, pltpu.roll(si, 1, axis=0))
        grb, gib, ub = gr.astype(BF16), gi.astype(BF16), u.astype(BF16)
        srb, sib = sr.astype(BF16), si.astype(BF16)
        cat0 = lambda f: jnp.concatenate([f(j) for j in range(N_GB)], axis=0)
        cat1 = lambda f: jnp.concatenate([f(j) for j in range(N_GB)], axis=1)
        parts = [
            (dcre_ref, cat0(lambda j: _dot_tn(srb[:, _gb_s(j)], dyb[:, _gb_u(j)]))),
            (dcim_ref, cat0(lambda j: -_dot_tn(sib[:, _gb_s(j)], dyb[:, _gb_u(j)]))),
            (dbre_ref, cat0(lambda j: _dot_tn(ub[:, _gb_u(j)], grb[:, _gb_s(j)]))),
            (dbim_ref, cat0(lambda j: _dot_tn(ub[:, _gb_u(j)], gib[:, _gb_s(j)]))),
            (dare_ref, _colsum(gr * spr + gi * spi)),
            (daim_ref, _colsum(gi * spr - gr * spi)),
            (dd_ref, _colsum(dyv * u)),
        ]
        du = cat1(lambda j: _dot_nt(grb[:, _gb_s(j)], bre_ref[_gb_u(j), :])
                  + _dot_nt(gib[:, _gb_s(j)], bim_ref[_gb_u(j), :]))
        du_ref[...] = du + d_ref[...] * dyv

        @pl.when(first)
        def _set():
            for ref, val in parts:
                ref[...] = val

        @pl.when(jnp.logical_not(first))
        def _add():
            for ref, val in parts:
                ref[...] = ref[...] + val

    row = lambda b, s: (b * ns + (ns - 1 - s), 0)
    prev = lambda b, s: (jnp.maximum((b * ns + (ns - 1 - s)) * t8 - 1, 0), 0)
    whole = lambda b, s: (0, 0)
    acc_shapes = [(N_ST, GB_U), (N_ST, GB_U), (D_SSM, GB_ST), (D_SSM, GB_ST), (1, N_ST), (1, N_ST), (1, D_SSM)]
    return pl.pallas_call(
        kern, name="ssm_bwd", grid=(B, ns),
        in_specs=[pl.BlockSpec((ts, D_SSM), row), pl.BlockSpec((ts, D_SSM), row),
                  pl.BlockSpec((ts, N_ST), row), pl.BlockSpec((ts, N_ST), row),
                  pl.BlockSpec((8, N_ST), prev), pl.BlockSpec((8, N_ST), prev),
                  pl.BlockSpec((D_SSM, GB_ST), whole), pl.BlockSpec((D_SSM, GB_ST), whole),
                  pl.BlockSpec((N_ST, GB_U), whole), pl.BlockSpec((N_ST, GB_U), whole),
                  pl.BlockSpec((1, D_SSM), whole), pl.BlockSpec((1, N_ST), whole), pl.BlockSpec((1, N_ST), whole)],
        out_specs=[pl.BlockSpec((ts, D_SSM), row)] + [pl.BlockSpec(sh, whole) for sh in acc_shapes],
        out_shape=[jax.ShapeDtypeStruct((T, D_SSM), F32)] + [jax.ShapeDtypeStruct(sh, F32) for sh in acc_shapes],
        scratch_shapes=[pltpu.VMEM((8, N_ST), F32), pltpu.VMEM((8, N_ST), F32),
                        pltpu.VMEM((ts, N_ST), F32), pltpu.VMEM((ts, N_ST), F32)],
        compiler_params=_params(("arbitrary", "arbitrary")),
    )(dy, proj, s_re, s_im, s_re, s_im, b_re, b_im, c_re, c_im, d_vec, a_re, a_im)


def modulation_fwd(c_all, ada_w, ada_b, fada_w, fada_b):
    def kern(c_ref, w_ref, b_ref, fw_ref, fb_ref, cond_ref, mod_ref, fmod_ref):
        cv = c_ref[...]
        cond = (cv * _sigmoid(cv)).astype(BF16)
        cond_ref[...] = cond
        mod_ref[...] = _dot(cond, w_ref[...].astype(BF16)) + b_ref[...]
        fmod_ref[...] = _dot(cond, fw_ref[...].astype(BF16)) + fb_ref[...]

    n = c_all.shape[0]
    return pl.pallas_call(
        kern, name="modulation_fwd",
        out_shape=[jax.ShapeDtypeStruct(c_all.shape, BF16), jax.ShapeDtypeStruct((n, ada_w.shape[1]), F32),
                   jax.ShapeDtypeStruct((n, fada_w.shape[1]), F32)],
        compiler_params=_params(),
    )(c_all, ada_w, ada_b, fada_w, fada_b)


def modulation_bwd(cond_all, dmod, dfmod, dmod_all):
    def kern(c_ref, dm_ref, dfm_ref, all_ref, gw_ref, gfw_ref, gb_ref):
        cond = c_ref[...]
        gw_ref[...] = _dot_tn(cond, dm_ref[...].astype(BF16))
        gfw_ref[...] = _dot_tn(cond, dfm_ref[...].astype(BF16))
        gb_ref[...] = _colsum(all_ref[...])

    d = cond_all.shape[1]
    return pl.pallas_call(
        kern, name="modulation_bwd",
        out_shape=[jax.ShapeDtypeStruct((d, dmod.shape[1]), F32), jax.ShapeDtypeStruct((d, dfmod.shape[1]), F32),
                   jax.ShapeDtypeStruct((1, dmod_all.shape[1]), F32)],
        compiler_params=_params(),
    )(cond_all, dmod, dfmod, dmod_all)


def adamw(name, parts, w, m, v):
    n, R, C = parts.shape
    tr = R
    while n * tr * C * 4 > 4 * 1024 * 1024 and tr % 16 == 0:
        tr //= 2

    def kern(p_ref, w_ref, m_ref, v_ref, g_o, d_o, m_o, v_o):
        g = p_ref[0]
        for i in range(1, n):
            g = g + p_ref[i]
        m_new = ADAM_B1 * m_ref[...] + (1.0 - ADAM_B1) * g
        v_new = ADAM_B2 * v_ref[...] + (1.0 - ADAM_B2) * (g * g)
        m_hat = m_new / (1.0 - ADAM_B1 ** ADAM_STEP)
        v_hat = v_new / (1.0 - ADAM_B2 ** ADAM_STEP)
        g_o[...] = g
        d_o[...] = -ADAM_LR * (m_hat / (jnp.sqrt(v_hat) + ADAM_EPS) + ADAM_WD * w_ref[...])
        m_o[...] = m_new
        v_o[...] = v_new

    blk = pl.BlockSpec((tr, C), lambda i: (i, 0))
    shp = jax.ShapeDtypeStruct((R, C), F32)
    return pl.pallas_call(
        kern, name=name, grid=(R // tr,),
        in_specs=[pl.BlockSpec((n, tr, C), lambda i: (0, i, 0)), blk, blk, blk],
        out_specs=[blk] * 4, out_shape=[shp] * 4,
        compiler_params=_params(("parallel",)),
    )(parts, w, m, v)


def _cols_from_gathered(g):
    return jnp.transpose(g, (1, 0, 2)).reshape(g.shape[1], N_DEV * g.shape[2])


def _cols_to_shards(w):
    k, n8 = w.shape
    return jnp.transpose(w.reshape(k, N_DEV, n8 // N_DEV), (1, 0, 2))


def _pad_w_in(w):
    z = functools.partial(jnp.zeros, dtype=w.dtype)
    k = w.shape[0]
    cut = D_SSM + Q_LORA + KV_LORA
    return jnp.concatenate([w[:, :cut], z((k, 64)), w[:, cut:], z((k, 32))], axis=1)


def _unpad_w_in(w):
    cut = D_SSM + Q_LORA + KV_LORA
    return jnp.concatenate([w[:, :cut], w[:, cut + 64:cut + 96]], axis=1)


def _pad_w_uq(w):
    k = w.shape[0]
    w3 = w.reshape(k, N_HEADS, QK_NOPE + QK_ROPE)
    return jnp.pad(w3, ((0, 0), (0, 0), (0, HEAD_PAD - QK_NOPE - QK_ROPE))).reshape(k, QK_W)


def _unpad_w_uq(w):
    k = w.shape[0]
    return w.reshape(k, N_HEADS, HEAD_PAD)[:, :, :QK_NOPE + QK_ROPE].reshape(k, N_HEADS * (QK_NOPE + QK_ROPE))


def _pad_w_ukv(w):
    k = w.shape[0]
    w3 = w.reshape(k, N_HEADS, QK_NOPE + V_HEAD)
    kpart = jnp.pad(w3[:, :, :QK_NOPE], ((0, 0), (0, 0), (0, HEAD_PAD - QK_NOPE))).reshape(k, QK_W)
    vpart = w3[:, :, QK_NOPE:].reshape(k, V_W)
    return jnp.concatenate([kpart, vpart], axis=1)


def _unpad_w_ukv(w):
    k = w.shape[0]
    kpart = w[:, :QK_W].reshape(k, N_HEADS, HEAD_PAD)[:, :, :QK_NOPE]
    vpart = w[:, QK_W:].reshape(k, N_HEADS, V_HEAD)
    return jnp.concatenate([kpart, vpart], axis=2).reshape(k, N_HEADS * (QK_NOPE + V_HEAD))


def _block_diag(blocks):
    g, r, c = blocks.shape
    b4 = blocks.reshape(N_GB, GB, r, c)
    keep = jnp.eye(GB, dtype=bool)[None, :, None, :, None]
    return jnp.where(keep, b4[:, :, :, None, :], 0).reshape(g * r, GB * c)


def _block_diag_take(stacked, r, c):
    d5 = stacked.reshape(N_GB, GB, r, GB, c)
    idx = jnp.arange(GB)
    return jnp.transpose(d5[:, idx, :, idx, :], (1, 0, 2, 3)).reshape(N_GROUPS, r, c)


SMALL = ["norm1_g", "ssm_lambda_re", "ssm_lambda_im", "ssm_b_re", "ssm_b_im", "ssm_c_re", "ssm_c_im",
         "ssm_d", "ssm_log_dt", "q_norm_g", "kv_norm_g", "ssm_out_g", "attn_out_g", "norm2_g", "final_norm_g"]
BIASES = ["ada_b", "final_ada_b"]
SMALL_COLS = 1024


def _pack_small(values):
    flat = jnp.concatenate([v.reshape(-1) for v in values])
    rows = -(-flat.shape[0] // (8 * SMALL_COLS)) * 8
    return jnp.pad(flat, (0, rows * SMALL_COLS - flat.shape[0])).reshape(rows, SMALL_COLS)


def _unpack_small(packed, like):
    flat = packed.reshape(-1)
    out, off = [], 0
    for ref in like:
        n = int(np.prod(ref.shape))
        out.append(flat[off:off + n].reshape(ref.shape))
        off += n
    return out


def kernel(x, c, positions, ada_w, ada_b, norm1_g, w_in, ssm_lambda_re, ssm_lambda_im, ssm_b_re, ssm_b_im, ssm_c_re, ssm_c_im, ssm_d, ssm_log_dt, w_glu, q_norm_g, w_uq, kv_norm_g, w_ukv, ssm_out_g, attn_out_g, w_out, norm2_g, w_ff1, w_ff2, final_ada_w, final_ada_b, final_norm_g, loss_target, m_ada_w, m_ada_b, m_norm1_g, m_w_in, m_ssm_lambda_re, m_ssm_lambda_im, m_ssm_b_re, m_ssm_b_im, m_ssm_c_re, m_ssm_c_im, m_ssm_d, m_ssm_log_dt, m_w_glu, m_q_norm_g, m_w_uq, m_kv_norm_g, m_w_ukv, m_ssm_out_g, m_attn_out_g, m_w_out, m_norm2_g, m_w_ff1, m_w_ff2, m_final_ada_w, m_final_ada_b, m_final_norm_g, v_ada_w, v_ada_b, v_norm1_g, v_w_in, v_ssm_lambda_re, v_ssm_lambda_im, v_ssm_b_re, v_ssm_b_im, v_ssm_c_re, v_ssm_c_im, v_ssm_d, v_ssm_log_dt, v_w_glu, v_q_norm_g, v_w_uq, v_kv_norm_g, v_w_ukv, v_ssm_out_g, v_attn_out_g, v_w_out, v_norm2_g, v_w_ff1, v_w_ff2, v_final_ada_w, v_final_ada_b, v_final_norm_g):
    given = dict(locals())
    B, S, D = x.shape
    T = B * S
    tm = min(256, S)
    me = 4 * lax.axis_index("x") + 2 * lax.axis_index("y") + lax.axis_index("c")

    big = ["w_in", "w_glu", "w_uq", "w_ukv", "w_out", "w_ff1", "w_ff2"]
    shards = {n: given[n][0] for n in big}
    gathered = all_gather("gather_weights", [c] + [shards[n].astype(BF16) for n in big])
    c_all = gathered[0].reshape(N_DEV * B, D)
    gw = dict(zip(big, gathered[1:]))
    w_in_p = _pad_w_in(_cols_from_gathered(gw["w_in"]))
    w_glu_f = _cols_from_gathered(gw["w_glu"])
    w_uq_p = _pad_w_uq(_cols_from_gathered(gw["w_uq"]))
    w_ukv_p = _pad_w_ukv(_cols_from_gathered(gw["w_ukv"]))
    w_out_f = gw["w_out"].reshape(-1, D)
    w_ff1_f = _cols_from_gathered(gw["w_ff1"])
    w_ff2_f = gw["w_ff2"].reshape(-1, D)

    n_mod = ada_w.shape[2]
    n_fmod = final_ada_w.shape[1]
    ada_b_sh = lax.dynamic_slice_in_dim(ada_b, me * n_mod, n_mod, axis=1)
    fada_b_sh = lax.dynamic_slice_in_dim(final_ada_b[None, :], me * n_fmod, n_fmod, axis=1)
    cond_all, mod_sh, fmod_sh = modulation_fwd(c_all, ada_w[0], ada_b_sh, final_ada_w, fada_b_sh)
    mod_g, fmod_g = all_gather("gather_modulation", [mod_sh, fmod_sh])
    mod_mine = lax.dynamic_slice_in_dim(_cols_from_gathered(mod_g), me * B, B, axis=0)
    fmod_mine = lax.dynamic_slice_in_dim(_cols_from_gathered(fmod_g), me * B, B, axis=0)
    shift1, scale1, gate1, shift2, scale2, gate2 = [mod_mine[:, None, i * D:(i + 1) * D] for i in range(6)]
    fshift, fscale = fmod_mine[:, None, :D], fmod_mine[:, None, D:]

    x2d = x.reshape(T, D)
    tgt2d = loss_target.reshape(T, D)
    pos = positions.astype(F32).reshape(T, 1)
    invf, m_lo, m_hi = _rope_consts()
    rope_mask = m_lo + m_hi

    def fwd_in(rows, bv, vecs, res):
        (xv,), (sc, sh), (g1,), (w,) = rows, bv, vecs, res
        r, xh = _rms_stats(xv)
        h = xh * g1 * (1.0 + sc) + sh
        return [_dot(h.astype(BF16), w[...])], [], []

    (proj,), _, _ = row_call("fwd_in", fwd_in, B, S, tm, [x2d], [scale1, shift1], [norm1_g], [w_in_p],
                             [(IN_PAD, F32)], [], [])

    def fwd_qkv(rows, bv, vecs, res):
        (pr, ps), (gq, gkv, fr, lo, hi), (wq, wkv) = rows, vecs, res
        cs, s_lo, s_hi = _rope_tables(ps, fr, lo, hi)
        _, qh = _rms_stats(pr[:, D_SSM:D_SSM + Q_LORA])
        _, kvh = _rms_stats(pr[:, D_SSM + Q_LORA:D_SSM + Q_LORA + KV_LORA])
        qf = _dot((qh * gq).astype(BF16), wq[...])
        kvf = _dot((kvh * gkv).astype(BF16), wkv[...])
        k_rope = _rope_tile(pr[:, IN_PAD - HEAD_PAD:], cs, s_lo, s_hi, 1.0)
        q_t, k_t = [], []
        for h in range(N_HEADS):
            hs = slice(h * HEAD_PAD, (h + 1) * HEAD_PAD)
            q_t.append(_rope_tile(qf[:, hs], cs, s_lo, s_hi, 1.0))
            k_t.append(kvf[:, hs] + k_rope)
        kf, vf = jnp.concatenate(k_t, axis=1), kvf[:, QK_W:]
        return [jnp.concatenate(q_t, axis=1), kf, vf, kf.T, vf.T], [], []

    ns = S // tm
    col_tile = lambda b, s: (0, b * ns + s)
    (q_r, k_r, v_r, k_t, v_t), _, _ = row_call(
        "fwd_qkv", fwd_qkv, B, S, tm, [proj, pos], [], [q_norm_g, kv_norm_g, invf, m_lo, m_hi],
        [w_uq_p, w_ukv_p],
        [(QK_W, BF16), (QK_W, BF16), (V_W, BF16), ((QK_W, T), BF16, (QK_W, tm), col_tile),
         ((V_W, T), BF16, (V_W, tm), col_tile)], [], [])
    y_attn, lse = attention_fwd(q_r, k_r, v_t, B, S, tm)

    rep = lambda a: jnp.repeat(a, GROUP, axis=0)
    lam_rep = [rep(ssm_lambda_re[0]), rep(ssm_lambda_im[0]),
               rep(jnp.broadcast_to(ssm_log_dt[0][:, None], (N_GROUPS, N_STATE)))]
    bt = lambda a: jnp.transpose(a, (0, 2, 1)).reshape(D_SSM, N_STATE)
    reps = lam_rep + [bt(ssm_b_re[0]), bt(ssm_b_im[0])]
    a_re_rep, a_im_rep, bb_re, bb_im = ssm_params_fwd(reps)
    a_re = a_re_rep.reshape(N_GROUPS, GROUP, N_STATE)[:, 0, :].reshape(1, N_ST)
    a_im = a_im_rep.reshape(N_GROUPS, GROUP, N_STATE)[:, 0, :].reshape(1, N_ST)
    bbd_re = _block_diag(bb_re.reshape(N_GROUPS, GROUP, N_STATE)).astype(BF16)
    bbd_im = _block_diag(bb_im.reshape(N_GROUPS, GROUP, N_STATE)).astype(BF16)
    cbd_re = _block_diag(jnp.transpose(ssm_c_re[0], (0, 2, 1))).astype(BF16)
    cbd_im = _block_diag(jnp.transpose(ssm_c_im[0], (0, 2, 1))).astype(BF16)
    d_vec = ssm_d.reshape(1, D_SSM)
    st_re, st_im, y_pre = ssm_fwd(proj, bbd_re, bbd_im, cbd_re, cbd_im, d_vec, a_re, a_im, B, S, tm)

    def fwd_mix(rows, bv, vecs, res):
        (yp, ya, xv), (g1v,), (gso, gao), (wg, wo) = rows, bv, vecs, res
        z = _dot(_gelu(yp).astype(BF16), wg[...])
        y_ssm = z[:, :D_SSM] * _sigmoid(z[:, D_SSM:])
        _, sh = _rms_stats(y_ssm)
        _, ah = _rms_stats(ya)
        o = _dot((sh * gso).astype(BF16), wo[0:D_SSM, :]) + _dot((ah * gao).astype(BF16), wo[D_SSM:, :])
        return [z, o, xv + g1v * o], [], []

    (z, o, x2), _, _ = row_call("fwd_mix", fwd_mix, B, S, tm, [y_pre, y_attn, x2d], [gate1],
                                [ssm_out_g, attn_out_g], [w_glu_f, w_out_f],
                                [(2 * D_SSM, F32), (D, F32), (D, F32)], [], [])

    def final_out(x3, gf, fsc, fsh):
        r3, xh3 = _rms_stats(x3)
        n3 = xh3 * gf
        return r3, xh3, n3, n3 * (1.0 + fsc) + fsh

    def mlp_hidden(xv, g2, sc, sh, w1):
        r2, xh2 = _rms_stats(xv)
        n2 = xh2 * g2
        h2 = n2 * (1.0 + sc) + sh
        return r2, xh2, n2, h2, _dot(h2.astype(BF16), w1[...])

    def fwd_mlp(rows, bv, vecs, res):
        (xv, tg), (sc, sh, g2v, fsc, fsh), (g2, gf), (w1, w2) = rows, bv, vecs, res
        _, _, _, _, a = mlp_hidden(xv, g2, sc, sh, w1)
        ra = jnp.maximum(a, 0.0)
        ff = _dot((ra * ra).astype(BF16), w2[...])
        x3 = xv + g2v * ff
        _, _, _, out = final_out(x3, gf, fsc, fsh)
        err = out - tg
        loss = 0.5 * jnp.sum(jnp.mean(err * err, axis=-1, keepdims=True), axis=0, keepdims=True)
        return [ff, x3], [], [jnp.broadcast_to(loss, (1, 128))]

    fnorm_g = final_norm_g[None, :]
    (ff, x3), _, (loss_acc,) = row_call(
        "fwd_mlp", fwd_mlp, B, S, tm, [x2, tgt2d], [scale2, shift2, gate2, fscale, fshift], [norm2_g, fnorm_g],
        [w_ff1_f, w_ff2_f], [(D, F32), (D, F32)], [], [(1, 128)])
    loss = lax.psum(loss_acc[0, 0], ("x", "y", "c"))

    def bwd_mlp(rows, bv, vecs, res):
        (x3v, tg, x2v, ffv), (sc, sh, g2v, fsc, fsh), (g2, gf), (w1, w2) = rows, bv, vecs, res
        r3, xh3, n3, out = final_out(x3v, gf, fsc, fsh)
        dout = (out - tg) * (1.0 / D)
        dn3 = dout * (1.0 + fsc)
        dx3 = _rms_bwd(dn3 * gf, xh3, r3)
        dff = dx3 * g2v
        r2, xh2, n2, h2, a = mlp_hidden(x2v, g2, sc, sh, w1)
        ra = jnp.maximum(a, 0.0)
        dffb = dff.astype(BF16)
        da = _dot_nt(dffb, w2[...]) * (2.0 * ra)
        dab = da.astype(BF16)
        dh2 = _dot_nt(dab, w1[...])
        dn2 = dh2 * (1.0 + sc)
        dx2 = dx3 + _rms_bwd(dn2 * g2, xh2, r2)
        return ([dx2, h2, ra * ra, dab, dffb],
                [_colsum(dout), _colsum(dout * n3), _colsum(dx3 * ffv), _colsum(dh2), _colsum(dh2 * n2)],
                [_colsum(dn3 * xh3), _colsum(dn2 * xh2)])

    ((dx2, h2b, fb, dab, dffb), (d_fshift, d_fscale, d_gate2, d_shift2, d_scale2), (d_gf, d_g2)) = row_call(
        "bwd_mlp", bwd_mlp, B, S, tm, [x3, tgt2d, x2, ff], [scale2, shift2, gate2, fscale, fshift],
        [norm2_g, fnorm_g], [w_ff1_f, w_ff2_f],
        [(D, F32), (D, BF16), (4 * D, BF16), (4 * D, BF16), (D, BF16)], [D] * 5, [(1, D), (1, D)])
    g_w_ff1 = matmul_tn("grad_w_ff1", h2b, dab)
    g_w_ff2 = matmul_tn("grad_w_ff2", fb, dffb)

    def bwd_mix(rows, bv, vecs, res):
        (dxv, ov, zv, ya, yp), (g1v,), (gso, gao), (wo, wg) = rows, bv, vecs, res
        do = (dxv * g1v).astype(BF16)
        dyn = _dot_nt(do, wo[...])
        z1, sg = zv[:, :D_SSM], _sigmoid(zv[:, D_SSM:])
        y_ssm = z1 * sg
        rs, sh = _rms_stats(y_ssm)
        ra, ah = _rms_stats(ya)
        dyn_s, dyn_a = dyn[:, :D_SSM], dyn[:, D_SSM:]
        dy_ssm = _rms_bwd(dyn_s * gso, sh, rs)
        dy_attn = _rms_bwd(dyn_a * gao, ah, ra)
        dz = jnp.concatenate([dy_ssm * sg, dy_ssm * z1 * sg * (1.0 - sg)], axis=1).astype(BF16)
        dy_pre = _dot_nt(dz, wg[...]) * _gelu_grad(yp)
        yn = jnp.concatenate([sh * gso, ah * gao], axis=1)
        delta = jnp.sum((dy_attn * ya).T.reshape(N_HEADS, V_HEAD, tm), axis=1, keepdims=True)
        return ([do, yn, dz, _gelu(yp), dy_attn, dy_pre, delta], [_colsum(dxv * ov)],
                [_colsum(dyn_s * sh), _colsum(dyn_a * ah)])

    ((dob, ynb, dzb, ygb, dy_attn, dy_pre, delta), (d_gate1,), (d_gso, d_gao)) = row_call(
        "bwd_mix", bwd_mix, B, S, tm, [dx2, o, z, y_attn, y_pre], [gate1], [ssm_out_g, attn_out_g],
        [w_out_f, w_glu_f],
        [(D, BF16), (D, BF16), (2 * D_SSM, BF16), (D_SSM, BF16), (V_W, F32), (D_SSM, F32),
         ((N_HEADS, 1, T), F32, (N_HEADS, 1, tm), lambda b, s: (0, 0, b * ns + s))],
        [D], [(1, D_SSM), (1, V_W)])
    g_w_out = matmul_tn("grad_w_out", ynb, dob)
    g_w_glu = matmul_tn("grad_w_glu", ygb, dzb)

    dq_t, dk_r, dv_r = attention_bwd(q_r, k_r, k_t, v_r, dy_attn, lse, delta, B, S, tm)
    dq_t = dq_t.reshape(B, QK_W, S)

    def bwd_qkv(rows, bv, vecs, res):
        (dqt, dkv, dvv, pr, ps), (gq, gkv, fr, lo, hi, rmask), (wq, wkv) = rows, vecs, res
        dqv = dqt[0].T
        cs, s_lo, s_hi = _rope_tables(ps, fr, lo, hi)
        dq_t = []
        dk_rope = jnp.zeros((dkv.shape[0], HEAD_PAD), F32)
        for h in range(N_HEADS):
            hs = slice(h * HEAD_PAD, (h + 1) * HEAD_PAD)
            dq_t.append(_rope_tile(dqv[:, hs], cs, s_lo, s_hi, -1.0))
            dk_rope = dk_rope + dkv[:, hs]
        dk_rope = _rope_tile(dk_rope * rmask, cs, s_lo, s_hi, -1.0)
        dqb = jnp.concatenate(dq_t, axis=1).astype(BF16)
        dkvb = jnp.concatenate([dkv, dvv], axis=1).astype(BF16)
        rq, qh = _rms_stats(pr[:, D_SSM:D_SSM + Q_LORA])
        rk, kvh = _rms_stats(pr[:, D_SSM + Q_LORA:D_SSM + Q_LORA + KV_LORA])
        dqn = _dot_nt(dqb, wq[...])
        dkvn = _dot_nt(dkvb, wkv[...])
        dpa = jnp.concatenate([_rms_bwd(dqn * gq, qh, rq), _rms_bwd(dkvn * gkv, kvh, rk), dk_rope], axis=1)
        return [dpa, qh * gq, kvh * gkv, dqb, dkvb], [], [_colsum(dqn * qh), _colsum(dkvn * kvh)]

    ((dpa, qnb, kvnb, dqb, dkvb), _, (d_gq, d_gkv)) = row_call(
        "bwd_qkv", bwd_qkv, B, S, tm,
        [(dq_t, (1, QK_W, tm), lambda b, s: (b, 0, s)), dk_r, dv_r, proj, pos], [],
        [q_norm_g, kv_norm_g, invf, m_lo, m_hi, rope_mask], [w_uq_p, w_ukv_p],
        [(Q_LORA + KV_LORA + HEAD_PAD, F32), (Q_LORA, BF16), (KV_LORA, BF16), (QK_W, BF16), (QK_W + V_W, BF16)],
        [], [(1, Q_LORA), (1, KV_LORA)])
    g_w_uq = _unpad_w_uq(matmul_tn("grad_w_uq", qnb, dqb))
    g_w_ukv = _unpad_w_ukv(matmul_tn("grad_w_ukv", kvnb, dkvb))

    du, dc_re_d, dc_im_d, db_re_d, db_im_d, da_re, da_im, d_d = ssm_bwd(
        dy_pre, proj, st_re, st_im, bbd_re, bbd_im, cbd_re, cbd_im, d_vec, a_re, a_im, B, S, tm)
    place = lambda a: jnp.zeros((N_GROUPS, GROUP, N_STATE), F32).at[:, 0, :].set(
        a.reshape(N_GROUPS, N_STATE)).reshape(D_SSM, N_STATE)
    cots = [place(da_re), place(da_im),
            _block_diag_take(db_re_d, GROUP, N_STATE).reshape(D_SSM, N_STATE),
            _block_diag_take(db_im_d, GROUP, N_STATE).reshape(D_SSM, N_STATE)]
    g_lam_re, g_lam_im, g_log_dt, g_bt_re, g_bt_im = ssm_params_bwd(reps, cots)
    unbt = lambda a: jnp.transpose(a.reshape(N_GROUPS, GROUP, N_STATE), (0, 2, 1))
    g_c_re = jnp.transpose(_block_diag_take(dc_re_d, N_STATE, GROUP), (0, 2, 1))
    g_c_im = jnp.transpose(_block_diag_take(dc_im_d, N_STATE, GROUP), (0, 2, 1))

    def bwd_in(rows, bv, vecs, res):
        (duv, dpav, xv, dx2v), (sc, sh), (g1,), (w,) = rows, bv, vecs, res
        dproj = jnp.concatenate([duv, dpav], axis=1).astype(BF16)
        dh = _dot_nt(dproj, w[...])
        r, xh = _rms_stats(xv)
        n1 = xh * g1
        dn1 = dh * (1.0 + sc)
        dx = dx2v + _rms_bwd(dn1 * g1, xh, r)
        return [dx, n1 * (1.0 + sc) + sh, dproj], [_colsum(dh), _colsum(dh * n1)], [_colsum(dn1 * xh)]

    ((grad_x, h1b, dprojb), (d_shift1, d_scale1), (d_g1,)) = row_call(
        "bwd_in", bwd_in, B, S, tm, [du, dpa, x2d, dx2], [scale1, shift1], [norm1_g], [w_in_p],
        [(D, F32), (D, BF16), (IN_PAD, BF16)], [D, D], [(1, D)])
    g_w_in = _unpad_w_in(matmul_tn("grad_w_in", h1b, dprojb))

    dmod = jnp.concatenate([d_shift1, d_scale1, d_gate1, d_shift2, d_scale2, d_gate2, d_fshift, d_fscale],
                           axis=2).reshape(B, 8 * D)
    small_part = {
        "norm1_g": d_g1, "ssm_lambda_re": g_lam_re,
        "ssm_lambda_im": g_lam_im, "ssm_b_re": unbt(g_bt_re), "ssm_b_im": unbt(g_bt_im), "ssm_c_re": g_c_re,
        "ssm_c_im": g_c_im, "ssm_d": d_d, "ssm_log_dt": g_log_dt, "q_norm_g": d_gq, "kv_norm_g": d_gkv,
        "ssm_out_g": d_gso, "attn_out_g": d_gao, "norm2_g": d_g2, "final_norm_g": d_gf}
    small_packed = _pack_small([small_part[n] for n in SMALL])
    dmod_g, small_g = all_gather("gather_small_grads", [dmod, small_packed])
    dmod_all = dmod_g.reshape(N_DEV * B, 8 * D)
    dm_sh = lax.dynamic_slice_in_dim(dmod_all[:, :6 * D], me * n_mod, n_mod, axis=1)
    dfm_sh = lax.dynamic_slice_in_dim(dmod_all[:, 6 * D:], me * n_fmod, n_fmod, axis=1)
    g_ada_w, g_fada_w, g_biases = modulation_bwd(cond_all, dm_sh, dfm_sh, dmod_all)

    exchanged = all_to_all(
        "exchange_weight_grads",
        [_cols_to_shards(g_w_in), g_w_glu, _cols_to_shards(g_w_uq), g_w_ukv, g_w_out.reshape(N_DEV, -1, D),
         g_w_ff1, g_w_ff2.reshape(N_DEV, -1, D)],
        [None, w_glu.shape[2], None, w_ukv.shape[2], None, w_ff1.shape[2], None])
    parts = dict(zip(big, exchanged))
    parts["ada_w"] = g_ada_w[None]
    parts["final_ada_w"] = g_fada_w[None]

    res_g, res_d, res_m, res_v = {}, {}, {}, {}
    for n in ["ada_w"] + big + ["final_ada_w"]:
        w_full = given[n]
        w2 = w_full.reshape(w_full.shape[-2], w_full.shape[-1])
        out = adamw("adamw_" + n, parts[n], w2, given["m_" + n].reshape(w2.shape), given["v_" + n].reshape(w2.shape))
        res_g[n], res_d[n], res_m[n], res_v[n] = [a.reshape(w_full.shape) for a in out]
    small_w = [given[n] for n in SMALL]
    out = adamw("adamw_small", small_g, _pack_small(small_w), _pack_small([given["m_" + n] for n in SMALL]),
                _pack_small([given["v_" + n] for n in SMALL]))
    for res, packed in zip((res_g, res_d, res_m, res_v), out):
        for n, val in zip(SMALL, _unpack_small(packed, small_w)):
            res[n] = val
    bias_w = [given[n] for n in BIASES]
    out = adamw("adamw_biases", g_biases.reshape(1, 8, D), _pack_small(bias_w),
                _pack_small([given["m_" + n] for n in BIASES]), _pack_small([given["v_" + n] for n in BIASES]))
    for res, packed in zip((res_g, res_d, res_m, res_v), out):
        for n, val in zip(BIASES, _unpack_small(packed, bias_w)):
            res[n] = val

    order = ["ada_w", "ada_b", "norm1_g", "w_in", "ssm_lambda_re", "ssm_lambda_im", "ssm_b_re", "ssm_b_im",
             "ssm_c_re", "ssm_c_im", "ssm_d", "ssm_log_dt", "w_glu", "q_norm_g", "w_uq", "kv_norm_g", "w_ukv",
             "ssm_out_g", "attn_out_g", "w_out", "norm2_g", "w_ff1", "w_ff2", "final_ada_w", "final_ada_b",
             "final_norm_g"]
    return (loss, grad_x.reshape(B, S, D), *[res_g[n] for n in order], *[res_d[n] for n in order],
            *[res_m[n] for n in order], *[res_v[n] for n in order])
```

```python
import functools
import math

import numpy as np
import jax
import jax.numpy as jnp
from jax import lax
from jax.experimental import pallas as pl
from jax.experimental.pallas import tpu as pltpu

F32 = jnp.float32
BF16 = jnp.bfloat16

N_DEV = 8
EPS = 1e-6
D_SSM = 512
N_GROUPS = 32
GROUP = 16
N_STATE = 64
N_ST = N_GROUPS * N_STATE
Q_LORA = 384
KV_LORA = 256
QK_NOPE = 64
QK_ROPE = 32
V_HEAD = 64
N_HEADS = 8
HEAD_PAD = 128
QK_W = N_HEADS * HEAD_PAD
V_W = N_HEADS * V_HEAD
IN_COLS = D_SSM + Q_LORA + KV_LORA + QK_ROPE
IN_PAD = D_SSM + Q_LORA + KV_LORA + HEAD_PAD
ROPE_BASE = 10000.0
ATTN_SCALE = (QK_NOPE + QK_ROPE) ** -0.5
NEG = -1e30

ADAM_LR = 0.001
ADAM_B1 = 0.9
ADAM_B2 = 0.999
ADAM_EPS = 1e-08
ADAM_WD = 0.01
ADAM_STEP = 10

VMEM_LIMIT = 56 * 1024 * 1024
MESH_ID = pl.DeviceIdType.MESH


def _dot(a, b):
    return jnp.dot(a, b, preferred_element_type=F32)


def _dot_nt(a, b):
    return lax.dot_general(a, b, (((1,), (1,)), ((), ())), preferred_element_type=F32)


def _dot_tn(a, b):
    return lax.dot_general(a, b, (((0,), (0,)), ((), ())), preferred_element_type=F32)


def _rms_stats(x):
    r = lax.rsqrt(jnp.mean(x * x, axis=-1, keepdims=True) + EPS)
    return r, x * r


def _rms_bwd(dy, xh, r):
    return r * (dy - xh * jnp.mean(dy * xh, axis=-1, keepdims=True))


def _sigmoid(x):
    return 1.0 / (1.0 + jnp.exp(-x))


_GELU_C = math.sqrt(2.0 / math.pi)


def _gelu(x):
    t = jnp.tanh(_GELU_C * (x + 0.044715 * x * x * x))
    return 0.5 * x * (1.0 + t)


def _gelu_grad(x):
    t = jnp.tanh(_GELU_C * (x + 0.044715 * x * x * x))
    return 0.5 * (1.0 + t) + 0.5 * x * (1.0 - t * t) * _GELU_C * (1.0 + 3.0 * 0.044715 * x * x)


def _colsum(a):
    return jnp.sum(a, axis=0, keepdims=True)


def _params(sem=None):
    return pltpu.CompilerParams(dimension_semantics=sem, vmem_limit_bytes=VMEM_LIMIT)


def _mesh_pos():
    x, y, c = lax.axis_index("x"), lax.axis_index("y"), lax.axis_index("c")
    return x, y, c, 4 * x + 2 * y + c


def _peer(x, y, c, k):
    px = 1 - x if k & 4 else x
    py = 1 - y if k & 2 else y
    pc = 1 - c if k & 1 else c
    return (px, py, pc), 4 * px + 2 * py + pc


GATHER = "gather"
ANY_SPEC = pl.BlockSpec(memory_space=pl.ANY)


class Exchange:
    def __init__(self, arrays, kinds):
        self.arrays, self.kinds, self.n = list(arrays), list(kinds), len(arrays)

    def _shard(self, i):
        a, kind = self.arrays[i], self.kinds[i]
        if kind == GATHER:
            return a.shape
        return a.shape[1:] if kind is None else (a.shape[0], kind)

    def out_shapes(self):
        return [jax.ShapeDtypeStruct((N_DEV,) + tuple(self._shard(i)), self.arrays[i].dtype) for i in range(self.n)]

    def scratch(self):
        return [pltpu.SemaphoreType.DMA((self.n, N_DEV - 1)), pltpu.SemaphoreType.DMA((self.n, N_DEV - 1)),
                pltpu.SemaphoreType.DMA((self.n,))]

    def _src(self, ins, i, j):
        kind = self.kinds[i]
        if kind == GATHER:
            return ins[i]
        if kind is None:
            return ins[i].at[j]
        return ins[i].at[:, pl.ds(pl.multiple_of(j * kind, 128), kind)]

    def _copies(self, ins, outs, sems):
        send_sems, recv_sems, loc_sems = sems
        x, y, c, me = _mesh_pos()
        local, sent, received = [], [], []
        for i in range(self.n):
            local.append(pltpu.make_async_copy(self._src(ins, i, me), outs[i].at[me], loc_sems.at[i]))
            for k in range(1, N_DEV):
                peer, pidx = _peer(x, y, c, k)
                pair = dict(send_sem=send_sems.at[i, k - 1], recv_sem=recv_sems.at[i, k - 1], device_id=peer,
                            device_id_type=MESH_ID)
                sent.append(pltpu.make_async_remote_copy(
                    src_ref=self._src(ins, i, pidx), dst_ref=outs[i].at[me], **pair))
                received.append(pltpu.make_async_remote_copy(
                    src_ref=self._src(ins, i, pidx), dst_ref=outs[i].at[pidx], **pair))
        return local, sent, received

    def start(self, ins, outs, sems):
        local, sent, _ = self._copies(ins, outs, sems)
        for cp in local + sent:
            cp.start()

    def wait(self, ins, outs, sems):
        local, sent, received = self._copies(ins, outs, sems)
        for cp in received:
            cp.wait_recv()
        for cp in sent:
            cp.wait_send()
        for cp in local:
            cp.wait()


def exchange(name, arrays, kinds):
    ex = Exchange(arrays, kinds)
    n = ex.n

    def body(*refs):
        ins, outs, sems = refs[:n], refs[n:2 * n], refs[2 * n:]
        ex.start(ins, outs, sems)
        ex.wait(ins, outs, sems)

    return pl.pallas_call(
        body, name=name, out_shape=ex.out_shapes(), in_specs=[ANY_SPEC] * n, out_specs=[ANY_SPEC] * n,
        scratch_shapes=ex.scratch())(*arrays)


def all_gather(name, arrays):
    return exchange(name, arrays, [GATHER] * len(arrays))


def row_call(name, body, B, S, tm, rows, bvecs, vecs, res, row_outs, bacc, gacc):
    ns = S // tm
    T = B * S
    n_r, n_b, n_v, n_w = len(rows), len(bvecs), len(vecs), len(res)
    n_ro, n_ba, n_ga = len(row_outs), len(bacc), len(gacc)

    def kern(*refs):
        i = 0
        r_refs = refs[i:i + n_r]; i += n_r
        b_refs = refs[i:i + n_b]; i += n_b
        v_refs = refs[i:i + n_v]; i += n_v
        w_hbm = refs[i:i + n_w]; i += n_w
        ro_refs = refs[i:i + n_ro]; i += n_ro
        ba_refs = refs[i:i + n_ba]; i += n_ba
        ga_refs = refs[i:i + n_ga]; i += n_ga
        w_vmem = refs[i:i + n_w]
        b = pl.program_id(0)
        s = pl.program_id(1)
        first = jnp.logical_and(b == 0, s == 0)

        if n_w:
            @pl.when(first)
            def _load_weights():
                for h, v in zip(w_hbm, w_vmem):
                    pltpu.sync_copy(h, v)

        ro, ba, ga = body([r[...] for r in r_refs], [r[0] for r in b_refs], [r[...] for r in v_refs],
                          list(w_vmem))
        for ref, val in zip(ro_refs, ro):
            ref[...] = val.astype(ref.dtype)

        def accumulate(ref, val, is_first, idx):
            @pl.when(is_first)
            def _set():
                ref[idx] = val

            @pl.when(jnp.logical_not(is_first))
            def _add():
                ref[idx] = ref[idx] + val

        for ref, val in zip(ba_refs, ba):
            accumulate(ref, val, s == 0, 0)
        for ref, val in zip(ga_refs, ga):
            accumulate(ref, val, first, Ellipsis)

    row_arrays = [r[0] if isinstance(r, tuple) else r for r in rows]
    row_in_specs = [pl.BlockSpec(r[1], r[2]) if isinstance(r, tuple)
                    else pl.BlockSpec((tm, r.shape[1]), lambda b, s: (b * ns + s, 0)) for r in rows]
    ro_shapes = [jax.ShapeDtypeStruct(tuple(o[0]), o[1]) if len(o) == 4 else jax.ShapeDtypeStruct((T, o[0]), o[1])
                 for o in row_outs]
    ro_specs = [pl.BlockSpec(o[2], o[3]) if len(o) == 4 else pl.BlockSpec((tm, o[0]), lambda b, s: (b * ns + s, 0))
                for o in row_outs]
    in_specs = (row_in_specs
                + [pl.BlockSpec((1, 1, v.shape[2]), lambda b, s: (b, 0, 0)) for v in bvecs]
                + [pl.BlockSpec(v.shape, lambda b, s, nd=v.ndim: (0,) * nd) for v in vecs]
                + [pl.BlockSpec(memory_space=pl.ANY) for _ in res])
    out_shape = (ro_shapes
                 + [jax.ShapeDtypeStruct((B, 1, w), F32) for w in bacc]
                 + [jax.ShapeDtypeStruct(tuple(sh), F32) for sh in gacc])
    out_specs = (ro_specs
                 + [pl.BlockSpec((1, 1, w), lambda b, s: (b, 0, 0)) for w in bacc]
                 + [pl.BlockSpec(tuple(sh), lambda b, s, nd=len(sh): (0,) * nd) for sh in gacc])
    outs = pl.pallas_call(
        kern, name=name, grid=(B, ns), in_specs=in_specs, out_specs=out_specs, out_shape=out_shape,
        scratch_shapes=[pltpu.VMEM(w.shape, w.dtype) for w in res],
        compiler_params=_params(("arbitrary", "arbitrary")),
    )(*row_arrays, *bvecs, *vecs, *res)
    return outs[:n_ro], outs[n_ro:n_ro + n_ba], outs[n_ro + n_ba:]


def matmul_tn(name, a, b):
    T, K = a.shape
    N = b.shape[1]
    tk = min(K, 512)
    tn = 512 if N % 512 == 0 else 256
    tt = min(T, 512)
    nt = T // tt

    def kern(a_ref, b_ref, o_ref):
        t = pl.program_id(2)
        part = _dot_tn(a_ref[...], b_ref[...])

        @pl.when(t == 0)
        def _set():
            o_ref[...] = part

        @pl.when(t > 0)
        def _add():
            o_ref[...] = o_ref[...] + part

    return pl.pallas_call(
        kern, name=name, grid=(K // tk, N // tn, nt),
        in_specs=[pl.BlockSpec((tt, tk), lambda i, j, t: (t, i)), pl.BlockSpec((tt, tn), lambda i, j, t: (t, j))],
        out_specs=pl.BlockSpec((tk, tn), lambda i, j, t: (i, j)),
        out_shape=jax.ShapeDtypeStruct((K, N), F32),
        compiler_params=_params(("parallel", "parallel", "arbitrary")),
    )(a, b)


def _rope_consts():
    inv_freq = ROPE_BASE ** (-jnp.arange(0, QK_ROPE, 2, dtype=F32) / QK_ROPE)
    zeros64 = jnp.zeros((64,), F32)
    zeros32 = jnp.zeros((32,), F32)
    zeros16 = jnp.zeros((16,), F32)
    ones16 = jnp.ones((16,), F32)
    invf = jnp.concatenate([zeros64, inv_freq, inv_freq, zeros32])[None, :]
    m_lo = jnp.concatenate([zeros64, ones16, zeros16, zeros32])[None, :]
    m_hi = jnp.concatenate([zeros64, zeros16, ones16, zeros32])[None, :]
    return invf, m_lo, m_hi


def _rope_tables(pos, invf, m_lo, m_hi):
    ang = pos * invf
    return jnp.cos(ang), jnp.sin(ang) * m_lo, jnp.sin(ang) * m_hi


def _rope_tile(t, cs, s_lo, s_hi, sign):
    up = pltpu.roll(t, HEAD_PAD - 16, axis=1)
    down = pltpu.roll(t, 16, axis=1)
    return t * cs + sign * (down * s_hi - up * s_lo)


def _heads(ref, width):
    return jnp.stack([ref[:, h * width:(h + 1) * width] for h in range(N_HEADS)], axis=0)


def _bmm(a, b, ca, cb):
    return lax.dot_general(a, b, (((ca,), (cb,)), ((0,), (0,))), preferred_element_type=F32)


def _scores_t(k_ref, q_ref, masked, tq):
    st = _bmm(_heads(k_ref, HEAD_PAD), _heads(q_ref, HEAD_PAD), 2, 2) * ATTN_SCALE
    if masked:
        key = lax.broadcasted_iota(jnp.int32, (tq, tq), 0)
        qry = lax.broadcasted_iota(jnp.int32, (tq, tq), 1)
        st = jnp.where((key <= qry)[None], st, NEG)
    return st


def attention_fwd(q, k, vt, B, S, tq, side):
    nq = S // tq
    ns = side.n

    def kern(*refs):
        q_ref, k_ref, vt_ref = refs[:3]
        side_in = refs[3:3 + ns]
        o_ref, lse_ref = refs[3 + ns:5 + ns]
        side_out = refs[5 + ns:5 + 2 * ns]
        m_scr, l_scr, acc_scr = refs[5 + 2 * ns:8 + 2 * ns]
        sems = refs[8 + 2 * ns:]
        b, qi, ki = pl.program_id(0), pl.program_id(1), pl.program_id(2)

        @pl.when(jnp.logical_and(b == 0, jnp.logical_and(qi == 0, ki == 0)))
        def _start_side():
            side.start(side_in, side_out, sems)

        @pl.when(ki == 0)
        def _init():
            m_scr[...] = jnp.full(m_scr.shape, NEG, F32)
            l_scr[...] = jnp.zeros(l_scr.shape, F32)
            acc_scr[...] = jnp.zeros(acc_scr.shape, F32)

        def block(masked):
            st = _scores_t(k_ref, q_ref, masked, tq)
            m_prev = m_scr[...]
            m_new = jnp.maximum(m_prev, jnp.max(st, axis=1, keepdims=True))
            alpha = jnp.exp(m_prev - m_new)
            p = jnp.exp(st - m_new)
            l_scr[...] = alpha * l_scr[...] + jnp.sum(p, axis=1, keepdims=True)
            vt3 = vt_ref[...].reshape(N_HEADS, V_HEAD, tq)
            acc_scr[...] = alpha * acc_scr[...] + _bmm(vt3, p.astype(BF16), 2, 1)
            m_scr[...] = m_new

        @pl.when(ki < qi)
        def _full():
            block(False)

        @pl.when(ki == qi)
        def _diagonal():
            block(True)
            ot = acc_scr[...] / l_scr[...]
            for h in range(N_HEADS):
                o_ref[:, h * V_HEAD:(h + 1) * V_HEAD] = ot[h].T
            lse_ref[...] = m_scr[...] + jnp.log(l_scr[...])

        @pl.when(jnp.logical_and(b == B - 1, jnp.logical_and(qi == nq - 1, ki == nq - 1)))
        def _wait_side():
            side.wait(side_in, side_out, sems)

    T = B * S
    return pl.pallas_call(
        kern, name="attention_fwd", grid=(B, nq, nq),
        in_specs=[pl.BlockSpec((tq, QK_W), lambda b, i, j: (b * nq + i, 0)),
                  pl.BlockSpec((tq, QK_W), lambda b, i, j: (b * nq + jnp.minimum(i, j), 0)),
                  pl.BlockSpec((V_W, tq), lambda b, i, j: (0, b * nq + jnp.minimum(i, j)))] + [ANY_SPEC] * ns,
        out_specs=[pl.BlockSpec((tq, V_W), lambda b, i, j: (b * nq + i, 0)),
                   pl.BlockSpec((N_HEADS, 1, tq), lambda b, i, j: (0, 0, b * nq + i))] + [ANY_SPEC] * ns,
        out_shape=[jax.ShapeDtypeStruct((T, V_W), F32), jax.ShapeDtypeStruct((N_HEADS, 1, T), F32)]
        + side.out_shapes(),
        scratch_shapes=[pltpu.VMEM((N_HEADS, 1, tq), F32), pltpu.VMEM((N_HEADS, 1, tq), F32),
                        pltpu.VMEM((N_HEADS, V_HEAD, tq), F32)] + side.scratch(),
        compiler_params=_params(("arbitrary", "arbitrary", "arbitrary")),
    )(q, k, vt, *side.arrays)


def attention_bwd(q, k, kt, v, do, lse, delta, B, S, tq, side):
    nq = S // tq
    ns = side.n

    def kern(*refs):
        q_ref, k_ref, kt_ref, v_ref, do_ref, lse_ref, delta_ref = refs[:7]
        side_in = refs[7:7 + ns]
        dq_hbm, dk_ref, dv_ref = refs[7 + ns:10 + ns]
        side_out = refs[10 + ns:10 + 2 * ns]
        dq_acc, dk_acc, dv_acc = refs[10 + 2 * ns:13 + 2 * ns]
        sems = refs[13 + 2 * ns:]
        b, kj, qi = pl.program_id(0), pl.program_id(1), pl.program_id(2)

        @pl.when(jnp.logical_and(b == 0, jnp.logical_and(kj == 0, qi == 0)))
        def _start_side():
            side.start(side_in, side_out, sems)

        @pl.when(jnp.logical_and(kj == 0, qi == 0))
        def _zero_dq():
            dq_acc[...] = jnp.zeros(dq_acc.shape, F32)

        @pl.when(qi == 0)
        def _zero_dkv():
            dk_acc[...] = jnp.zeros(dk_acc.shape, F32)
            dv_acc[...] = jnp.zeros(dv_acc.shape, F32)

        def block(masked):
            pt = jnp.exp(_scores_t(k_ref, q_ref, masked, tq) - lse_ref[...])
            do3 = _heads(do_ref, V_HEAD).astype(BF16)
            dv_acc[...] = dv_acc[...] + _bmm(pt.astype(BF16), do3, 2, 1)
            dpt = _bmm(_heads(v_ref, V_HEAD), do3, 2, 2)
            dst = (pt * (dpt - delta_ref[...]) * ATTN_SCALE).astype(BF16)
            dk_acc[...] = dk_acc[...] + _bmm(dst, _heads(q_ref, HEAD_PAD), 2, 1)
            q_cols = pl.ds(pl.multiple_of(qi * tq, tq), tq)
            kt3 = kt_ref[...].reshape(N_HEADS, HEAD_PAD, tq)
            dq_acc[:, :, q_cols] = dq_acc[:, :, q_cols] + _bmm(kt3, dst, 2, 1)

        @pl.when(qi > kj)
        def _full():
            block(False)

        @pl.when(qi == kj)
        def _diagonal():
            block(True)

        @pl.when(qi == nq - 1)
        def _write_dkv():
            for h in range(N_HEADS):
                dk_ref[:, h * HEAD_PAD:(h + 1) * HEAD_PAD] = dk_acc[h]
                dv_ref[:, h * V_HEAD:(h + 1) * V_HEAD] = dv_acc[h]

        @pl.when(jnp.logical_and(kj == nq - 1, qi == nq - 1))
        def _write_dq():
            pltpu.sync_copy(dq_acc, dq_hbm.at[b])

        @pl.when(jnp.logical_and(b == B - 1, jnp.logical_and(kj == nq - 1, qi == nq - 1)))
        def _wait_side():
            side.wait(side_in, side_out, sems)

    T = B * S
    qrow = lambda b, j, i: (b * nq + jnp.maximum(i, j), 0)
    qcol3 = lambda b, j, i: (0, 0, b * nq + jnp.maximum(i, j))
    krow = lambda b, j, i: (b * nq + j, 0)
    return pl.pallas_call(
        kern, name="attention_bwd", grid=(B, nq, nq),
        in_specs=[pl.BlockSpec((tq, QK_W), qrow), pl.BlockSpec((tq, QK_W), krow),
                  pl.BlockSpec((QK_W, tq), lambda b, j, i: (0, b * nq + j)), pl.BlockSpec((tq, V_W), krow),
                  pl.BlockSpec((tq, V_W), qrow), pl.BlockSpec((N_HEADS, 1, tq), qcol3),
                  pl.BlockSpec((N_HEADS, 1, tq), qcol3)] + [ANY_SPEC] * ns,
        out_specs=[ANY_SPEC, pl.BlockSpec((tq, QK_W), krow), pl.BlockSpec((tq, V_W), krow)] + [ANY_SPEC] * ns,
        out_shape=[jax.ShapeDtypeStruct((B, N_HEADS, HEAD_PAD, S), F32), jax.ShapeDtypeStruct((T, QK_W), F32),
                   jax.ShapeDtypeStruct((T, V_W), F32)] + side.out_shapes(),
        scratch_shapes=[pltpu.VMEM((N_HEADS, HEAD_PAD, S), F32), pltpu.VMEM((N_HEADS, tq, HEAD_PAD), F32),
                        pltpu.VMEM((N_HEADS, tq, V_HEAD), F32)] + side.scratch(),
        compiler_params=_params(("arbitrary", "arbitrary", "arbitrary")),
    )(q, k, kt, v, do, lse, delta, *side.arrays)


GB = 8
N_GB = N_GROUPS // GB
GB_U = GB * GROUP
GB_ST = GB * N_STATE


def _gb_u(j):
    return slice(j * GB_U, (j + 1) * GB_U)


def _gb_s(j):
    return slice(j * GB_ST, (j + 1) * GB_ST)


def ssm_param_fn(lr, li, ld, br, bi):
    dt = jnp.exp(ld)
    er = jnp.exp(lr * dt)
    ar = er * jnp.cos(li * dt)
    ai = er * jnp.sin(li * dt)
    nr = ar - 1.0
    den = lr * lr + li * li
    cr = (nr * lr + ai * li) / den
    ci = (ai * lr - nr * li) / den
    return ar, ai, cr * br - ci * bi, cr * bi + ci * br


def ssm_params_fwd(reps):
    def kern(lr, li, ld, br, bi, ar_o, ai_o, bbr_o, bbi_o):
        ar, ai, bbr, bbi = ssm_param_fn(lr[...], li[...], ld[...], br[...], bi[...])
        ar_o[...] = ar
        ai_o[...] = ai
        bbr_o[...] = bbr
        bbi_o[...] = bbi

    shp = jax.ShapeDtypeStruct(reps[0].shape, F32)
    return pl.pallas_call(kern, name="ssm_params_fwd", out_shape=[shp] * 4)(*reps)


def ssm_params_bwd(reps, cots):
    rows = reps[0].shape[0]

    def kern(lr, li, ld, br, bi, d_ar, d_ai, d_bbr, d_bbi, dlr_o, dli_o, dld_o, dbr_o, dbi_o):
        _, vjp = jax.vjp(ssm_param_fn, lr[...], li[...], ld[...], br[...], bi[...])
        dlr, dli, dld, dbr, dbi = vjp((d_ar[...], d_ai[...], d_bbr[...], d_bbi[...]))
        dlr_o[...] = jnp.sum(dlr.reshape(N_GROUPS, GROUP, N_STATE), axis=1)
        dli_o[...] = jnp.sum(dli.reshape(N_GROUPS, GROUP, N_STATE), axis=1)
        dld_o[...] = jnp.sum(jnp.sum(dld.reshape(N_GROUPS, GROUP, N_STATE), axis=1), axis=-1, keepdims=True)
        dbr_o[...] = dbr
        dbi_o[...] = dbi

    g = jax.ShapeDtypeStruct((N_GROUPS, N_STATE), F32)
    full = jax.ShapeDtypeStruct((rows, N_STATE), F32)
    return pl.pallas_call(
        kern, name="ssm_params_bwd",
        out_shape=[g, g, jax.ShapeDtypeStruct((N_GROUPS, 1), F32), full, full])(*reps, *cots)


def ssm_fwd(proj, b_re, b_im, c_re, c_im, d_vec, a_re, a_im, B, S, ts):
    ns = S // ts
    T = B * S

    def kern(u_ref, bre_ref, bim_ref, cre_ref, cim_ref, d_ref, are_ref, aim_ref,
             sre_ref, sim_ref, y_ref, car_re, car_im, bu_re, bu_im):
        s = pl.program_id(1)

        @pl.when(s == 0)
        def _reset():
            car_re[...] = jnp.zeros(car_re.shape, F32)
            car_im[...] = jnp.zeros(car_im.shape, F32)

        u = u_ref[...]
        ub = u.astype(BF16)
        for j in range(N_GB):
            uj, bj, sj = ub[:, _gb_u(j)], _gb_u(j), _gb_s(j)
            bu_re[:, sj] = _dot(uj, bre_ref[bj, :])
            bu_im[:, sj] = _dot(uj, bim_ref[bj, :])
        ar, ai = are_ref[...], aim_ref[...]

        def step(t, carry):
            xr, xi = carry
            row = pl.ds(t, 1)
            nr = ar * xr - ai * xi + bu_re[row, :]
            ni = ar * xi + ai * xr + bu_im[row, :]
            sre_ref[row, :] = nr
            sim_ref[row, :] = ni
            return nr, ni

        xr, xi = lax.fori_loop(0, ts, step, (car_re[0:1, :], car_im[0:1, :]))
        car_re[0:1, :] = xr
        car_im[0:1, :] = xi
        y = [_dot(sre_ref[:, _gb_s(j)].astype(BF16), cre_ref[_gb_s(j), :])
             - _dot(sim_ref[:, _gb_s(j)].astype(BF16), cim_ref[_gb_s(j), :]) for j in range(N_GB)]
        y_ref[...] = jnp.concatenate(y, axis=1) + d_ref[...] * u

    row = lambda b, s: (b * ns + s, 0)
    whole = lambda b, s: (0, 0)
    return pl.pallas_call(
        kern, name="ssm_fwd", grid=(B, ns),
        in_specs=[pl.BlockSpec((ts, D_SSM), row),
                  pl.BlockSpec((D_SSM, GB_ST), whole), pl.BlockSpec((D_SSM, GB_ST), whole),
                  pl.BlockSpec((N_ST, GB_U), whole), pl.BlockSpec((N_ST, GB_U), whole),
                  pl.BlockSpec((1, D_SSM), whole), pl.BlockSpec((1, N_ST), whole), pl.BlockSpec((1, N_ST), whole)],
        out_specs=[pl.BlockSpec((ts, N_ST), row), pl.BlockSpec((ts, N_ST), row), pl.BlockSpec((ts, D_SSM), row)],
        out_shape=[jax.ShapeDtypeStruct((T, N_ST), F32), jax.ShapeDtypeStruct((T, N_ST), F32),
                   jax.ShapeDtypeStruct((T, D_SSM), F32)],
        scratch_shapes=[pltpu.VMEM((8, N_ST), F32), pltpu.VMEM((8, N_ST), F32),
                        pltpu.VMEM((ts, N_ST), F32), pltpu.VMEM((ts, N_ST), F32)],
        compiler_params=_params(("arbitrary", "arbitrary")),
    )(proj, b_re, b_im, c_re, c_im, d_vec, a_re, a_im)


def ssm_bwd(dy, proj, s_re, s_im, b_re, b_im, c_re, c_im, d_vec, a_re, a_im, B, S, ts):
    ns = S // ts
    T = B * S
    t8 = ts // 8

    def kern(dy_ref, u_ref, sre_ref, sim_ref, pre_ref, pim_ref, bre_ref, bim_ref, cre_ref, cim_ref,
             d_ref, are_ref, aim_ref,
             du_ref, dcre_ref, dcim_ref, dbre_ref, dbim_ref, dare_ref, daim_ref, dd_ref,
             car_re, car_im, g_re, g_im):
        b, s = pl.program_id(0), pl.program_id(1)
        first = jnp.logical_and(b == 0, s == 0)

        @pl.when(s == 0)
        def _reset():
            car_re[...] = jnp.zeros(car_re.shape, F32)
            car_im[...] = jnp.zeros(car_im.shape, F32)

        dyv = dy_ref[...]
        dyb = dyv.astype(BF16)
        u = u_ref[...]
        for j in range(N_GB):
            g_re[:, _gb_s(j)] = _dot_nt(dyb[:, _gb_u(j)], cre_ref[_gb_s(j), :])
            g_im[:, _gb_s(j)] = -_dot_nt(dyb[:, _gb_u(j)], cim_ref[_gb_s(j), :])
        ar, ai = are_ref[...], aim_ref[...]

        def step(i, carry):
            gr_next, gi_next = carry
            row = pl.ds(ts - 1 - i, 1)
            gr = g_re[row, :] + ar * gr_next + ai * gi_next
            gi = g_im[row, :] + ar * gi_next - ai * gr_next
            g_re[row, :] = gr
            g_im[row, :] = gi
            return gr, gi

        gr0, gi0 = lax.fori_loop(0, ts, step, (car_re[0:1, :], car_im[0:1, :]))
        car_re[0:1, :] = gr0
        car_im[0:1, :] = gi0

        gr, gi = g_re[...], g_im[...]
        sr, si = sre_ref[...], sim_ref[...]
        has_prev = (s < ns - 1).astype(F32)
        row_id = lax.broadcasted_iota(jnp.int32, (ts, N_ST), 0)
        spr = jnp.where(row_id == 0, pre_ref[7:8, :] * has_prev, pltpu.roll(sr, 1, axis=0))
        spi = jnp.where(row_id == 0, pim_ref[7:8, :] * has_prev, pltpu.roll(si, 1, axis=0))
        grb, gib, ub = gr.astype(BF16), gi.astype(BF16), u.astype(BF16)
        srb, sib = sr.astype(BF16), si.astype(BF16)
        cat0 = lambda f: jnp.concatenate([f(j) for j in range(N_GB)], axis=0)
        cat1 = lambda f: jnp.concatenate([f(j) for j in range(N_GB)], axis=1)
        parts = [
            (dcre_ref, cat0(lambda j: _dot_tn(srb[:, _gb_s(j)], dyb[:, _gb_u(j)]))),
            (dcim_ref, cat0(lambda j: -_dot_tn(sib[:, _gb_s(j)], dyb[:, _gb_u(j)]))),
            (dbre_ref, cat0(lambda j: _dot_tn(ub[:, _gb_u(j)], grb[:, _gb_s(j)]))),
            (dbim_ref, cat0(lambda j: _dot_tn(ub[:, _gb_u(j)], gib[:, _gb_s(j)]))),
            (dare_ref, _colsum(gr * spr + gi * spi)),
            (daim_ref, _colsum(gi * spr - gr * spi)),
            (dd_ref, _colsum(dyv * u)),
        ]
        du = cat1(lambda j: _dot_nt(grb[:, _gb_s(j)], bre_ref[_gb_u(j), :])
                  + _dot_nt(gib[:, _gb_s(j)], bim_ref[_gb_u(j), :]))
        du_ref[...] = du + d_ref[...] * dyv

        @pl.when(first)
        def _set():
            for ref, val in parts:
                ref[...] = val

        @pl.when(jnp.logical_not(first))
        def _add():
            for ref, val in parts:
                ref[...] = ref[...] + val

    row = lambda b, s: (b * ns + (ns - 1 - s), 0)
    prev = lambda b, s: (jnp.maximum((b * ns + (ns - 1 - s)) * t8 - 1, 0), 0)
    whole = lambda b, s: (0, 0)
    acc_shapes = [(N_ST, GB_U), (N_ST, GB_U), (D_SSM, GB_ST), (D_SSM, GB_ST), (1, N_ST), (1, N_ST), (1, D_SSM)]
    return pl.pallas_call(
        kern, name="ssm_bwd", grid=(B, ns),
        in_specs=[pl.BlockSpec((ts, D_SSM), row), pl.BlockSpec((ts, D_SSM), row),
                  pl.BlockSpec((ts, N_ST), row), pl.BlockSpec((ts, N_ST), row),
                  pl.BlockSpec((8, N_ST), prev), pl.BlockSpec((8, N_ST), prev),
                  pl.BlockSpec((D_SSM, GB_ST), whole), pl.BlockSpec((D_SSM, GB_ST), whole),
                  pl.BlockSpec((N_ST, GB_U), whole), pl.BlockSpec((N_ST, GB_U), whole),
                  pl.BlockSpec((1, D_SSM), whole), pl.BlockSpec((1, N_ST), whole), pl.BlockSpec((1, N_ST), whole)],
        out_specs=[pl.BlockSpec((ts, D_SSM), row)] + [pl.BlockSpec(sh, whole) for sh in acc_shapes],
        out_shape=[jax.ShapeDtypeStruct((T, D_SSM), F32)] + [jax.ShapeDtypeStruct(sh, F32) for sh in acc_shapes],
        scratch_shapes=[pltpu.VMEM((8, N_ST), F32), pltpu.VMEM((8, N_ST), F32),
                        pltpu.VMEM((ts, N_ST), F32), pltpu.VMEM((ts, N_ST), F32)],
        compiler_params=_params(("arbitrary", "arbitrary")),
    )(dy, proj, s_re, s_im, s_re, s_im, b_re, b_im, c_re, c_im, d_vec, a_re, a_im)


def modulation_fwd(c_all, ada_w, ada_b, fada_w, fada_b):
    def kern(c_ref, w_ref, b_ref, fw_ref, fb_ref, cond_ref, mod_ref, fmod_ref):
        cv = c_ref[...]
        cond = (cv * _sigmoid(cv)).astype(BF16)
        cond_ref[...] = cond
        mod_ref[...] = _dot(cond, w_ref[...].astype(BF16)) + b_ref[...]
        fmod_ref[...] = _dot(cond, fw_ref[...].astype(BF16)) + fb_ref[...]

    n = c_all.shape[0]
    return pl.pallas_call(
        kern, name="modulation_fwd",
        out_shape=[jax.ShapeDtypeStruct(c_all.shape, BF16), jax.ShapeDtypeStruct((n, ada_w.shape[1]), F32),
                   jax.ShapeDtypeStruct((n, fada_w.shape[1]), F32)],
        compiler_params=_params(),
    )(c_all, ada_w, ada_b, fada_w, fada_b)


def modulation_bwd(cond_all, dmod, dfmod, dmod_all):
    def kern(c_ref, dm_ref, dfm_ref, all_ref, gw_ref, gfw_ref, gb_ref):
        cond = c_ref[...]
        gw_ref[...] = _dot_tn(cond, dm_ref[...].astype(BF16))
        gfw_ref[...] = _dot_tn(cond, dfm_ref[...].astype(BF16))
        gb_ref[...] = _colsum(all_ref[...])

    d = cond_all.shape[1]
    return pl.pallas_call(
        kern, name="modulation_bwd",
        out_shape=[jax.ShapeDtypeStruct((d, dmod.shape[1]), F32), jax.ShapeDtypeStruct((d, dfmod.shape[1]), F32),
                   jax.ShapeDtypeStruct((1, dmod_all.shape[1]), F32)],
        compiler_params=_params(),
    )(cond_all, dmod, dfmod, dmod_all)


def adamw(name, parts, w, m, v):
    n, R, C = parts.shape
    tr = R
    while n * tr * C * 4 > 4 * 1024 * 1024 and tr % 16 == 0:
        tr //= 2

    def kern(p_ref, w_ref, m_ref, v_ref, g_o, d_o, m_o, v_o):
        g = p_ref[0]
        for i in range(1, n):
            g = g + p_ref[i]
        m_new = ADAM_B1 * m_ref[...] + (1.0 - ADAM_B1) * g
        v_new = ADAM_B2 * v_ref[...] + (1.0 - ADAM_B2) * (g * g)
        m_hat = m_new / (1.0 - ADAM_B1 ** ADAM_STEP)
        v_hat = v_new / (1.0 - ADAM_B2 ** ADAM_STEP)
        g_o[...] = g
        d_o[...] = -ADAM_LR * (m_hat / (jnp.sqrt(v_hat) + ADAM_EPS) + ADAM_WD * w_ref[...])
        m_o[...] = m_new
        v_o[...] = v_new

    blk = pl.BlockSpec((tr, C), lambda i: (i, 0))
    shp = jax.ShapeDtypeStruct((R, C), F32)
    return pl.pallas_call(
        kern, name=name, grid=(R // tr,),
        in_specs=[pl.BlockSpec((n, tr, C), lambda i: (0, i, 0)), blk, blk, blk],
        out_specs=[blk] * 4, out_shape=[shp] * 4,
        compiler_params=_params(("parallel",)),
    )(parts, w, m, v)


def _cols_from_gathered(g):
    return jnp.transpose(g, (1, 0, 2)).reshape(g.shape[1], N_DEV * g.shape[2])


def _cols_to_shards(w):
    k, n8 = w.shape
    return jnp.transpose(w.reshape(k, N_DEV, n8 // N_DEV), (1, 0, 2))


def _pad_w_in(w):
    z = functools.partial(jnp.zeros, dtype=w.dtype)
    k = w.shape[0]
    cut = D_SSM + Q_LORA + KV_LORA
    return jnp.concatenate([w[:, :cut], z((k, 64)), w[:, cut:], z((k, 32))], axis=1)


def _unpad_w_in(w):
    cut = D_SSM + Q_LORA + KV_LORA
    return jnp.concatenate([w[:, :cut], w[:, cut + 64:cut + 96]], axis=1)


def _pad_w_uq(w):
    k = w.shape[0]
    w3 = w.reshape(k, N_HEADS, QK_NOPE + QK_ROPE)
    return jnp.pad(w3, ((0, 0), (0, 0), (0, HEAD_PAD - QK_NOPE - QK_ROPE))).reshape(k, QK_W)


def _unpad_w_uq(w):
    k = w.shape[0]
    return w.reshape(k, N_HEADS, HEAD_PAD)[:, :, :QK_NOPE + QK_ROPE].reshape(k, N_HEADS * (QK_NOPE + QK_ROPE))


def _pad_w_ukv(w):
    k = w.shape[0]
    w3 = w.reshape(k, N_HEADS, QK_NOPE + V_HEAD)
    kpart = jnp.pad(w3[:, :, :QK_NOPE], ((0, 0), (0, 0), (0, HEAD_PAD - QK_NOPE))).reshape(k, QK_W)
    vpart = w3[:, :, QK_NOPE:].reshape(k, V_W)
    return jnp.concatenate([kpart, vpart], axis=1)


def _unpad_w_ukv(w):
    k = w.shape[0]
    kpart = w[:, :QK_W].reshape(k, N_HEADS, HEAD_PAD)[:, :, :QK_NOPE]
    vpart = w[:, QK_W:].reshape(k, N_HEADS, V_HEAD)
    return jnp.concatenate([kpart, vpart], axis=2).reshape(k, N_HEADS * (QK_NOPE + V_HEAD))


def _block_diag(blocks):
    g, r, c = blocks.shape
    b4 = blocks.reshape(N_GB, GB, r, c)
    keep = jnp.eye(GB, dtype=bool)[None, :, None, :, None]
    return jnp.where(keep, b4[:, :, :, None, :], 0).reshape(g * r, GB * c)


def _block_diag_take(stacked, r, c):
    d5 = stacked.reshape(N_GB, GB, r, GB, c)
    idx = jnp.arange(GB)
    return jnp.transpose(d5[:, idx, :, idx, :], (1, 0, 2, 3)).reshape(N_GROUPS, r, c)


SMALL = ["norm1_g", "ssm_lambda_re", "ssm_lambda_im", "ssm_b_re", "ssm_b_im", "ssm_c_re", "ssm_c_im",
         "ssm_d", "ssm_log_dt", "q_norm_g", "kv_norm_g", "ssm_out_g", "attn_out_g", "norm2_g", "final_norm_g"]
BIASES = ["ada_b", "final_ada_b"]
SMALL_COLS = 1024


def _pack_small(values):
    flat = jnp.concatenate([v.reshape(-1) for v in values])
    rows = -(-flat.shape[0] // (8 * SMALL_COLS)) * 8
    return jnp.pad(flat, (0, rows * SMALL_COLS - flat.shape[0])).reshape(rows, SMALL_COLS)


def _unpack_small(packed, like):
    flat = packed.reshape(-1)
    out, off = [], 0
    for ref in like:
        n = int(np.prod(ref.shape))
        out.append(flat[off:off + n].reshape(ref.shape))
        off += n
    return out


def kernel(x, c, positions, ada_w, ada_b, norm1_g, w_in, ssm_lambda_re, ssm_lambda_im, ssm_b_re, ssm_b_im, ssm_c_re, ssm_c_im, ssm_d, ssm_log_dt, w_glu, q_norm_g, w_uq, kv_norm_g, w_ukv, ssm_out_g, attn_out_g, w_out, norm2_g, w_ff1, w_ff2, final_ada_w, final_ada_b, final_norm_g, loss_target, m_ada_w, m_ada_b, m_norm1_g, m_w_in, m_ssm_lambda_re, m_ssm_lambda_im, m_ssm_b_re, m_ssm_b_im, m_ssm_c_re, m_ssm_c_im, m_ssm_d, m_ssm_log_dt, m_w_glu, m_q_norm_g, m_w_uq, m_kv_norm_g, m_w_ukv, m_ssm_out_g, m_attn_out_g, m_w_out, m_norm2_g, m_w_ff1, m_w_ff2, m_final_ada_w, m_final_ada_b, m_final_norm_g, v_ada_w, v_ada_b, v_norm1_g, v_w_in, v_ssm_lambda_re, v_ssm_lambda_im, v_ssm_b_re, v_ssm_b_im, v_ssm_c_re, v_ssm_c_im, v_ssm_d, v_ssm_log_dt, v_w_glu, v_q_norm_g, v_w_uq, v_kv_norm_g, v_w_ukv, v_ssm_out_g, v_attn_out_g, v_w_out, v_norm2_g, v_w_ff1, v_w_ff2, v_final_ada_w, v_final_ada_b, v_final_norm_g):
    given = dict(locals())
    B, S, D = x.shape
    T = B * S
    tm = min(256, S)
    me = 4 * lax.axis_index("x") + 2 * lax.axis_index("y") + lax.axis_index("c")

    big = ["w_in", "w_glu", "w_uq", "w_ukv", "w_out", "w_ff1", "w_ff2"]
    shards = {n: given[n][0] for n in big}
    early, late = ["w_in", "w_uq", "w_ukv"], ["w_glu", "w_out", "w_ff1", "w_ff2"]
    gathered = all_gather("gather_first_weights", [c] + [shards[n].astype(BF16) for n in early])
    c_all = gathered[0].reshape(N_DEV * B, D)
    gw = dict(zip(early, gathered[1:]))
    w_in_p = _pad_w_in(_cols_from_gathered(gw["w_in"]))
    w_uq_p = _pad_w_uq(_cols_from_gathered(gw["w_uq"]))
    w_ukv_p = _pad_w_ukv(_cols_from_gathered(gw["w_ukv"]))

    n_mod = ada_w.shape[2]
    n_fmod = final_ada_w.shape[1]
    ada_b_sh = lax.dynamic_slice_in_dim(ada_b, me * n_mod, n_mod, axis=1)
    fada_b_sh = lax.dynamic_slice_in_dim(final_ada_b[None, :], me * n_fmod, n_fmod, axis=1)
    cond_all, mod_sh, fmod_sh = modulation_fwd(c_all, ada_w[0], ada_b_sh, final_ada_w, fada_b_sh)
    mod_g, fmod_g = all_gather("gather_modulation", [mod_sh, fmod_sh])
    mod_mine = lax.dynamic_slice_in_dim(_cols_from_gathered(mod_g), me * B, B, axis=0)
    fmod_mine = lax.dynamic_slice_in_dim(_cols_from_gathered(fmod_g), me * B, B, axis=0)
    shift1, scale1, gate1, shift2, scale2, gate2 = [mod_mine[:, None, i * D:(i + 1) * D] for i in range(6)]
    fshift, fscale = fmod_mine[:, None, :D], fmod_mine[:, None, D:]

    x2d = x.reshape(T, D)
    tgt2d = loss_target.reshape(T, D)
    pos = positions.astype(F32).reshape(T, 1)
    invf, m_lo, m_hi = _rope_consts()
    rope_mask = m_lo + m_hi

    def fwd_in(rows, bv, vecs, res):
        (xv,), (sc, sh), (g1,), (w,) = rows, bv, vecs, res
        r, xh = _rms_stats(xv)
        h = xh * g1 * (1.0 + sc) + sh
        return [_dot(h.astype(BF16), w[...])], [], []

    (proj,), _, _ = row_call("fwd_in", fwd_in, B, S, tm, [x2d], [scale1, shift1], [norm1_g], [w_in_p],
                             [(IN_PAD, F32)], [], [])

    def fwd_qkv(rows, bv, vecs, res):
        (pr, ps), (gq, gkv, fr, lo, hi), (wq, wkv) = rows, vecs, res
        cs, s_lo, s_hi = _rope_tables(ps, fr, lo, hi)
        _, qh = _rms_stats(pr[:, D_SSM:D_SSM + Q_LORA])
        _, kvh = _rms_stats(pr[:, D_SSM + Q_LORA:D_SSM + Q_LORA + KV_LORA])
        qf = _dot((qh * gq).astype(BF16), wq[...])
        kvf = _dot((kvh * gkv).astype(BF16), wkv[...])
        k_rope = _rope_tile(pr[:, IN_PAD - HEAD_PAD:], cs, s_lo, s_hi, 1.0)
        q_t, k_t = [], []
        for h in range(N_HEADS):
            hs = slice(h * HEAD_PAD, (h + 1) * HEAD_PAD)
            q_t.append(_rope_tile(qf[:, hs], cs, s_lo, s_hi, 1.0))
            k_t.append(kvf[:, hs] + k_rope)
        kf, vf = jnp.concatenate(k_t, axis=1), kvf[:, QK_W:]
        return [jnp.concatenate(q_t, axis=1), kf, vf, kf.T, vf.T], [], []

    ns = S // tm
    col_tile = lambda b, s: (0, b * ns + s)
    (q_r, k_r, v_r, k_t, v_t), _, _ = row_call(
        "fwd_qkv", fwd_qkv, B, S, tm, [proj, pos], [], [q_norm_g, kv_norm_g, invf, m_lo, m_hi],
        [w_uq_p, w_ukv_p],
        [(QK_W, BF16), (QK_W, BF16), (V_W, BF16), ((QK_W, T), BF16, (QK_W, tm), col_tile),
         ((V_W, T), BF16, (V_W, tm), col_tile)], [], [])
    late_gather = Exchange([shards[n].astype(BF16) for n in late], [GATHER] * len(late))
    y_attn, lse, *late_g = attention_fwd(q_r, k_r, v_t, B, S, tm, late_gather)
    gw = dict(zip(late, late_g))
    w_glu_f = _cols_from_gathered(gw["w_glu"])
    w_out_f = gw["w_out"].reshape(-1, D)
    w_ff1_f = _cols_from_gathered(gw["w_ff1"])
    w_ff2_f = gw["w_ff2"].reshape(-1, D)

    rep = lambda a: jnp.repeat(a, GROUP, axis=0)
    lam_rep = [rep(ssm_lambda_re[0]), rep(ssm_lambda_im[0]),
               rep(jnp.broadcast_to(ssm_log_dt[0][:, None], (N_GROUPS, N_STATE)))]
    bt = lambda a: jnp.transpose(a, (0, 2, 1)).reshape(D_SSM, N_STATE)
    reps = lam_rep + [bt(ssm_b_re[0]), bt(ssm_b_im[0])]
    a_re_rep, a_im_rep, bb_re, bb_im = ssm_params_fwd(reps)
    a_re = a_re_rep.reshape(N_GROUPS, GROUP, N_STATE)[:, 0, :].reshape(1, N_ST)
    a_im = a_im_rep.reshape(N_GROUPS, GROUP, N_STATE)[:, 0, :].reshape(1, N_ST)
    bbd_re = _block_diag(bb_re.reshape(N_GROUPS, GROUP, N_STATE)).astype(BF16)
    bbd_im = _block_diag(bb_im.reshape(N_GROUPS, GROUP, N_STATE)).astype(BF16)
    cbd_re = _block_diag(jnp.transpose(ssm_c_re[0], (0, 2, 1))).astype(BF16)
    cbd_im = _block_diag(jnp.transpose(ssm_c_im[0], (0, 2, 1))).astype(BF16)
    d_vec = ssm_d.reshape(1, D_SSM)
    st_re, st_im, y_pre = ssm_fwd(proj, bbd_re, bbd_im, cbd_re, cbd_im, d_vec, a_re, a_im, B, S, tm)

    def fwd_mix(rows, bv, vecs, res):
        (yp, ya, xv), (g1v,), (gso, gao), (wg, wo) = rows, bv, vecs, res
        z = _dot(_gelu(yp).astype(BF16), wg[...])
        y_ssm = z[:, :D_SSM] * _sigmoid(z[:, D_SSM:])
        _, sh = _rms_stats(y_ssm)
        _, ah = _rms_stats(ya)
        o = _dot((sh * gso).astype(BF16), wo[0:D_SSM, :]) + _dot((ah * gao).astype(BF16), wo[D_SSM:, :])
        return [z, o, xv + g1v * o], [], []

    (z, o, x2), _, _ = row_call("fwd_mix", fwd_mix, B, S, tm, [y_pre, y_attn, x2d], [gate1],
                                [ssm_out_g, attn_out_g], [w_glu_f, w_out_f],
                                [(2 * D_SSM, F32), (D, F32), (D, F32)], [], [])

    def final_out(x3, gf, fsc, fsh):
        r3, xh3 = _rms_stats(x3)
        n3 = xh3 * gf
        return r3, xh3, n3, n3 * (1.0 + fsc) + fsh

    def mlp_hidden(xv, g2, sc, sh, w1):
        r2, xh2 = _rms_stats(xv)
        n2 = xh2 * g2
        h2 = n2 * (1.0 + sc) + sh
        return r2, xh2, n2, h2, _dot(h2.astype(BF16), w1[...])

    def fwd_mlp(rows, bv, vecs, res):
        (xv, tg), (sc, sh, g2v, fsc, fsh), (g2, gf), (w1, w2) = rows, bv, vecs, res
        _, _, _, _, a = mlp_hidden(xv, g2, sc, sh, w1)
        ra = jnp.maximum(a, 0.0)
        ff = _dot((ra * ra).astype(BF16), w2[...])
        x3 = xv + g2v * ff
        _, _, _, out = final_out(x3, gf, fsc, fsh)
        err = out - tg
        loss = 0.5 * jnp.sum(jnp.mean(err * err, axis=-1, keepdims=True), axis=0, keepdims=True)
        return [ff, x3], [], [jnp.broadcast_to(loss, (1, 128))]

    fnorm_g = final_norm_g[None, :]
    (ff, x3), _, (loss_acc,) = row_call(
        "fwd_mlp", fwd_mlp, B, S, tm, [x2, tgt2d], [scale2, shift2, gate2, fscale, fshift], [norm2_g, fnorm_g],
        [w_ff1_f, w_ff2_f], [(D, F32), (D, F32)], [], [(1, 128)])
    loss = lax.psum(loss_acc[0, 0], ("x", "y", "c"))

    def bwd_mlp(rows, bv, vecs, res):
        (x3v, tg, x2v, ffv), (sc, sh, g2v, fsc, fsh), (g2, gf), (w1, w2) = rows, bv, vecs, res
        r3, xh3, n3, out = final_out(x3v, gf, fsc, fsh)
        dout = (out - tg) * (1.0 / D)
        dn3 = dout * (1.0 + fsc)
        dx3 = _rms_bwd(dn3 * gf, xh3, r3)
        dff = dx3 * g2v
        r2, xh2, n2, h2, a = mlp_hidden(x2v, g2, sc, sh, w1)
        ra = jnp.maximum(a, 0.0)
        dffb = dff.astype(BF16)
        da = _dot_nt(dffb, w2[...]) * (2.0 * ra)
        dab = da.astype(BF16)
        dh2 = _dot_nt(dab, w1[...])
        dn2 = dh2 * (1.0 + sc)
        dx2 = dx3 + _rms_bwd(dn2 * g2, xh2, r2)
        return ([dx2, h2, ra * ra, dab, dffb],
                [_colsum(dout), _colsum(dout * n3), _colsum(dx3 * ffv), _colsum(dh2), _colsum(dh2 * n2)],
                [_colsum(dn3 * xh3), _colsum(dn2 * xh2)])

    ((dx2, h2b, fb, dab, dffb), (d_fshift, d_fscale, d_gate2, d_shift2, d_scale2), (d_gf, d_g2)) = row_call(
        "bwd_mlp", bwd_mlp, B, S, tm, [x3, tgt2d, x2, ff], [scale2, shift2, gate2, fscale, fshift],
        [norm2_g, fnorm_g], [w_ff1_f, w_ff2_f],
        [(D, F32), (D, BF16), (4 * D, BF16), (4 * D, BF16), (D, BF16)], [D] * 5, [(1, D), (1, D)])
    g_w_ff1 = matmul_tn("grad_w_ff1", h2b, dab)
    g_w_ff2 = matmul_tn("grad_w_ff2", fb, dffb)

    def bwd_mix(rows, bv, vecs, res):
        (dxv, ov, zv, ya, yp), (g1v,), (gso, gao), (wo, wg) = rows, bv, vecs, res
        do = (dxv * g1v).astype(BF16)
        dyn = _dot_nt(do, wo[...])
        z1, sg = zv[:, :D_SSM], _sigmoid(zv[:, D_SSM:])
        y_ssm = z1 * sg
        rs, sh = _rms_stats(y_ssm)
        ra, ah = _rms_stats(ya)
        dyn_s, dyn_a = dyn[:, :D_SSM], dyn[:, D_SSM:]
        dy_ssm = _rms_bwd(dyn_s * gso, sh, rs)
        dy_attn = _rms_bwd(dyn_a * gao, ah, ra)
        dz = jnp.concatenate([dy_ssm * sg, dy_ssm * z1 * sg * (1.0 - sg)], axis=1).astype(BF16)
        dy_pre = _dot_nt(dz, wg[...]) * _gelu_grad(yp)
        yn = jnp.concatenate([sh * gso, ah * gao], axis=1)
        delta = jnp.sum((dy_attn * ya).T.reshape(N_HEADS, V_HEAD, tm), axis=1, keepdims=True)
        return ([do, yn, dz, _gelu(yp), dy_attn, dy_pre, delta], [_colsum(dxv * ov)],
                [_colsum(dyn_s * sh), _colsum(dyn_a * ah)])

    ((dob, ynb, dzb, ygb, dy_attn, dy_pre, delta), (d_gate1,), (d_gso, d_gao)) = row_call(
        "bwd_mix", bwd_mix, B, S, tm, [dx2, o, z, y_attn, y_pre], [gate1], [ssm_out_g, attn_out_g],
        [w_out_f, w_glu_f],
        [(D, BF16), (D, BF16), (2 * D_SSM, BF16), (D_SSM, BF16), (V_W, F32), (D_SSM, F32),
         ((N_HEADS, 1, T), F32, (N_HEADS, 1, tm), lambda b, s: (0, 0, b * ns + s))],
        [D], [(1, D_SSM), (1, V_W)])
    g_w_out = matmul_tn("grad_w_out", ynb, dob)
    g_w_glu = matmul_tn("grad_w_glu", ygb, dzb)

    grads_a = Exchange([g_w_glu, g_w_out.reshape(N_DEV, -1, D), g_w_ff1, g_w_ff2.reshape(N_DEV, -1, D)],
                       [w_glu.shape[2], None, w_ff1.shape[2], None])
    dq_t, dk_r, dv_r, *parts_a = attention_bwd(q_r, k_r, k_t, v_r, dy_attn, lse, delta, B, S, tm, grads_a)
    parts = dict(zip(late, parts_a))
    dq_t = dq_t.reshape(B, QK_W, S)

    def bwd_qkv(rows, bv, vecs, res):
        (dqt, dkv, dvv, pr, ps), (gq, gkv, fr, lo, hi, rmask), (wq, wkv) = rows, vecs, res
        dqv = dqt[0].T
        cs, s_lo, s_hi = _rope_tables(ps, fr, lo, hi)
        dq_t = []
        dk_rope = jnp.zeros((dkv.shape[0], HEAD_PAD), F32)
        for h in range(N_HEADS):
            hs = slice(h * HEAD_PAD, (h + 1) * HEAD_PAD)
            dq_t.append(_rope_tile(dqv[:, hs], cs, s_lo, s_hi, -1.0))
            dk_rope = dk_rope + dkv[:, hs]
        dk_rope = _rope_tile(dk_rope * rmask, cs, s_lo, s_hi, -1.0)
        dqb = jnp.concatenate(dq_t, axis=1).astype(BF16)
        dkvb = jnp.concatenate([dkv, dvv], axis=1).astype(BF16)
        rq, qh = _rms_stats(pr[:, D_SSM:D_SSM + Q_LORA])
        rk, kvh = _rms_stats(pr[:, D_SSM + Q_LORA:D_SSM + Q_LORA + KV_LORA])
        dqn = _dot_nt(dqb, wq[...])
        dkvn = _dot_nt(dkvb, wkv[...])
        dpa = jnp.concatenate([_rms_bwd(dqn * gq, qh, rq), _rms_bwd(dkvn * gkv, kvh, rk), dk_rope], axis=1)
        return [dpa, qh * gq, kvh * gkv, dqb, dkvb], [], [_colsum(dqn * qh), _colsum(dkvn * kvh)]

    ((dpa, qnb, kvnb, dqb, dkvb), _, (d_gq, d_gkv)) = row_call(
        "bwd_qkv", bwd_qkv, B, S, tm,
        [(dq_t, (1, QK_W, tm), lambda b, s: (b, 0, s)), dk_r, dv_r, proj, pos], [],
        [q_norm_g, kv_norm_g, invf, m_lo, m_hi, rope_mask], [w_uq_p, w_ukv_p],
        [(Q_LORA + KV_LORA + HEAD_PAD, F32), (Q_LORA, BF16), (KV_LORA, BF16), (QK_W, BF16), (QK_W + V_W, BF16)],
        [], [(1, Q_LORA), (1, KV_LORA)])
    g_w_uq = _unpad_w_uq(matmul_tn("grad_w_uq", qnb, dqb))
    g_w_ukv = _unpad_w_ukv(matmul_tn("grad_w_ukv", kvnb, dkvb))

    du, dc_re_d, dc_im_d, db_re_d, db_im_d, da_re, da_im, d_d = ssm_bwd(
        dy_pre, proj, st_re, st_im, bbd_re, bbd_im, cbd_re, cbd_im, d_vec, a_re, a_im, B, S, tm)
    place = lambda a: jnp.zeros((N_GROUPS, GROUP, N_STATE), F32).at[:, 0, :].set(
        a.reshape(N_GROUPS, N_STATE)).reshape(D_SSM, N_STATE)
    cots = [place(da_re), place(da_im),
            _block_diag_take(db_re_d, GROUP, N_STATE).reshape(D_SSM, N_STATE),
            _block_diag_take(db_im_d, GROUP, N_STATE).reshape(D_SSM, N_STATE)]
    g_lam_re, g_lam_im, g_log_dt, g_bt_re, g_bt_im = ssm_params_bwd(reps, cots)
    unbt = lambda a: jnp.transpose(a.reshape(N_GROUPS, GROUP, N_STATE), (0, 2, 1))
    g_c_re = jnp.transpose(_block_diag_take(dc_re_d, N_STATE, GROUP), (0, 2, 1))
    g_c_im = jnp.transpose(_block_diag_take(dc_im_d, N_STATE, GROUP), (0, 2, 1))

    def bwd_in(rows, bv, vecs, res):
        (duv, dpav, xv, dx2v), (sc, sh), (g1,), (w,) = rows, bv, vecs, res
        dproj = jnp.concatenate([duv, dpav], axis=1).astype(BF16)
        dh = _dot_nt(dproj, w[...])
        r, xh = _rms_stats(xv)
        n1 = xh * g1
        dn1 = dh * (1.0 + sc)
        dx = dx2v + _rms_bwd(dn1 * g1, xh, r)
        return [dx, n1 * (1.0 + sc) + sh, dproj], [_colsum(dh), _colsum(dh * n1)], [_colsum(dn1 * xh)]

    ((grad_x, h1b, dprojb), (d_shift1, d_scale1), (d_g1,)) = row_call(
        "bwd_in", bwd_in, B, S, tm, [du, dpa, x2d, dx2], [scale1, shift1], [norm1_g], [w_in_p],
        [(D, F32), (D, BF16), (IN_PAD, BF16)], [D, D], [(1, D)])
    g_w_in = _unpad_w_in(matmul_tn("grad_w_in", h1b, dprojb))

    dmod = jnp.concatenate([d_shift1, d_scale1, d_gate1, d_shift2, d_scale2, d_gate2, d_fshift, d_fscale],
                           axis=2).reshape(B, 8 * D)
    small_part = {
        "norm1_g": d_g1, "ssm_lambda_re": g_lam_re,
        "ssm_lambda_im": g_lam_im, "ssm_b_re": unbt(g_bt_re), "ssm_b_im": unbt(g_bt_im), "ssm_c_re": g_c_re,
        "ssm_c_im": g_c_im, "ssm_d": d_d, "ssm_log_dt": g_log_dt, "q_norm_g": d_gq, "kv_norm_g": d_gkv,
        "ssm_out_g": d_gso, "attn_out_g": d_gao, "norm2_g": d_g2, "final_norm_g": d_gf}
    small_packed = _pack_small([small_part[n] for n in SMALL])
    dmod_g, small_g, *parts_b = exchange(
        "exchange_last_grads", [dmod, small_packed, _cols_to_shards(g_w_in), _cols_to_shards(g_w_uq), g_w_ukv],
        [GATHER, GATHER, None, None, w_ukv.shape[2]])
    parts.update(zip(early, parts_b))
    dmod_all = dmod_g.reshape(N_DEV * B, 8 * D)
    dm_sh = lax.dynamic_slice_in_dim(dmod_all[:, :6 * D], me * n_mod, n_mod, axis=1)
    dfm_sh = lax.dynamic_slice_in_dim(dmod_all[:, 6 * D:], me * n_fmod, n_fmod, axis=1)
    g_ada_w, g_fada_w, g_biases = modulation_bwd(cond_all, dm_sh, dfm_sh, dmod_all)
    parts["ada_w"] = g_ada_w[None]
    parts["final_ada_w"] = g_fada_w[None]

    res_g, res_d, res_m, res_v = {}, {}, {}, {}
    for n in ["ada_w"] + big + ["final_ada_w"]:
        w_full = given[n]
        w2 = w_full.reshape(w_full.shape[-2], w_full.shape[-1])
        out = adamw("adamw_" + n, parts[n], w2, given["m_" + n].reshape(w2.shape), given["v_" + n].reshape(w2.shape))
        res_g[n], res_d[n], res_m[n], res_v[n] = [a.reshape(w_full.shape) for a in out]
    small_w = [given[n] for n in SMALL]
    out = adamw("adamw_small", small_g, _pack_small(small_w), _pack_small([given["m_" + n] for n in SMALL]),
                _pack_small([given["v_" + n] for n in SMALL]))
    for res, packed in zip((res_g, res_d, res_m, res_v), out):
        for n, val in zip(SMALL, _unpack_small(packed, small_w)):
            res[n] = val
    bias_w = [given[n] for n in BIASES]
    out = adamw("adamw_biases", g_biases.reshape(1, 8, D), _pack_small(bias_w),
                _pack_small([given["m_" + n] for n in BIASES]), _pack_small([given["v_" + n] for n in BIASES]))
    for res, packed in zip((res_g, res_d, res_m, res_v), out):
        for n, val in zip(BIASES, _unpack_small(packed, bias_w)):
            res[n] = val

    order = ["ada_w", "ada_b", "norm1_g", "w_in", "ssm_lambda_re", "ssm_lambda_im", "ssm_b_re", "ssm_b_im",
             "ssm_c_re", "ssm_c_im", "ssm_d", "ssm_log_dt", "w_glu", "q_norm_g", "w_uq", "kv_norm_g", "w_ukv",
             "ssm_out_g", "attn_out_g", "w_out", "norm2_g", "w_ff1", "w_ff2", "final_ada_w", "final_ada_b",
             "final_norm_g"]
    return (loss, grad_x.reshape(B, S, D), *[res_g[n] for n in order], *[res_d[n] for n in order],
            *[res_m[n] for n in order], *[res_v[n] for n in order])
```

```python
import functools
import math

import numpy as np
import jax
import jax.numpy as jnp
from jax import lax
from jax.experimental import pallas as pl
from jax.experimental.pallas import tpu as pltpu

F32 = jnp.float32
BF16 = jnp.bfloat16

N_DEV = 8
EPS = 1e-6
D_SSM = 512
N_GROUPS = 32
GROUP = 16
N_STATE = 64
N_ST = N_GROUPS * N_STATE
Q_LORA = 384
KV_LORA = 256
QK_NOPE = 64
QK_ROPE = 32
V_HEAD = 64
N_HEADS = 8
HEAD_PAD = 128
QK_W = N_HEADS * HEAD_PAD
V_W = N_HEADS * V_HEAD
IN_COLS = D_SSM + Q_LORA + KV_LORA + QK_ROPE
IN_PAD = D_SSM + Q_LORA + KV_LORA + HEAD_PAD
ROPE_BASE = 10000.0
ATTN_SCALE = (QK_NOPE + QK_ROPE) ** -0.5
NEG = -1e30

ADAM_LR = 0.001
ADAM_B1 = 0.9
ADAM_B2 = 0.999
ADAM_EPS = 1e-08
ADAM_WD = 0.01
ADAM_STEP = 10

VMEM_LIMIT = 56 * 1024 * 1024
MESH_ID = pl.DeviceIdType.MESH


def _dot(a, b):
    return jnp.dot(a, b, preferred_element_type=F32)


def _dot_nt(a, b):
    return lax.dot_general(a, b, (((1,), (1,)), ((), ())), preferred_element_type=F32)


def _dot_tn(a, b):
    return lax.dot_general(a, b, (((0,), (0,)), ((), ())), preferred_element_type=F32)


def _rms_stats(x):
    r = lax.rsqrt(jnp.mean(x * x, axis=-1, keepdims=True) + EPS)
    return r, x * r


def _rms_bwd(dy, xh, r):
    return r * (dy - xh * jnp.mean(dy * xh, axis=-1, keepdims=True))


def _sigmoid(x):
    return 1.0 / (1.0 + jnp.exp(-x))


_GELU_C = math.sqrt(2.0 / math.pi)


def _gelu(x):
    t = jnp.tanh(_GELU_C * (x + 0.044715 * x * x * x))
    return 0.5 * x * (1.0 + t)


def _gelu_grad(x):
    t = jnp.tanh(_GELU_C * (x + 0.044715 * x * x * x))
    return 0.5 * (1.0 + t) + 0.5 * x * (1.0 - t * t) * _GELU_C * (1.0 + 3.0 * 0.044715 * x * x)


def _colsum(a):
    return jnp.sum(a, axis=0, keepdims=True)


def _params(sem=None):
    return pltpu.CompilerParams(dimension_semantics=sem, vmem_limit_bytes=VMEM_LIMIT)


def _mesh_pos():
    x, y, c = lax.axis_index("x"), lax.axis_index("y"), lax.axis_index("c")
    return x, y, c, 4 * x + 2 * y + c


def _peer(x, y, c, k):
    px = 1 - x if k & 4 else x
    py = 1 - y if k & 2 else y
    pc = 1 - c if k & 1 else c
    return (px, py, pc), 4 * px + 2 * py + pc


GATHER = "gather"
ANY_SPEC = pl.BlockSpec(memory_space=pl.ANY)


class Exchange:
    def __init__(self, arrays, kinds):
        self.arrays, self.kinds, self.n = list(arrays), list(kinds), len(arrays)

    def _shard(self, i):
        a, kind = self.arrays[i], self.kinds[i]
        if kind == GATHER:
            return a.shape
        return a.shape[1:] if kind is None else (a.shape[0], kind)

    def out_shapes(self):
        return [jax.ShapeDtypeStruct((N_DEV,) + tuple(self._shard(i)), self.arrays[i].dtype) for i in range(self.n)]

    def scratch(self):
        return [pltpu.SemaphoreType.DMA((self.n, N_DEV - 1)), pltpu.SemaphoreType.DMA((self.n, N_DEV - 1)),
                pltpu.SemaphoreType.DMA((self.n,))]

    def _src(self, ins, i, j):
        kind = self.kinds[i]
        if kind == GATHER:
            return ins[i]
        if kind is None:
            return ins[i].at[j]
        return ins[i].at[:, pl.ds(pl.multiple_of(j * kind, 128), kind)]

    def _copies(self, ins, outs, sems):
        send_sems, recv_sems, loc_sems = sems
        x, y, c, me = _mesh_pos()
        local, sent, received = [], [], []
        for i in range(self.n):
            local.append(pltpu.make_async_copy(self._src(ins, i, me), outs[i].at[me], loc_sems.at[i]))
            for k in range(1, N_DEV):
                peer, pidx = _peer(x, y, c, k)
                pair = dict(send_sem=send_sems.at[i, k - 1], recv_sem=recv_sems.at[i, k - 1], device_id=peer,
                            device_id_type=MESH_ID)
                sent.append(pltpu.make_async_remote_copy(
                    src_ref=self._src(ins, i, pidx), dst_ref=outs[i].at[me], **pair))
                received.append(pltpu.make_async_remote_copy(
                    src_ref=self._src(ins, i, pidx), dst_ref=outs[i].at[pidx], **pair))
        return local, sent, received

    def start(self, ins, outs, sems):
        local, sent, _ = self._copies(ins, outs, sems)
        for cp in local + sent:
            cp.start()

    def wait(self, ins, outs, sems):
        local, sent, received = self._copies(ins, outs, sems)
        for cp in received:
            cp.wait_recv()
        for cp in sent:
            cp.wait_send()
        for cp in local:
            cp.wait()


def exchange(name, arrays, kinds):
    ex = Exchange(arrays, kinds)
    n = ex.n

    def body(*refs):
        ins, outs, sems = refs[:n], refs[n:2 * n], refs[2 * n:]
        ex.start(ins, outs, sems)
        ex.wait(ins, outs, sems)

    return pl.pallas_call(
        body, name=name, out_shape=ex.out_shapes(), in_specs=[ANY_SPEC] * n, out_specs=[ANY_SPEC] * n,
        scratch_shapes=ex.scratch())(*arrays)


def all_gather(name, arrays):
    return exchange(name, arrays, [GATHER] * len(arrays))


def row_call(name, body, B, S, tm, rows, bvecs, vecs, res, row_outs, bacc, gacc):
    ns = S // tm
    T = B * S
    n_r, n_b, n_v, n_w = len(rows), len(bvecs), len(vecs), len(res)
    n_ro, n_ba, n_ga = len(row_outs), len(bacc), len(gacc)

    def kern(*refs):
        i = 0
        r_refs = refs[i:i + n_r]; i += n_r
        b_refs = refs[i:i + n_b]; i += n_b
        v_refs = refs[i:i + n_v]; i += n_v
        w_hbm = refs[i:i + n_w]; i += n_w
        ro_refs = refs[i:i + n_ro]; i += n_ro
        ba_refs = refs[i:i + n_ba]; i += n_ba
        ga_refs = refs[i:i + n_ga]; i += n_ga
        w_vmem = refs[i:i + n_w]
        b = pl.program_id(0)
        s = pl.program_id(1)
        first = jnp.logical_and(b == 0, s == 0)

        if n_w:
            @pl.when(first)
            def _load_weights():
                for h, v in zip(w_hbm, w_vmem):
                    pltpu.sync_copy(h, v)

        ro, ba, ga = body([r[...] for r in r_refs], [r[0] for r in b_refs], [r[...] for r in v_refs],
                          list(w_vmem))
        for ref, val in zip(ro_refs, ro):
            ref[...] = val.astype(ref.dtype)

        def accumulate(ref, val, is_first, idx):
            @pl.when(is_first)
            def _set():
                ref[idx] = val

            @pl.when(jnp.logical_not(is_first))
            def _add():
                ref[idx] = ref[idx] + val

        for ref, val in zip(ba_refs, ba):
            accumulate(ref, val, s == 0, 0)
        for ref, val in zip(ga_refs, ga):
            accumulate(ref, val, first, Ellipsis)

    row_arrays = [r[0] if isinstance(r, tuple) else r for r in rows]
    row_in_specs = [pl.BlockSpec(r[1], r[2]) if isinstance(r, tuple)
                    else pl.BlockSpec((tm, r.shape[1]), lambda b, s: (b * ns + s, 0)) for r in rows]
    ro_shapes = [jax.ShapeDtypeStruct(tuple(o[0]), o[1]) if len(o) == 4 else jax.ShapeDtypeStruct((T, o[0]), o[1])
                 for o in row_outs]
    ro_specs = [pl.BlockSpec(o[2], o[3]) if len(o) == 4 else pl.BlockSpec((tm, o[0]), lambda b, s: (b * ns + s, 0))
                for o in row_outs]
    in_specs = (row_in_specs
                + [pl.BlockSpec((1, 1, v.shape[2]), lambda b, s: (b, 0, 0)) for v in bvecs]
                + [pl.BlockSpec(v.shape, lambda b, s, nd=v.ndim: (0,) * nd) for v in vecs]
                + [pl.BlockSpec(memory_space=pl.ANY) for _ in res])
    out_shape = (ro_shapes
                 + [jax.ShapeDtypeStruct((B, 1, w), F32) for w in bacc]
                 + [jax.ShapeDtypeStruct(tuple(sh), F32) for sh in gacc])
    out_specs = (ro_specs
                 + [pl.BlockSpec((1, 1, w), lambda b, s: (b, 0, 0)) for w in bacc]
                 + [pl.BlockSpec(tuple(sh), lambda b, s, nd=len(sh): (0,) * nd) for sh in gacc])
    outs = pl.pallas_call(
        kern, name=name, grid=(B, ns), in_specs=in_specs, out_specs=out_specs, out_shape=out_shape,
        scratch_shapes=[pltpu.VMEM(w.shape, w.dtype) for w in res],
        compiler_params=_params(("arbitrary", "arbitrary")),
    )(*row_arrays, *bvecs, *vecs, *res)
    return outs[:n_ro], outs[n_ro:n_ro + n_ba], outs[n_ro + n_ba:]


def matmul_tn(name, a, b):
    T, K = a.shape
    N = b.shape[1]
    tk = min(K, 512)
    tt = min(T, 512)
    nt = T // tt

    def kern(a_ref, b_ref, o_ref):
        t = pl.program_id(1)
        part = _dot_tn(a_ref[...], b_ref[...])

        @pl.when(t == 0)
        def _set():
            o_ref[...] = part

        @pl.when(t > 0)
        def _add():
            o_ref[...] = o_ref[...] + part

    return pl.pallas_call(
        kern, name=name, grid=(K // tk, nt),
        in_specs=[pl.BlockSpec((tt, tk), lambda i, t: (t, i)), pl.BlockSpec((tt, N), lambda i, t: (t, 0))],
        out_specs=pl.BlockSpec((tk, N), lambda i, t: (i, 0)),
        out_shape=jax.ShapeDtypeStruct((K, N), F32),
        compiler_params=_params(("parallel", "arbitrary")),
    )(a, b)


def _rope_consts():
    inv_freq = ROPE_BASE ** (-jnp.arange(0, QK_ROPE, 2, dtype=F32) / QK_ROPE)
    zeros64 = jnp.zeros((64,), F32)
    zeros32 = jnp.zeros((32,), F32)
    zeros16 = jnp.zeros((16,), F32)
    ones16 = jnp.ones((16,), F32)
    invf = jnp.concatenate([zeros64, inv_freq, inv_freq, zeros32])[None, :]
    m_lo = jnp.concatenate([zeros64, ones16, zeros16, zeros32])[None, :]
    m_hi = jnp.concatenate([zeros64, zeros16, ones16, zeros32])[None, :]
    return invf, m_lo, m_hi


def _rope_tables(pos, invf, m_lo, m_hi):
    ang = pos * invf
    return jnp.cos(ang), jnp.sin(ang) * m_lo, jnp.sin(ang) * m_hi


def _rope_tile(t, cs, s_lo, s_hi, sign):
    up = pltpu.roll(t, HEAD_PAD - 16, axis=1)
    down = pltpu.roll(t, 16, axis=1)
    return t * cs + sign * (down * s_hi - up * s_lo)


def _heads(ref, width):
    return jnp.stack([ref[:, h * width:(h + 1) * width] for h in range(N_HEADS)], axis=0)


def _bmm(a, b, ca, cb):
    return lax.dot_general(a, b, (((ca,), (cb,)), ((0,), (0,))), preferred_element_type=F32)


def _scores_t(k_ref, q_ref, masked, tq):
    st = _bmm(_heads(k_ref, HEAD_PAD), _heads(q_ref, HEAD_PAD), 2, 2) * ATTN_SCALE
    if masked:
        key = lax.broadcasted_iota(jnp.int32, (tq, tq), 0)
        qry = lax.broadcasted_iota(jnp.int32, (tq, tq), 1)
        st = jnp.where((key <= qry)[None], st, NEG)
    return st


def attention_fwd(q, k, vt, B, S, tq, side):
    nq = S // tq
    ns = side.n
    pairs = [(i, j) for i in range(nq) for j in range(i + 1)]
    n_pairs = len(pairs)
    q_tab = jnp.asarray([p[0] for p in pairs], jnp.int32)
    k_tab = jnp.asarray([p[1] for p in pairs], jnp.int32)

    def kern(q_tab_ref, k_tab_ref, *refs):
        q_ref, k_ref, vt_ref = refs[:3]
        side_in = refs[3:3 + ns]
        o_ref, lse_ref = refs[3 + ns:5 + ns]
        side_out = refs[5 + ns:5 + 2 * ns]
        m_scr, l_scr, acc_scr = refs[5 + 2 * ns:8 + 2 * ns]
        sems = refs[8 + 2 * ns:]
        b, t = pl.program_id(0), pl.program_id(1)
        qi, ki = q_tab_ref[t], k_tab_ref[t]

        @pl.when(jnp.logical_and(b == 0, t == 0))
        def _start_side():
            side.start(side_in, side_out, sems)

        @pl.when(ki == 0)
        def _init():
            m_scr[...] = jnp.full(m_scr.shape, NEG, F32)
            l_scr[...] = jnp.zeros(l_scr.shape, F32)
            acc_scr[...] = jnp.zeros(acc_scr.shape, F32)

        def block(masked):
            st = _scores_t(k_ref, q_ref, masked, tq)
            m_prev = m_scr[...]
            m_new = jnp.maximum(m_prev, jnp.max(st, axis=1, keepdims=True))
            alpha = jnp.exp(m_prev - m_new)
            p = jnp.exp(st - m_new)
            l_scr[...] = alpha * l_scr[...] + jnp.sum(p, axis=1, keepdims=True)
            vt3 = vt_ref[...].reshape(N_HEADS, V_HEAD, tq)
            acc_scr[...] = alpha * acc_scr[...] + _bmm(vt3, p.astype(BF16), 2, 1)
            m_scr[...] = m_new

        @pl.when(ki < qi)
        def _full():
            block(False)

        @pl.when(ki == qi)
        def _diagonal():
            block(True)
            ot = acc_scr[...] / l_scr[...]
            for h in range(N_HEADS):
                o_ref[:, h * V_HEAD:(h + 1) * V_HEAD] = ot[h].T
            lse_ref[...] = m_scr[...] + jnp.log(l_scr[...])

        @pl.when(jnp.logical_and(b == B - 1, t == n_pairs - 1))
        def _wait_side():
            side.wait(side_in, side_out, sems)

    T = B * S
    grid_spec = pltpu.PrefetchScalarGridSpec(
        num_scalar_prefetch=2, grid=(B, n_pairs),
        in_specs=[pl.BlockSpec((tq, QK_W), lambda b, t, qt, kt: (b * nq + qt[t], 0)),
                  pl.BlockSpec((tq, QK_W), lambda b, t, qt, kt: (b * nq + kt[t], 0)),
                  pl.BlockSpec((V_W, tq), lambda b, t, qt, kt: (0, b * nq + kt[t]))] + [ANY_SPEC] * ns,
        out_specs=[pl.BlockSpec((tq, V_W), lambda b, t, qt, kt: (b * nq + qt[t], 0)),
                   pl.BlockSpec((N_HEADS, 1, tq), lambda b, t, qt, kt: (0, 0, b * nq + qt[t]))] + [ANY_SPEC] * ns,
        scratch_shapes=[pltpu.VMEM((N_HEADS, 1, tq), F32), pltpu.VMEM((N_HEADS, 1, tq), F32),
                        pltpu.VMEM((N_HEADS, V_HEAD, tq), F32)] + side.scratch())
    return pl.pallas_call(
        kern, name="attention_fwd", grid_spec=grid_spec,
        out_shape=[jax.ShapeDtypeStruct((T, V_W), F32), jax.ShapeDtypeStruct((N_HEADS, 1, T), F32)]
        + side.out_shapes(),
        compiler_params=_params(("arbitrary", "arbitrary")),
    )(q_tab, k_tab, q, k, vt, *side.arrays)


def attention_bwd(q, k, kt, v, do, lse, delta, B, S, tq, side):
    nq = S // tq
    ns = side.n
    pairs = [(j, i) for j in range(nq) for i in range(j, nq)]
    n_pairs = len(pairs)
    k_tab = jnp.asarray([p[0] for p in pairs], jnp.int32)
    q_tab = jnp.asarray([p[1] for p in pairs], jnp.int32)

    def kern(k_tab_ref, q_tab_ref, *refs):
        q_ref, k_ref, kt_ref, v_ref, do_ref, lse_ref, delta_ref = refs[:7]
        side_in = refs[7:7 + ns]
        dq_hbm, dk_ref, dv_ref = refs[7 + ns:10 + ns]
        side_out = refs[10 + ns:10 + 2 * ns]
        dq_acc, dk_acc, dv_acc = refs[10 + 2 * ns:13 + 2 * ns]
        sems = refs[13 + 2 * ns:]
        b, t = pl.program_id(0), pl.program_id(1)
        kj, qi = k_tab_ref[t], q_tab_ref[t]

        @pl.when(jnp.logical_and(b == 0, t == 0))
        def _start_side():
            side.start(side_in, side_out, sems)

        @pl.when(t == 0)
        def _zero_dq():
            dq_acc[...] = jnp.zeros(dq_acc.shape, F32)

        @pl.when(qi == kj)
        def _zero_dkv():
            dk_acc[...] = jnp.zeros(dk_acc.shape, F32)
            dv_acc[...] = jnp.zeros(dv_acc.shape, F32)

        def block(masked):
            pt = jnp.exp(_scores_t(k_ref, q_ref, masked, tq) - lse_ref[...])
            do3 = _heads(do_ref, V_HEAD).astype(BF16)
            dv_acc[...] = dv_acc[...] + _bmm(pt.astype(BF16), do3, 2, 1)
            dpt = _bmm(_heads(v_ref, V_HEAD), do3, 2, 2)
            dst = (pt * (dpt - delta_ref[...]) * ATTN_SCALE).astype(BF16)
            dk_acc[...] = dk_acc[...] + _bmm(dst, _heads(q_ref, HEAD_PAD), 2, 1)
            q_cols = pl.ds(pl.multiple_of(qi * tq, tq), tq)
            kt3 = kt_ref[...].reshape(N_HEADS, HEAD_PAD, tq)
            dq_acc[:, :, q_cols] = dq_acc[:, :, q_cols] + _bmm(kt3, dst, 2, 1)

        @pl.when(qi > kj)
        def _full():
            block(False)

        @pl.when(qi == kj)
        def _diagonal():
            block(True)

        @pl.when(qi == nq - 1)
        def _write_dkv():
            for h in range(N_HEADS):
                dk_ref[:, h * HEAD_PAD:(h + 1) * HEAD_PAD] = dk_acc[h]
                dv_ref[:, h * V_HEAD:(h + 1) * V_HEAD] = dv_acc[h]

        @pl.when(t == n_pairs - 1)
        def _write_dq():
            pltpu.sync_copy(dq_acc, dq_hbm.at[b])

        @pl.when(jnp.logical_and(b == B - 1, t == n_pairs - 1))
        def _wait_side():
            side.wait(side_in, side_out, sems)

    T = B * S
    qrow = lambda b, t, kt, qt: (b * nq + qt[t], 0)
    qcol3 = lambda b, t, kt, qt: (0, 0, b * nq + qt[t])
    krow = lambda b, t, kt, qt: (b * nq + kt[t], 0)
    grid_spec = pltpu.PrefetchScalarGridSpec(
        num_scalar_prefetch=2, grid=(B, n_pairs),
        in_specs=[pl.BlockSpec((tq, QK_W), qrow), pl.BlockSpec((tq, QK_W), krow),
                  pl.BlockSpec((QK_W, tq), lambda b, t, kt, qt: (0, b * nq + kt[t])), pl.BlockSpec((tq, V_W), krow),
                  pl.BlockSpec((tq, V_W), qrow), pl.BlockSpec((N_HEADS, 1, tq), qcol3),
                  pl.BlockSpec((N_HEADS, 1, tq), qcol3)] + [ANY_SPEC] * ns,
        out_specs=[ANY_SPEC, pl.BlockSpec((tq, QK_W), krow), pl.BlockSpec((tq, V_W), krow)] + [ANY_SPEC] * ns,
        scratch_shapes=[pltpu.VMEM((N_HEADS, HEAD_PAD, S), F32), pltpu.VMEM((N_HEADS, tq, HEAD_PAD), F32),
                        pltpu.VMEM((N_HEADS, tq, V_HEAD), F32)] + side.scratch())
    return pl.pallas_call(
        kern, name="attention_bwd", grid_spec=grid_spec,
        out_shape=[jax.ShapeDtypeStruct((B, N_HEADS, HEAD_PAD, S), F32), jax.ShapeDtypeStruct((T, QK_W), F32),
                   jax.ShapeDtypeStruct((T, V_W), F32)] + side.out_shapes(),
        compiler_params=_params(("arbitrary", "arbitrary")),
    )(k_tab, q_tab, q, k, kt, v, do, lse, delta, *side.arrays)


GB = 8
N_GB = N_GROUPS // GB
GB_U = GB * GROUP
GB_ST = GB * N_STATE


def _gb_u(j):
    return slice(j * GB_U, (j + 1) * GB_U)


def _gb_s(j):
    return slice(j * GB_ST, (j + 1) * GB_ST)


def ssm_param_fn(lr, li, ld, br, bi):
    dt = jnp.exp(ld)
    er = jnp.exp(lr * dt)
    ar = er * jnp.cos(li * dt)
    ai = er * jnp.sin(li * dt)
    nr = ar - 1.0
    den = lr * lr + li * li
    cr = (nr * lr + ai * li) / den
    ci = (ai * lr - nr * li) / den
    return ar, ai, cr * br - ci * bi, cr * bi + ci * br


def ssm_params_fwd(reps):
    def kern(lr, li, ld, br, bi, ar_o, ai_o, bbr_o, bbi_o):
        ar, ai, bbr, bbi = ssm_param_fn(lr[...], li[...], ld[...], br[...], bi[...])
        ar_o[...] = ar
        ai_o[...] = ai
        bbr_o[...] = bbr
        bbi_o[...] = bbi

    shp = jax.ShapeDtypeStruct(reps[0].shape, F32)
    return pl.pallas_call(kern, name="ssm_params_fwd", out_shape=[shp] * 4)(*reps)


def ssm_params_bwd(reps, cots):
    rows = reps[0].shape[0]

    def kern(lr, li, ld, br, bi, d_ar, d_ai, d_bbr, d_bbi, dlr_o, dli_o, dld_o, dbr_o, dbi_o):
        _, vjp = jax.vjp(ssm_param_fn, lr[...], li[...], ld[...], br[...], bi[...])
        dlr, dli, dld, dbr, dbi = vjp((d_ar[...], d_ai[...], d_bbr[...], d_bbi[...]))
        dlr_o[...] = jnp.sum(dlr.reshape(N_GROUPS, GROUP, N_STATE), axis=1)
        dli_o[...] = jnp.sum(dli.reshape(N_GROUPS, GROUP, N_STATE), axis=1)
        dld_o[...] = jnp.sum(jnp.sum(dld.reshape(N_GROUPS, GROUP, N_STATE), axis=1), axis=-1, keepdims=True)
        dbr_o[...] = dbr
        dbi_o[...] = dbi

    g = jax.ShapeDtypeStruct((N_GROUPS, N_STATE), F32)
    full = jax.ShapeDtypeStruct((rows, N_STATE), F32)
    return pl.pallas_call(
        kern, name="ssm_params_bwd",
        out_shape=[g, g, jax.ShapeDtypeStruct((N_GROUPS, 1), F32), full, full])(*reps, *cots)


def ssm_fwd(proj, b_re, b_im, c_re, c_im, d_vec, a_re, a_im, B, S, ts):
    ns = S // ts
    T = B * S

    def kern(u_ref, bre_ref, bim_ref, cre_ref, cim_ref, d_ref, are_ref, aim_ref,
             sre_ref, sim_ref, y_ref, car_re, car_im, bu_re, bu_im):
        s = pl.program_id(1)

        @pl.when(s == 0)
        def _reset():
            car_re[...] = jnp.zeros(car_re.shape, F32)
            car_im[...] = jnp.zeros(car_im.shape, F32)

        u = u_ref[...]
        ub = u.astype(BF16)
        for j in range(N_GB):
            uj, bj, sj = ub[:, _gb_u(j)], _gb_u(j), _gb_s(j)
            bu_re[:, sj] = _dot(uj, bre_ref[bj, :])
            bu_im[:, sj] = _dot(uj, bim_ref[bj, :])
        ar, ai = are_ref[...], aim_ref[...]

        def step(t, carry):
            xr, xi = carry
            row = pl.ds(t, 1)
            nr = ar * xr - ai * xi + bu_re[row, :]
            ni = ar * xi + ai * xr + bu_im[row, :]
            sre_ref[row, :] = nr
            sim_ref[row, :] = ni
            return nr, ni

        xr, xi = lax.fori_loop(0, ts, step, (car_re[0:1, :], car_im[0:1, :]))
        car_re[0:1, :] = xr
        car_im[0:1, :] = xi
        y = [_dot(sre_ref[:, _gb_s(j)].astype(BF16), cre_ref[_gb_s(j), :])
             - _dot(sim_ref[:, _gb_s(j)].astype(BF16), cim_ref[_gb_s(j), :]) for j in range(N_GB)]
        y_ref[...] = jnp.concatenate(y, axis=1) + d_ref[...] * u

    row = lambda b, s: (b * ns + s, 0)
    whole = lambda b, s: (0, 0)
    return pl.pallas_call(
        kern, name="ssm_fwd", grid=(B, ns),
        in_specs=[pl.BlockSpec((ts, D_SSM), row),
                  pl.BlockSpec((D_SSM, GB_ST), whole), pl.BlockSpec((D_SSM, GB_ST), whole),
                  pl.BlockSpec((N_ST, GB_U), whole), pl.BlockSpec((N_ST, GB_U), whole),
                  pl.BlockSpec((1, D_SSM), whole), pl.BlockSpec((1, N_ST), whole), pl.BlockSpec((1, N_ST), whole)],
        out_specs=[pl.BlockSpec((ts, N_ST), row), pl.BlockSpec((ts, N_ST), row), pl.BlockSpec((ts, D_SSM), row)],
        out_shape=[jax.ShapeDtypeStruct((T, N_ST), F32), jax.ShapeDtypeStruct((T, N_ST), F32),
                   jax.ShapeDtypeStruct((T, D_SSM), F32)],
        scratch_shapes=[pltpu.VMEM((8, N_ST), F32), pltpu.VMEM((8, N_ST), F32),
                        pltpu.VMEM((ts, N_ST), F32), pltpu.VMEM((ts, N_ST), F32)],
        compiler_params=_params(("arbitrary", "arbitrary")),
    )(proj, b_re, b_im, c_re, c_im, d_vec, a_re, a_im)


def ssm_bwd(dy, proj, s_re, s_im, b_re, b_im, c_re, c_im, d_vec, a_re, a_im, B, S, ts):
    ns = S // ts
    T = B * S
    t8 = ts // 8

    def kern(dy_ref, u_ref, sre_ref, sim_ref, pre_ref, pim_ref, bre_ref, bim_ref, cre_ref, cim_ref,
             d_ref, are_ref, aim_ref,
             du_ref, dcre_ref, dcim_ref, dbre_ref, dbim_ref, dare_ref, daim_ref, dd_ref,
             car_re, car_im, g_re, g_im):
        b, s = pl.program_id(0), pl.program_id(1)
        first = jnp.logical_and(b == 0, s == 0)

        @pl.when(s == 0)
        def _reset():
            car_re[...] = jnp.zeros(car_re.shape, F32)
            car_im[...] = jnp.zeros(car_im.shape, F32)

        dyv = dy_ref[...]
        dyb = dyv.astype(BF16)
        u = u_ref[...]
        for j in range(N_GB):
            g_re[:, _gb_s(j)] = _dot_nt(dyb[:, _gb_u(j)], cre_ref[_gb_s(j), :])
            g_im[:, _gb_s(j)] = -_dot_nt(dyb[:, _gb_u(j)], cim_ref[_gb_s(j), :])
        ar, ai = are_ref[...], aim_ref[...]

        def step(i, carry):
            gr_next, gi_next = carry
            row = pl.ds(ts - 1 - i, 1)
            gr = g_re[row, :] + ar * gr_next + ai * gi_next
            gi = g_im[row, :] + ar * gi_next - ai * gr_next
            g_re[row, :] = gr
            g_im[row, :] = gi
            return gr, gi

        gr0, gi0 = lax.fori_loop(0, ts, step, (car_re[0:1, :], car_im[0:1, :]))
        car_re[0:1, :] = gr0
        car_im[0:1, :] = gi0

        gr, gi = g_re[...], g_im[...]
        sr, si = sre_ref[...], sim_ref[...]
        has_prev = (s < ns - 1).astype(F32)
        row_id = lax.broadcasted_iota(jnp.int32, (ts, N_ST), 0)
        spr = jnp.where(row_id == 0, pre_ref[7:8, :] * has_prev, pltpu.roll(sr, 1, axis=0))
        spi = jnp.where(row_id == 0, pim_ref[7:8, :] * has_prev, pltpu.roll(si, 1, axis=0))
        grb, gib, ub = gr.astype(BF16), gi.astype(BF16), u.astype(BF16)
        srb, sib = sr.astype(BF16), si.astype(BF16)
        cat0 = lambda f: jnp.concatenate([f(j) for j in range(N_GB)], axis=0)
        cat1 = lambda f: jnp.concatenate([f(j) for j in range(N_GB)], axis=1)
        parts = [
            (dcre_ref, cat0(lambda j: _dot_tn(srb[:, _gb_s(j)], dyb[:, _gb_u(j)]))),
            (dcim_ref, cat0(lambda j: -_dot_tn(sib[:, _gb_s(j)], dyb[:, _gb_u(j)]))),
            (dbre_ref, cat0(lambda j: _dot_tn(ub[:, _gb_u(j)], grb[:, _gb_s(j)]))),
            (dbim_ref, cat0(lambda j: _dot_tn(ub[:, _gb_u(j)], gib[:, _gb_s(j)]))),
            (dare_ref, _colsum(gr * spr + gi * spi)),
            (daim_ref, _colsum(gi * spr - gr * spi)),
            (dd_ref, _colsum(dyv * u)),
        ]
        du = cat1(lambda j: _dot_nt(grb[:, _gb_s(j)], bre_ref[_gb_u(j), :])
                  + _dot_nt(gib[:, _gb_s(j)], bim_ref[_gb_u(j), :]))
        du_ref[...] = du + d_ref[...] * dyv

        @pl.when(first)
        def _set():
            for ref, val in parts:
                ref[...] = val

        @pl.when(jnp.logical_not(first))
        def _add():
            for ref, val in parts:
                ref[...] = ref[...] + val

    row = lambda b, s: (b * ns + (ns - 1 - s), 0)
    prev = lambda b, s: (jnp.maximum((b * ns + (ns - 1 - s)) * t8 - 1, 0), 0)
    whole = lambda b, s: (0, 0)
    acc_shapes = [(N_ST, GB_U), (N_ST, GB_U), (D_SSM, GB_ST), (D_SSM, GB_ST), (1, N_ST), (1, N_ST), (1, D_SSM)]
    return pl.pallas_call(
        kern, name="ssm_bwd", grid=(B, ns),
        in_specs=[pl.BlockSpec((ts, D_SSM), row), pl.BlockSpec((ts, D_SSM), row),
                  pl.BlockSpec((ts, N_ST), row), pl.BlockSpec((ts, N_ST), row),
                  pl.BlockSpec((8, N_ST), prev), pl.BlockSpec((8, N_ST), prev),
                  pl.BlockSpec((D_SSM, GB_ST), whole), pl.BlockSpec((D_SSM, GB_ST), whole),
                  pl.BlockSpec((N_ST, GB_U), whole), pl.BlockSpec((N_ST, GB_U), whole),
                  pl.BlockSpec((1, D_SSM), whole), pl.BlockSpec((1, N_ST), whole), pl.BlockSpec((1, N_ST), whole)],
        out_specs=[pl.BlockSpec((ts, D_SSM), row)] + [pl.BlockSpec(sh, whole) for sh in acc_shapes],
        out_shape=[jax.ShapeDtypeStruct((T, D_SSM), F32)] + [jax.ShapeDtypeStruct(sh, F32) for sh in acc_shapes],
        scratch_shapes=[pltpu.VMEM((8, N_ST), F32), pltpu.VMEM((8, N_ST), F32),
                        pltpu.VMEM((ts, N_ST), F32), pltpu.VMEM((ts, N_ST), F32)],
        compiler_params=_params(("arbitrary", "arbitrary")),
    )(dy, proj, s_re, s_im, s_re, s_im, b_re, b_im, c_re, c_im, d_vec, a_re, a_im)


def modulation_fwd(c_all, ada_w, ada_b, fada_w, fada_b):
    def kern(c_ref, w_ref, b_ref, fw_ref, fb_ref, cond_ref, mod_ref, fmod_ref):
        cv = c_ref[...]
        cond = (cv * _sigmoid(cv)).astype(BF16)
        cond_ref[...] = cond
        mod_ref[...] = _dot(cond, w_ref[...].astype(BF16)) + b_ref[...]
        fmod_ref[...] = _dot(cond, fw_ref[...].astype(BF16)) + fb_ref[...]

    n = c_all.shape[0]
    return pl.pallas_call(
        kern, name="modulation_fwd",
        out_shape=[jax.ShapeDtypeStruct(c_all.shape, BF16), jax.ShapeDtypeStruct((n, ada_w.shape[1]), F32),
                   jax.ShapeDtypeStruct((n, fada_w.shape[1]), F32)],
        compiler_params=_params(),
    )(c_all, ada_w, ada_b, fada_w, fada_b)


def modulation_bwd(cond_all, dmod, dfmod, dmod_all):
    def kern(c_ref, dm_ref, dfm_ref, all_ref, gw_ref, gfw_ref, gb_ref):
        cond = c_ref[...]
        gw_ref[...] = _dot_tn(cond, dm_ref[...].astype(BF16))
        gfw_ref[...] = _dot_tn(cond, dfm_ref[...].astype(BF16))
        gb_ref[...] = _colsum(all_ref[...])

    d = cond_all.shape[1]
    return pl.pallas_call(
        kern, name="modulation_bwd",
        out_shape=[jax.ShapeDtypeStruct((d, dmod.shape[1]), F32), jax.ShapeDtypeStruct((d, dfmod.shape[1]), F32),
                   jax.ShapeDtypeStruct((1, dmod_all.shape[1]), F32)],
        compiler_params=_params(),
    )(cond_all, dmod, dfmod, dmod_all)


def adamw(name, parts, w, m, v):
    n, R, C = parts.shape
    tr = R
    while n * tr * C * 4 > 4 * 1024 * 1024 and tr % 16 == 0:
        tr //= 2

    def kern(p_ref, w_ref, m_ref, v_ref, g_o, d_o, m_o, v_o):
        g = p_ref[0]
        for i in range(1, n):
            g = g + p_ref[i]
        m_new = ADAM_B1 * m_ref[...] + (1.0 - ADAM_B1) * g
        v_new = ADAM_B2 * v_ref[...] + (1.0 - ADAM_B2) * (g * g)
        m_hat = m_new / (1.0 - ADAM_B1 ** ADAM_STEP)
        v_hat = v_new / (1.0 - ADAM_B2 ** ADAM_STEP)
        g_o[...] = g
        d_o[...] = -ADAM_LR * (m_hat / (jnp.sqrt(v_hat) + ADAM_EPS) + ADAM_WD * w_ref[...])
        m_o[...] = m_new
        v_o[...] = v_new

    blk = pl.BlockSpec((tr, C), lambda i: (i, 0))
    shp = jax.ShapeDtypeStruct((R, C), F32)
    return pl.pallas_call(
        kern, name=name, grid=(R // tr,),
        in_specs=[pl.BlockSpec((n, tr, C), lambda i: (0, i, 0)), blk, blk, blk],
        out_specs=[blk] * 4, out_shape=[shp] * 4,
        compiler_params=_params(("parallel",)),
    )(parts, w, m, v)


def _cols_from_gathered(g):
    return jnp.transpose(g, (1, 0, 2)).reshape(g.shape[1], N_DEV * g.shape[2])


def _cols_to_shards(w):
    k, n8 = w.shape
    return jnp.transpose(w.reshape(k, N_DEV, n8 // N_DEV), (1, 0, 2))


def _pad_w_in(w):
    z = functools.partial(jnp.zeros, dtype=w.dtype)
    k = w.shape[0]
    cut = D_SSM + Q_LORA + KV_LORA
    return jnp.concatenate([w[:, :cut], z((k, 64)), w[:, cut:], z((k, 32))], axis=1)


def _unpad_w_in(w):
    cut = D_SSM + Q_LORA + KV_LORA
    return jnp.concatenate([w[:, :cut], w[:, cut + 64:cut + 96]], axis=1)


def _pad_w_uq(w):
    k = w.shape[0]
    w3 = w.reshape(k, N_HEADS, QK_NOPE + QK_ROPE)
    return jnp.pad(w3, ((0, 0), (0, 0), (0, HEAD_PAD - QK_NOPE - QK_ROPE))).reshape(k, QK_W)


def _unpad_w_uq(w):
    k = w.shape[0]
    return w.reshape(k, N_HEADS, HEAD_PAD)[:, :, :QK_NOPE + QK_ROPE].reshape(k, N_HEADS * (QK_NOPE + QK_ROPE))


def _pad_w_ukv(w):
    k = w.shape[0]
    w3 = w.reshape(k, N_HEADS, QK_NOPE + V_HEAD)
    kpart = jnp.pad(w3[:, :, :QK_NOPE], ((0, 0), (0, 0), (0, HEAD_PAD - QK_NOPE))).reshape(k, QK_W)
    vpart = w3[:, :, QK_NOPE:].reshape(k, V_W)
    return jnp.concatenate([kpart, vpart], axis=1)


def _unpad_w_ukv(w):
    k = w.shape[0]
    kpart = w[:, :QK_W].reshape(k, N_HEADS, HEAD_PAD)[:, :, :QK_NOPE]
    vpart = w[:, QK_W:].reshape(k, N_HEADS, V_HEAD)
    return jnp.concatenate([kpart, vpart], axis=2).reshape(k, N_HEADS * (QK_NOPE + V_HEAD))


def _block_diag(blocks):
    g, r, c = blocks.shape
    b4 = blocks.reshape(N_GB, GB, r, c)
    keep = jnp.eye(GB, dtype=bool)[None, :, None, :, None]
    return jnp.where(keep, b4[:, :, :, None, :], 0).reshape(g * r, GB * c)


def _block_diag_take(stacked, r, c):
    d5 = stacked.reshape(N_GB, GB, r, GB, c)
    idx = jnp.arange(GB)
    return jnp.transpose(d5[:, idx, :, idx, :], (1, 0, 2, 3)).reshape(N_GROUPS, r, c)


SMALL = ["norm1_g", "ssm_lambda_re", "ssm_lambda_im", "ssm_b_re", "ssm_b_im", "ssm_c_re", "ssm_c_im",
         "ssm_d", "ssm_log_dt", "q_norm_g", "kv_norm_g", "ssm_out_g", "attn_out_g", "norm2_g", "final_norm_g"]
BIASES = ["ada_b", "final_ada_b"]
SMALL_COLS = 1024


def _pack_small(values):
    flat = jnp.concatenate([v.reshape(-1) for v in values])
    rows = -(-flat.shape[0] // (8 * SMALL_COLS)) * 8
    return jnp.pad(flat, (0, rows * SMALL_COLS - flat.shape[0])).reshape(rows, SMALL_COLS)


def _unpack_small(packed, like):
    flat = packed.reshape(-1)
    out, off = [], 0
    for ref in like:
        n = int(np.prod(ref.shape))
        out.append(flat[off:off + n].reshape(ref.shape))
        off += n
    return out


def kernel(x, c, positions, ada_w, ada_b, norm1_g, w_in, ssm_lambda_re, ssm_lambda_im, ssm_b_re, ssm_b_im, ssm_c_re, ssm_c_im, ssm_d, ssm_log_dt, w_glu, q_norm_g, w_uq, kv_norm_g, w_ukv, ssm_out_g, attn_out_g, w_out, norm2_g, w_ff1, w_ff2, final_ada_w, final_ada_b, final_norm_g, loss_target, m_ada_w, m_ada_b, m_norm1_g, m_w_in, m_ssm_lambda_re, m_ssm_lambda_im, m_ssm_b_re, m_ssm_b_im, m_ssm_c_re, m_ssm_c_im, m_ssm_d, m_ssm_log_dt, m_w_glu, m_q_norm_g, m_w_uq, m_kv_norm_g, m_w_ukv, m_ssm_out_g, m_attn_out_g, m_w_out, m_norm2_g, m_w_ff1, m_w_ff2, m_final_ada_w, m_final_ada_b, m_final_norm_g, v_ada_w, v_ada_b, v_norm1_g, v_w_in, v_ssm_lambda_re, v_ssm_lambda_im, v_ssm_b_re, v_ssm_b_im, v_ssm_c_re, v_ssm_c_im, v_ssm_d, v_ssm_log_dt, v_w_glu, v_q_norm_g, v_w_uq, v_kv_norm_g, v_w_ukv, v_ssm_out_g, v_attn_out_g, v_w_out, v_norm2_g, v_w_ff1, v_w_ff2, v_final_ada_w, v_final_ada_b, v_final_norm_g):
    given = dict(locals())
    B, S, D = x.shape
    T = B * S
    tm = min(256, S)
    me = 4 * lax.axis_index("x") + 2 * lax.axis_index("y") + lax.axis_index("c")

    big = ["w_in", "w_glu", "w_uq", "w_ukv", "w_out", "w_ff1", "w_ff2"]
    shards = {n: given[n][0] for n in big}
    early, late = ["w_in", "w_uq", "w_ukv"], ["w_glu", "w_out", "w_ff1", "w_ff2"]
    gathered = all_gather("gather_first_weights", [c] + [shards[n].astype(BF16) for n in early])
    c_all = gathered[0].reshape(N_DEV * B, D)
    gw = dict(zip(early, gathered[1:]))
    w_in_p = _pad_w_in(_cols_from_gathered(gw["w_in"]))
    w_uq_p = _pad_w_uq(_cols_from_gathered(gw["w_uq"]))
    w_ukv_p = _pad_w_ukv(_cols_from_gathered(gw["w_ukv"]))

    n_mod = ada_w.shape[2]
    n_fmod = final_ada_w.shape[1]
    ada_b_sh = lax.dynamic_slice_in_dim(ada_b, me * n_mod, n_mod, axis=1)
    fada_b_sh = lax.dynamic_slice_in_dim(final_ada_b[None, :], me * n_fmod, n_fmod, axis=1)
    cond_all, mod_sh, fmod_sh = modulation_fwd(c_all, ada_w[0], ada_b_sh, final_ada_w, fada_b_sh)
    mod_g, fmod_g = all_gather("gather_modulation", [mod_sh, fmod_sh])
    mod_mine = lax.dynamic_slice_in_dim(_cols_from_gathered(mod_g), me * B, B, axis=0)
    fmod_mine = lax.dynamic_slice_in_dim(_cols_from_gathered(fmod_g), me * B, B, axis=0)
    shift1, scale1, gate1, shift2, scale2, gate2 = [mod_mine[:, None, i * D:(i + 1) * D] for i in range(6)]
    fshift, fscale = fmod_mine[:, None, :D], fmod_mine[:, None, D:]

    x2d = x.reshape(T, D)
    tgt2d = loss_target.reshape(T, D)
    pos = positions.astype(F32).reshape(T, 1)
    invf, m_lo, m_hi = _rope_consts()
    rope_mask = m_lo + m_hi

    def fwd_in(rows, bv, vecs, res):
        (xv,), (sc, sh), (g1,), (w,) = rows, bv, vecs, res
        r, xh = _rms_stats(xv)
        h = xh * g1 * (1.0 + sc) + sh
        return [_dot(h.astype(BF16), w[...])], [], []

    (proj,), _, _ = row_call("fwd_in", fwd_in, B, S, tm, [x2d], [scale1, shift1], [norm1_g], [w_in_p],
                             [(IN_PAD, F32)], [], [])

    def fwd_qkv(rows, bv, vecs, res):
        (pr, ps), (gq, gkv, fr, lo, hi), (wq, wkv) = rows, vecs, res
        cs, s_lo, s_hi = _rope_tables(ps, fr, lo, hi)
        _, qh = _rms_stats(pr[:, D_SSM:D_SSM + Q_LORA])
        _, kvh = _rms_stats(pr[:, D_SSM + Q_LORA:D_SSM + Q_LORA + KV_LORA])
        qf = _dot((qh * gq).astype(BF16), wq[...])
        kvf = _dot((kvh * gkv).astype(BF16), wkv[...])
        k_rope = _rope_tile(pr[:, IN_PAD - HEAD_PAD:], cs, s_lo, s_hi, 1.0)
        q_t, k_t = [], []
        for h in range(N_HEADS):
            hs = slice(h * HEAD_PAD, (h + 1) * HEAD_PAD)
            q_t.append(_rope_tile(qf[:, hs], cs, s_lo, s_hi, 1.0))
            k_t.append(kvf[:, hs] + k_rope)
        kf, vf = jnp.concatenate(k_t, axis=1), kvf[:, QK_W:]
        return [jnp.concatenate(q_t, axis=1), kf, vf, kf.T, vf.T], [], []

    ns = S // tm
    col_tile = lambda b, s: (0, b * ns + s)
    (q_r, k_r, v_r, k_t, v_t), _, _ = row_call(
        "fwd_qkv", fwd_qkv, B, S, tm, [proj, pos], [], [q_norm_g, kv_norm_g, invf, m_lo, m_hi],
        [w_uq_p, w_ukv_p],
        [(QK_W, BF16), (QK_W, BF16), (V_W, BF16), ((QK_W, T), BF16, (QK_W, tm), col_tile),
         ((V_W, T), BF16, (V_W, tm), col_tile)], [], [])
    late_gather = Exchange([shards[n].astype(BF16) for n in late], [GATHER] * len(late))
    y_attn, lse, *late_g = attention_fwd(q_r, k_r, v_t, B, S, tm, late_gather)
    gw = dict(zip(late, late_g))
    w_glu_f = _cols_from_gathered(gw["w_glu"])
    w_out_f = gw["w_out"].reshape(-1, D)
    w_ff1_f = _cols_from_gathered(gw["w_ff1"])
    w_ff2_f = gw["w_ff2"].reshape(-1, D)

    rep = lambda a: jnp.repeat(a, GROUP, axis=0)
    lam_rep = [rep(ssm_lambda_re[0]), rep(ssm_lambda_im[0]),
               rep(jnp.broadcast_to(ssm_log_dt[0][:, None], (N_GROUPS, N_STATE)))]
    bt = lambda a: jnp.transpose(a, (0, 2, 1)).reshape(D_SSM, N_STATE)
    reps = lam_rep + [bt(ssm_b_re[0]), bt(ssm_b_im[0])]
    a_re_rep, a_im_rep, bb_re, bb_im = ssm_params_fwd(reps)
    a_re = a_re_rep.reshape(N_GROUPS, GROUP, N_STATE)[:, 0, :].reshape(1, N_ST)
    a_im = a_im_rep.reshape(N_GROUPS, GROUP, N_STATE)[:, 0, :].reshape(1, N_ST)
    bbd_re = _block_diag(bb_re.reshape(N_GROUPS, GROUP, N_STATE)).astype(BF16)
    bbd_im = _block_diag(bb_im.reshape(N_GROUPS, GROUP, N_STATE)).astype(BF16)
    cbd_re = _block_diag(jnp.transpose(ssm_c_re[0], (0, 2, 1))).astype(BF16)
    cbd_im = _block_diag(jnp.transpose(ssm_c_im[0], (0, 2, 1))).astype(BF16)
    d_vec = ssm_d.reshape(1, D_SSM)
    st_re, st_im, y_pre = ssm_fwd(proj, bbd_re, bbd_im, cbd_re, cbd_im, d_vec, a_re, a_im, B, S, tm)

    def fwd_mix(rows, bv, vecs, res):
        (yp, ya, xv), (g1v,), (gso, gao), (wg, wo) = rows, bv, vecs, res
        z = _dot(_gelu(yp).astype(BF16), wg[...])
        y_ssm = z[:, :D_SSM] * _sigmoid(z[:, D_SSM:])
        _, sh = _rms_stats(y_ssm)
        _, ah = _rms_stats(ya)
        o = _dot((sh * gso).astype(BF16), wo[0:D_SSM, :]) + _dot((ah * gao).astype(BF16), wo[D_SSM:, :])
        return [z, o, xv + g1v * o], [], []

    (z, o, x2), _, _ = row_call("fwd_mix", fwd_mix, B, S, tm, [y_pre, y_attn, x2d], [gate1],
                                [ssm_out_g, attn_out_g], [w_glu_f, w_out_f],
                                [(2 * D_SSM, F32), (D, F32), (D, F32)], [], [])

    def final_out(x3, gf, fsc, fsh):
        r3, xh3 = _rms_stats(x3)
        n3 = xh3 * gf
        return r3, xh3, n3, n3 * (1.0 + fsc) + fsh

    def mlp_hidden(xv, g2, sc, sh, w1):
        r2, xh2 = _rms_stats(xv)
        n2 = xh2 * g2
        h2 = n2 * (1.0 + sc) + sh
        return r2, xh2, n2, h2, _dot(h2.astype(BF16), w1[...])

    def fwd_mlp(rows, bv, vecs, res):
        (xv, tg), (sc, sh, g2v, fsc, fsh), (g2, gf), (w1, w2) = rows, bv, vecs, res
        _, _, _, _, a = mlp_hidden(xv, g2, sc, sh, w1)
        ra = jnp.maximum(a, 0.0)
        ff = _dot((ra * ra).astype(BF16), w2[...])
        x3 = xv + g2v * ff
        _, _, _, out = final_out(x3, gf, fsc, fsh)
        err = out - tg
        loss = 0.5 * jnp.sum(jnp.mean(err * err, axis=-1, keepdims=True), axis=0, keepdims=True)
        return [ff, x3], [], [jnp.broadcast_to(loss, (1, 128))]

    fnorm_g = final_norm_g[None, :]
    (ff, x3), _, (loss_acc,) = row_call(
        "fwd_mlp", fwd_mlp, B, S, tm, [x2, tgt2d], [scale2, shift2, gate2, fscale, fshift], [norm2_g, fnorm_g],
        [w_ff1_f, w_ff2_f], [(D, F32), (D, F32)], [], [(1, 128)])
    loss = lax.psum(loss_acc[0, 0], ("x", "y", "c"))

    def bwd_mlp(rows, bv, vecs, res):
        (x3v, tg, x2v, ffv), (sc, sh, g2v, fsc, fsh), (g2, gf), (w1, w2) = rows, bv, vecs, res
        r3, xh3, n3, out = final_out(x3v, gf, fsc, fsh)
        dout = (out - tg) * (1.0 / D)
        dn3 = dout * (1.0 + fsc)
        dx3 = _rms_bwd(dn3 * gf, xh3, r3)
        dff = dx3 * g2v
        r2, xh2, n2, h2, a = mlp_hidden(x2v, g2, sc, sh, w1)
        ra = jnp.maximum(a, 0.0)
        dffb = dff.astype(BF16)
        da = _dot_nt(dffb, w2[...]) * (2.0 * ra)
        dab = da.astype(BF16)
        dh2 = _dot_nt(dab, w1[...])
        dn2 = dh2 * (1.0 + sc)
        dx2 = dx3 + _rms_bwd(dn2 * g2, xh2, r2)
        return ([dx2, h2, ra * ra, dab, dffb],
                [_colsum(dout), _colsum(dout * n3), _colsum(dx3 * ffv), _colsum(dh2), _colsum(dh2 * n2)],
                [_colsum(dn3 * xh3), _colsum(dn2 * xh2)])

    ((dx2, h2b, fb, dab, dffb), (d_fshift, d_fscale, d_gate2, d_shift2, d_scale2), (d_gf, d_g2)) = row_call(
        "bwd_mlp", bwd_mlp, B, S, tm, [x3, tgt2d, x2, ff], [scale2, shift2, gate2, fscale, fshift],
        [norm2_g, fnorm_g], [w_ff1_f, w_ff2_f],
        [(D, F32), (D, BF16), (4 * D, BF16), (4 * D, BF16), (D, BF16)], [D] * 5, [(1, D), (1, D)])
    g_w_ff1 = matmul_tn("grad_w_ff1", h2b, dab)
    g_w_ff2 = matmul_tn("grad_w_ff2", fb, dffb)

    def bwd_mix(rows, bv, vecs, res):
        (dxv, ov, zv, ya, yp), (g1v,), (gso, gao), (wo, wg) = rows, bv, vecs, res
        do = (dxv * g1v).astype(BF16)
        dyn = _dot_nt(do, wo[...])
        z1, sg = zv[:, :D_SSM], _sigmoid(zv[:, D_SSM:])
        y_ssm = z1 * sg
        rs, sh = _rms_stats(y_ssm)
        ra, ah = _rms_stats(ya)
        dyn_s, dyn_a = dyn[:, :D_SSM], dyn[:, D_SSM:]
        dy_ssm = _rms_bwd(dyn_s * gso, sh, rs)
        dy_attn = _rms_bwd(dyn_a * gao, ah, ra)
        dz = jnp.concatenate([dy_ssm * sg, dy_ssm * z1 * sg * (1.0 - sg)], axis=1).astype(BF16)
        dy_pre = _dot_nt(dz, wg[...]) * _gelu_grad(yp)
        yn = jnp.concatenate([sh * gso, ah * gao], axis=1)
        delta = jnp.sum((dy_attn * ya).T.reshape(N_HEADS, V_HEAD, tm), axis=1, keepdims=True)
        return ([do, yn, dz, _gelu(yp), dy_attn, dy_pre, delta], [_colsum(dxv * ov)],
                [_colsum(dyn_s * sh), _colsum(dyn_a * ah)])

    ((dob, ynb, dzb, ygb, dy_attn, dy_pre, delta), (d_gate1,), (d_gso, d_gao)) = row_call(
        "bwd_mix", bwd_mix, B, S, tm, [dx2, o, z, y_attn, y_pre], [gate1], [ssm_out_g, attn_out_g],
        [w_out_f, w_glu_f],
        [(D, BF16), (D, BF16), (2 * D_SSM, BF16), (D_SSM, BF16), (V_W, F32), (D_SSM, F32),
         ((N_HEADS, 1, T), F32, (N_HEADS, 1, tm), lambda b, s: (0, 0, b * ns + s))],
        [D], [(1, D_SSM), (1, V_W)])
    g_w_out = matmul_tn("grad_w_out", ynb, dob)
    g_w_glu = matmul_tn("grad_w_glu", ygb, dzb)

    grads_a = Exchange([g_w_glu, g_w_out.reshape(N_DEV, -1, D), g_w_ff1, g_w_ff2.reshape(N_DEV, -1, D)],
                       [w_glu.shape[2], None, w_ff1.shape[2], None])
    dq_t, dk_r, dv_r, *parts_a = attention_bwd(q_r, k_r, k_t, v_r, dy_attn, lse, delta, B, S, tm, grads_a)
    parts = dict(zip(late, parts_a))
    dq_t = dq_t.reshape(B, QK_W, S)

    def bwd_qkv(rows, bv, vecs, res):
        (dqt, dkv, dvv, pr, ps), (gq, gkv, fr, lo, hi, rmask), (wq, wkv) = rows, vecs, res
        dqv = dqt[0].T
        cs, s_lo, s_hi = _rope_tables(ps, fr, lo, hi)
        dq_t = []
        dk_rope = jnp.zeros((dkv.shape[0], HEAD_PAD), F32)
        for h in range(N_HEADS):
            hs = slice(h * HEAD_PAD, (h + 1) * HEAD_PAD)
            dq_t.append(_rope_tile(dqv[:, hs], cs, s_lo, s_hi, -1.0))
            dk_rope = dk_rope + dkv[:, hs]
        dk_rope = _rope_tile(dk_rope * rmask, cs, s_lo, s_hi, -1.0)
        dqb = jnp.concatenate(dq_t, axis=1).astype(BF16)
        dkvb = jnp.concatenate([dkv, dvv], axis=1).astype(BF16)
        rq, qh = _rms_stats(pr[:, D_SSM:D_SSM + Q_LORA])
        rk, kvh = _rms_stats(pr[:, D_SSM + Q_LORA:D_SSM + Q_LORA + KV_LORA])
        dqn = _dot_nt(dqb, wq[...])
        dkvn = _dot_nt(dkvb, wkv[...])
        dpa = jnp.concatenate([_rms_bwd(dqn * gq, qh, rq), _rms_bwd(dkvn * gkv, kvh, rk), dk_rope], axis=1)
        return [dpa, qh * gq, kvh * gkv, dqb, dkvb], [], [_colsum(dqn * qh), _colsum(dkvn * kvh)]

    ((dpa, qnb, kvnb, dqb, dkvb), _, (d_gq, d_gkv)) = row_call(
        "bwd_qkv", bwd_qkv, B, S, tm,
        [(dq_t, (1, QK_W, tm), lambda b, s: (b, 0, s)), dk_r, dv_r, proj, pos], [],
        [q_norm_g, kv_norm_g, invf, m_lo, m_hi, rope_mask], [w_uq_p, w_ukv_p],
        [(Q_LORA + KV_LORA + HEAD_PAD, F32), (Q_LORA, BF16), (KV_LORA, BF16), (QK_W, BF16), (QK_W + V_W, BF16)],
        [], [(1, Q_LORA), (1, KV_LORA)])
    g_w_uq = _unpad_w_uq(matmul_tn("grad_w_uq", qnb, dqb))
    g_w_ukv = _unpad_w_ukv(matmul_tn("grad_w_ukv", kvnb, dkvb))

    du, dc_re_d, dc_im_d, db_re_d, db_im_d, da_re, da_im, d_d = ssm_bwd(
        dy_pre, proj, st_re, st_im, bbd_re, bbd_im, cbd_re, cbd_im, d_vec, a_re, a_im, B, S, tm)
    place = lambda a: jnp.zeros((N_GROUPS, GROUP, N_STATE), F32).at[:, 0, :].set(
        a.reshape(N_GROUPS, N_STATE)).reshape(D_SSM, N_STATE)
    cots = [place(da_re), place(da_im),
            _block_diag_take(db_re_d, GROUP, N_STATE).reshape(D_SSM, N_STATE),
            _block_diag_take(db_im_d, GROUP, N_STATE).reshape(D_SSM, N_STATE)]
    g_lam_re, g_lam_im, g_log_dt, g_bt_re, g_bt_im = ssm_params_bwd(reps, cots)
    unbt = lambda a: jnp.transpose(a.reshape(N_GROUPS, GROUP, N_STATE), (0, 2, 1))
    g_c_re = jnp.transpose(_block_diag_take(dc_re_d, N_STATE, GROUP), (0, 2, 1))
    g_c_im = jnp.transpose(_block_diag_take(dc_im_d, N_STATE, GROUP), (0, 2, 1))

    def bwd_in(rows, bv, vecs, res):
        (duv, dpav, xv, dx2v), (sc, sh), (g1,), (w,) = rows, bv, vecs, res
        dproj = jnp.concatenate([duv, dpav], axis=1).astype(BF16)
        dh = _dot_nt(dproj, w[...])
        r, xh = _rms_stats(xv)
        n1 = xh * g1
        dn1 = dh * (1.0 + sc)
        dx = dx2v + _rms_bwd(dn1 * g1, xh, r)
        return [dx, n1 * (1.0 + sc) + sh, dproj], [_colsum(dh), _colsum(dh * n1)], [_colsum(dn1 * xh)]

    ((grad_x, h1b, dprojb), (d_shift1, d_scale1), (d_g1,)) = row_call(
        "bwd_in", bwd_in, B, S, tm, [du, dpa, x2d, dx2], [scale1, shift1], [norm1_g], [w_in_p],
        [(D, F32), (D, BF16), (IN_PAD, BF16)], [D, D], [(1, D)])
    g_w_in = _unpad_w_in(matmul_tn("grad_w_in", h1b, dprojb))

    dmod = jnp.concatenate([d_shift1, d_scale1, d_gate1, d_shift2, d_scale2, d_gate2, d_fshift, d_fscale],
                           axis=2).reshape(B, 8 * D)
    small_part = {
        "norm1_g": d_g1, "ssm_lambda_re": g_lam_re,
        "ssm_lambda_im": g_lam_im, "ssm_b_re": unbt(g_bt_re), "ssm_b_im": unbt(g_bt_im), "ssm_c_re": g_c_re,
        "ssm_c_im": g_c_im, "ssm_d": d_d, "ssm_log_dt": g_log_dt, "q_norm_g": d_gq, "kv_norm_g": d_gkv,
        "ssm_out_g": d_gso, "attn_out_g": d_gao, "norm2_g": d_g2, "final_norm_g": d_gf}
    small_packed = _pack_small([small_part[n] for n in SMALL])
    dmod_g, small_g, *parts_b = exchange(
        "exchange_last_grads", [dmod, small_packed, _cols_to_shards(g_w_in), _cols_to_shards(g_w_uq), g_w_ukv],
        [GATHER, GATHER, None, None, w_ukv.shape[2]])
    parts.update(zip(early, parts_b))
    dmod_all = dmod_g.reshape(N_DEV * B, 8 * D)
    dm_sh = lax.dynamic_slice_in_dim(dmod_all[:, :6 * D], me * n_mod, n_mod, axis=1)
    dfm_sh = lax.dynamic_slice_in_dim(dmod_all[:, 6 * D:], me * n_fmod, n_fmod, axis=1)
    g_ada_w, g_fada_w, g_biases = modulation_bwd(cond_all, dm_sh, dfm_sh, dmod_all)
    parts["ada_w"] = g_ada_w[None]
    parts["final_ada_w"] = g_fada_w[None]

    res_g, res_d, res_m, res_v = {}, {}, {}, {}
    for n in ["ada_w"] + big + ["final_ada_w"]:
        w_full = given[n]
        w2 = w_full.reshape(w_full.shape[-2], w_full.shape[-1])
        out = adamw("adamw_" + n, parts[n], w2, given["m_" + n].reshape(w2.shape), given["v_" + n].reshape(w2.shape))
        res_g[n], res_d[n], res_m[n], res_v[n] = [a.reshape(w_full.shape) for a in out]
    small_w = [given[n] for n in SMALL]
    out = adamw("adamw_small", small_g, _pack_small(small_w), _pack_small([given["m_" + n] for n in SMALL]),
                _pack_small([given["v_" + n] for n in SMALL]))
    for res, packed in zip((res_g, res_d, res_m, res_v), out):
        for n, val in zip(SMALL, _unpack_small(packed, small_w)):
            res[n] = val
    bias_w = [given[n] for n in BIASES]
    out = adamw("adamw_biases", g_biases.reshape(1, 8, D), _pack_small(bias_w),
                _pack_small([given["m_" + n] for n in BIASES]), _pack_small([given["v_" + n] for n in BIASES]))
    for res, packed in zip((res_g, res_d, res_m, res_v), out):
        for n, val in zip(BIASES, _unpack_small(packed, bias_w)):
            res[n] = val

    order = ["ada_w", "ada_b", "norm1_g", "w_in", "ssm_lambda_re", "ssm_lambda_im", "ssm_b_re", "ssm_b_im",
             "ssm_c_re", "ssm_c_im", "ssm_d", "ssm_log_dt", "w_glu", "q_norm_g", "w_uq", "kv_norm_g", "w_ukv",
             "ssm_out_g", "attn_out_g", "w_out", "norm2_g", "w_ff1", "w_ff2", "final_ada_w", "final_ada_b",
             "final_norm_g"]
    return (loss, grad_x.reshape(B, S, D), *[res_g[n] for n in order], *[res_d[n] for n in order],
            *[res_m[n] for n in order], *[res_v[n] for n in order])
```

```python
import functools
import math

import numpy as np
import jax
import jax.numpy as jnp
from jax import lax
from jax.experimental import pallas as pl
from jax.experimental.pallas import tpu as pltpu

F32 = jnp.float32
BF16 = jnp.bfloat16

N_DEV = 8
EPS = 1e-6
D_SSM = 512
N_GROUPS = 32
GROUP = 16
N_STATE = 64
N_ST = N_GROUPS * N_STATE
Q_LORA = 384
KV_LORA = 256
QK_NOPE = 64
QK_ROPE = 32
V_HEAD = 64
N_HEADS = 8
HEAD_PAD = 128
QK_W = N_HEADS * HEAD_PAD
V_W = N_HEADS * V_HEAD
IN_COLS = D_SSM + Q_LORA + KV_LORA + QK_ROPE
IN_PAD = D_SSM + Q_LORA + KV_LORA + HEAD_PAD
ROPE_BASE = 10000.0
ATTN_SCALE = (QK_NOPE + QK_ROPE) ** -0.5
NEG = -1e30

ADAM_LR = 0.001
ADAM_B1 = 0.9
ADAM_B2 = 0.999
ADAM_EPS = 1e-08
ADAM_WD = 0.01
ADAM_STEP = 10

VMEM_LIMIT = 56 * 1024 * 1024
MESH_ID = pl.DeviceIdType.MESH


def _dot(a, b):
    return jnp.dot(a, b, preferred_element_type=F32)


def _dot_nt(a, b):
    return lax.dot_general(a, b, (((1,), (1,)), ((), ())), preferred_element_type=F32)


def _dot_tn(a, b):
    return lax.dot_general(a, b, (((0,), (0,)), ((), ())), preferred_element_type=F32)


def _rms_stats(x):
    r = lax.rsqrt(jnp.mean(x * x, axis=-1, keepdims=True) + EPS)
    return r, x * r


def _rms_bwd(dy, xh, r):
    return r * (dy - xh * jnp.mean(dy * xh, axis=-1, keepdims=True))


def _sigmoid(x):
    return 1.0 / (1.0 + jnp.exp(-x))


_GELU_C = math.sqrt(2.0 / math.pi)


def _gelu(x):
    t = jnp.tanh(_GELU_C * (x + 0.044715 * x * x * x))
    return 0.5 * x * (1.0 + t)


def _gelu_grad(x):
    t = jnp.tanh(_GELU_C * (x + 0.044715 * x * x * x))
    return 0.5 * (1.0 + t) + 0.5 * x * (1.0 - t * t) * _GELU_C * (1.0 + 3.0 * 0.044715 * x * x)


def _colsum(a):
    return jnp.sum(a, axis=0, keepdims=True)


def _params(sem=None):
    return pltpu.CompilerParams(dimension_semantics=sem, vmem_limit_bytes=VMEM_LIMIT)


def _mesh_pos():
    x, y, c = lax.axis_index("x"), lax.axis_index("y"), lax.axis_index("c")
    return x, y, c, 4 * x + 2 * y + c


def _peer(x, y, c, k):
    px = 1 - x if k & 4 else x
    py = 1 - y if k & 2 else y
    pc = 1 - c if k & 1 else c
    return (px, py, pc), 4 * px + 2 * py + pc


GATHER = "gather"
ANY_SPEC = pl.BlockSpec(memory_space=pl.ANY)


class Exchange:
    def __init__(self, arrays, kinds):
        self.arrays, self.kinds, self.n = list(arrays), list(kinds), len(arrays)

    def _shard(self, i):
        a, kind = self.arrays[i], self.kinds[i]
        if kind == GATHER:
            return a.shape
        return a.shape[1:] if kind is None else (a.shape[0], kind)

    def out_shapes(self):
        return [jax.ShapeDtypeStruct((N_DEV,) + tuple(self._shard(i)), self.arrays[i].dtype) for i in range(self.n)]

    def scratch(self):
        return [pltpu.SemaphoreType.DMA((self.n, N_DEV - 1)), pltpu.SemaphoreType.DMA((self.n, N_DEV - 1)),
                pltpu.SemaphoreType.DMA((self.n,))]

    def _src(self, ins, i, j):
        kind = self.kinds[i]
        if kind == GATHER:
            return ins[i]
        if kind is None:
            return ins[i].at[j]
        return ins[i].at[:, pl.ds(pl.multiple_of(j * kind, 128), kind)]

    def _copies(self, ins, outs, sems):
        send_sems, recv_sems, loc_sems = sems
        x, y, c, me = _mesh_pos()
        local, sent, received = [], [], []
        for i in range(self.n):
            local.append(pltpu.make_async_copy(self._src(ins, i, me), outs[i].at[me], loc_sems.at[i]))
            for k in range(1, N_DEV):
                peer, pidx = _peer(x, y, c, k)
                pair = dict(send_sem=send_sems.at[i, k - 1], recv_sem=recv_sems.at[i, k - 1], device_id=peer,
                            device_id_type=MESH_ID)
                sent.append(pltpu.make_async_remote_copy(
                    src_ref=self._src(ins, i, pidx), dst_ref=outs[i].at[me], **pair))
                received.append(pltpu.make_async_remote_copy(
                    src_ref=self._src(ins, i, pidx), dst_ref=outs[i].at[pidx], **pair))
        return local, sent, received

    def start(self, ins, outs, sems):
        local, sent, _ = self._copies(ins, outs, sems)
        for cp in local + sent:
            cp.start()

    def wait(self, ins, outs, sems):
        local, sent, received = self._copies(ins, outs, sems)
        for cp in received:
            cp.wait_recv()
        for cp in sent:
            cp.wait_send()
        for cp in local:
            cp.wait()


def exchange(name, arrays, kinds):
    ex = Exchange(arrays, kinds)
    n = ex.n

    def body(*refs):
        ins, outs, sems = refs[:n], refs[n:2 * n], refs[2 * n:]
        ex.start(ins, outs, sems)
        ex.wait(ins, outs, sems)

    return pl.pallas_call(
        body, name=name, out_shape=ex.out_shapes(), in_specs=[ANY_SPEC] * n, out_specs=[ANY_SPEC] * n,
        scratch_shapes=ex.scratch())(*arrays)


def all_gather(name, arrays):
    return exchange(name, arrays, [GATHER] * len(arrays))


def row_call(name, body, B, S, tm, rows, bvecs, vecs, res, row_outs, bacc, gacc):
    ns = S // tm
    T = B * S
    n_r, n_b, n_v, n_w = len(rows), len(bvecs), len(vecs), len(res)
    n_ro, n_ba, n_ga = len(row_outs), len(bacc), len(gacc)

    def kern(*refs):
        i = 0
        r_refs = refs[i:i + n_r]; i += n_r
        b_refs = refs[i:i + n_b]; i += n_b
        v_refs = refs[i:i + n_v]; i += n_v
        w_hbm = refs[i:i + n_w]; i += n_w
        ro_refs = refs[i:i + n_ro]; i += n_ro
        ba_refs = refs[i:i + n_ba]; i += n_ba
        ga_refs = refs[i:i + n_ga]; i += n_ga
        w_vmem = refs[i:i + n_w]
        b = pl.program_id(0)
        s = pl.program_id(1)
        first = jnp.logical_and(b == 0, s == 0)

        if n_w:
            @pl.when(first)
            def _load_weights():
                for h, v in zip(w_hbm, w_vmem):
                    pltpu.sync_copy(h, v)

        ro, ba, ga = body([r[...] for r in r_refs], [r[0] for r in b_refs], [r[...] for r in v_refs],
                          list(w_vmem))
        for ref, val in zip(ro_refs, ro):
            ref[...] = val.astype(ref.dtype)

        def accumulate(ref, val, is_first, idx):
            @pl.when(is_first)
            def _set():
                ref[idx] = val

            @pl.when(jnp.logical_not(is_first))
            def _add():
                ref[idx] = ref[idx] + val

        for ref, val in zip(ba_refs, ba):
            accumulate(ref, val, s == 0, 0)
        for ref, val in zip(ga_refs, ga):
            accumulate(ref, val, first, Ellipsis)

    row_arrays = [r[0] if isinstance(r, tuple) else r for r in rows]
    row_in_specs = [pl.BlockSpec(r[1], r[2]) if isinstance(r, tuple)
                    else pl.BlockSpec((tm, r.shape[1]), lambda b, s: (b * ns + s, 0)) for r in rows]
    ro_shapes = [jax.ShapeDtypeStruct(tuple(o[0]), o[1]) if len(o) == 4 else jax.ShapeDtypeStruct((T, o[0]), o[1])
                 for o in row_outs]
    ro_specs = [pl.BlockSpec(o[2], o[3]) if len(o) == 4 else pl.BlockSpec((tm, o[0]), lambda b, s: (b * ns + s, 0))
                for o in row_outs]
    in_specs = (row_in_specs
                + [pl.BlockSpec((1, 1, v.shape[2]), lambda b, s: (b, 0, 0)) for v in bvecs]
                + [pl.BlockSpec(v.shape, lambda b, s, nd=v.ndim: (0,) * nd) for v in vecs]
                + [pl.BlockSpec(memory_space=pl.ANY) for _ in res])
    out_shape = (ro_shapes
                 + [jax.ShapeDtypeStruct((B, 1, w), F32) for w in bacc]
                 + [jax.ShapeDtypeStruct(tuple(sh), F32) for sh in gacc])
    out_specs = (ro_specs
                 + [pl.BlockSpec((1, 1, w), lambda b, s: (b, 0, 0)) for w in bacc]
                 + [pl.BlockSpec(tuple(sh), lambda b, s, nd=len(sh): (0,) * nd) for sh in gacc])
    outs = pl.pallas_call(
        kern, name=name, grid=(B, ns), in_specs=in_specs, out_specs=out_specs, out_shape=out_shape,
        scratch_shapes=[pltpu.VMEM(w.shape, w.dtype) for w in res],
        compiler_params=_params(("arbitrary", "arbitrary")),
    )(*row_arrays, *bvecs, *vecs, *res)
    return outs[:n_ro], outs[n_ro:n_ro + n_ba], outs[n_ro + n_ba:]


def matmul_tn(name, a, b):
    T, K = a.shape
    N = b.shape[1]
    tk = min(K, 512)
    tt = min(T, 512)
    nt = T // tt

    def kern(a_ref, b_ref, o_ref, acc_ref):
        t = pl.program_id(1)
        part = _dot_tn(a_ref[...], b_ref[...])

        @pl.when(t == 0)
        def _set():
            acc_ref[...] = part

        @pl.when(t > 0)
        def _add():
            acc_ref[...] = acc_ref[...] + part

        @pl.when(t == nt - 1)
        def _write():
            o_ref[...] = acc_ref[...].astype(o_ref.dtype)

    return pl.pallas_call(
        kern, name=name, grid=(K // tk, nt),
        in_specs=[pl.BlockSpec((tt, tk), lambda i, t: (t, i)), pl.BlockSpec((tt, N), lambda i, t: (t, 0))],
        out_specs=pl.BlockSpec((tk, N), lambda i, t: (i, 0)),
        out_shape=jax.ShapeDtypeStruct((K, N), BF16),
        scratch_shapes=[pltpu.VMEM((tk, N), F32)],
        compiler_params=_params(("parallel", "arbitrary")),
    )(a, b)


def _rope_consts():
    inv_freq = ROPE_BASE ** (-jnp.arange(0, QK_ROPE, 2, dtype=F32) / QK_ROPE)
    zeros64 = jnp.zeros((64,), F32)
    zeros32 = jnp.zeros((32,), F32)
    zeros16 = jnp.zeros((16,), F32)
    ones16 = jnp.ones((16,), F32)
    invf = jnp.concatenate([zeros64, inv_freq, inv_freq, zeros32])[None, :]
    m_lo = jnp.concatenate([zeros64, ones16, zeros16, zeros32])[None, :]
    m_hi = jnp.concatenate([zeros64, zeros16, ones16, zeros32])[None, :]
    return invf, m_lo, m_hi


def _rope_tables(pos, invf, m_lo, m_hi):
    ang = pos * invf
    return jnp.cos(ang), jnp.sin(ang) * m_lo, jnp.sin(ang) * m_hi


def _rope_tile(t, cs, s_lo, s_hi, sign):
    up = pltpu.roll(t, HEAD_PAD - 16, axis=1)
    down = pltpu.roll(t, 16, axis=1)
    return t * cs + sign * (down * s_hi - up * s_lo)


def _heads(ref, width):
    return jnp.stack([ref[:, h * width:(h + 1) * width] for h in range(N_HEADS)], axis=0)


def _bmm(a, b, ca, cb):
    return lax.dot_general(a, b, (((ca,), (cb,)), ((0,), (0,))), preferred_element_type=F32)


def _scores_t(k_ref, q_ref, masked, tq):
    st = _bmm(_heads(k_ref, HEAD_PAD), _heads(q_ref, HEAD_PAD), 2, 2) * ATTN_SCALE
    if masked:
        key = lax.broadcasted_iota(jnp.int32, (tq, tq), 0)
        qry = lax.broadcasted_iota(jnp.int32, (tq, tq), 1)
        st = jnp.where((key <= qry)[None], st, NEG)
    return st


def attention_fwd(q, k, vt, B, S, tq, side):
    nq = S // tq
    ns = side.n
    pairs = [(i, j) for i in range(nq) for j in range(i + 1)]
    n_pairs = len(pairs)
    q_tab = jnp.asarray([p[0] for p in pairs], jnp.int32)
    k_tab = jnp.asarray([p[1] for p in pairs], jnp.int32)

    def kern(q_tab_ref, k_tab_ref, *refs):
        q_ref, k_ref, vt_ref = refs[:3]
        side_in = refs[3:3 + ns]
        o_ref, lse_ref = refs[3 + ns:5 + ns]
        side_out = refs[5 + ns:5 + 2 * ns]
        m_scr, l_scr, acc_scr = refs[5 + 2 * ns:8 + 2 * ns]
        sems = refs[8 + 2 * ns:]
        b, t = pl.program_id(0), pl.program_id(1)
        qi, ki = q_tab_ref[t], k_tab_ref[t]

        @pl.when(jnp.logical_and(b == 0, t == 0))
        def _start_side():
            side.start(side_in, side_out, sems)

        @pl.when(ki == 0)
        def _init():
            m_scr[...] = jnp.full(m_scr.shape, NEG, F32)
            l_scr[...] = jnp.zeros(l_scr.shape, F32)
            acc_scr[...] = jnp.zeros(acc_scr.shape, F32)

        def block(masked):
            st = _scores_t(k_ref, q_ref, masked, tq)
            m_prev = m_scr[...]
            m_new = jnp.maximum(m_prev, jnp.max(st, axis=1, keepdims=True))
            alpha = jnp.exp(m_prev - m_new)
            p = jnp.exp(st - m_new)
            l_scr[...] = alpha * l_scr[...] + jnp.sum(p, axis=1, keepdims=True)
            vt3 = vt_ref[...].reshape(N_HEADS, V_HEAD, tq)
            acc_scr[...] = alpha * acc_scr[...] + _bmm(vt3, p.astype(BF16), 2, 1)
            m_scr[...] = m_new

        @pl.when(ki < qi)
        def _full():
            block(False)

        @pl.when(ki == qi)
        def _diagonal():
            block(True)
            ot = acc_scr[...] / l_scr[...]
            for h in range(N_HEADS):
                o_ref[:, h * V_HEAD:(h + 1) * V_HEAD] = ot[h].T
            lse_ref[...] = m_scr[...] + jnp.log(l_scr[...])

        @pl.when(jnp.logical_and(b == B - 1, t == n_pairs - 1))
        def _wait_side():
            side.wait(side_in, side_out, sems)

    T = B * S
    grid_spec = pltpu.PrefetchScalarGridSpec(
        num_scalar_prefetch=2, grid=(B, n_pairs),
        in_specs=[pl.BlockSpec((tq, QK_W), lambda b, t, qt, kt: (b * nq + qt[t], 0)),
                  pl.BlockSpec((tq, QK_W), lambda b, t, qt, kt: (b * nq + kt[t], 0)),
                  pl.BlockSpec((V_W, tq), lambda b, t, qt, kt: (0, b * nq + kt[t]))] + [ANY_SPEC] * ns,
        out_specs=[pl.BlockSpec((tq, V_W), lambda b, t, qt, kt: (b * nq + qt[t], 0)),
                   pl.BlockSpec((N_HEADS, 1, tq), lambda b, t, qt, kt: (0, 0, b * nq + qt[t]))] + [ANY_SPEC] * ns,
        scratch_shapes=[pltpu.VMEM((N_HEADS, 1, tq), F32), pltpu.VMEM((N_HEADS, 1, tq), F32),
                        pltpu.VMEM((N_HEADS, V_HEAD, tq), F32)] + side.scratch())
    return pl.pallas_call(
        kern, name="attention_fwd", grid_spec=grid_spec,
        out_shape=[jax.ShapeDtypeStruct((T, V_W), F32), jax.ShapeDtypeStruct((N_HEADS, 1, T), F32)]
        + side.out_shapes(),
        compiler_params=_params(("arbitrary", "arbitrary")),
    )(q_tab, k_tab, q, k, vt, *side.arrays)


def attention_bwd(q, k, kt, v, do, lse, delta, B, S, tq, side):
    nq = S // tq
    ns = side.n
    pairs = [(j, i) for j in range(nq) for i in range(j, nq)]
    n_pairs = len(pairs)
    k_tab = jnp.asarray([p[0] for p in pairs], jnp.int32)
    q_tab = jnp.asarray([p[1] for p in pairs], jnp.int32)

    def kern(k_tab_ref, q_tab_ref, *refs):
        q_ref, k_ref, kt_ref, v_ref, do_ref, lse_ref, delta_ref = refs[:7]
        side_in = refs[7:7 + ns]
        dq_hbm, dk_ref, dv_ref = refs[7 + ns:10 + ns]
        side_out = refs[10 + ns:10 + 2 * ns]
        dq_acc, dk_acc, dv_acc = refs[10 + 2 * ns:13 + 2 * ns]
        sems = refs[13 + 2 * ns:]
        b, t = pl.program_id(0), pl.program_id(1)
        kj, qi = k_tab_ref[t], q_tab_ref[t]

        @pl.when(jnp.logical_and(b == 0, t == 0))
        def _start_side():
            side.start(side_in, side_out, sems)

        @pl.when(t == 0)
        def _zero_dq():
            dq_acc[...] = jnp.zeros(dq_acc.shape, F32)

        @pl.when(qi == kj)
        def _zero_dkv():
            dk_acc[...] = jnp.zeros(dk_acc.shape, F32)
            dv_acc[...] = jnp.zeros(dv_acc.shape, F32)

        def block(masked):
            pt = jnp.exp(_scores_t(k_ref, q_ref, masked, tq) - lse_ref[...])
            do3 = _heads(do_ref, V_HEAD).astype(BF16)
            dv_acc[...] = dv_acc[...] + _bmm(pt.astype(BF16), do3, 2, 1)
            dpt = _bmm(_heads(v_ref, V_HEAD), do3, 2, 2)
            dst = (pt * (dpt - delta_ref[...]) * ATTN_SCALE).astype(BF16)
            dk_acc[...] = dk_acc[...] + _bmm(dst, _heads(q_ref, HEAD_PAD), 2, 1)
            q_cols = pl.ds(pl.multiple_of(qi * tq, tq), tq)
            kt3 = kt_ref[...].reshape(N_HEADS, HEAD_PAD, tq)
            dq_acc[:, :, q_cols] = dq_acc[:, :, q_cols] + _bmm(kt3, dst, 2, 1)

        @pl.when(qi > kj)
        def _full():
            block(False)

        @pl.when(qi == kj)
        def _diagonal():
            block(True)

        @pl.when(qi == nq - 1)
        def _write_dkv():
            for h in range(N_HEADS):
                dk_ref[:, h * HEAD_PAD:(h + 1) * HEAD_PAD] = dk_acc[h]
                dv_ref[:, h * V_HEAD:(h + 1) * V_HEAD] = dv_acc[h]

        @pl.when(t == n_pairs - 1)
        def _write_dq():
            pltpu.sync_copy(dq_acc, dq_hbm.at[b])

        @pl.when(jnp.logical_and(b == B - 1, t == n_pairs - 1))
        def _wait_side():
            side.wait(side_in, side_out, sems)

    T = B * S
    qrow = lambda b, t, kt, qt: (b * nq + qt[t], 0)
    qcol3 = lambda b, t, kt, qt: (0, 0, b * nq + qt[t])
    krow = lambda b, t, kt, qt: (b * nq + kt[t], 0)
    grid_spec = pltpu.PrefetchScalarGridSpec(
        num_scalar_prefetch=2, grid=(B, n_pairs),
        in_specs=[pl.BlockSpec((tq, QK_W), qrow), pl.BlockSpec((tq, QK_W), krow),
                  pl.BlockSpec((QK_W, tq), lambda b, t, kt, qt: (0, b * nq + kt[t])), pl.BlockSpec((tq, V_W), krow),
                  pl.BlockSpec((tq, V_W), qrow), pl.BlockSpec((N_HEADS, 1, tq), qcol3),
                  pl.BlockSpec((N_HEADS, 1, tq), qcol3)] + [ANY_SPEC] * ns,
        out_specs=[ANY_SPEC, pl.BlockSpec((tq, QK_W), krow), pl.BlockSpec((tq, V_W), krow)] + [ANY_SPEC] * ns,
        scratch_shapes=[pltpu.VMEM((N_HEADS, HEAD_PAD, S), F32), pltpu.VMEM((N_HEADS, tq, HEAD_PAD), F32),
                        pltpu.VMEM((N_HEADS, tq, V_HEAD), F32)] + side.scratch())
    return pl.pallas_call(
        kern, name="attention_bwd", grid_spec=grid_spec,
        out_shape=[jax.ShapeDtypeStruct((B, N_HEADS, HEAD_PAD, S), F32), jax.ShapeDtypeStruct((T, QK_W), F32),
                   jax.ShapeDtypeStruct((T, V_W), F32)] + side.out_shapes(),
        compiler_params=_params(("arbitrary", "arbitrary")),
    )(k_tab, q_tab, q, k, kt, v, do, lse, delta, *side.arrays)


GB = 8
N_GB = N_GROUPS // GB
GB_U = GB * GROUP
GB_ST = GB * N_STATE


def _gb_u(j):
    return slice(j * GB_U, (j + 1) * GB_U)


def _gb_s(j):
    return slice(j * GB_ST, (j + 1) * GB_ST)


def ssm_param_fn(lr, li, ld, br, bi):
    dt = jnp.exp(ld)
    er = jnp.exp(lr * dt)
    ar = er * jnp.cos(li * dt)
    ai = er * jnp.sin(li * dt)
    nr = ar - 1.0
    den = lr * lr + li * li
    cr = (nr * lr + ai * li) / den
    ci = (ai * lr - nr * li) / den
    return ar, ai, cr * br - ci * bi, cr * bi + ci * br


def ssm_params_fwd(reps):
    def kern(lr, li, ld, br, bi, ar_o, ai_o, bbr_o, bbi_o):
        ar, ai, bbr, bbi = ssm_param_fn(lr[...], li[...], ld[...], br[...], bi[...])
        ar_o[...] = ar
        ai_o[...] = ai
        bbr_o[...] = bbr
        bbi_o[...] = bbi

    shp = jax.ShapeDtypeStruct(reps[0].shape, F32)
    return pl.pallas_call(kern, name="ssm_params_fwd", out_shape=[shp] * 4)(*reps)


def ssm_params_bwd(reps, cots):
    rows = reps[0].shape[0]

    def kern(lr, li, ld, br, bi, d_ar, d_ai, d_bbr, d_bbi, dlr_o, dli_o, dld_o, dbr_o, dbi_o):
        _, vjp = jax.vjp(ssm_param_fn, lr[...], li[...], ld[...], br[...], bi[...])
        dlr, dli, dld, dbr, dbi = vjp((d_ar[...], d_ai[...], d_bbr[...], d_bbi[...]))
        dlr_o[...] = jnp.sum(dlr.reshape(N_GROUPS, GROUP, N_STATE), axis=1)
        dli_o[...] = jnp.sum(dli.reshape(N_GROUPS, GROUP, N_STATE), axis=1)
        dld_o[...] = jnp.sum(jnp.sum(dld.reshape(N_GROUPS, GROUP, N_STATE), axis=1), axis=-1, keepdims=True)
        dbr_o[...] = dbr
        dbi_o[...] = dbi

    g = jax.ShapeDtypeStruct((N_GROUPS, N_STATE), F32)
    full = jax.ShapeDtypeStruct((rows, N_STATE), F32)
    return pl.pallas_call(
        kern, name="ssm_params_bwd",
        out_shape=[g, g, jax.ShapeDtypeStruct((N_GROUPS, 1), F32), full, full])(*reps, *cots)


CHUNK = 32


LANES = 128
N_LC = N_ST // LANES
LC_PER_GB = GB_ST // LANES


def _put_gb(ref3, j, val):
    for q in range(LC_PER_GB):
        ref3[j * LC_PER_GB + q] = val[:, q * LANES:(q + 1) * LANES]


def _get_gb(x3, j):
    return jnp.concatenate([x3[j * LC_PER_GB + q] for q in range(LC_PER_GB)], axis=1)


def _power_table(pw_re, pw_im, ar, ai):
    pw_re[:, 0:1, :] = ar
    pw_im[:, 0:1, :] = ai

    def step(k, carry):
        pr, pi = carry
        nr = pr * ar - pi * ai
        ni = pr * ai + pi * ar
        pw_re[:, pl.ds(k, 1), :] = nr
        pw_im[:, pl.ds(k, 1), :] = ni
        return nr, ni

    lax.fori_loop(1, CHUNK, step, (ar, ai))


def _chunk_scan(src, dst, pw, car, edge, ar, ai, ts, reverse):
    n_ch = ts // CHUNK
    sgn = -1.0 if reverse else 1.0
    ai = sgn * ai
    (s_re, s_im), (d_re, d_im), (p_re, p_im), (c_re, c_im), (e_re, e_im) = src, dst, pw, car, edge
    rows = lambda i: pl.ds(i, n_ch, stride=CHUNK)

    def local(k, carry):
        xr, xi = carry
        r = rows(CHUNK - 1 - k if reverse else k)
        nr = ar * xr - ai * xi + s_re[:, r, :]
        ni = ar * xi + ai * xr + s_im[:, r, :]
        d_re[:, r, :] = nr
        d_im[:, r, :] = ni
        return nr, ni

    zero = jnp.zeros((N_LC, n_ch, LANES), F32)
    end_r, end_i = lax.fori_loop(0, CHUNK, local, (zero, zero))

    a_re_n, a_im_n = p_re[:, CHUNK - 1:CHUNK, :], sgn * p_im[:, CHUNK - 1:CHUNK, :]
    xr, xi = c_re[:, 0:1, :], c_im[:, 0:1, :]
    for c in (range(n_ch - 1, -1, -1) if reverse else range(n_ch)):
        e_re[:, c:c + 1, :] = xr
        e_im[:, c:c + 1, :] = xi
        lr, li = end_r[:, c:c + 1, :], end_i[:, c:c + 1, :]
        xr, xi = a_re_n * xr - a_im_n * xi + lr, a_re_n * xi + a_im_n * xr + li
    c_re[:, 0:1, :] = xr
    c_im[:, 0:1, :] = xi
    er, ei = e_re[...], e_im[...]

    def fix(i, carry):
        k = CHUNK - 1 - i if reverse else i
        pr, pi = p_re[:, pl.ds(k, 1), :], sgn * p_im[:, pl.ds(k, 1), :]
        r = rows(i)
        d_re[:, r, :] = d_re[:, r, :] + pr * er - pi * ei
        d_im[:, r, :] = d_im[:, r, :] + pr * ei + pi * er
        return carry

    lax.fori_loop(0, CHUNK, fix, 0)


def _ssm_scratch(ts, n_tiles_of_states):
    buf = lambda rows: pltpu.VMEM((N_LC, rows, LANES), F32)
    return [buf(8)] * 2 + [buf(ts)] * (2 * n_tiles_of_states) + [buf(CHUNK)] * 2 + [buf(ts // CHUNK)] * 2


def ssm_fwd(proj, b_re, b_im, c_re, c_im, d_vec, a_re, a_im, B, S, ts):
    ns = S // ts
    T = B * S

    def kern(u_ref, bre_ref, bim_ref, cre_ref, cim_ref, d_ref, are_ref, aim_ref,
             y_ref, cin_re_ref, cin_im_ref, car_re, car_im, x_re, x_im, pw_re, pw_im, e_re, e_im):
        s = pl.program_id(1)
        ar, ai = are_ref[...], aim_ref[...]

        @pl.when(s == 0)
        def _reset():
            car_re[...] = jnp.zeros(car_re.shape, F32)
            car_im[...] = jnp.zeros(car_im.shape, F32)
            _power_table(pw_re, pw_im, ar, ai)

        cin_re_ref[0] = car_re[...]
        cin_im_ref[0] = car_im[...]
        u = u_ref[...]
        ub = u.astype(BF16)
        for j in range(N_GB):
            _put_gb(x_re, j, _dot(ub[:, _gb_u(j)], bre_ref[_gb_u(j), :]))
            _put_gb(x_im, j, _dot(ub[:, _gb_u(j)], bim_ref[_gb_u(j), :]))
        x = (x_re, x_im)
        _chunk_scan(x, x, (pw_re, pw_im), (car_re, car_im), (e_re, e_im), ar, ai, ts, False)
        y = [_dot(_get_gb(x_re, j).astype(BF16), cre_ref[_gb_s(j), :])
             - _dot(_get_gb(x_im, j).astype(BF16), cim_ref[_gb_s(j), :]) for j in range(N_GB)]
        y_ref[...] = jnp.concatenate(y, axis=1) + d_ref[...] * u

    row = lambda b, s: (b * ns + s, 0)
    tile4 = lambda b, s: (b * ns + s, 0, 0, 0)
    whole = lambda b, s: (0, 0)
    whole3 = lambda b, s: (0, 0, 0)
    return pl.pallas_call(
        kern, name="ssm_fwd", grid=(B, ns),
        in_specs=[pl.BlockSpec((ts, D_SSM), row),
                  pl.BlockSpec((D_SSM, GB_ST), whole), pl.BlockSpec((D_SSM, GB_ST), whole),
                  pl.BlockSpec((N_ST, GB_U), whole), pl.BlockSpec((N_ST, GB_U), whole),
                  pl.BlockSpec((1, D_SSM), whole), pl.BlockSpec((N_LC, 1, LANES), whole3),
                  pl.BlockSpec((N_LC, 1, LANES), whole3)],
        out_specs=[pl.BlockSpec((ts, D_SSM), row), pl.BlockSpec((1, N_LC, 8, LANES), tile4),
                   pl.BlockSpec((1, N_LC, 8, LANES), tile4)],
        out_shape=[jax.ShapeDtypeStruct((T, D_SSM), F32), jax.ShapeDtypeStruct((B * ns, N_LC, 8, LANES), F32),
                   jax.ShapeDtypeStruct((B * ns, N_LC, 8, LANES), F32)],
        scratch_shapes=_ssm_scratch(ts, 1),
        compiler_params=_params(("arbitrary", "arbitrary")),
    )(proj, b_re, b_im, c_re, c_im, d_vec, a_re, a_im)


def ssm_bwd(dy, proj, cin_re, cin_im, b_re, b_im, c_re, c_im, d_vec, a_re, a_im, B, S, ts):
    ns = S // ts
    T = B * S

    def kern(dy_ref, u_ref, cin_re_ref, cin_im_ref, bre_ref, bim_ref, cre_ref, cim_ref,
             d_ref, are_ref, aim_ref,
             du_ref, dcre_ref, dcim_ref, dbre_ref, dbim_ref, dare_ref, daim_ref, dd_ref,
             car_re, car_im, x_re, x_im, g_re, g_im, pw_re, pw_im, e_re, e_im, fcar_re, fcar_im):
        b, s = pl.program_id(0), pl.program_id(1)
        first = jnp.logical_and(b == 0, s == 0)
        ar, ai = are_ref[...], aim_ref[...]

        @pl.when(s == 0)
        def _reset():
            car_re[...] = jnp.zeros(car_re.shape, F32)
            car_im[...] = jnp.zeros(car_im.shape, F32)
            _power_table(pw_re, pw_im, ar, ai)

        dyv = dy_ref[...]
        dyb = dyv.astype(BF16)
        u = u_ref[...]
        ub = u.astype(BF16)
        for j in range(N_GB):
            _put_gb(x_re, j, _dot(ub[:, _gb_u(j)], bre_ref[_gb_u(j), :]))
            _put_gb(x_im, j, _dot(ub[:, _gb_u(j)], bim_ref[_gb_u(j), :]))
            _put_gb(g_re, j, _dot_nt(dyb[:, _gb_u(j)], cre_ref[_gb_s(j), :]))
            _put_gb(g_im, j, -_dot_nt(dyb[:, _gb_u(j)], cim_ref[_gb_s(j), :]))
        fcar_re[...] = cin_re_ref[0]
        fcar_im[...] = cin_im_ref[0]
        pw, edge = (pw_re, pw_im), (e_re, e_im)
        _chunk_scan((x_re, x_im), (x_re, x_im), pw, (fcar_re, fcar_im), edge, ar, ai, ts, False)
        _chunk_scan((g_re, g_im), (g_re, g_im), pw, (car_re, car_im), edge, ar, ai, ts, True)

        gr, gi = g_re[...], g_im[...]
        sr, si = x_re[...], x_im[...]
        row_id = lax.broadcasted_iota(jnp.int32, (N_LC, ts, LANES), 1)
        spr = jnp.where(row_id == 0, cin_re_ref[0][:, 0:1, :], pltpu.roll(sr, 1, axis=1))
        spi = jnp.where(row_id == 0, cin_im_ref[0][:, 0:1, :], pltpu.roll(si, 1, axis=1))
        grb, gib = gr.astype(BF16), gi.astype(BF16)
        srb, sib = sr.astype(BF16), si.astype(BF16)
        cat0 = lambda f: jnp.concatenate([f(j) for j in range(N_GB)], axis=0)
        cat1 = lambda f: jnp.concatenate([f(j) for j in range(N_GB)], axis=1)
        parts = [
            (dcre_ref, cat0(lambda j: _dot_tn(_get_gb(srb, j), dyb[:, _gb_u(j)]))),
            (dcim_ref, cat0(lambda j: -_dot_tn(_get_gb(sib, j), dyb[:, _gb_u(j)]))),
            (dbre_ref, cat0(lambda j: _dot_tn(ub[:, _gb_u(j)], _get_gb(grb, j)))),
            (dbim_ref, cat0(lambda j: _dot_tn(ub[:, _gb_u(j)], _get_gb(gib, j)))),
            (dare_ref, jnp.sum(gr * spr + gi * spi, axis=1, keepdims=True)),
            (daim_ref, jnp.sum(gi * spr - gr * spi, axis=1, keepdims=True)),
            (dd_ref, _colsum(dyv * u)),
        ]
        du = cat1(lambda j: _dot_nt(_get_gb(grb, j), bre_ref[_gb_u(j), :])
                  + _dot_nt(_get_gb(gib, j), bim_ref[_gb_u(j), :]))
        du_ref[...] = du + d_ref[...] * dyv

        @pl.when(first)
        def _set():
            for ref, val in parts:
                ref[...] = val

        @pl.when(jnp.logical_not(first))
        def _add():
            for ref, val in parts:
                ref[...] = ref[...] + val

    row = lambda b, s: (b * ns + (ns - 1 - s), 0)
    tile4 = lambda b, s: (b * ns + (ns - 1 - s), 0, 0, 0)
    whole = lambda b, s: (0, 0)
    whole3 = lambda b, s: (0, 0, 0)
    st3 = (N_LC, 1, LANES)
    acc_shapes = [(N_ST, GB_U), (N_ST, GB_U), (D_SSM, GB_ST), (D_SSM, GB_ST), st3, st3, (1, D_SSM)]
    return pl.pallas_call(
        kern, name="ssm_bwd", grid=(B, ns),
        in_specs=[pl.BlockSpec((ts, D_SSM), row), pl.BlockSpec((ts, D_SSM), row),
                  pl.BlockSpec((1, N_LC, 8, LANES), tile4), pl.BlockSpec((1, N_LC, 8, LANES), tile4),
                  pl.BlockSpec((D_SSM, GB_ST), whole), pl.BlockSpec((D_SSM, GB_ST), whole),
                  pl.BlockSpec((N_ST, GB_U), whole), pl.BlockSpec((N_ST, GB_U), whole),
                  pl.BlockSpec((1, D_SSM), whole), pl.BlockSpec(st3, whole3), pl.BlockSpec(st3, whole3)],
        out_specs=[pl.BlockSpec((ts, D_SSM), row)]
        + [pl.BlockSpec(sh, whole3 if len(sh) == 3 else whole) for sh in acc_shapes],
        out_shape=[jax.ShapeDtypeStruct((T, D_SSM), F32)] + [jax.ShapeDtypeStruct(sh, F32) for sh in acc_shapes],
        scratch_shapes=_ssm_scratch(ts, 2) + [pltpu.VMEM((N_LC, 8, LANES), F32)] * 2,
        compiler_params=_params(("arbitrary", "arbitrary")),
    )(dy, proj, cin_re, cin_im, b_re, b_im, c_re, c_im, d_vec, a_re, a_im)


def modulation_fwd(c_all, ada_w, ada_b, fada_w, fada_b):
    def kern(c_ref, w_ref, b_ref, fw_ref, fb_ref, cond_ref, mod_ref, fmod_ref):
        cv = c_ref[...]
        cond = (cv * _sigmoid(cv)).astype(BF16)
        cond_ref[...] = cond
        mod_ref[...] = _dot(cond, w_ref[...].astype(BF16)) + b_ref[...]
        fmod_ref[...] = _dot(cond, fw_ref[...].astype(BF16)) + fb_ref[...]

    n = c_all.shape[0]
    return pl.pallas_call(
        kern, name="modulation_fwd",
        out_shape=[jax.ShapeDtypeStruct(c_all.shape, BF16), jax.ShapeDtypeStruct((n, ada_w.shape[1]), F32),
                   jax.ShapeDtypeStruct((n, fada_w.shape[1]), F32)],
        compiler_params=_params(),
    )(c_all, ada_w, ada_b, fada_w, fada_b)


def modulation_bwd(cond_all, dmod, dfmod, dmod_all):
    def kern(c_ref, dm_ref, dfm_ref, all_ref, gw_ref, gfw_ref, gb_ref):
        cond = c_ref[...]
        gw_ref[...] = _dot_tn(cond, dm_ref[...].astype(BF16))
        gfw_ref[...] = _dot_tn(cond, dfm_ref[...].astype(BF16))
        gb_ref[...] = _colsum(all_ref[...])

    d = cond_all.shape[1]
    return pl.pallas_call(
        kern, name="modulation_bwd",
        out_shape=[jax.ShapeDtypeStruct((d, dmod.shape[1]), F32), jax.ShapeDtypeStruct((d, dfmod.shape[1]), F32),
                   jax.ShapeDtypeStruct((1, dmod_all.shape[1]), F32)],
        compiler_params=_params(),
    )(cond_all, dmod, dfmod, dmod_all)


def adamw(name, parts, w, m, v):
    n, R, C = parts.shape
    tr = R
    while n * tr * C * 4 > 4 * 1024 * 1024 and tr % 16 == 0:
        tr //= 2

    def kern(p_ref, w_ref, m_ref, v_ref, g_o, d_o, m_o, v_o):
        g = p_ref[0].astype(F32)
        for i in range(1, n):
            g = g + p_ref[i].astype(F32)
        m_new = ADAM_B1 * m_ref[...] + (1.0 - ADAM_B1) * g
        v_new = ADAM_B2 * v_ref[...] + (1.0 - ADAM_B2) * (g * g)
        m_hat = m_new / (1.0 - ADAM_B1 ** ADAM_STEP)
        v_hat = v_new / (1.0 - ADAM_B2 ** ADAM_STEP)
        g_o[...] = g
        d_o[...] = -ADAM_LR * (m_hat / (jnp.sqrt(v_hat) + ADAM_EPS) + ADAM_WD * w_ref[...])
        m_o[...] = m_new
        v_o[...] = v_new

    blk = pl.BlockSpec((tr, C), lambda i: (i, 0))
    shp = jax.ShapeDtypeStruct((R, C), F32)
    return pl.pallas_call(
        kern, name=name, grid=(R // tr,),
        in_specs=[pl.BlockSpec((n, tr, C), lambda i: (0, i, 0)), blk, blk, blk],
        out_specs=[blk] * 4, out_shape=[shp] * 4,
        compiler_params=_params(("parallel",)),
    )(parts, w, m, v)


def _cols_from_gathered(g):
    return jnp.transpose(g, (1, 0, 2)).reshape(g.shape[1], N_DEV * g.shape[2])


def _cols_to_shards(w):
    k, n8 = w.shape
    return jnp.transpose(w.reshape(k, N_DEV, n8 // N_DEV), (1, 0, 2))


def _pad_w_in(w):
    z = functools.partial(jnp.zeros, dtype=w.dtype)
    k = w.shape[0]
    cut = D_SSM + Q_LORA + KV_LORA
    return jnp.concatenate([w[:, :cut], z((k, 64)), w[:, cut:], z((k, 32))], axis=1)


def _unpad_w_in(w):
    cut = D_SSM + Q_LORA + KV_LORA
    return jnp.concatenate([w[:, :cut], w[:, cut + 64:cut + 96]], axis=1)


def _pad_w_uq(w):
    k = w.shape[0]
    w3 = w.reshape(k, N_HEADS, QK_NOPE + QK_ROPE)
    return jnp.pad(w3, ((0, 0), (0, 0), (0, HEAD_PAD - QK_NOPE - QK_ROPE))).reshape(k, QK_W)


def _unpad_w_uq(w):
    k = w.shape[0]
    return w.reshape(k, N_HEADS, HEAD_PAD)[:, :, :QK_NOPE + QK_ROPE].reshape(k, N_HEADS * (QK_NOPE + QK_ROPE))


def _pad_w_ukv(w):
    k = w.shape[0]
    w3 = w.reshape(k, N_HEADS, QK_NOPE + V_HEAD)
    kpart = jnp.pad(w3[:, :, :QK_NOPE], ((0, 0), (0, 0), (0, HEAD_PAD - QK_NOPE))).reshape(k, QK_W)
    vpart = w3[:, :, QK_NOPE:].reshape(k, V_W)
    return jnp.concatenate([kpart, vpart], axis=1)


def _unpad_w_ukv(w):
    k = w.shape[0]
    kpart = w[:, :QK_W].reshape(k, N_HEADS, HEAD_PAD)[:, :, :QK_NOPE]
    vpart = w[:, QK_W:].reshape(k, N_HEADS, V_HEAD)
    return jnp.concatenate([kpart, vpart], axis=2).reshape(k, N_HEADS * (QK_NOPE + V_HEAD))


def _block_diag(blocks):
    g, r, c = blocks.shape
    b4 = blocks.reshape(N_GB, GB, r, c)
    keep = jnp.eye(GB, dtype=bool)[None, :, None, :, None]
    return jnp.where(keep, b4[:, :, :, None, :], 0).reshape(g * r, GB * c)


def _block_diag_take(stacked, r, c):
    d5 = stacked.reshape(N_GB, GB, r, GB, c)
    idx = jnp.arange(GB)
    return jnp.transpose(d5[:, idx, :, idx, :], (1, 0, 2, 3)).reshape(N_GROUPS, r, c)


SMALL = ["norm1_g", "ssm_lambda_re", "ssm_lambda_im", "ssm_b_re", "ssm_b_im", "ssm_c_re", "ssm_c_im",
         "ssm_d", "ssm_log_dt", "q_norm_g", "kv_norm_g", "ssm_out_g", "attn_out_g", "norm2_g", "final_norm_g"]
BIASES = ["ada_b", "final_ada_b"]
SMALL_COLS = 1024


def _pack_small(values):
    flat = jnp.concatenate([v.reshape(-1) for v in values])
    rows = -(-flat.shape[0] // (8 * SMALL_COLS)) * 8
    return jnp.pad(flat, (0, rows * SMALL_COLS - flat.shape[0])).reshape(rows, SMALL_COLS)


def _unpack_small(packed, like):
    flat = packed.reshape(-1)
    out, off = [], 0
    for ref in like:
        n = int(np.prod(ref.shape))
        out.append(flat[off:off + n].reshape(ref.shape))
        off += n
    return out


def kernel(x, c, positions, ada_w, ada_b, norm1_g, w_in, ssm_lambda_re, ssm_lambda_im, ssm_b_re, ssm_b_im, ssm_c_re, ssm_c_im, ssm_d, ssm_log_dt, w_glu, q_norm_g, w_uq, kv_norm_g, w_ukv, ssm_out_g, attn_out_g, w_out, norm2_g, w_ff1, w_ff2, final_ada_w, final_ada_b, final_norm_g, loss_target, m_ada_w, m_ada_b, m_norm1_g, m_w_in, m_ssm_lambda_re, m_ssm_lambda_im, m_ssm_b_re, m_ssm_b_im, m_ssm_c_re, m_ssm_c_im, m_ssm_d, m_ssm_log_dt, m_w_glu, m_q_norm_g, m_w_uq, m_kv_norm_g, m_w_ukv, m_ssm_out_g, m_attn_out_g, m_w_out, m_norm2_g, m_w_ff1, m_w_ff2, m_final_ada_w, m_final_ada_b, m_final_norm_g, v_ada_w, v_ada_b, v_norm1_g, v_w_in, v_ssm_lambda_re, v_ssm_lambda_im, v_ssm_b_re, v_ssm_b_im, v_ssm_c_re, v_ssm_c_im, v_ssm_d, v_ssm_log_dt, v_w_glu, v_q_norm_g, v_w_uq, v_kv_norm_g, v_w_ukv, v_ssm_out_g, v_attn_out_g, v_w_out, v_norm2_g, v_w_ff1, v_w_ff2, v_final_ada_w, v_final_ada_b, v_final_norm_g):
    given = dict(locals())
    B, S, D = x.shape
    T = B * S
    tm = min(256, S)
    me = 4 * lax.axis_index("x") + 2 * lax.axis_index("y") + lax.axis_index("c")

    big = ["w_in", "w_glu", "w_uq", "w_ukv", "w_out", "w_ff1", "w_ff2"]
    shards = {n: given[n][0] for n in big}
    early, late = ["w_in", "w_uq", "w_ukv"], ["w_glu", "w_out", "w_ff1", "w_ff2"]
    gathered = all_gather("gather_first_weights", [c] + [shards[n].astype(BF16) for n in early])
    c_all = gathered[0].reshape(N_DEV * B, D)
    gw = dict(zip(early, gathered[1:]))
    w_in_p = _pad_w_in(_cols_from_gathered(gw["w_in"]))
    w_uq_p = _pad_w_uq(_cols_from_gathered(gw["w_uq"]))
    w_ukv_p = _pad_w_ukv(_cols_from_gathered(gw["w_ukv"]))

    n_mod = ada_w.shape[2]
    n_fmod = final_ada_w.shape[1]
    ada_b_sh = lax.dynamic_slice_in_dim(ada_b, me * n_mod, n_mod, axis=1)
    fada_b_sh = lax.dynamic_slice_in_dim(final_ada_b[None, :], me * n_fmod, n_fmod, axis=1)
    cond_all, mod_sh, fmod_sh = modulation_fwd(c_all, ada_w[0], ada_b_sh, final_ada_w, fada_b_sh)
    mod_g, fmod_g = all_gather("gather_modulation", [mod_sh, fmod_sh])
    mod_mine = lax.dynamic_slice_in_dim(_cols_from_gathered(mod_g), me * B, B, axis=0)
    fmod_mine = lax.dynamic_slice_in_dim(_cols_from_gathered(fmod_g), me * B, B, axis=0)
    shift1, scale1, gate1, shift2, scale2, gate2 = [mod_mine[:, None, i * D:(i + 1) * D] for i in range(6)]
    fshift, fscale = fmod_mine[:, None, :D], fmod_mine[:, None, D:]

    x2d = x.reshape(T, D)
    tgt2d = loss_target.reshape(T, D)
    pos = positions.astype(F32).reshape(T, 1)
    invf, m_lo, m_hi = _rope_consts()
    rope_mask = m_lo + m_hi

    def fwd_in(rows, bv, vecs, res):
        (xv,), (sc, sh), (g1,), (w,) = rows, bv, vecs, res
        r, xh = _rms_stats(xv)
        h = xh * g1 * (1.0 + sc) + sh
        return [_dot(h.astype(BF16), w[...])], [], []

    (proj,), _, _ = row_call("fwd_in", fwd_in, B, S, tm, [x2d], [scale1, shift1], [norm1_g], [w_in_p],
                             [(IN_PAD, F32)], [], [])

    def fwd_qkv(rows, bv, vecs, res):
        (pr, ps), (gq, gkv, fr, lo, hi), (wq, wkv) = rows, vecs, res
        cs, s_lo, s_hi = _rope_tables(ps, fr, lo, hi)
        _, qh = _rms_stats(pr[:, D_SSM:D_SSM + Q_LORA])
        _, kvh = _rms_stats(pr[:, D_SSM + Q_LORA:D_SSM + Q_LORA + KV_LORA])
        qf = _dot((qh * gq).astype(BF16), wq[...])
        kvf = _dot((kvh * gkv).astype(BF16), wkv[...])
        k_rope = _rope_tile(pr[:, IN_PAD - HEAD_PAD:], cs, s_lo, s_hi, 1.0)
        q_t, k_t = [], []
        for h in range(N_HEADS):
            hs = slice(h * HEAD_PAD, (h + 1) * HEAD_PAD)
            q_t.append(_rope_tile(qf[:, hs], cs, s_lo, s_hi, 1.0))
            k_t.append(kvf[:, hs] + k_rope)
        kf, vf = jnp.concatenate(k_t, axis=1), kvf[:, QK_W:]
        return [jnp.concatenate(q_t, axis=1), kf, vf, kf.T, vf.T], [], []

    ns = S // tm
    col_tile = lambda b, s: (0, b * ns + s)
    (q_r, k_r, v_r, k_t, v_t), _, _ = row_call(
        "fwd_qkv", fwd_qkv, B, S, tm, [proj, pos], [], [q_norm_g, kv_norm_g, invf, m_lo, m_hi],
        [w_uq_p, w_ukv_p],
        [(QK_W, BF16), (QK_W, BF16), (V_W, BF16), ((QK_W, T), BF16, (QK_W, tm), col_tile),
         ((V_W, T), BF16, (V_W, tm), col_tile)], [], [])
    late_gather = Exchange([shards[n].astype(BF16) for n in late], [GATHER] * len(late))
    y_attn, lse, *late_g = attention_fwd(q_r, k_r, v_t, B, S, tm, late_gather)
    gw = dict(zip(late, late_g))
    w_glu_f = _cols_from_gathered(gw["w_glu"])
    w_out_f = gw["w_out"].reshape(-1, D)
    w_ff1_f = _cols_from_gathered(gw["w_ff1"])
    w_ff2_f = gw["w_ff2"].reshape(-1, D)

    rep = lambda a: jnp.repeat(a, GROUP, axis=0)
    lam_rep = [rep(ssm_lambda_re[0]), rep(ssm_lambda_im[0]),
               rep(jnp.broadcast_to(ssm_log_dt[0][:, None], (N_GROUPS, N_STATE)))]
    bt = lambda a: jnp.transpose(a, (0, 2, 1)).reshape(D_SSM, N_STATE)
    reps = lam_rep + [bt(ssm_b_re[0]), bt(ssm_b_im[0])]
    a_re_rep, a_im_rep, bb_re, bb_im = ssm_params_fwd(reps)
    a_re = a_re_rep.reshape(N_GROUPS, GROUP, N_STATE)[:, 0, :].reshape(N_LC, 1, LANES)
    a_im = a_im_rep.reshape(N_GROUPS, GROUP, N_STATE)[:, 0, :].reshape(N_LC, 1, LANES)
    bbd_re = _block_diag(bb_re.reshape(N_GROUPS, GROUP, N_STATE)).astype(BF16)
    bbd_im = _block_diag(bb_im.reshape(N_GROUPS, GROUP, N_STATE)).astype(BF16)
    cbd_re = _block_diag(jnp.transpose(ssm_c_re[0], (0, 2, 1))).astype(BF16)
    cbd_im = _block_diag(jnp.transpose(ssm_c_im[0], (0, 2, 1))).astype(BF16)
    d_vec = ssm_d.reshape(1, D_SSM)
    y_pre, cin_re, cin_im = ssm_fwd(proj, bbd_re, bbd_im, cbd_re, cbd_im, d_vec, a_re, a_im, B, S, tm)

    def fwd_mix(rows, bv, vecs, res):
        (yp, ya, xv), (g1v,), (gso, gao), (wg, wo) = rows, bv, vecs, res
        z = _dot(_gelu(yp).astype(BF16), wg[...])
        y_ssm = z[:, :D_SSM] * _sigmoid(z[:, D_SSM:])
        _, sh = _rms_stats(y_ssm)
        _, ah = _rms_stats(ya)
        o = _dot((sh * gso).astype(BF16), wo[0:D_SSM, :]) + _dot((ah * gao).astype(BF16), wo[D_SSM:, :])
        return [z, o, xv + g1v * o], [], []

    (z, o, x2), _, _ = row_call("fwd_mix", fwd_mix, B, S, tm, [y_pre, y_attn, x2d], [gate1],
                                [ssm_out_g, attn_out_g], [w_glu_f, w_out_f],
                                [(2 * D_SSM, F32), (D, F32), (D, F32)], [], [])

    def final_out(x3, gf, fsc, fsh):
        r3, xh3 = _rms_stats(x3)
        n3 = xh3 * gf
        return r3, xh3, n3, n3 * (1.0 + fsc) + fsh

    def mlp_hidden(xv, g2, sc, sh, w1):
        r2, xh2 = _rms_stats(xv)
        n2 = xh2 * g2
        h2 = n2 * (1.0 + sc) + sh
        return r2, xh2, n2, h2, _dot(h2.astype(BF16), w1[...])

    def fwd_mlp(rows, bv, vecs, res):
        (xv, tg), (sc, sh, g2v, fsc, fsh), (g2, gf), (w1, w2) = rows, bv, vecs, res
        _, _, _, _, a = mlp_hidden(xv, g2, sc, sh, w1)
        ra = jnp.maximum(a, 0.0)
        ff = _dot((ra * ra).astype(BF16), w2[...])
        x3 = xv + g2v * ff
        _, _, _, out = final_out(x3, gf, fsc, fsh)
        err = out - tg
        loss = 0.5 * jnp.sum(jnp.mean(err * err, axis=-1, keepdims=True), axis=0, keepdims=True)
        return [ff, x3], [], [jnp.broadcast_to(loss, (1, 128))]

    fnorm_g = final_norm_g[None, :]
    (ff, x3), _, (loss_acc,) = row_call(
        "fwd_mlp", fwd_mlp, B, S, tm, [x2, tgt2d], [scale2, shift2, gate2, fscale, fshift], [norm2_g, fnorm_g],
        [w_ff1_f, w_ff2_f], [(D, F32), (D, F32)], [], [(1, 128)])
    loss = lax.psum(loss_acc[0, 0], ("x", "y", "c"))

    def bwd_mlp(rows, bv, vecs, res):
        (x3v, tg, x2v, ffv), (sc, sh, g2v, fsc, fsh), (g2, gf), (w1, w2) = rows, bv, vecs, res
        r3, xh3, n3, out = final_out(x3v, gf, fsc, fsh)
        dout = (out - tg) * (1.0 / D)
        dn3 = dout * (1.0 + fsc)
        dx3 = _rms_bwd(dn3 * gf, xh3, r3)
        dff = dx3 * g2v
        r2, xh2, n2, h2, a = mlp_hidden(x2v, g2, sc, sh, w1)
        ra = jnp.maximum(a, 0.0)
        dffb = dff.astype(BF16)
        da = _dot_nt(dffb, w2[...]) * (2.0 * ra)
        dab = da.astype(BF16)
        dh2 = _dot_nt(dab, w1[...])
        dn2 = dh2 * (1.0 + sc)
        dx2 = dx3 + _rms_bwd(dn2 * g2, xh2, r2)
        return ([dx2, h2, ra * ra, dab, dffb],
                [_colsum(dout), _colsum(dout * n3), _colsum(dx3 * ffv), _colsum(dh2), _colsum(dh2 * n2)],
                [_colsum(dn3 * xh3), _colsum(dn2 * xh2)])

    ((dx2, h2b, fb, dab, dffb), (d_fshift, d_fscale, d_gate2, d_shift2, d_scale2), (d_gf, d_g2)) = row_call(
        "bwd_mlp", bwd_mlp, B, S, tm, [x3, tgt2d, x2, ff], [scale2, shift2, gate2, fscale, fshift],
        [norm2_g, fnorm_g], [w_ff1_f, w_ff2_f],
        [(D, F32), (D, BF16), (4 * D, BF16), (4 * D, BF16), (D, BF16)], [D] * 5, [(1, D), (1, D)])
    g_w_ff1 = matmul_tn("grad_w_ff1", h2b, dab)
    g_w_ff2 = matmul_tn("grad_w_ff2", fb, dffb)

    def bwd_mix(rows, bv, vecs, res):
        (dxv, ov, zv, ya, yp), (g1v,), (gso, gao), (wo, wg) = rows, bv, vecs, res
        do = (dxv * g1v).astype(BF16)
        dyn = _dot_nt(do, wo[...])
        z1, sg = zv[:, :D_SSM], _sigmoid(zv[:, D_SSM:])
        y_ssm = z1 * sg
        rs, sh = _rms_stats(y_ssm)
        ra, ah = _rms_stats(ya)
        dyn_s, dyn_a = dyn[:, :D_SSM], dyn[:, D_SSM:]
        dy_ssm = _rms_bwd(dyn_s * gso, sh, rs)
        dy_attn = _rms_bwd(dyn_a * gao, ah, ra)
        dz = jnp.concatenate([dy_ssm * sg, dy_ssm * z1 * sg * (1.0 - sg)], axis=1).astype(BF16)
        dy_pre = _dot_nt(dz, wg[...]) * _gelu_grad(yp)
        yn = jnp.concatenate([sh * gso, ah * gao], axis=1)
        delta = jnp.sum((dy_attn * ya).T.reshape(N_HEADS, V_HEAD, tm), axis=1, keepdims=True)
        return ([do, yn, dz, _gelu(yp), dy_attn, dy_pre, delta], [_colsum(dxv * ov)],
                [_colsum(dyn_s * sh), _colsum(dyn_a * ah)])

    ((dob, ynb, dzb, ygb, dy_attn, dy_pre, delta), (d_gate1,), (d_gso, d_gao)) = row_call(
        "bwd_mix", bwd_mix, B, S, tm, [dx2, o, z, y_attn, y_pre], [gate1], [ssm_out_g, attn_out_g],
        [w_out_f, w_glu_f],
        [(D, BF16), (D, BF16), (2 * D_SSM, BF16), (D_SSM, BF16), (V_W, F32), (D_SSM, F32),
         ((N_HEADS, 1, T), F32, (N_HEADS, 1, tm), lambda b, s: (0, 0, b * ns + s))],
        [D], [(1, D_SSM), (1, V_W)])
    g_w_out = matmul_tn("grad_w_out", ynb, dob)
    g_w_glu = matmul_tn("grad_w_glu", ygb, dzb)

    grads_a = Exchange([g_w_glu, g_w_out.reshape(N_DEV, -1, D), g_w_ff1, g_w_ff2.reshape(N_DEV, -1, D)],
                       [w_glu.shape[2], None, w_ff1.shape[2], None])
    dq_t, dk_r, dv_r, *parts_a = attention_bwd(q_r, k_r, k_t, v_r, dy_attn, lse, delta, B, S, tm, grads_a)
    parts = dict(zip(late, parts_a))
    dq_t = dq_t.reshape(B, QK_W, S)

    def bwd_qkv(rows, bv, vecs, res):
        (dqt, dkv, dvv, pr, ps), (gq, gkv, fr, lo, hi, rmask), (wq, wkv) = rows, vecs, res
        dqv = dqt[0].T
        cs, s_lo, s_hi = _rope_tables(ps, fr, lo, hi)
        dq_t = []
        dk_rope = jnp.zeros((dkv.shape[0], HEAD_PAD), F32)
        for h in range(N_HEADS):
            hs = slice(h * HEAD_PAD, (h + 1) * HEAD_PAD)
            dq_t.append(_rope_tile(dqv[:, hs], cs, s_lo, s_hi, -1.0))
            dk_rope = dk_rope + dkv[:, hs]
        dk_rope = _rope_tile(dk_rope * rmask, cs, s_lo, s_hi, -1.0)
        dqb = jnp.concatenate(dq_t, axis=1).astype(BF16)
        dkvb = jnp.concatenate([dkv, dvv], axis=1).astype(BF16)
        rq, qh = _rms_stats(pr[:, D_SSM:D_SSM + Q_LORA])
        rk, kvh = _rms_stats(pr[:, D_SSM + Q_LORA:D_SSM + Q_LORA + KV_LORA])
        dqn = _dot_nt(dqb, wq[...])
        dkvn = _dot_nt(dkvb, wkv[...])
        dpa = jnp.concatenate([_rms_bwd(dqn * gq, qh, rq), _rms_bwd(dkvn * gkv, kvh, rk), dk_rope], axis=1)
        return [dpa, qh * gq, kvh * gkv, dqb, dkvb], [], [_colsum(dqn * qh), _colsum(dkvn * kvh)]

    ((dpa, qnb, kvnb, dqb, dkvb), _, (d_gq, d_gkv)) = row_call(
        "bwd_qkv", bwd_qkv, B, S, tm,
        [(dq_t, (1, QK_W, tm), lambda b, s: (b, 0, s)), dk_r, dv_r, proj, pos], [],
        [q_norm_g, kv_norm_g, invf, m_lo, m_hi, rope_mask], [w_uq_p, w_ukv_p],
        [(Q_LORA + KV_LORA + HEAD_PAD, F32), (Q_LORA, BF16), (KV_LORA, BF16), (QK_W, BF16), (QK_W + V_W, BF16)],
        [], [(1, Q_LORA), (1, KV_LORA)])
    g_w_uq = _unpad_w_uq(matmul_tn("grad_w_uq", qnb, dqb))
    g_w_ukv = _unpad_w_ukv(matmul_tn("grad_w_ukv", kvnb, dkvb))

    du, dc_re_d, dc_im_d, db_re_d, db_im_d, da_re, da_im, d_d = ssm_bwd(
        dy_pre, proj, cin_re, cin_im, bbd_re, bbd_im, cbd_re, cbd_im, d_vec, a_re, a_im, B, S, tm)
    place = lambda a: jnp.zeros((N_GROUPS, GROUP, N_STATE), F32).at[:, 0, :].set(
        a.reshape(N_GROUPS, N_STATE)).reshape(D_SSM, N_STATE)
    cots = [place(da_re), place(da_im),
            _block_diag_take(db_re_d, GROUP, N_STATE).reshape(D_SSM, N_STATE),
            _block_diag_take(db_im_d, GROUP, N_STATE).reshape(D_SSM, N_STATE)]
    g_lam_re, g_lam_im, g_log_dt, g_bt_re, g_bt_im = ssm_params_bwd(reps, cots)
    unbt = lambda a: jnp.transpose(a.reshape(N_GROUPS, GROUP, N_STATE), (0, 2, 1))
    g_c_re = jnp.transpose(_block_diag_take(dc_re_d, N_STATE, GROUP), (0, 2, 1))
    g_c_im = jnp.transpose(_block_diag_take(dc_im_d, N_STATE, GROUP), (0, 2, 1))

    def bwd_in(rows, bv, vecs, res):
        (duv, dpav, xv, dx2v), (sc, sh), (g1,), (w,) = rows, bv, vecs, res
        dproj = jnp.concatenate([duv, dpav], axis=1).astype(BF16)
        dh = _dot_nt(dproj, w[...])
        r, xh = _rms_stats(xv)
        n1 = xh * g1
        dn1 = dh * (1.0 + sc)
        dx = dx2v + _rms_bwd(dn1 * g1, xh, r)
        return [dx, n1 * (1.0 + sc) + sh, dproj], [_colsum(dh), _colsum(dh * n1)], [_colsum(dn1 * xh)]

    ((grad_x, h1b, dprojb), (d_shift1, d_scale1), (d_g1,)) = row_call(
        "bwd_in", bwd_in, B, S, tm, [du, dpa, x2d, dx2], [scale1, shift1], [norm1_g], [w_in_p],
        [(D, F32), (D, BF16), (IN_PAD, BF16)], [D, D], [(1, D)])
    g_w_in = _unpad_w_in(matmul_tn("grad_w_in", h1b, dprojb))

    dmod = jnp.concatenate([d_shift1, d_scale1, d_gate1, d_shift2, d_scale2, d_gate2, d_fshift, d_fscale],
                           axis=2).reshape(B, 8 * D)
    small_part = {
        "norm1_g": d_g1, "ssm_lambda_re": g_lam_re,
        "ssm_lambda_im": g_lam_im, "ssm_b_re": unbt(g_bt_re), "ssm_b_im": unbt(g_bt_im), "ssm_c_re": g_c_re,
        "ssm_c_im": g_c_im, "ssm_d": d_d, "ssm_log_dt": g_log_dt, "q_norm_g": d_gq, "kv_norm_g": d_gkv,
        "ssm_out_g": d_gso, "attn_out_g": d_gao, "norm2_g": d_g2, "final_norm_g": d_gf}
    small_packed = _pack_small([small_part[n] for n in SMALL])
    dmod_g, small_g, *parts_b = exchange(
        "exchange_last_grads", [dmod, small_packed, _cols_to_shards(g_w_in), _cols_to_shards(g_w_uq), g_w_ukv],
        [GATHER, GATHER, None, None, w_ukv.shape[2]])
    parts.update(zip(early, parts_b))
    dmod_all = dmod_g.reshape(N_DEV * B, 8 * D)
    dm_sh = lax.dynamic_slice_in_dim(dmod_all[:, :6 * D], me * n_mod, n_mod, axis=1)
    dfm_sh = lax.dynamic_slice_in_dim(dmod_all[:, 6 * D:], me * n_fmod, n_fmod, axis=1)
    g_ada_w, g_fada_w, g_biases = modulation_bwd(cond_all, dm_sh, dfm_sh, dmod_all)
    parts["ada_w"] = g_ada_w[None]
    parts["final_ada_w"] = g_fada_w[None]

    res_g, res_d, res_m, res_v = {}, {}, {}, {}
    for n in ["ada_w"] + big + ["final_ada_w"]:
        w_full = given[n]
        w2 = w_full.reshape(w_full.shape[-2], w_full.shape[-1])
        out = adamw("adamw_" + n, parts[n], w2, given["m_" + n].reshape(w2.shape), given["v_" + n].reshape(w2.shape))
        res_g[n], res_d[n], res_m[n], res_v[n] = [a.reshape(w_full.shape) for a in out]
    small_w = [given[n] for n in SMALL]
    out = adamw("adamw_small", small_g, _pack_small(small_w), _pack_small([given["m_" + n] for n in SMALL]),
                _pack_small([given["v_" + n] for n in SMALL]))
    for res, packed in zip((res_g, res_d, res_m, res_v), out):
        for n, val in zip(SMALL, _unpack_small(packed, small_w)):
            res[n] = val
    bias_w = [given[n] for n in BIASES]
    out = adamw("adamw_biases", g_biases.reshape(1, 8, D), _pack_small(bias_w),
                _pack_small([given["m_" + n] for n in BIASES]), _pack_small([given["v_" + n] for n in BIASES]))
    for res, packed in zip((res_g, res_d, res_m, res_v), out):
        for n, val in zip(BIASES, _unpack_small(packed, bias_w)):
            res[n] = val

    order = ["ada_w", "ada_b", "norm1_g", "w_in", "ssm_lambda_re", "ssm_lambda_im", "ssm_b_re", "ssm_b_im",
             "ssm_c_re", "ssm_c_im", "ssm_d", "ssm_log_dt", "w_glu", "q_norm_g", "w_uq", "kv_norm_g", "w_ukv",
             "ssm_out_g", "attn_out_g", "w_out", "norm2_g", "w_ff1", "w_ff2", "final_ada_w", "final_ada_b",
             "final_norm_g"]
    return (loss, grad_x.reshape(B, S, D), *[res_g[n] for n in order], *[res_d[n] for n in order],
            *[res_m[n] for n in order], *[res_v[n] for n in order])
```

```python
import functools
import math

import numpy as np
import jax
import jax.numpy as jnp
from jax import lax
from jax.experimental import pallas as pl
from jax.experimental.pallas import tpu as pltpu

F32 = jnp.float32
BF16 = jnp.bfloat16

N_DEV = 8
EPS = 1e-6
D_SSM = 512
N_GROUPS = 32
GROUP = 16
N_STATE = 64
N_ST = N_GROUPS * N_STATE
Q_LORA = 384
KV_LORA = 256
QK_NOPE = 64
QK_ROPE = 32
V_HEAD = 64
N_HEADS = 8
HEAD_PAD = 128
QK_W = N_HEADS * HEAD_PAD
V_W = N_HEADS * V_HEAD
IN_COLS = D_SSM + Q_LORA + KV_LORA + QK_ROPE
IN_PAD = D_SSM + Q_LORA + KV_LORA + HEAD_PAD
ROPE_BASE = 10000.0
ATTN_SCALE = (QK_NOPE + QK_ROPE) ** -0.5
NEG = -1e30

ADAM_LR = 0.001
ADAM_B1 = 0.9
ADAM_B2 = 0.999
ADAM_EPS = 1e-08
ADAM_WD = 0.01
ADAM_STEP = 10

VMEM_LIMIT = 56 * 1024 * 1024
MESH_ID = pl.DeviceIdType.MESH


def _dot(a, b):
    return jnp.dot(a, b, preferred_element_type=F32)


def _dot_nt(a, b):
    return lax.dot_general(a, b, (((1,), (1,)), ((), ())), preferred_element_type=F32)


def _dot_tn(a, b):
    return lax.dot_general(a, b, (((0,), (0,)), ((), ())), preferred_element_type=F32)


def _rms_stats(x):
    r = lax.rsqrt(jnp.mean(x * x, axis=-1, keepdims=True) + EPS)
    return r, x * r


def _rms_bwd(dy, xh, r):
    return r * (dy - xh * jnp.mean(dy * xh, axis=-1, keepdims=True))


def _sigmoid(x):
    return 1.0 / (1.0 + jnp.exp(-x))


_GELU_C = math.sqrt(2.0 / math.pi)


def _gelu(x):
    t = jnp.tanh(_GELU_C * (x + 0.044715 * x * x * x))
    return 0.5 * x * (1.0 + t)


def _gelu_grad(x):
    t = jnp.tanh(_GELU_C * (x + 0.044715 * x * x * x))
    return 0.5 * (1.0 + t) + 0.5 * x * (1.0 - t * t) * _GELU_C * (1.0 + 3.0 * 0.044715 * x * x)


def _colsum(a):
    return jnp.sum(a, axis=0, keepdims=True)


def _params(sem=None):
    return pltpu.CompilerParams(dimension_semantics=sem, vmem_limit_bytes=VMEM_LIMIT)


def _mesh_pos():
    x, y, c = lax.axis_index("x"), lax.axis_index("y"), lax.axis_index("c")
    return x, y, c, 4 * x + 2 * y + c


def _peer(x, y, c, k):
    px = 1 - x if k & 4 else x
    py = 1 - y if k & 2 else y
    pc = 1 - c if k & 1 else c
    return (px, py, pc), 4 * px + 2 * py + pc


GATHER = "gather"
ANY_SPEC = pl.BlockSpec(memory_space=pl.ANY)


class Exchange:
    def __init__(self, arrays, kinds):
        self.arrays, self.kinds, self.n = list(arrays), list(kinds), len(arrays)

    def _shard(self, i):
        a, kind = self.arrays[i], self.kinds[i]
        if kind == GATHER:
            return a.shape
        return a.shape[1:] if kind is None else (a.shape[0], kind)

    def out_shapes(self):
        return [jax.ShapeDtypeStruct((N_DEV,) + tuple(self._shard(i)), self.arrays[i].dtype) for i in range(self.n)]

    def scratch(self):
        return [pltpu.SemaphoreType.DMA((self.n, N_DEV - 1)), pltpu.SemaphoreType.DMA((self.n, N_DEV - 1)),
                pltpu.SemaphoreType.DMA((self.n,))]

    def _src(self, ins, i, j):
        kind = self.kinds[i]
        if kind == GATHER:
            return ins[i]
        if kind is None:
            return ins[i].at[j]
        return ins[i].at[:, pl.ds(pl.multiple_of(j * kind, 128), kind)]

    def _copies(self, ins, outs, sems, with_received):
        send_sems, recv_sems, loc_sems = sems
        x, y, c, me = _mesh_pos()
        local, sent, received = [], [], []
        for i in range(self.n):
            local.append(pltpu.make_async_copy(self._src(ins, i, me), outs[i].at[me], loc_sems.at[i]))
            for k in range(1, N_DEV):
                peer, pidx = _peer(x, y, c, k)
                pair = dict(send_sem=send_sems.at[i, k - 1], recv_sem=recv_sems.at[i, k - 1], device_id=peer,
                            device_id_type=MESH_ID)
                sent.append(pltpu.make_async_remote_copy(
                    src_ref=self._src(ins, i, pidx), dst_ref=outs[i].at[me], **pair))
                if with_received:
                    received.append(pltpu.make_async_remote_copy(
                        src_ref=self._src(ins, i, pidx), dst_ref=outs[i].at[pidx], **pair))
        return local, sent, received

    def start(self, ins, outs, sems):
        local, sent, _ = self._copies(ins, outs, sems, False)
        for cp in local + sent:
            cp.start()

    def wait(self, ins, outs, sems):
        local, sent, received = self._copies(ins, outs, sems, True)
        for cp in received:
            cp.wait_recv()
        for cp in sent:
            cp.wait_send()
        for cp in local:
            cp.wait()


def exchange(name, arrays, kinds):
    ex = Exchange(arrays, kinds)
    n = ex.n

    def body(*refs):
        ins, outs, sems = refs[:n], refs[n:2 * n], refs[2 * n:]
        ex.start(ins, outs, sems)
        ex.wait(ins, outs, sems)

    return pl.pallas_call(
        body, name=name, out_shape=ex.out_shapes(), in_specs=[ANY_SPEC] * n, out_specs=[ANY_SPEC] * n,
        scratch_shapes=ex.scratch())(*arrays)


def all_gather(name, arrays):
    return exchange(name, arrays, [GATHER] * len(arrays))


def row_call(name, body, B, S, tm, rows, bvecs, vecs, res, row_outs, bacc, gacc):
    ns = S // tm
    T = B * S
    n_r, n_b, n_v, n_w = len(rows), len(bvecs), len(vecs), len(res)
    n_ro, n_ba, n_ga = len(row_outs), len(bacc), len(gacc)

    def kern(*refs):
        i = 0
        r_refs = refs[i:i + n_r]; i += n_r
        b_refs = refs[i:i + n_b]; i += n_b
        v_refs = refs[i:i + n_v]; i += n_v
        w_hbm = refs[i:i + n_w]; i += n_w
        ro_refs = refs[i:i + n_ro]; i += n_ro
        ba_refs = refs[i:i + n_ba]; i += n_ba
        ga_refs = refs[i:i + n_ga]; i += n_ga
        w_vmem = refs[i:i + n_w]
        b = pl.program_id(0)
        s = pl.program_id(1)
        first = jnp.logical_and(b == 0, s == 0)

        if n_w:
            @pl.when(first)
            def _load_weights():
                for h, v in zip(w_hbm, w_vmem):
                    pltpu.sync_copy(h, v)

        ro, ba, ga = body([r[...] for r in r_refs], [r[0] for r in b_refs], [r[...] for r in v_refs],
                          list(w_vmem))
        for ref, val in zip(ro_refs, ro):
            ref[...] = val.astype(ref.dtype)

        def accumulate(ref, val, is_first, idx):
            @pl.when(is_first)
            def _set():
                ref[idx] = val

            @pl.when(jnp.logical_not(is_first))
            def _add():
                ref[idx] = ref[idx] + val

        for ref, val in zip(ba_refs, ba):
            accumulate(ref, val, s == 0, 0)
        for ref, val in zip(ga_refs, ga):
            accumulate(ref, val, first, Ellipsis)

    row_arrays = [r[0] if isinstance(r, tuple) else r for r in rows]
    row_in_specs = [pl.BlockSpec(r[1], r[2]) if isinstance(r, tuple)
                    else pl.BlockSpec((tm, r.shape[1]), lambda b, s: (b * ns + s, 0)) for r in rows]
    ro_shapes = [jax.ShapeDtypeStruct(tuple(o[0]), o[1]) if len(o) == 4 else jax.ShapeDtypeStruct((T, o[0]), o[1])
                 for o in row_outs]
    ro_specs = [pl.BlockSpec(o[2], o[3]) if len(o) == 4 else pl.BlockSpec((tm, o[0]), lambda b, s: (b * ns + s, 0))
                for o in row_outs]
    in_specs = (row_in_specs
                + [pl.BlockSpec((1, 1, v.shape[2]), lambda b, s: (b, 0, 0)) for v in bvecs]
                + [pl.BlockSpec(v.shape, lambda b, s, nd=v.ndim: (0,) * nd) for v in vecs]
                + [pl.BlockSpec(memory_space=pl.ANY) for _ in res])
    out_shape = (ro_shapes
                 + [jax.ShapeDtypeStruct((B, 1, w), F32) for w in bacc]
                 + [jax.ShapeDtypeStruct(tuple(sh), F32) for sh in gacc])
    out_specs = (ro_specs
                 + [pl.BlockSpec((1, 1, w), lambda b, s: (b, 0, 0)) for w in bacc]
                 + [pl.BlockSpec(tuple(sh), lambda b, s, nd=len(sh): (0,) * nd) for sh in gacc])
    outs = pl.pallas_call(
        kern, name=name, grid=(B, ns), in_specs=in_specs, out_specs=out_specs, out_shape=out_shape,
        scratch_shapes=[pltpu.VMEM(w.shape, w.dtype) for w in res],
        compiler_params=_params(("arbitrary", "arbitrary")),
    )(*row_arrays, *bvecs, *vecs, *res)
    return outs[:n_ro], outs[n_ro:n_ro + n_ba], outs[n_ro + n_ba:]


def matmul_nn(name, at, b):
    K, T = at.shape
    N = b.shape[1]
    tk = min(K, 512)
    tn = N if N <= 1280 else (1024 if N % 1024 == 0 else 512)
    assert K % tk == 0 and N % tn == 0 and T % min(T, 2048) == 0, (K, N, T)
    tt = min(T, 2048)
    nt = T // tt

    def kern(a_ref, b_ref, o_ref, acc_ref):
        t = pl.program_id(2)
        part = _dot(a_ref[...], b_ref[...])

        @pl.when(t == 0)
        def _set():
            acc_ref[...] = part

        @pl.when(t > 0)
        def _add():
            acc_ref[...] = acc_ref[...] + part

        @pl.when(t == nt - 1)
        def _write():
            o_ref[...] = acc_ref[...].astype(o_ref.dtype)

    return pl.pallas_call(
        kern, name=name, grid=(K // tk, N // tn, nt),
        in_specs=[pl.BlockSpec((tk, tt), lambda i, j, t: (i, t)), pl.BlockSpec((tt, tn), lambda i, j, t: (t, j))],
        out_specs=pl.BlockSpec((tk, tn), lambda i, j, t: (i, j)),
        out_shape=jax.ShapeDtypeStruct((K, N), BF16),
        scratch_shapes=[pltpu.VMEM((tk, tn), F32)],
        compiler_params=_params(("parallel", "parallel", "arbitrary")),
    )(at, b)


def _rope_consts():
    inv_freq = ROPE_BASE ** (-jnp.arange(0, QK_ROPE, 2, dtype=F32) / QK_ROPE)
    zeros64 = jnp.zeros((64,), F32)
    zeros32 = jnp.zeros((32,), F32)
    zeros16 = jnp.zeros((16,), F32)
    ones16 = jnp.ones((16,), F32)
    invf = jnp.concatenate([zeros64, inv_freq, inv_freq, zeros32])[None, :]
    m_lo = jnp.concatenate([zeros64, ones16, zeros16, zeros32])[None, :]
    m_hi = jnp.concatenate([zeros64, zeros16, ones16, zeros32])[None, :]
    return invf, m_lo, m_hi


def _rope_tables(pos, invf, m_lo, m_hi):
    ang = pos * invf
    return jnp.cos(ang), jnp.sin(ang) * m_lo, jnp.sin(ang) * m_hi


def _rope_tile(t, cs, s_lo, s_hi, sign):
    up = pltpu.roll(t, HEAD_PAD - 16, axis=1)
    down = pltpu.roll(t, 16, axis=1)
    return t * cs + sign * (down * s_hi - up * s_lo)


def _heads(ref, width):
    return jnp.stack([ref[:, h * width:(h + 1) * width] for h in range(N_HEADS)], axis=0)


def _bmm(a, b, ca, cb):
    return lax.dot_general(a, b, (((ca,), (cb,)), ((0,), (0,))), preferred_element_type=F32)


def _scores_t(k_ref, q_ref, masked, tq):
    st = _bmm(_heads(k_ref, HEAD_PAD), _heads(q_ref, HEAD_PAD), 2, 2) * ATTN_SCALE
    if masked:
        key = lax.broadcasted_iota(jnp.int32, (tq, tq), 0)
        qry = lax.broadcasted_iota(jnp.int32, (tq, tq), 1)
        st = jnp.where((key <= qry)[None], st, NEG)
    return st


def attention_fwd(q, k, vt, B, S, tq, side):
    nq = S // tq
    ns = side.n
    pairs = [(i, j) for i in range(nq) for j in range(i + 1)]
    n_pairs = len(pairs)
    q_tab = jnp.asarray([p[0] for p in pairs], jnp.int32)
    k_tab = jnp.asarray([p[1] for p in pairs], jnp.int32)

    def kern(q_tab_ref, k_tab_ref, *refs):
        q_ref, k_ref, vt_ref = refs[:3]
        side_in = refs[3:3 + ns]
        o_ref, lse_ref = refs[3 + ns:5 + ns]
        side_out = refs[5 + ns:5 + 2 * ns]
        m_scr, l_scr, acc_scr = refs[5 + 2 * ns:8 + 2 * ns]
        sems = refs[8 + 2 * ns:]
        b, t = pl.program_id(0), pl.program_id(1)
        qi, ki = q_tab_ref[t], k_tab_ref[t]

        @pl.when(jnp.logical_and(b == 0, t == 0))
        def _start_side():
            side.start(side_in, side_out, sems)

        @pl.when(ki == 0)
        def _init():
            m_scr[...] = jnp.full(m_scr.shape, NEG, F32)
            l_scr[...] = jnp.zeros(l_scr.shape, F32)
            acc_scr[...] = jnp.zeros(acc_scr.shape, F32)

        def block(masked):
            st = _scores_t(k_ref, q_ref, masked, tq)
            m_prev = m_scr[...]
            m_new = jnp.maximum(m_prev, jnp.max(st, axis=1, keepdims=True))
            alpha = jnp.exp(m_prev - m_new)
            p = jnp.exp(st - m_new)
            l_scr[...] = alpha * l_scr[...] + jnp.sum(p, axis=1, keepdims=True)
            vt3 = vt_ref[...].reshape(N_HEADS, V_HEAD, tq)
            acc_scr[...] = alpha * acc_scr[...] + _bmm(vt3, p.astype(BF16), 2, 1)
            m_scr[...] = m_new

        @pl.when(ki < qi)
        def _full():
            block(False)

        @pl.when(ki == qi)
        def _diagonal():
            block(True)
            ot = acc_scr[...] / l_scr[...]
            for h in range(N_HEADS):
                o_ref[:, h * V_HEAD:(h + 1) * V_HEAD] = ot[h].T
            lse_ref[...] = m_scr[...] + jnp.log(l_scr[...])

        @pl.when(jnp.logical_and(b == B - 1, t == n_pairs - 1))
        def _wait_side():
            side.wait(side_in, side_out, sems)

    T = B * S
    grid_spec = pltpu.PrefetchScalarGridSpec(
        num_scalar_prefetch=2, grid=(B, n_pairs),
        in_specs=[pl.BlockSpec((tq, QK_W), lambda b, t, qt, kt: (b * nq + qt[t], 0)),
                  pl.BlockSpec((tq, QK_W), lambda b, t, qt, kt: (b * nq + kt[t], 0)),
                  pl.BlockSpec((V_W, tq), lambda b, t, qt, kt: (0, b * nq + kt[t]))] + [ANY_SPEC] * ns,
        out_specs=[pl.BlockSpec((tq, V_W), lambda b, t, qt, kt: (b * nq + qt[t], 0)),
                   pl.BlockSpec((N_HEADS, 1, tq), lambda b, t, qt, kt: (0, 0, b * nq + qt[t]))] + [ANY_SPEC] * ns,
        scratch_shapes=[pltpu.VMEM((N_HEADS, 1, tq), F32), pltpu.VMEM((N_HEADS, 1, tq), F32),
                        pltpu.VMEM((N_HEADS, V_HEAD, tq), F32)] + side.scratch())
    return pl.pallas_call(
        kern, name="attention_fwd", grid_spec=grid_spec,
        out_shape=[jax.ShapeDtypeStruct((T, V_W), F32), jax.ShapeDtypeStruct((N_HEADS, 1, T), F32)]
        + side.out_shapes(),
        compiler_params=_params(("arbitrary", "arbitrary")),
    )(q_tab, k_tab, q, k, vt, *side.arrays)


def attention_bwd(q, k, kt, v, do, lse, delta, B, S, tq, side):
    nq = S // tq
    ns = side.n
    pairs = [(j, i) for j in range(nq) for i in range(j, nq)]
    n_pairs = len(pairs)
    k_tab = jnp.asarray([p[0] for p in pairs], jnp.int32)
    q_tab = jnp.asarray([p[1] for p in pairs], jnp.int32)

    def kern(k_tab_ref, q_tab_ref, *refs):
        q_ref, k_ref, kt_ref, v_ref, do_ref, lse_ref, delta_ref = refs[:7]
        side_in = refs[7:7 + ns]
        dq_hbm, dk_ref, dv_ref = refs[7 + ns:10 + ns]
        side_out = refs[10 + ns:10 + 2 * ns]
        dq_acc, dk_acc, dv_acc = refs[10 + 2 * ns:13 + 2 * ns]
        sems = refs[13 + 2 * ns:]
        b, t = pl.program_id(0), pl.program_id(1)
        kj, qi = k_tab_ref[t], q_tab_ref[t]

        @pl.when(jnp.logical_and(b == 0, t == 0))
        def _start_side():
            side.start(side_in, side_out, sems)

        @pl.when(t == 0)
        def _zero_dq():
            dq_acc[...] = jnp.zeros(dq_acc.shape, F32)

        @pl.when(qi == kj)
        def _zero_dkv():
            dk_acc[...] = jnp.zeros(dk_acc.shape, F32)
            dv_acc[...] = jnp.zeros(dv_acc.shape, F32)

        def block(masked):
            pt = jnp.exp(_scores_t(k_ref, q_ref, masked, tq) - lse_ref[...])
            do3 = _heads(do_ref, V_HEAD).astype(BF16)
            dv_acc[...] = dv_acc[...] + _bmm(pt.astype(BF16), do3, 2, 1)
            dpt = _bmm(_heads(v_ref, V_HEAD), do3, 2, 2)
            dst = (pt * (dpt - delta_ref[...]) * ATTN_SCALE).astype(BF16)
            dk_acc[...] = dk_acc[...] + _bmm(dst, _heads(q_ref, HEAD_PAD), 2, 1)
            q_cols = pl.ds(pl.multiple_of(qi * tq, tq), tq)
            kt3 = kt_ref[...].reshape(N_HEADS, HEAD_PAD, tq)
            dq_acc[:, :, q_cols] = dq_acc[:, :, q_cols] + _bmm(kt3, dst, 2, 1)

        @pl.when(qi > kj)
        def _full():
            block(False)

        @pl.when(qi == kj)
        def _diagonal():
            block(True)

        @pl.when(qi == nq - 1)
        def _write_dkv():
            for h in range(N_HEADS):
                dk_ref[:, h * HEAD_PAD:(h + 1) * HEAD_PAD] = dk_acc[h]
                dv_ref[:, h * V_HEAD:(h + 1) * V_HEAD] = dv_acc[h]

        @pl.when(t == n_pairs - 1)
        def _write_dq():
            pltpu.sync_copy(dq_acc, dq_hbm.at[b])

        @pl.when(jnp.logical_and(b == B - 1, t == n_pairs - 1))
        def _wait_side():
            side.wait(side_in, side_out, sems)

    T = B * S
    qrow = lambda b, t, kt, qt: (b * nq + qt[t], 0)
    qcol3 = lambda b, t, kt, qt: (0, 0, b * nq + qt[t])
    krow = lambda b, t, kt, qt: (b * nq + kt[t], 0)
    grid_spec = pltpu.PrefetchScalarGridSpec(
        num_scalar_prefetch=2, grid=(B, n_pairs),
        in_specs=[pl.BlockSpec((tq, QK_W), qrow), pl.BlockSpec((tq, QK_W), krow),
                  pl.BlockSpec((QK_W, tq), lambda b, t, kt, qt: (0, b * nq + kt[t])), pl.BlockSpec((tq, V_W), krow),
                  pl.BlockSpec((tq, V_W), qrow), pl.BlockSpec((N_HEADS, 1, tq), qcol3),
                  pl.BlockSpec((N_HEADS, 1, tq), qcol3)] + [ANY_SPEC] * ns,
        out_specs=[ANY_SPEC, pl.BlockSpec((tq, QK_W), krow), pl.BlockSpec((tq, V_W), krow)] + [ANY_SPEC] * ns,
        scratch_shapes=[pltpu.VMEM((N_HEADS, HEAD_PAD, S), F32), pltpu.VMEM((N_HEADS, tq, HEAD_PAD), F32),
                        pltpu.VMEM((N_HEADS, tq, V_HEAD), F32)] + side.scratch())
    return pl.pallas_call(
        kern, name="attention_bwd", grid_spec=grid_spec,
        out_shape=[jax.ShapeDtypeStruct((B, N_HEADS, HEAD_PAD, S), F32), jax.ShapeDtypeStruct((T, QK_W), F32),
                   jax.ShapeDtypeStruct((T, V_W), F32)] + side.out_shapes(),
        compiler_params=_params(("arbitrary", "arbitrary")),
    )(k_tab, q_tab, q, k, kt, v, do, lse, delta, *side.arrays)


GB = 8
N_GB = N_GROUPS // GB
GB_U = GB * GROUP
GB_ST = GB * N_STATE


def _gb_u(j):
    return slice(j * GB_U, (j + 1) * GB_U)


def _gb_s(j):
    return slice(j * GB_ST, (j + 1) * GB_ST)


def ssm_param_fn(lr, li, ld, br, bi):
    dt = jnp.exp(ld)
    er = jnp.exp(lr * dt)
    ar = er * jnp.cos(li * dt)
    ai = er * jnp.sin(li * dt)
    nr = ar - 1.0
    den = lr * lr + li * li
    cr = (nr * lr + ai * li) / den
    ci = (ai * lr - nr * li) / den
    return ar, ai, cr * br - ci * bi, cr * bi + ci * br


def ssm_params_fwd(reps):
    def kern(lr, li, ld, br, bi, ar_o, ai_o, bbr_o, bbi_o):
        ar, ai, bbr, bbi = ssm_param_fn(lr[...], li[...], ld[...], br[...], bi[...])
        ar_o[...] = ar
        ai_o[...] = ai
        bbr_o[...] = bbr
        bbi_o[...] = bbi

    shp = jax.ShapeDtypeStruct(reps[0].shape, F32)
    return pl.pallas_call(kern, name="ssm_params_fwd", out_shape=[shp] * 4)(*reps)


def ssm_params_bwd(reps, cots):
    rows = reps[0].shape[0]

    def kern(lr, li, ld, br, bi, d_ar, d_ai, d_bbr, d_bbi, dlr_o, dli_o, dld_o, dbr_o, dbi_o):
        _, vjp = jax.vjp(ssm_param_fn, lr[...], li[...], ld[...], br[...], bi[...])
        dlr, dli, dld, dbr, dbi = vjp((d_ar[...], d_ai[...], d_bbr[...], d_bbi[...]))
        dlr_o[...] = jnp.sum(dlr.reshape(N_GROUPS, GROUP, N_STATE), axis=1)
        dli_o[...] = jnp.sum(dli.reshape(N_GROUPS, GROUP, N_STATE), axis=1)
        dld_o[...] = jnp.sum(jnp.sum(dld.reshape(N_GROUPS, GROUP, N_STATE), axis=1), axis=-1, keepdims=True)
        dbr_o[...] = dbr
        dbi_o[...] = dbi

    g = jax.ShapeDtypeStruct((N_GROUPS, N_STATE), F32)
    full = jax.ShapeDtypeStruct((rows, N_STATE), F32)
    return pl.pallas_call(
        kern, name="ssm_params_bwd",
        out_shape=[g, g, jax.ShapeDtypeStruct((N_GROUPS, 1), F32), full, full])(*reps, *cots)


def ssm_fwd(proj, b_re, b_im, c_re, c_im, d_vec, a_re, a_im, B, S, ts):
    ns = S // ts
    T = B * S

    def kern(u_ref, bre_ref, bim_ref, cre_ref, cim_ref, d_ref, are_ref, aim_ref,
             sre_ref, sim_ref, y_ref, car_re, car_im, bu_re, bu_im):
        s = pl.program_id(1)

        @pl.when(s == 0)
        def _reset():
            car_re[...] = jnp.zeros(car_re.shape, F32)
            car_im[...] = jnp.zeros(car_im.shape, F32)

        u = u_ref[...]
        ub = u.astype(BF16)
        for j in range(N_GB):
            uj, bj, sj = ub[:, _gb_u(j)], _gb_u(j), _gb_s(j)
            bu_re[:, sj] = _dot(uj, bre_ref[bj, :])
            bu_im[:, sj] = _dot(uj, bim_ref[bj, :])
        ar, ai = are_ref[...], aim_ref[...]

        def step(t, carry):
            xr, xi = carry
            row = pl.ds(t, 1)
            nr = ar * xr - ai * xi + bu_re[row, :]
            ni = ar * xi + ai * xr + bu_im[row, :]
            sre_ref[row, :] = nr
            sim_ref[row, :] = ni
            return nr, ni

        xr, xi = lax.fori_loop(0, ts, step, (car_re[0:1, :], car_im[0:1, :]))
        car_re[0:1, :] = xr
        car_im[0:1, :] = xi
        y = [_dot(sre_ref[:, _gb_s(j)].astype(BF16), cre_ref[_gb_s(j), :])
             - _dot(sim_ref[:, _gb_s(j)].astype(BF16), cim_ref[_gb_s(j), :]) for j in range(N_GB)]
        y_ref[...] = jnp.concatenate(y, axis=1) + d_ref[...] * u

    row = lambda b, s: (b * ns + s, 0)
    whole = lambda b, s: (0, 0)
    return pl.pallas_call(
        kern, name="ssm_fwd", grid=(B, ns),
        in_specs=[pl.BlockSpec((ts, D_SSM), row),
                  pl.BlockSpec((D_SSM, GB_ST), whole), pl.BlockSpec((D_SSM, GB_ST), whole),
                  pl.BlockSpec((N_ST, GB_U), whole), pl.BlockSpec((N_ST, GB_U), whole),
                  pl.BlockSpec((1, D_SSM), whole), pl.BlockSpec((1, N_ST), whole), pl.BlockSpec((1, N_ST), whole)],
        out_specs=[pl.BlockSpec((ts, N_ST), row), pl.BlockSpec((ts, N_ST), row), pl.BlockSpec((ts, D_SSM), row)],
        out_shape=[jax.ShapeDtypeStruct((T, N_ST), F32), jax.ShapeDtypeStruct((T, N_ST), F32),
                   jax.ShapeDtypeStruct((T, D_SSM), F32)],
        scratch_shapes=[pltpu.VMEM((8, N_ST), F32), pltpu.VMEM((8, N_ST), F32),
                        pltpu.VMEM((ts, N_ST), F32), pltpu.VMEM((ts, N_ST), F32)],
        compiler_params=_params(("arbitrary", "arbitrary")),
    )(proj, b_re, b_im, c_re, c_im, d_vec, a_re, a_im)


def ssm_bwd(dy, proj, s_re, s_im, b_re, b_im, c_re, c_im, d_vec, a_re, a_im, B, S, ts):
    ns = S // ts
    T = B * S
    t8 = ts // 8

    def kern(dy_ref, u_ref, sre_ref, sim_ref, pre_ref, pim_ref, bre_ref, bim_ref, cre_ref, cim_ref,
             d_ref, are_ref, aim_ref,
             du_ref, dcre_ref, dcim_ref, dbre_ref, dbim_ref, dare_ref, daim_ref, dd_ref,
             car_re, car_im, g_re, g_im):
        b, s = pl.program_id(0), pl.program_id(1)
        first = jnp.logical_and(b == 0, s == 0)

        @pl.when(s == 0)
        def _reset():
            car_re[...] = jnp.zeros(car_re.shape, F32)
            car_im[...] = jnp.zeros(car_im.shape, F32)

        dyv = dy_ref[...]
        dyb = dyv.astype(BF16)
        u = u_ref[...]
        for j in range(N_GB):
            g_re[:, _gb_s(j)] = _dot_nt(dyb[:, _gb_u(j)], cre_ref[_gb_s(j), :])
            g_im[:, _gb_s(j)] = -_dot_nt(dyb[:, _gb_u(j)], cim_ref[_gb_s(j), :])
        ar, ai = are_ref[...], aim_ref[...]

        def step(i, carry):
            gr_next, gi_next = carry
            row = pl.ds(ts - 1 - i, 1)
            gr = g_re[row, :] + ar * gr_next + ai * gi_next
            gi = g_im[row, :] + ar * gi_next - ai * gr_next
            g_re[row, :] = gr
            g_im[row, :] = gi
            return gr, gi

        gr0, gi0 = lax.fori_loop(0, ts, step, (car_re[0:1, :], car_im[0:1, :]))
        car_re[0:1, :] = gr0
        car_im[0:1, :] = gi0

        gr, gi = g_re[...], g_im[...]
        sr, si = sre_ref[...], sim_ref[...]
        has_prev = (s < ns - 1).astype(F32)
        row_id = lax.broadcasted_iota(jnp.int32, (ts, N_ST), 0)
        spr = jnp.where(row_id == 0, pre_ref[7:8, :] * has_prev, pltpu.roll(sr, 1, axis=0))
        spi = jnp.where(row_id == 0, pim_ref[7:8, :] * has_prev, pltpu.roll(si, 1, axis=0))
        grb, gib, ub = gr.astype(BF16), gi.astype(BF16), u.astype(BF16)
        srb, sib = sr.astype(BF16), si.astype(BF16)
        cat0 = lambda f: jnp.concatenate([f(j) for j in range(N_GB)], axis=0)
        cat1 = lambda f: jnp.concatenate([f(j) for j in range(N_GB)], axis=1)
        parts = [
            (dcre_ref, cat0(lambda j: _dot_tn(srb[:, _gb_s(j)], dyb[:, _gb_u(j)]))),
            (dcim_ref, cat0(lambda j: -_dot_tn(sib[:, _gb_s(j)], dyb[:, _gb_u(j)]))),
            (dbre_ref, cat0(lambda j: _dot_tn(ub[:, _gb_u(j)], grb[:, _gb_s(j)]))),
            (dbim_ref, cat0(lambda j: _dot_tn(ub[:, _gb_u(j)], gib[:, _gb_s(j)]))),
            (dare_ref, _colsum(gr * spr + gi * spi)),
            (daim_ref, _colsum(gi * spr - gr * spi)),
            (dd_ref, _colsum(dyv * u)),
        ]
        du = cat1(lambda j: _dot_nt(grb[:, _gb_s(j)], bre_ref[_gb_u(j), :])
                  + _dot_nt(gib[:, _gb_s(j)], bim_ref[_gb_u(j), :]))
        du_ref[...] = du + d_ref[...] * dyv

        @pl.when(first)
        def _set():
            for ref, val in parts:
                ref[...] = val

        @pl.when(jnp.logical_not(first))
        def _add():
            for ref, val in parts:
                ref[...] = ref[...] + val

    row = lambda b, s: (b * ns + (ns - 1 - s), 0)
    prev = lambda b, s: (jnp.maximum((b * ns + (ns - 1 - s)) * t8 - 1, 0), 0)
    whole = lambda b, s: (0, 0)
    acc_shapes = [(N_ST, GB_U), (N_ST, GB_U), (D_SSM, GB_ST), (D_SSM, GB_ST), (1, N_ST), (1, N_ST), (1, D_SSM)]
    return pl.pallas_call(
        kern, name="ssm_bwd", grid=(B, ns),
        in_specs=[pl.BlockSpec((ts, D_SSM), row), pl.BlockSpec((ts, D_SSM), row),
                  pl.BlockSpec((ts, N_ST), row), pl.BlockSpec((ts, N_ST), row),
                  pl.BlockSpec((8, N_ST), prev), pl.BlockSpec((8, N_ST), prev),
                  pl.BlockSpec((D_SSM, GB_ST), whole), pl.BlockSpec((D_SSM, GB_ST), whole),
                  pl.BlockSpec((N_ST, GB_U), whole), pl.BlockSpec((N_ST, GB_U), whole),
                  pl.BlockSpec((1, D_SSM), whole), pl.BlockSpec((1, N_ST), whole), pl.BlockSpec((1, N_ST), whole)],
        out_specs=[pl.BlockSpec((ts, D_SSM), row)] + [pl.BlockSpec(sh, whole) for sh in acc_shapes],
        out_shape=[jax.ShapeDtypeStruct((T, D_SSM), F32)] + [jax.ShapeDtypeStruct(sh, F32) for sh in acc_shapes],
        scratch_shapes=[pltpu.VMEM((8, N_ST), F32), pltpu.VMEM((8, N_ST), F32),
                        pltpu.VMEM((ts, N_ST), F32), pltpu.VMEM((ts, N_ST), F32)],
        compiler_params=_params(("arbitrary", "arbitrary")),
    )(dy, proj, s_re, s_im, s_re, s_im, b_re, b_im, c_re, c_im, d_vec, a_re, a_im)


def modulation_fwd(c_all, ada_w, ada_b, fada_w, fada_b):
    def kern(c_ref, w_ref, b_ref, fw_ref, fb_ref, cond_ref, mod_ref, fmod_ref):
        cv = c_ref[...]
        cond = (cv * _sigmoid(cv)).astype(BF16)
        cond_ref[...] = cond
        mod_ref[...] = _dot(cond, w_ref[...].astype(BF16)) + b_ref[...]
        fmod_ref[...] = _dot(cond, fw_ref[...].astype(BF16)) + fb_ref[...]

    n = c_all.shape[0]
    return pl.pallas_call(
        kern, name="modulation_fwd",
        out_shape=[jax.ShapeDtypeStruct(c_all.shape, BF16), jax.ShapeDtypeStruct((n, ada_w.shape[1]), F32),
                   jax.ShapeDtypeStruct((n, fada_w.shape[1]), F32)],
        compiler_params=_params(),
    )(c_all, ada_w, ada_b, fada_w, fada_b)


def modulation_bwd(cond_all, dmod, dfmod, dmod_all):
    def kern(c_ref, dm_ref, dfm_ref, all_ref, gw_ref, gfw_ref, gb_ref):
        cond = c_ref[...]
        gw_ref[...] = _dot_tn(cond, dm_ref[...].astype(BF16))
        gfw_ref[...] = _dot_tn(cond, dfm_ref[...].astype(BF16))
        gb_ref[...] = _colsum(all_ref[...].astype(F32))

    d = cond_all.shape[1]
    return pl.pallas_call(
        kern, name="modulation_bwd",
        out_shape=[jax.ShapeDtypeStruct((d, dmod.shape[1]), F32), jax.ShapeDtypeStruct((d, dfmod.shape[1]), F32),
                   jax.ShapeDtypeStruct((1, dmod_all.shape[1]), F32)],
        compiler_params=_params(),
    )(cond_all, dmod, dfmod, dmod_all)


def adamw(name, parts, w, m, v):
    n, R, C = parts.shape
    tr = R
    while n * tr * C * 4 > 4 * 1024 * 1024 and tr % 32 == 0:
        tr //= 2

    def kern(p_ref, w_ref, m_ref, v_ref, g_o, d_o, m_o, v_o):
        g = p_ref[0].astype(F32)
        for i in range(1, n):
            g = g + p_ref[i].astype(F32)
        m_new = ADAM_B1 * m_ref[...] + (1.0 - ADAM_B1) * g
        v_new = ADAM_B2 * v_ref[...] + (1.0 - ADAM_B2) * (g * g)
        m_hat = m_new / (1.0 - ADAM_B1 ** ADAM_STEP)
        v_hat = v_new / (1.0 - ADAM_B2 ** ADAM_STEP)
        g_o[...] = g
        d_o[...] = -ADAM_LR * (m_hat / (jnp.sqrt(v_hat) + ADAM_EPS) + ADAM_WD * w_ref[...])
        m_o[...] = m_new
        v_o[...] = v_new

    blk = pl.BlockSpec((tr, C), lambda i: (i, 0))
    shp = jax.ShapeDtypeStruct((R, C), F32)
    return pl.pallas_call(
        kern, name=name, grid=(R // tr,),
        in_specs=[pl.BlockSpec((n, tr, C), lambda i: (0, i, 0)), blk, blk, blk],
        out_specs=[blk] * 4, out_shape=[shp] * 4,
        compiler_params=_params(("parallel",)),
    )(parts, w, m, v)


def _cols_from_gathered(g):
    return jnp.transpose(g, (1, 0, 2)).reshape(g.shape[1], N_DEV * g.shape[2])


def _cols_to_shards(w):
    k, n8 = w.shape
    return jnp.transpose(w.reshape(k, N_DEV, n8 // N_DEV), (1, 0, 2))


def _pad_w_in(w):
    z = functools.partial(jnp.zeros, dtype=w.dtype)
    k = w.shape[0]
    cut = D_SSM + Q_LORA + KV_LORA
    return jnp.concatenate([w[:, :cut], z((k, 64)), w[:, cut:], z((k, 32))], axis=1)


def _unpad_w_in(w):
    cut = D_SSM + Q_LORA + KV_LORA
    return jnp.concatenate([w[:, :cut], w[:, cut + 64:cut + 96]], axis=1)


def _pad_w_uq(w):
    k = w.shape[0]
    w3 = w.reshape(k, N_HEADS, QK_NOPE + QK_ROPE)
    return jnp.pad(w3, ((0, 0), (0, 0), (0, HEAD_PAD - QK_NOPE - QK_ROPE))).reshape(k, QK_W)


def _unpad_w_uq(w):
    k = w.shape[0]
    return w.reshape(k, N_HEADS, HEAD_PAD)[:, :, :QK_NOPE + QK_ROPE].reshape(k, N_HEADS * (QK_NOPE + QK_ROPE))


def _pad_w_ukv(w):
    k = w.shape[0]
    w3 = w.reshape(k, N_HEADS, QK_NOPE + V_HEAD)
    kpart = jnp.pad(w3[:, :, :QK_NOPE], ((0, 0), (0, 0), (0, HEAD_PAD - QK_NOPE))).reshape(k, QK_W)
    vpart = w3[:, :, QK_NOPE:].reshape(k, V_W)
    return jnp.concatenate([kpart, vpart], axis=1)


def _unpad_w_ukv(w):
    k = w.shape[0]
    kpart = w[:, :QK_W].reshape(k, N_HEADS, HEAD_PAD)[:, :, :QK_NOPE]
    vpart = w[:, QK_W:].reshape(k, N_HEADS, V_HEAD)
    return jnp.concatenate([kpart, vpart], axis=2).reshape(k, N_HEADS * (QK_NOPE + V_HEAD))


def _block_diag(blocks):
    g, r, c = blocks.shape
    b4 = blocks.reshape(N_GB, GB, r, c)
    keep = jnp.eye(GB, dtype=bool)[None, :, None, :, None]
    return jnp.where(keep, b4[:, :, :, None, :], 0).reshape(g * r, GB * c)


def _block_diag_take(stacked, r, c):
    d5 = stacked.reshape(N_GB, GB, r, GB, c)
    idx = jnp.arange(GB)
    return jnp.transpose(d5[:, idx, :, idx, :], (1, 0, 2, 3)).reshape(N_GROUPS, r, c)


SMALL = ["norm1_g", "ssm_lambda_re", "ssm_lambda_im", "ssm_b_re", "ssm_b_im", "ssm_c_re", "ssm_c_im",
         "ssm_d", "ssm_log_dt", "q_norm_g", "kv_norm_g", "ssm_out_g", "attn_out_g", "norm2_g", "final_norm_g"]
BIASES = ["ada_b", "final_ada_b"]
SMALL_COLS = 1024


def _pack_small(values):
    flat = jnp.concatenate([v.reshape(-1) for v in values])
    rows = -(-flat.shape[0] // (8 * SMALL_COLS)) * 8
    return jnp.pad(flat, (0, rows * SMALL_COLS - flat.shape[0])).reshape(rows, SMALL_COLS)


def _unpack_small(packed, like):
    flat = packed.reshape(-1)
    out, off = [], 0
    for ref in like:
        n = int(np.prod(ref.shape))
        out.append(flat[off:off + n].reshape(ref.shape))
        off += n
    return out


def kernel(x, c, positions, ada_w, ada_b, norm1_g, w_in, ssm_lambda_re, ssm_lambda_im, ssm_b_re, ssm_b_im, ssm_c_re, ssm_c_im, ssm_d, ssm_log_dt, w_glu, q_norm_g, w_uq, kv_norm_g, w_ukv, ssm_out_g, attn_out_g, w_out, norm2_g, w_ff1, w_ff2, final_ada_w, final_ada_b, final_norm_g, loss_target, m_ada_w, m_ada_b, m_norm1_g, m_w_in, m_ssm_lambda_re, m_ssm_lambda_im, m_ssm_b_re, m_ssm_b_im, m_ssm_c_re, m_ssm_c_im, m_ssm_d, m_ssm_log_dt, m_w_glu, m_q_norm_g, m_w_uq, m_kv_norm_g, m_w_ukv, m_ssm_out_g, m_attn_out_g, m_w_out, m_norm2_g, m_w_ff1, m_w_ff2, m_final_ada_w, m_final_ada_b, m_final_norm_g, v_ada_w, v_ada_b, v_norm1_g, v_w_in, v_ssm_lambda_re, v_ssm_lambda_im, v_ssm_b_re, v_ssm_b_im, v_ssm_c_re, v_ssm_c_im, v_ssm_d, v_ssm_log_dt, v_w_glu, v_q_norm_g, v_w_uq, v_kv_norm_g, v_w_ukv, v_ssm_out_g, v_attn_out_g, v_w_out, v_norm2_g, v_w_ff1, v_w_ff2, v_final_ada_w, v_final_ada_b, v_final_norm_g):
    given = dict(locals())
    B, S, D = x.shape
    T = B * S
    tm = min(256, S)
    me = 4 * lax.axis_index("x") + 2 * lax.axis_index("y") + lax.axis_index("c")

    big = ["w_in", "w_glu", "w_uq", "w_ukv", "w_out", "w_ff1", "w_ff2"]
    shards = {n: given[n][0] for n in big}
    early, late = ["w_in", "w_uq", "w_ukv"], ["w_glu", "w_out", "w_ff1", "w_ff2"]
    gathered = all_gather("gather_first_weights", [c] + [shards[n].astype(BF16) for n in early])
    c_all = gathered[0].reshape(N_DEV * B, D)
    gw = dict(zip(early, gathered[1:]))
    w_in_p = _pad_w_in(_cols_from_gathered(gw["w_in"]))
    w_uq_p = _pad_w_uq(_cols_from_gathered(gw["w_uq"]))
    w_ukv_p = _pad_w_ukv(_cols_from_gathered(gw["w_ukv"]))

    n_mod = ada_w.shape[2]
    n_fmod = final_ada_w.shape[1]
    ada_b_sh = lax.dynamic_slice_in_dim(ada_b, me * n_mod, n_mod, axis=1)
    fada_b_sh = lax.dynamic_slice_in_dim(final_ada_b[None, :], me * n_fmod, n_fmod, axis=1)
    cond_all, mod_sh, fmod_sh = modulation_fwd(c_all, ada_w[0], ada_b_sh, final_ada_w, fada_b_sh)
    mod_g, fmod_g = all_gather("gather_modulation", [mod_sh, fmod_sh])
    mod_mine = lax.dynamic_slice_in_dim(_cols_from_gathered(mod_g), me * B, B, axis=0)
    fmod_mine = lax.dynamic_slice_in_dim(_cols_from_gathered(fmod_g), me * B, B, axis=0)
    shift1, scale1, gate1, shift2, scale2, gate2 = [mod_mine[:, None, i * D:(i + 1) * D] for i in range(6)]
    fshift, fscale = fmod_mine[:, None, :D], fmod_mine[:, None, D:]

    x2d = x.reshape(T, D)
    tgt2d = loss_target.reshape(T, D)
    pos = positions.astype(F32).reshape(T, 1)
    invf, m_lo, m_hi = _rope_consts()
    rope_mask = m_lo + m_hi

    def fwd_in(rows, bv, vecs, res):
        (xv,), (sc, sh), (g1,), (w,) = rows, bv, vecs, res
        r, xh = _rms_stats(xv)
        h = xh * g1 * (1.0 + sc) + sh
        return [_dot(h.astype(BF16), w[...])], [], []

    (proj,), _, _ = row_call("fwd_in", fwd_in, B, S, tm, [x2d], [scale1, shift1], [norm1_g], [w_in_p],
                             [(IN_PAD, F32)], [], [])

    def fwd_qkv(rows, bv, vecs, res):
        (pr, ps), (gq, gkv, fr, lo, hi), (wq, wkv) = rows, vecs, res
        cs, s_lo, s_hi = _rope_tables(ps, fr, lo, hi)
        _, qh = _rms_stats(pr[:, D_SSM:D_SSM + Q_LORA])
        _, kvh = _rms_stats(pr[:, D_SSM + Q_LORA:D_SSM + Q_LORA + KV_LORA])
        qf = _dot((qh * gq).astype(BF16), wq[...])
        kvf = _dot((kvh * gkv).astype(BF16), wkv[...])
        k_rope = _rope_tile(pr[:, IN_PAD - HEAD_PAD:], cs, s_lo, s_hi, 1.0)
        q_t, k_t = [], []
        for h in range(N_HEADS):
            hs = slice(h * HEAD_PAD, (h + 1) * HEAD_PAD)
            q_t.append(_rope_tile(qf[:, hs], cs, s_lo, s_hi, 1.0))
            k_t.append(kvf[:, hs] + k_rope)
        kf, vf = jnp.concatenate(k_t, axis=1), kvf[:, QK_W:]
        return [jnp.concatenate(q_t, axis=1), kf, vf, t_val(kf), t_val(vf)], [], []

    ns = S // tm
    col_tile = lambda b, s: (0, b * ns + s)
    t_out = lambda w: ((w, T), BF16, (w, tm), col_tile)
    t_val = lambda v: v.astype(BF16).T
    (q_r, k_r, v_r, k_t, v_t), _, _ = row_call(
        "fwd_qkv", fwd_qkv, B, S, tm, [proj, pos], [], [q_norm_g, kv_norm_g, invf, m_lo, m_hi],
        [w_uq_p, w_ukv_p],
        [(QK_W, BF16), (QK_W, BF16), (V_W, BF16), ((QK_W, T), BF16, (QK_W, tm), col_tile),
         ((V_W, T), BF16, (V_W, tm), col_tile)], [], [])
    late_gather = Exchange([shards[n].astype(BF16) for n in late], [GATHER] * len(late))
    y_attn, lse, *late_g = attention_fwd(q_r, k_r, v_t, B, S, tm, late_gather)
    gw = dict(zip(late, late_g))
    w_glu_f = _cols_from_gathered(gw["w_glu"])
    w_out_f = gw["w_out"].reshape(-1, D)
    w_ff1_f = _cols_from_gathered(gw["w_ff1"])
    w_ff2_f = gw["w_ff2"].reshape(-1, D)

    rep = lambda a: jnp.repeat(a, GROUP, axis=0)
    lam_rep = [rep(ssm_lambda_re[0]), rep(ssm_lambda_im[0]),
               rep(jnp.broadcast_to(ssm_log_dt[0][:, None], (N_GROUPS, N_STATE)))]
    bt = lambda a: jnp.transpose(a, (0, 2, 1)).reshape(D_SSM, N_STATE)
    reps = lam_rep + [bt(ssm_b_re[0]), bt(ssm_b_im[0])]
    a_re_rep, a_im_rep, bb_re, bb_im = ssm_params_fwd(reps)
    a_re = a_re_rep.reshape(N_GROUPS, GROUP, N_STATE)[:, 0, :].reshape(1, N_ST)
    a_im = a_im_rep.reshape(N_GROUPS, GROUP, N_STATE)[:, 0, :].reshape(1, N_ST)
    bbd_re = _block_diag(bb_re.reshape(N_GROUPS, GROUP, N_STATE)).astype(BF16)
    bbd_im = _block_diag(bb_im.reshape(N_GROUPS, GROUP, N_STATE)).astype(BF16)
    cbd_re = _block_diag(jnp.transpose(ssm_c_re[0], (0, 2, 1))).astype(BF16)
    cbd_im = _block_diag(jnp.transpose(ssm_c_im[0], (0, 2, 1))).astype(BF16)
    d_vec = ssm_d.reshape(1, D_SSM)
    st_re, st_im, y_pre = ssm_fwd(proj, bbd_re, bbd_im, cbd_re, cbd_im, d_vec, a_re, a_im, B, S, tm)

    def fwd_mix(rows, bv, vecs, res):
        (yp, ya, xv), (g1v,), (gso, gao), (wg, wo) = rows, bv, vecs, res
        z = _dot(_gelu(yp).astype(BF16), wg[...])
        y_ssm = z[:, :D_SSM] * _sigmoid(z[:, D_SSM:])
        _, sh = _rms_stats(y_ssm)
        _, ah = _rms_stats(ya)
        o = _dot((sh * gso).astype(BF16), wo[0:D_SSM, :]) + _dot((ah * gao).astype(BF16), wo[D_SSM:, :])
        return [z, o, xv + g1v * o], [], []

    (z, o, x2), _, _ = row_call("fwd_mix", fwd_mix, B, S, tm, [y_pre, y_attn, x2d], [gate1],
                                [ssm_out_g, attn_out_g], [w_glu_f, w_out_f],
                                [(2 * D_SSM, F32), (D, F32), (D, F32)], [], [])

    def final_out(x3, gf, fsc, fsh):
        r3, xh3 = _rms_stats(x3)
        n3 = xh3 * gf
        return r3, xh3, n3, n3 * (1.0 + fsc) + fsh

    def mlp_hidden(xv, g2, sc, sh, w1):
        r2, xh2 = _rms_stats(xv)
        n2 = xh2 * g2
        h2 = n2 * (1.0 + sc) + sh
        return r2, xh2, n2, h2, _dot(h2.astype(BF16), w1[...])

    def fwd_mlp(rows, bv, vecs, res):
        (xv, tg), (sc, sh, g2v, fsc, fsh), (g2, gf), (w1, w2) = rows, bv, vecs, res
        _, _, _, _, a = mlp_hidden(xv, g2, sc, sh, w1)
        ra = jnp.maximum(a, 0.0)
        ff = _dot((ra * ra).astype(BF16), w2[...])
        x3 = xv + g2v * ff
        _, _, _, out = final_out(x3, gf, fsc, fsh)
        err = out - tg
        loss = 0.5 * jnp.sum(jnp.mean(err * err, axis=-1, keepdims=True), axis=0, keepdims=True)
        return [ff, x3], [], [jnp.broadcast_to(loss, (1, 128))]

    fnorm_g = final_norm_g[None, :]
    (ff, x3), _, (loss_acc,) = row_call(
        "fwd_mlp", fwd_mlp, B, S, tm, [x2, tgt2d], [scale2, shift2, gate2, fscale, fshift], [norm2_g, fnorm_g],
        [w_ff1_f, w_ff2_f], [(D, F32), (D, F32)], [], [(1, 128)])
    loss = lax.psum(loss_acc[0, 0], ("x", "y", "c"))

    def bwd_mlp(rows, bv, vecs, res):
        (x3v, tg, x2v, ffv), (sc, sh, g2v, fsc, fsh), (g2, gf), (w1, w2) = rows, bv, vecs, res
        r3, xh3, n3, out = final_out(x3v, gf, fsc, fsh)
        dout = (out - tg) * (1.0 / D)
        dn3 = dout * (1.0 + fsc)
        dx3 = _rms_bwd(dn3 * gf, xh3, r3)
        dff = dx3 * g2v
        r2, xh2, n2, h2, a = mlp_hidden(x2v, g2, sc, sh, w1)
        ra = jnp.maximum(a, 0.0)
        dffb = dff.astype(BF16)
        da = _dot_nt(dffb, w2[...]) * (2.0 * ra)
        dab = da.astype(BF16)
        dh2 = _dot_nt(dab, w1[...])
        dn2 = dh2 * (1.0 + sc)
        dx2 = dx3 + _rms_bwd(dn2 * g2, xh2, r2)
        return ([dx2, t_val(h2), t_val(ra * ra), dab, dffb],
                [_colsum(dout), _colsum(dout * n3), _colsum(dx3 * ffv), _colsum(dh2), _colsum(dh2 * n2)],
                [_colsum(dn3 * xh3), _colsum(dn2 * xh2)])

    ((dx2, h2b, fb, dab, dffb), (d_fshift, d_fscale, d_gate2, d_shift2, d_scale2), (d_gf, d_g2)) = row_call(
        "bwd_mlp", bwd_mlp, B, S, tm, [x3, tgt2d, x2, ff], [scale2, shift2, gate2, fscale, fshift],
        [norm2_g, fnorm_g], [w_ff1_f, w_ff2_f],
        [(D, F32), t_out(D), t_out(4 * D), (4 * D, BF16), (D, BF16)], [D] * 5, [(1, D), (1, D)])
    g_w_ff1 = matmul_nn("grad_w_ff1", h2b, dab)
    g_w_ff2 = matmul_nn("grad_w_ff2", fb, dffb)

    def bwd_mix(rows, bv, vecs, res):
        (dxv, ov, zv, ya, yp), (g1v,), (gso, gao), (wo, wg) = rows, bv, vecs, res
        do = (dxv * g1v).astype(BF16)
        dyn = _dot_nt(do, wo[...])
        z1, sg = zv[:, :D_SSM], _sigmoid(zv[:, D_SSM:])
        y_ssm = z1 * sg
        rs, sh = _rms_stats(y_ssm)
        ra, ah = _rms_stats(ya)
        dyn_s, dyn_a = dyn[:, :D_SSM], dyn[:, D_SSM:]
        dy_ssm = _rms_bwd(dyn_s * gso, sh, rs)
        dy_attn = _rms_bwd(dyn_a * gao, ah, ra)
        dz = jnp.concatenate([dy_ssm * sg, dy_ssm * z1 * sg * (1.0 - sg)], axis=1).astype(BF16)
        dy_pre = _dot_nt(dz, wg[...]) * _gelu_grad(yp)
        yn = jnp.concatenate([sh * gso, ah * gao], axis=1)
        delta = jnp.sum((dy_attn * ya).T.reshape(N_HEADS, V_HEAD, tm), axis=1, keepdims=True)
        return ([do, t_val(yn), dz, t_val(_gelu(yp)), dy_attn, dy_pre, delta], [_colsum(dxv * ov)],
                [_colsum(dyn_s * sh), _colsum(dyn_a * ah)])

    ((dob, ynb, dzb, ygb, dy_attn, dy_pre, delta), (d_gate1,), (d_gso, d_gao)) = row_call(
        "bwd_mix", bwd_mix, B, S, tm, [dx2, o, z, y_attn, y_pre], [gate1], [ssm_out_g, attn_out_g],
        [w_out_f, w_glu_f],
        [(D, BF16), t_out(D), (2 * D_SSM, BF16), t_out(D_SSM), (V_W, F32), (D_SSM, F32),
         ((N_HEADS, 1, T), F32, (N_HEADS, 1, tm), lambda b, s: (0, 0, b * ns + s))],
        [D], [(1, D_SSM), (1, V_W)])
    g_w_out = matmul_nn("grad_w_out", ynb, dob)
    g_w_glu = matmul_nn("grad_w_glu", ygb, dzb)

    grads_a = Exchange([g_w_glu, g_w_out.reshape(N_DEV, -1, D), g_w_ff1, g_w_ff2.reshape(N_DEV, -1, D)],
                       [w_glu.shape[2], None, w_ff1.shape[2], None])
    dq_t, dk_r, dv_r, *parts_a = attention_bwd(q_r, k_r, k_t, v_r, dy_attn, lse, delta, B, S, tm, grads_a)
    parts = dict(zip(late, parts_a))
    dq_t = dq_t.reshape(B, QK_W, S)

    def bwd_qkv(rows, bv, vecs, res):
        (dqt, dkv, dvv, pr, ps), (gq, gkv, fr, lo, hi, rmask), (wq, wkv) = rows, vecs, res
        dqv = dqt[0].T
        cs, s_lo, s_hi = _rope_tables(ps, fr, lo, hi)
        dq_t = []
        dk_rope = jnp.zeros((dkv.shape[0], HEAD_PAD), F32)
        for h in range(N_HEADS):
            hs = slice(h * HEAD_PAD, (h + 1) * HEAD_PAD)
            dq_t.append(_rope_tile(dqv[:, hs], cs, s_lo, s_hi, -1.0))
            dk_rope = dk_rope + dkv[:, hs]
        dk_rope = _rope_tile(dk_rope * rmask, cs, s_lo, s_hi, -1.0)
        dqb = jnp.concatenate(dq_t, axis=1).astype(BF16)
        dkvb = jnp.concatenate([dkv, dvv], axis=1).astype(BF16)
        rq, qh = _rms_stats(pr[:, D_SSM:D_SSM + Q_LORA])
        rk, kvh = _rms_stats(pr[:, D_SSM + Q_LORA:D_SSM + Q_LORA + KV_LORA])
        dqn = _dot_nt(dqb, wq[...])
        dkvn = _dot_nt(dkvb, wkv[...])
        dpa = jnp.concatenate([_rms_bwd(dqn * gq, qh, rq), _rms_bwd(dkvn * gkv, kvh, rk), dk_rope], axis=1)
        return [dpa, t_val(qh * gq), t_val(kvh * gkv), dqb, dkvb], [], [_colsum(dqn * qh), _colsum(dkvn * kvh)]

    ((dpa, qnb, kvnb, dqb, dkvb), _, (d_gq, d_gkv)) = row_call(
        "bwd_qkv", bwd_qkv, B, S, tm,
        [(dq_t, (1, QK_W, tm), lambda b, s: (b, 0, s)), dk_r, dv_r, proj, pos], [],
        [q_norm_g, kv_norm_g, invf, m_lo, m_hi, rope_mask], [w_uq_p, w_ukv_p],
        [(Q_LORA + KV_LORA + HEAD_PAD, F32), t_out(Q_LORA), t_out(KV_LORA), (QK_W, BF16), (QK_W + V_W, BF16)],
        [], [(1, Q_LORA), (1, KV_LORA)])
    g_w_uq = _unpad_w_uq(matmul_nn("grad_w_uq", qnb, dqb))
    g_w_ukv = _unpad_w_ukv(matmul_nn("grad_w_ukv", kvnb, dkvb))

    du, dc_re_d, dc_im_d, db_re_d, db_im_d, da_re, da_im, d_d = ssm_bwd(
        dy_pre, proj, st_re, st_im, bbd_re, bbd_im, cbd_re, cbd_im, d_vec, a_re, a_im, B, S, tm)
    place = lambda a: jnp.zeros((N_GROUPS, GROUP, N_STATE), F32).at[:, 0, :].set(
        a.reshape(N_GROUPS, N_STATE)).reshape(D_SSM, N_STATE)
    cots = [place(da_re), place(da_im),
            _block_diag_take(db_re_d, GROUP, N_STATE).reshape(D_SSM, N_STATE),
            _block_diag_take(db_im_d, GROUP, N_STATE).reshape(D_SSM, N_STATE)]
    g_lam_re, g_lam_im, g_log_dt, g_bt_re, g_bt_im = ssm_params_bwd(reps, cots)
    unbt = lambda a: jnp.transpose(a.reshape(N_GROUPS, GROUP, N_STATE), (0, 2, 1))
    g_c_re = jnp.transpose(_block_diag_take(dc_re_d, N_STATE, GROUP), (0, 2, 1))
    g_c_im = jnp.transpose(_block_diag_take(dc_im_d, N_STATE, GROUP), (0, 2, 1))

    def bwd_in(rows, bv, vecs, res):
        (duv, dpav, xv, dx2v), (sc, sh), (g1,), (w,) = rows, bv, vecs, res
        dproj = jnp.concatenate([duv, dpav], axis=1).astype(BF16)
        dh = _dot_nt(dproj, w[...])
        r, xh = _rms_stats(xv)
        n1 = xh * g1
        dn1 = dh * (1.0 + sc)
        dx = dx2v + _rms_bwd(dn1 * g1, xh, r)
        return [dx, t_val(n1 * (1.0 + sc) + sh), dproj], [_colsum(dh), _colsum(dh * n1)], [_colsum(dn1 * xh)]

    ((grad_x, h1b, dprojb), (d_shift1, d_scale1), (d_g1,)) = row_call(
        "bwd_in", bwd_in, B, S, tm, [du, dpa, x2d, dx2], [scale1, shift1], [norm1_g], [w_in_p],
        [(D, F32), t_out(D), (IN_PAD, BF16)], [D, D], [(1, D)])
    g_w_in = _unpad_w_in(matmul_nn("grad_w_in", h1b, dprojb))

    dmod = jnp.concatenate([d_shift1, d_scale1, d_gate1, d_shift2, d_scale2, d_gate2, d_fshift, d_fscale],
                           axis=2).reshape(B, 8 * D)
    small_part = {
        "norm1_g": d_g1, "ssm_lambda_re": g_lam_re,
        "ssm_lambda_im": g_lam_im, "ssm_b_re": unbt(g_bt_re), "ssm_b_im": unbt(g_bt_im), "ssm_c_re": g_c_re,
        "ssm_c_im": g_c_im, "ssm_d": d_d, "ssm_log_dt": g_log_dt, "q_norm_g": d_gq, "kv_norm_g": d_gkv,
        "ssm_out_g": d_gso, "attn_out_g": d_gao, "norm2_g": d_g2, "final_norm_g": d_gf}
    small_packed = _pack_small([small_part[n] for n in SMALL])
    dmod_g, small_g, *parts_b = exchange(
        "exchange_last_grads",
        [dmod.astype(BF16), small_packed.astype(BF16), _cols_to_shards(g_w_in), _cols_to_shards(g_w_uq), g_w_ukv],
        [GATHER, GATHER, None, None, w_ukv.shape[2]])
    parts.update(zip(early, parts_b))
    dmod_all = dmod_g.reshape(N_DEV * B, 8 * D)
    dm_sh = lax.dynamic_slice_in_dim(dmod_all[:, :6 * D], me * n_mod, n_mod, axis=1)
    dfm_sh = lax.dynamic_slice_in_dim(dmod_all[:, 6 * D:], me * n_fmod, n_fmod, axis=1)
    g_ada_w, g_fada_w, g_biases = modulation_bwd(cond_all, dm_sh, dfm_sh, dmod_all)
    parts["ada_w"] = g_ada_w[None]
    parts["final_ada_w"] = g_fada_w[None]

    res_g, res_d, res_m, res_v = {}, {}, {}, {}
    for n in ["ada_w"] + big + ["final_ada_w"]:
        w_full = given[n]
        w2 = w_full.reshape(w_full.shape[-2], w_full.shape[-1])
        out = adamw("adamw_" + n, parts[n], w2, given["m_" + n].reshape(w2.shape), given["v_" + n].reshape(w2.shape))
        res_g[n], res_d[n], res_m[n], res_v[n] = [a.reshape(w_full.shape) for a in out]
    small_w = [given[n] for n in SMALL]
    out = adamw("adamw_small", small_g, _pack_small(small_w), _pack_small([given["m_" + n] for n in SMALL]),
                _pack_small([given["v_" + n] for n in SMALL]))
    for res, packed in zip((res_g, res_d, res_m, res_v), out):
        for n, val in zip(SMALL, _unpack_small(packed, small_w)):
            res[n] = val
    bias_w = [given[n] for n in BIASES]
    out = adamw("adamw_biases", g_biases.reshape(1, 8, D), _pack_small(bias_w),
                _pack_small([given["m_" + n] for n in BIASES]), _pack_small([given["v_" + n] for n in BIASES]))
    for res, packed in zip((res_g, res_d, res_m, res_v), out):
        for n, val in zip(BIASES, _unpack_small(packed, bias_w)):
            res[n] = val

    order = ["ada_w", "ada_b", "norm1_g", "w_in", "ssm_lambda_re", "ssm_lambda_im", "ssm_b_re", "ssm_b_im",
             "ssm_c_re", "ssm_c_im", "ssm_d", "ssm_log_dt", "w_glu", "q_norm_g", "w_uq", "kv_norm_g", "w_ukv",
             "ssm_out_g", "attn_out_g", "w_out", "norm2_g", "w_ff1", "w_ff2", "final_ada_w", "final_ada_b",
             "final_norm_g"]
    return (loss, grad_x.reshape(B, S, D), *[res_g[n] for n in order], *[res_d[n] for n in order],
            *[res_m[n] for n in order], *[res_v[n] for n in order])
```

```python
import functools
import math

import numpy as np
import jax
import jax.numpy as jnp
from jax import lax
from jax.experimental import pallas as pl
from jax.experimental.pallas import tpu as pltpu

F32 = jnp.float32
BF16 = jnp.bfloat16

N_DEV = 8
EPS = 1e-6
D_SSM = 512
N_GROUPS = 32
GROUP = 16
N_STATE = 64
N_ST = N_GROUPS * N_STATE
Q_LORA = 384
KV_LORA = 256
QK_NOPE = 64
QK_ROPE = 32
V_HEAD = 64
N_HEADS = 8
HEAD_PAD = 128
QK_W = N_HEADS * HEAD_PAD
V_W = N_HEADS * V_HEAD
IN_COLS = D_SSM + Q_LORA + KV_LORA + QK_ROPE
IN_PAD = D_SSM + Q_LORA + KV_LORA + HEAD_PAD
ROPE_BASE = 10000.0
ATTN_SCALE = (QK_NOPE + QK_ROPE) ** -0.5
NEG = -1e30

ADAM_LR = 0.001
ADAM_B1 = 0.9
ADAM_B2 = 0.999
ADAM_EPS = 1e-08
ADAM_WD = 0.01
ADAM_STEP = 10

VMEM_LIMIT = 56 * 1024 * 1024
MESH_ID = pl.DeviceIdType.MESH


def _dot(a, b):
    return jnp.dot(a, b, preferred_element_type=F32)


def _dot_nt(a, b):
    return lax.dot_general(a, b, (((1,), (1,)), ((), ())), preferred_element_type=F32)


def _dot_tn(a, b):
    return lax.dot_general(a, b, (((0,), (0,)), ((), ())), preferred_element_type=F32)


def _rms_stats(x):
    r = lax.rsqrt(jnp.mean(x * x, axis=-1, keepdims=True) + EPS)
    return r, x * r


def _rms_bwd(dy, xh, r):
    return r * (dy - xh * jnp.mean(dy * xh, axis=-1, keepdims=True))


def _sigmoid(x):
    return 1.0 / (1.0 + jnp.exp(-x))


_GELU_C = math.sqrt(2.0 / math.pi)


def _gelu(x):
    t = jnp.tanh(_GELU_C * (x + 0.044715 * x * x * x))
    return 0.5 * x * (1.0 + t)


def _gelu_grad(x):
    t = jnp.tanh(_GELU_C * (x + 0.044715 * x * x * x))
    return 0.5 * (1.0 + t) + 0.5 * x * (1.0 - t * t) * _GELU_C * (1.0 + 3.0 * 0.044715 * x * x)


def _colsum(a):
    return jnp.sum(a, axis=0, keepdims=True)


def _params(sem=None):
    return pltpu.CompilerParams(dimension_semantics=sem, vmem_limit_bytes=VMEM_LIMIT)


def _mesh_pos():
    x, y, c = lax.axis_index("x"), lax.axis_index("y"), lax.axis_index("c")
    return x, y, c, 4 * x + 2 * y + c


def _peer(x, y, c, k):
    px = 1 - x if k & 4 else x
    py = 1 - y if k & 2 else y
    pc = 1 - c if k & 1 else c
    return (px, py, pc), 4 * px + 2 * py + pc


GATHER = "gather"
ANY_SPEC = pl.BlockSpec(memory_space=pl.ANY)


class Exchange:
    def __init__(self, arrays, kinds):
        self.arrays, self.kinds, self.n = list(arrays), list(kinds), len(arrays)

    def _shard(self, i):
        a, kind = self.arrays[i], self.kinds[i]
        if kind == GATHER:
            return a.shape
        return a.shape[1:] if kind is None else (a.shape[0], kind)

    def out_shapes(self):
        return [jax.ShapeDtypeStruct((N_DEV,) + tuple(self._shard(i)), self.arrays[i].dtype) for i in range(self.n)]

    def scratch(self):
        return [pltpu.SemaphoreType.DMA((self.n, N_DEV - 1)), pltpu.SemaphoreType.DMA((self.n, N_DEV - 1)),
                pltpu.SemaphoreType.DMA((self.n,))]

    def _src(self, ins, i, j):
        kind = self.kinds[i]
        if kind == GATHER:
            return ins[i]
        if kind is None:
            return ins[i].at[j]
        return ins[i].at[:, pl.ds(pl.multiple_of(j * kind, 128), kind)]

    def _copies(self, ins, outs, sems, with_received):
        send_sems, recv_sems, loc_sems = sems
        x, y, c, me = _mesh_pos()
        local, sent, received = [], [], []
        for i in range(self.n):
            local.append(pltpu.make_async_copy(self._src(ins, i, me), outs[i].at[me], loc_sems.at[i]))
            for k in range(1, N_DEV):
                peer, pidx = _peer(x, y, c, k)
                pair = dict(send_sem=send_sems.at[i, k - 1], recv_sem=recv_sems.at[i, k - 1], device_id=peer,
                            device_id_type=MESH_ID)
                sent.append(pltpu.make_async_remote_copy(
                    src_ref=self._src(ins, i, pidx), dst_ref=outs[i].at[me], **pair))
                if with_received:
                    received.append(pltpu.make_async_remote_copy(
                        src_ref=self._src(ins, i, pidx), dst_ref=outs[i].at[pidx], **pair))
        return local, sent, received

    def start(self, ins, outs, sems):
        local, sent, _ = self._copies(ins, outs, sems, False)
        for cp in local + sent:
            cp.start()

    def wait(self, ins, outs, sems):
        local, sent, received = self._copies(ins, outs, sems, True)
        for cp in received:
            cp.wait_recv()
        for cp in sent:
            cp.wait_send()
        for cp in local:
            cp.wait()


def exchange(name, arrays, kinds):
    ex = Exchange(arrays, kinds)
    n = ex.n

    def body(*refs):
        ins, outs, sems = refs[:n], refs[n:2 * n], refs[2 * n:]
        ex.start(ins, outs, sems)
        ex.wait(ins, outs, sems)

    return pl.pallas_call(
        body, name=name, out_shape=ex.out_shapes(), in_specs=[ANY_SPEC] * n, out_specs=[ANY_SPEC] * n,
        scratch_shapes=ex.scratch())(*arrays)


def all_gather(name, arrays):
    return exchange(name, arrays, [GATHER] * len(arrays))


def row_call(name, body, B, S, tm, rows, bvecs, vecs, res, row_outs, bacc, gacc):
    ns = S // tm
    T = B * S
    n_r, n_b, n_v, n_w = len(rows), len(bvecs), len(vecs), len(res)
    n_ro, n_ba, n_ga = len(row_outs), len(bacc), len(gacc)

    def kern(*refs):
        i = 0
        r_refs = refs[i:i + n_r]; i += n_r
        b_refs = refs[i:i + n_b]; i += n_b
        v_refs = refs[i:i + n_v]; i += n_v
        w_hbm = refs[i:i + n_w]; i += n_w
        ro_refs = refs[i:i + n_ro]; i += n_ro
        ba_refs = refs[i:i + n_ba]; i += n_ba
        ga_refs = refs[i:i + n_ga]; i += n_ga
        w_vmem = refs[i:i + n_w]
        b = pl.program_id(0)
        s = pl.program_id(1)
        first = jnp.logical_and(b == 0, s == 0)

        if n_w:
            @pl.when(first)
            def _load_weights():
                for h, v in zip(w_hbm, w_vmem):
                    pltpu.sync_copy(h, v)

        ro, ba, ga = body([r[...] for r in r_refs], [r[0] for r in b_refs], [r[...] for r in v_refs],
                          list(w_vmem))
        for ref, val in zip(ro_refs, ro):
            ref[...] = val.astype(ref.dtype)

        def accumulate(ref, val, is_first, idx):
            @pl.when(is_first)
            def _set():
                ref[idx] = val

            @pl.when(jnp.logical_not(is_first))
            def _add():
                ref[idx] = ref[idx] + val

        for ref, val in zip(ba_refs, ba):
            accumulate(ref, val, s == 0, 0)
        for ref, val in zip(ga_refs, ga):
            accumulate(ref, val, first, Ellipsis)

    row_arrays = [r[0] if isinstance(r, tuple) else r for r in rows]
    row_in_specs = [pl.BlockSpec(r[1], r[2]) if isinstance(r, tuple)
                    else pl.BlockSpec((tm, r.shape[1]), lambda b, s: (b * ns + s, 0)) for r in rows]
    ro_shapes = [jax.ShapeDtypeStruct(tuple(o[0]), o[1]) if len(o) == 4 else jax.ShapeDtypeStruct((T, o[0]), o[1])
                 for o in row_outs]
    ro_specs = [pl.BlockSpec(o[2], o[3]) if len(o) == 4 else pl.BlockSpec((tm, o[0]), lambda b, s: (b * ns + s, 0))
                for o in row_outs]
    in_specs = (row_in_specs
                + [pl.BlockSpec((1, 1, v.shape[2]), lambda b, s: (b, 0, 0)) for v in bvecs]
                + [pl.BlockSpec(v.shape, lambda b, s, nd=v.ndim: (0,) * nd) for v in vecs]
                + [pl.BlockSpec(memory_space=pl.ANY) for _ in res])
    out_shape = (ro_shapes
                 + [jax.ShapeDtypeStruct((B, 1, w), F32) for w in bacc]
                 + [jax.ShapeDtypeStruct(tuple(sh), F32) for sh in gacc])
    out_specs = (ro_specs
                 + [pl.BlockSpec((1, 1, w), lambda b, s: (b, 0, 0)) for w in bacc]
                 + [pl.BlockSpec(tuple(sh), lambda b, s, nd=len(sh): (0,) * nd) for sh in gacc])
    outs = pl.pallas_call(
        kern, name=name, grid=(B, ns), in_specs=in_specs, out_specs=out_specs, out_shape=out_shape,
        scratch_shapes=[pltpu.VMEM(w.shape, w.dtype) for w in res],
        compiler_params=_params(("arbitrary", "arbitrary")),
    )(*row_arrays, *bvecs, *vecs, *res)
    return outs[:n_ro], outs[n_ro:n_ro + n_ba], outs[n_ro + n_ba:]


def matmul_nn(name, at, b):
    K, T = at.shape
    N = b.shape[1]
    tk = min(K, 512)
    tn = N if N <= 1280 else (1024 if N % 1024 == 0 else 512)
    assert K % tk == 0 and N % tn == 0 and T % min(T, 2048) == 0, (K, N, T)
    tt = min(T, 2048)
    nt = T // tt

    def kern(a_ref, b_ref, o_ref, acc_ref):
        t = pl.program_id(2)
        part = _dot(a_ref[...], b_ref[...])

        @pl.when(t == 0)
        def _set():
            acc_ref[...] = part

        @pl.when(t > 0)
        def _add():
            acc_ref[...] = acc_ref[...] + part

        @pl.when(t == nt - 1)
        def _write():
            o_ref[...] = acc_ref[...].astype(o_ref.dtype)

    return pl.pallas_call(
        kern, name=name, grid=(K // tk, N // tn, nt),
        in_specs=[pl.BlockSpec((tk, tt), lambda i, j, t: (i, t)), pl.BlockSpec((tt, tn), lambda i, j, t: (t, j))],
        out_specs=pl.BlockSpec((tk, tn), lambda i, j, t: (i, j)),
        out_shape=jax.ShapeDtypeStruct((K, N), BF16),
        scratch_shapes=[pltpu.VMEM((tk, tn), F32)],
        compiler_params=_params(("parallel", "parallel", "arbitrary")),
    )(at, b)


def _rope_consts():
    inv_freq = ROPE_BASE ** (-jnp.arange(0, QK_ROPE, 2, dtype=F32) / QK_ROPE)
    zeros64 = jnp.zeros((64,), F32)
    zeros32 = jnp.zeros((32,), F32)
    zeros16 = jnp.zeros((16,), F32)
    ones16 = jnp.ones((16,), F32)
    invf = jnp.concatenate([zeros64, inv_freq, inv_freq, zeros32])[None, :]
    m_lo = jnp.concatenate([zeros64, ones16, zeros16, zeros32])[None, :]
    m_hi = jnp.concatenate([zeros64, zeros16, ones16, zeros32])[None, :]
    return invf, m_lo, m_hi


def _rope_tables(pos, invf, m_lo, m_hi):
    ang = pos * invf
    return jnp.cos(ang), jnp.sin(ang) * m_lo, jnp.sin(ang) * m_hi


def _rope_tile(t, cs, s_lo, s_hi, sign):
    up = pltpu.roll(t, HEAD_PAD - 16, axis=1)
    down = pltpu.roll(t, 16, axis=1)
    return t * cs + sign * (down * s_hi - up * s_lo)


def _heads(ref, width):
    return jnp.stack([ref[:, h * width:(h + 1) * width] for h in range(N_HEADS)], axis=0)


def _bmm(a, b, ca, cb):
    return lax.dot_general(a, b, (((ca,), (cb,)), ((0,), (0,))), preferred_element_type=F32)


LOG2E = math.log2(math.e)
EXP2_SCALE = ATTN_SCALE * LOG2E


def _scores_t(k_ref, q_ref, masked, tq):
    st = _bmm(_heads(k_ref, HEAD_PAD), _heads(q_ref, HEAD_PAD), 2, 2)
    if masked:
        key = lax.broadcasted_iota(jnp.int32, (tq, tq), 0)
        qry = lax.broadcasted_iota(jnp.int32, (tq, tq), 1)
        st = jnp.where((key <= qry)[None], st, NEG)
    return st


def attention_fwd(q, k, vt, B, S, tq, side):
    nq = S // tq
    ns = side.n
    pairs = [(i, j) for i in range(nq) for j in range(i + 1)]
    n_pairs = len(pairs)
    q_tab = jnp.asarray([p[0] for p in pairs], jnp.int32)
    k_tab = jnp.asarray([p[1] for p in pairs], jnp.int32)

    def kern(q_tab_ref, k_tab_ref, *refs):
        q_ref, k_ref, vt_ref = refs[:3]
        side_in = refs[3:3 + ns]
        o_ref, lse_ref = refs[3 + ns:5 + ns]
        side_out = refs[5 + ns:5 + 2 * ns]
        m_scr, l_scr, acc_scr = refs[5 + 2 * ns:8 + 2 * ns]
        sems = refs[8 + 2 * ns:]
        b, t = pl.program_id(0), pl.program_id(1)
        qi, ki = q_tab_ref[t], k_tab_ref[t]

        @pl.when(jnp.logical_and(b == 0, t == 0))
        def _start_side():
            side.start(side_in, side_out, sems)

        @pl.when(ki == 0)
        def _init():
            m_scr[...] = jnp.full(m_scr.shape, NEG, F32)
            l_scr[...] = jnp.zeros(l_scr.shape, F32)
            acc_scr[...] = jnp.zeros(acc_scr.shape, F32)

        def block(masked):
            st = _scores_t(k_ref, q_ref, masked, tq)
            m_prev = m_scr[...]
            m_new = jnp.maximum(m_prev, jnp.max(st, axis=1, keepdims=True))
            alpha = jnp.exp2((m_prev - m_new) * EXP2_SCALE)
            p = jnp.exp2((st - m_new) * EXP2_SCALE)
            l_scr[...] = alpha * l_scr[...] + jnp.sum(p, axis=1, keepdims=True)
            vt3 = vt_ref[...].reshape(N_HEADS, V_HEAD, tq)
            acc_scr[...] = alpha * acc_scr[...] + _bmm(vt3, p.astype(BF16), 2, 1)
            m_scr[...] = m_new

        @pl.when(ki < qi)
        def _full():
            block(False)

        @pl.when(ki == qi)
        def _diagonal():
            block(True)
            ot = acc_scr[...] / l_scr[...]
            for h in range(N_HEADS):
                o_ref[:, h * V_HEAD:(h + 1) * V_HEAD] = ot[h].T
            lse_ref[...] = m_scr[...] * ATTN_SCALE + jnp.log(l_scr[...])

        @pl.when(jnp.logical_and(b == B - 1, t == n_pairs - 1))
        def _wait_side():
            side.wait(side_in, side_out, sems)

    T = B * S
    grid_spec = pltpu.PrefetchScalarGridSpec(
        num_scalar_prefetch=2, grid=(B, n_pairs),
        in_specs=[pl.BlockSpec((tq, QK_W), lambda b, t, qt, kt: (b * nq + qt[t], 0)),
                  pl.BlockSpec((tq, QK_W), lambda b, t, qt, kt: (b * nq + kt[t], 0)),
                  pl.BlockSpec((V_W, tq), lambda b, t, qt, kt: (0, b * nq + kt[t]))] + [ANY_SPEC] * ns,
        out_specs=[pl.BlockSpec((tq, V_W), lambda b, t, qt, kt: (b * nq + qt[t], 0)),
                   pl.BlockSpec((N_HEADS, 1, tq), lambda b, t, qt, kt: (0, 0, b * nq + qt[t]))] + [ANY_SPEC] * ns,
        scratch_shapes=[pltpu.VMEM((N_HEADS, 1, tq), F32), pltpu.VMEM((N_HEADS, 1, tq), F32),
                        pltpu.VMEM((N_HEADS, V_HEAD, tq), F32)] + side.scratch())
    return pl.pallas_call(
        kern, name="attention_fwd", grid_spec=grid_spec,
        out_shape=[jax.ShapeDtypeStruct((T, V_W), F32), jax.ShapeDtypeStruct((N_HEADS, 1, T), F32)]
        + side.out_shapes(),
        compiler_params=_params(("arbitrary", "arbitrary")),
    )(q_tab, k_tab, q, k, vt, *side.arrays)


def attention_bwd(q, k, kt, v, do, lse, delta, B, S, tq, side):
    nq = S // tq
    ns = side.n
    pairs = [(j, i) for j in range(nq) for i in range(j, nq)]
    n_pairs = len(pairs)
    k_tab = jnp.asarray([p[0] for p in pairs], jnp.int32)
    q_tab = jnp.asarray([p[1] for p in pairs], jnp.int32)

    def kern(k_tab_ref, q_tab_ref, *refs):
        q_ref, k_ref, kt_ref, v_ref, do_ref, lse_ref, delta_ref = refs[:7]
        side_in = refs[7:7 + ns]
        dq_hbm, dk_ref, dv_ref = refs[7 + ns:10 + ns]
        side_out = refs[10 + ns:10 + 2 * ns]
        dq_acc, dk_acc, dv_acc = refs[10 + 2 * ns:13 + 2 * ns]
        sems = refs[13 + 2 * ns:]
        b, t = pl.program_id(0), pl.program_id(1)
        kj, qi = k_tab_ref[t], q_tab_ref[t]

        @pl.when(jnp.logical_and(b == 0, t == 0))
        def _start_side():
            side.start(side_in, side_out, sems)

        @pl.when(t == 0)
        def _zero_dq():
            dq_acc[...] = jnp.zeros(dq_acc.shape, F32)

        @pl.when(qi == kj)
        def _zero_dkv():
            dk_acc[...] = jnp.zeros(dk_acc.shape, F32)
            dv_acc[...] = jnp.zeros(dv_acc.shape, F32)

        def block(masked):
            pt = jnp.exp2(_scores_t(k_ref, q_ref, masked, tq) * EXP2_SCALE - lse_ref[...] * LOG2E)
            do3 = _heads(do_ref, V_HEAD).astype(BF16)
            dv_acc[...] = dv_acc[...] + _bmm(pt.astype(BF16), do3, 2, 1)
            dpt = _bmm(_heads(v_ref, V_HEAD), do3, 2, 2)
            dst = (pt * (dpt - delta_ref[...]) * ATTN_SCALE).astype(BF16)
            dk_acc[...] = dk_acc[...] + _bmm(dst, _heads(q_ref, HEAD_PAD), 2, 1)
            q_cols = pl.ds(pl.multiple_of(qi * tq, tq), tq)
            kt3 = kt_ref[...].reshape(N_HEADS, HEAD_PAD, tq)
            dq_acc[:, :, q_cols] = dq_acc[:, :, q_cols] + _bmm(kt3, dst, 2, 1)

        @pl.when(qi > kj)
        def _full():
            block(False)

        @pl.when(qi == kj)
        def _diagonal():
            block(True)

        @pl.when(qi == nq - 1)
        def _write_dkv():
            for h in range(N_HEADS):
                dk_ref[:, h * HEAD_PAD:(h + 1) * HEAD_PAD] = dk_acc[h]
                dv_ref[:, h * V_HEAD:(h + 1) * V_HEAD] = dv_acc[h]

        @pl.when(t == n_pairs - 1)
        def _write_dq():
            pltpu.sync_copy(dq_acc, dq_hbm.at[b])

        @pl.when(jnp.logical_and(b == B - 1, t == n_pairs - 1))
        def _wait_side():
            side.wait(side_in, side_out, sems)

    T = B * S
    qrow = lambda b, t, kt, qt: (b * nq + qt[t], 0)
    qcol3 = lambda b, t, kt, qt: (0, 0, b * nq + qt[t])
    krow = lambda b, t, kt, qt: (b * nq + kt[t], 0)
    grid_spec = pltpu.PrefetchScalarGridSpec(
        num_scalar_prefetch=2, grid=(B, n_pairs),
        in_specs=[pl.BlockSpec((tq, QK_W), qrow), pl.BlockSpec((tq, QK_W), krow),
                  pl.BlockSpec((QK_W, tq), lambda b, t, kt, qt: (0, b * nq + kt[t])), pl.BlockSpec((tq, V_W), krow),
                  pl.BlockSpec((tq, V_W), qrow), pl.BlockSpec((N_HEADS, 1, tq), qcol3),
                  pl.BlockSpec((N_HEADS, 1, tq), qcol3)] + [ANY_SPEC] * ns,
        out_specs=[ANY_SPEC, pl.BlockSpec((tq, QK_W), krow), pl.BlockSpec((tq, V_W), krow)] + [ANY_SPEC] * ns,
        scratch_shapes=[pltpu.VMEM((N_HEADS, HEAD_PAD, S), F32), pltpu.VMEM((N_HEADS, tq, HEAD_PAD), F32),
                        pltpu.VMEM((N_HEADS, tq, V_HEAD), F32)] + side.scratch())
    return pl.pallas_call(
        kern, name="attention_bwd", grid_spec=grid_spec,
        out_shape=[jax.ShapeDtypeStruct((B, N_HEADS, HEAD_PAD, S), F32), jax.ShapeDtypeStruct((T, QK_W), F32),
                   jax.ShapeDtypeStruct((T, V_W), F32)] + side.out_shapes(),
        compiler_params=_params(("arbitrary", "arbitrary")),
    )(k_tab, q_tab, q, k, kt, v, do, lse, delta, *side.arrays)


GB = 8
N_GB = N_GROUPS // GB
GB_U = GB * GROUP
GB_ST = GB * N_STATE


def _gb_u(j):
    return slice(j * GB_U, (j + 1) * GB_U)


def _gb_s(j):
    return slice(j * GB_ST, (j + 1) * GB_ST)


def ssm_param_fn(lr, li, ld, br, bi):
    dt = jnp.exp(ld)
    er = jnp.exp(lr * dt)
    ar = er * jnp.cos(li * dt)
    ai = er * jnp.sin(li * dt)
    nr = ar - 1.0
    den = lr * lr + li * li
    cr = (nr * lr + ai * li) / den
    ci = (ai * lr - nr * li) / den
    return ar, ai, cr * br - ci * bi, cr * bi + ci * br


def ssm_params_fwd(reps):
    def kern(lr, li, ld, br, bi, ar_o, ai_o, bbr_o, bbi_o):
        ar, ai, bbr, bbi = ssm_param_fn(lr[...], li[...], ld[...], br[...], bi[...])
        ar_o[...] = ar
        ai_o[...] = ai
        bbr_o[...] = bbr
        bbi_o[...] = bbi

    shp = jax.ShapeDtypeStruct(reps[0].shape, F32)
    return pl.pallas_call(kern, name="ssm_params_fwd", out_shape=[shp] * 4)(*reps)


def ssm_params_bwd(reps, cots):
    rows = reps[0].shape[0]

    def kern(lr, li, ld, br, bi, d_ar, d_ai, d_bbr, d_bbi, dlr_o, dli_o, dld_o, dbr_o, dbi_o):
        _, vjp = jax.vjp(ssm_param_fn, lr[...], li[...], ld[...], br[...], bi[...])
        dlr, dli, dld, dbr, dbi = vjp((d_ar[...], d_ai[...], d_bbr[...], d_bbi[...]))
        dlr_o[...] = jnp.sum(dlr.reshape(N_GROUPS, GROUP, N_STATE), axis=1)
        dli_o[...] = jnp.sum(dli.reshape(N_GROUPS, GROUP, N_STATE), axis=1)
        dld_o[...] = jnp.sum(jnp.sum(dld.reshape(N_GROUPS, GROUP, N_STATE), axis=1), axis=-1, keepdims=True)
        dbr_o[...] = dbr
        dbi_o[...] = dbi

    g = jax.ShapeDtypeStruct((N_GROUPS, N_STATE), F32)
    full = jax.ShapeDtypeStruct((rows, N_STATE), F32)
    return pl.pallas_call(
        kern, name="ssm_params_bwd",
        out_shape=[g, g, jax.ShapeDtypeStruct((N_GROUPS, 1), F32), full, full])(*reps, *cots)


def ssm_fwd(proj, b_re, b_im, c_re, c_im, d_vec, a_re, a_im, B, S, ts):
    ns = S // ts
    T = B * S

    def kern(u_ref, bre_ref, bim_ref, cre_ref, cim_ref, d_ref, are_ref, aim_ref,
             sre_ref, sim_ref, y_ref, car_re, car_im, bu_re, bu_im):
        s = pl.program_id(1)

        @pl.when(s == 0)
        def _reset():
            car_re[...] = jnp.zeros(car_re.shape, F32)
            car_im[...] = jnp.zeros(car_im.shape, F32)

        u = u_ref[...]
        ub = u.astype(BF16)
        for j in range(N_GB):
            uj, bj, sj = ub[:, _gb_u(j)], _gb_u(j), _gb_s(j)
            bu_re[:, sj] = _dot(uj, bre_ref[bj, :])
            bu_im[:, sj] = _dot(uj, bim_ref[bj, :])
        ar, ai = are_ref[...], aim_ref[...]

        def step(t, carry):
            xr, xi = carry
            row = pl.ds(t, 1)
            nr = ar * xr - ai * xi + bu_re[row, :]
            ni = ar * xi + ai * xr + bu_im[row, :]
            sre_ref[row, :] = nr
            sim_ref[row, :] = ni
            return nr, ni

        xr, xi = lax.fori_loop(0, ts, step, (car_re[0:1, :], car_im[0:1, :]))
        car_re[0:1, :] = xr
        car_im[0:1, :] = xi
        y = [_dot(sre_ref[:, _gb_s(j)].astype(BF16), cre_ref[_gb_s(j), :])
             - _dot(sim_ref[:, _gb_s(j)].astype(BF16), cim_ref[_gb_s(j), :]) for j in range(N_GB)]
        y_ref[...] = jnp.concatenate(y, axis=1) + d_ref[...] * u

    row = lambda b, s: (b * ns + s, 0)
    whole = lambda b, s: (0, 0)
    return pl.pallas_call(
        kern, name="ssm_fwd", grid=(B, ns),
        in_specs=[pl.BlockSpec((ts, D_SSM), row),
                  pl.BlockSpec((D_SSM, GB_ST), whole), pl.BlockSpec((D_SSM, GB_ST), whole),
                  pl.BlockSpec((N_ST, GB_U), whole), pl.BlockSpec((N_ST, GB_U), whole),
                  pl.BlockSpec((1, D_SSM), whole), pl.BlockSpec((1, N_ST), whole), pl.BlockSpec((1, N_ST), whole)],
        out_specs=[pl.BlockSpec((ts, N_ST), row), pl.BlockSpec((ts, N_ST), row), pl.BlockSpec((ts, D_SSM), row)],
        out_shape=[jax.ShapeDtypeStruct((T, N_ST), F32), jax.ShapeDtypeStruct((T, N_ST), F32),
                   jax.ShapeDtypeStruct((T, D_SSM), F32)],
        scratch_shapes=[pltpu.VMEM((8, N_ST), F32), pltpu.VMEM((8, N_ST), F32),
                        pltpu.VMEM((ts, N_ST), F32), pltpu.VMEM((ts, N_ST), F32)],
        compiler_params=_params(("arbitrary", "arbitrary")),
    )(proj, b_re, b_im, c_re, c_im, d_vec, a_re, a_im)


def ssm_bwd(dy, proj, s_re, s_im, b_re, b_im, c_re, c_im, d_vec, a_re, a_im, B, S, ts):
    ns = S // ts
    T = B * S
    t8 = ts // 8

    def kern(dy_ref, u_ref, sre_ref, sim_ref, pre_ref, pim_ref, bre_ref, bim_ref, cre_ref, cim_ref,
             d_ref, are_ref, aim_ref,
             du_ref, dcre_ref, dcim_ref, dbre_ref, dbim_ref, dare_ref, daim_ref, dd_ref,
             car_re, car_im, g_re, g_im):
        b, s = pl.program_id(0), pl.program_id(1)
        first = jnp.logical_and(b == 0, s == 0)

        @pl.when(s == 0)
        def _reset():
            car_re[...] = jnp.zeros(car_re.shape, F32)
            car_im[...] = jnp.zeros(car_im.shape, F32)

        dyv = dy_ref[...]
        dyb = dyv.astype(BF16)
        u = u_ref[...]
        for j in range(N_GB):
            g_re[:, _gb_s(j)] = _dot_nt(dyb[:, _gb_u(j)], cre_ref[_gb_s(j), :])
            g_im[:, _gb_s(j)] = -_dot_nt(dyb[:, _gb_u(j)], cim_ref[_gb_s(j), :])
        ar, ai = are_ref[...], aim_ref[...]

        def step(i, carry):
            gr_next, gi_next = carry
            row = pl.ds(ts - 1 - i, 1)
            gr = g_re[row, :] + ar * gr_next + ai * gi_next
            gi = g_im[row, :] + ar * gi_next - ai * gr_next
            g_re[row, :] = gr
            g_im[row, :] = gi
            return gr, gi

        gr0, gi0 = lax.fori_loop(0, ts, step, (car_re[0:1, :], car_im[0:1, :]))
        car_re[0:1, :] = gr0
        car_im[0:1, :] = gi0

        gr, gi = g_re[...], g_im[...]
        sr, si = sre_ref[...], sim_ref[...]
        has_prev = (s < ns - 1).astype(F32)
        row_id = lax.broadcasted_iota(jnp.int32, (ts, N_ST), 0)
        spr = jnp.where(row_id == 0, pre_ref[7:8, :] * has_prev, pltpu.roll(sr, 1, axis=0))
        spi = jnp.where(row_id == 0, pim_ref[7:8, :] * has_prev, pltpu.roll(si, 1, axis=0))
        grb, gib, ub = gr.astype(BF16), gi.astype(BF16), u.astype(BF16)
        srb, sib = sr.astype(BF16), si.astype(BF16)
        cat0 = lambda f: jnp.concatenate([f(j) for j in range(N_GB)], axis=0)
        cat1 = lambda f: jnp.concatenate([f(j) for j in range(N_GB)], axis=1)
        parts = [
            (dcre_ref, cat0(lambda j: _dot_tn(srb[:, _gb_s(j)], dyb[:, _gb_u(j)]))),
            (dcim_ref, cat0(lambda j: -_dot_tn(sib[:, _gb_s(j)], dyb[:, _gb_u(j)]))),
            (dbre_ref, cat0(lambda j: _dot_tn(ub[:, _gb_u(j)], grb[:, _gb_s(j)]))),
            (dbim_ref, cat0(lambda j: _dot_tn(ub[:, _gb_u(j)], gib[:, _gb_s(j)]))),
            (dare_ref, _colsum(gr * spr + gi * spi)),
            (daim_ref, _colsum(gi * spr - gr * spi)),
            (dd_ref, _colsum(dyv * u)),
        ]
        du = cat1(lambda j: _dot_nt(grb[:, _gb_s(j)], bre_ref[_gb_u(j), :])
                  + _dot_nt(gib[:, _gb_s(j)], bim_ref[_gb_u(j), :]))
        du_ref[...] = du + d_ref[...] * dyv

        @pl.when(first)
        def _set():
            for ref, val in parts:
                ref[...] = val

        @pl.when(jnp.logical_not(first))
        def _add():
            for ref, val in parts:
                ref[...] = ref[...] + val

    row = lambda b, s: (b * ns + (ns - 1 - s), 0)
    prev = lambda b, s: (jnp.maximum((b * ns + (ns - 1 - s)) * t8 - 1, 0), 0)
    whole = lambda b, s: (0, 0)
    acc_shapes = [(N_ST, GB_U), (N_ST, GB_U), (D_SSM, GB_ST), (D_SSM, GB_ST), (1, N_ST), (1, N_ST), (1, D_SSM)]
    return pl.pallas_call(
        kern, name="ssm_bwd", grid=(B, ns),
        in_specs=[pl.BlockSpec((ts, D_SSM), row), pl.BlockSpec((ts, D_SSM), row),
                  pl.BlockSpec((ts, N_ST), row), pl.BlockSpec((ts, N_ST), row),
                  pl.BlockSpec((8, N_ST), prev), pl.BlockSpec((8, N_ST), prev),
                  pl.BlockSpec((D_SSM, GB_ST), whole), pl.BlockSpec((D_SSM, GB_ST), whole),
                  pl.BlockSpec((N_ST, GB_U), whole), pl.BlockSpec((N_ST, GB_U), whole),
                  pl.BlockSpec((1, D_SSM), whole), pl.BlockSpec((1, N_ST), whole), pl.BlockSpec((1, N_ST), whole)],
        out_specs=[pl.BlockSpec((ts, D_SSM), row)] + [pl.BlockSpec(sh, whole) for sh in acc_shapes],
        out_shape=[jax.ShapeDtypeStruct((T, D_SSM), F32)] + [jax.ShapeDtypeStruct(sh, F32) for sh in acc_shapes],
        scratch_shapes=[pltpu.VMEM((8, N_ST), F32), pltpu.VMEM((8, N_ST), F32),
                        pltpu.VMEM((ts, N_ST), F32), pltpu.VMEM((ts, N_ST), F32)],
        compiler_params=_params(("arbitrary", "arbitrary")),
    )(dy, proj, s_re, s_im, s_re, s_im, b_re, b_im, c_re, c_im, d_vec, a_re, a_im)


def modulation_fwd(c_all, ada_w, ada_b, fada_w, fada_b):
    def kern(c_ref, w_ref, b_ref, fw_ref, fb_ref, cond_ref, mod_ref, fmod_ref):
        cv = c_ref[...]
        cond = (cv * _sigmoid(cv)).astype(BF16)
        cond_ref[...] = cond
        mod_ref[...] = _dot(cond, w_ref[...].astype(BF16)) + b_ref[...]
        fmod_ref[...] = _dot(cond, fw_ref[...].astype(BF16)) + fb_ref[...]

    n = c_all.shape[0]
    return pl.pallas_call(
        kern, name="modulation_fwd",
        out_shape=[jax.ShapeDtypeStruct(c_all.shape, BF16), jax.ShapeDtypeStruct((n, ada_w.shape[1]), F32),
                   jax.ShapeDtypeStruct((n, fada_w.shape[1]), F32)],
        compiler_params=_params(),
    )(c_all, ada_w, ada_b, fada_w, fada_b)


def modulation_bwd(cond_all, dmod, dfmod, dmod_all):
    def kern(c_ref, dm_ref, dfm_ref, all_ref, gw_ref, gfw_ref, gb_ref):
        cond = c_ref[...]
        gw_ref[...] = _dot_tn(cond, dm_ref[...].astype(BF16))
        gfw_ref[...] = _dot_tn(cond, dfm_ref[...].astype(BF16))
        gb_ref[...] = _colsum(all_ref[...].astype(F32))

    d = cond_all.shape[1]
    return pl.pallas_call(
        kern, name="modulation_bwd",
        out_shape=[jax.ShapeDtypeStruct((d, dmod.shape[1]), F32), jax.ShapeDtypeStruct((d, dfmod.shape[1]), F32),
                   jax.ShapeDtypeStruct((1, dmod_all.shape[1]), F32)],
        compiler_params=_params(),
    )(cond_all, dmod, dfmod, dmod_all)


def _adamw_update(p_ref, w_ref, m_ref, v_ref, g_o, d_o, m_o, v_o):
    g = p_ref[0].astype(F32)
    for i in range(1, p_ref.shape[0]):
        g = g + p_ref[i].astype(F32)
    m_new = ADAM_B1 * m_ref[...] + (1.0 - ADAM_B1) * g
    v_new = ADAM_B2 * v_ref[...] + (1.0 - ADAM_B2) * (g * g)
    m_hat = m_new / (1.0 - ADAM_B1 ** ADAM_STEP)
    v_hat = v_new / (1.0 - ADAM_B2 ** ADAM_STEP)
    g_o[...] = g
    d_o[...] = -ADAM_LR * (m_hat / (jnp.sqrt(v_hat) + ADAM_EPS) + ADAM_WD * w_ref[...])
    m_o[...] = m_new
    v_o[...] = v_new


def adamw_many(name, items):
    k = len(items)

    def kern(*refs):
        ins, outs = refs[:4 * k], refs[4 * k:]
        for i in range(k):
            _adamw_update(*ins[4 * i:4 * i + 4], *outs[4 * i:4 * i + 4])

    flat = [a for item in items for a in item]
    out_shape = [jax.ShapeDtypeStruct(item[1].shape, F32) for item in items for _ in range(4)]
    outs = pl.pallas_call(kern, name=name, out_shape=out_shape, compiler_params=_params())(*flat)
    return [tuple(outs[4 * i:4 * i + 4]) for i in range(k)]


def adamw(name, parts, w, m, v):
    n, R, C = parts.shape
    tr = R
    while n * tr * C * 4 > 4 * 1024 * 1024 and tr % 32 == 0:
        tr //= 2

    def kern(p_ref, w_ref, m_ref, v_ref, g_o, d_o, m_o, v_o):
        _adamw_update(p_ref, w_ref, m_ref, v_ref, g_o, d_o, m_o, v_o)

    blk = pl.BlockSpec((tr, C), lambda i: (i, 0))
    shp = jax.ShapeDtypeStruct((R, C), F32)
    return pl.pallas_call(
        kern, name=name, grid=(R // tr,),
        in_specs=[pl.BlockSpec((n, tr, C), lambda i: (0, i, 0)), blk, blk, blk],
        out_specs=[blk] * 4, out_shape=[shp] * 4,
        compiler_params=_params(("parallel",)),
    )(parts, w, m, v)


def _cols_from_gathered(g):
    return jnp.transpose(g, (1, 0, 2)).reshape(g.shape[1], N_DEV * g.shape[2])


def _cols_to_shards(w):
    k, n8 = w.shape
    return jnp.transpose(w.reshape(k, N_DEV, n8 // N_DEV), (1, 0, 2))


def _pad_w_in(w):
    z = functools.partial(jnp.zeros, dtype=w.dtype)
    k = w.shape[0]
    cut = D_SSM + Q_LORA + KV_LORA
    return jnp.concatenate([w[:, :cut], z((k, 64)), w[:, cut:], z((k, 32))], axis=1)


def _unpad_w_in(w):
    cut = D_SSM + Q_LORA + KV_LORA
    return jnp.concatenate([w[:, :cut], w[:, cut + 64:cut + 96]], axis=1)


def _pad_w_uq(w):
    k = w.shape[0]
    w3 = w.reshape(k, N_HEADS, QK_NOPE + QK_ROPE)
    return jnp.pad(w3, ((0, 0), (0, 0), (0, HEAD_PAD - QK_NOPE - QK_ROPE))).reshape(k, QK_W)


def _unpad_w_uq(w):
    k = w.shape[0]
    return w.reshape(k, N_HEADS, HEAD_PAD)[:, :, :QK_NOPE + QK_ROPE].reshape(k, N_HEADS * (QK_NOPE + QK_ROPE))


def _pad_w_ukv(w):
    k = w.shape[0]
    w3 = w.reshape(k, N_HEADS, QK_NOPE + V_HEAD)
    kpart = jnp.pad(w3[:, :, :QK_NOPE], ((0, 0), (0, 0), (0, HEAD_PAD - QK_NOPE))).reshape(k, QK_W)
    vpart = w3[:, :, QK_NOPE:].reshape(k, V_W)
    return jnp.concatenate([kpart, vpart], axis=1)


def _unpad_w_ukv(w):
    k = w.shape[0]
    kpart = w[:, :QK_W].reshape(k, N_HEADS, HEAD_PAD)[:, :, :QK_NOPE]
    vpart = w[:, QK_W:].reshape(k, N_HEADS, V_HEAD)
    return jnp.concatenate([kpart, vpart], axis=2).reshape(k, N_HEADS * (QK_NOPE + V_HEAD))


def _block_diag(blocks):
    g, r, c = blocks.shape
    b4 = blocks.reshape(N_GB, GB, r, c)
    keep = jnp.eye(GB, dtype=bool)[None, :, None, :, None]
    return jnp.where(keep, b4[:, :, :, None, :], 0).reshape(g * r, GB * c)


def _block_diag_take(stacked, r, c):
    d5 = stacked.reshape(N_GB, GB, r, GB, c)
    idx = jnp.arange(GB)
    return jnp.transpose(d5[:, idx, :, idx, :], (1, 0, 2, 3)).reshape(N_GROUPS, r, c)


SMALL = ["norm1_g", "ssm_lambda_re", "ssm_lambda_im", "ssm_b_re", "ssm_b_im", "ssm_c_re", "ssm_c_im",
         "ssm_d", "ssm_log_dt", "q_norm_g", "kv_norm_g", "ssm_out_g", "attn_out_g", "norm2_g", "final_norm_g"]
BIASES = ["ada_b", "final_ada_b"]


def _dense_2d(shape):
    dims = [d for d in shape[1:]] if len(shape) > 1 and shape[0] == 1 else list(shape)
    if len(dims) == 1:
        return (1, dims[0])
    return (dims[0], int(np.prod(dims[1:])))


def kernel(x, c, positions, ada_w, ada_b, norm1_g, w_in, ssm_lambda_re, ssm_lambda_im, ssm_b_re, ssm_b_im, ssm_c_re, ssm_c_im, ssm_d, ssm_log_dt, w_glu, q_norm_g, w_uq, kv_norm_g, w_ukv, ssm_out_g, attn_out_g, w_out, norm2_g, w_ff1, w_ff2, final_ada_w, final_ada_b, final_norm_g, loss_target, m_ada_w, m_ada_b, m_norm1_g, m_w_in, m_ssm_lambda_re, m_ssm_lambda_im, m_ssm_b_re, m_ssm_b_im, m_ssm_c_re, m_ssm_c_im, m_ssm_d, m_ssm_log_dt, m_w_glu, m_q_norm_g, m_w_uq, m_kv_norm_g, m_w_ukv, m_ssm_out_g, m_attn_out_g, m_w_out, m_norm2_g, m_w_ff1, m_w_ff2, m_final_ada_w, m_final_ada_b, m_final_norm_g, v_ada_w, v_ada_b, v_norm1_g, v_w_in, v_ssm_lambda_re, v_ssm_lambda_im, v_ssm_b_re, v_ssm_b_im, v_ssm_c_re, v_ssm_c_im, v_ssm_d, v_ssm_log_dt, v_w_glu, v_q_norm_g, v_w_uq, v_kv_norm_g, v_w_ukv, v_ssm_out_g, v_attn_out_g, v_w_out, v_norm2_g, v_w_ff1, v_w_ff2, v_final_ada_w, v_final_ada_b, v_final_norm_g):
    given = dict(locals())
    B, S, D = x.shape
    T = B * S
    tm = min(256, S)
    me = 4 * lax.axis_index("x") + 2 * lax.axis_index("y") + lax.axis_index("c")

    big = ["w_in", "w_glu", "w_uq", "w_ukv", "w_out", "w_ff1", "w_ff2"]
    shards = {n: given[n][0] for n in big}
    early, late = ["w_in", "w_uq", "w_ukv"], ["w_glu", "w_out", "w_ff1", "w_ff2"]
    gathered = all_gather("gather_first_weights", [c] + [shards[n].astype(BF16) for n in early])
    c_all = gathered[0].reshape(N_DEV * B, D)
    gw = dict(zip(early, gathered[1:]))
    w_in_p = _pad_w_in(_cols_from_gathered(gw["w_in"]))
    w_uq_p = _pad_w_uq(_cols_from_gathered(gw["w_uq"]))
    w_ukv_p = _pad_w_ukv(_cols_from_gathered(gw["w_ukv"]))

    n_mod = ada_w.shape[2]
    n_fmod = final_ada_w.shape[1]
    ada_b_sh = lax.dynamic_slice_in_dim(ada_b, me * n_mod, n_mod, axis=1)
    fada_b_sh = lax.dynamic_slice_in_dim(final_ada_b[None, :], me * n_fmod, n_fmod, axis=1)
    cond_all, mod_sh, fmod_sh = modulation_fwd(c_all, ada_w[0], ada_b_sh, final_ada_w, fada_b_sh)
    mod_g, fmod_g = all_gather("gather_modulation", [mod_sh, fmod_sh])
    mod_mine = lax.dynamic_slice_in_dim(_cols_from_gathered(mod_g), me * B, B, axis=0)
    fmod_mine = lax.dynamic_slice_in_dim(_cols_from_gathered(fmod_g), me * B, B, axis=0)
    shift1, scale1, gate1, shift2, scale2, gate2 = [mod_mine[:, None, i * D:(i + 1) * D] for i in range(6)]
    fshift, fscale = fmod_mine[:, None, :D], fmod_mine[:, None, D:]

    x2d = x.reshape(T, D)
    tgt2d = loss_target.reshape(T, D)
    pos = positions.astype(F32).reshape(T, 1)
    invf, m_lo, m_hi = _rope_consts()
    rope_mask = m_lo + m_hi

    def fwd_in(rows, bv, vecs, res):
        (xv,), (sc, sh), (g1,), (w,) = rows, bv, vecs, res
        r, xh = _rms_stats(xv)
        h = xh * g1 * (1.0 + sc) + sh
        return [_dot(h.astype(BF16), w[...])], [], []

    (proj,), _, _ = row_call("fwd_in", fwd_in, B, S, tm, [x2d], [scale1, shift1], [norm1_g], [w_in_p],
                             [(IN_PAD, F32)], [], [])

    def fwd_qkv(rows, bv, vecs, res):
        (pr, ps), (gq, gkv, fr, lo, hi), (wq, wkv) = rows, vecs, res
        cs, s_lo, s_hi = _rope_tables(ps, fr, lo, hi)
        _, qh = _rms_stats(pr[:, D_SSM:D_SSM + Q_LORA])
        _, kvh = _rms_stats(pr[:, D_SSM + Q_LORA:D_SSM + Q_LORA + KV_LORA])
        qf = _dot((qh * gq).astype(BF16), wq[...])
        kvf = _dot((kvh * gkv).astype(BF16), wkv[...])
        k_rope = _rope_tile(pr[:, IN_PAD - HEAD_PAD:], cs, s_lo, s_hi, 1.0)
        q_t, k_t = [], []
        for h in range(N_HEADS):
            hs = slice(h * HEAD_PAD, (h + 1) * HEAD_PAD)
            q_t.append(_rope_tile(qf[:, hs], cs, s_lo, s_hi, 1.0))
            k_t.append(kvf[:, hs] + k_rope)
        kf, vf = jnp.concatenate(k_t, axis=1), kvf[:, QK_W:]
        return [jnp.concatenate(q_t, axis=1), kf, vf, t_val(kf), t_val(vf)], [], []

    ns = S // tm
    col_tile = lambda b, s: (0, b * ns + s)
    t_out = lambda w: ((w, T), BF16, (w, tm), col_tile)
    t_val = lambda v: v.astype(BF16).T
    (q_r, k_r, v_r, k_t, v_t), _, _ = row_call(
        "fwd_qkv", fwd_qkv, B, S, tm, [proj, pos], [], [q_norm_g, kv_norm_g, invf, m_lo, m_hi],
        [w_uq_p, w_ukv_p],
        [(QK_W, BF16), (QK_W, BF16), (V_W, BF16), ((QK_W, T), BF16, (QK_W, tm), col_tile),
         ((V_W, T), BF16, (V_W, tm), col_tile)], [], [])
    late_gather = Exchange([shards[n].astype(BF16) for n in late], [GATHER] * len(late))
    y_attn, lse, *late_g = attention_fwd(q_r, k_r, v_t, B, S, tm, late_gather)
    gw = dict(zip(late, late_g))
    w_glu_f = _cols_from_gathered(gw["w_glu"])
    w_out_f = gw["w_out"].reshape(-1, D)
    w_ff1_f = _cols_from_gathered(gw["w_ff1"])
    w_ff2_f = gw["w_ff2"].reshape(-1, D)

    rep = lambda a: jnp.repeat(a, GROUP, axis=0)
    lam_rep = [rep(ssm_lambda_re[0]), rep(ssm_lambda_im[0]),
               rep(jnp.broadcast_to(ssm_log_dt[0][:, None], (N_GROUPS, N_STATE)))]
    bt = lambda a: jnp.transpose(a, (0, 2, 1)).reshape(D_SSM, N_STATE)
    reps = lam_rep + [bt(ssm_b_re[0]), bt(ssm_b_im[0])]
    a_re_rep, a_im_rep, bb_re, bb_im = ssm_params_fwd(reps)
    a_re = a_re_rep.reshape(N_GROUPS, GROUP, N_STATE)[:, 0, :].reshape(1, N_ST)
    a_im = a_im_rep.reshape(N_GROUPS, GROUP, N_STATE)[:, 0, :].reshape(1, N_ST)
    bbd_re = _block_diag(bb_re.reshape(N_GROUPS, GROUP, N_STATE)).astype(BF16)
    bbd_im = _block_diag(bb_im.reshape(N_GROUPS, GROUP, N_STATE)).astype(BF16)
    cbd_re = _block_diag(jnp.transpose(ssm_c_re[0], (0, 2, 1))).astype(BF16)
    cbd_im = _block_diag(jnp.transpose(ssm_c_im[0], (0, 2, 1))).astype(BF16)
    d_vec = ssm_d.reshape(1, D_SSM)
    st_re, st_im, y_pre = ssm_fwd(proj, bbd_re, bbd_im, cbd_re, cbd_im, d_vec, a_re, a_im, B, S, tm)

    def fwd_mix(rows, bv, vecs, res):
        (yp, ya, xv), (g1v,), (gso, gao), (wg, wo) = rows, bv, vecs, res
        z = _dot(_gelu(yp).astype(BF16), wg[...])
        y_ssm = z[:, :D_SSM] * _sigmoid(z[:, D_SSM:])
        _, sh = _rms_stats(y_ssm)
        _, ah = _rms_stats(ya)
        o = _dot((sh * gso).astype(BF16), wo[0:D_SSM, :]) + _dot((ah * gao).astype(BF16), wo[D_SSM:, :])
        return [z, o, xv + g1v * o], [], []

    (z, o, x2), _, _ = row_call("fwd_mix", fwd_mix, B, S, tm, [y_pre, y_attn, x2d], [gate1],
                                [ssm_out_g, attn_out_g], [w_glu_f, w_out_f],
                                [(2 * D_SSM, F32), (D, F32), (D, F32)], [], [])

    def final_out(x3, gf, fsc, fsh):
        r3, xh3 = _rms_stats(x3)
        n3 = xh3 * gf
        return r3, xh3, n3, n3 * (1.0 + fsc) + fsh

    def mlp_hidden(xv, g2, sc, sh, w1):
        r2, xh2 = _rms_stats(xv)
        n2 = xh2 * g2
        h2 = n2 * (1.0 + sc) + sh
        return r2, xh2, n2, h2, _dot(h2.astype(BF16), w1[...])

    def fwd_mlp(rows, bv, vecs, res):
        (xv, tg), (sc, sh, g2v, fsc, fsh), (g2, gf), (w1, w2) = rows, bv, vecs, res
        _, _, _, _, a = mlp_hidden(xv, g2, sc, sh, w1)
        ra = jnp.maximum(a, 0.0)
        ff = _dot((ra * ra).astype(BF16), w2[...])
        x3 = xv + g2v * ff
        _, _, _, out = final_out(x3, gf, fsc, fsh)
        err = out - tg
        loss = 0.5 * jnp.sum(jnp.mean(err * err, axis=-1, keepdims=True), axis=0, keepdims=True)
        return [ff, x3], [], [jnp.broadcast_to(loss, (1, 128))]

    fnorm_g = final_norm_g[None, :]
    (ff, x3), _, (loss_acc,) = row_call(
        "fwd_mlp", fwd_mlp, B, S, tm, [x2, tgt2d], [scale2, shift2, gate2, fscale, fshift], [norm2_g, fnorm_g],
        [w_ff1_f, w_ff2_f], [(D, F32), (D, F32)], [], [(1, 128)])
    loss = lax.psum(loss_acc[0, 0], ("x", "y", "c"))

    def bwd_mlp(rows, bv, vecs, res):
        (x3v, tg, x2v, ffv), (sc, sh, g2v, fsc, fsh), (g2, gf), (w1, w2) = rows, bv, vecs, res
        r3, xh3, n3, out = final_out(x3v, gf, fsc, fsh)
        dout = (out - tg) * (1.0 / D)
        dn3 = dout * (1.0 + fsc)
        dx3 = _rms_bwd(dn3 * gf, xh3, r3)
        dff = dx3 * g2v
        r2, xh2, n2, h2, a = mlp_hidden(x2v, g2, sc, sh, w1)
        ra = jnp.maximum(a, 0.0)
        dffb = dff.astype(BF16)
        da = _dot_nt(dffb, w2[...]) * (2.0 * ra)
        dab = da.astype(BF16)
        dh2 = _dot_nt(dab, w1[...])
        dn2 = dh2 * (1.0 + sc)
        dx2 = dx3 + _rms_bwd(dn2 * g2, xh2, r2)
        return ([dx2, t_val(h2), t_val(ra * ra), dab, dffb],
                [_colsum(dout), _colsum(dout * n3), _colsum(dx3 * ffv), _colsum(dh2), _colsum(dh2 * n2)],
                [_colsum(dn3 * xh3), _colsum(dn2 * xh2)])

    ((dx2, h2b, fb, dab, dffb), (d_fshift, d_fscale, d_gate2, d_shift2, d_scale2), (d_gf, d_g2)) = row_call(
        "bwd_mlp", bwd_mlp, B, S, tm, [x3, tgt2d, x2, ff], [scale2, shift2, gate2, fscale, fshift],
        [norm2_g, fnorm_g], [w_ff1_f, w_ff2_f],
        [(D, F32), t_out(D), t_out(4 * D), (4 * D, BF16), (D, BF16)], [D] * 5, [(1, D), (1, D)])
    g_w_ff1 = matmul_nn("grad_w_ff1", h2b, dab)
    g_w_ff2 = matmul_nn("grad_w_ff2", fb, dffb)

    def bwd_mix(rows, bv, vecs, res):
        (dxv, ov, zv, ya, yp), (g1v,), (gso, gao), (wo, wg) = rows, bv, vecs, res
        do = (dxv * g1v).astype(BF16)
        dyn = _dot_nt(do, wo[...])
        z1, sg = zv[:, :D_SSM], _sigmoid(zv[:, D_SSM:])
        y_ssm = z1 * sg
        rs, sh = _rms_stats(y_ssm)
        ra, ah = _rms_stats(ya)
        dyn_s, dyn_a = dyn[:, :D_SSM], dyn[:, D_SSM:]
        dy_ssm = _rms_bwd(dyn_s * gso, sh, rs)
        dy_attn = _rms_bwd(dyn_a * gao, ah, ra)
        dz = jnp.concatenate([dy_ssm * sg, dy_ssm * z1 * sg * (1.0 - sg)], axis=1).astype(BF16)
        dy_pre = _dot_nt(dz, wg[...]) * _gelu_grad(yp)
        yn = jnp.concatenate([sh * gso, ah * gao], axis=1)
        delta = jnp.sum((dy_attn * ya).T.reshape(N_HEADS, V_HEAD, tm), axis=1, keepdims=True)
        return ([do, t_val(yn), dz, t_val(_gelu(yp)), dy_attn, dy_pre, delta], [_colsum(dxv * ov)],
                [_colsum(dyn_s * sh), _colsum(dyn_a * ah)])

    ((dob, ynb, dzb, ygb, dy_attn, dy_pre, delta), (d_gate1,), (d_gso, d_gao)) = row_call(
        "bwd_mix", bwd_mix, B, S, tm, [dx2, o, z, y_attn, y_pre], [gate1], [ssm_out_g, attn_out_g],
        [w_out_f, w_glu_f],
        [(D, BF16), t_out(D), (2 * D_SSM, BF16), t_out(D_SSM), (V_W, F32), (D_SSM, F32),
         ((N_HEADS, 1, T), F32, (N_HEADS, 1, tm), lambda b, s: (0, 0, b * ns + s))],
        [D], [(1, D_SSM), (1, V_W)])
    g_w_out = matmul_nn("grad_w_out", ynb, dob)
    g_w_glu = matmul_nn("grad_w_glu", ygb, dzb)

    grads_a = Exchange([g_w_glu, g_w_out.reshape(N_DEV, -1, D), g_w_ff1, g_w_ff2.reshape(N_DEV, -1, D)],
                       [w_glu.shape[2], None, w_ff1.shape[2], None])
    dq_t, dk_r, dv_r, *parts_a = attention_bwd(q_r, k_r, k_t, v_r, dy_attn, lse, delta, B, S, tm, grads_a)
    parts = dict(zip(late, parts_a))
    dq_t = dq_t.reshape(B, QK_W, S)

    def bwd_qkv(rows, bv, vecs, res):
        (dqt, dkv, dvv, pr, ps), (gq, gkv, fr, lo, hi, rmask), (wq, wkv) = rows, vecs, res
        dqv = dqt[0].T
        cs, s_lo, s_hi = _rope_tables(ps, fr, lo, hi)
        dq_t = []
        dk_rope = jnp.zeros((dkv.shape[0], HEAD_PAD), F32)
        for h in range(N_HEADS):
            hs = slice(h * HEAD_PAD, (h + 1) * HEAD_PAD)
            dq_t.append(_rope_tile(dqv[:, hs], cs, s_lo, s_hi, -1.0))
            dk_rope = dk_rope + dkv[:, hs]
        dk_rope = _rope_tile(dk_rope * rmask, cs, s_lo, s_hi, -1.0)
        dqb = jnp.concatenate(dq_t, axis=1).astype(BF16)
        dkvb = jnp.concatenate([dkv, dvv], axis=1).astype(BF16)
        rq, qh = _rms_stats(pr[:, D_SSM:D_SSM + Q_LORA])
        rk, kvh = _rms_stats(pr[:, D_SSM + Q_LORA:D_SSM + Q_LORA + KV_LORA])
        dqn = _dot_nt(dqb, wq[...])
        dkvn = _dot_nt(dkvb, wkv[...])
        dpa = jnp.concatenate([_rms_bwd(dqn * gq, qh, rq), _rms_bwd(dkvn * gkv, kvh, rk), dk_rope], axis=1)
        return [dpa, t_val(qh * gq), t_val(kvh * gkv), dqb, dkvb], [], [_colsum(dqn * qh), _colsum(dkvn * kvh)]

    ((dpa, qnb, kvnb, dqb, dkvb), _, (d_gq, d_gkv)) = row_call(
        "bwd_qkv", bwd_qkv, B, S, tm,
        [(dq_t, (1, QK_W, tm), lambda b, s: (b, 0, s)), dk_r, dv_r, proj, pos], [],
        [q_norm_g, kv_norm_g, invf, m_lo, m_hi, rope_mask], [w_uq_p, w_ukv_p],
        [(Q_LORA + KV_LORA + HEAD_PAD, F32), t_out(Q_LORA), t_out(KV_LORA), (QK_W, BF16), (QK_W + V_W, BF16)],
        [], [(1, Q_LORA), (1, KV_LORA)])
    g_w_uq = _unpad_w_uq(matmul_nn("grad_w_uq", qnb, dqb))
    g_w_ukv = _unpad_w_ukv(matmul_nn("grad_w_ukv", kvnb, dkvb))

    du, dc_re_d, dc_im_d, db_re_d, db_im_d, da_re, da_im, d_d = ssm_bwd(
        dy_pre, proj, st_re, st_im, bbd_re, bbd_im, cbd_re, cbd_im, d_vec, a_re, a_im, B, S, tm)
    place = lambda a: jnp.zeros((N_GROUPS, GROUP, N_STATE), F32).at[:, 0, :].set(
        a.reshape(N_GROUPS, N_STATE)).reshape(D_SSM, N_STATE)
    cots = [place(da_re), place(da_im),
            _block_diag_take(db_re_d, GROUP, N_STATE).reshape(D_SSM, N_STATE),
            _block_diag_take(db_im_d, GROUP, N_STATE).reshape(D_SSM, N_STATE)]
    g_lam_re, g_lam_im, g_log_dt, g_bt_re, g_bt_im = ssm_params_bwd(reps, cots)
    unbt = lambda a: jnp.transpose(a.reshape(N_GROUPS, GROUP, N_STATE), (0, 2, 1))
    g_c_re = jnp.transpose(_block_diag_take(dc_re_d, N_STATE, GROUP), (0, 2, 1))
    g_c_im = jnp.transpose(_block_diag_take(dc_im_d, N_STATE, GROUP), (0, 2, 1))

    def bwd_in(rows, bv, vecs, res):
        (duv, dpav, xv, dx2v), (sc, sh), (g1,), (w,) = rows, bv, vecs, res
        dproj = jnp.concatenate([duv, dpav], axis=1).astype(BF16)
        dh = _dot_nt(dproj, w[...])
        r, xh = _rms_stats(xv)
        n1 = xh * g1
        dn1 = dh * (1.0 + sc)
        dx = dx2v + _rms_bwd(dn1 * g1, xh, r)
        return [dx, t_val(n1 * (1.0 + sc) + sh), dproj], [_colsum(dh), _colsum(dh * n1)], [_colsum(dn1 * xh)]

    ((grad_x, h1b, dprojb), (d_shift1, d_scale1), (d_g1,)) = row_call(
        "bwd_in", bwd_in, B, S, tm, [du, dpa, x2d, dx2], [scale1, shift1], [norm1_g], [w_in_p],
        [(D, F32), t_out(D), (IN_PAD, BF16)], [D, D], [(1, D)])
    g_w_in = _unpad_w_in(matmul_nn("grad_w_in", h1b, dprojb))

    dmod = jnp.concatenate([d_shift1, d_scale1, d_gate1, d_shift2, d_scale2, d_gate2, d_fshift, d_fscale],
                           axis=2).reshape(B, 8 * D)
    small_part = {
        "norm1_g": d_g1, "ssm_lambda_re": g_lam_re,
        "ssm_lambda_im": g_lam_im, "ssm_b_re": unbt(g_bt_re), "ssm_b_im": unbt(g_bt_im), "ssm_c_re": g_c_re,
        "ssm_c_im": g_c_im, "ssm_d": d_d, "ssm_log_dt": g_log_dt, "q_norm_g": d_gq, "kv_norm_g": d_gkv,
        "ssm_out_g": d_gso, "attn_out_g": d_gao, "norm2_g": d_g2, "final_norm_g": d_gf}
    small_2d = [small_part[n].reshape(_dense_2d(given[n].shape)).astype(BF16) for n in SMALL]
    dmod_g, *rest = exchange(
        "exchange_last_grads",
        [dmod.astype(BF16)] + small_2d + [_cols_to_shards(g_w_in), _cols_to_shards(g_w_uq), g_w_ukv],
        [GATHER] * (1 + len(SMALL)) + [None, None, w_ukv.shape[2]])
    small_g, parts_b = rest[:len(SMALL)], rest[len(SMALL):]
    parts.update(zip(early, parts_b))
    dmod_all = dmod_g.reshape(N_DEV * B, 8 * D)
    dm_sh = lax.dynamic_slice_in_dim(dmod_all[:, :6 * D], me * n_mod, n_mod, axis=1)
    dfm_sh = lax.dynamic_slice_in_dim(dmod_all[:, 6 * D:], me * n_fmod, n_fmod, axis=1)
    g_ada_w, g_fada_w, g_biases = modulation_bwd(cond_all, dm_sh, dfm_sh, dmod_all)
    parts["ada_w"] = g_ada_w[None]
    parts["final_ada_w"] = g_fada_w[None]

    res_g, res_d, res_m, res_v = {}, {}, {}, {}
    for n in ["ada_w"] + big + ["final_ada_w"]:
        w_full = given[n]
        w2 = w_full.reshape(w_full.shape[-2], w_full.shape[-1])
        out = adamw("adamw_" + n, parts[n], w2, given["m_" + n].reshape(w2.shape), given["v_" + n].reshape(w2.shape))
        res_g[n], res_d[n], res_m[n], res_v[n] = [a.reshape(w_full.shape) for a in out]
    bias_g = {"ada_b": g_biases[None, :, :6 * D], "final_ada_b": g_biases[None, :, 6 * D:]}
    names = SMALL + BIASES
    items = []
    for n, g_parts in zip(names, small_g + [bias_g[b] for b in BIASES]):
        dense = _dense_2d(given[n].shape)
        items.append((g_parts, given[n].reshape(dense), given["m_" + n].reshape(dense), given["v_" + n].reshape(dense)))
    for n, out in zip(names, adamw_many("adamw_small", items)):
        res_g[n], res_d[n], res_m[n], res_v[n] = [a.reshape(given[n].shape) for a in out]

    order = ["ada_w", "ada_b", "norm1_g", "w_in", "ssm_lambda_re", "ssm_lambda_im", "ssm_b_re", "ssm_b_im",
             "ssm_c_re", "ssm_c_im", "ssm_d", "ssm_log_dt", "w_glu", "q_norm_g", "w_uq", "kv_norm_g", "w_ukv",
             "ssm_out_g", "attn_out_g", "w_out", "norm2_g", "w_ff1", "w_ff2", "final_ada_w", "final_ada_b",
             "final_norm_g"]
    return (loss, grad_x.reshape(B, S, D), *[res_g[n] for n in order], *[res_d[n] for n in order],
            *[res_m[n] for n in order], *[res_v[n] for n in order])
```

```python
import functools
import math

import numpy as np
import jax
import jax.numpy as jnp
from jax import lax
from jax.experimental import pallas as pl
from jax.experimental.pallas import tpu as pltpu

F32 = jnp.float32
BF16 = jnp.bfloat16

N_DEV = 8
EPS = 1e-6
D_SSM = 512
N_GROUPS = 32
GROUP = 16
N_STATE = 64
N_ST = N_GROUPS * N_STATE
Q_LORA = 384
KV_LORA = 256
QK_NOPE = 64
QK_ROPE = 32
V_HEAD = 64
N_HEADS = 8
HEAD_PAD = 128
QK_W = N_HEADS * HEAD_PAD
V_W = N_HEADS * V_HEAD
IN_COLS = D_SSM + Q_LORA + KV_LORA + QK_ROPE
IN_PAD = D_SSM + Q_LORA + KV_LORA + HEAD_PAD
ROPE_BASE = 10000.0
ATTN_SCALE = (QK_NOPE + QK_ROPE) ** -0.5
NEG = -1e30

ADAM_LR = 0.001
ADAM_B1 = 0.9
ADAM_B2 = 0.999
ADAM_EPS = 1e-08
ADAM_WD = 0.01
ADAM_STEP = 10

VMEM_LIMIT = 56 * 1024 * 1024
MESH_ID = pl.DeviceIdType.MESH


def _dot(a, b):
    return jnp.dot(a, b, preferred_element_type=F32)


def _dot_nt(a, b):
    return lax.dot_general(a, b, (((1,), (1,)), ((), ())), preferred_element_type=F32)


def _dot_tn(a, b):
    return lax.dot_general(a, b, (((0,), (0,)), ((), ())), preferred_element_type=F32)


def _rms_stats(x):
    r = lax.rsqrt(jnp.mean(x * x, axis=-1, keepdims=True) + EPS)
    return r, x * r


def _rms_bwd(dy, xh, r):
    return r * (dy - xh * jnp.mean(dy * xh, axis=-1, keepdims=True))


def _sigmoid(x):
    return 1.0 / (1.0 + jnp.exp(-x))


_GELU_C = math.sqrt(2.0 / math.pi)


def _gelu(x):
    t = jnp.tanh(_GELU_C * (x + 0.044715 * x * x * x))
    return 0.5 * x * (1.0 + t)


def _gelu_grad(x):
    t = jnp.tanh(_GELU_C * (x + 0.044715 * x * x * x))
    return 0.5 * (1.0 + t) + 0.5 * x * (1.0 - t * t) * _GELU_C * (1.0 + 3.0 * 0.044715 * x * x)


def _colsum(a):
    return jnp.sum(a, axis=0, keepdims=True)


def _params(sem=None):
    return pltpu.CompilerParams(dimension_semantics=sem, vmem_limit_bytes=VMEM_LIMIT)


def _mesh_pos():
    x, y, c = lax.axis_index("x"), lax.axis_index("y"), lax.axis_index("c")
    return x, y, c, 4 * x + 2 * y + c


def _peer(x, y, c, k):
    px = 1 - x if k & 4 else x
    py = 1 - y if k & 2 else y
    pc = 1 - c if k & 1 else c
    return (px, py, pc), 4 * px + 2 * py + pc


GATHER = "gather"
ANY_SPEC = pl.BlockSpec(memory_space=pl.ANY)


class Exchange:
    def __init__(self, arrays, kinds):
        self.arrays, self.kinds, self.n = list(arrays), list(kinds), len(arrays)

    def _shard(self, i):
        a, kind = self.arrays[i], self.kinds[i]
        if kind == GATHER:
            return a.shape
        return a.shape[1:] if kind is None else (a.shape[0], kind)

    def out_shapes(self):
        return [jax.ShapeDtypeStruct((N_DEV,) + tuple(self._shard(i)), self.arrays[i].dtype) for i in range(self.n)]

    def scratch(self):
        return [pltpu.SemaphoreType.DMA((self.n, N_DEV - 1)), pltpu.SemaphoreType.DMA((self.n, N_DEV - 1)),
                pltpu.SemaphoreType.DMA((self.n,))]

    def _src(self, ins, i, j):
        kind = self.kinds[i]
        if kind == GATHER:
            return ins[i]
        if kind is None:
            return ins[i].at[j]
        return ins[i].at[:, pl.ds(pl.multiple_of(j * kind, 128), kind)]

    def _copies(self, ins, outs, sems, with_received):
        send_sems, recv_sems, loc_sems = sems
        x, y, c, me = _mesh_pos()
        local, sent, received = [], [], []
        for i in range(self.n):
            local.append(pltpu.make_async_copy(self._src(ins, i, me), outs[i].at[me], loc_sems.at[i]))
            for k in range(1, N_DEV):
                peer, pidx = _peer(x, y, c, k)
                pair = dict(send_sem=send_sems.at[i, k - 1], recv_sem=recv_sems.at[i, k - 1], device_id=peer,
                            device_id_type=MESH_ID)
                sent.append(pltpu.make_async_remote_copy(
                    src_ref=self._src(ins, i, pidx), dst_ref=outs[i].at[me], **pair))
                if with_received:
                    received.append(pltpu.make_async_remote_copy(
                        src_ref=self._src(ins, i, pidx), dst_ref=outs[i].at[pidx], **pair))
        return local, sent, received

    def start(self, ins, outs, sems):
        local, sent, _ = self._copies(ins, outs, sems, False)
        for cp in local + sent:
            cp.start()

    def wait(self, ins, outs, sems):
        local, sent, received = self._copies(ins, outs, sems, True)
        for cp in received:
            cp.wait_recv()
        for cp in sent:
            cp.wait_send()
        for cp in local:
            cp.wait()


def exchange(name, arrays, kinds):
    ex = Exchange(arrays, kinds)
    n = ex.n

    def body(*refs):
        ins, outs, sems = refs[:n], refs[n:2 * n], refs[2 * n:]
        ex.start(ins, outs, sems)
        ex.wait(ins, outs, sems)

    return pl.pallas_call(
        body, name=name, out_shape=ex.out_shapes(), in_specs=[ANY_SPEC] * n, out_specs=[ANY_SPEC] * n,
        scratch_shapes=ex.scratch())(*arrays)


def all_gather(name, arrays):
    return exchange(name, arrays, [GATHER] * len(arrays))


def row_call(name, body, B, S, tm, rows, bvecs, vecs, res, row_outs, bacc, gacc, side=None):
    ns = S // tm
    T = B * S
    n_r, n_b, n_v, n_w = len(rows), len(bvecs), len(vecs), len(res)
    n_ro, n_ba, n_ga = len(row_outs), len(bacc), len(gacc)
    n_x = side.n if side is not None else 0

    def kern(*refs):
        i = 0
        r_refs = refs[i:i + n_r]; i += n_r
        b_refs = refs[i:i + n_b]; i += n_b
        v_refs = refs[i:i + n_v]; i += n_v
        w_hbm = refs[i:i + n_w]; i += n_w
        x_in = refs[i:i + n_x]; i += n_x
        ro_refs = refs[i:i + n_ro]; i += n_ro
        ba_refs = refs[i:i + n_ba]; i += n_ba
        ga_refs = refs[i:i + n_ga]; i += n_ga
        x_out = refs[i:i + n_x]; i += n_x
        w_vmem = refs[i:i + n_w]; i += n_w
        x_sems = refs[i:]
        b = pl.program_id(0)
        s = pl.program_id(1)
        first = jnp.logical_and(b == 0, s == 0)

        if n_x:
            @pl.when(first)
            def _start_side():
                side.start(x_in, x_out, x_sems)

        if n_w:
            @pl.when(first)
            def _load_weights():
                for h, v in zip(w_hbm, w_vmem):
                    pltpu.sync_copy(h, v)

        ro, ba, ga = body([r[...] for r in r_refs], [r[0] for r in b_refs], [r[...] for r in v_refs],
                          list(w_vmem))
        for ref, val in zip(ro_refs, ro):
            ref[...] = val.astype(ref.dtype)

        def accumulate(ref, val, is_first, idx):
            @pl.when(is_first)
            def _set():
                ref[idx] = val

            @pl.when(jnp.logical_not(is_first))
            def _add():
                ref[idx] = ref[idx] + val

        for ref, val in zip(ba_refs, ba):
            accumulate(ref, val, s == 0, 0)
        for ref, val in zip(ga_refs, ga):
            accumulate(ref, val, first, Ellipsis)

        if n_x:
            @pl.when(jnp.logical_and(b == B - 1, s == ns - 1))
            def _wait_side():
                side.wait(x_in, x_out, x_sems)

    row_arrays = [r[0] if isinstance(r, tuple) else r for r in rows]
    row_in_specs = [pl.BlockSpec(r[1], r[2]) if isinstance(r, tuple)
                    else pl.BlockSpec((tm, r.shape[1]), lambda b, s: (b * ns + s, 0)) for r in rows]
    ro_shapes = [jax.ShapeDtypeStruct(tuple(o[0]), o[1]) if len(o) == 4 else jax.ShapeDtypeStruct((T, o[0]), o[1])
                 for o in row_outs]
    ro_specs = [pl.BlockSpec(o[2], o[3]) if len(o) == 4 else pl.BlockSpec((tm, o[0]), lambda b, s: (b * ns + s, 0))
                for o in row_outs]
    in_specs = (row_in_specs
                + [pl.BlockSpec((1, 1, v.shape[2]), lambda b, s: (b, 0, 0)) for v in bvecs]
                + [pl.BlockSpec(v.shape, lambda b, s, nd=v.ndim: (0,) * nd) for v in vecs]
                + [ANY_SPEC] * (n_w + n_x))
    out_shape = (ro_shapes
                 + [jax.ShapeDtypeStruct((B, 1, w), F32) for w in bacc]
                 + [jax.ShapeDtypeStruct(tuple(sh), F32) for sh in gacc]
                 + (side.out_shapes() if n_x else []))
    out_specs = (ro_specs
                 + [pl.BlockSpec((1, 1, w), lambda b, s: (b, 0, 0)) for w in bacc]
                 + [pl.BlockSpec(tuple(sh), lambda b, s, nd=len(sh): (0,) * nd) for sh in gacc]
                 + [ANY_SPEC] * n_x)
    outs = pl.pallas_call(
        kern, name=name, grid=(B, ns), in_specs=in_specs, out_specs=out_specs, out_shape=out_shape,
        scratch_shapes=[pltpu.VMEM(w.shape, w.dtype) for w in res] + (side.scratch() if n_x else []),
        compiler_params=_params(("arbitrary", "arbitrary")),
    )(*row_arrays, *bvecs, *vecs, *res, *(side.arrays if n_x else []))
    n_acc = n_ro + n_ba + n_ga
    if n_x:
        return outs[:n_ro], outs[n_ro:n_ro + n_ba], outs[n_ro + n_ba:n_acc], outs[n_acc:]
    return outs[:n_ro], outs[n_ro:n_ro + n_ba], outs[n_ro + n_ba:]


def matmul_nn(name, at, b):
    K, T = at.shape
    N = b.shape[1]
    tk = min(K, 512)
    tn = N if N <= 1280 else (1024 if N % 1024 == 0 else 512)
    assert K % tk == 0 and N % tn == 0 and T % min(T, 2048) == 0, (K, N, T)
    tt = min(T, 2048)
    nt = T // tt

    def kern(a_ref, b_ref, o_ref, acc_ref):
        t = pl.program_id(2)
        part = _dot(a_ref[...], b_ref[...])

        @pl.when(t == 0)
        def _set():
            acc_ref[...] = part

        @pl.when(t > 0)
        def _add():
            acc_ref[...] = acc_ref[...] + part

        @pl.when(t == nt - 1)
        def _write():
            o_ref[...] = acc_ref[...].astype(o_ref.dtype)

    return pl.pallas_call(
        kern, name=name, grid=(K // tk, N // tn, nt),
        in_specs=[pl.BlockSpec((tk, tt), lambda i, j, t: (i, t)), pl.BlockSpec((tt, tn), lambda i, j, t: (t, j))],
        out_specs=pl.BlockSpec((tk, tn), lambda i, j, t: (i, j)),
        out_shape=jax.ShapeDtypeStruct((K, N), BF16),
        scratch_shapes=[pltpu.VMEM((tk, tn), F32)],
        compiler_params=_params(("parallel", "parallel", "arbitrary")),
    )(at, b)


def _rope_consts():
    inv_freq = ROPE_BASE ** (-jnp.arange(0, QK_ROPE, 2, dtype=F32) / QK_ROPE)
    zeros64 = jnp.zeros((64,), F32)
    zeros32 = jnp.zeros((32,), F32)
    zeros16 = jnp.zeros((16,), F32)
    ones16 = jnp.ones((16,), F32)
    invf = jnp.concatenate([zeros64, inv_freq, inv_freq, zeros32])[None, :]
    m_lo = jnp.concatenate([zeros64, ones16, zeros16, zeros32])[None, :]
    m_hi = jnp.concatenate([zeros64, zeros16, ones16, zeros32])[None, :]
    return invf, m_lo, m_hi


def _rope_tables(pos, invf, m_lo, m_hi):
    ang = pos * invf
    return jnp.cos(ang), jnp.sin(ang) * m_lo, jnp.sin(ang) * m_hi


def _rope_tile(t, cs, s_lo, s_hi, sign):
    up = pltpu.roll(t, HEAD_PAD - 16, axis=1)
    down = pltpu.roll(t, 16, axis=1)
    return t * cs + sign * (down * s_hi - up * s_lo)


def _heads(ref, width):
    return jnp.stack([ref[:, h * width:(h + 1) * width] for h in range(N_HEADS)], axis=0)


def _bmm(a, b, ca, cb):
    return lax.dot_general(a, b, (((ca,), (cb,)), ((0,), (0,))), preferred_element_type=F32)


LOG2E = math.log2(math.e)
EXP2_SCALE = ATTN_SCALE * LOG2E


def _scores_t(k_ref, q_ref, masked, tq):
    st = _bmm(_heads(k_ref, HEAD_PAD), _heads(q_ref, HEAD_PAD), 2, 2)
    if masked:
        key = lax.broadcasted_iota(jnp.int32, (tq, tq), 0)
        qry = lax.broadcasted_iota(jnp.int32, (tq, tq), 1)
        st = jnp.where((key <= qry)[None], st, NEG)
    return st


def attention_fwd(q, k, vt, B, S, tq, side):
    nq = S // tq
    ns = side.n
    pairs = [(i, j) for i in range(nq) for j in range(i + 1)]
    n_pairs = len(pairs)
    q_tab = jnp.asarray([p[0] for p in pairs], jnp.int32)
    k_tab = jnp.asarray([p[1] for p in pairs], jnp.int32)

    def kern(q_tab_ref, k_tab_ref, *refs):
        q_ref, k_ref, vt_ref = refs[:3]
        side_in = refs[3:3 + ns]
        o_ref, lse_ref = refs[3 + ns:5 + ns]
        side_out = refs[5 + ns:5 + 2 * ns]
        m_scr, l_scr, acc_scr = refs[5 + 2 * ns:8 + 2 * ns]
        sems = refs[8 + 2 * ns:]
        b, t = pl.program_id(0), pl.program_id(1)
        qi, ki = q_tab_ref[t], k_tab_ref[t]

        @pl.when(jnp.logical_and(b == 0, t == 0))
        def _start_side():
            side.start(side_in, side_out, sems)

        @pl.when(ki == 0)
        def _init():
            m_scr[...] = jnp.full(m_scr.shape, NEG, F32)
            l_scr[...] = jnp.zeros(l_scr.shape, F32)
            acc_scr[...] = jnp.zeros(acc_scr.shape, F32)

        def block(masked):
            st = _scores_t(k_ref, q_ref, masked, tq)
            m_prev = m_scr[...]
            m_new = jnp.maximum(m_prev, jnp.max(st, axis=1, keepdims=True))
            alpha = jnp.exp2((m_prev - m_new) * EXP2_SCALE)
            p = jnp.exp2((st - m_new) * EXP2_SCALE)
            l_scr[...] = alpha * l_scr[...] + jnp.sum(p, axis=1, keepdims=True)
            vt3 = vt_ref[...].reshape(N_HEADS, V_HEAD, tq)
            acc_scr[...] = alpha * acc_scr[...] + _bmm(vt3, p.astype(BF16), 2, 1)
            m_scr[...] = m_new

        @pl.when(ki < qi)
        def _full():
            block(False)

        @pl.when(ki == qi)
        def _diagonal():
            block(True)
            ot = acc_scr[...] / l_scr[...]
            for h in range(N_HEADS):
                o_ref[:, h * V_HEAD:(h + 1) * V_HEAD] = ot[h].T
            lse_ref[...] = m_scr[...] * ATTN_SCALE + jnp.log(l_scr[...])

        @pl.when(jnp.logical_and(b == B - 1, t == n_pairs - 1))
        def _wait_side():
            side.wait(side_in, side_out, sems)

    T = B * S
    grid_spec = pltpu.PrefetchScalarGridSpec(
        num_scalar_prefetch=2, grid=(B, n_pairs),
        in_specs=[pl.BlockSpec((tq, QK_W), lambda b, t, qt, kt: (b * nq + qt[t], 0)),
                  pl.BlockSpec((tq, QK_W), lambda b, t, qt, kt: (b * nq + kt[t], 0)),
                  pl.BlockSpec((V_W, tq), lambda b, t, qt, kt: (0, b * nq + kt[t]))] + [ANY_SPEC] * ns,
        out_specs=[pl.BlockSpec((tq, V_W), lambda b, t, qt, kt: (b * nq + qt[t], 0)),
                   pl.BlockSpec((N_HEADS, 1, tq), lambda b, t, qt, kt: (0, 0, b * nq + qt[t]))] + [ANY_SPEC] * ns,
        scratch_shapes=[pltpu.VMEM((N_HEADS, 1, tq), F32), pltpu.VMEM((N_HEADS, 1, tq), F32),
                        pltpu.VMEM((N_HEADS, V_HEAD, tq), F32)] + side.scratch())
    return pl.pallas_call(
        kern, name="attention_fwd", grid_spec=grid_spec,
        out_shape=[jax.ShapeDtypeStruct((T, V_W), F32), jax.ShapeDtypeStruct((N_HEADS, 1, T), F32)]
        + side.out_shapes(),
        compiler_params=_params(("arbitrary", "arbitrary")),
    )(q_tab, k_tab, q, k, vt, *side.arrays)


def attention_bwd(q, k, kt, v, do, lse, delta, B, S, tq, side):
    nq = S // tq
    ns = side.n
    pairs = [(j, i) for j in range(nq) for i in range(j, nq)]
    n_pairs = len(pairs)
    k_tab = jnp.asarray([p[0] for p in pairs], jnp.int32)
    q_tab = jnp.asarray([p[1] for p in pairs], jnp.int32)

    def kern(k_tab_ref, q_tab_ref, *refs):
        q_ref, k_ref, kt_ref, v_ref, do_ref, lse_ref, delta_ref = refs[:7]
        side_in = refs[7:7 + ns]
        dq_hbm, dk_ref, dv_ref = refs[7 + ns:10 + ns]
        side_out = refs[10 + ns:10 + 2 * ns]
        dq_acc, dk_acc, dv_acc = refs[10 + 2 * ns:13 + 2 * ns]
        sems = refs[13 + 2 * ns:]
        b, t = pl.program_id(0), pl.program_id(1)
        kj, qi = k_tab_ref[t], q_tab_ref[t]

        @pl.when(jnp.logical_and(b == 0, t == 0))
        def _start_side():
            side.start(side_in, side_out, sems)

        @pl.when(t == 0)
        def _zero_dq():
            dq_acc[...] = jnp.zeros(dq_acc.shape, F32)

        @pl.when(qi == kj)
        def _zero_dkv():
            dk_acc[...] = jnp.zeros(dk_acc.shape, F32)
            dv_acc[...] = jnp.zeros(dv_acc.shape, F32)

        def block(masked):
            pt = jnp.exp2(_scores_t(k_ref, q_ref, masked, tq) * EXP2_SCALE - lse_ref[...] * LOG2E)
            do3 = _heads(do_ref, V_HEAD).astype(BF16)
            dv_acc[...] = dv_acc[...] + _bmm(pt.astype(BF16), do3, 2, 1)
            dpt = _bmm(_heads(v_ref, V_HEAD), do3, 2, 2)
            dst = (pt * (dpt - delta_ref[...]) * ATTN_SCALE).astype(BF16)
            dk_acc[...] = dk_acc[...] + _bmm(dst, _heads(q_ref, HEAD_PAD), 2, 1)
            q_cols = pl.ds(pl.multiple_of(qi * tq, tq), tq)
            kt3 = kt_ref[...].reshape(N_HEADS, HEAD_PAD, tq)
            dq_acc[:, :, q_cols] = dq_acc[:, :, q_cols] + _bmm(kt3, dst, 2, 1)

        @pl.when(qi > kj)
        def _full():
            block(False)

        @pl.when(qi == kj)
        def _diagonal():
            block(True)

        @pl.when(qi == nq - 1)
        def _write_dkv():
            for h in range(N_HEADS):
                dk_ref[:, h * HEAD_PAD:(h + 1) * HEAD_PAD] = dk_acc[h]
                dv_ref[:, h * V_HEAD:(h + 1) * V_HEAD] = dv_acc[h]

        @pl.when(t == n_pairs - 1)
        def _write_dq():
            pltpu.sync_copy(dq_acc, dq_hbm.at[b])

        @pl.when(jnp.logical_and(b == B - 1, t == n_pairs - 1))
        def _wait_side():
            side.wait(side_in, side_out, sems)

    T = B * S
    qrow = lambda b, t, kt, qt: (b * nq + qt[t], 0)
    qcol3 = lambda b, t, kt, qt: (0, 0, b * nq + qt[t])
    krow = lambda b, t, kt, qt: (b * nq + kt[t], 0)
    grid_spec = pltpu.PrefetchScalarGridSpec(
        num_scalar_prefetch=2, grid=(B, n_pairs),
        in_specs=[pl.BlockSpec((tq, QK_W), qrow), pl.BlockSpec((tq, QK_W), krow),
                  pl.BlockSpec((QK_W, tq), lambda b, t, kt, qt: (0, b * nq + kt[t])), pl.BlockSpec((tq, V_W), krow),
                  pl.BlockSpec((tq, V_W), qrow), pl.BlockSpec((N_HEADS, 1, tq), qcol3),
                  pl.BlockSpec((N_HEADS, 1, tq), qcol3)] + [ANY_SPEC] * ns,
        out_specs=[ANY_SPEC, pl.BlockSpec((tq, QK_W), krow), pl.BlockSpec((tq, V_W), krow)] + [ANY_SPEC] * ns,
        scratch_shapes=[pltpu.VMEM((N_HEADS, HEAD_PAD, S), F32), pltpu.VMEM((N_HEADS, tq, HEAD_PAD), F32),
                        pltpu.VMEM((N_HEADS, tq, V_HEAD), F32)] + side.scratch())
    return pl.pallas_call(
        kern, name="attention_bwd", grid_spec=grid_spec,
        out_shape=[jax.ShapeDtypeStruct((B, N_HEADS, HEAD_PAD, S), F32), jax.ShapeDtypeStruct((T, QK_W), F32),
                   jax.ShapeDtypeStruct((T, V_W), F32)] + side.out_shapes(),
        compiler_params=_params(("arbitrary", "arbitrary")),
    )(k_tab, q_tab, q, k, kt, v, do, lse, delta, *side.arrays)


GB = 8
N_GB = N_GROUPS // GB
GB_U = GB * GROUP
GB_ST = GB * N_STATE


def _gb_u(j):
    return slice(j * GB_U, (j + 1) * GB_U)


def _gb_s(j):
    return slice(j * GB_ST, (j + 1) * GB_ST)


def ssm_param_fn(lr, li, ld, br, bi):
    dt = jnp.exp(ld)
    er = jnp.exp(lr * dt)
    ar = er * jnp.cos(li * dt)
    ai = er * jnp.sin(li * dt)
    nr = ar - 1.0
    den = lr * lr + li * li
    cr = (nr * lr + ai * li) / den
    ci = (ai * lr - nr * li) / den
    return ar, ai, cr * br - ci * bi, cr * bi + ci * br


def ssm_params_fwd(reps):
    def kern(lr, li, ld, br, bi, ar_o, ai_o, bbr_o, bbi_o):
        ar, ai, bbr, bbi = ssm_param_fn(lr[...], li[...], ld[...], br[...], bi[...])
        ar_o[...] = ar
        ai_o[...] = ai
        bbr_o[...] = bbr
        bbi_o[...] = bbi

    shp = jax.ShapeDtypeStruct(reps[0].shape, F32)
    return pl.pallas_call(kern, name="ssm_params_fwd", out_shape=[shp] * 4)(*reps)


def ssm_params_bwd(reps, cots):
    rows = reps[0].shape[0]

    def kern(lr, li, ld, br, bi, d_ar, d_ai, d_bbr, d_bbi, dlr_o, dli_o, dld_o, dbr_o, dbi_o):
        _, vjp = jax.vjp(ssm_param_fn, lr[...], li[...], ld[...], br[...], bi[...])
        dlr, dli, dld, dbr, dbi = vjp((d_ar[...], d_ai[...], d_bbr[...], d_bbi[...]))
        dlr_o[...] = jnp.sum(dlr.reshape(N_GROUPS, GROUP, N_STATE), axis=1)
        dli_o[...] = jnp.sum(dli.reshape(N_GROUPS, GROUP, N_STATE), axis=1)
        dld_o[...] = jnp.sum(jnp.sum(dld.reshape(N_GROUPS, GROUP, N_STATE), axis=1), axis=-1, keepdims=True)
        dbr_o[...] = dbr
        dbi_o[...] = dbi

    g = jax.ShapeDtypeStruct((N_GROUPS, N_STATE), F32)
    full = jax.ShapeDtypeStruct((rows, N_STATE), F32)
    return pl.pallas_call(
        kern, name="ssm_params_bwd",
        out_shape=[g, g, jax.ShapeDtypeStruct((N_GROUPS, 1), F32), full, full])(*reps, *cots)


def ssm_fwd(proj, b_re, b_im, c_re, c_im, d_vec, a_re, a_im, B, S, ts):
    ns = S // ts
    T = B * S

    def kern(u_ref, bre_ref, bim_ref, cre_ref, cim_ref, d_ref, are_ref, aim_ref,
             sre_ref, sim_ref, y_ref, car_re, car_im, bu_re, bu_im):
        s = pl.program_id(1)

        @pl.when(s == 0)
        def _reset():
            car_re[...] = jnp.zeros(car_re.shape, F32)
            car_im[...] = jnp.zeros(car_im.shape, F32)

        u = u_ref[...]
        ub = u.astype(BF16)
        for j in range(N_GB):
            uj, bj, sj = ub[:, _gb_u(j)], _gb_u(j), _gb_s(j)
            bu_re[:, sj] = _dot(uj, bre_ref[bj, :])
            bu_im[:, sj] = _dot(uj, bim_ref[bj, :])
        ar, ai = are_ref[...], aim_ref[...]

        def step(t, carry):
            xr, xi = carry
            row = pl.ds(t, 1)
            nr = ar * xr - ai * xi + bu_re[row, :]
            ni = ar * xi + ai * xr + bu_im[row, :]
            sre_ref[row, :] = nr
            sim_ref[row, :] = ni
            return nr, ni

        xr, xi = lax.fori_loop(0, ts, step, (car_re[0:1, :], car_im[0:1, :]))
        car_re[0:1, :] = xr
        car_im[0:1, :] = xi
        y = [_dot(sre_ref[:, _gb_s(j)].astype(BF16), cre_ref[_gb_s(j), :])
             - _dot(sim_ref[:, _gb_s(j)].astype(BF16), cim_ref[_gb_s(j), :]) for j in range(N_GB)]
        y_ref[...] = jnp.concatenate(y, axis=1) + d_ref[...] * u

    row = lambda b, s: (b * ns + s, 0)
    whole = lambda b, s: (0, 0)
    return pl.pallas_call(
        kern, name="ssm_fwd", grid=(B, ns),
        in_specs=[pl.BlockSpec((ts, D_SSM), row),
                  pl.BlockSpec((D_SSM, GB_ST), whole), pl.BlockSpec((D_SSM, GB_ST), whole),
                  pl.BlockSpec((N_ST, GB_U), whole), pl.BlockSpec((N_ST, GB_U), whole),
                  pl.BlockSpec((1, D_SSM), whole), pl.BlockSpec((1, N_ST), whole), pl.BlockSpec((1, N_ST), whole)],
        out_specs=[pl.BlockSpec((ts, N_ST), row), pl.BlockSpec((ts, N_ST), row), pl.BlockSpec((ts, D_SSM), row)],
        out_shape=[jax.ShapeDtypeStruct((T, N_ST), F32), jax.ShapeDtypeStruct((T, N_ST), F32),
                   jax.ShapeDtypeStruct((T, D_SSM), F32)],
        scratch_shapes=[pltpu.VMEM((8, N_ST), F32), pltpu.VMEM((8, N_ST), F32),
                        pltpu.VMEM((ts, N_ST), F32), pltpu.VMEM((ts, N_ST), F32)],
        compiler_params=_params(("arbitrary", "arbitrary")),
    )(proj, b_re, b_im, c_re, c_im, d_vec, a_re, a_im)


def ssm_bwd(dy, proj, s_re, s_im, b_re, b_im, c_re, c_im, d_vec, a_re, a_im, B, S, ts):
    ns = S // ts
    T = B * S
    t8 = ts // 8

    def kern(dy_ref, u_ref, sre_ref, sim_ref, pre_ref, pim_ref, bre_ref, bim_ref, cre_ref, cim_ref,
             d_ref, are_ref, aim_ref,
             du_ref, dcre_ref, dcim_ref, dbre_ref, dbim_ref, dare_ref, daim_ref, dd_ref,
             car_re, car_im, g_re, g_im):
        b, s = pl.program_id(0), pl.program_id(1)
        first = jnp.logical_and(b == 0, s == 0)

        @pl.when(s == 0)
        def _reset():
            car_re[...] = jnp.zeros(car_re.shape, F32)
            car_im[...] = jnp.zeros(car_im.shape, F32)

        dyv = dy_ref[...]
        dyb = dyv.astype(BF16)
        u = u_ref[...]
        for j in range(N_GB):
            g_re[:, _gb_s(j)] = _dot_nt(dyb[:, _gb_u(j)], cre_ref[_gb_s(j), :])
            g_im[:, _gb_s(j)] = -_dot_nt(dyb[:, _gb_u(j)], cim_ref[_gb_s(j), :])
        ar, ai = are_ref[...], aim_ref[...]

        def step(i, carry):
            gr_next, gi_next = carry
            row = pl.ds(ts - 1 - i, 1)
            gr = g_re[row, :] + ar * gr_next + ai * gi_next
            gi = g_im[row, :] + ar * gi_next - ai * gr_next
            g_re[row, :] = gr
            g_im[row, :] = gi
            return gr, gi

        gr0, gi0 = lax.fori_loop(0, ts, step, (car_re[0:1, :], car_im[0:1, :]))
        car_re[0:1, :] = gr0
        car_im[0:1, :] = gi0

        gr, gi = g_re[...], g_im[...]
        sr, si = sre_ref[...], sim_ref[...]
        has_prev = (s < ns - 1).astype(F32)
        row_id = lax.broadcasted_iota(jnp.int32, (ts, N_ST), 0)
        spr = jnp.where(row_id == 0, pre_ref[7:8, :] * has_prev, pltpu.roll(sr, 1, axis=0))
        spi = jnp.where(row_id == 0, pim_ref[7:8, :] * has_prev, pltpu.roll(si, 1, axis=0))
        grb, gib, ub = gr.astype(BF16), gi.astype(BF16), u.astype(BF16)
        srb, sib = sr.astype(BF16), si.astype(BF16)
        cat0 = lambda f: jnp.concatenate([f(j) for j in range(N_GB)], axis=0)
        cat1 = lambda f: jnp.concatenate([f(j) for j in range(N_GB)], axis=1)
        parts = [
            (dcre_ref, cat0(lambda j: _dot_tn(srb[:, _gb_s(j)], dyb[:, _gb_u(j)]))),
            (dcim_ref, cat0(lambda j: -_dot_tn(sib[:, _gb_s(j)], dyb[:, _gb_u(j)]))),
            (dbre_ref, cat0(lambda j: _dot_tn(ub[:, _gb_u(j)], grb[:, _gb_s(j)]))),
            (dbim_ref, cat0(lambda j: _dot_tn(ub[:, _gb_u(j)], gib[:, _gb_s(j)]))),
            (dare_ref, _colsum(gr * spr + gi * spi)),
            (daim_ref, _colsum(gi * spr - gr * spi)),
            (dd_ref, _colsum(dyv * u)),
        ]
        du = cat1(lambda j: _dot_nt(grb[:, _gb_s(j)], bre_ref[_gb_u(j), :])
                  + _dot_nt(gib[:, _gb_s(j)], bim_ref[_gb_u(j), :]))
        du_ref[...] = du + d_ref[...] * dyv

        @pl.when(first)
        def _set():
            for ref, val in parts:
                ref[...] = val

        @pl.when(jnp.logical_not(first))
        def _add():
            for ref, val in parts:
                ref[...] = ref[...] + val

    row = lambda b, s: (b * ns + (ns - 1 - s), 0)
    prev = lambda b, s: (jnp.maximum((b * ns + (ns - 1 - s)) * t8 - 1, 0), 0)
    whole = lambda b, s: (0, 0)
    acc_shapes = [(N_ST, GB_U), (N_ST, GB_U), (D_SSM, GB_ST), (D_SSM, GB_ST), (1, N_ST), (1, N_ST), (1, D_SSM)]
    return pl.pallas_call(
        kern, name="ssm_bwd", grid=(B, ns),
        in_specs=[pl.BlockSpec((ts, D_SSM), row), pl.BlockSpec((ts, D_SSM), row),
                  pl.BlockSpec((ts, N_ST), row), pl.BlockSpec((ts, N_ST), row),
                  pl.BlockSpec((8, N_ST), prev), pl.BlockSpec((8, N_ST), prev),
                  pl.BlockSpec((D_SSM, GB_ST), whole), pl.BlockSpec((D_SSM, GB_ST), whole),
                  pl.BlockSpec((N_ST, GB_U), whole), pl.BlockSpec((N_ST, GB_U), whole),
                  pl.BlockSpec((1, D_SSM), whole), pl.BlockSpec((1, N_ST), whole), pl.BlockSpec((1, N_ST), whole)],
        out_specs=[pl.BlockSpec((ts, D_SSM), row)] + [pl.BlockSpec(sh, whole) for sh in acc_shapes],
        out_shape=[jax.ShapeDtypeStruct((T, D_SSM), F32)] + [jax.ShapeDtypeStruct(sh, F32) for sh in acc_shapes],
        scratch_shapes=[pltpu.VMEM((8, N_ST), F32), pltpu.VMEM((8, N_ST), F32),
                        pltpu.VMEM((ts, N_ST), F32), pltpu.VMEM((ts, N_ST), F32)],
        compiler_params=_params(("arbitrary", "arbitrary")),
    )(dy, proj, s_re, s_im, s_re, s_im, b_re, b_im, c_re, c_im, d_vec, a_re, a_im)


def modulation_fwd(c_all, ada_w, ada_b, fada_w, fada_b):
    def kern(c_ref, w_ref, b_ref, fw_ref, fb_ref, cond_ref, mod_ref, fmod_ref):
        cv = c_ref[...]
        cond = (cv * _sigmoid(cv)).astype(BF16)
        cond_ref[...] = cond
        mod_ref[...] = _dot(cond, w_ref[...].astype(BF16)) + b_ref[...]
        fmod_ref[...] = _dot(cond, fw_ref[...].astype(BF16)) + fb_ref[...]

    n = c_all.shape[0]
    return pl.pallas_call(
        kern, name="modulation_fwd",
        out_shape=[jax.ShapeDtypeStruct(c_all.shape, BF16), jax.ShapeDtypeStruct((n, ada_w.shape[1]), F32),
                   jax.ShapeDtypeStruct((n, fada_w.shape[1]), F32)],
        compiler_params=_params(),
    )(c_all, ada_w, ada_b, fada_w, fada_b)


def modulation_bwd(cond_all, dmod, dfmod, dmod_all):
    def kern(c_ref, dm_ref, dfm_ref, all_ref, gw_ref, gfw_ref, gb_ref):
        cond = c_ref[...]
        gw_ref[...] = _dot_tn(cond, dm_ref[...].astype(BF16))
        gfw_ref[...] = _dot_tn(cond, dfm_ref[...].astype(BF16))
        gb_ref[...] = _colsum(all_ref[...].astype(F32))

    d = cond_all.shape[1]
    return pl.pallas_call(
        kern, name="modulation_bwd",
        out_shape=[jax.ShapeDtypeStruct((d, dmod.shape[1]), F32), jax.ShapeDtypeStruct((d, dfmod.shape[1]), F32),
                   jax.ShapeDtypeStruct((1, dmod_all.shape[1]), F32)],
        compiler_params=_params(),
    )(cond_all, dmod, dfmod, dmod_all)


def _adamw_update(p_ref, w_ref, m_ref, v_ref, g_o, d_o, m_o, v_o):
    g = p_ref[0].astype(F32)
    for i in range(1, p_ref.shape[0]):
        g = g + p_ref[i].astype(F32)
    m_new = ADAM_B1 * m_ref[...] + (1.0 - ADAM_B1) * g
    v_new = ADAM_B2 * v_ref[...] + (1.0 - ADAM_B2) * (g * g)
    m_hat = m_new / (1.0 - ADAM_B1 ** ADAM_STEP)
    v_hat = v_new / (1.0 - ADAM_B2 ** ADAM_STEP)
    g_o[...] = g
    d_o[...] = -ADAM_LR * (m_hat / (jnp.sqrt(v_hat) + ADAM_EPS) + ADAM_WD * w_ref[...])
    m_o[...] = m_new
    v_o[...] = v_new


def adamw_many(name, items):
    k = len(items)

    def kern(*refs):
        ins, outs = refs[:4 * k], refs[4 * k:]
        for i in range(k):
            _adamw_update(*ins[4 * i:4 * i + 4], *outs[4 * i:4 * i + 4])

    flat = [a for item in items for a in item]
    out_shape = [jax.ShapeDtypeStruct(item[1].shape, F32) for item in items for _ in range(4)]
    outs = pl.pallas_call(kern, name=name, out_shape=out_shape, compiler_params=_params())(*flat)
    return [tuple(outs[4 * i:4 * i + 4]) for i in range(k)]


def adamw(name, parts, w, m, v):
    n, R, C = parts.shape
    tr = R
    while n * tr * C * 4 > 4 * 1024 * 1024 and tr % 32 == 0:
        tr //= 2

    def kern(p_ref, w_ref, m_ref, v_ref, g_o, d_o, m_o, v_o):
        _adamw_update(p_ref, w_ref, m_ref, v_ref, g_o, d_o, m_o, v_o)

    blk = pl.BlockSpec((tr, C), lambda i: (i, 0))
    shp = jax.ShapeDtypeStruct((R, C), F32)
    return pl.pallas_call(
        kern, name=name, grid=(R // tr,),
        in_specs=[pl.BlockSpec((n, tr, C), lambda i: (0, i, 0)), blk, blk, blk],
        out_specs=[blk] * 4, out_shape=[shp] * 4,
        compiler_params=_params(("parallel",)),
    )(parts, w, m, v)


def _cols_from_gathered(g):
    return jnp.transpose(g, (1, 0, 2)).reshape(g.shape[1], N_DEV * g.shape[2])


def _cols_to_shards(w):
    k, n8 = w.shape
    return jnp.transpose(w.reshape(k, N_DEV, n8 // N_DEV), (1, 0, 2))


def _pad_w_in(w):
    z = functools.partial(jnp.zeros, dtype=w.dtype)
    k = w.shape[0]
    cut = D_SSM + Q_LORA + KV_LORA
    return jnp.concatenate([w[:, :cut], z((k, 64)), w[:, cut:], z((k, 32))], axis=1)


def _unpad_w_in(w):
    cut = D_SSM + Q_LORA + KV_LORA
    return jnp.concatenate([w[:, :cut], w[:, cut + 64:cut + 96]], axis=1)


def _pad_w_uq(w):
    k = w.shape[0]
    w3 = w.reshape(k, N_HEADS, QK_NOPE + QK_ROPE)
    return jnp.pad(w3, ((0, 0), (0, 0), (0, HEAD_PAD - QK_NOPE - QK_ROPE))).reshape(k, QK_W)


def _unpad_w_uq(w):
    k = w.shape[0]
    return w.reshape(k, N_HEADS, HEAD_PAD)[:, :, :QK_NOPE + QK_ROPE].reshape(k, N_HEADS * (QK_NOPE + QK_ROPE))


def _pad_w_ukv(w):
    k = w.shape[0]
    w3 = w.reshape(k, N_HEADS, QK_NOPE + V_HEAD)
    kpart = jnp.pad(w3[:, :, :QK_NOPE], ((0, 0), (0, 0), (0, HEAD_PAD - QK_NOPE))).reshape(k, QK_W)
    vpart = w3[:, :, QK_NOPE:].reshape(k, V_W)
    return jnp.concatenate([kpart, vpart], axis=1)


def _unpad_w_ukv(w):
    k = w.shape[0]
    kpart = w[:, :QK_W].reshape(k, N_HEADS, HEAD_PAD)[:, :, :QK_NOPE]
    vpart = w[:, QK_W:].reshape(k, N_HEADS, V_HEAD)
    return jnp.concatenate([kpart, vpart], axis=2).reshape(k, N_HEADS * (QK_NOPE + V_HEAD))


def _block_diag(blocks):
    g, r, c = blocks.shape
    b4 = blocks.reshape(N_GB, GB, r, c)
    keep = jnp.eye(GB, dtype=bool)[None, :, None, :, None]
    return jnp.where(keep, b4[:, :, :, None, :], 0).reshape(g * r, GB * c)


def _block_diag_take(stacked, r, c):
    d5 = stacked.reshape(N_GB, GB, r, GB, c)
    idx = jnp.arange(GB)
    return jnp.transpose(d5[:, idx, :, idx, :], (1, 0, 2, 3)).reshape(N_GROUPS, r, c)


SMALL = ["norm1_g", "ssm_lambda_re", "ssm_lambda_im", "ssm_b_re", "ssm_b_im", "ssm_c_re", "ssm_c_im",
         "ssm_d", "ssm_log_dt", "q_norm_g", "kv_norm_g", "ssm_out_g", "attn_out_g", "norm2_g", "final_norm_g"]
BIASES = ["ada_b", "final_ada_b"]


def _dense_2d(shape):
    dims = [d for d in shape[1:]] if len(shape) > 1 and shape[0] == 1 else list(shape)
    if len(dims) == 1:
        return (1, dims[0])
    return (dims[0], int(np.prod(dims[1:])))


def kernel(x, c, positions, ada_w, ada_b, norm1_g, w_in, ssm_lambda_re, ssm_lambda_im, ssm_b_re, ssm_b_im, ssm_c_re, ssm_c_im, ssm_d, ssm_log_dt, w_glu, q_norm_g, w_uq, kv_norm_g, w_ukv, ssm_out_g, attn_out_g, w_out, norm2_g, w_ff1, w_ff2, final_ada_w, final_ada_b, final_norm_g, loss_target, m_ada_w, m_ada_b, m_norm1_g, m_w_in, m_ssm_lambda_re, m_ssm_lambda_im, m_ssm_b_re, m_ssm_b_im, m_ssm_c_re, m_ssm_c_im, m_ssm_d, m_ssm_log_dt, m_w_glu, m_q_norm_g, m_w_uq, m_kv_norm_g, m_w_ukv, m_ssm_out_g, m_attn_out_g, m_w_out, m_norm2_g, m_w_ff1, m_w_ff2, m_final_ada_w, m_final_ada_b, m_final_norm_g, v_ada_w, v_ada_b, v_norm1_g, v_w_in, v_ssm_lambda_re, v_ssm_lambda_im, v_ssm_b_re, v_ssm_b_im, v_ssm_c_re, v_ssm_c_im, v_ssm_d, v_ssm_log_dt, v_w_glu, v_q_norm_g, v_w_uq, v_kv_norm_g, v_w_ukv, v_ssm_out_g, v_attn_out_g, v_w_out, v_norm2_g, v_w_ff1, v_w_ff2, v_final_ada_w, v_final_ada_b, v_final_norm_g):
    given = dict(locals())
    B, S, D = x.shape
    T = B * S
    tm = min(256, S)
    me = 4 * lax.axis_index("x") + 2 * lax.axis_index("y") + lax.axis_index("c")

    big = ["w_in", "w_glu", "w_uq", "w_ukv", "w_out", "w_ff1", "w_ff2"]
    shards = {n: given[n][0] for n in big}
    early, late = ["w_in", "w_uq", "w_ukv"], ["w_glu", "w_out", "w_ff1", "w_ff2"]
    c_g, w_in_g = all_gather("gather_first_weights", [c, shards["w_in"].astype(BF16)])
    c_all = c_g.reshape(N_DEV * B, D)
    w_in_p = _pad_w_in(_cols_from_gathered(w_in_g))

    n_mod = ada_w.shape[2]
    n_fmod = final_ada_w.shape[1]
    ada_b_sh = lax.dynamic_slice_in_dim(ada_b, me * n_mod, n_mod, axis=1)
    fada_b_sh = lax.dynamic_slice_in_dim(final_ada_b[None, :], me * n_fmod, n_fmod, axis=1)
    cond_all, mod_sh, fmod_sh = modulation_fwd(c_all, ada_w[0], ada_b_sh, final_ada_w, fada_b_sh)
    mod_g, fmod_g = all_gather("gather_modulation", [mod_sh, fmod_sh])
    mod_mine = lax.dynamic_slice_in_dim(_cols_from_gathered(mod_g), me * B, B, axis=0)
    fmod_mine = lax.dynamic_slice_in_dim(_cols_from_gathered(fmod_g), me * B, B, axis=0)
    shift1, scale1, gate1, shift2, scale2, gate2 = [mod_mine[:, None, i * D:(i + 1) * D] for i in range(6)]
    fshift, fscale = fmod_mine[:, None, :D], fmod_mine[:, None, D:]

    x2d = x.reshape(T, D)
    tgt2d = loss_target.reshape(T, D)
    pos = positions.astype(F32).reshape(T, 1)
    invf, m_lo, m_hi = _rope_consts()
    rope_mask = m_lo + m_hi

    def fwd_in(rows, bv, vecs, res):
        (xv,), (sc, sh), (g1,), (w,) = rows, bv, vecs, res
        r, xh = _rms_stats(xv)
        h = xh * g1 * (1.0 + sc) + sh
        return [_dot(h.astype(BF16), w[...])], [], []

    qkv_gather = Exchange([shards["w_uq"].astype(BF16), shards["w_ukv"].astype(BF16)], [GATHER, GATHER])
    (proj,), _, _, (w_uq_g, w_ukv_g) = row_call(
        "fwd_in", fwd_in, B, S, tm, [x2d], [scale1, shift1], [norm1_g], [w_in_p], [(IN_PAD, F32)], [], [],
        side=qkv_gather)
    w_uq_p = _pad_w_uq(_cols_from_gathered(w_uq_g))
    w_ukv_p = _pad_w_ukv(_cols_from_gathered(w_ukv_g))

    def fwd_qkv(rows, bv, vecs, res):
        (pr, ps), (gq, gkv, fr, lo, hi), (wq, wkv) = rows, vecs, res
        cs, s_lo, s_hi = _rope_tables(ps, fr, lo, hi)
        _, qh = _rms_stats(pr[:, D_SSM:D_SSM + Q_LORA])
        _, kvh = _rms_stats(pr[:, D_SSM + Q_LORA:D_SSM + Q_LORA + KV_LORA])
        qf = _dot((qh * gq).astype(BF16), wq[...])
        kvf = _dot((kvh * gkv).astype(BF16), wkv[...])
        k_rope = _rope_tile(pr[:, IN_PAD - HEAD_PAD:], cs, s_lo, s_hi, 1.0)
        q_t, k_t = [], []
        for h in range(N_HEADS):
            hs = slice(h * HEAD_PAD, (h + 1) * HEAD_PAD)
            q_t.append(_rope_tile(qf[:, hs], cs, s_lo, s_hi, 1.0))
            k_t.append(kvf[:, hs] + k_rope)
        kf, vf = jnp.concatenate(k_t, axis=1), kvf[:, QK_W:]
        return [jnp.concatenate(q_t, axis=1), kf, vf, t_val(kf), t_val(vf), cs, s_lo, s_hi], [], []

    ns = S // tm
    col_tile = lambda b, s: (0, b * ns + s)
    t_out = lambda w: ((w, T), BF16, (w, tm), col_tile)
    t_val = lambda v: v.astype(BF16).T
    (q_r, k_r, v_r, k_t, v_t, rope_cs, rope_lo, rope_hi), _, _ = row_call(
        "fwd_qkv", fwd_qkv, B, S, tm, [proj, pos], [], [q_norm_g, kv_norm_g, invf, m_lo, m_hi],
        [w_uq_p, w_ukv_p],
        [(QK_W, BF16), (QK_W, BF16), (V_W, BF16), ((QK_W, T), BF16, (QK_W, tm), col_tile),
         ((V_W, T), BF16, (V_W, tm), col_tile), (HEAD_PAD, F32), (HEAD_PAD, F32), (HEAD_PAD, F32)], [], [])
    late_gather = Exchange([shards[n].astype(BF16) for n in late], [GATHER] * len(late))
    y_attn, lse, *late_g = attention_fwd(q_r, k_r, v_t, B, S, tm, late_gather)
    gw = dict(zip(late, late_g))
    w_glu_f = _cols_from_gathered(gw["w_glu"])
    w_out_f = gw["w_out"].reshape(-1, D)
    w_ff1_f = _cols_from_gathered(gw["w_ff1"])
    w_ff2_f = gw["w_ff2"].reshape(-1, D)

    rep = lambda a: jnp.repeat(a, GROUP, axis=0)
    lam_rep = [rep(ssm_lambda_re[0]), rep(ssm_lambda_im[0]),
               rep(jnp.broadcast_to(ssm_log_dt[0][:, None], (N_GROUPS, N_STATE)))]
    bt = lambda a: jnp.transpose(a, (0, 2, 1)).reshape(D_SSM, N_STATE)
    reps = lam_rep + [bt(ssm_b_re[0]), bt(ssm_b_im[0])]
    a_re_rep, a_im_rep, bb_re, bb_im = ssm_params_fwd(reps)
    a_re = a_re_rep.reshape(N_GROUPS, GROUP, N_STATE)[:, 0, :].reshape(1, N_ST)
    a_im = a_im_rep.reshape(N_GROUPS, GROUP, N_STATE)[:, 0, :].reshape(1, N_ST)
    bbd_re = _block_diag(bb_re.reshape(N_GROUPS, GROUP, N_STATE)).astype(BF16)
    bbd_im = _block_diag(bb_im.reshape(N_GROUPS, GROUP, N_STATE)).astype(BF16)
    cbd_re = _block_diag(jnp.transpose(ssm_c_re[0], (0, 2, 1))).astype(BF16)
    cbd_im = _block_diag(jnp.transpose(ssm_c_im[0], (0, 2, 1))).astype(BF16)
    d_vec = ssm_d.reshape(1, D_SSM)
    st_re, st_im, y_pre = ssm_fwd(proj, bbd_re, bbd_im, cbd_re, cbd_im, d_vec, a_re, a_im, B, S, tm)

    def fwd_mix(rows, bv, vecs, res):
        (yp, ya, xv), (g1v,), (gso, gao), (wg, wo) = rows, bv, vecs, res
        z = _dot(_gelu(yp).astype(BF16), wg[...])
        y_ssm = z[:, :D_SSM] * _sigmoid(z[:, D_SSM:])
        _, sh = _rms_stats(y_ssm)
        _, ah = _rms_stats(ya)
        o = _dot((sh * gso).astype(BF16), wo[0:D_SSM, :]) + _dot((ah * gao).astype(BF16), wo[D_SSM:, :])
        return [z, o, xv + g1v * o], [], []

    (z, o, x2), _, _ = row_call("fwd_mix", fwd_mix, B, S, tm, [y_pre, y_attn, x2d], [gate1],
                                [ssm_out_g, attn_out_g], [w_glu_f, w_out_f],
                                [(2 * D_SSM, F32), (D, F32), (D, F32)], [], [])

    def final_out(x3, gf, fsc, fsh):
        r3, xh3 = _rms_stats(x3)
        n3 = xh3 * gf
        return r3, xh3, n3, n3 * (1.0 + fsc) + fsh

    def mlp_hidden(xv, g2, sc, sh, w1):
        r2, xh2 = _rms_stats(xv)
        n2 = xh2 * g2
        h2 = n2 * (1.0 + sc) + sh
        return r2, xh2, n2, h2, _dot(h2.astype(BF16), w1[...])

    def fwd_mlp(rows, bv, vecs, res):
        (xv, tg), (sc, sh, g2v, fsc, fsh), (g2, gf), (w1, w2) = rows, bv, vecs, res
        _, _, _, _, a = mlp_hidden(xv, g2, sc, sh, w1)
        ra = jnp.maximum(a, 0.0)
        ff = _dot((ra * ra).astype(BF16), w2[...])
        x3 = xv + g2v * ff
        _, _, _, out = final_out(x3, gf, fsc, fsh)
        err = out - tg
        loss = 0.5 * jnp.sum(jnp.mean(err * err, axis=-1, keepdims=True), axis=0, keepdims=True)
        return [ff, x3], [], [jnp.broadcast_to(loss, (1, 128))]

    fnorm_g = final_norm_g[None, :]
    (ff, x3), _, (loss_acc,) = row_call(
        "fwd_mlp", fwd_mlp, B, S, tm, [x2, tgt2d], [scale2, shift2, gate2, fscale, fshift], [norm2_g, fnorm_g],
        [w_ff1_f, w_ff2_f], [(D, F32), (D, F32)], [], [(1, 128)])
    loss = lax.psum(loss_acc[0, 0], ("x", "y", "c"))

    def bwd_mlp(rows, bv, vecs, res):
        (x3v, tg, x2v, ffv), (sc, sh, g2v, fsc, fsh), (g2, gf), (w1, w2) = rows, bv, vecs, res
        r3, xh3, n3, out = final_out(x3v, gf, fsc, fsh)
        dout = (out - tg) * (1.0 / D)
        dn3 = dout * (1.0 + fsc)
        dx3 = _rms_bwd(dn3 * gf, xh3, r3)
        dff = dx3 * g2v
        r2, xh2, n2, h2, a = mlp_hidden(x2v, g2, sc, sh, w1)
        ra = jnp.maximum(a, 0.0)
        dffb = dff.astype(BF16)
        da = _dot_nt(dffb, w2[...]) * (2.0 * ra)
        dab = da.astype(BF16)
        dh2 = _dot_nt(dab, w1[...])
        dn2 = dh2 * (1.0 + sc)
        dx2 = dx3 + _rms_bwd(dn2 * g2, xh2, r2)
        return ([dx2, t_val(h2), t_val(ra * ra), dab, dffb],
                [_colsum(dout), _colsum(dout * n3), _colsum(dx3 * ffv), _colsum(dh2), _colsum(dh2 * n2)],
                [_colsum(dn3 * xh3), _colsum(dn2 * xh2)])

    ((dx2, h2b, fb, dab, dffb), (d_fshift, d_fscale, d_gate2, d_shift2, d_scale2), (d_gf, d_g2)) = row_call(
        "bwd_mlp", bwd_mlp, B, S, tm, [x3, tgt2d, x2, ff], [scale2, shift2, gate2, fscale, fshift],
        [norm2_g, fnorm_g], [w_ff1_f, w_ff2_f],
        [(D, F32), t_out(D), t_out(4 * D), (4 * D, BF16), (D, BF16)], [D] * 5, [(1, D), (1, D)])
    g_w_ff1 = matmul_nn("grad_w_ff1", h2b, dab)
    g_w_ff2 = matmul_nn("grad_w_ff2", fb, dffb)

    def bwd_mix(rows, bv, vecs, res):
        (dxv, ov, zv, ya, yp), (g1v,), (gso, gao), (wo, wg) = rows, bv, vecs, res
        do = (dxv * g1v).astype(BF16)
        dyn = _dot_nt(do, wo[...])
        z1, sg = zv[:, :D_SSM], _sigmoid(zv[:, D_SSM:])
        y_ssm = z1 * sg
        rs, sh = _rms_stats(y_ssm)
        ra, ah = _rms_stats(ya)
        dyn_s, dyn_a = dyn[:, :D_SSM], dyn[:, D_SSM:]
        dy_ssm = _rms_bwd(dyn_s * gso, sh, rs)
        dy_attn = _rms_bwd(dyn_a * gao, ah, ra)
        dz = jnp.concatenate([dy_ssm * sg, dy_ssm * z1 * sg * (1.0 - sg)], axis=1).astype(BF16)
        dy_pre = _dot_nt(dz, wg[...]) * _gelu_grad(yp)
        yn = jnp.concatenate([sh * gso, ah * gao], axis=1)
        delta = jnp.sum((dy_attn * ya).T.reshape(N_HEADS, V_HEAD, tm), axis=1, keepdims=True)
        return ([do, t_val(yn), dz, t_val(_gelu(yp)), dy_attn, dy_pre, delta], [_colsum(dxv * ov)],
                [_colsum(dyn_s * sh), _colsum(dyn_a * ah)])

    ((dob, ynb, dzb, ygb, dy_attn, dy_pre, delta), (d_gate1,), (d_gso, d_gao)) = row_call(
        "bwd_mix", bwd_mix, B, S, tm, [dx2, o, z, y_attn, y_pre], [gate1], [ssm_out_g, attn_out_g],
        [w_out_f, w_glu_f],
        [(D, BF16), t_out(D), (2 * D_SSM, BF16), t_out(D_SSM), (V_W, F32), (D_SSM, F32),
         ((N_HEADS, 1, T), F32, (N_HEADS, 1, tm), lambda b, s: (0, 0, b * ns + s))],
        [D], [(1, D_SSM), (1, V_W)])
    g_w_out = matmul_nn("grad_w_out", ynb, dob)
    g_w_glu = matmul_nn("grad_w_glu", ygb, dzb)

    grads_a = Exchange([g_w_glu, g_w_out.reshape(N_DEV, -1, D), g_w_ff1, g_w_ff2.reshape(N_DEV, -1, D)],
                       [w_glu.shape[2], None, w_ff1.shape[2], None])
    dq_t, dk_r, dv_r, *parts_a = attention_bwd(q_r, k_r, k_t, v_r, dy_attn, lse, delta, B, S, tm, grads_a)
    parts = dict(zip(late, parts_a))
    dq_t = dq_t.reshape(B, QK_W, S)

    def bwd_qkv(rows, bv, vecs, res):
        (dqt, dkv, dvv, pr, cs, s_lo, s_hi), (gq, gkv, rmask), (wq, wkv) = rows, vecs, res
        dqv = dqt[0].T
        dq_t = []
        dk_rope = jnp.zeros((dkv.shape[0], HEAD_PAD), F32)
        for h in range(N_HEADS):
            hs = slice(h * HEAD_PAD, (h + 1) * HEAD_PAD)
            dq_t.append(_rope_tile(dqv[:, hs], cs, s_lo, s_hi, -1.0))
            dk_rope = dk_rope + dkv[:, hs]
        dk_rope = _rope_tile(dk_rope * rmask, cs, s_lo, s_hi, -1.0)
        dqb = jnp.concatenate(dq_t, axis=1).astype(BF16)
        dkvb = jnp.concatenate([dkv, dvv], axis=1).astype(BF16)
        rq, qh = _rms_stats(pr[:, D_SSM:D_SSM + Q_LORA])
        rk, kvh = _rms_stats(pr[:, D_SSM + Q_LORA:D_SSM + Q_LORA + KV_LORA])
        dqn = _dot_nt(dqb, wq[...])
        dkvn = _dot_nt(dkvb, wkv[...])
        dpa = jnp.concatenate([_rms_bwd(dqn * gq, qh, rq), _rms_bwd(dkvn * gkv, kvh, rk), dk_rope], axis=1)
        return [dpa, t_val(qh * gq), t_val(kvh * gkv), dqb, dkvb], [], [_colsum(dqn * qh), _colsum(dkvn * kvh)]

    ((dpa, qnb, kvnb, dqb, dkvb), _, (d_gq, d_gkv)) = row_call(
        "bwd_qkv", bwd_qkv, B, S, tm,
        [(dq_t, (1, QK_W, tm), lambda b, s: (b, 0, s)), dk_r, dv_r, proj, rope_cs, rope_lo, rope_hi], [],
        [q_norm_g, kv_norm_g, rope_mask], [w_uq_p, w_ukv_p],
        [(Q_LORA + KV_LORA + HEAD_PAD, F32), t_out(Q_LORA), t_out(KV_LORA), (QK_W, BF16), (QK_W + V_W, BF16)],
        [], [(1, Q_LORA), (1, KV_LORA)])
    g_w_uq = _unpad_w_uq(matmul_nn("grad_w_uq", qnb, dqb))
    g_w_ukv = _unpad_w_ukv(matmul_nn("grad_w_ukv", kvnb, dkvb))

    du, dc_re_d, dc_im_d, db_re_d, db_im_d, da_re, da_im, d_d = ssm_bwd(
        dy_pre, proj, st_re, st_im, bbd_re, bbd_im, cbd_re, cbd_im, d_vec, a_re, a_im, B, S, tm)
    place = lambda a: jnp.zeros((N_GROUPS, GROUP, N_STATE), F32).at[:, 0, :].set(
        a.reshape(N_GROUPS, N_STATE)).reshape(D_SSM, N_STATE)
    cots = [place(da_re), place(da_im),
            _block_diag_take(db_re_d, GROUP, N_STATE).reshape(D_SSM, N_STATE),
            _block_diag_take(db_im_d, GROUP, N_STATE).reshape(D_SSM, N_STATE)]
    g_lam_re, g_lam_im, g_log_dt, g_bt_re, g_bt_im = ssm_params_bwd(reps, cots)
    unbt = lambda a: jnp.transpose(a.reshape(N_GROUPS, GROUP, N_STATE), (0, 2, 1))
    g_c_re = jnp.transpose(_block_diag_take(dc_re_d, N_STATE, GROUP), (0, 2, 1))
    g_c_im = jnp.transpose(_block_diag_take(dc_im_d, N_STATE, GROUP), (0, 2, 1))

    def bwd_in(rows, bv, vecs, res):
        (duv, dpav, xv, dx2v), (sc, sh), (g1,), (w,) = rows, bv, vecs, res
        dproj = jnp.concatenate([duv, dpav], axis=1).astype(BF16)
        dh = _dot_nt(dproj, w[...])
        r, xh = _rms_stats(xv)
        n1 = xh * g1
        dn1 = dh * (1.0 + sc)
        dx = dx2v + _rms_bwd(dn1 * g1, xh, r)
        return [dx, t_val(n1 * (1.0 + sc) + sh), dproj], [_colsum(dh), _colsum(dh * n1)], [_colsum(dn1 * xh)]

    small_part = {
        "ssm_lambda_re": g_lam_re,
        "ssm_lambda_im": g_lam_im, "ssm_b_re": unbt(g_bt_re), "ssm_b_im": unbt(g_bt_im), "ssm_c_re": g_c_re,
        "ssm_c_im": g_c_im, "ssm_d": d_d, "ssm_log_dt": g_log_dt, "q_norm_g": d_gq, "kv_norm_g": d_gkv,
        "ssm_out_g": d_gso, "attn_out_g": d_gao, "norm2_g": d_g2, "final_norm_g": d_gf}
    dense_bf16 = lambda n, g: g.reshape(_dense_2d(given[n].shape)).astype(BF16)
    ready = [n for n in SMALL if n in small_part]
    grads_b = Exchange([dense_bf16(n, small_part[n]) for n in ready] + [_cols_to_shards(g_w_uq), g_w_ukv],
                       [GATHER] * len(ready) + [None, w_ukv.shape[2]])
    ((grad_x, h1b, dprojb), (d_shift1, d_scale1), (d_g1,), side_b) = row_call(
        "bwd_in", bwd_in, B, S, tm, [du, dpa, x2d, dx2], [scale1, shift1], [norm1_g], [w_in_p],
        [(D, F32), t_out(D), (IN_PAD, BF16)], [D, D], [(1, D)], side=grads_b)
    small_parts = dict(zip(ready, side_b[:len(ready)]))
    parts["w_uq"], parts["w_ukv"] = side_b[len(ready):]
    g_w_in = _unpad_w_in(matmul_nn("grad_w_in", h1b, dprojb))

    dmod = jnp.concatenate([d_shift1, d_scale1, d_gate1, d_shift2, d_scale2, d_gate2, d_fshift, d_fscale],
                           axis=2).reshape(B, 8 * D)
    dmod_g, small_parts["norm1_g"], parts["w_in"] = exchange(
        "exchange_last_grads", [dmod.astype(BF16), dense_bf16("norm1_g", d_g1), _cols_to_shards(g_w_in)],
        [GATHER, GATHER, None])
    small_g = [small_parts[n] for n in SMALL]
    dmod_all = dmod_g.reshape(N_DEV * B, 8 * D)
    dm_sh = lax.dynamic_slice_in_dim(dmod_all[:, :6 * D], me * n_mod, n_mod, axis=1)
    dfm_sh = lax.dynamic_slice_in_dim(dmod_all[:, 6 * D:], me * n_fmod, n_fmod, axis=1)
    g_ada_w, g_fada_w, g_biases = modulation_bwd(cond_all, dm_sh, dfm_sh, dmod_all)
    parts["ada_w"] = g_ada_w[None]
    parts["final_ada_w"] = g_fada_w[None]

    res_g, res_d, res_m, res_v = {}, {}, {}, {}
    for n in ["ada_w"] + big + ["final_ada_w"]:
        w_full = given[n]
        w2 = w_full.reshape(w_full.shape[-2], w_full.shape[-1])
        out = adamw("adamw_" + n, parts[n], w2, given["m_" + n].reshape(w2.shape), given["v_" + n].reshape(w2.shape))
        res_g[n], res_d[n], res_m[n], res_v[n] = [a.reshape(w_full.shape) for a in out]
    bias_g = {"ada_b": g_biases[None, :, :6 * D], "final_ada_b": g_biases[None, :, 6 * D:]}
    names = SMALL + BIASES
    items = []
    for n, g_parts in zip(names, small_g + [bias_g[b] for b in BIASES]):
        dense = _dense_2d(given[n].shape)
        items.append((g_parts, given[n].reshape(dense), given["m_" + n].reshape(dense), given["v_" + n].reshape(dense)))
    for n, out in zip(names, adamw_many("adamw_small", items)):
        res_g[n], res_d[n], res_m[n], res_v[n] = [a.reshape(given[n].shape) for a in out]

    order = ["ada_w", "ada_b", "norm1_g", "w_in", "ssm_lambda_re", "ssm_lambda_im", "ssm_b_re", "ssm_b_im",
             "ssm_c_re", "ssm_c_im", "ssm_d", "ssm_log_dt", "w_glu", "q_norm_g", "w_uq", "kv_norm_g", "w_ukv",
             "ssm_out_g", "attn_out_g", "w_out", "norm2_g", "w_ff1", "w_ff2", "final_ada_w", "final_ada_b",
             "final_norm_g"]
    return (loss, grad_x.reshape(B, S, D), *[res_g[n] for n in order], *[res_d[n] for n in order],
            *[res_m[n] for n in order], *[res_v[n] for n in order])
```

```python
import functools
import math

import numpy as np
import jax
import jax.numpy as jnp
from jax import lax
from jax.experimental import pallas as pl
from jax.experimental.pallas import tpu as pltpu

F32 = jnp.float32
BF16 = jnp.bfloat16

N_DEV = 8
EPS = 1e-6
D_SSM = 512
N_GROUPS = 32
GROUP = 16
N_STATE = 64
N_ST = N_GROUPS * N_STATE
Q_LORA = 384
KV_LORA = 256
QK_NOPE = 64
QK_ROPE = 32
V_HEAD = 64
N_HEADS = 8
HEAD_PAD = 128
QK_W = N_HEADS * HEAD_PAD
V_W = N_HEADS * V_HEAD
IN_COLS = D_SSM + Q_LORA + KV_LORA + QK_ROPE
IN_PAD = D_SSM + Q_LORA + KV_LORA + HEAD_PAD
ROPE_BASE = 10000.0
ATTN_SCALE = (QK_NOPE + QK_ROPE) ** -0.5
NEG = -1e30

ADAM_LR = 0.001
ADAM_B1 = 0.9
ADAM_B2 = 0.999
ADAM_EPS = 1e-08
ADAM_WD = 0.01
ADAM_STEP = 10

VMEM_LIMIT = 56 * 1024 * 1024
MESH_ID = pl.DeviceIdType.MESH


def _dot(a, b):
    return jnp.dot(a, b, preferred_element_type=F32)


def _dot_nt(a, b):
    return lax.dot_general(a, b, (((1,), (1,)), ((), ())), preferred_element_type=F32)


def _dot_tn(a, b):
    return lax.dot_general(a, b, (((0,), (0,)), ((), ())), preferred_element_type=F32)


def _rms_stats(x):
    r = lax.rsqrt(jnp.mean(x * x, axis=-1, keepdims=True) + EPS)
    return r, x * r


def _rms_bwd(dy, xh, r):
    return r * (dy - xh * jnp.mean(dy * xh, axis=-1, keepdims=True))


def _sigmoid(x):
    return 1.0 / (1.0 + jnp.exp(-x))


_GELU_C = math.sqrt(2.0 / math.pi)


def _gelu(x):
    t = jnp.tanh(_GELU_C * (x + 0.044715 * x * x * x))
    return 0.5 * x * (1.0 + t)


def _gelu_grad(x):
    t = jnp.tanh(_GELU_C * (x + 0.044715 * x * x * x))
    return 0.5 * (1.0 + t) + 0.5 * x * (1.0 - t * t) * _GELU_C * (1.0 + 3.0 * 0.044715 * x * x)


def _colsum(a):
    return jnp.sum(a, axis=0, keepdims=True)


def _params(sem=None):
    return pltpu.CompilerParams(dimension_semantics=sem, vmem_limit_bytes=VMEM_LIMIT)


def _mesh_pos():
    x, y, c = lax.axis_index("x"), lax.axis_index("y"), lax.axis_index("c")
    return x, y, c, 4 * x + 2 * y + c


def _peer(x, y, c, k):
    px = 1 - x if k & 4 else x
    py = 1 - y if k & 2 else y
    pc = 1 - c if k & 1 else c
    return (px, py, pc), 4 * px + 2 * py + pc


GATHER = "gather"
ANY_SPEC = pl.BlockSpec(memory_space=pl.ANY)


class Exchange:
    def __init__(self, arrays, kinds):
        self.arrays, self.kinds, self.n = list(arrays), list(kinds), len(arrays)

    def _shard(self, i):
        a, kind = self.arrays[i], self.kinds[i]
        if kind == GATHER:
            return a.shape
        return a.shape[1:] if kind is None else (a.shape[0], kind)

    def out_shapes(self):
        return [jax.ShapeDtypeStruct((N_DEV,) + tuple(self._shard(i)), self.arrays[i].dtype) for i in range(self.n)]

    def scratch(self):
        return [pltpu.SemaphoreType.DMA((self.n, N_DEV - 1)), pltpu.SemaphoreType.DMA((self.n, N_DEV - 1)),
                pltpu.SemaphoreType.DMA((self.n,))]

    def _src(self, ins, i, j):
        kind = self.kinds[i]
        if kind == GATHER:
            return ins[i]
        if kind is None:
            return ins[i].at[j]
        return ins[i].at[:, pl.ds(pl.multiple_of(j * kind, 128), kind)]

    def _copies(self, ins, outs, sems, with_received):
        send_sems, recv_sems, loc_sems = sems
        x, y, c, me = _mesh_pos()
        local, sent, received = [], [], []
        for i in range(self.n):
            local.append(pltpu.make_async_copy(self._src(ins, i, me), outs[i].at[me], loc_sems.at[i]))
            for k in range(1, N_DEV):
                peer, pidx = _peer(x, y, c, k)
                pair = dict(send_sem=send_sems.at[i, k - 1], recv_sem=recv_sems.at[i, k - 1], device_id=peer,
                            device_id_type=MESH_ID)
                sent.append(pltpu.make_async_remote_copy(
                    src_ref=self._src(ins, i, pidx), dst_ref=outs[i].at[me], **pair))
                if with_received:
                    received.append(pltpu.make_async_remote_copy(
                        src_ref=self._src(ins, i, pidx), dst_ref=outs[i].at[pidx], **pair))
        return local, sent, received

    def start(self, ins, outs, sems):
        local, sent, _ = self._copies(ins, outs, sems, False)
        for cp in local + sent:
            cp.start()

    def wait(self, ins, outs, sems):
        local, sent, received = self._copies(ins, outs, sems, True)
        for cp in received:
            cp.wait_recv()
        for cp in sent:
            cp.wait_send()
        for cp in local:
            cp.wait()


def exchange(name, arrays, kinds):
    ex = Exchange(arrays, kinds)
    n = ex.n

    def body(*refs):
        ins, outs, sems = refs[:n], refs[n:2 * n], refs[2 * n:]
        ex.start(ins, outs, sems)
        ex.wait(ins, outs, sems)

    return pl.pallas_call(
        body, name=name, out_shape=ex.out_shapes(), in_specs=[ANY_SPEC] * n, out_specs=[ANY_SPEC] * n,
        scratch_shapes=ex.scratch())(*arrays)


def all_gather(name, arrays):
    return exchange(name, arrays, [GATHER] * len(arrays))


def row_call(name, body, B, S, tm, rows, bvecs, vecs, res, row_outs, bacc, gacc, side=None):
    ns = S // tm
    T = B * S
    n_r, n_b, n_v, n_w = len(rows), len(bvecs), len(vecs), len(res)
    n_ro, n_ba, n_ga = len(row_outs), len(bacc), len(gacc)
    n_x = side.n if side is not None else 0

    def kern(*refs):
        i = 0
        r_refs = refs[i:i + n_r]; i += n_r
        b_refs = refs[i:i + n_b]; i += n_b
        v_refs = refs[i:i + n_v]; i += n_v
        w_hbm = refs[i:i + n_w]; i += n_w
        x_in = refs[i:i + n_x]; i += n_x
        ro_refs = refs[i:i + n_ro]; i += n_ro
        ba_refs = refs[i:i + n_ba]; i += n_ba
        ga_refs = refs[i:i + n_ga]; i += n_ga
        x_out = refs[i:i + n_x]; i += n_x
        w_vmem = refs[i:i + n_w]; i += n_w
        x_sems = refs[i:]
        b = pl.program_id(0)
        s = pl.program_id(1)
        first = jnp.logical_and(b == 0, s == 0)

        if n_x:
            @pl.when(first)
            def _start_side():
                side.start(x_in, x_out, x_sems)

        if n_w:
            @pl.when(first)
            def _load_weights():
                for h, v in zip(w_hbm, w_vmem):
                    pltpu.sync_copy(h, v)

        ro, ba, ga = body([r[...] for r in r_refs], [r[0] for r in b_refs], [r[...] for r in v_refs],
                          list(w_vmem))
        for ref, val in zip(ro_refs, ro):
            ref[...] = val.astype(ref.dtype)

        def accumulate(ref, val, is_first, idx):
            @pl.when(is_first)
            def _set():
                ref[idx] = val

            @pl.when(jnp.logical_not(is_first))
            def _add():
                ref[idx] = ref[idx] + val

        for ref, val in zip(ba_refs, ba):
            accumulate(ref, val, s == 0, 0)
        for ref, val in zip(ga_refs, ga):
            accumulate(ref, val, first, Ellipsis)

        if n_x:
            @pl.when(jnp.logical_and(b == B - 1, s == ns - 1))
            def _wait_side():
                side.wait(x_in, x_out, x_sems)

    row_arrays = [r[0] if isinstance(r, tuple) else r for r in rows]
    row_in_specs = [pl.BlockSpec(r[1], r[2]) if isinstance(r, tuple)
                    else pl.BlockSpec((tm, r.shape[1]), lambda b, s: (b * ns + s, 0)) for r in rows]
    ro_shapes = [jax.ShapeDtypeStruct(tuple(o[0]), o[1]) if len(o) == 4 else jax.ShapeDtypeStruct((T, o[0]), o[1])
                 for o in row_outs]
    ro_specs = [pl.BlockSpec(o[2], o[3]) if len(o) == 4 else pl.BlockSpec((tm, o[0]), lambda b, s: (b * ns + s, 0))
                for o in row_outs]
    in_specs = (row_in_specs
                + [pl.BlockSpec((1, 1, v.shape[2]), lambda b, s: (b, 0, 0)) for v in bvecs]
                + [pl.BlockSpec(v.shape, lambda b, s, nd=v.ndim: (0,) * nd) for v in vecs]
                + [ANY_SPEC] * (n_w + n_x))
    out_shape = (ro_shapes
                 + [jax.ShapeDtypeStruct((B, 1, w), F32) for w in bacc]
                 + [jax.ShapeDtypeStruct(tuple(sh), F32) for sh in gacc]
                 + (side.out_shapes() if n_x else []))
    out_specs = (ro_specs
                 + [pl.BlockSpec((1, 1, w), lambda b, s: (b, 0, 0)) for w in bacc]
                 + [pl.BlockSpec(tuple(sh), lambda b, s, nd=len(sh): (0,) * nd) for sh in gacc]
                 + [ANY_SPEC] * n_x)
    outs = pl.pallas_call(
        kern, name=name, grid=(B, ns), in_specs=in_specs, out_specs=out_specs, out_shape=out_shape,
        scratch_shapes=[pltpu.VMEM(w.shape, w.dtype) for w in res] + (side.scratch() if n_x else []),
        compiler_params=_params(("arbitrary", "arbitrary")),
    )(*row_arrays, *bvecs, *vecs, *res, *(side.arrays if n_x else []))
    n_acc = n_ro + n_ba + n_ga
    if n_x:
        return outs[:n_ro], outs[n_ro:n_ro + n_ba], outs[n_ro + n_ba:n_acc], outs[n_acc:]
    return outs[:n_ro], outs[n_ro:n_ro + n_ba], outs[n_ro + n_ba:]


def matmul_nn(name, at, b, side=None):
    K, T = at.shape
    N = b.shape[1]
    tk = min(K, 512)
    tn = N if N <= 1280 else (1024 if N % 1024 == 0 else 512)
    assert K % tk == 0 and N % tn == 0 and T % min(T, 2048) == 0, (K, N, T)
    tt = min(T, 2048)
    nt = T // tt
    n_x = side.n if side is not None else 0
    grid = (K // tk, N // tn, nt)

    def kern(*refs):
        a_ref, b_ref = refs[:2]
        x_in = refs[2:2 + n_x]
        o_ref = refs[2 + n_x]
        x_out = refs[3 + n_x:3 + 2 * n_x]
        acc_ref = refs[3 + 2 * n_x]
        x_sems = refs[4 + 2 * n_x:]
        i, j, t = pl.program_id(0), pl.program_id(1), pl.program_id(2)
        if n_x:
            @pl.when(jnp.logical_and(i == 0, jnp.logical_and(j == 0, t == 0)))
            def _start_side():
                side.start(x_in, x_out, x_sems)

        part = _dot(a_ref[...], b_ref[...])

        @pl.when(t == 0)
        def _set():
            acc_ref[...] = part

        @pl.when(t > 0)
        def _add():
            acc_ref[...] = acc_ref[...] + part

        @pl.when(t == nt - 1)
        def _write():
            o_ref[...] = acc_ref[...].astype(o_ref.dtype)

        if n_x:
            @pl.when(jnp.logical_and(i == grid[0] - 1, jnp.logical_and(j == grid[1] - 1, t == nt - 1)))
            def _wait_side():
                side.wait(x_in, x_out, x_sems)

    outs = pl.pallas_call(
        kern, name=name, grid=grid,
        in_specs=[pl.BlockSpec((tk, tt), lambda i, j, t: (i, t)), pl.BlockSpec((tt, tn), lambda i, j, t: (t, j))]
        + [ANY_SPEC] * n_x,
        out_specs=[pl.BlockSpec((tk, tn), lambda i, j, t: (i, j))] + [ANY_SPEC] * n_x,
        out_shape=[jax.ShapeDtypeStruct((K, N), BF16)] + (side.out_shapes() if n_x else []),
        scratch_shapes=[pltpu.VMEM((tk, tn), F32)] + (side.scratch() if n_x else []),
        compiler_params=_params(("arbitrary",) * 3 if n_x else ("parallel", "parallel", "arbitrary")),
    )(at, b, *(side.arrays if n_x else []))
    return outs if n_x else outs[0]


def _rope_consts():
    inv_freq = ROPE_BASE ** (-jnp.arange(0, QK_ROPE, 2, dtype=F32) / QK_ROPE)
    zeros64 = jnp.zeros((64,), F32)
    zeros32 = jnp.zeros((32,), F32)
    zeros16 = jnp.zeros((16,), F32)
    ones16 = jnp.ones((16,), F32)
    invf = jnp.concatenate([zeros64, inv_freq, inv_freq, zeros32])[None, :]
    m_lo = jnp.concatenate([zeros64, ones16, zeros16, zeros32])[None, :]
    m_hi = jnp.concatenate([zeros64, zeros16, ones16, zeros32])[None, :]
    return invf, m_lo, m_hi


def _rope_tables(pos, invf, m_lo, m_hi):
    ang = pos * invf
    return jnp.cos(ang), jnp.sin(ang) * m_lo, jnp.sin(ang) * m_hi


def _rope_tile(t, cs, s_lo, s_hi, sign):
    up = pltpu.roll(t, HEAD_PAD - 16, axis=1)
    down = pltpu.roll(t, 16, axis=1)
    return t * cs + sign * (down * s_hi - up * s_lo)


def _heads(ref, width):
    if len(ref.shape) == 2:
        return jnp.stack([ref[:, h * width:(h + 1) * width] for h in range(N_HEADS)], axis=0)
    return jnp.stack([ref[b, :, h * width:(h + 1) * width] for b in range(ref.shape[0]) for h in range(N_HEADS)],
                     axis=0)


def _bmm(a, b, ca, cb):
    return lax.dot_general(a, b, (((ca,), (cb,)), ((0,), (0,))), preferred_element_type=F32)


LOG2E = math.log2(math.e)
EXP2_SCALE = ATTN_SCALE * LOG2E


def _scores_t(k_ref, q_ref, masked, tq):
    st = _bmm(_heads(k_ref, HEAD_PAD), _heads(q_ref, HEAD_PAD), 2, 2)
    if masked:
        key = lax.broadcasted_iota(jnp.int32, (tq, tq), 0)
        qry = lax.broadcasted_iota(jnp.int32, (tq, tq), 1)
        st = jnp.where((key <= qry)[None], st, NEG)
    return st


def attention_fwd(q, k, vt, B, S, tq, side):
    nq = S // tq
    ns = side.n
    nbh = B * N_HEADS
    pairs = [(i, j) for i in range(nq) for j in range(i + 1)]
    n_pairs = len(pairs)
    q_tab = jnp.asarray([p[0] for p in pairs], jnp.int32)
    k_tab = jnp.asarray([p[1] for p in pairs], jnp.int32)

    def kern(q_tab_ref, k_tab_ref, *refs):
        q_ref, k_ref, vt_ref = refs[:3]
        side_in = refs[3:3 + ns]
        o_ref, lse_ref = refs[3 + ns:5 + ns]
        side_out = refs[5 + ns:5 + 2 * ns]
        m_scr, l_scr, acc_scr = refs[5 + 2 * ns:8 + 2 * ns]
        sems = refs[8 + 2 * ns:]
        t = pl.program_id(0)
        qi, ki = q_tab_ref[t], k_tab_ref[t]

        @pl.when(t == 0)
        def _start_side():
            side.start(side_in, side_out, sems)

        @pl.when(ki == 0)
        def _init():
            m_scr[...] = jnp.full(m_scr.shape, NEG, F32)
            l_scr[...] = jnp.zeros(l_scr.shape, F32)
            acc_scr[...] = jnp.zeros(acc_scr.shape, F32)

        def block(masked):
            st = _scores_t(k_ref, q_ref, masked, tq)
            m_prev = m_scr[...]
            m_new = jnp.maximum(m_prev, jnp.max(st, axis=1, keepdims=True))
            alpha = jnp.exp2((m_prev - m_new) * EXP2_SCALE)
            p = jnp.exp2((st - m_new) * EXP2_SCALE)
            l_scr[...] = alpha * l_scr[...] + jnp.sum(p, axis=1, keepdims=True)
            vt3 = vt_ref[...].reshape(nbh, V_HEAD, tq)
            acc_scr[...] = alpha * acc_scr[...] + _bmm(vt3, p.astype(BF16), 2, 1)
            m_scr[...] = m_new

        @pl.when(ki < qi)
        def _full():
            block(False)

        @pl.when(ki == qi)
        def _diagonal():
            block(True)
            ot = acc_scr[...] / l_scr[...]
            for b in range(B):
                for h in range(N_HEADS):
                    o_ref[b, :, h * V_HEAD:(h + 1) * V_HEAD] = ot[b * N_HEADS + h].T
            lse_ref[...] = m_scr[...] * ATTN_SCALE + jnp.log(l_scr[...])

        @pl.when(t == n_pairs - 1)
        def _wait_side():
            side.wait(side_in, side_out, sems)

    grid_spec = pltpu.PrefetchScalarGridSpec(
        num_scalar_prefetch=2, grid=(n_pairs,),
        in_specs=[pl.BlockSpec((B, tq, QK_W), lambda t, qt, kt: (0, qt[t], 0)),
                  pl.BlockSpec((B, tq, QK_W), lambda t, qt, kt: (0, kt[t], 0)),
                  pl.BlockSpec((B, V_W, tq), lambda t, qt, kt: (0, 0, kt[t]))] + [ANY_SPEC] * ns,
        out_specs=[pl.BlockSpec((B, tq, V_W), lambda t, qt, kt: (0, qt[t], 0)),
                   pl.BlockSpec((nbh, 1, tq), lambda t, qt, kt: (0, 0, qt[t]))] + [ANY_SPEC] * ns,
        scratch_shapes=[pltpu.VMEM((nbh, 1, tq), F32), pltpu.VMEM((nbh, 1, tq), F32),
                        pltpu.VMEM((nbh, V_HEAD, tq), F32)] + side.scratch())
    return pl.pallas_call(
        kern, name="attention_fwd", grid_spec=grid_spec,
        out_shape=[jax.ShapeDtypeStruct((B, S, V_W), F32), jax.ShapeDtypeStruct((nbh, 1, S), F32)]
        + side.out_shapes(),
        compiler_params=_params(("arbitrary",)),
    )(q_tab, k_tab, q, k, vt, *side.arrays)


def attention_bwd(q, k, kt, v, do, lse, delta, B, S, tq, side):
    nq = S // tq
    ns = side.n
    pairs = [(j, i) for j in range(nq) for i in range(j, nq)]
    n_pairs = len(pairs)
    k_tab = jnp.asarray([p[0] for p in pairs], jnp.int32)
    q_tab = jnp.asarray([p[1] for p in pairs], jnp.int32)

    def kern(k_tab_ref, q_tab_ref, *refs):
        q_ref, k_ref, kt_ref, v_ref, do_ref, lse_ref, delta_ref = refs[:7]
        side_in = refs[7:7 + ns]
        dq_hbm, dk_ref, dv_ref = refs[7 + ns:10 + ns]
        side_out = refs[10 + ns:10 + 2 * ns]
        dq_acc, dk_acc, dv_acc = refs[10 + 2 * ns:13 + 2 * ns]
        sems = refs[13 + 2 * ns:]
        b, t = pl.program_id(0), pl.program_id(1)
        kj, qi = k_tab_ref[t], q_tab_ref[t]

        @pl.when(jnp.logical_and(b == 0, t == 0))
        def _start_side():
            side.start(side_in, side_out, sems)

        @pl.when(t == 0)
        def _zero_dq():
            dq_acc[...] = jnp.zeros(dq_acc.shape, F32)

        @pl.when(qi == kj)
        def _zero_dkv():
            dk_acc[...] = jnp.zeros(dk_acc.shape, F32)
            dv_acc[...] = jnp.zeros(dv_acc.shape, F32)

        def block(masked):
            pt = jnp.exp2(_scores_t(k_ref, q_ref, masked, tq) * EXP2_SCALE - lse_ref[...] * LOG2E)
            do3 = _heads(do_ref, V_HEAD).astype(BF16)
            dv_acc[...] = dv_acc[...] + _bmm(pt.astype(BF16), do3, 2, 1)
            dpt = _bmm(_heads(v_ref, V_HEAD), do3, 2, 2)
            dst = (pt * (dpt - delta_ref[...]) * ATTN_SCALE).astype(BF16)
            dk_acc[...] = dk_acc[...] + _bmm(dst, _heads(q_ref, HEAD_PAD), 2, 1)
            q_cols = pl.ds(pl.multiple_of(qi * tq, tq), tq)
            kt3 = kt_ref[...].reshape(N_HEADS, HEAD_PAD, tq)
            dq_acc[:, :, q_cols] = dq_acc[:, :, q_cols] + _bmm(kt3, dst, 2, 1)

        @pl.when(qi > kj)
        def _full():
            block(False)

        @pl.when(qi == kj)
        def _diagonal():
            block(True)

        @pl.when(qi == nq - 1)
        def _write_dkv():
            for h in range(N_HEADS):
                dk_ref[:, h * HEAD_PAD:(h + 1) * HEAD_PAD] = dk_acc[h]
                dv_ref[:, h * V_HEAD:(h + 1) * V_HEAD] = dv_acc[h]

        @pl.when(t == n_pairs - 1)
        def _write_dq():
            pltpu.sync_copy(dq_acc, dq_hbm.at[b])

        @pl.when(jnp.logical_and(b == B - 1, t == n_pairs - 1))
        def _wait_side():
            side.wait(side_in, side_out, sems)

    T = B * S
    qrow = lambda b, t, kt, qt: (b * nq + qt[t], 0)
    qcol3 = lambda b, t, kt, qt: (0, 0, b * nq + qt[t])
    krow = lambda b, t, kt, qt: (b * nq + kt[t], 0)
    grid_spec = pltpu.PrefetchScalarGridSpec(
        num_scalar_prefetch=2, grid=(B, n_pairs),
        in_specs=[pl.BlockSpec((tq, QK_W), qrow), pl.BlockSpec((tq, QK_W), krow),
                  pl.BlockSpec((QK_W, tq), lambda b, t, kt, qt: (0, b * nq + kt[t])), pl.BlockSpec((tq, V_W), krow),
                  pl.BlockSpec((tq, V_W), qrow), pl.BlockSpec((N_HEADS, 1, tq), qcol3),
                  pl.BlockSpec((N_HEADS, 1, tq), qcol3)] + [ANY_SPEC] * ns,
        out_specs=[ANY_SPEC, pl.BlockSpec((tq, QK_W), krow), pl.BlockSpec((tq, V_W), krow)] + [ANY_SPEC] * ns,
        scratch_shapes=[pltpu.VMEM((N_HEADS, HEAD_PAD, S), F32), pltpu.VMEM((N_HEADS, tq, HEAD_PAD), F32),
                        pltpu.VMEM((N_HEADS, tq, V_HEAD), F32)] + side.scratch())
    return pl.pallas_call(
        kern, name="attention_bwd", grid_spec=grid_spec,
        out_shape=[jax.ShapeDtypeStruct((B, N_HEADS, HEAD_PAD, S), F32), jax.ShapeDtypeStruct((T, QK_W), F32),
                   jax.ShapeDtypeStruct((T, V_W), F32)] + side.out_shapes(),
        compiler_params=_params(("arbitrary", "arbitrary")),
    )(k_tab, q_tab, q, k, kt, v, do, lse, delta, *side.arrays)


GB = 8
N_GB = N_GROUPS // GB
GB_U = GB * GROUP
GB_ST = GB * N_STATE


def _gb_u(j):
    return slice(j * GB_U, (j + 1) * GB_U)


def _gb_s(j):
    return slice(j * GB_ST, (j + 1) * GB_ST)


def ssm_param_fn(lr, li, ld, br, bi):
    dt = jnp.exp(ld)
    er = jnp.exp(lr * dt)
    ar = er * jnp.cos(li * dt)
    ai = er * jnp.sin(li * dt)
    nr = ar - 1.0
    den = lr * lr + li * li
    cr = (nr * lr + ai * li) / den
    ci = (ai * lr - nr * li) / den
    return ar, ai, cr * br - ci * bi, cr * bi + ci * br


def ssm_params_fwd(reps):
    def kern(lr, li, ld, br, bi, ar_o, ai_o, bbr_o, bbi_o):
        ar, ai, bbr, bbi = ssm_param_fn(lr[...], li[...], ld[...], br[...], bi[...])
        ar_o[...] = ar
        ai_o[...] = ai
        bbr_o[...] = bbr
        bbi_o[...] = bbi

    shp = jax.ShapeDtypeStruct(reps[0].shape, F32)
    return pl.pallas_call(kern, name="ssm_params_fwd", out_shape=[shp] * 4)(*reps)


def ssm_params_bwd(reps, cots):
    rows = reps[0].shape[0]

    def kern(lr, li, ld, br, bi, d_ar, d_ai, d_bbr, d_bbi, dlr_o, dli_o, dld_o, dbr_o, dbi_o):
        _, vjp = jax.vjp(ssm_param_fn, lr[...], li[...], ld[...], br[...], bi[...])
        dlr, dli, dld, dbr, dbi = vjp((d_ar[...], d_ai[...], d_bbr[...], d_bbi[...]))
        dlr_o[...] = jnp.sum(dlr.reshape(N_GROUPS, GROUP, N_STATE), axis=1)
        dli_o[...] = jnp.sum(dli.reshape(N_GROUPS, GROUP, N_STATE), axis=1)
        dld_o[...] = jnp.sum(jnp.sum(dld.reshape(N_GROUPS, GROUP, N_STATE), axis=1), axis=-1, keepdims=True)
        dbr_o[...] = dbr
        dbi_o[...] = dbi

    g = jax.ShapeDtypeStruct((N_GROUPS, N_STATE), F32)
    full = jax.ShapeDtypeStruct((rows, N_STATE), F32)
    return pl.pallas_call(
        kern, name="ssm_params_bwd",
        out_shape=[g, g, jax.ShapeDtypeStruct((N_GROUPS, 1), F32), full, full])(*reps, *cots)


def ssm_fwd(proj, b_re, b_im, c_re, c_im, d_vec, a_re, a_im, B, S, ts):
    ns = S // ts
    T = B * S

    def kern(u_ref, bre_ref, bim_ref, cre_ref, cim_ref, d_ref, are_ref, aim_ref,
             sre_ref, sim_ref, y_ref, car_re, car_im, bu_re, bu_im):
        s = pl.program_id(1)

        @pl.when(s == 0)
        def _reset():
            car_re[...] = jnp.zeros(car_re.shape, F32)
            car_im[...] = jnp.zeros(car_im.shape, F32)

        u = u_ref[...]
        ub = u.astype(BF16)
        for j in range(N_GB):
            uj, bj, sj = ub[:, _gb_u(j)], _gb_u(j), _gb_s(j)
            bu_re[:, sj] = _dot(uj, bre_ref[bj, :])
            bu_im[:, sj] = _dot(uj, bim_ref[bj, :])
        ar, ai = are_ref[...], aim_ref[...]

        def step(t, carry):
            xr, xi = carry
            row = pl.ds(t, 1)
            nr = ar * xr - ai * xi + bu_re[row, :]
            ni = ar * xi + ai * xr + bu_im[row, :]
            sre_ref[row, :] = nr
            sim_ref[row, :] = ni
            return nr, ni

        xr, xi = lax.fori_loop(0, ts, step, (car_re[0:1, :], car_im[0:1, :]))
        car_re[0:1, :] = xr
        car_im[0:1, :] = xi
        y = [_dot(sre_ref[:, _gb_s(j)].astype(BF16), cre_ref[_gb_s(j), :])
             - _dot(sim_ref[:, _gb_s(j)].astype(BF16), cim_ref[_gb_s(j), :]) for j in range(N_GB)]
        y_ref[...] = jnp.concatenate(y, axis=1) + d_ref[...] * u

    row = lambda b, s: (b * ns + s, 0)
    whole = lambda b, s: (0, 0)
    return pl.pallas_call(
        kern, name="ssm_fwd", grid=(B, ns),
        in_specs=[pl.BlockSpec((ts, D_SSM), row),
                  pl.BlockSpec((D_SSM, GB_ST), whole), pl.BlockSpec((D_SSM, GB_ST), whole),
                  pl.BlockSpec((N_ST, GB_U), whole), pl.BlockSpec((N_ST, GB_U), whole),
                  pl.BlockSpec((1, D_SSM), whole), pl.BlockSpec((1, N_ST), whole), pl.BlockSpec((1, N_ST), whole)],
        out_specs=[pl.BlockSpec((ts, N_ST), row), pl.BlockSpec((ts, N_ST), row), pl.BlockSpec((ts, D_SSM), row)],
        out_shape=[jax.ShapeDtypeStruct((T, N_ST), F32), jax.ShapeDtypeStruct((T, N_ST), F32),
                   jax.ShapeDtypeStruct((T, D_SSM), F32)],
        scratch_shapes=[pltpu.VMEM((8, N_ST), F32), pltpu.VMEM((8, N_ST), F32),
                        pltpu.VMEM((ts, N_ST), F32), pltpu.VMEM((ts, N_ST), F32)],
        compiler_params=_params(("arbitrary", "arbitrary")),
    )(proj, b_re, b_im, c_re, c_im, d_vec, a_re, a_im)


def ssm_bwd(dy, proj, s_re, s_im, b_re, b_im, c_re, c_im, d_vec, a_re, a_im, B, S, ts):
    ns = S // ts
    T = B * S
    t8 = ts // 8

    def kern(dy_ref, u_ref, sre_ref, sim_ref, pre_ref, pim_ref, bre_ref, bim_ref, cre_ref, cim_ref,
             d_ref, are_ref, aim_ref,
             du_ref, dcre_ref, dcim_ref, dbre_ref, dbim_ref, dare_ref, daim_ref, dd_ref,
             car_re, car_im, g_re, g_im):
        b, s = pl.program_id(0), pl.program_id(1)
        first = jnp.logical_and(b == 0, s == 0)

        @pl.when(s == 0)
        def _reset():
            car_re[...] = jnp.zeros(car_re.shape, F32)
            car_im[...] = jnp.zeros(car_im.shape, F32)

        dyv = dy_ref[...]
        dyb = dyv.astype(BF16)
        u = u_ref[...]
        for j in range(N_GB):
            g_re[:, _gb_s(j)] = _dot_nt(dyb[:, _gb_u(j)], cre_ref[_gb_s(j), :])
            g_im[:, _gb_s(j)] = -_dot_nt(dyb[:, _gb_u(j)], cim_ref[_gb_s(j), :])
        ar, ai = are_ref[...], aim_ref[...]

        def step(i, carry):
            gr_next, gi_next = carry
            row = pl.ds(ts - 1 - i, 1)
            gr = g_re[row, :] + ar * gr_next + ai * gi_next
            gi = g_im[row, :] + ar * gi_next - ai * gr_next
            g_re[row, :] = gr
            g_im[row, :] = gi
            return gr, gi

        gr0, gi0 = lax.fori_loop(0, ts, step, (car_re[0:1, :], car_im[0:1, :]))
        car_re[0:1, :] = gr0
        car_im[0:1, :] = gi0

        gr, gi = g_re[...], g_im[...]
        sr, si = sre_ref[...], sim_ref[...]
        has_prev = (s < ns - 1).astype(F32)
        row_id = lax.broadcasted_iota(jnp.int32, (ts, N_ST), 0)
        spr = jnp.where(row_id == 0, pre_ref[7:8, :] * has_prev, pltpu.roll(sr, 1, axis=0))
        spi = jnp.where(row_id == 0, pim_ref[7:8, :] * has_prev, pltpu.roll(si, 1, axis=0))
        grb, gib, ub = gr.astype(BF16), gi.astype(BF16), u.astype(BF16)
        srb, sib = sr.astype(BF16), si.astype(BF16)
        cat0 = lambda f: jnp.concatenate([f(j) for j in range(N_GB)], axis=0)
        cat1 = lambda f: jnp.concatenate([f(j) for j in range(N_GB)], axis=1)
        parts = [
            (dcre_ref, cat0(lambda j: _dot_tn(srb[:, _gb_s(j)], dyb[:, _gb_u(j)]))),
            (dcim_ref, cat0(lambda j: -_dot_tn(sib[:, _gb_s(j)], dyb[:, _gb_u(j)]))),
            (dbre_ref, cat0(lambda j: _dot_tn(ub[:, _gb_u(j)], grb[:, _gb_s(j)]))),
            (dbim_ref, cat0(lambda j: _dot_tn(ub[:, _gb_u(j)], gib[:, _gb_s(j)]))),
            (dare_ref, _colsum(gr * spr + gi * spi)),
            (daim_ref, _colsum(gi * spr - gr * spi)),
            (dd_ref, _colsum(dyv * u)),
        ]
        du = cat1(lambda j: _dot_nt(grb[:, _gb_s(j)], bre_ref[_gb_u(j), :])
                  + _dot_nt(gib[:, _gb_s(j)], bim_ref[_gb_u(j), :]))
        du_ref[...] = du + d_ref[...] * dyv

        @pl.when(first)
        def _set():
            for ref, val in parts:
                ref[...] = val

        @pl.when(jnp.logical_not(first))
        def _add():
            for ref, val in parts:
                ref[...] = ref[...] + val

    row = lambda b, s: (b * ns + (ns - 1 - s), 0)
    prev = lambda b, s: (jnp.maximum((b * ns + (ns - 1 - s)) * t8 - 1, 0), 0)
    whole = lambda b, s: (0, 0)
    acc_shapes = [(N_ST, GB_U), (N_ST, GB_U), (D_SSM, GB_ST), (D_SSM, GB_ST), (1, N_ST), (1, N_ST), (1, D_SSM)]
    return pl.pallas_call(
        kern, name="ssm_bwd", grid=(B, ns),
        in_specs=[pl.BlockSpec((ts, D_SSM), row), pl.BlockSpec((ts, D_SSM), row),
                  pl.BlockSpec((ts, N_ST), row), pl.BlockSpec((ts, N_ST), row),
                  pl.BlockSpec((8, N_ST), prev), pl.BlockSpec((8, N_ST), prev),
                  pl.BlockSpec((D_SSM, GB_ST), whole), pl.BlockSpec((D_SSM, GB_ST), whole),
                  pl.BlockSpec((N_ST, GB_U), whole), pl.BlockSpec((N_ST, GB_U), whole),
                  pl.BlockSpec((1, D_SSM), whole), pl.BlockSpec((1, N_ST), whole), pl.BlockSpec((1, N_ST), whole)],
        out_specs=[pl.BlockSpec((ts, D_SSM), row)] + [pl.BlockSpec(sh, whole) for sh in acc_shapes],
        out_shape=[jax.ShapeDtypeStruct((T, D_SSM), F32)] + [jax.ShapeDtypeStruct(sh, F32) for sh in acc_shapes],
        scratch_shapes=[pltpu.VMEM((8, N_ST), F32), pltpu.VMEM((8, N_ST), F32),
                        pltpu.VMEM((ts, N_ST), F32), pltpu.VMEM((ts, N_ST), F32)],
        compiler_params=_params(("arbitrary", "arbitrary")),
    )(dy, proj, s_re, s_im, s_re, s_im, b_re, b_im, c_re, c_im, d_vec, a_re, a_im)


def modulation_fwd(c_all, ada_w, ada_b, fada_w, fada_b):
    def kern(c_ref, w_ref, b_ref, fw_ref, fb_ref, cond_ref, mod_ref, fmod_ref):
        cv = c_ref[...]
        cond = (cv * _sigmoid(cv)).astype(BF16)
        cond_ref[...] = cond
        mod_ref[...] = _dot(cond, w_ref[...].astype(BF16)) + b_ref[...]
        fmod_ref[...] = _dot(cond, fw_ref[...].astype(BF16)) + fb_ref[...]

    n = c_all.shape[0]
    return pl.pallas_call(
        kern, name="modulation_fwd",
        out_shape=[jax.ShapeDtypeStruct(c_all.shape, BF16), jax.ShapeDtypeStruct((n, ada_w.shape[1]), F32),
                   jax.ShapeDtypeStruct((n, fada_w.shape[1]), F32)],
        compiler_params=_params(),
    )(c_all, ada_w, ada_b, fada_w, fada_b)


def modulation_bwd(cond_all, dmod, dfmod, dmod_all):
    def kern(c_ref, dm_ref, dfm_ref, all_ref, gw_ref, gfw_ref, gb_ref):
        cond = c_ref[...]
        gw_ref[...] = _dot_tn(cond, dm_ref[...].astype(BF16))
        gfw_ref[...] = _dot_tn(cond, dfm_ref[...].astype(BF16))
        gb_ref[...] = _colsum(all_ref[...].astype(F32))

    d = cond_all.shape[1]
    return pl.pallas_call(
        kern, name="modulation_bwd",
        out_shape=[jax.ShapeDtypeStruct((d, dmod.shape[1]), F32), jax.ShapeDtypeStruct((d, dfmod.shape[1]), F32),
                   jax.ShapeDtypeStruct((1, dmod_all.shape[1]), F32)],
        compiler_params=_params(),
    )(cond_all, dmod, dfmod, dmod_all)


def _adamw_update(p_ref, w_ref, m_ref, v_ref, g_o, d_o, m_o, v_o):
    g = p_ref[0].astype(F32)
    for i in range(1, p_ref.shape[0]):
        g = g + p_ref[i].astype(F32)
    m_new = ADAM_B1 * m_ref[...] + (1.0 - ADAM_B1) * g
    v_new = ADAM_B2 * v_ref[...] + (1.0 - ADAM_B2) * (g * g)
    m_hat = m_new / (1.0 - ADAM_B1 ** ADAM_STEP)
    v_hat = v_new / (1.0 - ADAM_B2 ** ADAM_STEP)
    g_o[...] = g
    d_o[...] = -ADAM_LR * (m_hat / (jnp.sqrt(v_hat) + ADAM_EPS) + ADAM_WD * w_ref[...])
    m_o[...] = m_new
    v_o[...] = v_new


def adamw_many(name, items):
    k = len(items)

    def kern(*refs):
        ins, outs = refs[:4 * k], refs[4 * k:]
        for i in range(k):
            _adamw_update(*ins[4 * i:4 * i + 4], *outs[4 * i:4 * i + 4])

    flat = [a for item in items for a in item]
    out_shape = [jax.ShapeDtypeStruct(item[1].shape, F32) for item in items for _ in range(4)]
    outs = pl.pallas_call(kern, name=name, out_shape=out_shape, compiler_params=_params())(*flat)
    return [tuple(outs[4 * i:4 * i + 4]) for i in range(k)]


def adamw(name, parts, w, m, v):
    n, R, C = parts.shape
    tr = R
    while n * tr * C * 4 > 4 * 1024 * 1024 and tr % 32 == 0:
        tr //= 2

    def kern(p_ref, w_ref, m_ref, v_ref, g_o, d_o, m_o, v_o):
        _adamw_update(p_ref, w_ref, m_ref, v_ref, g_o, d_o, m_o, v_o)

    blk = pl.BlockSpec((tr, C), lambda i: (i, 0))
    shp = jax.ShapeDtypeStruct((R, C), F32)
    return pl.pallas_call(
        kern, name=name, grid=(R // tr,),
        in_specs=[pl.BlockSpec((n, tr, C), lambda i: (0, i, 0)), blk, blk, blk],
        out_specs=[blk] * 4, out_shape=[shp] * 4,
        compiler_params=_params(("parallel",)),
    )(parts, w, m, v)


def _cols_from_gathered(g):
    return jnp.transpose(g, (1, 0, 2)).reshape(g.shape[1], N_DEV * g.shape[2])


def _cols_to_shards(w):
    k, n8 = w.shape
    return jnp.transpose(w.reshape(k, N_DEV, n8 // N_DEV), (1, 0, 2))


def _pad_w_in(w):
    z = functools.partial(jnp.zeros, dtype=w.dtype)
    k = w.shape[0]
    cut = D_SSM + Q_LORA + KV_LORA
    return jnp.concatenate([w[:, :cut], z((k, 64)), w[:, cut:], z((k, 32))], axis=1)


def _unpad_w_in(w):
    cut = D_SSM + Q_LORA + KV_LORA
    return jnp.concatenate([w[:, :cut], w[:, cut + 64:cut + 96]], axis=1)


def _pad_w_uq(w):
    k = w.shape[0]
    w3 = w.reshape(k, N_HEADS, QK_NOPE + QK_ROPE)
    return jnp.pad(w3, ((0, 0), (0, 0), (0, HEAD_PAD - QK_NOPE - QK_ROPE))).reshape(k, QK_W)


def _unpad_w_uq(w):
    k = w.shape[0]
    return w.reshape(k, N_HEADS, HEAD_PAD)[:, :, :QK_NOPE + QK_ROPE].reshape(k, N_HEADS * (QK_NOPE + QK_ROPE))


def _pad_w_ukv(w):
    k = w.shape[0]
    w3 = w.reshape(k, N_HEADS, QK_NOPE + V_HEAD)
    kpart = jnp.pad(w3[:, :, :QK_NOPE], ((0, 0), (0, 0), (0, HEAD_PAD - QK_NOPE))).reshape(k, QK_W)
    vpart = w3[:, :, QK_NOPE:].reshape(k, V_W)
    return jnp.concatenate([kpart, vpart], axis=1)


def _unpad_w_ukv(w):
    k = w.shape[0]
    kpart = w[:, :QK_W].reshape(k, N_HEADS, HEAD_PAD)[:, :, :QK_NOPE]
    vpart = w[:, QK_W:].reshape(k, N_HEADS, V_HEAD)
    return jnp.concatenate([kpart, vpart], axis=2).reshape(k, N_HEADS * (QK_NOPE + V_HEAD))


def _block_diag(blocks):
    g, r, c = blocks.shape
    b4 = blocks.reshape(N_GB, GB, r, c)
    keep = jnp.eye(GB, dtype=bool)[None, :, None, :, None]
    return jnp.where(keep, b4[:, :, :, None, :], 0).reshape(g * r, GB * c)


def _block_diag_take(stacked, r, c):
    d5 = stacked.reshape(N_GB, GB, r, GB, c)
    idx = jnp.arange(GB)
    return jnp.transpose(d5[:, idx, :, idx, :], (1, 0, 2, 3)).reshape(N_GROUPS, r, c)


SMALL = ["norm1_g", "ssm_lambda_re", "ssm_lambda_im", "ssm_b_re", "ssm_b_im", "ssm_c_re", "ssm_c_im",
         "ssm_d", "ssm_log_dt", "q_norm_g", "kv_norm_g", "ssm_out_g", "attn_out_g", "norm2_g", "final_norm_g"]
BIASES = ["ada_b", "final_ada_b"]


def _dense_2d(shape):
    dims = [d for d in shape[1:]] if len(shape) > 1 and shape[0] == 1 else list(shape)
    if len(dims) == 1:
        return (1, dims[0])
    return (dims[0], int(np.prod(dims[1:])))


def kernel(x, c, positions, ada_w, ada_b, norm1_g, w_in, ssm_lambda_re, ssm_lambda_im, ssm_b_re, ssm_b_im, ssm_c_re, ssm_c_im, ssm_d, ssm_log_dt, w_glu, q_norm_g, w_uq, kv_norm_g, w_ukv, ssm_out_g, attn_out_g, w_out, norm2_g, w_ff1, w_ff2, final_ada_w, final_ada_b, final_norm_g, loss_target, m_ada_w, m_ada_b, m_norm1_g, m_w_in, m_ssm_lambda_re, m_ssm_lambda_im, m_ssm_b_re, m_ssm_b_im, m_ssm_c_re, m_ssm_c_im, m_ssm_d, m_ssm_log_dt, m_w_glu, m_q_norm_g, m_w_uq, m_kv_norm_g, m_w_ukv, m_ssm_out_g, m_attn_out_g, m_w_out, m_norm2_g, m_w_ff1, m_w_ff2, m_final_ada_w, m_final_ada_b, m_final_norm_g, v_ada_w, v_ada_b, v_norm1_g, v_w_in, v_ssm_lambda_re, v_ssm_lambda_im, v_ssm_b_re, v_ssm_b_im, v_ssm_c_re, v_ssm_c_im, v_ssm_d, v_ssm_log_dt, v_w_glu, v_q_norm_g, v_w_uq, v_kv_norm_g, v_w_ukv, v_ssm_out_g, v_attn_out_g, v_w_out, v_norm2_g, v_w_ff1, v_w_ff2, v_final_ada_w, v_final_ada_b, v_final_norm_g):
    given = dict(locals())
    B, S, D = x.shape
    T = B * S
    tm = min(256, S)
    me = 4 * lax.axis_index("x") + 2 * lax.axis_index("y") + lax.axis_index("c")

    big = ["w_in", "w_glu", "w_uq", "w_ukv", "w_out", "w_ff1", "w_ff2"]
    shards = {n: given[n][0] for n in big}
    early, late = ["w_in", "w_uq", "w_ukv"], ["w_glu", "w_out", "w_ff1", "w_ff2"]
    c_g, w_in_g = all_gather("gather_first_weights", [c, shards["w_in"].astype(BF16)])
    c_all = c_g.reshape(N_DEV * B, D)
    w_in_p = _pad_w_in(_cols_from_gathered(w_in_g))

    n_mod = ada_w.shape[2]
    n_fmod = final_ada_w.shape[1]
    ada_b_sh = lax.dynamic_slice_in_dim(ada_b, me * n_mod, n_mod, axis=1)
    fada_b_sh = lax.dynamic_slice_in_dim(final_ada_b[None, :], me * n_fmod, n_fmod, axis=1)
    cond_all, mod_sh, fmod_sh = modulation_fwd(c_all, ada_w[0], ada_b_sh, final_ada_w, fada_b_sh)
    mod_g, fmod_g = all_gather("gather_modulation", [mod_sh, fmod_sh])
    mod_mine = lax.dynamic_slice_in_dim(_cols_from_gathered(mod_g), me * B, B, axis=0)
    fmod_mine = lax.dynamic_slice_in_dim(_cols_from_gathered(fmod_g), me * B, B, axis=0)
    shift1, scale1, gate1, shift2, scale2, gate2 = [mod_mine[:, None, i * D:(i + 1) * D] for i in range(6)]
    fshift, fscale = fmod_mine[:, None, :D], fmod_mine[:, None, D:]

    x2d = x.reshape(T, D)
    tgt2d = loss_target.reshape(T, D)
    pos = positions.astype(F32).reshape(T, 1)
    invf, m_lo, m_hi = _rope_consts()
    rope_mask = m_lo + m_hi

    def fwd_in(rows, bv, vecs, res):
        (xv,), (sc, sh), (g1,), (w,) = rows, bv, vecs, res
        r, xh = _rms_stats(xv)
        h = xh * g1 * (1.0 + sc) + sh
        return [_dot(h.astype(BF16), w[...])], [], []

    qkv_gather = Exchange([shards["w_uq"].astype(BF16), shards["w_ukv"].astype(BF16)], [GATHER, GATHER])
    (proj,), _, _, (w_uq_g, w_ukv_g) = row_call(
        "fwd_in", fwd_in, B, S, tm, [x2d], [scale1, shift1], [norm1_g], [w_in_p], [(IN_PAD, F32)], [], [],
        side=qkv_gather)
    w_uq_p = _pad_w_uq(_cols_from_gathered(w_uq_g))
    w_ukv_p = _pad_w_ukv(_cols_from_gathered(w_ukv_g))

    def fwd_qkv(rows, bv, vecs, res):
        (pr, ps), (gq, gkv, fr, lo, hi), (wq, wkv) = rows, vecs, res
        cs, s_lo, s_hi = _rope_tables(ps, fr, lo, hi)
        _, qh = _rms_stats(pr[:, D_SSM:D_SSM + Q_LORA])
        _, kvh = _rms_stats(pr[:, D_SSM + Q_LORA:D_SSM + Q_LORA + KV_LORA])
        qf = _dot((qh * gq).astype(BF16), wq[...])
        kvf = _dot((kvh * gkv).astype(BF16), wkv[...])
        k_rope = _rope_tile(pr[:, IN_PAD - HEAD_PAD:], cs, s_lo, s_hi, 1.0)
        q_t, k_t = [], []
        for h in range(N_HEADS):
            hs = slice(h * HEAD_PAD, (h + 1) * HEAD_PAD)
            q_t.append(_rope_tile(qf[:, hs], cs, s_lo, s_hi, 1.0))
            k_t.append(kvf[:, hs] + k_rope)
        kf, vf = jnp.concatenate(k_t, axis=1), kvf[:, QK_W:]
        return [jnp.concatenate(q_t, axis=1), kf, vf, t_val(kf), t_val(vf)[None], cs, s_lo, s_hi], [], []

    ns = S // tm
    col_tile = lambda b, s: (0, b * ns + s)
    t_out = lambda w: ((w, T), BF16, (w, tm), col_tile)
    t_val = lambda v: v.astype(BF16).T
    (q_r, k_r, v_r, k_t, v_t, rope_cs, rope_lo, rope_hi), _, _ = row_call(
        "fwd_qkv", fwd_qkv, B, S, tm, [proj, pos], [], [q_norm_g, kv_norm_g, invf, m_lo, m_hi],
        [w_uq_p, w_ukv_p],
        [(QK_W, BF16), (QK_W, BF16), (V_W, BF16), ((QK_W, T), BF16, (QK_W, tm), col_tile),
         ((B, V_W, S), BF16, (1, V_W, tm), lambda b, s: (b, 0, s)), (HEAD_PAD, F32), (HEAD_PAD, F32),
         (HEAD_PAD, F32)], [], [])
    late_gather = Exchange([shards[n].astype(BF16) for n in late], [GATHER] * len(late))
    y_attn, lse, *late_g = attention_fwd(q_r.reshape(B, S, QK_W), k_r.reshape(B, S, QK_W), v_t, B, S, tm,
                                         late_gather)
    y_attn = y_attn.reshape(T, V_W)
    lse = jnp.transpose(lse.reshape(B, N_HEADS, 1, S), (1, 2, 0, 3)).reshape(N_HEADS, 1, T)
    gw = dict(zip(late, late_g))
    w_glu_f = _cols_from_gathered(gw["w_glu"])
    w_out_f = gw["w_out"].reshape(-1, D)
    w_ff1_f = _cols_from_gathered(gw["w_ff1"])
    w_ff2_f = gw["w_ff2"].reshape(-1, D)

    rep = lambda a: jnp.repeat(a, GROUP, axis=0)
    lam_rep = [rep(ssm_lambda_re[0]), rep(ssm_lambda_im[0]),
               rep(jnp.broadcast_to(ssm_log_dt[0][:, None], (N_GROUPS, N_STATE)))]
    bt = lambda a: jnp.transpose(a, (0, 2, 1)).reshape(D_SSM, N_STATE)
    reps = lam_rep + [bt(ssm_b_re[0]), bt(ssm_b_im[0])]
    a_re_rep, a_im_rep, bb_re, bb_im = ssm_params_fwd(reps)
    a_re = a_re_rep.reshape(N_GROUPS, GROUP, N_STATE)[:, 0, :].reshape(1, N_ST)
    a_im = a_im_rep.reshape(N_GROUPS, GROUP, N_STATE)[:, 0, :].reshape(1, N_ST)
    bbd_re = _block_diag(bb_re.reshape(N_GROUPS, GROUP, N_STATE)).astype(BF16)
    bbd_im = _block_diag(bb_im.reshape(N_GROUPS, GROUP, N_STATE)).astype(BF16)
    cbd_re = _block_diag(jnp.transpose(ssm_c_re[0], (0, 2, 1))).astype(BF16)
    cbd_im = _block_diag(jnp.transpose(ssm_c_im[0], (0, 2, 1))).astype(BF16)
    d_vec = ssm_d.reshape(1, D_SSM)
    st_re, st_im, y_pre = ssm_fwd(proj, bbd_re, bbd_im, cbd_re, cbd_im, d_vec, a_re, a_im, B, S, tm)

    def fwd_mix(rows, bv, vecs, res):
        (yp, ya, xv), (g1v,), (gso, gao), (wg, wo) = rows, bv, vecs, res
        z = _dot(_gelu(yp).astype(BF16), wg[...])
        y_ssm = z[:, :D_SSM] * _sigmoid(z[:, D_SSM:])
        _, sh = _rms_stats(y_ssm)
        _, ah = _rms_stats(ya)
        o = _dot((sh * gso).astype(BF16), wo[0:D_SSM, :]) + _dot((ah * gao).astype(BF16), wo[D_SSM:, :])
        return [z, o, xv + g1v * o], [], []

    (z, o, x2), _, _ = row_call("fwd_mix", fwd_mix, B, S, tm, [y_pre, y_attn, x2d], [gate1],
                                [ssm_out_g, attn_out_g], [w_glu_f, w_out_f],
                                [(2 * D_SSM, F32), (D, F32), (D, F32)], [], [])

    def final_out(x3, gf, fsc, fsh):
        r3, xh3 = _rms_stats(x3)
        n3 = xh3 * gf
        return r3, xh3, n3, n3 * (1.0 + fsc) + fsh

    def mlp_hidden(xv, g2, sc, sh, w1):
        r2, xh2 = _rms_stats(xv)
        n2 = xh2 * g2
        h2 = n2 * (1.0 + sc) + sh
        return r2, xh2, n2, h2, _dot(h2.astype(BF16), w1[...])

    def fwd_mlp(rows, bv, vecs, res):
        (xv, tg), (sc, sh, g2v, fsc, fsh), (g2, gf), (w1, w2) = rows, bv, vecs, res
        _, _, _, _, a = mlp_hidden(xv, g2, sc, sh, w1)
        ra = jnp.maximum(a, 0.0)
        ff = _dot((ra * ra).astype(BF16), w2[...])
        x3 = xv + g2v * ff
        _, _, _, out = final_out(x3, gf, fsc, fsh)
        err = out - tg
        loss = 0.5 * jnp.sum(jnp.mean(err * err, axis=-1, keepdims=True), axis=0, keepdims=True)
        return [ff, x3], [], [jnp.broadcast_to(loss, (1, 128))]

    fnorm_g = final_norm_g[None, :]
    (ff, x3), _, (loss_acc,) = row_call(
        "fwd_mlp", fwd_mlp, B, S, tm, [x2, tgt2d], [scale2, shift2, gate2, fscale, fshift], [norm2_g, fnorm_g],
        [w_ff1_f, w_ff2_f], [(D, F32), (D, F32)], [], [(1, 128)])
    loss = lax.psum(loss_acc[0, 0], ("x", "y", "c"))

    def bwd_mlp(rows, bv, vecs, res):
        (x3v, tg, x2v, ffv), (sc, sh, g2v, fsc, fsh), (g2, gf), (w1, w2) = rows, bv, vecs, res
        r3, xh3, n3, out = final_out(x3v, gf, fsc, fsh)
        dout = (out - tg) * (1.0 / D)
        dn3 = dout * (1.0 + fsc)
        dx3 = _rms_bwd(dn3 * gf, xh3, r3)
        dff = dx3 * g2v
        r2, xh2, n2, h2, a = mlp_hidden(x2v, g2, sc, sh, w1)
        ra = jnp.maximum(a, 0.0)
        dffb = dff.astype(BF16)
        da = _dot_nt(dffb, w2[...]) * (2.0 * ra)
        dab = da.astype(BF16)
        dh2 = _dot_nt(dab, w1[...])
        dn2 = dh2 * (1.0 + sc)
        dx2 = dx3 + _rms_bwd(dn2 * g2, xh2, r2)
        return ([dx2, t_val(h2), t_val(ra * ra), dab, dffb],
                [_colsum(dout), _colsum(dout * n3), _colsum(dx3 * ffv), _colsum(dh2), _colsum(dh2 * n2)],
                [_colsum(dn3 * xh3), _colsum(dn2 * xh2)])

    ((dx2, h2b, fb, dab, dffb), (d_fshift, d_fscale, d_gate2, d_shift2, d_scale2), (d_gf, d_g2)) = row_call(
        "bwd_mlp", bwd_mlp, B, S, tm, [x3, tgt2d, x2, ff], [scale2, shift2, gate2, fscale, fshift],
        [norm2_g, fnorm_g], [w_ff1_f, w_ff2_f],
        [(D, F32), t_out(D), t_out(4 * D), (4 * D, BF16), (D, BF16)], [D] * 5, [(1, D), (1, D)])
    g_w_ff1 = matmul_nn("grad_w_ff1", h2b, dab)
    g_w_ff2 = matmul_nn("grad_w_ff2", fb, dffb)

    def bwd_mix(rows, bv, vecs, res):
        (dxv, ov, zv, ya, yp), (g1v,), (gso, gao), (wo, wg) = rows, bv, vecs, res
        do = (dxv * g1v).astype(BF16)
        dyn = _dot_nt(do, wo[...])
        z1, sg = zv[:, :D_SSM], _sigmoid(zv[:, D_SSM:])
        y_ssm = z1 * sg
        rs, sh = _rms_stats(y_ssm)
        ra, ah = _rms_stats(ya)
        dyn_s, dyn_a = dyn[:, :D_SSM], dyn[:, D_SSM:]
        dy_ssm = _rms_bwd(dyn_s * gso, sh, rs)
        dy_attn = _rms_bwd(dyn_a * gao, ah, ra)
        dz = jnp.concatenate([dy_ssm * sg, dy_ssm * z1 * sg * (1.0 - sg)], axis=1).astype(BF16)
        dy_pre = _dot_nt(dz, wg[...]) * _gelu_grad(yp)
        yn = jnp.concatenate([sh * gso, ah * gao], axis=1)
        delta = jnp.sum((dy_attn * ya).T.reshape(N_HEADS, V_HEAD, tm), axis=1, keepdims=True)
        return ([do, t_val(yn), dz, t_val(_gelu(yp)), dy_attn, dy_pre, delta], [_colsum(dxv * ov)],
                [_colsum(dyn_s * sh), _colsum(dyn_a * ah)])

    ((dob, ynb, dzb, ygb, dy_attn, dy_pre, delta), (d_gate1,), (d_gso, d_gao)) = row_call(
        "bwd_mix", bwd_mix, B, S, tm, [dx2, o, z, y_attn, y_pre], [gate1], [ssm_out_g, attn_out_g],
        [w_out_f, w_glu_f],
        [(D, BF16), t_out(D), (2 * D_SSM, BF16), t_out(D_SSM), (V_W, F32), (D_SSM, F32),
         ((N_HEADS, 1, T), F32, (N_HEADS, 1, tm), lambda b, s: (0, 0, b * ns + s))],
        [D], [(1, D_SSM), (1, V_W)])
    g_w_out = matmul_nn("grad_w_out", ynb, dob)
    g_w_glu = matmul_nn("grad_w_glu", ygb, dzb)

    grads_a = Exchange([g_w_glu, g_w_out.reshape(N_DEV, -1, D), g_w_ff1, g_w_ff2.reshape(N_DEV, -1, D)],
                       [w_glu.shape[2], None, w_ff1.shape[2], None])
    dq_t, dk_r, dv_r, *parts_a = attention_bwd(q_r, k_r, k_t, v_r, dy_attn, lse, delta, B, S, tm, grads_a)
    parts = dict(zip(late, parts_a))
    dq_t = dq_t.reshape(B, QK_W, S)

    def bwd_qkv(rows, bv, vecs, res):
        (dqt, dkv, dvv, pr, cs, s_lo, s_hi), (gq, gkv, rmask), (wq, wkv) = rows, vecs, res
        dqv = dqt[0].T
        dq_t = []
        dk_rope = jnp.zeros((dkv.shape[0], HEAD_PAD), F32)
        for h in range(N_HEADS):
            hs = slice(h * HEAD_PAD, (h + 1) * HEAD_PAD)
            dq_t.append(_rope_tile(dqv[:, hs], cs, s_lo, s_hi, -1.0))
            dk_rope = dk_rope + dkv[:, hs]
        dk_rope = _rope_tile(dk_rope * rmask, cs, s_lo, s_hi, -1.0)
        dqb = jnp.concatenate(dq_t, axis=1).astype(BF16)
        dkvb = jnp.concatenate([dkv, dvv], axis=1).astype(BF16)
        rq, qh = _rms_stats(pr[:, D_SSM:D_SSM + Q_LORA])
        rk, kvh = _rms_stats(pr[:, D_SSM + Q_LORA:D_SSM + Q_LORA + KV_LORA])
        dqn = _dot_nt(dqb, wq[...])
        dkvn = _dot_nt(dkvb, wkv[...])
        dpa = jnp.concatenate([_rms_bwd(dqn * gq, qh, rq), _rms_bwd(dkvn * gkv, kvh, rk), dk_rope], axis=1)
        return [dpa, t_val(qh * gq), t_val(kvh * gkv), dqb, dkvb], [], [_colsum(dqn * qh), _colsum(dkvn * kvh)]

    ((dpa, qnb, kvnb, dqb, dkvb), _, (d_gq, d_gkv)) = row_call(
        "bwd_qkv", bwd_qkv, B, S, tm,
        [(dq_t, (1, QK_W, tm), lambda b, s: (b, 0, s)), dk_r, dv_r, proj, rope_cs, rope_lo, rope_hi], [],
        [q_norm_g, kv_norm_g, rope_mask], [w_uq_p, w_ukv_p],
        [(Q_LORA + KV_LORA + HEAD_PAD, F32), t_out(Q_LORA), t_out(KV_LORA), (QK_W, BF16), (QK_W + V_W, BF16)],
        [], [(1, Q_LORA), (1, KV_LORA)])
    g_w_uq = _unpad_w_uq(matmul_nn("grad_w_uq", qnb, dqb))
    g_w_ukv = _unpad_w_ukv(matmul_nn("grad_w_ukv", kvnb, dkvb))

    du, dc_re_d, dc_im_d, db_re_d, db_im_d, da_re, da_im, d_d = ssm_bwd(
        dy_pre, proj, st_re, st_im, bbd_re, bbd_im, cbd_re, cbd_im, d_vec, a_re, a_im, B, S, tm)
    place = lambda a: jnp.zeros((N_GROUPS, GROUP, N_STATE), F32).at[:, 0, :].set(
        a.reshape(N_GROUPS, N_STATE)).reshape(D_SSM, N_STATE)
    cots = [place(da_re), place(da_im),
            _block_diag_take(db_re_d, GROUP, N_STATE).reshape(D_SSM, N_STATE),
            _block_diag_take(db_im_d, GROUP, N_STATE).reshape(D_SSM, N_STATE)]
    g_lam_re, g_lam_im, g_log_dt, g_bt_re, g_bt_im = ssm_params_bwd(reps, cots)
    unbt = lambda a: jnp.transpose(a.reshape(N_GROUPS, GROUP, N_STATE), (0, 2, 1))
    g_c_re = jnp.transpose(_block_diag_take(dc_re_d, N_STATE, GROUP), (0, 2, 1))
    g_c_im = jnp.transpose(_block_diag_take(dc_im_d, N_STATE, GROUP), (0, 2, 1))

    def bwd_in(rows, bv, vecs, res):
        (duv, dpav, xv, dx2v), (sc, sh), (g1,), (w,) = rows, bv, vecs, res
        dproj = jnp.concatenate([duv, dpav], axis=1).astype(BF16)
        dh = _dot_nt(dproj, w[...])
        r, xh = _rms_stats(xv)
        n1 = xh * g1
        dn1 = dh * (1.0 + sc)
        dx = dx2v + _rms_bwd(dn1 * g1, xh, r)
        return [dx, t_val(n1 * (1.0 + sc) + sh), dproj], [_colsum(dh), _colsum(dh * n1)], [_colsum(dn1 * xh)]

    small_part = {
        "ssm_lambda_re": g_lam_re,
        "ssm_lambda_im": g_lam_im, "ssm_b_re": unbt(g_bt_re), "ssm_b_im": unbt(g_bt_im), "ssm_c_re": g_c_re,
        "ssm_c_im": g_c_im, "ssm_d": d_d, "ssm_log_dt": g_log_dt, "q_norm_g": d_gq, "kv_norm_g": d_gkv,
        "ssm_out_g": d_gso, "attn_out_g": d_gao, "norm2_g": d_g2, "final_norm_g": d_gf}
    dense_bf16 = lambda n, g: g.reshape(_dense_2d(given[n].shape)).astype(BF16)
    ready = [n for n in SMALL if n in small_part]
    grads_b = Exchange([dense_bf16(n, small_part[n]) for n in ready] + [_cols_to_shards(g_w_uq), g_w_ukv],
                       [GATHER] * len(ready) + [None, w_ukv.shape[2]])
    ((grad_x, h1b, dprojb), (d_shift1, d_scale1), (d_g1,)) = row_call(
        "bwd_in", bwd_in, B, S, tm, [du, dpa, x2d, dx2], [scale1, shift1], [norm1_g], [w_in_p],
        [(D, F32), t_out(D), (IN_PAD, BF16)], [D, D], [(1, D)])
    g_w_in_p, *side_b = matmul_nn("grad_w_in", h1b, dprojb, side=grads_b)
    small_parts = dict(zip(ready, side_b[:len(ready)]))
    parts["w_uq"], parts["w_ukv"] = side_b[len(ready):]
    g_w_in = _unpad_w_in(g_w_in_p)

    dmod = jnp.concatenate([d_shift1, d_scale1, d_gate1, d_shift2, d_scale2, d_gate2, d_fshift, d_fscale],
                           axis=2).reshape(B, 8 * D)
    dmod_g, small_parts["norm1_g"], parts["w_in"] = exchange(
        "exchange_last_grads", [dmod.astype(BF16), dense_bf16("norm1_g", d_g1), _cols_to_shards(g_w_in)],
        [GATHER, GATHER, None])
    small_g = [small_parts[n] for n in SMALL]
    dmod_all = dmod_g.reshape(N_DEV * B, 8 * D)
    dm_sh = lax.dynamic_slice_in_dim(dmod_all[:, :6 * D], me * n_mod, n_mod, axis=1)
    dfm_sh = lax.dynamic_slice_in_dim(dmod_all[:, 6 * D:], me * n_fmod, n_fmod, axis=1)
    g_ada_w, g_fada_w, g_biases = modulation_bwd(cond_all, dm_sh, dfm_sh, dmod_all)
    parts["ada_w"] = g_ada_w[None]
    parts["final_ada_w"] = g_fada_w[None]

    res_g, res_d, res_m, res_v = {}, {}, {}, {}
    for n in ["ada_w"] + big + ["final_ada_w"]:
        w_full = given[n]
        w2 = w_full.reshape(w_full.shape[-2], w_full.shape[-1])
        out = adamw("adamw_" + n, parts[n], w2, given["m_" + n].reshape(w2.shape), given["v_" + n].reshape(w2.shape))
        res_g[n], res_d[n], res_m[n], res_v[n] = [a.reshape(w_full.shape) for a in out]
    bias_g = {"ada_b": g_biases[None, :, :6 * D], "final_ada_b": g_biases[None, :, 6 * D:]}
    names = SMALL + BIASES
    items = []
    for n, g_parts in zip(names, small_g + [bias_g[b] for b in BIASES]):
        dense = _dense_2d(given[n].shape)
        items.append((g_parts, given[n].reshape(dense), given["m_" + n].reshape(dense), given["v_" + n].reshape(dense)))
    for n, out in zip(names, adamw_many("adamw_small", items)):
        res_g[n], res_d[n], res_m[n], res_v[n] = [a.reshape(given[n].shape) for a in out]

    order = ["ada_w", "ada_b", "norm1_g", "w_in", "ssm_lambda_re", "ssm_lambda_im", "ssm_b_re", "ssm_b_im",
             "ssm_c_re", "ssm_c_im", "ssm_d", "ssm_log_dt", "w_glu", "q_norm_g", "w_uq", "kv_norm_g", "w_ukv",
             "ssm_out_g", "attn_out_g", "w_out", "norm2_g", "w_ff1", "w_ff2", "final_ada_w", "final_ada_b",
             "final_norm_g"]
    return (loss, grad_x.reshape(B, S, D), *[res_g[n] for n in order], *[res_d[n] for n in order],
            *[res_m[n] for n in order], *[res_v[n] for n in order])
```

```python
import functools
import math

import numpy as np
import jax
import jax.numpy as jnp
from jax import lax
from jax.experimental import pallas as pl
from jax.experimental.pallas import tpu as pltpu

F32 = jnp.float32
BF16 = jnp.bfloat16

N_DEV = 8
EPS = 1e-6
D_SSM = 512
N_GROUPS = 32
GROUP = 16
N_STATE = 64
N_ST = N_GROUPS * N_STATE
Q_LORA = 384
KV_LORA = 256
QK_NOPE = 64
QK_ROPE = 32
V_HEAD = 64
N_HEADS = 8
HEAD_PAD = 128
QK_W = N_HEADS * HEAD_PAD
V_W = N_HEADS * V_HEAD
IN_COLS = D_SSM + Q_LORA + KV_LORA + QK_ROPE
IN_PAD = D_SSM + Q_LORA + KV_LORA + HEAD_PAD
ROPE_BASE = 10000.0
ATTN_SCALE = (QK_NOPE + QK_ROPE) ** -0.5
NEG = -1e30

ADAM_LR = 0.001
ADAM_B1 = 0.9
ADAM_B2 = 0.999
ADAM_EPS = 1e-08
ADAM_WD = 0.01
ADAM_STEP = 10

VMEM_LIMIT = 56 * 1024 * 1024
MESH_ID = pl.DeviceIdType.MESH


def _dot(a, b):
    return jnp.dot(a, b, preferred_element_type=F32)


def _dot_nt(a, b):
    return lax.dot_general(a, b, (((1,), (1,)), ((), ())), preferred_element_type=F32)


def _dot_tn(a, b):
    return lax.dot_general(a, b, (((0,), (0,)), ((), ())), preferred_element_type=F32)


def _rms_stats(x):
    r = lax.rsqrt(jnp.mean(x * x, axis=-1, keepdims=True) + EPS)
    return r, x * r


def _rms_bwd(dy, xh, r):
    return r * (dy - xh * jnp.mean(dy * xh, axis=-1, keepdims=True))


def _sigmoid(x):
    return 1.0 / (1.0 + jnp.exp(-x))


_GELU_C = math.sqrt(2.0 / math.pi)


def _gelu(x):
    t = jnp.tanh(_GELU_C * (x + 0.044715 * x * x * x))
    return 0.5 * x * (1.0 + t)


def _gelu_grad(x):
    t = jnp.tanh(_GELU_C * (x + 0.044715 * x * x * x))
    return 0.5 * (1.0 + t) + 0.5 * x * (1.0 - t * t) * _GELU_C * (1.0 + 3.0 * 0.044715 * x * x)


def _colsum(a):
    return jnp.sum(a, axis=0, keepdims=True)


def _params(sem=None):
    return pltpu.CompilerParams(dimension_semantics=sem, vmem_limit_bytes=VMEM_LIMIT)


def _mesh_pos():
    x, y, c = lax.axis_index("x"), lax.axis_index("y"), lax.axis_index("c")
    return x, y, c, 4 * x + 2 * y + c


def _peer(x, y, c, k):
    px = 1 - x if k & 4 else x
    py = 1 - y if k & 2 else y
    pc = 1 - c if k & 1 else c
    return (px, py, pc), 4 * px + 2 * py + pc


GATHER = "gather"
ANY_SPEC = pl.BlockSpec(memory_space=pl.ANY)


class Exchange:
    def __init__(self, arrays, kinds):
        self.arrays, self.kinds, self.n = list(arrays), list(kinds), len(arrays)

    def _shard(self, i):
        a, kind = self.arrays[i], self.kinds[i]
        if kind == GATHER:
            return a.shape
        return a.shape[1:] if kind is None else (a.shape[0], kind)

    def out_shapes(self):
        return [jax.ShapeDtypeStruct((N_DEV,) + tuple(self._shard(i)), self.arrays[i].dtype) for i in range(self.n)]

    def scratch(self):
        return [pltpu.SemaphoreType.DMA((self.n, N_DEV - 1)), pltpu.SemaphoreType.DMA((self.n, N_DEV - 1)),
                pltpu.SemaphoreType.DMA((self.n,))]

    def _src(self, ins, i, j):
        kind = self.kinds[i]
        if kind == GATHER:
            return ins[i]
        if kind is None:
            return ins[i].at[j]
        return ins[i].at[:, pl.ds(pl.multiple_of(j * kind, 128), kind)]

    def _copies(self, ins, outs, sems, with_received):
        send_sems, recv_sems, loc_sems = sems
        x, y, c, me = _mesh_pos()
        local, sent, received = [], [], []
        for i in range(self.n):
            local.append(pltpu.make_async_copy(self._src(ins, i, me), outs[i].at[me], loc_sems.at[i]))
            for k in range(1, N_DEV):
                peer, pidx = _peer(x, y, c, k)
                pair = dict(send_sem=send_sems.at[i, k - 1], recv_sem=recv_sems.at[i, k - 1], device_id=peer,
                            device_id_type=MESH_ID)
                sent.append(pltpu.make_async_remote_copy(
                    src_ref=self._src(ins, i, pidx), dst_ref=outs[i].at[me], **pair))
                if with_received:
                    received.append(pltpu.make_async_remote_copy(
                        src_ref=self._src(ins, i, pidx), dst_ref=outs[i].at[pidx], **pair))
        return local, sent, received

    def start(self, ins, outs, sems):
        local, sent, _ = self._copies(ins, outs, sems, False)
        for cp in local + sent:
            cp.start()

    def wait(self, ins, outs, sems):
        local, sent, received = self._copies(ins, outs, sems, True)
        for cp in received:
            cp.wait_recv()
        for cp in sent:
            cp.wait_send()
        for cp in local:
            cp.wait()


def exchange(name, arrays, kinds):
    ex = Exchange(arrays, kinds)
    n = ex.n

    def body(*refs):
        ins, outs, sems = refs[:n], refs[n:2 * n], refs[2 * n:]
        ex.start(ins, outs, sems)
        ex.wait(ins, outs, sems)

    return pl.pallas_call(
        body, name=name, out_shape=ex.out_shapes(), in_specs=[ANY_SPEC] * n, out_specs=[ANY_SPEC] * n,
        scratch_shapes=ex.scratch())(*arrays)


def all_gather(name, arrays):
    return exchange(name, arrays, [GATHER] * len(arrays))


def row_call(name, body, B, S, tm, rows, bvecs, vecs, res, row_outs, bacc, gacc, side=None):
    ns = S // tm
    T = B * S
    n_r, n_b, n_v, n_w = len(rows), len(bvecs), len(vecs), len(res)
    n_ro, n_ba, n_ga = len(row_outs), len(bacc), len(gacc)
    n_x = side.n if side is not None else 0

    def kern(*refs):
        i = 0
        r_refs = refs[i:i + n_r]; i += n_r
        b_refs = refs[i:i + n_b]; i += n_b
        v_refs = refs[i:i + n_v]; i += n_v
        w_hbm = refs[i:i + n_w]; i += n_w
        x_in = refs[i:i + n_x]; i += n_x
        ro_refs = refs[i:i + n_ro]; i += n_ro
        ba_refs = refs[i:i + n_ba]; i += n_ba
        ga_refs = refs[i:i + n_ga]; i += n_ga
        x_out = refs[i:i + n_x]; i += n_x
        w_vmem = refs[i:i + n_w]; i += n_w
        x_sems = refs[i:]
        b = pl.program_id(0)
        s = pl.program_id(1)
        first = jnp.logical_and(b == 0, s == 0)

        if n_x:
            @pl.when(first)
            def _start_side():
                side.start(x_in, x_out, x_sems)

        if n_w:
            @pl.when(first)
            def _load_weights():
                for h, v in zip(w_hbm, w_vmem):
                    pltpu.sync_copy(h, v)

        ro, ba, ga = body([r[...] for r in r_refs], [r[0] for r in b_refs], [r[...] for r in v_refs],
                          list(w_vmem))
        for ref, val in zip(ro_refs, ro):
            ref[...] = val.astype(ref.dtype)

        def accumulate(ref, val, is_first, idx):
            @pl.when(is_first)
            def _set():
                ref[idx] = val

            @pl.when(jnp.logical_not(is_first))
            def _add():
                ref[idx] = ref[idx] + val

        for ref, val in zip(ba_refs, ba):
            accumulate(ref, val, s == 0, 0)
        for ref, val in zip(ga_refs, ga):
            accumulate(ref, val, first, Ellipsis)

        if n_x:
            @pl.when(jnp.logical_and(b == B - 1, s == ns - 1))
            def _wait_side():
                side.wait(x_in, x_out, x_sems)

    row_arrays = [r[0] if isinstance(r, tuple) else r for r in rows]
    row_in_specs = [pl.BlockSpec(r[1], r[2]) if isinstance(r, tuple)
                    else pl.BlockSpec((tm, r.shape[1]), lambda b, s: (b * ns + s, 0)) for r in rows]
    ro_shapes = [jax.ShapeDtypeStruct(tuple(o[0]), o[1]) if len(o) == 4 else jax.ShapeDtypeStruct((T, o[0]), o[1])
                 for o in row_outs]
    ro_specs = [pl.BlockSpec(o[2], o[3]) if len(o) == 4 else pl.BlockSpec((tm, o[0]), lambda b, s: (b * ns + s, 0))
                for o in row_outs]
    in_specs = (row_in_specs
                + [pl.BlockSpec((1, 1, v.shape[2]), lambda b, s: (b, 0, 0)) for v in bvecs]
                + [pl.BlockSpec(v.shape, lambda b, s, nd=v.ndim: (0,) * nd) for v in vecs]
                + [ANY_SPEC] * (n_w + n_x))
    out_shape = (ro_shapes
                 + [jax.ShapeDtypeStruct((B, 1, w), F32) for w in bacc]
                 + [jax.ShapeDtypeStruct(tuple(sh), F32) for sh in gacc]
                 + (side.out_shapes() if n_x else []))
    out_specs = (ro_specs
                 + [pl.BlockSpec((1, 1, w), lambda b, s: (b, 0, 0)) for w in bacc]
                 + [pl.BlockSpec(tuple(sh), lambda b, s, nd=len(sh): (0,) * nd) for sh in gacc]
                 + [ANY_SPEC] * n_x)
    outs = pl.pallas_call(
        kern, name=name, grid=(B, ns), in_specs=in_specs, out_specs=out_specs, out_shape=out_shape,
        scratch_shapes=[pltpu.VMEM(w.shape, w.dtype) for w in res] + (side.scratch() if n_x else []),
        compiler_params=_params(("arbitrary", "arbitrary")),
    )(*row_arrays, *bvecs, *vecs, *res, *(side.arrays if n_x else []))
    n_acc = n_ro + n_ba + n_ga
    if n_x:
        return outs[:n_ro], outs[n_ro:n_ro + n_ba], outs[n_ro + n_ba:n_acc], outs[n_acc:]
    return outs[:n_ro], outs[n_ro:n_ro + n_ba], outs[n_ro + n_ba:]


def matmul_nn(name, at, b, side=None):
    K, T = at.shape
    N = b.shape[1]
    tk = min(K, 512)
    tn = N if N <= 1280 else (1024 if N % 1024 == 0 else 512)
    assert K % tk == 0 and N % tn == 0 and T % min(T, 2048) == 0, (K, N, T)
    tt = min(T, 2048)
    nt = T // tt
    n_x = side.n if side is not None else 0
    grid = (K // tk, N // tn, nt)

    def kern(*refs):
        a_ref, b_ref = refs[:2]
        x_in = refs[2:2 + n_x]
        o_ref = refs[2 + n_x]
        x_out = refs[3 + n_x:3 + 2 * n_x]
        acc_ref = refs[3 + 2 * n_x]
        x_sems = refs[4 + 2 * n_x:]
        i, j, t = pl.program_id(0), pl.program_id(1), pl.program_id(2)
        if n_x:
            @pl.when(jnp.logical_and(i == 0, jnp.logical_and(j == 0, t == 0)))
            def _start_side():
                side.start(x_in, x_out, x_sems)

        part = _dot(a_ref[...], b_ref[...])

        @pl.when(t == 0)
        def _set():
            acc_ref[...] = part

        @pl.when(t > 0)
        def _add():
            acc_ref[...] = acc_ref[...] + part

        @pl.when(t == nt - 1)
        def _write():
            o_ref[...] = acc_ref[...].astype(o_ref.dtype)

        if n_x:
            @pl.when(jnp.logical_and(i == grid[0] - 1, jnp.logical_and(j == grid[1] - 1, t == nt - 1)))
            def _wait_side():
                side.wait(x_in, x_out, x_sems)

    outs = pl.pallas_call(
        kern, name=name, grid=grid,
        in_specs=[pl.BlockSpec((tk, tt), lambda i, j, t: (i, t)), pl.BlockSpec((tt, tn), lambda i, j, t: (t, j))]
        + [ANY_SPEC] * n_x,
        out_specs=[pl.BlockSpec((tk, tn), lambda i, j, t: (i, j))] + [ANY_SPEC] * n_x,
        out_shape=[jax.ShapeDtypeStruct((K, N), BF16)] + (side.out_shapes() if n_x else []),
        scratch_shapes=[pltpu.VMEM((tk, tn), F32)] + (side.scratch() if n_x else []),
        compiler_params=_params(("arbitrary",) * 3 if n_x else ("parallel", "parallel", "arbitrary")),
    )(at, b, *(side.arrays if n_x else []))
    return outs if n_x else outs[0]


def _rope_consts():
    inv_freq = ROPE_BASE ** (-jnp.arange(0, QK_ROPE, 2, dtype=F32) / QK_ROPE)
    zeros64 = jnp.zeros((64,), F32)
    zeros32 = jnp.zeros((32,), F32)
    zeros16 = jnp.zeros((16,), F32)
    ones16 = jnp.ones((16,), F32)
    invf = jnp.concatenate([zeros64, inv_freq, inv_freq, zeros32])[None, :]
    m_lo = jnp.concatenate([zeros64, ones16, zeros16, zeros32])[None, :]
    m_hi = jnp.concatenate([zeros64, zeros16, ones16, zeros32])[None, :]
    return invf, m_lo, m_hi


def _rope_tables(pos, invf, m_lo, m_hi):
    ang = pos * invf
    return jnp.cos(ang), jnp.sin(ang) * m_lo, jnp.sin(ang) * m_hi


def _rope_tile(t, cs, s_lo, s_hi, sign):
    up = pltpu.roll(t, HEAD_PAD - 16, axis=1)
    down = pltpu.roll(t, 16, axis=1)
    return t * cs + sign * (down * s_hi - up * s_lo)


def _heads(ref, width):
    if len(ref.shape) == 2:
        return jnp.stack([ref[:, h * width:(h + 1) * width] for h in range(N_HEADS)], axis=0)
    return jnp.stack([ref[b, :, h * width:(h + 1) * width] for b in range(ref.shape[0]) for h in range(N_HEADS)],
                     axis=0)


def _bmm(a, b, ca, cb):
    return lax.dot_general(a, b, (((ca,), (cb,)), ((0,), (0,))), preferred_element_type=F32)


LOG2E = math.log2(math.e)
EXP2_SCALE = ATTN_SCALE * LOG2E


def _scores_t(k_ref, q_ref, masked, tq, key_offset=0):
    st = _bmm(_heads(k_ref, HEAD_PAD), _heads(q_ref, HEAD_PAD), 2, 2)
    if masked:
        tk = st.shape[1]
        key = lax.broadcasted_iota(jnp.int32, (tk, tq), 0) + key_offset
        qry = lax.broadcasted_iota(jnp.int32, (tk, tq), 1)
        st = jnp.where((key <= qry)[None], st, NEG)
    return st


def attention_fwd(q, k, vt, B, S, tq, side):
    nq = S // tq
    ns = side.n
    nbh = B * N_HEADS
    pairs = [(i, j) for i in range(nq) for j in range(i + 1)]
    n_pairs = len(pairs)
    q_tab = jnp.asarray([p[0] for p in pairs], jnp.int32)
    k_tab = jnp.asarray([p[1] for p in pairs], jnp.int32)

    def kern(q_tab_ref, k_tab_ref, *refs):
        q_ref, k_ref, vt_ref = refs[:3]
        side_in = refs[3:3 + ns]
        o_ref, lse_ref = refs[3 + ns:5 + ns]
        side_out = refs[5 + ns:5 + 2 * ns]
        m_scr, l_scr, acc_scr = refs[5 + 2 * ns:8 + 2 * ns]
        sems = refs[8 + 2 * ns:]
        t = pl.program_id(0)
        qi, ki = q_tab_ref[t], k_tab_ref[t]

        @pl.when(t == 0)
        def _start_side():
            side.start(side_in, side_out, sems)

        @pl.when(ki == 0)
        def _init():
            m_scr[...] = jnp.full(m_scr.shape, NEG, F32)
            l_scr[...] = jnp.zeros(l_scr.shape, F32)
            acc_scr[...] = jnp.zeros(acc_scr.shape, F32)

        def block(masked):
            st = _scores_t(k_ref, q_ref, masked, tq)
            m_prev = m_scr[...]
            m_new = jnp.maximum(m_prev, jnp.max(st, axis=1, keepdims=True))
            alpha = jnp.exp2((m_prev - m_new) * EXP2_SCALE)
            p = jnp.exp2((st - m_new) * EXP2_SCALE)
            l_scr[...] = alpha * l_scr[...] + jnp.sum(p, axis=1, keepdims=True)
            vt3 = vt_ref[...].reshape(nbh, V_HEAD, tq)
            acc_scr[...] = alpha * acc_scr[...] + _bmm(vt3, p.astype(BF16), 2, 1)
            m_scr[...] = m_new

        @pl.when(ki < qi)
        def _full():
            block(False)

        @pl.when(ki == qi)
        def _diagonal():
            block(True)
            ot = acc_scr[...] / l_scr[...]
            for b in range(B):
                for h in range(N_HEADS):
                    o_ref[b, :, h * V_HEAD:(h + 1) * V_HEAD] = ot[b * N_HEADS + h].T
            lse_ref[...] = m_scr[...] * ATTN_SCALE + jnp.log(l_scr[...])

        @pl.when(t == n_pairs - 1)
        def _wait_side():
            side.wait(side_in, side_out, sems)

    grid_spec = pltpu.PrefetchScalarGridSpec(
        num_scalar_prefetch=2, grid=(n_pairs,),
        in_specs=[pl.BlockSpec((B, tq, QK_W), lambda t, qt, kt: (0, qt[t], 0)),
                  pl.BlockSpec((B, tq, QK_W), lambda t, qt, kt: (0, kt[t], 0)),
                  pl.BlockSpec((B, V_W, tq), lambda t, qt, kt: (0, 0, kt[t]))] + [ANY_SPEC] * ns,
        out_specs=[pl.BlockSpec((B, tq, V_W), lambda t, qt, kt: (0, qt[t], 0)),
                   pl.BlockSpec((nbh, 1, tq), lambda t, qt, kt: (0, 0, qt[t]))] + [ANY_SPEC] * ns,
        scratch_shapes=[pltpu.VMEM((nbh, 1, tq), F32), pltpu.VMEM((nbh, 1, tq), F32),
                        pltpu.VMEM((nbh, V_HEAD, tq), F32)] + side.scratch())
    return pl.pallas_call(
        kern, name="attention_fwd", grid_spec=grid_spec,
        out_shape=[jax.ShapeDtypeStruct((B, S, V_W), F32), jax.ShapeDtypeStruct((nbh, 1, S), F32)]
        + side.out_shapes(),
        compiler_params=_params(("arbitrary",)),
    )(q_tab, k_tab, q, k, vt, *side.arrays)


def attention_bwd(q, k, kt, v, do, lse, delta, B, S, tq, side):
    tk = tq
    tq = min(2 * tk, S)
    r = tq // tk
    nk, nq = S // tk, S // tq
    ns = side.n
    pairs = [(j, i) for j in range(nk) for i in range(j // r, nq)]
    n_pairs = len(pairs)
    k_tab = jnp.asarray([p[0] for p in pairs], jnp.int32)
    q_tab = jnp.asarray([p[1] for p in pairs], jnp.int32)

    def kern(k_tab_ref, q_tab_ref, *refs):
        q_ref, k_ref, kt_ref, v_ref, do_ref, lse_ref, delta_ref = refs[:7]
        side_in = refs[7:7 + ns]
        dq_hbm, dk_ref, dv_ref = refs[7 + ns:10 + ns]
        side_out = refs[10 + ns:10 + 2 * ns]
        dq_acc, dk_acc, dv_acc = refs[10 + 2 * ns:13 + 2 * ns]
        sems = refs[13 + 2 * ns:]
        b, t = pl.program_id(0), pl.program_id(1)
        kj, qi = k_tab_ref[t], q_tab_ref[t]
        on_diagonal = qi == kj // r

        @pl.when(jnp.logical_and(b == 0, t == 0))
        def _start_side():
            side.start(side_in, side_out, sems)

        @pl.when(t == 0)
        def _zero_dq():
            dq_acc[...] = jnp.zeros(dq_acc.shape, F32)

        @pl.when(on_diagonal)
        def _zero_dkv():
            dk_acc[...] = jnp.zeros(dk_acc.shape, F32)
            dv_acc[...] = jnp.zeros(dv_acc.shape, F32)

        def block(masked):
            st = _scores_t(k_ref, q_ref, masked, tq, kj * tk - qi * tq)
            pt = jnp.exp2(st * EXP2_SCALE - lse_ref[...] * LOG2E)
            do3 = _heads(do_ref, V_HEAD).astype(BF16)
            dv_acc[...] = dv_acc[...] + _bmm(pt.astype(BF16), do3, 2, 1)
            dpt = _bmm(_heads(v_ref, V_HEAD), do3, 2, 2)
            dst = (pt * (dpt - delta_ref[...]) * ATTN_SCALE).astype(BF16)
            dk_acc[...] = dk_acc[...] + _bmm(dst, _heads(q_ref, HEAD_PAD), 2, 1)
            q_cols = pl.ds(pl.multiple_of(qi * tq, tq), tq)
            kt3 = kt_ref[...].reshape(N_HEADS, HEAD_PAD, tk)
            dq_acc[:, :, q_cols] = dq_acc[:, :, q_cols] + _bmm(kt3, dst, 2, 1)

        @pl.when(jnp.logical_not(on_diagonal))
        def _full():
            block(False)

        @pl.when(on_diagonal)
        def _diagonal():
            block(True)

        @pl.when(qi == nq - 1)
        def _write_dkv():
            for h in range(N_HEADS):
                dk_ref[:, h * HEAD_PAD:(h + 1) * HEAD_PAD] = dk_acc[h]
                dv_ref[:, h * V_HEAD:(h + 1) * V_HEAD] = dv_acc[h]

        @pl.when(t == n_pairs - 1)
        def _write_dq():
            pltpu.sync_copy(dq_acc, dq_hbm.at[b])

        @pl.when(jnp.logical_and(b == B - 1, t == n_pairs - 1))
        def _wait_side():
            side.wait(side_in, side_out, sems)

    T = B * S
    qrow = lambda b, t, kt, qt: (b * nq + qt[t], 0)
    qcol3 = lambda b, t, kt, qt: (0, 0, b * nq + qt[t])
    krow = lambda b, t, kt, qt: (b * nk + kt[t], 0)
    grid_spec = pltpu.PrefetchScalarGridSpec(
        num_scalar_prefetch=2, grid=(B, n_pairs),
        in_specs=[pl.BlockSpec((tq, QK_W), qrow), pl.BlockSpec((tk, QK_W), krow),
                  pl.BlockSpec((QK_W, tk), lambda b, t, kt, qt: (0, b * nk + kt[t])), pl.BlockSpec((tk, V_W), krow),
                  pl.BlockSpec((tq, V_W), qrow), pl.BlockSpec((N_HEADS, 1, tq), qcol3),
                  pl.BlockSpec((N_HEADS, 1, tq), qcol3)] + [ANY_SPEC] * ns,
        out_specs=[ANY_SPEC, pl.BlockSpec((tk, QK_W), krow), pl.BlockSpec((tk, V_W), krow)] + [ANY_SPEC] * ns,
        scratch_shapes=[pltpu.VMEM((N_HEADS, HEAD_PAD, S), F32), pltpu.VMEM((N_HEADS, tk, HEAD_PAD), F32),
                        pltpu.VMEM((N_HEADS, tk, V_HEAD), F32)] + side.scratch())
    return pl.pallas_call(
        kern, name="attention_bwd", grid_spec=grid_spec,
        out_shape=[jax.ShapeDtypeStruct((B, N_HEADS, HEAD_PAD, S), F32), jax.ShapeDtypeStruct((T, QK_W), F32),
                   jax.ShapeDtypeStruct((T, V_W), F32)] + side.out_shapes(),
        compiler_params=_params(("arbitrary", "arbitrary")),
    )(k_tab, q_tab, q, k, kt, v, do, lse, delta, *side.arrays)


GB = 8
N_GB = N_GROUPS // GB
GB_U = GB * GROUP
GB_ST = GB * N_STATE


def _gb_u(j):
    return slice(j * GB_U, (j + 1) * GB_U)


def _gb_s(j):
    return slice(j * GB_ST, (j + 1) * GB_ST)


def ssm_param_fn(lr, li, ld, br, bi):
    dt = jnp.exp(ld)
    er = jnp.exp(lr * dt)
    ar = er * jnp.cos(li * dt)
    ai = er * jnp.sin(li * dt)
    nr = ar - 1.0
    den = lr * lr + li * li
    cr = (nr * lr + ai * li) / den
    ci = (ai * lr - nr * li) / den
    return ar, ai, cr * br - ci * bi, cr * bi + ci * br


def ssm_params_fwd(reps):
    def kern(lr, li, ld, br, bi, ar_o, ai_o, bbr_o, bbi_o):
        ar, ai, bbr, bbi = ssm_param_fn(lr[...], li[...], ld[...], br[...], bi[...])
        ar_o[...] = ar
        ai_o[...] = ai
        bbr_o[...] = bbr
        bbi_o[...] = bbi

    shp = jax.ShapeDtypeStruct(reps[0].shape, F32)
    return pl.pallas_call(kern, name="ssm_params_fwd", out_shape=[shp] * 4)(*reps)


def ssm_params_bwd(reps, cots):
    rows = reps[0].shape[0]

    def kern(lr, li, ld, br, bi, d_ar, d_ai, d_bbr, d_bbi, dlr_o, dli_o, dld_o, dbr_o, dbi_o):
        _, vjp = jax.vjp(ssm_param_fn, lr[...], li[...], ld[...], br[...], bi[...])
        dlr, dli, dld, dbr, dbi = vjp((d_ar[...], d_ai[...], d_bbr[...], d_bbi[...]))
        dlr_o[...] = jnp.sum(dlr.reshape(N_GROUPS, GROUP, N_STATE), axis=1)
        dli_o[...] = jnp.sum(dli.reshape(N_GROUPS, GROUP, N_STATE), axis=1)
        dld_o[...] = jnp.sum(jnp.sum(dld.reshape(N_GROUPS, GROUP, N_STATE), axis=1), axis=-1, keepdims=True)
        dbr_o[...] = dbr
        dbi_o[...] = dbi

    g = jax.ShapeDtypeStruct((N_GROUPS, N_STATE), F32)
    full = jax.ShapeDtypeStruct((rows, N_STATE), F32)
    return pl.pallas_call(
        kern, name="ssm_params_bwd",
        out_shape=[g, g, jax.ShapeDtypeStruct((N_GROUPS, 1), F32), full, full])(*reps, *cots)


def ssm_fwd(proj, b_re, b_im, c_re, c_im, d_vec, a_re, a_im, B, S, ts):
    ns = S // ts
    T = B * S

    def kern(u_ref, bre_ref, bim_ref, cre_ref, cim_ref, d_ref, are_ref, aim_ref,
             sre_ref, sim_ref, y_ref, car_re, car_im, bu_re, bu_im):
        s = pl.program_id(1)

        @pl.when(s == 0)
        def _reset():
            car_re[...] = jnp.zeros(car_re.shape, F32)
            car_im[...] = jnp.zeros(car_im.shape, F32)

        u = u_ref[...]
        ub = u.astype(BF16)
        for j in range(N_GB):
            uj, bj, sj = ub[:, _gb_u(j)], _gb_u(j), _gb_s(j)
            bu_re[:, sj] = _dot(uj, bre_ref[bj, :])
            bu_im[:, sj] = _dot(uj, bim_ref[bj, :])
        ar, ai = are_ref[...], aim_ref[...]

        def step(t, carry):
            xr, xi = carry
            row = pl.ds(t, 1)
            nr = ar * xr - ai * xi + bu_re[row, :]
            ni = ar * xi + ai * xr + bu_im[row, :]
            sre_ref[row, :] = nr
            sim_ref[row, :] = ni
            return nr, ni

        xr, xi = lax.fori_loop(0, ts, step, (car_re[0:1, :], car_im[0:1, :]))
        car_re[0:1, :] = xr
        car_im[0:1, :] = xi
        y = [_dot(sre_ref[:, _gb_s(j)].astype(BF16), cre_ref[_gb_s(j), :])
             - _dot(sim_ref[:, _gb_s(j)].astype(BF16), cim_ref[_gb_s(j), :]) for j in range(N_GB)]
        y_ref[...] = jnp.concatenate(y, axis=1) + d_ref[...] * u

    row = lambda b, s: (b * ns + s, 0)
    whole = lambda b, s: (0, 0)
    return pl.pallas_call(
        kern, name="ssm_fwd", grid=(B, ns),
        in_specs=[pl.BlockSpec((ts, D_SSM), row),
                  pl.BlockSpec((D_SSM, GB_ST), whole), pl.BlockSpec((D_SSM, GB_ST), whole),
                  pl.BlockSpec((N_ST, GB_U), whole), pl.BlockSpec((N_ST, GB_U), whole),
                  pl.BlockSpec((1, D_SSM), whole), pl.BlockSpec((1, N_ST), whole), pl.BlockSpec((1, N_ST), whole)],
        out_specs=[pl.BlockSpec((ts, N_ST), row), pl.BlockSpec((ts, N_ST), row), pl.BlockSpec((ts, D_SSM), row)],
        out_shape=[jax.ShapeDtypeStruct((T, N_ST), F32), jax.ShapeDtypeStruct((T, N_ST), F32),
                   jax.ShapeDtypeStruct((T, D_SSM), F32)],
        scratch_shapes=[pltpu.VMEM((8, N_ST), F32), pltpu.VMEM((8, N_ST), F32),
                        pltpu.VMEM((ts, N_ST), F32), pltpu.VMEM((ts, N_ST), F32)],
        compiler_params=_params(("arbitrary", "arbitrary")),
    )(proj, b_re, b_im, c_re, c_im, d_vec, a_re, a_im)


def ssm_bwd(dy, proj, s_re, s_im, b_re, b_im, c_re, c_im, d_vec, a_re, a_im, B, S, ts):
    ns = S // ts
    T = B * S
    t8 = ts // 8

    def kern(dy_ref, u_ref, sre_ref, sim_ref, pre_ref, pim_ref, bre_ref, bim_ref, cre_ref, cim_ref,
             d_ref, are_ref, aim_ref,
             du_ref, dcre_ref, dcim_ref, dbre_ref, dbim_ref, dare_ref, daim_ref, dd_ref,
             car_re, car_im, g_re, g_im):
        b, s = pl.program_id(0), pl.program_id(1)
        first = jnp.logical_and(b == 0, s == 0)

        @pl.when(s == 0)
        def _reset():
            car_re[...] = jnp.zeros(car_re.shape, F32)
            car_im[...] = jnp.zeros(car_im.shape, F32)

        dyv = dy_ref[...]
        dyb = dyv.astype(BF16)
        u = u_ref[...]
        for j in range(N_GB):
            g_re[:, _gb_s(j)] = _dot_nt(dyb[:, _gb_u(j)], cre_ref[_gb_s(j), :])
            g_im[:, _gb_s(j)] = -_dot_nt(dyb[:, _gb_u(j)], cim_ref[_gb_s(j), :])
        ar, ai = are_ref[...], aim_ref[...]

        def step(i, carry):
            gr_next, gi_next = carry
            row = pl.ds(ts - 1 - i, 1)
            gr = g_re[row, :] + ar * gr_next + ai * gi_next
            gi = g_im[row, :] + ar * gi_next - ai * gr_next
            g_re[row, :] = gr
            g_im[row, :] = gi
            return gr, gi

        gr0, gi0 = lax.fori_loop(0, ts, step, (car_re[0:1, :], car_im[0:1, :]))
        car_re[0:1, :] = gr0
        car_im[0:1, :] = gi0

        gr, gi = g_re[...], g_im[...]
        sr, si = sre_ref[...], sim_ref[...]
        has_prev = (s < ns - 1).astype(F32)
        row_id = lax.broadcasted_iota(jnp.int32, (ts, N_ST), 0)
        spr = jnp.where(row_id == 0, pre_ref[7:8, :] * has_prev, pltpu.roll(sr, 1, axis=0))
        spi = jnp.where(row_id == 0, pim_ref[7:8, :] * has_prev, pltpu.roll(si, 1, axis=0))
        grb, gib, ub = gr.astype(BF16), gi.astype(BF16), u.astype(BF16)
        srb, sib = sr.astype(BF16), si.astype(BF16)
        cat0 = lambda f: jnp.concatenate([f(j) for j in range(N_GB)], axis=0)
        cat1 = lambda f: jnp.concatenate([f(j) for j in range(N_GB)], axis=1)
        parts = [
            (dcre_ref, cat0(lambda j: _dot_tn(srb[:, _gb_s(j)], dyb[:, _gb_u(j)]))),
            (dcim_ref, cat0(lambda j: -_dot_tn(sib[:, _gb_s(j)], dyb[:, _gb_u(j)]))),
            (dbre_ref, cat0(lambda j: _dot_tn(ub[:, _gb_u(j)], grb[:, _gb_s(j)]))),
            (dbim_ref, cat0(lambda j: _dot_tn(ub[:, _gb_u(j)], gib[:, _gb_s(j)]))),
            (dare_ref, _colsum(gr * spr + gi * spi)),
            (daim_ref, _colsum(gi * spr - gr * spi)),
            (dd_ref, _colsum(dyv * u)),
        ]
        du = cat1(lambda j: _dot_nt(grb[:, _gb_s(j)], bre_ref[_gb_u(j), :])
                  + _dot_nt(gib[:, _gb_s(j)], bim_ref[_gb_u(j), :]))
        du_ref[...] = du + d_ref[...] * dyv

        @pl.when(first)
        def _set():
            for ref, val in parts:
                ref[...] = val

        @pl.when(jnp.logical_not(first))
        def _add():
            for ref, val in parts:
                ref[...] = ref[...] + val

    row = lambda b, s: (b * ns + (ns - 1 - s), 0)
    prev = lambda b, s: (jnp.maximum((b * ns + (ns - 1 - s)) * t8 - 1, 0), 0)
    whole = lambda b, s: (0, 0)
    acc_shapes = [(N_ST, GB_U), (N_ST, GB_U), (D_SSM, GB_ST), (D_SSM, GB_ST), (1, N_ST), (1, N_ST), (1, D_SSM)]
    return pl.pallas_call(
        kern, name="ssm_bwd", grid=(B, ns),
        in_specs=[pl.BlockSpec((ts, D_SSM), row), pl.BlockSpec((ts, D_SSM), row),
                  pl.BlockSpec((ts, N_ST), row), pl.BlockSpec((ts, N_ST), row),
                  pl.BlockSpec((8, N_ST), prev), pl.BlockSpec((8, N_ST), prev),
                  pl.BlockSpec((D_SSM, GB_ST), whole), pl.BlockSpec((D_SSM, GB_ST), whole),
                  pl.BlockSpec((N_ST, GB_U), whole), pl.BlockSpec((N_ST, GB_U), whole),
                  pl.BlockSpec((1, D_SSM), whole), pl.BlockSpec((1, N_ST), whole), pl.BlockSpec((1, N_ST), whole)],
        out_specs=[pl.BlockSpec((ts, D_SSM), row)] + [pl.BlockSpec(sh, whole) for sh in acc_shapes],
        out_shape=[jax.ShapeDtypeStruct((T, D_SSM), F32)] + [jax.ShapeDtypeStruct(sh, F32) for sh in acc_shapes],
        scratch_shapes=[pltpu.VMEM((8, N_ST), F32), pltpu.VMEM((8, N_ST), F32),
                        pltpu.VMEM((ts, N_ST), F32), pltpu.VMEM((ts, N_ST), F32)],
        compiler_params=_params(("arbitrary", "arbitrary")),
    )(dy, proj, s_re, s_im, s_re, s_im, b_re, b_im, c_re, c_im, d_vec, a_re, a_im)


def modulation_fwd(c_all, ada_w, ada_b, fada_w, fada_b):
    def kern(c_ref, w_ref, b_ref, fw_ref, fb_ref, cond_ref, mod_ref, fmod_ref):
        cv = c_ref[...]
        cond = (cv * _sigmoid(cv)).astype(BF16)
        cond_ref[...] = cond
        mod_ref[...] = _dot(cond, w_ref[...].astype(BF16)) + b_ref[...]
        fmod_ref[...] = _dot(cond, fw_ref[...].astype(BF16)) + fb_ref[...]

    n = c_all.shape[0]
    return pl.pallas_call(
        kern, name="modulation_fwd",
        out_shape=[jax.ShapeDtypeStruct(c_all.shape, BF16), jax.ShapeDtypeStruct((n, ada_w.shape[1]), F32),
                   jax.ShapeDtypeStruct((n, fada_w.shape[1]), F32)],
        compiler_params=_params(),
    )(c_all, ada_w, ada_b, fada_w, fada_b)


def modulation_bwd(cond_all, dmod, dfmod, dmod_all):
    def kern(c_ref, dm_ref, dfm_ref, all_ref, gw_ref, gfw_ref, gb_ref):
        cond = c_ref[...]
        gw_ref[...] = _dot_tn(cond, dm_ref[...].astype(BF16))
        gfw_ref[...] = _dot_tn(cond, dfm_ref[...].astype(BF16))
        gb_ref[...] = _colsum(all_ref[...].astype(F32))

    d = cond_all.shape[1]
    return pl.pallas_call(
        kern, name="modulation_bwd",
        out_shape=[jax.ShapeDtypeStruct((d, dmod.shape[1]), F32), jax.ShapeDtypeStruct((d, dfmod.shape[1]), F32),
                   jax.ShapeDtypeStruct((1, dmod_all.shape[1]), F32)],
        compiler_params=_params(),
    )(cond_all, dmod, dfmod, dmod_all)


def _adamw_update(p_ref, w_ref, m_ref, v_ref, g_o, d_o, m_o, v_o):
    g = p_ref[0].astype(F32)
    for i in range(1, p_ref.shape[0]):
        g = g + p_ref[i].astype(F32)
    m_new = ADAM_B1 * m_ref[...] + (1.0 - ADAM_B1) * g
    v_new = ADAM_B2 * v_ref[...] + (1.0 - ADAM_B2) * (g * g)
    m_hat = m_new / (1.0 - ADAM_B1 ** ADAM_STEP)
    v_hat = v_new / (1.0 - ADAM_B2 ** ADAM_STEP)
    g_o[...] = g
    d_o[...] = -ADAM_LR * (m_hat / (jnp.sqrt(v_hat) + ADAM_EPS) + ADAM_WD * w_ref[...])
    m_o[...] = m_new
    v_o[...] = v_new


def adamw_many(name, items):
    k = len(items)

    def kern(*refs):
        ins, outs = refs[:4 * k], refs[4 * k:]
        for i in range(k):
            _adamw_update(*ins[4 * i:4 * i + 4], *outs[4 * i:4 * i + 4])

    flat = [a for item in items for a in item]
    out_shape = [jax.ShapeDtypeStruct(item[1].shape, F32) for item in items for _ in range(4)]
    outs = pl.pallas_call(kern, name=name, out_shape=out_shape, compiler_params=_params())(*flat)
    return [tuple(outs[4 * i:4 * i + 4]) for i in range(k)]


def adamw(name, parts, w, m, v):
    n, R, C = parts.shape
    tr = R
    while n * tr * C * 4 > 4 * 1024 * 1024 and tr % 32 == 0:
        tr //= 2

    def kern(p_ref, w_ref, m_ref, v_ref, g_o, d_o, m_o, v_o):
        _adamw_update(p_ref, w_ref, m_ref, v_ref, g_o, d_o, m_o, v_o)

    blk = pl.BlockSpec((tr, C), lambda i: (i, 0))
    shp = jax.ShapeDtypeStruct((R, C), F32)
    return pl.pallas_call(
        kern, name=name, grid=(R // tr,),
        in_specs=[pl.BlockSpec((n, tr, C), lambda i: (0, i, 0)), blk, blk, blk],
        out_specs=[blk] * 4, out_shape=[shp] * 4,
        compiler_params=_params(("parallel",)),
    )(parts, w, m, v)


def _cols_from_gathered(g):
    return jnp.transpose(g, (1, 0, 2)).reshape(g.shape[1], N_DEV * g.shape[2])


def _cols_to_shards(w):
    k, n8 = w.shape
    return jnp.transpose(w.reshape(k, N_DEV, n8 // N_DEV), (1, 0, 2))


def _pad_w_in(w):
    z = functools.partial(jnp.zeros, dtype=w.dtype)
    k = w.shape[0]
    cut = D_SSM + Q_LORA + KV_LORA
    return jnp.concatenate([w[:, :cut], z((k, 64)), w[:, cut:], z((k, 32))], axis=1)


def _unpad_w_in(w):
    cut = D_SSM + Q_LORA + KV_LORA
    return jnp.concatenate([w[:, :cut], w[:, cut + 64:cut + 96]], axis=1)


def _pad_w_uq(w):
    k = w.shape[0]
    w3 = w.reshape(k, N_HEADS, QK_NOPE + QK_ROPE)
    return jnp.pad(w3, ((0, 0), (0, 0), (0, HEAD_PAD - QK_NOPE - QK_ROPE))).reshape(k, QK_W)


def _unpad_w_uq(w):
    k = w.shape[0]
    return w.reshape(k, N_HEADS, HEAD_PAD)[:, :, :QK_NOPE + QK_ROPE].reshape(k, N_HEADS * (QK_NOPE + QK_ROPE))


def _pad_w_ukv(w):
    k = w.shape[0]
    w3 = w.reshape(k, N_HEADS, QK_NOPE + V_HEAD)
    kpart = jnp.pad(w3[:, :, :QK_NOPE], ((0, 0), (0, 0), (0, HEAD_PAD - QK_NOPE))).reshape(k, QK_W)
    vpart = w3[:, :, QK_NOPE:].reshape(k, V_W)
    return jnp.concatenate([kpart, vpart], axis=1)


def _unpad_w_ukv(w):
    k = w.shape[0]
    kpart = w[:, :QK_W].reshape(k, N_HEADS, HEAD_PAD)[:, :, :QK_NOPE]
    vpart = w[:, QK_W:].reshape(k, N_HEADS, V_HEAD)
    return jnp.concatenate([kpart, vpart], axis=2).reshape(k, N_HEADS * (QK_NOPE + V_HEAD))


def _block_diag(blocks):
    g, r, c = blocks.shape
    b4 = blocks.reshape(N_GB, GB, r, c)
    keep = jnp.eye(GB, dtype=bool)[None, :, None, :, None]
    return jnp.where(keep, b4[:, :, :, None, :], 0).reshape(g * r, GB * c)


def _block_diag_take(stacked, r, c):
    d5 = stacked.reshape(N_GB, GB, r, GB, c)
    idx = jnp.arange(GB)
    return jnp.transpose(d5[:, idx, :, idx, :], (1, 0, 2, 3)).reshape(N_GROUPS, r, c)


SMALL = ["norm1_g", "ssm_lambda_re", "ssm_lambda_im", "ssm_b_re", "ssm_b_im", "ssm_c_re", "ssm_c_im",
         "ssm_d", "ssm_log_dt", "q_norm_g", "kv_norm_g", "ssm_out_g", "attn_out_g", "norm2_g", "final_norm_g"]
BIASES = ["ada_b", "final_ada_b"]


def _dense_2d(shape):
    dims = [d for d in shape[1:]] if len(shape) > 1 and shape[0] == 1 else list(shape)
    if len(dims) == 1:
        return (1, dims[0])
    return (dims[0], int(np.prod(dims[1:])))


def kernel(x, c, positions, ada_w, ada_b, norm1_g, w_in, ssm_lambda_re, ssm_lambda_im, ssm_b_re, ssm_b_im, ssm_c_re, ssm_c_im, ssm_d, ssm_log_dt, w_glu, q_norm_g, w_uq, kv_norm_g, w_ukv, ssm_out_g, attn_out_g, w_out, norm2_g, w_ff1, w_ff2, final_ada_w, final_ada_b, final_norm_g, loss_target, m_ada_w, m_ada_b, m_norm1_g, m_w_in, m_ssm_lambda_re, m_ssm_lambda_im, m_ssm_b_re, m_ssm_b_im, m_ssm_c_re, m_ssm_c_im, m_ssm_d, m_ssm_log_dt, m_w_glu, m_q_norm_g, m_w_uq, m_kv_norm_g, m_w_ukv, m_ssm_out_g, m_attn_out_g, m_w_out, m_norm2_g, m_w_ff1, m_w_ff2, m_final_ada_w, m_final_ada_b, m_final_norm_g, v_ada_w, v_ada_b, v_norm1_g, v_w_in, v_ssm_lambda_re, v_ssm_lambda_im, v_ssm_b_re, v_ssm_b_im, v_ssm_c_re, v_ssm_c_im, v_ssm_d, v_ssm_log_dt, v_w_glu, v_q_norm_g, v_w_uq, v_kv_norm_g, v_w_ukv, v_ssm_out_g, v_attn_out_g, v_w_out, v_norm2_g, v_w_ff1, v_w_ff2, v_final_ada_w, v_final_ada_b, v_final_norm_g):
    given = dict(locals())
    B, S, D = x.shape
    T = B * S
    tm = min(256, S)
    me = 4 * lax.axis_index("x") + 2 * lax.axis_index("y") + lax.axis_index("c")

    big = ["w_in", "w_glu", "w_uq", "w_ukv", "w_out", "w_ff1", "w_ff2"]
    shards = {n: given[n][0] for n in big}
    early, late = ["w_in", "w_uq", "w_ukv"], ["w_glu", "w_out", "w_ff1", "w_ff2"]
    c_g, w_in_g = all_gather("gather_first_weights", [c, shards["w_in"].astype(BF16)])
    c_all = c_g.reshape(N_DEV * B, D)
    w_in_p = _pad_w_in(_cols_from_gathered(w_in_g))

    n_mod = ada_w.shape[2]
    n_fmod = final_ada_w.shape[1]
    ada_b_sh = lax.dynamic_slice_in_dim(ada_b, me * n_mod, n_mod, axis=1)
    fada_b_sh = lax.dynamic_slice_in_dim(final_ada_b[None, :], me * n_fmod, n_fmod, axis=1)
    cond_all, mod_sh, fmod_sh = modulation_fwd(c_all, ada_w[0], ada_b_sh, final_ada_w, fada_b_sh)
    mod_g, fmod_g = all_gather("gather_modulation", [mod_sh, fmod_sh])
    mod_mine = lax.dynamic_slice_in_dim(_cols_from_gathered(mod_g), me * B, B, axis=0)
    fmod_mine = lax.dynamic_slice_in_dim(_cols_from_gathered(fmod_g), me * B, B, axis=0)
    shift1, scale1, gate1, shift2, scale2, gate2 = [mod_mine[:, None, i * D:(i + 1) * D] for i in range(6)]
    fshift, fscale = fmod_mine[:, None, :D], fmod_mine[:, None, D:]

    x2d = x.reshape(T, D)
    tgt2d = loss_target.reshape(T, D)
    pos = positions.astype(F32).reshape(T, 1)
    invf, m_lo, m_hi = _rope_consts()
    rope_mask = m_lo + m_hi

    def fwd_in(rows, bv, vecs, res):
        (xv,), (sc, sh), (g1,), (w,) = rows, bv, vecs, res
        r, xh = _rms_stats(xv)
        h = xh * g1 * (1.0 + sc) + sh
        return [_dot(h.astype(BF16), w[...])], [], []

    qkv_gather = Exchange([shards["w_uq"].astype(BF16), shards["w_ukv"].astype(BF16)], [GATHER, GATHER])
    (proj,), _, _, (w_uq_g, w_ukv_g) = row_call(
        "fwd_in", fwd_in, B, S, tm, [x2d], [scale1, shift1], [norm1_g], [w_in_p], [(IN_PAD, F32)], [], [],
        side=qkv_gather)
    w_uq_p = _pad_w_uq(_cols_from_gathered(w_uq_g))
    w_ukv_p = _pad_w_ukv(_cols_from_gathered(w_ukv_g))

    def fwd_qkv(rows, bv, vecs, res):
        (pr, ps), (gq, gkv, fr, lo, hi), (wq, wkv) = rows, vecs, res
        cs, s_lo, s_hi = _rope_tables(ps, fr, lo, hi)
        _, qh = _rms_stats(pr[:, D_SSM:D_SSM + Q_LORA])
        _, kvh = _rms_stats(pr[:, D_SSM + Q_LORA:D_SSM + Q_LORA + KV_LORA])
        qf = _dot((qh * gq).astype(BF16), wq[...])
        kvf = _dot((kvh * gkv).astype(BF16), wkv[...])
        k_rope = _rope_tile(pr[:, IN_PAD - HEAD_PAD:], cs, s_lo, s_hi, 1.0)
        q_t, k_t = [], []
        for h in range(N_HEADS):
            hs = slice(h * HEAD_PAD, (h + 1) * HEAD_PAD)
            q_t.append(_rope_tile(qf[:, hs], cs, s_lo, s_hi, 1.0))
            k_t.append(kvf[:, hs] + k_rope)
        kf, vf = jnp.concatenate(k_t, axis=1), kvf[:, QK_W:]
        return [jnp.concatenate(q_t, axis=1), kf, vf, t_val(kf), t_val(vf)[None], cs, s_lo, s_hi], [], []

    ns = S // tm
    col_tile = lambda b, s: (0, b * ns + s)
    t_out = lambda w: ((w, T), BF16, (w, tm), col_tile)
    t_val = lambda v: v.astype(BF16).T
    (q_r, k_r, v_r, k_t, v_t, rope_cs, rope_lo, rope_hi), _, _ = row_call(
        "fwd_qkv", fwd_qkv, B, S, tm, [proj, pos], [], [q_norm_g, kv_norm_g, invf, m_lo, m_hi],
        [w_uq_p, w_ukv_p],
        [(QK_W, BF16), (QK_W, BF16), (V_W, BF16), ((QK_W, T), BF16, (QK_W, tm), col_tile),
         ((B, V_W, S), BF16, (1, V_W, tm), lambda b, s: (b, 0, s)), (HEAD_PAD, F32), (HEAD_PAD, F32),
         (HEAD_PAD, F32)], [], [])
    late_gather = Exchange([shards[n].astype(BF16) for n in late], [GATHER] * len(late))
    y_attn, lse, *late_g = attention_fwd(q_r.reshape(B, S, QK_W), k_r.reshape(B, S, QK_W), v_t, B, S, tm,
                                         late_gather)
    y_attn = y_attn.reshape(T, V_W)
    lse = jnp.transpose(lse.reshape(B, N_HEADS, 1, S), (1, 2, 0, 3)).reshape(N_HEADS, 1, T)
    gw = dict(zip(late, late_g))
    w_glu_f = _cols_from_gathered(gw["w_glu"])
    w_out_f = gw["w_out"].reshape(-1, D)
    w_ff1_f = _cols_from_gathered(gw["w_ff1"])
    w_ff2_f = gw["w_ff2"].reshape(-1, D)

    rep = lambda a: jnp.repeat(a, GROUP, axis=0)
    lam_rep = [rep(ssm_lambda_re[0]), rep(ssm_lambda_im[0]),
               rep(jnp.broadcast_to(ssm_log_dt[0][:, None], (N_GROUPS, N_STATE)))]
    bt = lambda a: jnp.transpose(a, (0, 2, 1)).reshape(D_SSM, N_STATE)
    reps = lam_rep + [bt(ssm_b_re[0]), bt(ssm_b_im[0])]
    a_re_rep, a_im_rep, bb_re, bb_im = ssm_params_fwd(reps)
    a_re = a_re_rep.reshape(N_GROUPS, GROUP, N_STATE)[:, 0, :].reshape(1, N_ST)
    a_im = a_im_rep.reshape(N_GROUPS, GROUP, N_STATE)[:, 0, :].reshape(1, N_ST)
    bbd_re = _block_diag(bb_re.reshape(N_GROUPS, GROUP, N_STATE)).astype(BF16)
    bbd_im = _block_diag(bb_im.reshape(N_GROUPS, GROUP, N_STATE)).astype(BF16)
    cbd_re = _block_diag(jnp.transpose(ssm_c_re[0], (0, 2, 1))).astype(BF16)
    cbd_im = _block_diag(jnp.transpose(ssm_c_im[0], (0, 2, 1))).astype(BF16)
    d_vec = ssm_d.reshape(1, D_SSM)
    st_re, st_im, y_pre = ssm_fwd(proj, bbd_re, bbd_im, cbd_re, cbd_im, d_vec, a_re, a_im, B, S, tm)

    def fwd_mix(rows, bv, vecs, res):
        (yp, ya, xv), (g1v,), (gso, gao), (wg, wo) = rows, bv, vecs, res
        z = _dot(_gelu(yp).astype(BF16), wg[...])
        y_ssm = z[:, :D_SSM] * _sigmoid(z[:, D_SSM:])
        _, sh = _rms_stats(y_ssm)
        _, ah = _rms_stats(ya)
        o = _dot((sh * gso).astype(BF16), wo[0:D_SSM, :]) + _dot((ah * gao).astype(BF16), wo[D_SSM:, :])
        return [z, o, xv + g1v * o], [], []

    (z, o, x2), _, _ = row_call("fwd_mix", fwd_mix, B, S, tm, [y_pre, y_attn, x2d], [gate1],
                                [ssm_out_g, attn_out_g], [w_glu_f, w_out_f],
                                [(2 * D_SSM, F32), (D, F32), (D, F32)], [], [])

    def final_out(x3, gf, fsc, fsh):
        r3, xh3 = _rms_stats(x3)
        n3 = xh3 * gf
        return r3, xh3, n3, n3 * (1.0 + fsc) + fsh

    def mlp_hidden(xv, g2, sc, sh, w1):
        r2, xh2 = _rms_stats(xv)
        n2 = xh2 * g2
        h2 = n2 * (1.0 + sc) + sh
        return r2, xh2, n2, h2, _dot(h2.astype(BF16), w1[...])

    def fwd_mlp(rows, bv, vecs, res):
        (xv, tg), (sc, sh, g2v, fsc, fsh), (g2, gf), (w1, w2) = rows, bv, vecs, res
        _, _, _, _, a = mlp_hidden(xv, g2, sc, sh, w1)
        ra = jnp.maximum(a, 0.0)
        ff = _dot((ra * ra).astype(BF16), w2[...])
        x3 = xv + g2v * ff
        _, _, _, out = final_out(x3, gf, fsc, fsh)
        err = out - tg
        loss = 0.5 * jnp.sum(jnp.mean(err * err, axis=-1, keepdims=True), axis=0, keepdims=True)
        return [ff, x3], [], [jnp.broadcast_to(loss, (1, 128))]

    fnorm_g = final_norm_g[None, :]
    (ff, x3), _, (loss_acc,) = row_call(
        "fwd_mlp", fwd_mlp, B, S, tm, [x2, tgt2d], [scale2, shift2, gate2, fscale, fshift], [norm2_g, fnorm_g],
        [w_ff1_f, w_ff2_f], [(D, F32), (D, F32)], [], [(1, 128)])
    loss = lax.psum(loss_acc[0, 0], ("x", "y", "c"))

    def bwd_mlp(rows, bv, vecs, res):
        (x3v, tg, x2v, ffv), (sc, sh, g2v, fsc, fsh), (g2, gf), (w1, w2) = rows, bv, vecs, res
        r3, xh3, n3, out = final_out(x3v, gf, fsc, fsh)
        dout = (out - tg) * (1.0 / D)
        dn3 = dout * (1.0 + fsc)
        dx3 = _rms_bwd(dn3 * gf, xh3, r3)
        dff = dx3 * g2v
        r2, xh2, n2, h2, a = mlp_hidden(x2v, g2, sc, sh, w1)
        ra = jnp.maximum(a, 0.0)
        dffb = dff.astype(BF16)
        da = _dot_nt(dffb, w2[...]) * (2.0 * ra)
        dab = da.astype(BF16)
        dh2 = _dot_nt(dab, w1[...])
        dn2 = dh2 * (1.0 + sc)
        dx2 = dx3 + _rms_bwd(dn2 * g2, xh2, r2)
        return ([dx2, t_val(h2), t_val(ra * ra), dab, dffb],
                [_colsum(dout), _colsum(dout * n3), _colsum(dx3 * ffv), _colsum(dh2), _colsum(dh2 * n2)],
                [_colsum(dn3 * xh3), _colsum(dn2 * xh2)])

    ((dx2, h2b, fb, dab, dffb), (d_fshift, d_fscale, d_gate2, d_shift2, d_scale2), (d_gf, d_g2)) = row_call(
        "bwd_mlp", bwd_mlp, B, S, tm, [x3, tgt2d, x2, ff], [scale2, shift2, gate2, fscale, fshift],
        [norm2_g, fnorm_g], [w_ff1_f, w_ff2_f],
        [(D, F32), t_out(D), t_out(4 * D), (4 * D, BF16), (D, BF16)], [D] * 5, [(1, D), (1, D)])
    g_w_ff1 = matmul_nn("grad_w_ff1", h2b, dab)
    g_w_ff2 = matmul_nn("grad_w_ff2", fb, dffb)

    def bwd_mix(rows, bv, vecs, res):
        (dxv, ov, zv, ya, yp), (g1v,), (gso, gao), (wo, wg) = rows, bv, vecs, res
        do = (dxv * g1v).astype(BF16)
        dyn = _dot_nt(do, wo[...])
        z1, sg = zv[:, :D_SSM], _sigmoid(zv[:, D_SSM:])
        y_ssm = z1 * sg
        rs, sh = _rms_stats(y_ssm)
        ra, ah = _rms_stats(ya)
        dyn_s, dyn_a = dyn[:, :D_SSM], dyn[:, D_SSM:]
        dy_ssm = _rms_bwd(dyn_s * gso, sh, rs)
        dy_attn = _rms_bwd(dyn_a * gao, ah, ra)
        dz = jnp.concatenate([dy_ssm * sg, dy_ssm * z1 * sg * (1.0 - sg)], axis=1).astype(BF16)
        dy_pre = _dot_nt(dz, wg[...]) * _gelu_grad(yp)
        yn = jnp.concatenate([sh * gso, ah * gao], axis=1)
        delta = jnp.sum((dy_attn * ya).T.reshape(N_HEADS, V_HEAD, tm), axis=1, keepdims=True)
        return ([do, t_val(yn), dz, t_val(_gelu(yp)), dy_attn, dy_pre, delta], [_colsum(dxv * ov)],
                [_colsum(dyn_s * sh), _colsum(dyn_a * ah)])

    ((dob, ynb, dzb, ygb, dy_attn, dy_pre, delta), (d_gate1,), (d_gso, d_gao)) = row_call(
        "bwd_mix", bwd_mix, B, S, tm, [dx2, o, z, y_attn, y_pre], [gate1], [ssm_out_g, attn_out_g],
        [w_out_f, w_glu_f],
        [(D, BF16), t_out(D), (2 * D_SSM, BF16), t_out(D_SSM), (V_W, F32), (D_SSM, F32),
         ((N_HEADS, 1, T), F32, (N_HEADS, 1, tm), lambda b, s: (0, 0, b * ns + s))],
        [D], [(1, D_SSM), (1, V_W)])
    g_w_out = matmul_nn("grad_w_out", ynb, dob)
    g_w_glu = matmul_nn("grad_w_glu", ygb, dzb)

    grads_a = Exchange([g_w_glu, g_w_out.reshape(N_DEV, -1, D), g_w_ff1, g_w_ff2.reshape(N_DEV, -1, D)],
                       [w_glu.shape[2], None, w_ff1.shape[2], None])
    dq_t, dk_r, dv_r, *parts_a = attention_bwd(q_r, k_r, k_t, v_r, dy_attn, lse, delta, B, S, tm, grads_a)
    parts = dict(zip(late, parts_a))
    dq_t = dq_t.reshape(B, QK_W, S)

    def bwd_qkv(rows, bv, vecs, res):
        (dqt, dkv, dvv, pr, cs, s_lo, s_hi), (gq, gkv, rmask), (wq, wkv) = rows, vecs, res
        dqv = dqt[0].T
        dq_t = []
        dk_rope = jnp.zeros((dkv.shape[0], HEAD_PAD), F32)
        for h in range(N_HEADS):
            hs = slice(h * HEAD_PAD, (h + 1) * HEAD_PAD)
            dq_t.append(_rope_tile(dqv[:, hs], cs, s_lo, s_hi, -1.0))
            dk_rope = dk_rope + dkv[:, hs]
        dk_rope = _rope_tile(dk_rope * rmask, cs, s_lo, s_hi, -1.0)
        dqb = jnp.concatenate(dq_t, axis=1).astype(BF16)
        dkvb = jnp.concatenate([dkv, dvv], axis=1).astype(BF16)
        rq, qh = _rms_stats(pr[:, D_SSM:D_SSM + Q_LORA])
        rk, kvh = _rms_stats(pr[:, D_SSM + Q_LORA:D_SSM + Q_LORA + KV_LORA])
        dqn = _dot_nt(dqb, wq[...])
        dkvn = _dot_nt(dkvb, wkv[...])
        dpa = jnp.concatenate([_rms_bwd(dqn * gq, qh, rq), _rms_bwd(dkvn * gkv, kvh, rk), dk_rope], axis=1)
        return [dpa, t_val(qh * gq), t_val(kvh * gkv), dqb, dkvb], [], [_colsum(dqn * qh), _colsum(dkvn * kvh)]

    ((dpa, qnb, kvnb, dqb, dkvb), _, (d_gq, d_gkv)) = row_call(
        "bwd_qkv", bwd_qkv, B, S, tm,
        [(dq_t, (1, QK_W, tm), lambda b, s: (b, 0, s)), dk_r, dv_r, proj, rope_cs, rope_lo, rope_hi], [],
        [q_norm_g, kv_norm_g, rope_mask], [w_uq_p, w_ukv_p],
        [(Q_LORA + KV_LORA + HEAD_PAD, F32), t_out(Q_LORA), t_out(KV_LORA), (QK_W, BF16), (QK_W + V_W, BF16)],
        [], [(1, Q_LORA), (1, KV_LORA)])
    g_w_uq = _unpad_w_uq(matmul_nn("grad_w_uq", qnb, dqb))
    g_w_ukv = _unpad_w_ukv(matmul_nn("grad_w_ukv", kvnb, dkvb))

    du, dc_re_d, dc_im_d, db_re_d, db_im_d, da_re, da_im, d_d = ssm_bwd(
        dy_pre, proj, st_re, st_im, bbd_re, bbd_im, cbd_re, cbd_im, d_vec, a_re, a_im, B, S, tm)
    place = lambda a: jnp.zeros((N_GROUPS, GROUP, N_STATE), F32).at[:, 0, :].set(
        a.reshape(N_GROUPS, N_STATE)).reshape(D_SSM, N_STATE)
    cots = [place(da_re), place(da_im),
            _block_diag_take(db_re_d, GROUP, N_STATE).reshape(D_SSM, N_STATE),
            _block_diag_take(db_im_d, GROUP, N_STATE).reshape(D_SSM, N_STATE)]
    g_lam_re, g_lam_im, g_log_dt, g_bt_re, g_bt_im = ssm_params_bwd(reps, cots)
    unbt = lambda a: jnp.transpose(a.reshape(N_GROUPS, GROUP, N_STATE), (0, 2, 1))
    g_c_re = jnp.transpose(_block_diag_take(dc_re_d, N_STATE, GROUP), (0, 2, 1))
    g_c_im = jnp.transpose(_block_diag_take(dc_im_d, N_STATE, GROUP), (0, 2, 1))

    def bwd_in(rows, bv, vecs, res):
        (duv, dpav, xv, dx2v), (sc, sh), (g1,), (w,) = rows, bv, vecs, res
        dproj = jnp.concatenate([duv, dpav], axis=1).astype(BF16)
        dh = _dot_nt(dproj, w[...])
        r, xh = _rms_stats(xv)
        n1 = xh * g1
        dn1 = dh * (1.0 + sc)
        dx = dx2v + _rms_bwd(dn1 * g1, xh, r)
        return [dx, t_val(n1 * (1.0 + sc) + sh), dproj], [_colsum(dh), _colsum(dh * n1)], [_colsum(dn1 * xh)]

    small_part = {
        "ssm_lambda_re": g_lam_re,
        "ssm_lambda_im": g_lam_im, "ssm_b_re": unbt(g_bt_re), "ssm_b_im": unbt(g_bt_im), "ssm_c_re": g_c_re,
        "ssm_c_im": g_c_im, "ssm_d": d_d, "ssm_log_dt": g_log_dt, "q_norm_g": d_gq, "kv_norm_g": d_gkv,
        "ssm_out_g": d_gso, "attn_out_g": d_gao, "norm2_g": d_g2, "final_norm_g": d_gf}
    dense_bf16 = lambda n, g: g.reshape(_dense_2d(given[n].shape)).astype(BF16)
    ready = [n for n in SMALL if n in small_part]
    grads_b = Exchange([dense_bf16(n, small_part[n]) for n in ready] + [_cols_to_shards(g_w_uq), g_w_ukv],
                       [GATHER] * len(ready) + [None, w_ukv.shape[2]])
    ((grad_x, h1b, dprojb), (d_shift1, d_scale1), (d_g1,)) = row_call(
        "bwd_in", bwd_in, B, S, tm, [du, dpa, x2d, dx2], [scale1, shift1], [norm1_g], [w_in_p],
        [(D, F32), t_out(D), (IN_PAD, BF16)], [D, D], [(1, D)])
    g_w_in_p, *side_b = matmul_nn("grad_w_in", h1b, dprojb, side=grads_b)
    small_parts = dict(zip(ready, side_b[:len(ready)]))
    parts["w_uq"], parts["w_ukv"] = side_b[len(ready):]
    g_w_in = _unpad_w_in(g_w_in_p)

    dmod = jnp.concatenate([d_shift1, d_scale1, d_gate1, d_shift2, d_scale2, d_gate2, d_fshift, d_fscale],
                           axis=2).reshape(B, 8 * D)
    dmod_g, small_parts["norm1_g"], parts["w_in"] = exchange(
        "exchange_last_grads", [dmod.astype(BF16), dense_bf16("norm1_g", d_g1), _cols_to_shards(g_w_in)],
        [GATHER, GATHER, None])
    small_g = [small_parts[n] for n in SMALL]
    dmod_all = dmod_g.reshape(N_DEV * B, 8 * D)
    dm_sh = lax.dynamic_slice_in_dim(dmod_all[:, :6 * D], me * n_mod, n_mod, axis=1)
    dfm_sh = lax.dynamic_slice_in_dim(dmod_all[:, 6 * D:], me * n_fmod, n_fmod, axis=1)
    g_ada_w, g_fada_w, g_biases = modulation_bwd(cond_all, dm_sh, dfm_sh, dmod_all)
    parts["ada_w"] = g_ada_w[None]
    parts["final_ada_w"] = g_fada_w[None]

    res_g, res_d, res_m, res_v = {}, {}, {}, {}
    for n in ["ada_w"] + big + ["final_ada_w"]:
        w_full = given[n]
        w2 = w_full.reshape(w_full.shape[-2], w_full.shape[-1])
        out = adamw("adamw_" + n, parts[n], w2, given["m_" + n].reshape(w2.shape), given["v_" + n].reshape(w2.shape))
        res_g[n], res_d[n], res_m[n], res_v[n] = [a.reshape(w_full.shape) for a in out]
    bias_g = {"ada_b": g_biases[None, :, :6 * D], "final_ada_b": g_biases[None, :, 6 * D:]}
    names = SMALL + BIASES
    items = []
    for n, g_parts in zip(names, small_g + [bias_g[b] for b in BIASES]):
        dense = _dense_2d(given[n].shape)
        items.append((g_parts, given[n].reshape(dense), given["m_" + n].reshape(dense), given["v_" + n].reshape(dense)))
    for n, out in zip(names, adamw_many("adamw_small", items)):
        res_g[n], res_d[n], res_m[n], res_v[n] = [a.reshape(given[n].shape) for a in out]

    order = ["ada_w", "ada_b", "norm1_g", "w_in", "ssm_lambda_re", "ssm_lambda_im", "ssm_b_re", "ssm_b_im",
             "ssm_c_re", "ssm_c_im", "ssm_d", "ssm_log_dt", "w_glu", "q_norm_g", "w_uq", "kv_norm_g", "w_ukv",
             "ssm_out_g", "attn_out_g", "w_out", "norm2_g", "w_ff1", "w_ff2", "final_ada_w", "final_ada_b",
             "final_norm_g"]
    return (loss, grad_x.reshape(B, S, D), *[res_g[n] for n in order], *[res_d[n] for n in order],
            *[res_m[n] for n in order], *[res_v[n] for n in order])
```

```python
import functools
import math

import numpy as np
import jax
import jax.numpy as jnp
from jax import lax
from jax.experimental import pallas as pl
from jax.experimental.pallas import tpu as pltpu

F32 = jnp.float32
BF16 = jnp.bfloat16

N_DEV = 8
EPS = 1e-6
D_SSM = 512
N_GROUPS = 32
GROUP = 16
N_STATE = 64
N_ST = N_GROUPS * N_STATE
Q_LORA = 384
KV_LORA = 256
QK_NOPE = 64
QK_ROPE = 32
V_HEAD = 64
N_HEADS = 8
HEAD_PAD = 128
QK_W = N_HEADS * HEAD_PAD
V_W = N_HEADS * V_HEAD
IN_COLS = D_SSM + Q_LORA + KV_LORA + QK_ROPE
IN_PAD = D_SSM + Q_LORA + KV_LORA + HEAD_PAD
ROPE_BASE = 10000.0
ATTN_SCALE = (QK_NOPE + QK_ROPE) ** -0.5
NEG = -1e30

ADAM_LR = 0.001
ADAM_B1 = 0.9
ADAM_B2 = 0.999
ADAM_EPS = 1e-08
ADAM_WD = 0.01
ADAM_STEP = 10

VMEM_LIMIT = 56 * 1024 * 1024
MESH_ID = pl.DeviceIdType.MESH


def _dot(a, b):
    return jnp.dot(a, b, preferred_element_type=F32)


def _dot_nt(a, b):
    return lax.dot_general(a, b, (((1,), (1,)), ((), ())), preferred_element_type=F32)


def _dot_tn(a, b):
    return lax.dot_general(a, b, (((0,), (0,)), ((), ())), preferred_element_type=F32)


def _rms_stats(x):
    r = lax.rsqrt(jnp.mean(x * x, axis=-1, keepdims=True) + EPS)
    return r, x * r


def _rms_bwd(dy, xh, r):
    return r * (dy - xh * jnp.mean(dy * xh, axis=-1, keepdims=True))


def _sigmoid(x):
    return 1.0 / (1.0 + jnp.exp(-x))


_GELU_C = math.sqrt(2.0 / math.pi)


def _gelu(x):
    t = jnp.tanh(_GELU_C * (x + 0.044715 * x * x * x))
    return 0.5 * x * (1.0 + t)


def _gelu_grad(x):
    t = jnp.tanh(_GELU_C * (x + 0.044715 * x * x * x))
    return 0.5 * (1.0 + t) + 0.5 * x * (1.0 - t * t) * _GELU_C * (1.0 + 3.0 * 0.044715 * x * x)


def _colsum(a):
    return jnp.sum(a, axis=0, keepdims=True)


def _params(sem=None):
    return pltpu.CompilerParams(dimension_semantics=sem, vmem_limit_bytes=VMEM_LIMIT)


def _mesh_pos():
    x, y, c = lax.axis_index("x"), lax.axis_index("y"), lax.axis_index("c")
    return x, y, c, 4 * x + 2 * y + c


def _peer(x, y, c, k):
    px = 1 - x if k & 4 else x
    py = 1 - y if k & 2 else y
    pc = 1 - c if k & 1 else c
    return (px, py, pc), 4 * px + 2 * py + pc


GATHER = "gather"
ANY_SPEC = pl.BlockSpec(memory_space=pl.ANY)


class Exchange:
    def __init__(self, arrays, kinds):
        self.arrays, self.kinds, self.n = list(arrays), list(kinds), len(arrays)

    def _shard(self, i):
        a, kind = self.arrays[i], self.kinds[i]
        if kind == GATHER:
            return a.shape
        return a.shape[1:] if kind is None else (a.shape[0], kind)

    def out_shapes(self):
        return [jax.ShapeDtypeStruct((N_DEV,) + tuple(self._shard(i)), self.arrays[i].dtype) for i in range(self.n)]

    def scratch(self):
        return [pltpu.SemaphoreType.DMA((self.n, N_DEV - 1)), pltpu.SemaphoreType.DMA((self.n, N_DEV - 1)),
                pltpu.SemaphoreType.DMA((self.n,))]

    def _src(self, ins, i, j):
        kind = self.kinds[i]
        if kind == GATHER:
            return ins[i]
        if kind is None:
            return ins[i].at[j]
        return ins[i].at[:, pl.ds(pl.multiple_of(j * kind, 128), kind)]

    def _copies(self, ins, outs, sems, with_received):
        send_sems, recv_sems, loc_sems = sems
        x, y, c, me = _mesh_pos()
        local, sent, received = [], [], []
        for i in range(self.n):
            local.append(pltpu.make_async_copy(self._src(ins, i, me), outs[i].at[me], loc_sems.at[i]))
            for k in range(1, N_DEV):
                peer, pidx = _peer(x, y, c, k)
                pair = dict(send_sem=send_sems.at[i, k - 1], recv_sem=recv_sems.at[i, k - 1], device_id=peer,
                            device_id_type=MESH_ID)
                sent.append(pltpu.make_async_remote_copy(
                    src_ref=self._src(ins, i, pidx), dst_ref=outs[i].at[me], **pair))
                if with_received:
                    received.append(pltpu.make_async_remote_copy(
                        src_ref=self._src(ins, i, pidx), dst_ref=outs[i].at[pidx], **pair))
        return local, sent, received

    def start(self, ins, outs, sems):
        local, sent, _ = self._copies(ins, outs, sems, False)
        for cp in local + sent:
            cp.start()

    def wait(self, ins, outs, sems):
        local, sent, received = self._copies(ins, outs, sems, True)
        for cp in received:
            cp.wait_recv()
        for cp in sent:
            cp.wait_send()
        for cp in local:
            cp.wait()


def exchange(name, arrays, kinds):
    ex = Exchange(arrays, kinds)
    n = ex.n

    def body(*refs):
        ins, outs, sems = refs[:n], refs[n:2 * n], refs[2 * n:]
        ex.start(ins, outs, sems)
        ex.wait(ins, outs, sems)

    return pl.pallas_call(
        body, name=name, out_shape=ex.out_shapes(), in_specs=[ANY_SPEC] * n, out_specs=[ANY_SPEC] * n,
        scratch_shapes=ex.scratch())(*arrays)


def all_gather(name, arrays):
    return exchange(name, arrays, [GATHER] * len(arrays))


def row_call(name, body, B, S, tm, rows, bvecs, vecs, res, row_outs, bacc, gacc, side=None):
    ns = S // tm
    T = B * S
    n_r, n_b, n_v, n_w = len(rows), len(bvecs), len(vecs), len(res)
    n_ro, n_ba, n_ga = len(row_outs), len(bacc), len(gacc)
    n_x = side.n if side is not None else 0

    def kern(*refs):
        i = 0
        r_refs = refs[i:i + n_r]; i += n_r
        b_refs = refs[i:i + n_b]; i += n_b
        v_refs = refs[i:i + n_v]; i += n_v
        w_hbm = refs[i:i + n_w]; i += n_w
        x_in = refs[i:i + n_x]; i += n_x
        ro_refs = refs[i:i + n_ro]; i += n_ro
        ba_refs = refs[i:i + n_ba]; i += n_ba
        ga_refs = refs[i:i + n_ga]; i += n_ga
        x_out = refs[i:i + n_x]; i += n_x
        w_vmem = refs[i:i + n_w]; i += n_w
        x_sems = refs[i:]
        b = pl.program_id(0)
        s = pl.program_id(1)
        first = jnp.logical_and(b == 0, s == 0)

        if n_x:
            @pl.when(first)
            def _start_side():
                side.start(x_in, x_out, x_sems)

        if n_w:
            @pl.when(first)
            def _load_weights():
                for h, v in zip(w_hbm, w_vmem):
                    pltpu.sync_copy(h, v)

        ro, ba, ga = body([r[...] for r in r_refs], [r[0] for r in b_refs], [r[...] for r in v_refs],
                          list(w_vmem))
        for ref, val in zip(ro_refs, ro):
            ref[...] = val.astype(ref.dtype)

        def accumulate(ref, val, is_first, idx):
            @pl.when(is_first)
            def _set():
                ref[idx] = val

            @pl.when(jnp.logical_not(is_first))
            def _add():
                ref[idx] = ref[idx] + val

        for ref, val in zip(ba_refs, ba):
            accumulate(ref, val, s == 0, 0)
        for ref, val in zip(ga_refs, ga):
            accumulate(ref, val, first, Ellipsis)

        if n_x:
            @pl.when(jnp.logical_and(b == B - 1, s == ns - 1))
            def _wait_side():
                side.wait(x_in, x_out, x_sems)

    row_arrays = [r[0] if isinstance(r, tuple) else r for r in rows]
    row_in_specs = [pl.BlockSpec(r[1], r[2]) if isinstance(r, tuple)
                    else pl.BlockSpec((tm, r.shape[1]), lambda b, s: (b * ns + s, 0)) for r in rows]
    ro_shapes = [jax.ShapeDtypeStruct(tuple(o[0]), o[1]) if len(o) == 4 else jax.ShapeDtypeStruct((T, o[0]), o[1])
                 for o in row_outs]
    ro_specs = [pl.BlockSpec(o[2], o[3]) if len(o) == 4 else pl.BlockSpec((tm, o[0]), lambda b, s: (b * ns + s, 0))
                for o in row_outs]
    in_specs = (row_in_specs
                + [pl.BlockSpec((1, 1, v.shape[2]), lambda b, s: (b, 0, 0)) for v in bvecs]
                + [pl.BlockSpec(v.shape, lambda b, s, nd=v.ndim: (0,) * nd) for v in vecs]
                + [ANY_SPEC] * (n_w + n_x))
    out_shape = (ro_shapes
                 + [jax.ShapeDtypeStruct((B, 1, w), F32) for w in bacc]
                 + [jax.ShapeDtypeStruct(tuple(sh), F32) for sh in gacc]
                 + (side.out_shapes() if n_x else []))
    out_specs = (ro_specs
                 + [pl.BlockSpec((1, 1, w), lambda b, s: (b, 0, 0)) for w in bacc]
                 + [pl.BlockSpec(tuple(sh), lambda b, s, nd=len(sh): (0,) * nd) for sh in gacc]
                 + [ANY_SPEC] * n_x)
    outs = pl.pallas_call(
        kern, name=name, grid=(B, ns), in_specs=in_specs, out_specs=out_specs, out_shape=out_shape,
        scratch_shapes=[pltpu.VMEM(w.shape, w.dtype) for w in res] + (side.scratch() if n_x else []),
        compiler_params=_params(("arbitrary", "arbitrary")),
    )(*row_arrays, *bvecs, *vecs, *res, *(side.arrays if n_x else []))
    n_acc = n_ro + n_ba + n_ga
    if n_x:
        return outs[:n_ro], outs[n_ro:n_ro + n_ba], outs[n_ro + n_ba:n_acc], outs[n_acc:]
    return outs[:n_ro], outs[n_ro:n_ro + n_ba], outs[n_ro + n_ba:]


def matmul_nn(name, at, b, side=None):
    K, T = at.shape
    N = b.shape[1]
    tk = min(K, 512)
    tn = N if N <= 1280 else (1024 if N % 1024 == 0 else 512)
    assert K % tk == 0 and N % tn == 0 and T % min(T, 2048) == 0, (K, N, T)
    tt = min(T, 2048)
    nt = T // tt
    n_x = side.n if side is not None else 0
    grid = (K // tk, N // tn, nt)

    def kern(*refs):
        a_ref, b_ref = refs[:2]
        x_in = refs[2:2 + n_x]
        o_ref = refs[2 + n_x]
        x_out = refs[3 + n_x:3 + 2 * n_x]
        acc_ref = refs[3 + 2 * n_x]
        x_sems = refs[4 + 2 * n_x:]
        i, j, t = pl.program_id(0), pl.program_id(1), pl.program_id(2)
        if n_x:
            @pl.when(jnp.logical_and(i == 0, jnp.logical_and(j == 0, t == 0)))
            def _start_side():
                side.start(x_in, x_out, x_sems)

        part = _dot(a_ref[...], b_ref[...])

        @pl.when(t == 0)
        def _set():
            acc_ref[...] = part

        @pl.when(t > 0)
        def _add():
            acc_ref[...] = acc_ref[...] + part

        @pl.when(t == nt - 1)
        def _write():
            o_ref[...] = acc_ref[...].astype(o_ref.dtype)

        if n_x:
            @pl.when(jnp.logical_and(i == grid[0] - 1, jnp.logical_and(j == grid[1] - 1, t == nt - 1)))
            def _wait_side():
                side.wait(x_in, x_out, x_sems)

    outs = pl.pallas_call(
        kern, name=name, grid=grid,
        in_specs=[pl.BlockSpec((tk, tt), lambda i, j, t: (i, t)), pl.BlockSpec((tt, tn), lambda i, j, t: (t, j))]
        + [ANY_SPEC] * n_x,
        out_specs=[pl.BlockSpec((tk, tn), lambda i, j, t: (i, j))] + [ANY_SPEC] * n_x,
        out_shape=[jax.ShapeDtypeStruct((K, N), BF16)] + (side.out_shapes() if n_x else []),
        scratch_shapes=[pltpu.VMEM((tk, tn), F32)] + (side.scratch() if n_x else []),
        compiler_params=_params(("arbitrary",) * 3 if n_x else ("parallel", "parallel", "arbitrary")),
    )(at, b, *(side.arrays if n_x else []))
    return outs if n_x else outs[0]


def _rope_consts():
    inv_freq = ROPE_BASE ** (-jnp.arange(0, QK_ROPE, 2, dtype=F32) / QK_ROPE)
    zeros64 = jnp.zeros((64,), F32)
    zeros32 = jnp.zeros((32,), F32)
    zeros16 = jnp.zeros((16,), F32)
    ones16 = jnp.ones((16,), F32)
    invf = jnp.concatenate([zeros64, inv_freq, inv_freq, zeros32])[None, :]
    m_lo = jnp.concatenate([zeros64, ones16, zeros16, zeros32])[None, :]
    m_hi = jnp.concatenate([zeros64, zeros16, ones16, zeros32])[None, :]
    return invf, m_lo, m_hi


def _rope_tables(pos, invf, m_lo, m_hi):
    ang = pos * invf
    return jnp.cos(ang), jnp.sin(ang) * m_lo, jnp.sin(ang) * m_hi


def _rope_tile(t, cs, s_lo, s_hi, sign):
    up = pltpu.roll(t, HEAD_PAD - 16, axis=1)
    down = pltpu.roll(t, 16, axis=1)
    return t * cs + sign * (down * s_hi - up * s_lo)


def _heads(ref, width):
    if len(ref.shape) == 2:
        return jnp.stack([ref[:, h * width:(h + 1) * width] for h in range(N_HEADS)], axis=0)
    return jnp.stack([ref[b, :, h * width:(h + 1) * width] for b in range(ref.shape[0]) for h in range(N_HEADS)],
                     axis=0)


def _bmm(a, b, ca, cb):
    return lax.dot_general(a, b, (((ca,), (cb,)), ((0,), (0,))), preferred_element_type=F32)


LOG2E = math.log2(math.e)
EXP2_SCALE = ATTN_SCALE * LOG2E


def _scores_t(k_ref, q_ref, masked, tq, key_offset=0):
    st = _bmm(_heads(k_ref, HEAD_PAD), _heads(q_ref, HEAD_PAD), 2, 2)
    if masked:
        tk = st.shape[1]
        key = lax.broadcasted_iota(jnp.int32, (tk, tq), 0) + key_offset
        qry = lax.broadcasted_iota(jnp.int32, (tk, tq), 1)
        st = jnp.where((key <= qry)[None], st, NEG)
    return st


def attention_fwd(q, k, vt, B, S, tq, side):
    nq = S // tq
    ns = side.n
    nbh = B * N_HEADS
    pairs = [(i, j) for i in range(nq) for j in range(i + 1)]
    n_pairs = len(pairs)
    q_tab = jnp.asarray([p[0] for p in pairs], jnp.int32)
    k_tab = jnp.asarray([p[1] for p in pairs], jnp.int32)

    def kern(q_tab_ref, k_tab_ref, *refs):
        q_ref, k_ref, vt_ref = refs[:3]
        side_in = refs[3:3 + ns]
        o_ref, lse_ref = refs[3 + ns:5 + ns]
        side_out = refs[5 + ns:5 + 2 * ns]
        m_scr, l_scr, acc_scr = refs[5 + 2 * ns:8 + 2 * ns]
        sems = refs[8 + 2 * ns:]
        t = pl.program_id(0)
        qi, ki = q_tab_ref[t], k_tab_ref[t]

        @pl.when(t == 0)
        def _start_side():
            side.start(side_in, side_out, sems)

        @pl.when(ki == 0)
        def _init():
            m_scr[...] = jnp.full(m_scr.shape, NEG, F32)
            l_scr[...] = jnp.zeros(l_scr.shape, F32)
            acc_scr[...] = jnp.zeros(acc_scr.shape, F32)

        def block(masked):
            st = _scores_t(k_ref, q_ref, masked, tq)
            m_prev = m_scr[...]
            m_new = jnp.maximum(m_prev, jnp.max(st, axis=1, keepdims=True))
            alpha = jnp.exp2((m_prev - m_new) * EXP2_SCALE)
            p = jnp.exp2((st - m_new) * EXP2_SCALE)
            l_scr[...] = alpha * l_scr[...] + jnp.sum(p, axis=1, keepdims=True)
            vt3 = vt_ref[...].reshape(nbh, V_HEAD, tq)
            acc_scr[...] = alpha * acc_scr[...] + _bmm(vt3, p.astype(BF16), 2, 1)
            m_scr[...] = m_new

        @pl.when(ki < qi)
        def _full():
            block(False)

        @pl.when(ki == qi)
        def _diagonal():
            block(True)
            ot = acc_scr[...] / l_scr[...]
            for b in range(B):
                for h in range(N_HEADS):
                    o_ref[b, :, h * V_HEAD:(h + 1) * V_HEAD] = ot[b * N_HEADS + h].T
            lse_ref[...] = m_scr[...] * ATTN_SCALE + jnp.log(l_scr[...])

        @pl.when(t == n_pairs - 1)
        def _wait_side():
            side.wait(side_in, side_out, sems)

    grid_spec = pltpu.PrefetchScalarGridSpec(
        num_scalar_prefetch=2, grid=(n_pairs,),
        in_specs=[pl.BlockSpec((B, tq, QK_W), lambda t, qt, kt: (0, qt[t], 0)),
                  pl.BlockSpec((B, tq, QK_W), lambda t, qt, kt: (0, kt[t], 0)),
                  pl.BlockSpec((B, V_W, tq), lambda t, qt, kt: (0, 0, kt[t]))] + [ANY_SPEC] * ns,
        out_specs=[pl.BlockSpec((B, tq, V_W), lambda t, qt, kt: (0, qt[t], 0)),
                   pl.BlockSpec((nbh, 1, tq), lambda t, qt, kt: (0, 0, qt[t]))] + [ANY_SPEC] * ns,
        scratch_shapes=[pltpu.VMEM((nbh, 1, tq), F32), pltpu.VMEM((nbh, 1, tq), F32),
                        pltpu.VMEM((nbh, V_HEAD, tq), F32)] + side.scratch())
    return pl.pallas_call(
        kern, name="attention_fwd", grid_spec=grid_spec,
        out_shape=[jax.ShapeDtypeStruct((B, S, V_W), F32), jax.ShapeDtypeStruct((nbh, 1, S), F32)]
        + side.out_shapes(),
        compiler_params=_params(("arbitrary",)),
    )(q_tab, k_tab, q, k, vt, *side.arrays)


def attention_bwd(q, k, kt, v, do, lse, delta, B, S, tq, side):
    tk = tq
    tq = min(2 * tk, S)
    r = tq // tk
    nk, nq = S // tk, S // tq
    ns = side.n
    pairs = [(j, i) for j in range(nk) for i in range(j // r, nq)]
    n_pairs = len(pairs)
    k_tab = jnp.asarray([p[0] for p in pairs], jnp.int32)
    q_tab = jnp.asarray([p[1] for p in pairs], jnp.int32)

    def kern(k_tab_ref, q_tab_ref, *refs):
        q_ref, k_ref, kt_ref, v_ref, do_ref, lse_ref, delta_ref = refs[:7]
        side_in = refs[7:7 + ns]
        dq_hbm, dk_ref, dv_ref = refs[7 + ns:10 + ns]
        side_out = refs[10 + ns:10 + 2 * ns]
        dq_acc, dk_acc, dv_acc = refs[10 + 2 * ns:13 + 2 * ns]
        sems = refs[13 + 2 * ns:]
        b, t = pl.program_id(0), pl.program_id(1)
        kj, qi = k_tab_ref[t], q_tab_ref[t]
        on_diagonal = qi == kj // r

        @pl.when(jnp.logical_and(b == 0, t == 0))
        def _start_side():
            side.start(side_in, side_out, sems)

        @pl.when(t == 0)
        def _zero_dq():
            dq_acc[...] = jnp.zeros(dq_acc.shape, F32)

        @pl.when(on_diagonal)
        def _zero_dkv():
            dk_acc[...] = jnp.zeros(dk_acc.shape, F32)
            dv_acc[...] = jnp.zeros(dv_acc.shape, F32)

        def block(masked):
            st = _scores_t(k_ref, q_ref, masked, tq, kj * tk - qi * tq)
            pt = jnp.exp2(st * EXP2_SCALE - lse_ref[...] * LOG2E)
            do3 = _heads(do_ref, V_HEAD).astype(BF16)
            dv_acc[...] = dv_acc[...] + _bmm(pt.astype(BF16), do3, 2, 1)
            dpt = _bmm(_heads(v_ref, V_HEAD), do3, 2, 2)
            dst = (pt * (dpt - delta_ref[...]) * ATTN_SCALE).astype(BF16)
            dk_acc[...] = dk_acc[...] + _bmm(dst, _heads(q_ref, HEAD_PAD), 2, 1)
            q_cols = pl.ds(pl.multiple_of(qi * tq, tq), tq)
            kt3 = kt_ref[...].reshape(N_HEADS, HEAD_PAD, tk)
            dq_acc[:, :, q_cols] = dq_acc[:, :, q_cols] + _bmm(kt3, dst, 2, 1)

        @pl.when(jnp.logical_not(on_diagonal))
        def _full():
            block(False)

        @pl.when(on_diagonal)
        def _diagonal():
            block(True)

        @pl.when(qi == nq - 1)
        def _write_dkv():
            for h in range(N_HEADS):
                dk_ref[:, h * HEAD_PAD:(h + 1) * HEAD_PAD] = dk_acc[h]
                dv_ref[:, h * V_HEAD:(h + 1) * V_HEAD] = dv_acc[h]

        @pl.when(t == n_pairs - 1)
        def _write_dq():
            pltpu.sync_copy(dq_acc, dq_hbm.at[b])

        @pl.when(jnp.logical_and(b == B - 1, t == n_pairs - 1))
        def _wait_side():
            side.wait(side_in, side_out, sems)

    T = B * S
    qrow = lambda b, t, kt, qt: (b * nq + qt[t], 0)
    qcol3 = lambda b, t, kt, qt: (0, 0, b * nq + qt[t])
    krow = lambda b, t, kt, qt: (b * nk + kt[t], 0)
    grid_spec = pltpu.PrefetchScalarGridSpec(
        num_scalar_prefetch=2, grid=(B, n_pairs),
        in_specs=[pl.BlockSpec((tq, QK_W), qrow), pl.BlockSpec((tk, QK_W), krow),
                  pl.BlockSpec((QK_W, tk), lambda b, t, kt, qt: (0, b * nk + kt[t])), pl.BlockSpec((tk, V_W), krow),
                  pl.BlockSpec((tq, V_W), qrow), pl.BlockSpec((N_HEADS, 1, tq), qcol3),
                  pl.BlockSpec((N_HEADS, 1, tq), qcol3)] + [ANY_SPEC] * ns,
        out_specs=[ANY_SPEC, pl.BlockSpec((tk, QK_W), krow), pl.BlockSpec((tk, V_W), krow)] + [ANY_SPEC] * ns,
        scratch_shapes=[pltpu.VMEM((N_HEADS, HEAD_PAD, S), F32), pltpu.VMEM((N_HEADS, tk, HEAD_PAD), F32),
                        pltpu.VMEM((N_HEADS, tk, V_HEAD), F32)] + side.scratch())
    return pl.pallas_call(
        kern, name="attention_bwd", grid_spec=grid_spec,
        out_shape=[jax.ShapeDtypeStruct((B, N_HEADS, HEAD_PAD, S), F32), jax.ShapeDtypeStruct((T, QK_W), F32),
                   jax.ShapeDtypeStruct((T, V_W), F32)] + side.out_shapes(),
        compiler_params=_params(("arbitrary", "arbitrary")),
    )(k_tab, q_tab, q, k, kt, v, do, lse, delta, *side.arrays)


GB = 8
N_GB = N_GROUPS // GB
GB_U = GB * GROUP
GB_ST = GB * N_STATE


def _gb_u(j):
    return slice(j * GB_U, (j + 1) * GB_U)


def _gb_s(j):
    return slice(j * GB_ST, (j + 1) * GB_ST)


def ssm_param_fn(lr, li, ld, br, bi):
    dt = jnp.exp(ld)
    er = jnp.exp(lr * dt)
    ar = er * jnp.cos(li * dt)
    ai = er * jnp.sin(li * dt)
    nr = ar - 1.0
    den = lr * lr + li * li
    cr = (nr * lr + ai * li) / den
    ci = (ai * lr - nr * li) / den
    return ar, ai, cr * br - ci * bi, cr * bi + ci * br


def ssm_params_fwd(reps):
    def kern(lr, li, ld, br, bi, ar_o, ai_o, bbr_o, bbi_o):
        ar, ai, bbr, bbi = ssm_param_fn(lr[...], li[...], ld[...], br[...], bi[...])
        ar_o[...] = ar
        ai_o[...] = ai
        bbr_o[...] = bbr
        bbi_o[...] = bbi

    shp = jax.ShapeDtypeStruct(reps[0].shape, F32)
    return pl.pallas_call(kern, name="ssm_params_fwd", out_shape=[shp] * 4)(*reps)


def ssm_params_bwd(reps, cots):
    rows = reps[0].shape[0]

    def kern(lr, li, ld, br, bi, d_ar, d_ai, d_bbr, d_bbi, dlr_o, dli_o, dld_o, dbr_o, dbi_o):
        _, vjp = jax.vjp(ssm_param_fn, lr[...], li[...], ld[...], br[...], bi[...])
        dlr, dli, dld, dbr, dbi = vjp((d_ar[...], d_ai[...], d_bbr[...], d_bbi[...]))
        dlr_o[...] = jnp.sum(dlr.reshape(N_GROUPS, GROUP, N_STATE), axis=1)
        dli_o[...] = jnp.sum(dli.reshape(N_GROUPS, GROUP, N_STATE), axis=1)
        dld_o[...] = jnp.sum(jnp.sum(dld.reshape(N_GROUPS, GROUP, N_STATE), axis=1), axis=-1, keepdims=True)
        dbr_o[...] = dbr
        dbi_o[...] = dbi

    g = jax.ShapeDtypeStruct((N_GROUPS, N_STATE), F32)
    full = jax.ShapeDtypeStruct((rows, N_STATE), F32)
    return pl.pallas_call(
        kern, name="ssm_params_bwd",
        out_shape=[g, g, jax.ShapeDtypeStruct((N_GROUPS, 1), F32), full, full])(*reps, *cots)


def ssm_fwd(proj, b_re, b_im, c_re, c_im, d_vec, a_re, a_im, B, S, ts):
    ns = S // ts
    T = B * S

    def kern(u_ref, bre_ref, bim_ref, cre_ref, cim_ref, d_ref, are_ref, aim_ref,
             sre_ref, sim_ref, y_ref, car_re, car_im, bu_re, bu_im):
        s = pl.program_id(1)

        @pl.when(s == 0)
        def _reset():
            car_re[...] = jnp.zeros(car_re.shape, F32)
            car_im[...] = jnp.zeros(car_im.shape, F32)

        u = u_ref[...]
        ub = u.astype(BF16)
        for j in range(N_GB):
            uj, bj, sj = ub[:, _gb_u(j)], _gb_u(j), _gb_s(j)
            bu_re[:, sj] = _dot(uj, bre_ref[bj, :])
            bu_im[:, sj] = _dot(uj, bim_ref[bj, :])
        ar, ai = are_ref[...], aim_ref[...]

        def step(t, carry):
            xr, xi = carry
            row = pl.ds(t, 1)
            nr = ar * xr - ai * xi + bu_re[row, :]
            ni = ar * xi + ai * xr + bu_im[row, :]
            sre_ref[row, :] = nr
            sim_ref[row, :] = ni
            return nr, ni

        xr, xi = lax.fori_loop(0, ts, step, (car_re[0:1, :], car_im[0:1, :]))
        car_re[0:1, :] = xr
        car_im[0:1, :] = xi
        y = [_dot(sre_ref[:, _gb_s(j)].astype(BF16), cre_ref[_gb_s(j), :])
             - _dot(sim_ref[:, _gb_s(j)].astype(BF16), cim_ref[_gb_s(j), :]) for j in range(N_GB)]
        y_ref[...] = jnp.concatenate(y, axis=1) + d_ref[...] * u

    row = lambda b, s: (b * ns + s, 0)
    whole = lambda b, s: (0, 0)
    return pl.pallas_call(
        kern, name="ssm_fwd", grid=(B, ns),
        in_specs=[pl.BlockSpec((ts, D_SSM), row),
                  pl.BlockSpec((D_SSM, GB_ST), whole), pl.BlockSpec((D_SSM, GB_ST), whole),
                  pl.BlockSpec((N_ST, GB_U), whole), pl.BlockSpec((N_ST, GB_U), whole),
                  pl.BlockSpec((1, D_SSM), whole), pl.BlockSpec((1, N_ST), whole), pl.BlockSpec((1, N_ST), whole)],
        out_specs=[pl.BlockSpec((ts, N_ST), row), pl.BlockSpec((ts, N_ST), row), pl.BlockSpec((ts, D_SSM), row)],
        out_shape=[jax.ShapeDtypeStruct((T, N_ST), F32), jax.ShapeDtypeStruct((T, N_ST), F32),
                   jax.ShapeDtypeStruct((T, D_SSM), F32)],
        scratch_shapes=[pltpu.VMEM((8, N_ST), F32), pltpu.VMEM((8, N_ST), F32),
                        pltpu.VMEM((ts, N_ST), F32), pltpu.VMEM((ts, N_ST), F32)],
        compiler_params=_params(("arbitrary", "arbitrary")),
    )(proj, b_re, b_im, c_re, c_im, d_vec, a_re, a_im)


def ssm_bwd(dy, proj, s_re, s_im, b_re, b_im, c_re, c_im, d_vec, a_re, a_im, B, S, ts):
    ns = S // ts
    T = B * S
    t8 = ts // 8

    def kern(dy_ref, u_ref, sre_ref, sim_ref, pre_ref, pim_ref, bre_ref, bim_ref, cre_ref, cim_ref,
             d_ref, are_ref, aim_ref,
             du_ref, dcre_ref, dcim_ref, dbre_ref, dbim_ref, dare_ref, daim_ref, dd_ref,
             car_re, car_im, g_re, g_im):
        b, s = pl.program_id(0), pl.program_id(1)
        first = jnp.logical_and(b == 0, s == 0)

        @pl.when(s == 0)
        def _reset():
            car_re[...] = jnp.zeros(car_re.shape, F32)
            car_im[...] = jnp.zeros(car_im.shape, F32)

        dyv = dy_ref[...]
        dyb = dyv.astype(BF16)
        u = u_ref[...]
        for j in range(N_GB):
            g_re[:, _gb_s(j)] = _dot_nt(dyb[:, _gb_u(j)], cre_ref[_gb_s(j), :])
            g_im[:, _gb_s(j)] = -_dot_nt(dyb[:, _gb_u(j)], cim_ref[_gb_s(j), :])
        ar, ai = are_ref[...], aim_ref[...]

        def step(i, carry):
            gr_next, gi_next = carry
            row = pl.ds(ts - 1 - i, 1)
            gr = g_re[row, :] + ar * gr_next + ai * gi_next
            gi = g_im[row, :] + ar * gi_next - ai * gr_next
            g_re[row, :] = gr
            g_im[row, :] = gi
            return gr, gi

        gr0, gi0 = lax.fori_loop(0, ts, step, (car_re[0:1, :], car_im[0:1, :]))
        car_re[0:1, :] = gr0
        car_im[0:1, :] = gi0

        gr, gi = g_re[...], g_im[...]
        sr, si = sre_ref[...], sim_ref[...]
        has_prev = (s < ns - 1).astype(F32)
        row_id = lax.broadcasted_iota(jnp.int32, (ts, N_ST), 0)
        spr = jnp.where(row_id == 0, pre_ref[7:8, :] * has_prev, pltpu.roll(sr, 1, axis=0))
        spi = jnp.where(row_id == 0, pim_ref[7:8, :] * has_prev, pltpu.roll(si, 1, axis=0))
        grb, gib, ub = gr.astype(BF16), gi.astype(BF16), u.astype(BF16)
        srb, sib = sr.astype(BF16), si.astype(BF16)
        cat0 = lambda f: jnp.concatenate([f(j) for j in range(N_GB)], axis=0)
        cat1 = lambda f: jnp.concatenate([f(j) for j in range(N_GB)], axis=1)
        parts = [
            (dcre_ref, cat0(lambda j: _dot_tn(srb[:, _gb_s(j)], dyb[:, _gb_u(j)]))),
            (dcim_ref, cat0(lambda j: -_dot_tn(sib[:, _gb_s(j)], dyb[:, _gb_u(j)]))),
            (dbre_ref, cat0(lambda j: _dot_tn(ub[:, _gb_u(j)], grb[:, _gb_s(j)]))),
            (dbim_ref, cat0(lambda j: _dot_tn(ub[:, _gb_u(j)], gib[:, _gb_s(j)]))),
            (dare_ref, _colsum(gr * spr + gi * spi)),
            (daim_ref, _colsum(gi * spr - gr * spi)),
            (dd_ref, _colsum(dyv * u)),
        ]
        du = cat1(lambda j: _dot_nt(grb[:, _gb_s(j)], bre_ref[_gb_u(j), :])
                  + _dot_nt(gib[:, _gb_s(j)], bim_ref[_gb_u(j), :]))
        du_ref[...] = du + d_ref[...] * dyv

        @pl.when(first)
        def _set():
            for ref, val in parts:
                ref[...] = val

        @pl.when(jnp.logical_not(first))
        def _add():
            for ref, val in parts:
                ref[...] = ref[...] + val

    row = lambda b, s: (b * ns + (ns - 1 - s), 0)
    prev = lambda b, s: (jnp.maximum((b * ns + (ns - 1 - s)) * t8 - 1, 0), 0)
    whole = lambda b, s: (0, 0)
    acc_shapes = [(N_ST, GB_U), (N_ST, GB_U), (D_SSM, GB_ST), (D_SSM, GB_ST), (1, N_ST), (1, N_ST), (1, D_SSM)]
    return pl.pallas_call(
        kern, name="ssm_bwd", grid=(B, ns),
        in_specs=[pl.BlockSpec((ts, D_SSM), row), pl.BlockSpec((ts, D_SSM), row),
                  pl.BlockSpec((ts, N_ST), row), pl.BlockSpec((ts, N_ST), row),
                  pl.BlockSpec((8, N_ST), prev), pl.BlockSpec((8, N_ST), prev),
                  pl.BlockSpec((D_SSM, GB_ST), whole), pl.BlockSpec((D_SSM, GB_ST), whole),
                  pl.BlockSpec((N_ST, GB_U), whole), pl.BlockSpec((N_ST, GB_U), whole),
                  pl.BlockSpec((1, D_SSM), whole), pl.BlockSpec((1, N_ST), whole), pl.BlockSpec((1, N_ST), whole)],
        out_specs=[pl.BlockSpec((ts, D_SSM), row)] + [pl.BlockSpec(sh, whole) for sh in acc_shapes],
        out_shape=[jax.ShapeDtypeStruct((T, D_SSM), F32)] + [jax.ShapeDtypeStruct(sh, F32) for sh in acc_shapes],
        scratch_shapes=[pltpu.VMEM((8, N_ST), F32), pltpu.VMEM((8, N_ST), F32),
                        pltpu.VMEM((ts, N_ST), F32), pltpu.VMEM((ts, N_ST), F32)],
        compiler_params=_params(("arbitrary", "arbitrary")),
    )(dy, proj, s_re, s_im, s_re, s_im, b_re, b_im, c_re, c_im, d_vec, a_re, a_im)


def modulation_fwd(c_all, ada_w, ada_b, fada_w, fada_b):
    def kern(c_ref, w_ref, b_ref, fw_ref, fb_ref, cond_ref, mod_ref, fmod_ref):
        cv = c_ref[...]
        cond = (cv * _sigmoid(cv)).astype(BF16)
        cond_ref[...] = cond
        mod_ref[...] = _dot(cond, w_ref[...].astype(BF16)) + b_ref[...]
        fmod_ref[...] = _dot(cond, fw_ref[...].astype(BF16)) + fb_ref[...]

    n = c_all.shape[0]
    return pl.pallas_call(
        kern, name="modulation_fwd",
        out_shape=[jax.ShapeDtypeStruct(c_all.shape, BF16), jax.ShapeDtypeStruct((n, ada_w.shape[1]), F32),
                   jax.ShapeDtypeStruct((n, fada_w.shape[1]), F32)],
        compiler_params=_params(),
    )(c_all, ada_w, ada_b, fada_w, fada_b)


def modulation_bwd(cond_all, dmod, dfmod, dmod_all):
    def kern(c_ref, dm_ref, dfm_ref, all_ref, gw_ref, gfw_ref, gb_ref):
        cond = c_ref[...]
        gw_ref[...] = _dot_tn(cond, dm_ref[...].astype(BF16))
        gfw_ref[...] = _dot_tn(cond, dfm_ref[...].astype(BF16))
        gb_ref[...] = _colsum(all_ref[...].astype(F32))

    d = cond_all.shape[1]
    return pl.pallas_call(
        kern, name="modulation_bwd",
        out_shape=[jax.ShapeDtypeStruct((d, dmod.shape[1]), F32), jax.ShapeDtypeStruct((d, dfmod.shape[1]), F32),
                   jax.ShapeDtypeStruct((1, dmod_all.shape[1]), F32)],
        compiler_params=_params(),
    )(cond_all, dmod, dfmod, dmod_all)


def _adamw_update(p_ref, w_ref, m_ref, v_ref, g_o, d_o, m_o, v_o):
    g = p_ref[0].astype(F32)
    for i in range(1, p_ref.shape[0]):
        g = g + p_ref[i].astype(F32)
    m_new = ADAM_B1 * m_ref[...] + (1.0 - ADAM_B1) * g
    v_new = ADAM_B2 * v_ref[...] + (1.0 - ADAM_B2) * (g * g)
    m_hat = m_new / (1.0 - ADAM_B1 ** ADAM_STEP)
    v_hat = v_new / (1.0 - ADAM_B2 ** ADAM_STEP)
    g_o[...] = g
    d_o[...] = -ADAM_LR * (m_hat / (jnp.sqrt(v_hat) + ADAM_EPS) + ADAM_WD * w_ref[...])
    m_o[...] = m_new
    v_o[...] = v_new


def adamw_many(name, items):
    k = len(items)

    def kern(*refs):
        ins, outs = refs[:4 * k], refs[4 * k:]
        for i in range(k):
            _adamw_update(*ins[4 * i:4 * i + 4], *outs[4 * i:4 * i + 4])

    flat = [a for item in items for a in item]
    out_shape = [jax.ShapeDtypeStruct(item[1].shape, F32) for item in items for _ in range(4)]
    outs = pl.pallas_call(kern, name=name, out_shape=out_shape, compiler_params=_params())(*flat)
    return [tuple(outs[4 * i:4 * i + 4]) for i in range(k)]


def adamw(name, parts, w, m, v):
    n, R, C = parts.shape
    tr = R
    while n * tr * C * 4 > 4 * 1024 * 1024 and tr % 32 == 0:
        tr //= 2

    def kern(p_ref, w_ref, m_ref, v_ref, g_o, d_o, m_o, v_o):
        _adamw_update(p_ref, w_ref, m_ref, v_ref, g_o, d_o, m_o, v_o)

    blk = pl.BlockSpec((tr, C), lambda i: (i, 0))
    shp = jax.ShapeDtypeStruct((R, C), F32)
    return pl.pallas_call(
        kern, name=name, grid=(R // tr,),
        in_specs=[pl.BlockSpec((n, tr, C), lambda i: (0, i, 0)), blk, blk, blk],
        out_specs=[blk] * 4, out_shape=[shp] * 4,
        compiler_params=_params(("parallel",)),
    )(parts, w, m, v)


def _cols_from_gathered(g):
    return jnp.transpose(g, (1, 0, 2)).reshape(g.shape[1], N_DEV * g.shape[2])


def _cols_to_shards(w):
    k, n8 = w.shape
    return jnp.transpose(w.reshape(k, N_DEV, n8 // N_DEV), (1, 0, 2))


def _pad_w_in(w):
    z = functools.partial(jnp.zeros, dtype=w.dtype)
    k = w.shape[0]
    cut = D_SSM + Q_LORA + KV_LORA
    return jnp.concatenate([w[:, :cut], z((k, 64)), w[:, cut:], z((k, 32))], axis=1)


def _unpad_w_in(w):
    cut = D_SSM + Q_LORA + KV_LORA
    return jnp.concatenate([w[:, :cut], w[:, cut + 64:cut + 96]], axis=1)


def _pad_w_uq(w):
    k = w.shape[0]
    w3 = w.reshape(k, N_HEADS, QK_NOPE + QK_ROPE)
    return jnp.pad(w3, ((0, 0), (0, 0), (0, HEAD_PAD - QK_NOPE - QK_ROPE))).reshape(k, QK_W)


def _unpad_w_uq(w):
    k = w.shape[0]
    return w.reshape(k, N_HEADS, HEAD_PAD)[:, :, :QK_NOPE + QK_ROPE].reshape(k, N_HEADS * (QK_NOPE + QK_ROPE))


def _pad_w_ukv(w):
    k = w.shape[0]
    w3 = w.reshape(k, N_HEADS, QK_NOPE + V_HEAD)
    kpart = jnp.pad(w3[:, :, :QK_NOPE], ((0, 0), (0, 0), (0, HEAD_PAD - QK_NOPE))).reshape(k, QK_W)
    vpart = w3[:, :, QK_NOPE:].reshape(k, V_W)
    return jnp.concatenate([kpart, vpart], axis=1)


def _unpad_w_ukv(w):
    k = w.shape[0]
    kpart = w[:, :QK_W].reshape(k, N_HEADS, HEAD_PAD)[:, :, :QK_NOPE]
    vpart = w[:, QK_W:].reshape(k, N_HEADS, V_HEAD)
    return jnp.concatenate([kpart, vpart], axis=2).reshape(k, N_HEADS * (QK_NOPE + V_HEAD))


def _block_diag(blocks):
    g, r, c = blocks.shape
    b4 = blocks.reshape(N_GB, GB, r, c)
    keep = jnp.eye(GB, dtype=bool)[None, :, None, :, None]
    return jnp.where(keep, b4[:, :, :, None, :], 0).reshape(g * r, GB * c)


def _block_diag_take(stacked, r, c):
    d5 = stacked.reshape(N_GB, GB, r, GB, c)
    idx = jnp.arange(GB)
    return jnp.transpose(d5[:, idx, :, idx, :], (1, 0, 2, 3)).reshape(N_GROUPS, r, c)


SMALL = ["norm1_g", "ssm_lambda_re", "ssm_lambda_im", "ssm_b_re", "ssm_b_im", "ssm_c_re", "ssm_c_im",
         "ssm_d", "ssm_log_dt", "q_norm_g", "kv_norm_g", "ssm_out_g", "attn_out_g", "norm2_g", "final_norm_g"]
BIASES = ["ada_b", "final_ada_b"]


def _dense_2d(shape):
    dims = [d for d in shape[1:]] if len(shape) > 1 and shape[0] == 1 else list(shape)
    if len(dims) == 1:
        return (1, dims[0])
    return (dims[0], int(np.prod(dims[1:])))


def kernel(x, c, positions, ada_w, ada_b, norm1_g, w_in, ssm_lambda_re, ssm_lambda_im, ssm_b_re, ssm_b_im, ssm_c_re, ssm_c_im, ssm_d, ssm_log_dt, w_glu, q_norm_g, w_uq, kv_norm_g, w_ukv, ssm_out_g, attn_out_g, w_out, norm2_g, w_ff1, w_ff2, final_ada_w, final_ada_b, final_norm_g, loss_target, m_ada_w, m_ada_b, m_norm1_g, m_w_in, m_ssm_lambda_re, m_ssm_lambda_im, m_ssm_b_re, m_ssm_b_im, m_ssm_c_re, m_ssm_c_im, m_ssm_d, m_ssm_log_dt, m_w_glu, m_q_norm_g, m_w_uq, m_kv_norm_g, m_w_ukv, m_ssm_out_g, m_attn_out_g, m_w_out, m_norm2_g, m_w_ff1, m_w_ff2, m_final_ada_w, m_final_ada_b, m_final_norm_g, v_ada_w, v_ada_b, v_norm1_g, v_w_in, v_ssm_lambda_re, v_ssm_lambda_im, v_ssm_b_re, v_ssm_b_im, v_ssm_c_re, v_ssm_c_im, v_ssm_d, v_ssm_log_dt, v_w_glu, v_q_norm_g, v_w_uq, v_kv_norm_g, v_w_ukv, v_ssm_out_g, v_attn_out_g, v_w_out, v_norm2_g, v_w_ff1, v_w_ff2, v_final_ada_w, v_final_ada_b, v_final_norm_g):
    given = dict(locals())
    B, S, D = x.shape
    T = B * S
    tm = min(256, S)
    me = 4 * lax.axis_index("x") + 2 * lax.axis_index("y") + lax.axis_index("c")

    big = ["w_in", "w_glu", "w_uq", "w_ukv", "w_out", "w_ff1", "w_ff2"]
    shards = {n: given[n][0] for n in big}
    early, late = ["w_in", "w_uq", "w_ukv"], ["w_glu", "w_out", "w_ff1", "w_ff2"]
    c_g, w_in_g = all_gather("gather_first_weights", [c, shards["w_in"].astype(BF16)])
    c_all = c_g.reshape(N_DEV * B, D)
    w_in_p = _pad_w_in(_cols_from_gathered(w_in_g))

    n_mod = ada_w.shape[2]
    n_fmod = final_ada_w.shape[1]
    ada_b_sh = lax.dynamic_slice_in_dim(ada_b, me * n_mod, n_mod, axis=1)
    fada_b_sh = lax.dynamic_slice_in_dim(final_ada_b[None, :], me * n_fmod, n_fmod, axis=1)
    cond_all, mod_sh, fmod_sh = modulation_fwd(c_all, ada_w[0], ada_b_sh, final_ada_w, fada_b_sh)
    mod_g, fmod_g = all_gather("gather_modulation", [mod_sh, fmod_sh])
    mod_mine = lax.dynamic_slice_in_dim(_cols_from_gathered(mod_g), me * B, B, axis=0)
    fmod_mine = lax.dynamic_slice_in_dim(_cols_from_gathered(fmod_g), me * B, B, axis=0)
    shift1, scale1, gate1, shift2, scale2, gate2 = [mod_mine[:, None, i * D:(i + 1) * D] for i in range(6)]
    fshift, fscale = fmod_mine[:, None, :D], fmod_mine[:, None, D:]

    x2d = x.reshape(T, D)
    tgt2d = loss_target.reshape(T, D)
    pos = positions.astype(F32).reshape(T, 1)
    invf, m_lo, m_hi = _rope_consts()
    rope_mask = m_lo + m_hi

    def fwd_in(rows, bv, vecs, res):
        (xv,), (sc, sh), (g1,), (w,) = rows, bv, vecs, res
        r, xh = _rms_stats(xv)
        h = xh * g1 * (1.0 + sc) + sh
        return [_dot(h.astype(BF16), w[...])], [], []

    qkv_gather = Exchange([shards["w_uq"].astype(BF16), shards["w_ukv"].astype(BF16)], [GATHER, GATHER])
    (proj,), _, _, (w_uq_g, w_ukv_g) = row_call(
        "fwd_in", fwd_in, B, S, tm, [x2d], [scale1, shift1], [norm1_g], [w_in_p], [(IN_PAD, F32)], [], [],
        side=qkv_gather)
    w_uq_p = _pad_w_uq(_cols_from_gathered(w_uq_g))
    w_ukv_p = _pad_w_ukv(_cols_from_gathered(w_ukv_g))

    def fwd_qkv(rows, bv, vecs, res):
        (pr, ps), (gq, gkv, fr, lo, hi), (wq, wkv) = rows, vecs, res
        cs, s_lo, s_hi = _rope_tables(ps, fr, lo, hi)
        _, qh = _rms_stats(pr[:, D_SSM:D_SSM + Q_LORA])
        _, kvh = _rms_stats(pr[:, D_SSM + Q_LORA:D_SSM + Q_LORA + KV_LORA])
        qf = _dot((qh * gq).astype(BF16), wq[...])
        kvf = _dot((kvh * gkv).astype(BF16), wkv[...])
        k_rope = _rope_tile(pr[:, IN_PAD - HEAD_PAD:], cs, s_lo, s_hi, 1.0)
        q_t, k_t = [], []
        for h in range(N_HEADS):
            hs = slice(h * HEAD_PAD, (h + 1) * HEAD_PAD)
            q_t.append(_rope_tile(qf[:, hs], cs, s_lo, s_hi, 1.0))
            k_t.append(kvf[:, hs] + k_rope)
        kf, vf = jnp.concatenate(k_t, axis=1), kvf[:, QK_W:]
        return [jnp.concatenate(q_t, axis=1), kf, vf, t_val(kf), t_val(vf)[None], cs, s_lo, s_hi], [], []

    ns = S // tm
    col_tile = lambda b, s: (0, b * ns + s)
    t_out = lambda w: ((w, T), BF16, (w, tm), col_tile)
    t_val = lambda v: v.astype(BF16).T
    (q_r, k_r, v_r, k_t, v_t, rope_cs, rope_lo, rope_hi), _, _ = row_call(
        "fwd_qkv", fwd_qkv, B, S, tm, [proj, pos], [], [q_norm_g, kv_norm_g, invf, m_lo, m_hi],
        [w_uq_p, w_ukv_p],
        [(QK_W, BF16), (QK_W, BF16), (V_W, BF16), ((QK_W, T), BF16, (QK_W, tm), col_tile),
         ((B, V_W, S), BF16, (1, V_W, tm), lambda b, s: (b, 0, s)), (HEAD_PAD, F32), (HEAD_PAD, F32),
         (HEAD_PAD, F32)], [], [])
    late_gather = Exchange([shards[n].astype(BF16) for n in late], [GATHER] * len(late))
    y_attn, lse, *late_g = attention_fwd(q_r.reshape(B, S, QK_W), k_r.reshape(B, S, QK_W), v_t, B, S, tm,
                                         late_gather)
    y_attn = y_attn.reshape(T, V_W)
    lse = jnp.transpose(lse.reshape(B, N_HEADS, 1, S), (1, 2, 0, 3)).reshape(N_HEADS, 1, T)
    gw = dict(zip(late, late_g))
    w_glu_f = _cols_from_gathered(gw["w_glu"])
    w_out_f = gw["w_out"].reshape(-1, D)
    w_ff1_f = _cols_from_gathered(gw["w_ff1"])
    w_ff2_f = gw["w_ff2"].reshape(-1, D)

    rep = lambda a: jnp.repeat(a, GROUP, axis=0)
    lam_rep = [rep(ssm_lambda_re[0]), rep(ssm_lambda_im[0]),
               rep(jnp.broadcast_to(ssm_log_dt[0][:, None], (N_GROUPS, N_STATE)))]
    bt = lambda a: jnp.transpose(a, (0, 2, 1)).reshape(D_SSM, N_STATE)
    reps = lam_rep + [bt(ssm_b_re[0]), bt(ssm_b_im[0])]
    a_re_rep, a_im_rep, bb_re, bb_im = ssm_params_fwd(reps)
    a_re = a_re_rep.reshape(N_GROUPS, GROUP, N_STATE)[:, 0, :].reshape(1, N_ST)
    a_im = a_im_rep.reshape(N_GROUPS, GROUP, N_STATE)[:, 0, :].reshape(1, N_ST)
    bbd_re = _block_diag(bb_re.reshape(N_GROUPS, GROUP, N_STATE)).astype(BF16)
    bbd_im = _block_diag(bb_im.reshape(N_GROUPS, GROUP, N_STATE)).astype(BF16)
    cbd_re = _block_diag(jnp.transpose(ssm_c_re[0], (0, 2, 1))).astype(BF16)
    cbd_im = _block_diag(jnp.transpose(ssm_c_im[0], (0, 2, 1))).astype(BF16)
    d_vec = ssm_d.reshape(1, D_SSM)
    st_re, st_im, y_pre = ssm_fwd(proj, bbd_re, bbd_im, cbd_re, cbd_im, d_vec, a_re, a_im, B, S, tm)

    def fwd_mix(rows, bv, vecs, res):
        (yp, ya, xv), (g1v,), (gso, gao), (wg, wo) = rows, bv, vecs, res
        z = _dot(_gelu(yp).astype(BF16), wg[...])
        y_ssm = z[:, :D_SSM] * _sigmoid(z[:, D_SSM:])
        _, sh = _rms_stats(y_ssm)
        _, ah = _rms_stats(ya)
        o = _dot((sh * gso).astype(BF16), wo[0:D_SSM, :]) + _dot((ah * gao).astype(BF16), wo[D_SSM:, :])
        return [z, o, xv + g1v * o], [], []

    (z, o, x2), _, _ = row_call("fwd_mix", fwd_mix, B, S, tm, [y_pre, y_attn, x2d], [gate1],
                                [ssm_out_g, attn_out_g], [w_glu_f, w_out_f],
                                [(2 * D_SSM, F32), (D, BF16), (D, F32)], [], [])

    def final_out(x3, gf, fsc, fsh):
        r3, xh3 = _rms_stats(x3)
        n3 = xh3 * gf
        return r3, xh3, n3, n3 * (1.0 + fsc) + fsh

    def mlp_norm(xv, g2, sc, sh):
        r2, xh2 = _rms_stats(xv)
        n2 = xh2 * g2
        return r2, xh2, n2, n2 * (1.0 + sc) + sh

    def fwd_mlp(rows, bv, vecs, res):
        (xv, tg), (sc, sh, g2v, fsc, fsh), (g2, gf), (w1, w2) = rows, bv, vecs, res
        _, _, _, h2 = mlp_norm(xv, g2, sc, sh)
        ra = jnp.maximum(_dot(h2.astype(BF16), w1[...]), 0.0)
        ff = _dot((ra * ra).astype(BF16), w2[...])
        x3 = xv + g2v * ff
        _, _, _, out = final_out(x3, gf, fsc, fsh)
        err = out - tg
        loss = 0.5 * jnp.sum(jnp.mean(err * err, axis=-1, keepdims=True), axis=0, keepdims=True)
        return [ff, x3, ra], [], [jnp.broadcast_to(loss, (1, 128))]

    fnorm_g = final_norm_g[None, :]
    (ff, x3, ra_b), _, (loss_acc,) = row_call(
        "fwd_mlp", fwd_mlp, B, S, tm, [x2, tgt2d], [scale2, shift2, gate2, fscale, fshift], [norm2_g, fnorm_g],
        [w_ff1_f, w_ff2_f], [(D, BF16), (D, F32), (4 * D, BF16)], [], [(1, 128)])
    loss = lax.psum(loss_acc[0, 0], ("x", "y", "c"))

    def bwd_mlp(rows, bv, vecs, res):
        (x3v, tg, x2v, ffv, rav), (sc, sh, g2v, fsc, fsh), (g2, gf), (w1, w2) = rows, bv, vecs, res
        r3, xh3, n3, out = final_out(x3v, gf, fsc, fsh)
        dout = (out - tg) * (1.0 / D)
        dn3 = dout * (1.0 + fsc)
        dx3 = _rms_bwd(dn3 * gf, xh3, r3)
        dff = dx3 * g2v
        r2, xh2, n2, h2 = mlp_norm(x2v, g2, sc, sh)
        ra = rav.astype(F32)
        dffb = dff.astype(BF16)
        da = _dot_nt(dffb, w2[...]) * (2.0 * ra)
        dab = da.astype(BF16)
        dh2 = _dot_nt(dab, w1[...])
        dn2 = dh2 * (1.0 + sc)
        dx2 = dx3 + _rms_bwd(dn2 * g2, xh2, r2)
        return ([dx2, t_val(h2), t_val(ra * ra), dab, dffb],
                [_colsum(dout), _colsum(dout * n3), _colsum(dx3 * ffv), _colsum(dh2), _colsum(dh2 * n2)],
                [_colsum(dn3 * xh3), _colsum(dn2 * xh2)])

    ((dx2, h2b, fb, dab, dffb), (d_fshift, d_fscale, d_gate2, d_shift2, d_scale2), (d_gf, d_g2)) = row_call(
        "bwd_mlp", bwd_mlp, B, S, tm, [x3, tgt2d, x2, ff, ra_b], [scale2, shift2, gate2, fscale, fshift],
        [norm2_g, fnorm_g], [w_ff1_f, w_ff2_f],
        [(D, F32), t_out(D), t_out(4 * D), (4 * D, BF16), (D, BF16)], [D] * 5, [(1, D), (1, D)])
    g_w_ff1 = matmul_nn("grad_w_ff1", h2b, dab)
    g_w_ff2 = matmul_nn("grad_w_ff2", fb, dffb)

    def bwd_mix(rows, bv, vecs, res):
        (dxv, ov, zv, ya, yp), (g1v,), (gso, gao), (wo, wg) = rows, bv, vecs, res
        do = (dxv * g1v).astype(BF16)
        dyn = _dot_nt(do, wo[...])
        z1, sg = zv[:, :D_SSM], _sigmoid(zv[:, D_SSM:])
        y_ssm = z1 * sg
        rs, sh = _rms_stats(y_ssm)
        ra, ah = _rms_stats(ya)
        dyn_s, dyn_a = dyn[:, :D_SSM], dyn[:, D_SSM:]
        dy_ssm = _rms_bwd(dyn_s * gso, sh, rs)
        dy_attn = _rms_bwd(dyn_a * gao, ah, ra)
        dz = jnp.concatenate([dy_ssm * sg, dy_ssm * z1 * sg * (1.0 - sg)], axis=1).astype(BF16)
        dy_pre = _dot_nt(dz, wg[...]) * _gelu_grad(yp)
        yn = jnp.concatenate([sh * gso, ah * gao], axis=1)
        delta = jnp.sum((dy_attn * ya).T.reshape(N_HEADS, V_HEAD, tm), axis=1, keepdims=True)
        return ([do, t_val(yn), dz, t_val(_gelu(yp)), dy_attn, dy_pre, delta], [_colsum(dxv * ov)],
                [_colsum(dyn_s * sh), _colsum(dyn_a * ah)])

    ((dob, ynb, dzb, ygb, dy_attn, dy_pre, delta), (d_gate1,), (d_gso, d_gao)) = row_call(
        "bwd_mix", bwd_mix, B, S, tm, [dx2, o, z, y_attn, y_pre], [gate1], [ssm_out_g, attn_out_g],
        [w_out_f, w_glu_f],
        [(D, BF16), t_out(D), (2 * D_SSM, BF16), t_out(D_SSM), (V_W, F32), (D_SSM, F32),
         ((N_HEADS, 1, T), F32, (N_HEADS, 1, tm), lambda b, s: (0, 0, b * ns + s))],
        [D], [(1, D_SSM), (1, V_W)])
    g_w_out = matmul_nn("grad_w_out", ynb, dob)
    g_w_glu = matmul_nn("grad_w_glu", ygb, dzb)

    grads_a = Exchange([g_w_glu, g_w_out.reshape(N_DEV, -1, D), g_w_ff1, g_w_ff2.reshape(N_DEV, -1, D)],
                       [w_glu.shape[2], None, w_ff1.shape[2], None])
    dq_t, dk_r, dv_r, *parts_a = attention_bwd(q_r, k_r, k_t, v_r, dy_attn, lse, delta, B, S, tm, grads_a)
    parts = dict(zip(late, parts_a))
    dq_t = dq_t.reshape(B, QK_W, S)

    def bwd_qkv(rows, bv, vecs, res):
        (dqt, dkv, dvv, pr, cs, s_lo, s_hi), (gq, gkv, rmask), (wq, wkv) = rows, vecs, res
        dqv = dqt[0].T
        dq_t = []
        dk_rope = jnp.zeros((dkv.shape[0], HEAD_PAD), F32)
        for h in range(N_HEADS):
            hs = slice(h * HEAD_PAD, (h + 1) * HEAD_PAD)
            dq_t.append(_rope_tile(dqv[:, hs], cs, s_lo, s_hi, -1.0))
            dk_rope = dk_rope + dkv[:, hs]
        dk_rope = _rope_tile(dk_rope * rmask, cs, s_lo, s_hi, -1.0)
        dqb = jnp.concatenate(dq_t, axis=1).astype(BF16)
        dkvb = jnp.concatenate([dkv, dvv], axis=1).astype(BF16)
        rq, qh = _rms_stats(pr[:, D_SSM:D_SSM + Q_LORA])
        rk, kvh = _rms_stats(pr[:, D_SSM + Q_LORA:D_SSM + Q_LORA + KV_LORA])
        dqn = _dot_nt(dqb, wq[...])
        dkvn = _dot_nt(dkvb, wkv[...])
        dpa = jnp.concatenate([_rms_bwd(dqn * gq, qh, rq), _rms_bwd(dkvn * gkv, kvh, rk), dk_rope], axis=1)
        return [dpa, t_val(qh * gq), t_val(kvh * gkv), dqb, dkvb], [], [_colsum(dqn * qh), _colsum(dkvn * kvh)]

    ((dpa, qnb, kvnb, dqb, dkvb), _, (d_gq, d_gkv)) = row_call(
        "bwd_qkv", bwd_qkv, B, S, tm,
        [(dq_t, (1, QK_W, tm), lambda b, s: (b, 0, s)), dk_r, dv_r, proj, rope_cs, rope_lo, rope_hi], [],
        [q_norm_g, kv_norm_g, rope_mask], [w_uq_p, w_ukv_p],
        [(Q_LORA + KV_LORA + HEAD_PAD, F32), t_out(Q_LORA), t_out(KV_LORA), (QK_W, BF16), (QK_W + V_W, BF16)],
        [], [(1, Q_LORA), (1, KV_LORA)])
    g_w_uq = _unpad_w_uq(matmul_nn("grad_w_uq", qnb, dqb))
    g_w_ukv = _unpad_w_ukv(matmul_nn("grad_w_ukv", kvnb, dkvb))

    du, dc_re_d, dc_im_d, db_re_d, db_im_d, da_re, da_im, d_d = ssm_bwd(
        dy_pre, proj, st_re, st_im, bbd_re, bbd_im, cbd_re, cbd_im, d_vec, a_re, a_im, B, S, tm)
    place = lambda a: jnp.zeros((N_GROUPS, GROUP, N_STATE), F32).at[:, 0, :].set(
        a.reshape(N_GROUPS, N_STATE)).reshape(D_SSM, N_STATE)
    cots = [place(da_re), place(da_im),
            _block_diag_take(db_re_d, GROUP, N_STATE).reshape(D_SSM, N_STATE),
            _block_diag_take(db_im_d, GROUP, N_STATE).reshape(D_SSM, N_STATE)]
    g_lam_re, g_lam_im, g_log_dt, g_bt_re, g_bt_im = ssm_params_bwd(reps, cots)
    unbt = lambda a: jnp.transpose(a.reshape(N_GROUPS, GROUP, N_STATE), (0, 2, 1))
    g_c_re = jnp.transpose(_block_diag_take(dc_re_d, N_STATE, GROUP), (0, 2, 1))
    g_c_im = jnp.transpose(_block_diag_take(dc_im_d, N_STATE, GROUP), (0, 2, 1))

    def bwd_in(rows, bv, vecs, res):
        (duv, dpav, xv, dx2v), (sc, sh), (g1,), (w,) = rows, bv, vecs, res
        dproj = jnp.concatenate([duv, dpav], axis=1).astype(BF16)
        dh = _dot_nt(dproj, w[...])
        r, xh = _rms_stats(xv)
        n1 = xh * g1
        dn1 = dh * (1.0 + sc)
        dx = dx2v + _rms_bwd(dn1 * g1, xh, r)
        return [dx, t_val(n1 * (1.0 + sc) + sh), dproj], [_colsum(dh), _colsum(dh * n1)], [_colsum(dn1 * xh)]

    small_part = {
        "ssm_lambda_re": g_lam_re,
        "ssm_lambda_im": g_lam_im, "ssm_b_re": unbt(g_bt_re), "ssm_b_im": unbt(g_bt_im), "ssm_c_re": g_c_re,
        "ssm_c_im": g_c_im, "ssm_d": d_d, "ssm_log_dt": g_log_dt, "q_norm_g": d_gq, "kv_norm_g": d_gkv,
        "ssm_out_g": d_gso, "attn_out_g": d_gao, "norm2_g": d_g2, "final_norm_g": d_gf}
    dense_bf16 = lambda n, g: g.reshape(_dense_2d(given[n].shape)).astype(BF16)
    ready = [n for n in SMALL if n in small_part]
    grads_b = Exchange([dense_bf16(n, small_part[n]) for n in ready] + [_cols_to_shards(g_w_uq), g_w_ukv],
                       [GATHER] * len(ready) + [None, w_ukv.shape[2]])
    ((grad_x, h1b, dprojb), (d_shift1, d_scale1), (d_g1,)) = row_call(
        "bwd_in", bwd_in, B, S, tm, [du, dpa, x2d, dx2], [scale1, shift1], [norm1_g], [w_in_p],
        [(D, F32), t_out(D), (IN_PAD, BF16)], [D, D], [(1, D)])
    g_w_in_p, *side_b = matmul_nn("grad_w_in", h1b, dprojb, side=grads_b)
    small_parts = dict(zip(ready, side_b[:len(ready)]))
    parts["w_uq"], parts["w_ukv"] = side_b[len(ready):]
    g_w_in = _unpad_w_in(g_w_in_p)

    dmod = jnp.concatenate([d_shift1, d_scale1, d_gate1, d_shift2, d_scale2, d_gate2, d_fshift, d_fscale],
                           axis=2).reshape(B, 8 * D)
    dmod_g, small_parts["norm1_g"], parts["w_in"] = exchange(
        "exchange_last_grads", [dmod.astype(BF16), dense_bf16("norm1_g", d_g1), _cols_to_shards(g_w_in)],
        [GATHER, GATHER, None])
    small_g = [small_parts[n] for n in SMALL]
    dmod_all = dmod_g.reshape(N_DEV * B, 8 * D)
    dm_sh = lax.dynamic_slice_in_dim(dmod_all[:, :6 * D], me * n_mod, n_mod, axis=1)
    dfm_sh = lax.dynamic_slice_in_dim(dmod_all[:, 6 * D:], me * n_fmod, n_fmod, axis=1)
    g_ada_w, g_fada_w, g_biases = modulation_bwd(cond_all, dm_sh, dfm_sh, dmod_all)
    parts["ada_w"] = g_ada_w[None]
    parts["final_ada_w"] = g_fada_w[None]

    res_g, res_d, res_m, res_v = {}, {}, {}, {}
    for n in ["ada_w"] + big + ["final_ada_w"]:
        w_full = given[n]
        w2 = w_full.reshape(w_full.shape[-2], w_full.shape[-1])
        out = adamw("adamw_" + n, parts[n], w2, given["m_" + n].reshape(w2.shape), given["v_" + n].reshape(w2.shape))
        res_g[n], res_d[n], res_m[n], res_v[n] = [a.reshape(w_full.shape) for a in out]
    bias_g = {"ada_b": g_biases[None, :, :6 * D], "final_ada_b": g_biases[None, :, 6 * D:]}
    names = SMALL + BIASES
    items = []
    for n, g_parts in zip(names, small_g + [bias_g[b] for b in BIASES]):
        dense = _dense_2d(given[n].shape)
        items.append((g_parts, given[n].reshape(dense), given["m_" + n].reshape(dense), given["v_" + n].reshape(dense)))
    for n, out in zip(names, adamw_many("adamw_small", items)):
        res_g[n], res_d[n], res_m[n], res_v[n] = [a.reshape(given[n].shape) for a in out]

    order = ["ada_w", "ada_b", "norm1_g", "w_in", "ssm_lambda_re", "ssm_lambda_im", "ssm_b_re", "ssm_b_im",
             "ssm_c_re", "ssm_c_im", "ssm_d", "ssm_log_dt", "w_glu", "q_norm_g", "w_uq", "kv_norm_g", "w_ukv",
             "ssm_out_g", "attn_out_g", "w_out", "norm2_g", "w_ff1", "w_ff2", "final_ada_w", "final_ada_b",
             "final_norm_g"]
    return (loss, grad_x.reshape(B, S, D), *[res_g[n] for n in order], *[res_d[n] for n in order],
            *[res_m[n] for n in order], *[res_v[n] for n in order])
```

```python
import functools
import math

import numpy as np
import jax
import jax.numpy as jnp
from jax import lax
from jax.experimental import pallas as pl
from jax.experimental.pallas import tpu as pltpu

F32 = jnp.float32
BF16 = jnp.bfloat16

N_DEV = 8
EPS = 1e-6
D_SSM = 512
N_GROUPS = 32
GROUP = 16
N_STATE = 64
N_ST = N_GROUPS * N_STATE
Q_LORA = 384
KV_LORA = 256
QK_NOPE = 64
QK_ROPE = 32
V_HEAD = 64
N_HEADS = 8
HEAD_PAD = 128
QK_W = N_HEADS * HEAD_PAD
V_W = N_HEADS * V_HEAD
IN_COLS = D_SSM + Q_LORA + KV_LORA + QK_ROPE
IN_PAD = D_SSM + Q_LORA + KV_LORA + HEAD_PAD
ROPE_BASE = 10000.0
ATTN_SCALE = (QK_NOPE + QK_ROPE) ** -0.5
NEG = -1e30

ADAM_LR = 0.001
ADAM_B1 = 0.9
ADAM_B2 = 0.999
ADAM_EPS = 1e-08
ADAM_WD = 0.01
ADAM_STEP = 10

VMEM_LIMIT = 56 * 1024 * 1024
MESH_ID = pl.DeviceIdType.MESH


def _dot(a, b):
    return jnp.dot(a, b, preferred_element_type=F32)


def _dot_nt(a, b):
    return lax.dot_general(a, b, (((1,), (1,)), ((), ())), preferred_element_type=F32)


def _dot_tn(a, b):
    return lax.dot_general(a, b, (((0,), (0,)), ((), ())), preferred_element_type=F32)


def _rms_stats(x):
    r = lax.rsqrt(jnp.mean(x * x, axis=-1, keepdims=True) + EPS)
    return r, x * r


def _rms_bwd(dy, xh, r):
    return r * (dy - xh * jnp.mean(dy * xh, axis=-1, keepdims=True))


def _sigmoid(x):
    return 1.0 / (1.0 + jnp.exp(-x))


_GELU_C = math.sqrt(2.0 / math.pi)


def _gelu(x):
    t = jnp.tanh(_GELU_C * (x + 0.044715 * x * x * x))
    return 0.5 * x * (1.0 + t)


def _gelu_grad(x):
    t = jnp.tanh(_GELU_C * (x + 0.044715 * x * x * x))
    return 0.5 * (1.0 + t) + 0.5 * x * (1.0 - t * t) * _GELU_C * (1.0 + 3.0 * 0.044715 * x * x)


def _colsum(a):
    return jnp.sum(a, axis=0, keepdims=True)


def _params(sem=None):
    return pltpu.CompilerParams(dimension_semantics=sem, vmem_limit_bytes=VMEM_LIMIT)


def _mesh_pos():
    x, y, c = lax.axis_index("x"), lax.axis_index("y"), lax.axis_index("c")
    return x, y, c, 4 * x + 2 * y + c


def _peer(x, y, c, k):
    px = 1 - x if k & 4 else x
    py = 1 - y if k & 2 else y
    pc = 1 - c if k & 1 else c
    return (px, py, pc), 4 * px + 2 * py + pc


GATHER = "gather"
ANY_SPEC = pl.BlockSpec(memory_space=pl.ANY)


class Exchange:
    def __init__(self, arrays, kinds):
        self.arrays, self.kinds, self.n = list(arrays), list(kinds), len(arrays)

    def _shard(self, i):
        a, kind = self.arrays[i], self.kinds[i]
        if kind == GATHER:
            return a.shape
        return a.shape[1:] if kind is None else (a.shape[0], kind)

    def out_shapes(self):
        return [jax.ShapeDtypeStruct((N_DEV,) + tuple(self._shard(i)), self.arrays[i].dtype) for i in range(self.n)]

    def scratch(self):
        return [pltpu.SemaphoreType.DMA((self.n, N_DEV - 1)), pltpu.SemaphoreType.DMA((self.n, N_DEV - 1)),
                pltpu.SemaphoreType.DMA((self.n,))]

    def _src(self, ins, i, j):
        kind = self.kinds[i]
        if kind == GATHER:
            return ins[i]
        if kind is None:
            return ins[i].at[j]
        return ins[i].at[:, pl.ds(pl.multiple_of(j * kind, 128), kind)]

    def _copies(self, ins, outs, sems, with_received):
        send_sems, recv_sems, loc_sems = sems
        x, y, c, me = _mesh_pos()
        local, sent, received = [], [], []
        for i in range(self.n):
            local.append(pltpu.make_async_copy(self._src(ins, i, me), outs[i].at[me], loc_sems.at[i]))
            for k in range(1, N_DEV):
                peer, pidx = _peer(x, y, c, k)
                pair = dict(send_sem=send_sems.at[i, k - 1], recv_sem=recv_sems.at[i, k - 1], device_id=peer,
                            device_id_type=MESH_ID)
                sent.append(pltpu.make_async_remote_copy(
                    src_ref=self._src(ins, i, pidx), dst_ref=outs[i].at[me], **pair))
                if with_received:
                    received.append(pltpu.make_async_remote_copy(
                        src_ref=self._src(ins, i, pidx), dst_ref=outs[i].at[pidx], **pair))
        return local, sent, received

    def start(self, ins, outs, sems):
        local, sent, _ = self._copies(ins, outs, sems, False)
        for cp in local + sent:
            cp.start()

    def wait(self, ins, outs, sems):
        local, sent, received = self._copies(ins, outs, sems, True)
        for cp in received:
            cp.wait_recv()
        for cp in sent:
            cp.wait_send()
        for cp in local:
            cp.wait()


def exchange(name, arrays, kinds):
    ex = Exchange(arrays, kinds)
    n = ex.n

    def body(*refs):
        ins, outs, sems = refs[:n], refs[n:2 * n], refs[2 * n:]
        ex.start(ins, outs, sems)
        ex.wait(ins, outs, sems)

    return pl.pallas_call(
        body, name=name, out_shape=ex.out_shapes(), in_specs=[ANY_SPEC] * n, out_specs=[ANY_SPEC] * n,
        scratch_shapes=ex.scratch())(*arrays)


def all_gather(name, arrays):
    return exchange(name, arrays, [GATHER] * len(arrays))


def row_call(name, body, B, S, tm, rows, bvecs, vecs, res, row_outs, bacc, gacc, side=None):
    ns = S // tm
    T = B * S
    n_r, n_b, n_v, n_w = len(rows), len(bvecs), len(vecs), len(res)
    n_ro, n_ba, n_ga = len(row_outs), len(bacc), len(gacc)
    n_x = side.n if side is not None else 0

    def kern(*refs):
        i = 0
        r_refs = refs[i:i + n_r]; i += n_r
        b_refs = refs[i:i + n_b]; i += n_b
        v_refs = refs[i:i + n_v]; i += n_v
        w_hbm = refs[i:i + n_w]; i += n_w
        x_in = refs[i:i + n_x]; i += n_x
        ro_refs = refs[i:i + n_ro]; i += n_ro
        ba_refs = refs[i:i + n_ba]; i += n_ba
        ga_refs = refs[i:i + n_ga]; i += n_ga
        x_out = refs[i:i + n_x]; i += n_x
        w_vmem = refs[i:i + n_w]; i += n_w
        x_sems = refs[i:]
        b = pl.program_id(0)
        s = pl.program_id(1)
        first = jnp.logical_and(b == 0, s == 0)

        if n_x:
            @pl.when(first)
            def _start_side():
                side.start(x_in, x_out, x_sems)

        if n_w:
            @pl.when(first)
            def _load_weights():
                for h, v in zip(w_hbm, w_vmem):
                    pltpu.sync_copy(h, v)

        ro, ba, ga = body([r[...] for r in r_refs], [r[0] for r in b_refs], [r[...] for r in v_refs],
                          list(w_vmem))
        for ref, val in zip(ro_refs, ro):
            ref[...] = val.astype(ref.dtype)

        def accumulate(ref, val, is_first, idx):
            @pl.when(is_first)
            def _set():
                ref[idx] = val

            @pl.when(jnp.logical_not(is_first))
            def _add():
                ref[idx] = ref[idx] + val

        for ref, val in zip(ba_refs, ba):
            accumulate(ref, val, s == 0, 0)
        for ref, val in zip(ga_refs, ga):
            accumulate(ref, val, first, Ellipsis)

        if n_x:
            @pl.when(jnp.logical_and(b == B - 1, s == ns - 1))
            def _wait_side():
                side.wait(x_in, x_out, x_sems)

    row_arrays = [r[0] if isinstance(r, tuple) else r for r in rows]
    row_in_specs = [pl.BlockSpec(r[1], r[2]) if isinstance(r, tuple)
                    else pl.BlockSpec((tm, r.shape[1]), lambda b, s: (b * ns + s, 0)) for r in rows]
    ro_shapes = [jax.ShapeDtypeStruct(tuple(o[0]), o[1]) if len(o) == 4 else jax.ShapeDtypeStruct((T, o[0]), o[1])
                 for o in row_outs]
    ro_specs = [pl.BlockSpec(o[2], o[3]) if len(o) == 4 else pl.BlockSpec((tm, o[0]), lambda b, s: (b * ns + s, 0))
                for o in row_outs]
    in_specs = (row_in_specs
                + [pl.BlockSpec((1, 1, v.shape[2]), lambda b, s: (b, 0, 0)) for v in bvecs]
                + [pl.BlockSpec(v.shape, lambda b, s, nd=v.ndim: (0,) * nd) for v in vecs]
                + [ANY_SPEC] * (n_w + n_x))
    out_shape = (ro_shapes
                 + [jax.ShapeDtypeStruct((B, 1, w), F32) for w in bacc]
                 + [jax.ShapeDtypeStruct(tuple(sh), F32) for sh in gacc]
                 + (side.out_shapes() if n_x else []))
    out_specs = (ro_specs
                 + [pl.BlockSpec((1, 1, w), lambda b, s: (b, 0, 0)) for w in bacc]
                 + [pl.BlockSpec(tuple(sh), lambda b, s, nd=len(sh): (0,) * nd) for sh in gacc]
                 + [ANY_SPEC] * n_x)
    outs = pl.pallas_call(
        kern, name=name, grid=(B, ns), in_specs=in_specs, out_specs=out_specs, out_shape=out_shape,
        scratch_shapes=[pltpu.VMEM(w.shape, w.dtype) for w in res] + (side.scratch() if n_x else []),
        compiler_params=_params(("arbitrary", "arbitrary")),
    )(*row_arrays, *bvecs, *vecs, *res, *(side.arrays if n_x else []))
    n_acc = n_ro + n_ba + n_ga
    if n_x:
        return outs[:n_ro], outs[n_ro:n_ro + n_ba], outs[n_ro + n_ba:n_acc], outs[n_acc:]
    return outs[:n_ro], outs[n_ro:n_ro + n_ba], outs[n_ro + n_ba:]


def matmul_nn(name, at, b, side=None):
    K, T = at.shape
    N = b.shape[1]
    tk = min(K, 512)
    tn = N if N <= 1280 else (1024 if N % 1024 == 0 else 512)
    assert K % tk == 0 and N % tn == 0 and T % min(T, 2048) == 0, (K, N, T)
    tt = min(T, 2048)
    nt = T // tt
    n_x = side.n if side is not None else 0
    grid = (K // tk, N // tn, nt)

    def kern(*refs):
        a_ref, b_ref = refs[:2]
        x_in = refs[2:2 + n_x]
        o_ref = refs[2 + n_x]
        x_out = refs[3 + n_x:3 + 2 * n_x]
        acc_ref = refs[3 + 2 * n_x]
        x_sems = refs[4 + 2 * n_x:]
        i, j, t = pl.program_id(0), pl.program_id(1), pl.program_id(2)
        if n_x:
            @pl.when(jnp.logical_and(i == 0, jnp.logical_and(j == 0, t == 0)))
            def _start_side():
                side.start(x_in, x_out, x_sems)

        part = _dot(a_ref[...], b_ref[...])

        @pl.when(t == 0)
        def _set():
            acc_ref[...] = part

        @pl.when(t > 0)
        def _add():
            acc_ref[...] = acc_ref[...] + part

        @pl.when(t == nt - 1)
        def _write():
            o_ref[...] = acc_ref[...].astype(o_ref.dtype)

        if n_x:
            @pl.when(jnp.logical_and(i == grid[0] - 1, jnp.logical_and(j == grid[1] - 1, t == nt - 1)))
            def _wait_side():
                side.wait(x_in, x_out, x_sems)

    outs = pl.pallas_call(
        kern, name=name, grid=grid,
        in_specs=[pl.BlockSpec((tk, tt), lambda i, j, t: (i, t)), pl.BlockSpec((tt, tn), lambda i, j, t: (t, j))]
        + [ANY_SPEC] * n_x,
        out_specs=[pl.BlockSpec((tk, tn), lambda i, j, t: (i, j))] + [ANY_SPEC] * n_x,
        out_shape=[jax.ShapeDtypeStruct((K, N), BF16)] + (side.out_shapes() if n_x else []),
        scratch_shapes=[pltpu.VMEM((tk, tn), F32)] + (side.scratch() if n_x else []),
        compiler_params=_params(("arbitrary",) * 3 if n_x else ("parallel", "parallel", "arbitrary")),
    )(at, b, *(side.arrays if n_x else []))
    return outs if n_x else outs[0]


def _rope_consts():
    inv_freq = ROPE_BASE ** (-jnp.arange(0, QK_ROPE, 2, dtype=F32) / QK_ROPE)
    zeros64 = jnp.zeros((64,), F32)
    zeros32 = jnp.zeros((32,), F32)
    zeros16 = jnp.zeros((16,), F32)
    ones16 = jnp.ones((16,), F32)
    invf = jnp.concatenate([zeros64, inv_freq, inv_freq, zeros32])[None, :]
    m_lo = jnp.concatenate([zeros64, ones16, zeros16, zeros32])[None, :]
    m_hi = jnp.concatenate([zeros64, zeros16, ones16, zeros32])[None, :]
    return invf, m_lo, m_hi


def _rope_tables(pos, invf, m_lo, m_hi):
    ang = pos * invf
    return jnp.cos(ang), jnp.sin(ang) * m_lo, jnp.sin(ang) * m_hi


def _rope_tile(t, cs, s_lo, s_hi, sign):
    up = pltpu.roll(t, HEAD_PAD - 16, axis=1)
    down = pltpu.roll(t, 16, axis=1)
    return t * cs + sign * (down * s_hi - up * s_lo)


def _heads(ref, width):
    if len(ref.shape) == 2:
        return jnp.stack([ref[:, h * width:(h + 1) * width] for h in range(N_HEADS)], axis=0)
    return jnp.stack([ref[b, :, h * width:(h + 1) * width] for b in range(ref.shape[0]) for h in range(N_HEADS)],
                     axis=0)


def _bmm(a, b, ca, cb):
    return lax.dot_general(a, b, (((ca,), (cb,)), ((0,), (0,))), preferred_element_type=F32)


LOG2E = math.log2(math.e)
EXP2_SCALE = ATTN_SCALE * LOG2E


def _scores_t(k_ref, q_ref, masked, tq, key_offset=0):
    st = _bmm(_heads(k_ref, HEAD_PAD), _heads(q_ref, HEAD_PAD), 2, 2)
    if masked:
        tk = st.shape[1]
        key = lax.broadcasted_iota(jnp.int32, (tk, tq), 0) + key_offset
        qry = lax.broadcasted_iota(jnp.int32, (tk, tq), 1)
        st = jnp.where((key <= qry)[None], st, NEG)
    return st


def attention_fwd(q, k, vt, B, S, tq, side):
    tk = tq
    tq = min(2 * tk, S)
    r = tq // tk
    nq = S // tq
    ns = side.n
    nbh = B * N_HEADS
    pairs = [(i, j) for i in range(nq) for j in range(r * (i + 1))]
    n_pairs = len(pairs)
    q_tab = jnp.asarray([p[0] for p in pairs], jnp.int32)
    k_tab = jnp.asarray([p[1] for p in pairs], jnp.int32)

    def kern(q_tab_ref, k_tab_ref, *refs):
        q_ref, k_ref, vt_ref = refs[:3]
        side_in = refs[3:3 + ns]
        o_ref, lse_ref = refs[3 + ns:5 + ns]
        side_out = refs[5 + ns:5 + 2 * ns]
        m_scr, l_scr, acc_scr = refs[5 + 2 * ns:8 + 2 * ns]
        sems = refs[8 + 2 * ns:]
        t = pl.program_id(0)
        qi, ki = q_tab_ref[t], k_tab_ref[t]

        @pl.when(t == 0)
        def _start_side():
            side.start(side_in, side_out, sems)

        @pl.when(ki == 0)
        def _init():
            m_scr[...] = jnp.full(m_scr.shape, NEG, F32)
            l_scr[...] = jnp.zeros(l_scr.shape, F32)
            acc_scr[...] = jnp.zeros(acc_scr.shape, F32)

        def block(masked):
            st = _scores_t(k_ref, q_ref, masked, tq, ki * tk - qi * tq)
            m_prev = m_scr[...]
            m_new = jnp.maximum(m_prev, jnp.max(st, axis=1, keepdims=True))
            alpha = jnp.exp2((m_prev - m_new) * EXP2_SCALE)
            p = jnp.exp2((st - m_new) * EXP2_SCALE)
            l_scr[...] = alpha * l_scr[...] + jnp.sum(p, axis=1, keepdims=True)
            vt3 = vt_ref[...].reshape(nbh, V_HEAD, tk)
            acc_scr[...] = alpha * acc_scr[...] + _bmm(vt3, p.astype(BF16), 2, 1)
            m_scr[...] = m_new

        @pl.when(ki < r * qi)
        def _full():
            block(False)

        @pl.when(ki >= r * qi)
        def _diagonal():
            block(True)

        @pl.when(ki == r * (qi + 1) - 1)
        def _finish():
            ot = acc_scr[...] / l_scr[...]
            for b in range(B):
                for h in range(N_HEADS):
                    o_ref[b, :, h * V_HEAD:(h + 1) * V_HEAD] = ot[b * N_HEADS + h].T
            lse_ref[...] = m_scr[...] * ATTN_SCALE + jnp.log(l_scr[...])

        @pl.when(t == n_pairs - 1)
        def _wait_side():
            side.wait(side_in, side_out, sems)

    grid_spec = pltpu.PrefetchScalarGridSpec(
        num_scalar_prefetch=2, grid=(n_pairs,),
        in_specs=[pl.BlockSpec((B, tq, QK_W), lambda t, qt, kt: (0, qt[t], 0)),
                  pl.BlockSpec((B, tk, QK_W), lambda t, qt, kt: (0, kt[t], 0)),
                  pl.BlockSpec((B, V_W, tk), lambda t, qt, kt: (0, 0, kt[t]))] + [ANY_SPEC] * ns,
        out_specs=[pl.BlockSpec((B, tq, V_W), lambda t, qt, kt: (0, qt[t], 0)),
                   pl.BlockSpec((nbh, 1, tq), lambda t, qt, kt: (0, 0, qt[t]))] + [ANY_SPEC] * ns,
        scratch_shapes=[pltpu.VMEM((nbh, 1, tq), F32), pltpu.VMEM((nbh, 1, tq), F32),
                        pltpu.VMEM((nbh, V_HEAD, tq), F32)] + side.scratch())
    return pl.pallas_call(
        kern, name="attention_fwd", grid_spec=grid_spec,
        out_shape=[jax.ShapeDtypeStruct((B, S, V_W), F32), jax.ShapeDtypeStruct((nbh, 1, S), F32)]
        + side.out_shapes(),
        compiler_params=_params(("arbitrary",)),
    )(q_tab, k_tab, q, k, vt, *side.arrays)


def attention_bwd(q, k, kt, v, do, lse, delta, B, S, tq, side):
    tk = tq
    tq = min(2 * tk, S)
    r = tq // tk
    nk, nq = S // tk, S // tq
    ns = side.n
    pairs = [(j, i) for j in range(nk) for i in range(j // r, nq)]
    n_pairs = len(pairs)
    k_tab = jnp.asarray([p[0] for p in pairs], jnp.int32)
    q_tab = jnp.asarray([p[1] for p in pairs], jnp.int32)

    def kern(k_tab_ref, q_tab_ref, *refs):
        q_ref, k_ref, kt_ref, v_ref, do_ref, lse_ref, delta_ref = refs[:7]
        side_in = refs[7:7 + ns]
        dq_hbm, dk_ref, dv_ref = refs[7 + ns:10 + ns]
        side_out = refs[10 + ns:10 + 2 * ns]
        dq_acc, dk_acc, dv_acc = refs[10 + 2 * ns:13 + 2 * ns]
        sems = refs[13 + 2 * ns:]
        b, t = pl.program_id(0), pl.program_id(1)
        kj, qi = k_tab_ref[t], q_tab_ref[t]
        on_diagonal = qi == kj // r

        @pl.when(jnp.logical_and(b == 0, t == 0))
        def _start_side():
            side.start(side_in, side_out, sems)

        @pl.when(t == 0)
        def _zero_dq():
            dq_acc[...] = jnp.zeros(dq_acc.shape, F32)

        @pl.when(on_diagonal)
        def _zero_dkv():
            dk_acc[...] = jnp.zeros(dk_acc.shape, F32)
            dv_acc[...] = jnp.zeros(dv_acc.shape, F32)

        def block(masked):
            st = _scores_t(k_ref, q_ref, masked, tq, kj * tk - qi * tq)
            pt = jnp.exp2(st * EXP2_SCALE - lse_ref[...] * LOG2E)
            do3 = _heads(do_ref, V_HEAD).astype(BF16)
            dv_acc[...] = dv_acc[...] + _bmm(pt.astype(BF16), do3, 2, 1)
            dpt = _bmm(_heads(v_ref, V_HEAD), do3, 2, 2)
            dst = (pt * (dpt - delta_ref[...]) * ATTN_SCALE).astype(BF16)
            dk_acc[...] = dk_acc[...] + _bmm(dst, _heads(q_ref, HEAD_PAD), 2, 1)
            q_cols = pl.ds(pl.multiple_of(qi * tq, tq), tq)
            kt3 = kt_ref[...].reshape(N_HEADS, HEAD_PAD, tk)
            dq_acc[:, :, q_cols] = dq_acc[:, :, q_cols] + _bmm(kt3, dst, 2, 1)

        @pl.when(jnp.logical_not(on_diagonal))
        def _full():
            block(False)

        @pl.when(on_diagonal)
        def _diagonal():
            block(True)

        @pl.when(qi == nq - 1)
        def _write_dkv():
            for h in range(N_HEADS):
                dk_ref[:, h * HEAD_PAD:(h + 1) * HEAD_PAD] = dk_acc[h]
                dv_ref[:, h * V_HEAD:(h + 1) * V_HEAD] = dv_acc[h]

        @pl.when(t == n_pairs - 1)
        def _write_dq():
            pltpu.sync_copy(dq_acc, dq_hbm.at[b])

        @pl.when(jnp.logical_and(b == B - 1, t == n_pairs - 1))
        def _wait_side():
            side.wait(side_in, side_out, sems)

    T = B * S
    qrow = lambda b, t, kt, qt: (b * nq + qt[t], 0)
    qcol3 = lambda b, t, kt, qt: (0, 0, b * nq + qt[t])
    krow = lambda b, t, kt, qt: (b * nk + kt[t], 0)
    grid_spec = pltpu.PrefetchScalarGridSpec(
        num_scalar_prefetch=2, grid=(B, n_pairs),
        in_specs=[pl.BlockSpec((tq, QK_W), qrow), pl.BlockSpec((tk, QK_W), krow),
                  pl.BlockSpec((QK_W, tk), lambda b, t, kt, qt: (0, b * nk + kt[t])), pl.BlockSpec((tk, V_W), krow),
                  pl.BlockSpec((tq, V_W), qrow), pl.BlockSpec((N_HEADS, 1, tq), qcol3),
                  pl.BlockSpec((N_HEADS, 1, tq), qcol3)] + [ANY_SPEC] * ns,
        out_specs=[ANY_SPEC, pl.BlockSpec((tk, QK_W), krow), pl.BlockSpec((tk, V_W), krow)] + [ANY_SPEC] * ns,
        scratch_shapes=[pltpu.VMEM((N_HEADS, HEAD_PAD, S), F32), pltpu.VMEM((N_HEADS, tk, HEAD_PAD), F32),
                        pltpu.VMEM((N_HEADS, tk, V_HEAD), F32)] + side.scratch())
    return pl.pallas_call(
        kern, name="attention_bwd", grid_spec=grid_spec,
        out_shape=[jax.ShapeDtypeStruct((B, N_HEADS, HEAD_PAD, S), F32), jax.ShapeDtypeStruct((T, QK_W), F32),
                   jax.ShapeDtypeStruct((T, V_W), F32)] + side.out_shapes(),
        compiler_params=_params(("arbitrary", "arbitrary")),
    )(k_tab, q_tab, q, k, kt, v, do, lse, delta, *side.arrays)


GB = 8
N_GB = N_GROUPS // GB
GB_U = GB * GROUP
GB_ST = GB * N_STATE


def _gb_u(j):
    return slice(j * GB_U, (j + 1) * GB_U)


def _gb_s(j):
    return slice(j * GB_ST, (j + 1) * GB_ST)


def ssm_param_fn(lr, li, ld, br, bi):
    dt = jnp.exp(ld)
    er = jnp.exp(lr * dt)
    ar = er * jnp.cos(li * dt)
    ai = er * jnp.sin(li * dt)
    nr = ar - 1.0
    den = lr * lr + li * li
    cr = (nr * lr + ai * li) / den
    ci = (ai * lr - nr * li) / den
    return ar, ai, cr * br - ci * bi, cr * bi + ci * br


def ssm_params_fwd(reps):
    def kern(lr, li, ld, br, bi, ar_o, ai_o, bbr_o, bbi_o):
        ar, ai, bbr, bbi = ssm_param_fn(lr[...], li[...], ld[...], br[...], bi[...])
        ar_o[...] = ar
        ai_o[...] = ai
        bbr_o[...] = bbr
        bbi_o[...] = bbi

    shp = jax.ShapeDtypeStruct(reps[0].shape, F32)
    return pl.pallas_call(kern, name="ssm_params_fwd", out_shape=[shp] * 4)(*reps)


def ssm_params_bwd(reps, cots):
    rows = reps[0].shape[0]

    def kern(lr, li, ld, br, bi, d_ar, d_ai, d_bbr, d_bbi, dlr_o, dli_o, dld_o, dbr_o, dbi_o):
        _, vjp = jax.vjp(ssm_param_fn, lr[...], li[...], ld[...], br[...], bi[...])
        dlr, dli, dld, dbr, dbi = vjp((d_ar[...], d_ai[...], d_bbr[...], d_bbi[...]))
        dlr_o[...] = jnp.sum(dlr.reshape(N_GROUPS, GROUP, N_STATE), axis=1)
        dli_o[...] = jnp.sum(dli.reshape(N_GROUPS, GROUP, N_STATE), axis=1)
        dld_o[...] = jnp.sum(jnp.sum(dld.reshape(N_GROUPS, GROUP, N_STATE), axis=1), axis=-1, keepdims=True)
        dbr_o[...] = dbr
        dbi_o[...] = dbi

    g = jax.ShapeDtypeStruct((N_GROUPS, N_STATE), F32)
    full = jax.ShapeDtypeStruct((rows, N_STATE), F32)
    return pl.pallas_call(
        kern, name="ssm_params_bwd",
        out_shape=[g, g, jax.ShapeDtypeStruct((N_GROUPS, 1), F32), full, full])(*reps, *cots)


def ssm_fwd(proj, b_re, b_im, c_re, c_im, d_vec, a_re, a_im, B, S, ts):
    ns = S // ts
    T = B * S

    def kern(u_ref, bre_ref, bim_ref, cre_ref, cim_ref, d_ref, are_ref, aim_ref,
             sre_ref, sim_ref, y_ref, car_re, car_im, bu_re, bu_im):
        s = pl.program_id(1)

        @pl.when(s == 0)
        def _reset():
            car_re[...] = jnp.zeros(car_re.shape, F32)
            car_im[...] = jnp.zeros(car_im.shape, F32)

        u = u_ref[...]
        ub = u.astype(BF16)
        for j in range(N_GB):
            uj, bj, sj = ub[:, _gb_u(j)], _gb_u(j), _gb_s(j)
            bu_re[:, sj] = _dot(uj, bre_ref[bj, :])
            bu_im[:, sj] = _dot(uj, bim_ref[bj, :])
        ar, ai = are_ref[...], aim_ref[...]

        def step(t, carry):
            xr, xi = carry
            row = pl.ds(t, 1)
            nr = ar * xr - ai * xi + bu_re[row, :]
            ni = ar * xi + ai * xr + bu_im[row, :]
            sre_ref[row, :] = nr
            sim_ref[row, :] = ni
            return nr, ni

        xr, xi = lax.fori_loop(0, ts, step, (car_re[0:1, :], car_im[0:1, :]))
        car_re[0:1, :] = xr
        car_im[0:1, :] = xi
        y = [_dot(sre_ref[:, _gb_s(j)].astype(BF16), cre_ref[_gb_s(j), :])
             - _dot(sim_ref[:, _gb_s(j)].astype(BF16), cim_ref[_gb_s(j), :]) for j in range(N_GB)]
        y_ref[...] = jnp.concatenate(y, axis=1) + d_ref[...] * u

    row = lambda b, s: (b * ns + s, 0)
    whole = lambda b, s: (0, 0)
    return pl.pallas_call(
        kern, name="ssm_fwd", grid=(B, ns),
        in_specs=[pl.BlockSpec((ts, D_SSM), row),
                  pl.BlockSpec((D_SSM, GB_ST), whole), pl.BlockSpec((D_SSM, GB_ST), whole),
                  pl.BlockSpec((N_ST, GB_U), whole), pl.BlockSpec((N_ST, GB_U), whole),
                  pl.BlockSpec((1, D_SSM), whole), pl.BlockSpec((1, N_ST), whole), pl.BlockSpec((1, N_ST), whole)],
        out_specs=[pl.BlockSpec((ts, N_ST), row), pl.BlockSpec((ts, N_ST), row), pl.BlockSpec((ts, D_SSM), row)],
        out_shape=[jax.ShapeDtypeStruct((T, N_ST), F32), jax.ShapeDtypeStruct((T, N_ST), F32),
                   jax.ShapeDtypeStruct((T, D_SSM), F32)],
        scratch_shapes=[pltpu.VMEM((8, N_ST), F32), pltpu.VMEM((8, N_ST), F32),
                        pltpu.VMEM((ts, N_ST), F32), pltpu.VMEM((ts, N_ST), F32)],
        compiler_params=_params(("arbitrary", "arbitrary")),
    )(proj, b_re, b_im, c_re, c_im, d_vec, a_re, a_im)


def ssm_bwd(dy, proj, s_re, s_im, b_re, b_im, c_re, c_im, d_vec, a_re, a_im, B, S, ts):
    ns = S // ts
    T = B * S
    t8 = ts // 8

    def kern(dy_ref, u_ref, sre_ref, sim_ref, pre_ref, pim_ref, bre_ref, bim_ref, cre_ref, cim_ref,
             d_ref, are_ref, aim_ref,
             du_ref, dcre_ref, dcim_ref, dbre_ref, dbim_ref, dare_ref, daim_ref, dd_ref,
             car_re, car_im, g_re, g_im):
        b, s = pl.program_id(0), pl.program_id(1)
        first = jnp.logical_and(b == 0, s == 0)

        @pl.when(s == 0)
        def _reset():
            car_re[...] = jnp.zeros(car_re.shape, F32)
            car_im[...] = jnp.zeros(car_im.shape, F32)

        dyv = dy_ref[...]
        dyb = dyv.astype(BF16)
        u = u_ref[...]
        for j in range(N_GB):
            g_re[:, _gb_s(j)] = _dot_nt(dyb[:, _gb_u(j)], cre_ref[_gb_s(j), :])
            g_im[:, _gb_s(j)] = -_dot_nt(dyb[:, _gb_u(j)], cim_ref[_gb_s(j), :])
        ar, ai = are_ref[...], aim_ref[...]

        def step(i, carry):
            gr_next, gi_next = carry
            row = pl.ds(ts - 1 - i, 1)
            gr = g_re[row, :] + ar * gr_next + ai * gi_next
            gi = g_im[row, :] + ar * gi_next - ai * gr_next
            g_re[row, :] = gr
            g_im[row, :] = gi
            return gr, gi

        gr0, gi0 = lax.fori_loop(0, ts, step, (car_re[0:1, :], car_im[0:1, :]))
        car_re[0:1, :] = gr0
        car_im[0:1, :] = gi0

        gr, gi = g_re[...], g_im[...]
        sr, si = sre_ref[...], sim_ref[...]
        has_prev = (s < ns - 1).astype(F32)
        row_id = lax.broadcasted_iota(jnp.int32, (ts, N_ST), 0)
        spr = jnp.where(row_id == 0, pre_ref[7:8, :] * has_prev, pltpu.roll(sr, 1, axis=0))
        spi = jnp.where(row_id == 0, pim_ref[7:8, :] * has_prev, pltpu.roll(si, 1, axis=0))
        grb, gib, ub = gr.astype(BF16), gi.astype(BF16), u.astype(BF16)
        srb, sib = sr.astype(BF16), si.astype(BF16)
        cat0 = lambda f: jnp.concatenate([f(j) for j in range(N_GB)], axis=0)
        cat1 = lambda f: jnp.concatenate([f(j) for j in range(N_GB)], axis=1)
        parts = [
            (dcre_ref, cat0(lambda j: _dot_tn(srb[:, _gb_s(j)], dyb[:, _gb_u(j)]))),
            (dcim_ref, cat0(lambda j: -_dot_tn(sib[:, _gb_s(j)], dyb[:, _gb_u(j)]))),
            (dbre_ref, cat0(lambda j: _dot_tn(ub[:, _gb_u(j)], grb[:, _gb_s(j)]))),
            (dbim_ref, cat0(lambda j: _dot_tn(ub[:, _gb_u(j)], gib[:, _gb_s(j)]))),
            (dare_ref, _colsum(gr * spr + gi * spi)),
            (daim_ref, _colsum(gi * spr - gr * spi)),
            (dd_ref, _colsum(dyv * u)),
        ]
        du = cat1(lambda j: _dot_nt(grb[:, _gb_s(j)], bre_ref[_gb_u(j), :])
                  + _dot_nt(gib[:, _gb_s(j)], bim_ref[_gb_u(j), :]))
        du_ref[...] = du + d_ref[...] * dyv

        @pl.when(first)
        def _set():
            for ref, val in parts:
                ref[...] = val

        @pl.when(jnp.logical_not(first))
        def _add():
            for ref, val in parts:
                ref[...] = ref[...] + val

    row = lambda b, s: (b * ns + (ns - 1 - s), 0)
    prev = lambda b, s: (jnp.maximum((b * ns + (ns - 1 - s)) * t8 - 1, 0), 0)
    whole = lambda b, s: (0, 0)
    acc_shapes = [(N_ST, GB_U), (N_ST, GB_U), (D_SSM, GB_ST), (D_SSM, GB_ST), (1, N_ST), (1, N_ST), (1, D_SSM)]
    return pl.pallas_call(
        kern, name="ssm_bwd", grid=(B, ns),
        in_specs=[pl.BlockSpec((ts, D_SSM), row), pl.BlockSpec((ts, D_SSM), row),
                  pl.BlockSpec((ts, N_ST), row), pl.BlockSpec((ts, N_ST), row),
                  pl.BlockSpec((8, N_ST), prev), pl.BlockSpec((8, N_ST), prev),
                  pl.BlockSpec((D_SSM, GB_ST), whole), pl.BlockSpec((D_SSM, GB_ST), whole),
                  pl.BlockSpec((N_ST, GB_U), whole), pl.BlockSpec((N_ST, GB_U), whole),
                  pl.BlockSpec((1, D_SSM), whole), pl.BlockSpec((1, N_ST), whole), pl.BlockSpec((1, N_ST), whole)],
        out_specs=[pl.BlockSpec((ts, D_SSM), row)] + [pl.BlockSpec(sh, whole) for sh in acc_shapes],
        out_shape=[jax.ShapeDtypeStruct((T, D_SSM), F32)] + [jax.ShapeDtypeStruct(sh, F32) for sh in acc_shapes],
        scratch_shapes=[pltpu.VMEM((8, N_ST), F32), pltpu.VMEM((8, N_ST), F32),
                        pltpu.VMEM((ts, N_ST), F32), pltpu.VMEM((ts, N_ST), F32)],
        compiler_params=_params(("arbitrary", "arbitrary")),
    )(dy, proj, s_re, s_im, s_re, s_im, b_re, b_im, c_re, c_im, d_vec, a_re, a_im)


def modulation_fwd(c_all, ada_w, ada_b, fada_w, fada_b):
    def kern(c_ref, w_ref, b_ref, fw_ref, fb_ref, cond_ref, mod_ref, fmod_ref):
        cv = c_ref[...]
        cond = (cv * _sigmoid(cv)).astype(BF16)
        cond_ref[...] = cond
        mod_ref[...] = _dot(cond, w_ref[...].astype(BF16)) + b_ref[...]
        fmod_ref[...] = _dot(cond, fw_ref[...].astype(BF16)) + fb_ref[...]

    n = c_all.shape[0]
    return pl.pallas_call(
        kern, name="modulation_fwd",
        out_shape=[jax.ShapeDtypeStruct(c_all.shape, BF16), jax.ShapeDtypeStruct((n, ada_w.shape[1]), F32),
                   jax.ShapeDtypeStruct((n, fada_w.shape[1]), F32)],
        compiler_params=_params(),
    )(c_all, ada_w, ada_b, fada_w, fada_b)


def modulation_bwd(cond_all, dmod, dfmod, dmod_all):
    def kern(c_ref, dm_ref, dfm_ref, all_ref, gw_ref, gfw_ref, gb_ref):
        cond = c_ref[...]
        gw_ref[...] = _dot_tn(cond, dm_ref[...].astype(BF16))
        gfw_ref[...] = _dot_tn(cond, dfm_ref[...].astype(BF16))
        gb_ref[...] = _colsum(all_ref[...].astype(F32))

    d = cond_all.shape[1]
    return pl.pallas_call(
        kern, name="modulation_bwd",
        out_shape=[jax.ShapeDtypeStruct((d, dmod.shape[1]), F32), jax.ShapeDtypeStruct((d, dfmod.shape[1]), F32),
                   jax.ShapeDtypeStruct((1, dmod_all.shape[1]), F32)],
        compiler_params=_params(),
    )(cond_all, dmod, dfmod, dmod_all)


def _adamw_update(p_ref, w_ref, m_ref, v_ref, g_o, d_o, m_o, v_o):
    g = p_ref[0].astype(F32)
    for i in range(1, p_ref.shape[0]):
        g = g + p_ref[i].astype(F32)
    m_new = ADAM_B1 * m_ref[...] + (1.0 - ADAM_B1) * g
    v_new = ADAM_B2 * v_ref[...] + (1.0 - ADAM_B2) * (g * g)
    m_hat = m_new / (1.0 - ADAM_B1 ** ADAM_STEP)
    v_hat = v_new / (1.0 - ADAM_B2 ** ADAM_STEP)
    g_o[...] = g
    d_o[...] = -ADAM_LR * (m_hat / (jnp.sqrt(v_hat) + ADAM_EPS) + ADAM_WD * w_ref[...])
    m_o[...] = m_new
    v_o[...] = v_new


def adamw_many(name, items):
    k = len(items)

    def kern(*refs):
        ins, outs = refs[:4 * k], refs[4 * k:]
        for i in range(k):
            _adamw_update(*ins[4 * i:4 * i + 4], *outs[4 * i:4 * i + 4])

    flat = [a for item in items for a in item]
    out_shape = [jax.ShapeDtypeStruct(item[1].shape, F32) for item in items for _ in range(4)]
    outs = pl.pallas_call(kern, name=name, out_shape=out_shape, compiler_params=_params())(*flat)
    return [tuple(outs[4 * i:4 * i + 4]) for i in range(k)]


def adamw(name, parts, w, m, v):
    n, R, C = parts.shape
    tr = R
    while n * tr * C * 4 > 4 * 1024 * 1024 and tr % 32 == 0:
        tr //= 2

    def kern(p_ref, w_ref, m_ref, v_ref, g_o, d_o, m_o, v_o):
        _adamw_update(p_ref, w_ref, m_ref, v_ref, g_o, d_o, m_o, v_o)

    blk = pl.BlockSpec((tr, C), lambda i: (i, 0))
    shp = jax.ShapeDtypeStruct((R, C), F32)
    return pl.pallas_call(
        kern, name=name, grid=(R // tr,),
        in_specs=[pl.BlockSpec((n, tr, C), lambda i: (0, i, 0)), blk, blk, blk],
        out_specs=[blk] * 4, out_shape=[shp] * 4,
        compiler_params=_params(("parallel",)),
    )(parts, w, m, v)


def _cols_from_gathered(g):
    return jnp.transpose(g, (1, 0, 2)).reshape(g.shape[1], N_DEV * g.shape[2])


def _cols_to_shards(w):
    k, n8 = w.shape
    return jnp.transpose(w.reshape(k, N_DEV, n8 // N_DEV), (1, 0, 2))


def _pad_w_in(w):
    z = functools.partial(jnp.zeros, dtype=w.dtype)
    k = w.shape[0]
    cut = D_SSM + Q_LORA + KV_LORA
    return jnp.concatenate([w[:, :cut], z((k, 64)), w[:, cut:], z((k, 32))], axis=1)


def _unpad_w_in(w):
    cut = D_SSM + Q_LORA + KV_LORA
    return jnp.concatenate([w[:, :cut], w[:, cut + 64:cut + 96]], axis=1)


def _pad_w_uq(w):
    k = w.shape[0]
    w3 = w.reshape(k, N_HEADS, QK_NOPE + QK_ROPE)
    return jnp.pad(w3, ((0, 0), (0, 0), (0, HEAD_PAD - QK_NOPE - QK_ROPE))).reshape(k, QK_W)


def _unpad_w_uq(w):
    k = w.shape[0]
    return w.reshape(k, N_HEADS, HEAD_PAD)[:, :, :QK_NOPE + QK_ROPE].reshape(k, N_HEADS * (QK_NOPE + QK_ROPE))


def _pad_w_ukv(w):
    k = w.shape[0]
    w3 = w.reshape(k, N_HEADS, QK_NOPE + V_HEAD)
    kpart = jnp.pad(w3[:, :, :QK_NOPE], ((0, 0), (0, 0), (0, HEAD_PAD - QK_NOPE))).reshape(k, QK_W)
    vpart = w3[:, :, QK_NOPE:].reshape(k, V_W)
    return jnp.concatenate([kpart, vpart], axis=1)


def _unpad_w_ukv(w):
    k = w.shape[0]
    kpart = w[:, :QK_W].reshape(k, N_HEADS, HEAD_PAD)[:, :, :QK_NOPE]
    vpart = w[:, QK_W:].reshape(k, N_HEADS, V_HEAD)
    return jnp.concatenate([kpart, vpart], axis=2).reshape(k, N_HEADS * (QK_NOPE + V_HEAD))


def _block_diag(blocks):
    g, r, c = blocks.shape
    b4 = blocks.reshape(N_GB, GB, r, c)
    keep = jnp.eye(GB, dtype=bool)[None, :, None, :, None]
    return jnp.where(keep, b4[:, :, :, None, :], 0).reshape(g * r, GB * c)


def _block_diag_take(stacked, r, c):
    d5 = stacked.reshape(N_GB, GB, r, GB, c)
    idx = jnp.arange(GB)
    return jnp.transpose(d5[:, idx, :, idx, :], (1, 0, 2, 3)).reshape(N_GROUPS, r, c)


SMALL = ["norm1_g", "ssm_lambda_re", "ssm_lambda_im", "ssm_b_re", "ssm_b_im", "ssm_c_re", "ssm_c_im",
         "ssm_d", "ssm_log_dt", "q_norm_g", "kv_norm_g", "ssm_out_g", "attn_out_g", "norm2_g", "final_norm_g"]
BIASES = ["ada_b", "final_ada_b"]


def _dense_2d(shape):
    dims = [d for d in shape[1:]] if len(shape) > 1 and shape[0] == 1 else list(shape)
    if len(dims) == 1:
        return (1, dims[0])
    return (dims[0], int(np.prod(dims[1:])))


def kernel(x, c, positions, ada_w, ada_b, norm1_g, w_in, ssm_lambda_re, ssm_lambda_im, ssm_b_re, ssm_b_im, ssm_c_re, ssm_c_im, ssm_d, ssm_log_dt, w_glu, q_norm_g, w_uq, kv_norm_g, w_ukv, ssm_out_g, attn_out_g, w_out, norm2_g, w_ff1, w_ff2, final_ada_w, final_ada_b, final_norm_g, loss_target, m_ada_w, m_ada_b, m_norm1_g, m_w_in, m_ssm_lambda_re, m_ssm_lambda_im, m_ssm_b_re, m_ssm_b_im, m_ssm_c_re, m_ssm_c_im, m_ssm_d, m_ssm_log_dt, m_w_glu, m_q_norm_g, m_w_uq, m_kv_norm_g, m_w_ukv, m_ssm_out_g, m_attn_out_g, m_w_out, m_norm2_g, m_w_ff1, m_w_ff2, m_final_ada_w, m_final_ada_b, m_final_norm_g, v_ada_w, v_ada_b, v_norm1_g, v_w_in, v_ssm_lambda_re, v_ssm_lambda_im, v_ssm_b_re, v_ssm_b_im, v_ssm_c_re, v_ssm_c_im, v_ssm_d, v_ssm_log_dt, v_w_glu, v_q_norm_g, v_w_uq, v_kv_norm_g, v_w_ukv, v_ssm_out_g, v_attn_out_g, v_w_out, v_norm2_g, v_w_ff1, v_w_ff2, v_final_ada_w, v_final_ada_b, v_final_norm_g):
    given = dict(locals())
    B, S, D = x.shape
    T = B * S
    tm = min(256, S)
    me = 4 * lax.axis_index("x") + 2 * lax.axis_index("y") + lax.axis_index("c")

    big = ["w_in", "w_glu", "w_uq", "w_ukv", "w_out", "w_ff1", "w_ff2"]
    shards = {n: given[n][0] for n in big}
    early, late = ["w_in", "w_uq", "w_ukv"], ["w_glu", "w_out", "w_ff1", "w_ff2"]
    c_g, w_in_g = all_gather("gather_first_weights", [c, shards["w_in"].astype(BF16)])
    c_all = c_g.reshape(N_DEV * B, D)
    w_in_p = _pad_w_in(_cols_from_gathered(w_in_g))

    n_mod = ada_w.shape[2]
    n_fmod = final_ada_w.shape[1]
    ada_b_sh = lax.dynamic_slice_in_dim(ada_b, me * n_mod, n_mod, axis=1)
    fada_b_sh = lax.dynamic_slice_in_dim(final_ada_b[None, :], me * n_fmod, n_fmod, axis=1)
    cond_all, mod_sh, fmod_sh = modulation_fwd(c_all, ada_w[0], ada_b_sh, final_ada_w, fada_b_sh)
    mod_g, fmod_g = all_gather("gather_modulation", [mod_sh, fmod_sh])
    mod_mine = lax.dynamic_slice_in_dim(_cols_from_gathered(mod_g), me * B, B, axis=0)
    fmod_mine = lax.dynamic_slice_in_dim(_cols_from_gathered(fmod_g), me * B, B, axis=0)
    shift1, scale1, gate1, shift2, scale2, gate2 = [mod_mine[:, None, i * D:(i + 1) * D] for i in range(6)]
    fshift, fscale = fmod_mine[:, None, :D], fmod_mine[:, None, D:]

    x2d = x.reshape(T, D)
    tgt2d = loss_target.reshape(T, D)
    pos = positions.astype(F32).reshape(T, 1)
    invf, m_lo, m_hi = _rope_consts()
    rope_mask = m_lo + m_hi

    def fwd_in(rows, bv, vecs, res):
        (xv,), (sc, sh), (g1,), (w,) = rows, bv, vecs, res
        r, xh = _rms_stats(xv)
        h = xh * g1 * (1.0 + sc) + sh
        return [_dot(h.astype(BF16), w[...])], [], []

    qkv_gather = Exchange([shards["w_uq"].astype(BF16), shards["w_ukv"].astype(BF16)], [GATHER, GATHER])
    (proj,), _, _, (w_uq_g, w_ukv_g) = row_call(
        "fwd_in", fwd_in, B, S, tm, [x2d], [scale1, shift1], [norm1_g], [w_in_p], [(IN_PAD, F32)], [], [],
        side=qkv_gather)
    w_uq_p = _pad_w_uq(_cols_from_gathered(w_uq_g))
    w_ukv_p = _pad_w_ukv(_cols_from_gathered(w_ukv_g))

    def fwd_qkv(rows, bv, vecs, res):
        (pr, ps), (gq, gkv, fr, lo, hi), (wq, wkv) = rows, vecs, res
        cs, s_lo, s_hi = _rope_tables(ps, fr, lo, hi)
        _, qh = _rms_stats(pr[:, D_SSM:D_SSM + Q_LORA])
        _, kvh = _rms_stats(pr[:, D_SSM + Q_LORA:D_SSM + Q_LORA + KV_LORA])
        qf = _dot((qh * gq).astype(BF16), wq[...])
        kvf = _dot((kvh * gkv).astype(BF16), wkv[...])
        k_rope = _rope_tile(pr[:, IN_PAD - HEAD_PAD:], cs, s_lo, s_hi, 1.0)
        q_t, k_t = [], []
        for h in range(N_HEADS):
            hs = slice(h * HEAD_PAD, (h + 1) * HEAD_PAD)
            q_t.append(_rope_tile(qf[:, hs], cs, s_lo, s_hi, 1.0))
            k_t.append(kvf[:, hs] + k_rope)
        kf, vf = jnp.concatenate(k_t, axis=1), kvf[:, QK_W:]
        return [jnp.concatenate(q_t, axis=1), kf, vf, t_val(kf), t_val(vf)[None], cs, s_lo, s_hi], [], []

    ns = S // tm
    col_tile = lambda b, s: (0, b * ns + s)
    t_out = lambda w: ((w, T), BF16, (w, tm), col_tile)
    t_val = lambda v: v.astype(BF16).T
    (q_r, k_r, v_r, k_t, v_t, rope_cs, rope_lo, rope_hi), _, _ = row_call(
        "fwd_qkv", fwd_qkv, B, S, tm, [proj, pos], [], [q_norm_g, kv_norm_g, invf, m_lo, m_hi],
        [w_uq_p, w_ukv_p],
        [(QK_W, BF16), (QK_W, BF16), (V_W, BF16), ((QK_W, T), BF16, (QK_W, tm), col_tile),
         ((B, V_W, S), BF16, (1, V_W, tm), lambda b, s: (b, 0, s)), (HEAD_PAD, F32), (HEAD_PAD, F32),
         (HEAD_PAD, F32)], [], [])
    late_gather = Exchange([shards[n].astype(BF16) for n in late], [GATHER] * len(late))
    y_attn, lse, *late_g = attention_fwd(q_r.reshape(B, S, QK_W), k_r.reshape(B, S, QK_W), v_t, B, S, tm,
                                         late_gather)
    y_attn = y_attn.reshape(T, V_W)
    lse = jnp.transpose(lse.reshape(B, N_HEADS, 1, S), (1, 2, 0, 3)).reshape(N_HEADS, 1, T)
    gw = dict(zip(late, late_g))
    w_glu_f = _cols_from_gathered(gw["w_glu"])
    w_out_f = gw["w_out"].reshape(-1, D)
    w_ff1_f = _cols_from_gathered(gw["w_ff1"])
    w_ff2_f = gw["w_ff2"].reshape(-1, D)

    rep = lambda a: jnp.repeat(a, GROUP, axis=0)
    lam_rep = [rep(ssm_lambda_re[0]), rep(ssm_lambda_im[0]),
               rep(jnp.broadcast_to(ssm_log_dt[0][:, None], (N_GROUPS, N_STATE)))]
    bt = lambda a: jnp.transpose(a, (0, 2, 1)).reshape(D_SSM, N_STATE)
    reps = lam_rep + [bt(ssm_b_re[0]), bt(ssm_b_im[0])]
    a_re_rep, a_im_rep, bb_re, bb_im = ssm_params_fwd(reps)
    a_re = a_re_rep.reshape(N_GROUPS, GROUP, N_STATE)[:, 0, :].reshape(1, N_ST)
    a_im = a_im_rep.reshape(N_GROUPS, GROUP, N_STATE)[:, 0, :].reshape(1, N_ST)
    bbd_re = _block_diag(bb_re.reshape(N_GROUPS, GROUP, N_STATE)).astype(BF16)
    bbd_im = _block_diag(bb_im.reshape(N_GROUPS, GROUP, N_STATE)).astype(BF16)
    cbd_re = _block_diag(jnp.transpose(ssm_c_re[0], (0, 2, 1))).astype(BF16)
    cbd_im = _block_diag(jnp.transpose(ssm_c_im[0], (0, 2, 1))).astype(BF16)
    d_vec = ssm_d.reshape(1, D_SSM)
    st_re, st_im, y_pre = ssm_fwd(proj, bbd_re, bbd_im, cbd_re, cbd_im, d_vec, a_re, a_im, B, S, tm)

    def fwd_mix(rows, bv, vecs, res):
        (yp, ya, xv), (g1v,), (gso, gao), (wg, wo) = rows, bv, vecs, res
        z = _dot(_gelu(yp).astype(BF16), wg[...])
        y_ssm = z[:, :D_SSM] * _sigmoid(z[:, D_SSM:])
        _, sh = _rms_stats(y_ssm)
        _, ah = _rms_stats(ya)
        o = _dot((sh * gso).astype(BF16), wo[0:D_SSM, :]) + _dot((ah * gao).astype(BF16), wo[D_SSM:, :])
        return [z, o, xv + g1v * o], [], []

    (z, o, x2), _, _ = row_call("fwd_mix", fwd_mix, B, S, tm, [y_pre, y_attn, x2d], [gate1],
                                [ssm_out_g, attn_out_g], [w_glu_f, w_out_f],
                                [(2 * D_SSM, F32), (D, BF16), (D, F32)], [], [])

    def final_out(x3, gf, fsc, fsh):
        r3, xh3 = _rms_stats(x3)
        n3 = xh3 * gf
        return r3, xh3, n3, n3 * (1.0 + fsc) + fsh

    def mlp_norm(xv, g2, sc, sh):
        r2, xh2 = _rms_stats(xv)
        n2 = xh2 * g2
        return r2, xh2, n2, n2 * (1.0 + sc) + sh

    def fwd_mlp(rows, bv, vecs, res):
        (xv, tg), (sc, sh, g2v, fsc, fsh), (g2, gf), (w1, w2) = rows, bv, vecs, res
        _, _, _, h2 = mlp_norm(xv, g2, sc, sh)
        ra = jnp.maximum(_dot(h2.astype(BF16), w1[...]), 0.0)
        ff = _dot((ra * ra).astype(BF16), w2[...])
        x3 = xv + g2v * ff
        _, _, _, out = final_out(x3, gf, fsc, fsh)
        err = out - tg
        loss = 0.5 * jnp.sum(jnp.mean(err * err, axis=-1, keepdims=True), axis=0, keepdims=True)
        return [ff, x3, ra], [], [jnp.broadcast_to(loss, (1, 128))]

    fnorm_g = final_norm_g[None, :]
    (ff, x3, ra_b), _, (loss_acc,) = row_call(
        "fwd_mlp", fwd_mlp, B, S, tm, [x2, tgt2d], [scale2, shift2, gate2, fscale, fshift], [norm2_g, fnorm_g],
        [w_ff1_f, w_ff2_f], [(D, BF16), (D, F32), (4 * D, BF16)], [], [(1, 128)])
    loss = lax.psum(loss_acc[0, 0], ("x", "y", "c"))

    def bwd_mlp(rows, bv, vecs, res):
        (x3v, tg, x2v, ffv, rav), (sc, sh, g2v, fsc, fsh), (g2, gf), (w1, w2) = rows, bv, vecs, res
        r3, xh3, n3, out = final_out(x3v, gf, fsc, fsh)
        dout = (out - tg) * (1.0 / D)
        dn3 = dout * (1.0 + fsc)
        dx3 = _rms_bwd(dn3 * gf, xh3, r3)
        dff = dx3 * g2v
        r2, xh2, n2, h2 = mlp_norm(x2v, g2, sc, sh)
        ra = rav.astype(F32)
        dffb = dff.astype(BF16)
        da = _dot_nt(dffb, w2[...]) * (2.0 * ra)
        dab = da.astype(BF16)
        dh2 = _dot_nt(dab, w1[...])
        dn2 = dh2 * (1.0 + sc)
        dx2 = dx3 + _rms_bwd(dn2 * g2, xh2, r2)
        return ([dx2, t_val(h2), t_val(ra * ra), dab, dffb],
                [_colsum(dout), _colsum(dout * n3), _colsum(dx3 * ffv), _colsum(dh2), _colsum(dh2 * n2)],
                [_colsum(dn3 * xh3), _colsum(dn2 * xh2)])

    ((dx2, h2b, fb, dab, dffb), (d_fshift, d_fscale, d_gate2, d_shift2, d_scale2), (d_gf, d_g2)) = row_call(
        "bwd_mlp", bwd_mlp, B, S, tm, [x3, tgt2d, x2, ff, ra_b], [scale2, shift2, gate2, fscale, fshift],
        [norm2_g, fnorm_g], [w_ff1_f, w_ff2_f],
        [(D, F32), t_out(D), t_out(4 * D), (4 * D, BF16), (D, BF16)], [D] * 5, [(1, D), (1, D)])
    g_w_ff1 = matmul_nn("grad_w_ff1", h2b, dab)
    g_w_ff2 = matmul_nn("grad_w_ff2", fb, dffb)

    def bwd_mix(rows, bv, vecs, res):
        (dxv, ov, zv, ya, yp), (g1v,), (gso, gao), (wo, wg) = rows, bv, vecs, res
        do = (dxv * g1v).astype(BF16)
        dyn = _dot_nt(do, wo[...])
        z1, sg = zv[:, :D_SSM], _sigmoid(zv[:, D_SSM:])
        y_ssm = z1 * sg
        rs, sh = _rms_stats(y_ssm)
        ra, ah = _rms_stats(ya)
        dyn_s, dyn_a = dyn[:, :D_SSM], dyn[:, D_SSM:]
        dy_ssm = _rms_bwd(dyn_s * gso, sh, rs)
        dy_attn = _rms_bwd(dyn_a * gao, ah, ra)
        dz = jnp.concatenate([dy_ssm * sg, dy_ssm * z1 * sg * (1.0 - sg)], axis=1).astype(BF16)
        dy_pre = _dot_nt(dz, wg[...]) * _gelu_grad(yp)
        yn = jnp.concatenate([sh * gso, ah * gao], axis=1)
        delta = jnp.sum((dy_attn * ya).T.reshape(N_HEADS, V_HEAD, tm), axis=1, keepdims=True)
        return ([do, t_val(yn), dz, t_val(_gelu(yp)), dy_attn, dy_pre, delta], [_colsum(dxv * ov)],
                [_colsum(dyn_s * sh), _colsum(dyn_a * ah)])

    ((dob, ynb, dzb, ygb, dy_attn, dy_pre, delta), (d_gate1,), (d_gso, d_gao)) = row_call(
        "bwd_mix", bwd_mix, B, S, tm, [dx2, o, z, y_attn, y_pre], [gate1], [ssm_out_g, attn_out_g],
        [w_out_f, w_glu_f],
        [(D, BF16), t_out(D), (2 * D_SSM, BF16), t_out(D_SSM), (V_W, F32), (D_SSM, F32),
         ((N_HEADS, 1, T), F32, (N_HEADS, 1, tm), lambda b, s: (0, 0, b * ns + s))],
        [D], [(1, D_SSM), (1, V_W)])
    g_w_out = matmul_nn("grad_w_out", ynb, dob)
    g_w_glu = matmul_nn("grad_w_glu", ygb, dzb)

    grads_a = Exchange([g_w_glu, g_w_out.reshape(N_DEV, -1, D), g_w_ff1, g_w_ff2.reshape(N_DEV, -1, D)],
                       [w_glu.shape[2], None, w_ff1.shape[2], None])
    dq_t, dk_r, dv_r, *parts_a = attention_bwd(q_r, k_r, k_t, v_r, dy_attn, lse, delta, B, S, tm, grads_a)
    parts = dict(zip(late, parts_a))
    dq_t = dq_t.reshape(B, QK_W, S)

    def bwd_qkv(rows, bv, vecs, res):
        (dqt, dkv, dvv, pr, cs, s_lo, s_hi), (gq, gkv, rmask), (wq, wkv) = rows, vecs, res
        dqv = dqt[0].T
        dq_t = []
        dk_rope = jnp.zeros((dkv.shape[0], HEAD_PAD), F32)
        for h in range(N_HEADS):
            hs = slice(h * HEAD_PAD, (h + 1) * HEAD_PAD)
            dq_t.append(_rope_tile(dqv[:, hs], cs, s_lo, s_hi, -1.0))
            dk_rope = dk_rope + dkv[:, hs]
        dk_rope = _rope_tile(dk_rope * rmask, cs, s_lo, s_hi, -1.0)
        dqb = jnp.concatenate(dq_t, axis=1).astype(BF16)
        dkvb = jnp.concatenate([dkv, dvv], axis=1).astype(BF16)
        rq, qh = _rms_stats(pr[:, D_SSM:D_SSM + Q_LORA])
        rk, kvh = _rms_stats(pr[:, D_SSM + Q_LORA:D_SSM + Q_LORA + KV_LORA])
        dqn = _dot_nt(dqb, wq[...])
        dkvn = _dot_nt(dkvb, wkv[...])
        dpa = jnp.concatenate([_rms_bwd(dqn * gq, qh, rq), _rms_bwd(dkvn * gkv, kvh, rk), dk_rope], axis=1)
        return [dpa, t_val(qh * gq), t_val(kvh * gkv), dqb, dkvb], [], [_colsum(dqn * qh), _colsum(dkvn * kvh)]

    ((dpa, qnb, kvnb, dqb, dkvb), _, (d_gq, d_gkv)) = row_call(
        "bwd_qkv", bwd_qkv, B, S, tm,
        [(dq_t, (1, QK_W, tm), lambda b, s: (b, 0, s)), dk_r, dv_r, proj, rope_cs, rope_lo, rope_hi], [],
        [q_norm_g, kv_norm_g, rope_mask], [w_uq_p, w_ukv_p],
        [(Q_LORA + KV_LORA + HEAD_PAD, F32), t_out(Q_LORA), t_out(KV_LORA), (QK_W, BF16), (QK_W + V_W, BF16)],
        [], [(1, Q_LORA), (1, KV_LORA)])
    g_w_uq = _unpad_w_uq(matmul_nn("grad_w_uq", qnb, dqb))
    g_w_ukv = _unpad_w_ukv(matmul_nn("grad_w_ukv", kvnb, dkvb))

    du, dc_re_d, dc_im_d, db_re_d, db_im_d, da_re, da_im, d_d = ssm_bwd(
        dy_pre, proj, st_re, st_im, bbd_re, bbd_im, cbd_re, cbd_im, d_vec, a_re, a_im, B, S, tm)
    place = lambda a: jnp.zeros((N_GROUPS, GROUP, N_STATE), F32).at[:, 0, :].set(
        a.reshape(N_GROUPS, N_STATE)).reshape(D_SSM, N_STATE)
    cots = [place(da_re), place(da_im),
            _block_diag_take(db_re_d, GROUP, N_STATE).reshape(D_SSM, N_STATE),
            _block_diag_take(db_im_d, GROUP, N_STATE).reshape(D_SSM, N_STATE)]
    g_lam_re, g_lam_im, g_log_dt, g_bt_re, g_bt_im = ssm_params_bwd(reps, cots)
    unbt = lambda a: jnp.transpose(a.reshape(N_GROUPS, GROUP, N_STATE), (0, 2, 1))
    g_c_re = jnp.transpose(_block_diag_take(dc_re_d, N_STATE, GROUP), (0, 2, 1))
    g_c_im = jnp.transpose(_block_diag_take(dc_im_d, N_STATE, GROUP), (0, 2, 1))

    def bwd_in(rows, bv, vecs, res):
        (duv, dpav, xv, dx2v), (sc, sh), (g1,), (w,) = rows, bv, vecs, res
        dproj = jnp.concatenate([duv, dpav], axis=1).astype(BF16)
        dh = _dot_nt(dproj, w[...])
        r, xh = _rms_stats(xv)
        n1 = xh * g1
        dn1 = dh * (1.0 + sc)
        dx = dx2v + _rms_bwd(dn1 * g1, xh, r)
        return [dx, t_val(n1 * (1.0 + sc) + sh), dproj], [_colsum(dh), _colsum(dh * n1)], [_colsum(dn1 * xh)]

    small_part = {
        "ssm_lambda_re": g_lam_re,
        "ssm_lambda_im": g_lam_im, "ssm_b_re": unbt(g_bt_re), "ssm_b_im": unbt(g_bt_im), "ssm_c_re": g_c_re,
        "ssm_c_im": g_c_im, "ssm_d": d_d, "ssm_log_dt": g_log_dt, "q_norm_g": d_gq, "kv_norm_g": d_gkv,
        "ssm_out_g": d_gso, "attn_out_g": d_gao, "norm2_g": d_g2, "final_norm_g": d_gf}
    dense_bf16 = lambda n, g: g.reshape(_dense_2d(given[n].shape)).astype(BF16)
    ready = [n for n in SMALL if n in small_part]
    grads_b = Exchange([dense_bf16(n, small_part[n]) for n in ready] + [_cols_to_shards(g_w_uq), g_w_ukv],
                       [GATHER] * len(ready) + [None, w_ukv.shape[2]])
    ((grad_x, h1b, dprojb), (d_shift1, d_scale1), (d_g1,)) = row_call(
        "bwd_in", bwd_in, B, S, tm, [du, dpa, x2d, dx2], [scale1, shift1], [norm1_g], [w_in_p],
        [(D, F32), t_out(D), (IN_PAD, BF16)], [D, D], [(1, D)])
    g_w_in_p, *side_b = matmul_nn("grad_w_in", h1b, dprojb, side=grads_b)
    small_parts = dict(zip(ready, side_b[:len(ready)]))
    parts["w_uq"], parts["w_ukv"] = side_b[len(ready):]
    g_w_in = _unpad_w_in(g_w_in_p)

    dmod = jnp.concatenate([d_shift1, d_scale1, d_gate1, d_shift2, d_scale2, d_gate2, d_fshift, d_fscale],
                           axis=2).reshape(B, 8 * D)
    dmod_g, small_parts["norm1_g"], parts["w_in"] = exchange(
        "exchange_last_grads", [dmod.astype(BF16), dense_bf16("norm1_g", d_g1), _cols_to_shards(g_w_in)],
        [GATHER, GATHER, None])
    small_g = [small_parts[n] for n in SMALL]
    dmod_all = dmod_g.reshape(N_DEV * B, 8 * D)
    dm_sh = lax.dynamic_slice_in_dim(dmod_all[:, :6 * D], me * n_mod, n_mod, axis=1)
    dfm_sh = lax.dynamic_slice_in_dim(dmod_all[:, 6 * D:], me * n_fmod, n_fmod, axis=1)
    g_ada_w, g_fada_w, g_biases = modulation_bwd(cond_all, dm_sh, dfm_sh, dmod_all)
    parts["ada_w"] = g_ada_w[None]
    parts["final_ada_w"] = g_fada_w[None]

    res_g, res_d, res_m, res_v = {}, {}, {}, {}
    for n in ["ada_w"] + big + ["final_ada_w"]:
        w_full = given[n]
        w2 = w_full.reshape(w_full.shape[-2], w_full.shape[-1])
        out = adamw("adamw_" + n, parts[n], w2, given["m_" + n].reshape(w2.shape), given["v_" + n].reshape(w2.shape))
        res_g[n], res_d[n], res_m[n], res_v[n] = [a.reshape(w_full.shape) for a in out]
    bias_g = {"ada_b": g_biases[None, :, :6 * D], "final_ada_b": g_biases[None, :, 6 * D:]}
    names = SMALL + BIASES
    items = []
    for n, g_parts in zip(names, small_g + [bias_g[b] for b in BIASES]):
        dense = _dense_2d(given[n].shape)
        items.append((g_parts, given[n].reshape(dense), given["m_" + n].reshape(dense), given["v_" + n].reshape(dense)))
    for n, out in zip(names, adamw_many("adamw_small", items)):
        res_g[n], res_d[n], res_m[n], res_v[n] = [a.reshape(given[n].shape) for a in out]

    order = ["ada_w", "ada_b", "norm1_g", "w_in", "ssm_lambda_re", "ssm_lambda_im", "ssm_b_re", "ssm_b_im",
             "ssm_c_re", "ssm_c_im", "ssm_d", "ssm_log_dt", "w_glu", "q_norm_g", "w_uq", "kv_norm_g", "w_ukv",
             "ssm_out_g", "attn_out_g", "w_out", "norm2_g", "w_ff1", "w_ff2", "final_ada_w", "final_ada_b",
             "final_norm_g"]
    return (loss, grad_x.reshape(B, S, D), *[res_g[n] for n in order], *[res_d[n] for n in order],
            *[res_m[n] for n in order], *[res_v[n] for n in order])
```

```python
import functools
import math

import numpy as np
import jax
import jax.numpy as jnp
from jax import lax
from jax.experimental import pallas as pl
from jax.experimental.pallas import tpu as pltpu

F32 = jnp.float32
BF16 = jnp.bfloat16

N_DEV = 8
EPS = 1e-6
D_SSM = 512
N_GROUPS = 32
GROUP = 16
N_STATE = 64
N_ST = N_GROUPS * N_STATE
Q_LORA = 384
KV_LORA = 256
QK_NOPE = 64
QK_ROPE = 32
V_HEAD = 64
N_HEADS = 8
HEAD_PAD = 128
QK_W = N_HEADS * HEAD_PAD
V_W = N_HEADS * V_HEAD
IN_COLS = D_SSM + Q_LORA + KV_LORA + QK_ROPE
IN_PAD = D_SSM + Q_LORA + KV_LORA + HEAD_PAD
ROPE_BASE = 10000.0
ATTN_SCALE = (QK_NOPE + QK_ROPE) ** -0.5
NEG = -1e30

ADAM_LR = 0.001
ADAM_B1 = 0.9
ADAM_B2 = 0.999
ADAM_EPS = 1e-08
ADAM_WD = 0.01
ADAM_STEP = 10

VMEM_LIMIT = 56 * 1024 * 1024
MESH_ID = pl.DeviceIdType.MESH


def _dot(a, b):
    return jnp.dot(a, b, preferred_element_type=F32)


def _dot_nt(a, b):
    return lax.dot_general(a, b, (((1,), (1,)), ((), ())), preferred_element_type=F32)


def _dot_tn(a, b):
    return lax.dot_general(a, b, (((0,), (0,)), ((), ())), preferred_element_type=F32)


def _rms_stats(x):
    r = lax.rsqrt(jnp.mean(x * x, axis=-1, keepdims=True) + EPS)
    return r, x * r


def _rms_bwd(dy, xh, r):
    return r * (dy - xh * jnp.mean(dy * xh, axis=-1, keepdims=True))


def _sigmoid(x):
    return 1.0 / (1.0 + jnp.exp(-x))


_GELU_C = math.sqrt(2.0 / math.pi)


def _gelu(x):
    t = jnp.tanh(_GELU_C * (x + 0.044715 * x * x * x))
    return 0.5 * x * (1.0 + t)


def _gelu_grad(x):
    t = jnp.tanh(_GELU_C * (x + 0.044715 * x * x * x))
    return 0.5 * (1.0 + t) + 0.5 * x * (1.0 - t * t) * _GELU_C * (1.0 + 3.0 * 0.044715 * x * x)


def _colsum(a):
    return jnp.sum(a, axis=0, keepdims=True)


def _params(sem=None):
    return pltpu.CompilerParams(dimension_semantics=sem, vmem_limit_bytes=VMEM_LIMIT)


def _mesh_pos():
    x, y, c = lax.axis_index("x"), lax.axis_index("y"), lax.axis_index("c")
    return x, y, c, 4 * x + 2 * y + c


def _peer(x, y, c, k):
    px = 1 - x if k & 4 else x
    py = 1 - y if k & 2 else y
    pc = 1 - c if k & 1 else c
    return (px, py, pc), 4 * px + 2 * py + pc


GATHER = "gather"
ANY_SPEC = pl.BlockSpec(memory_space=pl.ANY)


class Exchange:
    def __init__(self, arrays, kinds):
        self.arrays, self.kinds, self.n = list(arrays), list(kinds), len(arrays)

    def _shard(self, i):
        a, kind = self.arrays[i], self.kinds[i]
        if kind == GATHER:
            return a.shape
        return a.shape[1:] if kind is None else (a.shape[0], kind)

    def out_shapes(self):
        return [jax.ShapeDtypeStruct((N_DEV,) + tuple(self._shard(i)), self.arrays[i].dtype) for i in range(self.n)]

    def scratch(self):
        return [pltpu.SemaphoreType.DMA((self.n, N_DEV - 1)), pltpu.SemaphoreType.DMA((self.n, N_DEV - 1)),
                pltpu.SemaphoreType.DMA((self.n,))]

    def _src(self, ins, i, j):
        kind = self.kinds[i]
        if kind == GATHER:
            return ins[i]
        if kind is None:
            return ins[i].at[j]
        return ins[i].at[:, pl.ds(pl.multiple_of(j * kind, 128), kind)]

    def _copies(self, ins, outs, sems, with_received):
        send_sems, recv_sems, loc_sems = sems
        x, y, c, me = _mesh_pos()
        local, sent, received = [], [], []
        for i in range(self.n):
            local.append(pltpu.make_async_copy(self._src(ins, i, me), outs[i].at[me], loc_sems.at[i]))
            for k in range(1, N_DEV):
                peer, pidx = _peer(x, y, c, k)
                pair = dict(send_sem=send_sems.at[i, k - 1], recv_sem=recv_sems.at[i, k - 1], device_id=peer,
                            device_id_type=MESH_ID)
                sent.append(pltpu.make_async_remote_copy(
                    src_ref=self._src(ins, i, pidx), dst_ref=outs[i].at[me], **pair))
                if with_received:
                    received.append(pltpu.make_async_remote_copy(
                        src_ref=self._src(ins, i, pidx), dst_ref=outs[i].at[pidx], **pair))
        return local, sent, received

    def start(self, ins, outs, sems):
        local, sent, _ = self._copies(ins, outs, sems, False)
        for cp in local + sent:
            cp.start()

    def wait(self, ins, outs, sems):
        local, sent, received = self._copies(ins, outs, sems, True)
        for cp in received:
            cp.wait_recv()
        for cp in sent:
            cp.wait_send()
        for cp in local:
            cp.wait()


def exchange(name, arrays, kinds):
    ex = Exchange(arrays, kinds)
    n = ex.n

    def body(*refs):
        ins, outs, sems = refs[:n], refs[n:2 * n], refs[2 * n:]
        ex.start(ins, outs, sems)
        ex.wait(ins, outs, sems)

    return pl.pallas_call(
        body, name=name, out_shape=ex.out_shapes(), in_specs=[ANY_SPEC] * n, out_specs=[ANY_SPEC] * n,
        scratch_shapes=ex.scratch())(*arrays)


def all_gather(name, arrays):
    return exchange(name, arrays, [GATHER] * len(arrays))


def row_call(name, body, B, S, tm, rows, bvecs, vecs, res, row_outs, bacc, gacc, side=None):
    ns = S // tm
    T = B * S
    n_r, n_b, n_v, n_w = len(rows), len(bvecs), len(vecs), len(res)
    n_ro, n_ba, n_ga = len(row_outs), len(bacc), len(gacc)
    n_x = side.n if side is not None else 0

    def kern(*refs):
        i = 0
        r_refs = refs[i:i + n_r]; i += n_r
        b_refs = refs[i:i + n_b]; i += n_b
        v_refs = refs[i:i + n_v]; i += n_v
        w_hbm = refs[i:i + n_w]; i += n_w
        x_in = refs[i:i + n_x]; i += n_x
        ro_refs = refs[i:i + n_ro]; i += n_ro
        ba_refs = refs[i:i + n_ba]; i += n_ba
        ga_refs = refs[i:i + n_ga]; i += n_ga
        x_out = refs[i:i + n_x]; i += n_x
        w_vmem = refs[i:i + n_w]; i += n_w
        x_sems = refs[i:]
        b = pl.program_id(0)
        s = pl.program_id(1)
        first = jnp.logical_and(b == 0, s == 0)

        if n_x:
            @pl.when(first)
            def _start_side():
                side.start(x_in, x_out, x_sems)

        if n_w:
            @pl.when(first)
            def _load_weights():
                for h, v in zip(w_hbm, w_vmem):
                    pltpu.sync_copy(h, v)

        ro, ba, ga = body([r[...] for r in r_refs], [r[0] for r in b_refs], [r[...] for r in v_refs],
                          list(w_vmem))
        for ref, val in zip(ro_refs, ro):
            ref[...] = val.astype(ref.dtype)

        def accumulate(ref, val, is_first, idx):
            @pl.when(is_first)
            def _set():
                ref[idx] = val

            @pl.when(jnp.logical_not(is_first))
            def _add():
                ref[idx] = ref[idx] + val

        for ref, val in zip(ba_refs, ba):
            accumulate(ref, val, s == 0, 0)
        for ref, val in zip(ga_refs, ga):
            accumulate(ref, val, first, Ellipsis)

        if n_x:
            @pl.when(jnp.logical_and(b == B - 1, s == ns - 1))
            def _wait_side():
                side.wait(x_in, x_out, x_sems)

    row_arrays = [r[0] if isinstance(r, tuple) else r for r in rows]
    row_in_specs = [pl.BlockSpec(r[1], r[2]) if isinstance(r, tuple)
                    else pl.BlockSpec((tm, r.shape[1]), lambda b, s: (b * ns + s, 0)) for r in rows]
    ro_shapes = [jax.ShapeDtypeStruct(tuple(o[0]), o[1]) if len(o) == 4 else jax.ShapeDtypeStruct((T, o[0]), o[1])
                 for o in row_outs]
    ro_specs = [pl.BlockSpec(o[2], o[3]) if len(o) == 4 else pl.BlockSpec((tm, o[0]), lambda b, s: (b * ns + s, 0))
                for o in row_outs]
    in_specs = (row_in_specs
                + [pl.BlockSpec((1, 1, v.shape[2]), lambda b, s: (b, 0, 0)) for v in bvecs]
                + [pl.BlockSpec(v.shape, lambda b, s, nd=v.ndim: (0,) * nd) for v in vecs]
                + [ANY_SPEC] * (n_w + n_x))
    out_shape = (ro_shapes
                 + [jax.ShapeDtypeStruct((B, 1, w), F32) for w in bacc]
                 + [jax.ShapeDtypeStruct(tuple(sh), F32) for sh in gacc]
                 + (side.out_shapes() if n_x else []))
    out_specs = (ro_specs
                 + [pl.BlockSpec((1, 1, w), lambda b, s: (b, 0, 0)) for w in bacc]
                 + [pl.BlockSpec(tuple(sh), lambda b, s, nd=len(sh): (0,) * nd) for sh in gacc]
                 + [ANY_SPEC] * n_x)
    outs = pl.pallas_call(
        kern, name=name, grid=(B, ns), in_specs=in_specs, out_specs=out_specs, out_shape=out_shape,
        scratch_shapes=[pltpu.VMEM(w.shape, w.dtype) for w in res] + (side.scratch() if n_x else []),
        compiler_params=_params(("arbitrary", "arbitrary")),
    )(*row_arrays, *bvecs, *vecs, *res, *(side.arrays if n_x else []))
    n_acc = n_ro + n_ba + n_ga
    if n_x:
        return outs[:n_ro], outs[n_ro:n_ro + n_ba], outs[n_ro + n_ba:n_acc], outs[n_acc:]
    return outs[:n_ro], outs[n_ro:n_ro + n_ba], outs[n_ro + n_ba:]


def matmul_nn(name, at, b, side=None):
    K, T = at.shape
    N = b.shape[1]
    tk = 1024 if K % 1024 == 0 else min(K, 512)
    tn = N if N <= 1280 else (1024 if N % 1024 == 0 else 512)
    assert K % tk == 0 and N % tn == 0 and T % min(T, 2048) == 0, (K, N, T)
    tt = min(T, 2048)
    nt = T // tt
    n_x = side.n if side is not None else 0
    grid = (K // tk, N // tn, nt)

    def kern(*refs):
        a_ref, b_ref = refs[:2]
        x_in = refs[2:2 + n_x]
        o_ref = refs[2 + n_x]
        x_out = refs[3 + n_x:3 + 2 * n_x]
        acc_ref = refs[3 + 2 * n_x]
        x_sems = refs[4 + 2 * n_x:]
        i, j, t = pl.program_id(0), pl.program_id(1), pl.program_id(2)
        if n_x:
            @pl.when(jnp.logical_and(i == 0, jnp.logical_and(j == 0, t == 0)))
            def _start_side():
                side.start(x_in, x_out, x_sems)

        part = _dot(a_ref[...], b_ref[...])

        @pl.when(t == 0)
        def _set():
            acc_ref[...] = part

        @pl.when(t > 0)
        def _add():
            acc_ref[...] = acc_ref[...] + part

        @pl.when(t == nt - 1)
        def _write():
            o_ref[...] = acc_ref[...].astype(o_ref.dtype)

        if n_x:
            @pl.when(jnp.logical_and(i == grid[0] - 1, jnp.logical_and(j == grid[1] - 1, t == nt - 1)))
            def _wait_side():
                side.wait(x_in, x_out, x_sems)

    outs = pl.pallas_call(
        kern, name=name, grid=grid,
        in_specs=[pl.BlockSpec((tk, tt), lambda i, j, t: (i, t)), pl.BlockSpec((tt, tn), lambda i, j, t: (t, j))]
        + [ANY_SPEC] * n_x,
        out_specs=[pl.BlockSpec((tk, tn), lambda i, j, t: (i, j))] + [ANY_SPEC] * n_x,
        out_shape=[jax.ShapeDtypeStruct((K, N), BF16)] + (side.out_shapes() if n_x else []),
        scratch_shapes=[pltpu.VMEM((tk, tn), F32)] + (side.scratch() if n_x else []),
        compiler_params=_params(("arbitrary",) * 3 if n_x else ("parallel", "parallel", "arbitrary")),
    )(at, b, *(side.arrays if n_x else []))
    return outs if n_x else outs[0]


def _rope_consts():
    inv_freq = ROPE_BASE ** (-jnp.arange(0, QK_ROPE, 2, dtype=F32) / QK_ROPE)
    zeros64 = jnp.zeros((64,), F32)
    zeros32 = jnp.zeros((32,), F32)
    zeros16 = jnp.zeros((16,), F32)
    ones16 = jnp.ones((16,), F32)
    invf = jnp.concatenate([zeros64, inv_freq, inv_freq, zeros32])[None, :]
    m_lo = jnp.concatenate([zeros64, ones16, zeros16, zeros32])[None, :]
    m_hi = jnp.concatenate([zeros64, zeros16, ones16, zeros32])[None, :]
    return invf, m_lo, m_hi


def _rope_tables(pos, invf, m_lo, m_hi):
    ang = pos * invf
    return jnp.cos(ang), jnp.sin(ang) * m_lo, jnp.sin(ang) * m_hi


def _rope_tile(t, cs, s_lo, s_hi, sign):
    up = pltpu.roll(t, HEAD_PAD - 16, axis=1)
    down = pltpu.roll(t, 16, axis=1)
    return t * cs + sign * (down * s_hi - up * s_lo)


def _heads(ref, width):
    if len(ref.shape) == 2:
        return jnp.stack([ref[:, h * width:(h + 1) * width] for h in range(N_HEADS)], axis=0)
    return jnp.stack([ref[b, :, h * width:(h + 1) * width] for b in range(ref.shape[0]) for h in range(N_HEADS)],
                     axis=0)


def _bmm(a, b, ca, cb):
    return lax.dot_general(a, b, (((ca,), (cb,)), ((0,), (0,))), preferred_element_type=F32)


LOG2E = math.log2(math.e)
EXP2_SCALE = ATTN_SCALE * LOG2E


def _scores_t(k_ref, q_ref, masked, tq, key_offset=0):
    st = _bmm(_heads(k_ref, HEAD_PAD), _heads(q_ref, HEAD_PAD), 2, 2)
    if masked:
        tk = st.shape[1]
        key = lax.broadcasted_iota(jnp.int32, (tk, tq), 0) + key_offset
        qry = lax.broadcasted_iota(jnp.int32, (tk, tq), 1)
        st = jnp.where((key <= qry)[None], st, NEG)
    return st


def attention_fwd(q, k, vt, B, S, tq, side):
    tk = tq
    tq = min(2 * tk, S)
    r = tq // tk
    nq = S // tq
    ns = side.n
    nbh = B * N_HEADS
    pairs = [(i, j) for i in range(nq) for j in range(r * (i + 1))]
    n_pairs = len(pairs)
    q_tab = jnp.asarray([p[0] for p in pairs], jnp.int32)
    k_tab = jnp.asarray([p[1] for p in pairs], jnp.int32)

    def kern(q_tab_ref, k_tab_ref, *refs):
        q_ref, k_ref, vt_ref = refs[:3]
        side_in = refs[3:3 + ns]
        o_ref, lse_ref = refs[3 + ns:5 + ns]
        side_out = refs[5 + ns:5 + 2 * ns]
        m_scr, l_scr, acc_scr = refs[5 + 2 * ns:8 + 2 * ns]
        sems = refs[8 + 2 * ns:]
        t = pl.program_id(0)
        qi, ki = q_tab_ref[t], k_tab_ref[t]

        @pl.when(t == 0)
        def _start_side():
            side.start(side_in, side_out, sems)

        @pl.when(ki == 0)
        def _init():
            m_scr[...] = jnp.full(m_scr.shape, NEG, F32)
            l_scr[...] = jnp.zeros(l_scr.shape, F32)
            acc_scr[...] = jnp.zeros(acc_scr.shape, F32)

        def block(masked):
            st = _scores_t(k_ref, q_ref, masked, tq, ki * tk - qi * tq)
            m_prev = m_scr[...]
            m_new = jnp.maximum(m_prev, jnp.max(st, axis=1, keepdims=True))
            alpha = jnp.exp2((m_prev - m_new) * EXP2_SCALE)
            p = jnp.exp2((st - m_new) * EXP2_SCALE)
            l_scr[...] = alpha * l_scr[...] + jnp.sum(p, axis=1, keepdims=True)
            vt3 = vt_ref[...].reshape(nbh, V_HEAD, tk)
            acc_scr[...] = alpha * acc_scr[...] + _bmm(vt3, p.astype(BF16), 2, 1)
            m_scr[...] = m_new

        @pl.when(ki < r * qi)
        def _full():
            block(False)

        @pl.when(ki >= r * qi)
        def _diagonal():
            block(True)

        @pl.when(ki == r * (qi + 1) - 1)
        def _finish():
            ot = acc_scr[...] / l_scr[...]
            for b in range(B):
                for h in range(N_HEADS):
                    o_ref[b, :, h * V_HEAD:(h + 1) * V_HEAD] = ot[b * N_HEADS + h].T
            lse_ref[...] = m_scr[...] * ATTN_SCALE + jnp.log(l_scr[...])

        @pl.when(t == n_pairs - 1)
        def _wait_side():
            side.wait(side_in, side_out, sems)

    grid_spec = pltpu.PrefetchScalarGridSpec(
        num_scalar_prefetch=2, grid=(n_pairs,),
        in_specs=[pl.BlockSpec((B, tq, QK_W), lambda t, qt, kt: (0, qt[t], 0)),
                  pl.BlockSpec((B, tk, QK_W), lambda t, qt, kt: (0, kt[t], 0)),
                  pl.BlockSpec((B, V_W, tk), lambda t, qt, kt: (0, 0, kt[t]))] + [ANY_SPEC] * ns,
        out_specs=[pl.BlockSpec((B, tq, V_W), lambda t, qt, kt: (0, qt[t], 0)),
                   pl.BlockSpec((nbh, 1, tq), lambda t, qt, kt: (0, 0, qt[t]))] + [ANY_SPEC] * ns,
        scratch_shapes=[pltpu.VMEM((nbh, 1, tq), F32), pltpu.VMEM((nbh, 1, tq), F32),
                        pltpu.VMEM((nbh, V_HEAD, tq), F32)] + side.scratch())
    return pl.pallas_call(
        kern, name="attention_fwd", grid_spec=grid_spec,
        out_shape=[jax.ShapeDtypeStruct((B, S, V_W), F32), jax.ShapeDtypeStruct((nbh, 1, S), F32)]
        + side.out_shapes(),
        compiler_params=_params(("arbitrary",)),
    )(q_tab, k_tab, q, k, vt, *side.arrays)


def attention_bwd(q, k, kt, v, do, lse, delta, B, S, tq, side):
    tk = tq
    tq = min(2 * tk, S)
    r = tq // tk
    nk, nq = S // tk, S // tq
    ns = side.n
    pairs = [(j, i) for j in range(nk) for i in range(j // r, nq)]
    n_pairs = len(pairs)
    k_tab = jnp.asarray([p[0] for p in pairs], jnp.int32)
    q_tab = jnp.asarray([p[1] for p in pairs], jnp.int32)

    def kern(k_tab_ref, q_tab_ref, *refs):
        q_ref, k_ref, kt_ref, v_ref, do_ref, lse_ref, delta_ref = refs[:7]
        side_in = refs[7:7 + ns]
        dq_hbm, dk_ref, dv_ref = refs[7 + ns:10 + ns]
        side_out = refs[10 + ns:10 + 2 * ns]
        dq_acc, dk_acc, dv_acc = refs[10 + 2 * ns:13 + 2 * ns]
        sems = refs[13 + 2 * ns:]
        b, t = pl.program_id(0), pl.program_id(1)
        kj, qi = k_tab_ref[t], q_tab_ref[t]
        on_diagonal = qi == kj // r

        @pl.when(jnp.logical_and(b == 0, t == 0))
        def _start_side():
            side.start(side_in, side_out, sems)

        @pl.when(t == 0)
        def _zero_dq():
            dq_acc[...] = jnp.zeros(dq_acc.shape, F32)

        @pl.when(on_diagonal)
        def _zero_dkv():
            dk_acc[...] = jnp.zeros(dk_acc.shape, F32)
            dv_acc[...] = jnp.zeros(dv_acc.shape, F32)

        def block(masked):
            st = _scores_t(k_ref, q_ref, masked, tq, kj * tk - qi * tq)
            pt = jnp.exp2(st * EXP2_SCALE - lse_ref[...] * LOG2E)
            do3 = _heads(do_ref, V_HEAD).astype(BF16)
            dv_acc[...] = dv_acc[...] + _bmm(pt.astype(BF16), do3, 2, 1)
            dpt = _bmm(_heads(v_ref, V_HEAD), do3, 2, 2)
            dst = (pt * (dpt - delta_ref[...]) * ATTN_SCALE).astype(BF16)
            dk_acc[...] = dk_acc[...] + _bmm(dst, _heads(q_ref, HEAD_PAD), 2, 1)
            q_cols = pl.ds(pl.multiple_of(qi * tq, tq), tq)
            kt3 = kt_ref[...].reshape(N_HEADS, HEAD_PAD, tk)
            dq_acc[:, :, q_cols] = dq_acc[:, :, q_cols] + _bmm(kt3, dst, 2, 1)

        @pl.when(jnp.logical_not(on_diagonal))
        def _full():
            block(False)

        @pl.when(on_diagonal)
        def _diagonal():
            block(True)

        @pl.when(qi == nq - 1)
        def _write_dkv():
            for h in range(N_HEADS):
                dk_ref[:, h * HEAD_PAD:(h + 1) * HEAD_PAD] = dk_acc[h]
                dv_ref[:, h * V_HEAD:(h + 1) * V_HEAD] = dv_acc[h]

        @pl.when(t == n_pairs - 1)
        def _write_dq():
            pltpu.sync_copy(dq_acc, dq_hbm.at[b])

        @pl.when(jnp.logical_and(b == B - 1, t == n_pairs - 1))
        def _wait_side():
            side.wait(side_in, side_out, sems)

    T = B * S
    qrow = lambda b, t, kt, qt: (b * nq + qt[t], 0)
    qcol3 = lambda b, t, kt, qt: (0, 0, b * nq + qt[t])
    krow = lambda b, t, kt, qt: (b * nk + kt[t], 0)
    grid_spec = pltpu.PrefetchScalarGridSpec(
        num_scalar_prefetch=2, grid=(B, n_pairs),
        in_specs=[pl.BlockSpec((tq, QK_W), qrow), pl.BlockSpec((tk, QK_W), krow),
                  pl.BlockSpec((QK_W, tk), lambda b, t, kt, qt: (0, b * nk + kt[t])), pl.BlockSpec((tk, V_W), krow),
                  pl.BlockSpec((tq, V_W), qrow), pl.BlockSpec((N_HEADS, 1, tq), qcol3),
                  pl.BlockSpec((N_HEADS, 1, tq), qcol3)] + [ANY_SPEC] * ns,
        out_specs=[ANY_SPEC, pl.BlockSpec((tk, QK_W), krow), pl.BlockSpec((tk, V_W), krow)] + [ANY_SPEC] * ns,
        scratch_shapes=[pltpu.VMEM((N_HEADS, HEAD_PAD, S), F32), pltpu.VMEM((N_HEADS, tk, HEAD_PAD), F32),
                        pltpu.VMEM((N_HEADS, tk, V_HEAD), F32)] + side.scratch())
    return pl.pallas_call(
        kern, name="attention_bwd", grid_spec=grid_spec,
        out_shape=[jax.ShapeDtypeStruct((B, N_HEADS, HEAD_PAD, S), F32), jax.ShapeDtypeStruct((T, QK_W), F32),
                   jax.ShapeDtypeStruct((T, V_W), F32)] + side.out_shapes(),
        compiler_params=_params(("arbitrary", "arbitrary")),
    )(k_tab, q_tab, q, k, kt, v, do, lse, delta, *side.arrays)


GB = 8
N_GB = N_GROUPS // GB
GB_U = GB * GROUP
GB_ST = GB * N_STATE


def _gb_u(j):
    return slice(j * GB_U, (j + 1) * GB_U)


def _gb_s(j):
    return slice(j * GB_ST, (j + 1) * GB_ST)


def ssm_param_fn(lr, li, ld, br, bi):
    dt = jnp.exp(ld)
    er = jnp.exp(lr * dt)
    ar = er * jnp.cos(li * dt)
    ai = er * jnp.sin(li * dt)
    nr = ar - 1.0
    den = lr * lr + li * li
    cr = (nr * lr + ai * li) / den
    ci = (ai * lr - nr * li) / den
    return ar, ai, cr * br - ci * bi, cr * bi + ci * br


def ssm_params_fwd(reps):
    def kern(lr, li, ld, br, bi, ar_o, ai_o, bbr_o, bbi_o):
        ar, ai, bbr, bbi = ssm_param_fn(lr[...], li[...], ld[...], br[...], bi[...])
        ar_o[...] = ar
        ai_o[...] = ai
        bbr_o[...] = bbr
        bbi_o[...] = bbi

    shp = jax.ShapeDtypeStruct(reps[0].shape, F32)
    return pl.pallas_call(kern, name="ssm_params_fwd", out_shape=[shp] * 4)(*reps)


def ssm_params_bwd(reps, cots):
    rows = reps[0].shape[0]

    def kern(lr, li, ld, br, bi, d_ar, d_ai, d_bbr, d_bbi, dlr_o, dli_o, dld_o, dbr_o, dbi_o):
        _, vjp = jax.vjp(ssm_param_fn, lr[...], li[...], ld[...], br[...], bi[...])
        dlr, dli, dld, dbr, dbi = vjp((d_ar[...], d_ai[...], d_bbr[...], d_bbi[...]))
        dlr_o[...] = jnp.sum(dlr.reshape(N_GROUPS, GROUP, N_STATE), axis=1)
        dli_o[...] = jnp.sum(dli.reshape(N_GROUPS, GROUP, N_STATE), axis=1)
        dld_o[...] = jnp.sum(jnp.sum(dld.reshape(N_GROUPS, GROUP, N_STATE), axis=1), axis=-1, keepdims=True)
        dbr_o[...] = dbr
        dbi_o[...] = dbi

    g = jax.ShapeDtypeStruct((N_GROUPS, N_STATE), F32)
    full = jax.ShapeDtypeStruct((rows, N_STATE), F32)
    return pl.pallas_call(
        kern, name="ssm_params_bwd",
        out_shape=[g, g, jax.ShapeDtypeStruct((N_GROUPS, 1), F32), full, full])(*reps, *cots)


def ssm_fwd(proj, b_re, b_im, c_re, c_im, d_vec, a_re, a_im, B, S, ts):
    ns = S // ts
    T = B * S

    def kern(u_ref, bre_ref, bim_ref, cre_ref, cim_ref, d_ref, are_ref, aim_ref,
             sre_ref, sim_ref, y_ref, car_re, car_im, bu_re, bu_im):
        s = pl.program_id(1)

        @pl.when(s == 0)
        def _reset():
            car_re[...] = jnp.zeros(car_re.shape, F32)
            car_im[...] = jnp.zeros(car_im.shape, F32)

        u = u_ref[...]
        ub = u.astype(BF16)
        for j in range(N_GB):
            uj, bj, sj = ub[:, _gb_u(j)], _gb_u(j), _gb_s(j)
            bu_re[:, sj] = _dot(uj, bre_ref[bj, :])
            bu_im[:, sj] = _dot(uj, bim_ref[bj, :])
        ar, ai = are_ref[...], aim_ref[...]

        def step(t, carry):
            xr, xi = carry
            row = pl.ds(t, 1)
            nr = ar * xr - ai * xi + bu_re[row, :]
            ni = ar * xi + ai * xr + bu_im[row, :]
            sre_ref[row, :] = nr
            sim_ref[row, :] = ni
            return nr, ni

        xr, xi = lax.fori_loop(0, ts, step, (car_re[0:1, :], car_im[0:1, :]))
        car_re[0:1, :] = xr
        car_im[0:1, :] = xi
        y = [_dot(sre_ref[:, _gb_s(j)].astype(BF16), cre_ref[_gb_s(j), :])
             - _dot(sim_ref[:, _gb_s(j)].astype(BF16), cim_ref[_gb_s(j), :]) for j in range(N_GB)]
        y_ref[...] = jnp.concatenate(y, axis=1) + d_ref[...] * u

    row = lambda b, s: (b * ns + s, 0)
    whole = lambda b, s: (0, 0)
    return pl.pallas_call(
        kern, name="ssm_fwd", grid=(B, ns),
        in_specs=[pl.BlockSpec((ts, D_SSM), row),
                  pl.BlockSpec((D_SSM, GB_ST), whole), pl.BlockSpec((D_SSM, GB_ST), whole),
                  pl.BlockSpec((N_ST, GB_U), whole), pl.BlockSpec((N_ST, GB_U), whole),
                  pl.BlockSpec((1, D_SSM), whole), pl.BlockSpec((1, N_ST), whole), pl.BlockSpec((1, N_ST), whole)],
        out_specs=[pl.BlockSpec((ts, N_ST), row), pl.BlockSpec((ts, N_ST), row), pl.BlockSpec((ts, D_SSM), row)],
        out_shape=[jax.ShapeDtypeStruct((T, N_ST), F32), jax.ShapeDtypeStruct((T, N_ST), F32),
                   jax.ShapeDtypeStruct((T, D_SSM), F32)],
        scratch_shapes=[pltpu.VMEM((8, N_ST), F32), pltpu.VMEM((8, N_ST), F32),
                        pltpu.VMEM((ts, N_ST), F32), pltpu.VMEM((ts, N_ST), F32)],
        compiler_params=_params(("arbitrary", "arbitrary")),
    )(proj, b_re, b_im, c_re, c_im, d_vec, a_re, a_im)


def ssm_bwd(dy, proj, s_re, s_im, b_re, b_im, c_re, c_im, d_vec, a_re, a_im, B, S, ts):
    ns = S // ts
    T = B * S
    t8 = ts // 8

    def kern(dy_ref, u_ref, sre_ref, sim_ref, pre_ref, pim_ref, bre_ref, bim_ref, cre_ref, cim_ref,
             d_ref, are_ref, aim_ref,
             du_ref, dcre_ref, dcim_ref, dbre_ref, dbim_ref, dare_ref, daim_ref, dd_ref,
             car_re, car_im, g_re, g_im):
        b, s = pl.program_id(0), pl.program_id(1)
        first = jnp.logical_and(b == 0, s == 0)

        @pl.when(s == 0)
        def _reset():
            car_re[...] = jnp.zeros(car_re.shape, F32)
            car_im[...] = jnp.zeros(car_im.shape, F32)

        dyv = dy_ref[...]
        dyb = dyv.astype(BF16)
        u = u_ref[...]
        for j in range(N_GB):
            g_re[:, _gb_s(j)] = _dot_nt(dyb[:, _gb_u(j)], cre_ref[_gb_s(j), :])
            g_im[:, _gb_s(j)] = -_dot_nt(dyb[:, _gb_u(j)], cim_ref[_gb_s(j), :])
        ar, ai = are_ref[...], aim_ref[...]

        def step(i, carry):
            gr_next, gi_next = carry
            row = pl.ds(ts - 1 - i, 1)
            gr = g_re[row, :] + ar * gr_next + ai * gi_next
            gi = g_im[row, :] + ar * gi_next - ai * gr_next
            g_re[row, :] = gr
            g_im[row, :] = gi
            return gr, gi

        gr0, gi0 = lax.fori_loop(0, ts, step, (car_re[0:1, :], car_im[0:1, :]))
        car_re[0:1, :] = gr0
        car_im[0:1, :] = gi0

        gr, gi = g_re[...], g_im[...]
        sr, si = sre_ref[...], sim_ref[...]
        has_prev = (s < ns - 1).astype(F32)
        row_id = lax.broadcasted_iota(jnp.int32, (ts, N_ST), 0)
        spr = jnp.where(row_id == 0, pre_ref[7:8, :] * has_prev, pltpu.roll(sr, 1, axis=0))
        spi = jnp.where(row_id == 0, pim_ref[7:8, :] * has_prev, pltpu.roll(si, 1, axis=0))
        grb, gib, ub = gr.astype(BF16), gi.astype(BF16), u.astype(BF16)
        srb, sib = sr.astype(BF16), si.astype(BF16)
        cat0 = lambda f: jnp.concatenate([f(j) for j in range(N_GB)], axis=0)
        cat1 = lambda f: jnp.concatenate([f(j) for j in range(N_GB)], axis=1)
        parts = [
            (dcre_ref, cat0(lambda j: _dot_tn(srb[:, _gb_s(j)], dyb[:, _gb_u(j)]))),
            (dcim_ref, cat0(lambda j: -_dot_tn(sib[:, _gb_s(j)], dyb[:, _gb_u(j)]))),
            (dbre_ref, cat0(lambda j: _dot_tn(ub[:, _gb_u(j)], grb[:, _gb_s(j)]))),
            (dbim_ref, cat0(lambda j: _dot_tn(ub[:, _gb_u(j)], gib[:, _gb_s(j)]))),
            (dare_ref, _colsum(gr * spr + gi * spi)),
            (daim_ref, _colsum(gi * spr - gr * spi)),
            (dd_ref, _colsum(dyv * u)),
        ]
        du = cat1(lambda j: _dot_nt(grb[:, _gb_s(j)], bre_ref[_gb_u(j), :])
                  + _dot_nt(gib[:, _gb_s(j)], bim_ref[_gb_u(j), :]))
        du_ref[...] = du + d_ref[...] * dyv

        @pl.when(first)
        def _set():
            for ref, val in parts:
                ref[...] = val

        @pl.when(jnp.logical_not(first))
        def _add():
            for ref, val in parts:
                ref[...] = ref[...] + val

    row = lambda b, s: (b * ns + (ns - 1 - s), 0)
    prev = lambda b, s: (jnp.maximum((b * ns + (ns - 1 - s)) * t8 - 1, 0), 0)
    whole = lambda b, s: (0, 0)
    acc_shapes = [(N_ST, GB_U), (N_ST, GB_U), (D_SSM, GB_ST), (D_SSM, GB_ST), (1, N_ST), (1, N_ST), (1, D_SSM)]
    return pl.pallas_call(
        kern, name="ssm_bwd", grid=(B, ns),
        in_specs=[pl.BlockSpec((ts, D_SSM), row), pl.BlockSpec((ts, D_SSM), row),
                  pl.BlockSpec((ts, N_ST), row), pl.BlockSpec((ts, N_ST), row),
                  pl.BlockSpec((8, N_ST), prev), pl.BlockSpec((8, N_ST), prev),
                  pl.BlockSpec((D_SSM, GB_ST), whole), pl.BlockSpec((D_SSM, GB_ST), whole),
                  pl.BlockSpec((N_ST, GB_U), whole), pl.BlockSpec((N_ST, GB_U), whole),
                  pl.BlockSpec((1, D_SSM), whole), pl.BlockSpec((1, N_ST), whole), pl.BlockSpec((1, N_ST), whole)],
        out_specs=[pl.BlockSpec((ts, D_SSM), row)] + [pl.BlockSpec(sh, whole) for sh in acc_shapes],
        out_shape=[jax.ShapeDtypeStruct((T, D_SSM), F32)] + [jax.ShapeDtypeStruct(sh, F32) for sh in acc_shapes],
        scratch_shapes=[pltpu.VMEM((8, N_ST), F32), pltpu.VMEM((8, N_ST), F32),
                        pltpu.VMEM((ts, N_ST), F32), pltpu.VMEM((ts, N_ST), F32)],
        compiler_params=_params(("arbitrary", "arbitrary")),
    )(dy, proj, s_re, s_im, s_re, s_im, b_re, b_im, c_re, c_im, d_vec, a_re, a_im)


def modulation_fwd(c_all, ada_w, ada_b, fada_w, fada_b):
    def kern(c_ref, w_ref, b_ref, fw_ref, fb_ref, cond_ref, mod_ref, fmod_ref):
        cv = c_ref[...]
        cond = (cv * _sigmoid(cv)).astype(BF16)
        cond_ref[...] = cond
        mod_ref[...] = _dot(cond, w_ref[...].astype(BF16)) + b_ref[...]
        fmod_ref[...] = _dot(cond, fw_ref[...].astype(BF16)) + fb_ref[...]

    n = c_all.shape[0]
    return pl.pallas_call(
        kern, name="modulation_fwd",
        out_shape=[jax.ShapeDtypeStruct(c_all.shape, BF16), jax.ShapeDtypeStruct((n, ada_w.shape[1]), F32),
                   jax.ShapeDtypeStruct((n, fada_w.shape[1]), F32)],
        compiler_params=_params(),
    )(c_all, ada_w, ada_b, fada_w, fada_b)


def modulation_bwd(cond_all, dmod, dfmod, dmod_all):
    def kern(c_ref, dm_ref, dfm_ref, all_ref, gw_ref, gfw_ref, gb_ref):
        cond = c_ref[...]
        gw_ref[...] = _dot_tn(cond, dm_ref[...].astype(BF16))
        gfw_ref[...] = _dot_tn(cond, dfm_ref[...].astype(BF16))
        gb_ref[...] = _colsum(all_ref[...].astype(F32))

    d = cond_all.shape[1]
    return pl.pallas_call(
        kern, name="modulation_bwd",
        out_shape=[jax.ShapeDtypeStruct((d, dmod.shape[1]), F32), jax.ShapeDtypeStruct((d, dfmod.shape[1]), F32),
                   jax.ShapeDtypeStruct((1, dmod_all.shape[1]), F32)],
        compiler_params=_params(),
    )(cond_all, dmod, dfmod, dmod_all)


def _adamw_update(p_ref, w_ref, m_ref, v_ref, g_o, d_o, m_o, v_o):
    g = p_ref[0].astype(F32)
    for i in range(1, p_ref.shape[0]):
        g = g + p_ref[i].astype(F32)
    m_new = ADAM_B1 * m_ref[...] + (1.0 - ADAM_B1) * g
    v_new = ADAM_B2 * v_ref[...] + (1.0 - ADAM_B2) * (g * g)
    m_hat = m_new / (1.0 - ADAM_B1 ** ADAM_STEP)
    v_hat = v_new / (1.0 - ADAM_B2 ** ADAM_STEP)
    g_o[...] = g
    d_o[...] = -ADAM_LR * (m_hat / (jnp.sqrt(v_hat) + ADAM_EPS) + ADAM_WD * w_ref[...])
    m_o[...] = m_new
    v_o[...] = v_new


def adamw_many(name, items):
    k = len(items)

    def kern(*refs):
        ins, outs = refs[:4 * k], refs[4 * k:]
        for i in range(k):
            _adamw_update(*ins[4 * i:4 * i + 4], *outs[4 * i:4 * i + 4])

    flat = [a for item in items for a in item]
    out_shape = [jax.ShapeDtypeStruct(item[1].shape, F32) for item in items for _ in range(4)]
    outs = pl.pallas_call(kern, name=name, out_shape=out_shape, compiler_params=_params())(*flat)
    return [tuple(outs[4 * i:4 * i + 4]) for i in range(k)]


def adamw(name, parts, w, m, v):
    n, R, C = parts.shape
    tr = R
    while n * tr * C * 4 > 4 * 1024 * 1024 and tr % 32 == 0:
        tr //= 2

    def kern(p_ref, w_ref, m_ref, v_ref, g_o, d_o, m_o, v_o):
        _adamw_update(p_ref, w_ref, m_ref, v_ref, g_o, d_o, m_o, v_o)

    blk = pl.BlockSpec((tr, C), lambda i: (i, 0))
    shp = jax.ShapeDtypeStruct((R, C), F32)
    return pl.pallas_call(
        kern, name=name, grid=(R // tr,),
        in_specs=[pl.BlockSpec((n, tr, C), lambda i: (0, i, 0)), blk, blk, blk],
        out_specs=[blk] * 4, out_shape=[shp] * 4,
        compiler_params=_params(("parallel",)),
    )(parts, w, m, v)


def _cols_from_gathered(g):
    return jnp.transpose(g, (1, 0, 2)).reshape(g.shape[1], N_DEV * g.shape[2])


def _cols_to_shards(w):
    k, n8 = w.shape
    return jnp.transpose(w.reshape(k, N_DEV, n8 // N_DEV), (1, 0, 2))


def _pad_w_in(w):
    z = functools.partial(jnp.zeros, dtype=w.dtype)
    k = w.shape[0]
    cut = D_SSM + Q_LORA + KV_LORA
    return jnp.concatenate([w[:, :cut], z((k, 64)), w[:, cut:], z((k, 32))], axis=1)


def _unpad_w_in(w):
    cut = D_SSM + Q_LORA + KV_LORA
    return jnp.concatenate([w[:, :cut], w[:, cut + 64:cut + 96]], axis=1)


def _pad_w_uq(w):
    k = w.shape[0]
    w3 = w.reshape(k, N_HEADS, QK_NOPE + QK_ROPE)
    return jnp.pad(w3, ((0, 0), (0, 0), (0, HEAD_PAD - QK_NOPE - QK_ROPE))).reshape(k, QK_W)


def _unpad_w_uq(w):
    k = w.shape[0]
    return w.reshape(k, N_HEADS, HEAD_PAD)[:, :, :QK_NOPE + QK_ROPE].reshape(k, N_HEADS * (QK_NOPE + QK_ROPE))


def _pad_w_ukv(w):
    k = w.shape[0]
    w3 = w.reshape(k, N_HEADS, QK_NOPE + V_HEAD)
    kpart = jnp.pad(w3[:, :, :QK_NOPE], ((0, 0), (0, 0), (0, HEAD_PAD - QK_NOPE))).reshape(k, QK_W)
    vpart = w3[:, :, QK_NOPE:].reshape(k, V_W)
    return jnp.concatenate([kpart, vpart], axis=1)


def _unpad_w_ukv(w):
    k = w.shape[0]
    kpart = w[:, :QK_W].reshape(k, N_HEADS, HEAD_PAD)[:, :, :QK_NOPE]
    vpart = w[:, QK_W:].reshape(k, N_HEADS, V_HEAD)
    return jnp.concatenate([kpart, vpart], axis=2).reshape(k, N_HEADS * (QK_NOPE + V_HEAD))


def _block_diag(blocks):
    g, r, c = blocks.shape
    b4 = blocks.reshape(N_GB, GB, r, c)
    keep = jnp.eye(GB, dtype=bool)[None, :, None, :, None]
    return jnp.where(keep, b4[:, :, :, None, :], 0).reshape(g * r, GB * c)


def _block_diag_take(stacked, r, c):
    d5 = stacked.reshape(N_GB, GB, r, GB, c)
    idx = jnp.arange(GB)
    return jnp.transpose(d5[:, idx, :, idx, :], (1, 0, 2, 3)).reshape(N_GROUPS, r, c)


SMALL = ["norm1_g", "ssm_lambda_re", "ssm_lambda_im", "ssm_b_re", "ssm_b_im", "ssm_c_re", "ssm_c_im",
         "ssm_d", "ssm_log_dt", "q_norm_g", "kv_norm_g", "ssm_out_g", "attn_out_g", "norm2_g", "final_norm_g"]
BIASES = ["ada_b", "final_ada_b"]


def _dense_2d(shape):
    dims = [d for d in shape[1:]] if len(shape) > 1 and shape[0] == 1 else list(shape)
    if len(dims) == 1:
        return (1, dims[0])
    return (dims[0], int(np.prod(dims[1:])))


def kernel(x, c, positions, ada_w, ada_b, norm1_g, w_in, ssm_lambda_re, ssm_lambda_im, ssm_b_re, ssm_b_im, ssm_c_re, ssm_c_im, ssm_d, ssm_log_dt, w_glu, q_norm_g, w_uq, kv_norm_g, w_ukv, ssm_out_g, attn_out_g, w_out, norm2_g, w_ff1, w_ff2, final_ada_w, final_ada_b, final_norm_g, loss_target, m_ada_w, m_ada_b, m_norm1_g, m_w_in, m_ssm_lambda_re, m_ssm_lambda_im, m_ssm_b_re, m_ssm_b_im, m_ssm_c_re, m_ssm_c_im, m_ssm_d, m_ssm_log_dt, m_w_glu, m_q_norm_g, m_w_uq, m_kv_norm_g, m_w_ukv, m_ssm_out_g, m_attn_out_g, m_w_out, m_norm2_g, m_w_ff1, m_w_ff2, m_final_ada_w, m_final_ada_b, m_final_norm_g, v_ada_w, v_ada_b, v_norm1_g, v_w_in, v_ssm_lambda_re, v_ssm_lambda_im, v_ssm_b_re, v_ssm_b_im, v_ssm_c_re, v_ssm_c_im, v_ssm_d, v_ssm_log_dt, v_w_glu, v_q_norm_g, v_w_uq, v_kv_norm_g, v_w_ukv, v_ssm_out_g, v_attn_out_g, v_w_out, v_norm2_g, v_w_ff1, v_w_ff2, v_final_ada_w, v_final_ada_b, v_final_norm_g):
    given = dict(locals())
    B, S, D = x.shape
    T = B * S
    tm = min(256, S)
    me = 4 * lax.axis_index("x") + 2 * lax.axis_index("y") + lax.axis_index("c")

    big = ["w_in", "w_glu", "w_uq", "w_ukv", "w_out", "w_ff1", "w_ff2"]
    shards = {n: given[n][0] for n in big}
    early, late = ["w_in", "w_uq", "w_ukv"], ["w_glu", "w_out", "w_ff1", "w_ff2"]
    c_g, w_in_g = all_gather("gather_first_weights", [c, shards["w_in"].astype(BF16)])
    c_all = c_g.reshape(N_DEV * B, D)
    w_in_p = _pad_w_in(_cols_from_gathered(w_in_g))

    n_mod = ada_w.shape[2]
    n_fmod = final_ada_w.shape[1]
    ada_b_sh = lax.dynamic_slice_in_dim(ada_b, me * n_mod, n_mod, axis=1)
    fada_b_sh = lax.dynamic_slice_in_dim(final_ada_b[None, :], me * n_fmod, n_fmod, axis=1)
    cond_all, mod_sh, fmod_sh = modulation_fwd(c_all, ada_w[0], ada_b_sh, final_ada_w, fada_b_sh)
    mod_g, fmod_g = all_gather("gather_modulation", [mod_sh, fmod_sh])
    mod_mine = lax.dynamic_slice_in_dim(_cols_from_gathered(mod_g), me * B, B, axis=0)
    fmod_mine = lax.dynamic_slice_in_dim(_cols_from_gathered(fmod_g), me * B, B, axis=0)
    shift1, scale1, gate1, shift2, scale2, gate2 = [mod_mine[:, None, i * D:(i + 1) * D] for i in range(6)]
    fshift, fscale = fmod_mine[:, None, :D], fmod_mine[:, None, D:]

    x2d = x.reshape(T, D)
    tgt2d = loss_target.reshape(T, D)
    pos = positions.astype(F32).reshape(T, 1)
    invf, m_lo, m_hi = _rope_consts()
    rope_mask = m_lo + m_hi

    def fwd_in(rows, bv, vecs, res):
        (xv,), (sc, sh), (g1,), (w,) = rows, bv, vecs, res
        r, xh = _rms_stats(xv)
        h = xh * g1 * (1.0 + sc) + sh
        return [_dot(h.astype(BF16), w[...])], [], []

    qkv_gather = Exchange([shards["w_uq"].astype(BF16), shards["w_ukv"].astype(BF16)], [GATHER, GATHER])
    (proj,), _, _, (w_uq_g, w_ukv_g) = row_call(
        "fwd_in", fwd_in, B, S, tm, [x2d], [scale1, shift1], [norm1_g], [w_in_p], [(IN_PAD, F32)], [], [],
        side=qkv_gather)
    w_uq_p = _pad_w_uq(_cols_from_gathered(w_uq_g))
    w_ukv_p = _pad_w_ukv(_cols_from_gathered(w_ukv_g))

    def fwd_qkv(rows, bv, vecs, res):
        (pr, ps), (gq, gkv, fr, lo, hi), (wq, wkv) = rows, vecs, res
        cs, s_lo, s_hi = _rope_tables(ps, fr, lo, hi)
        _, qh = _rms_stats(pr[:, D_SSM:D_SSM + Q_LORA])
        _, kvh = _rms_stats(pr[:, D_SSM + Q_LORA:D_SSM + Q_LORA + KV_LORA])
        qf = _dot((qh * gq).astype(BF16), wq[...])
        kvf = _dot((kvh * gkv).astype(BF16), wkv[...])
        k_rope = _rope_tile(pr[:, IN_PAD - HEAD_PAD:], cs, s_lo, s_hi, 1.0)
        q_t, k_t = [], []
        for h in range(N_HEADS):
            hs = slice(h * HEAD_PAD, (h + 1) * HEAD_PAD)
            q_t.append(_rope_tile(qf[:, hs], cs, s_lo, s_hi, 1.0))
            k_t.append(kvf[:, hs] + k_rope)
        kf, vf = jnp.concatenate(k_t, axis=1), kvf[:, QK_W:]
        return [jnp.concatenate(q_t, axis=1), kf, vf, t_val(kf), t_val(vf)[None], cs, s_lo, s_hi], [], []

    ns = S // tm
    col_tile = lambda b, s: (0, b * ns + s)
    t_out = lambda w: ((w, T), BF16, (w, tm), col_tile)
    t_val = lambda v: v.astype(BF16).T
    (q_r, k_r, v_r, k_t, v_t, rope_cs, rope_lo, rope_hi), _, _ = row_call(
        "fwd_qkv", fwd_qkv, B, S, tm, [proj, pos], [], [q_norm_g, kv_norm_g, invf, m_lo, m_hi],
        [w_uq_p, w_ukv_p],
        [(QK_W, BF16), (QK_W, BF16), (V_W, BF16), ((QK_W, T), BF16, (QK_W, tm), col_tile),
         ((B, V_W, S), BF16, (1, V_W, tm), lambda b, s: (b, 0, s)), (HEAD_PAD, F32), (HEAD_PAD, F32),
         (HEAD_PAD, F32)], [], [])
    late_gather = Exchange([shards[n].astype(BF16) for n in late], [GATHER] * len(late))
    y_attn, lse, *late_g = attention_fwd(q_r.reshape(B, S, QK_W), k_r.reshape(B, S, QK_W), v_t, B, S, tm,
                                         late_gather)
    y_attn = y_attn.reshape(T, V_W)
    lse = jnp.transpose(lse.reshape(B, N_HEADS, 1, S), (1, 2, 0, 3)).reshape(N_HEADS, 1, T)
    gw = dict(zip(late, late_g))
    w_glu_f = _cols_from_gathered(gw["w_glu"])
    w_out_f = gw["w_out"].reshape(-1, D)
    w_ff1_f = _cols_from_gathered(gw["w_ff1"])
    w_ff2_f = gw["w_ff2"].reshape(-1, D)

    rep = lambda a: jnp.repeat(a, GROUP, axis=0)
    lam_rep = [rep(ssm_lambda_re[0]), rep(ssm_lambda_im[0]),
               rep(jnp.broadcast_to(ssm_log_dt[0][:, None], (N_GROUPS, N_STATE)))]
    bt = lambda a: jnp.transpose(a, (0, 2, 1)).reshape(D_SSM, N_STATE)
    reps = lam_rep + [bt(ssm_b_re[0]), bt(ssm_b_im[0])]
    a_re_rep, a_im_rep, bb_re, bb_im = ssm_params_fwd(reps)
    a_re = a_re_rep.reshape(N_GROUPS, GROUP, N_STATE)[:, 0, :].reshape(1, N_ST)
    a_im = a_im_rep.reshape(N_GROUPS, GROUP, N_STATE)[:, 0, :].reshape(1, N_ST)
    bbd_re = _block_diag(bb_re.reshape(N_GROUPS, GROUP, N_STATE)).astype(BF16)
    bbd_im = _block_diag(bb_im.reshape(N_GROUPS, GROUP, N_STATE)).astype(BF16)
    cbd_re = _block_diag(jnp.transpose(ssm_c_re[0], (0, 2, 1))).astype(BF16)
    cbd_im = _block_diag(jnp.transpose(ssm_c_im[0], (0, 2, 1))).astype(BF16)
    d_vec = ssm_d.reshape(1, D_SSM)
    st_re, st_im, y_pre = ssm_fwd(proj, bbd_re, bbd_im, cbd_re, cbd_im, d_vec, a_re, a_im, B, S, tm)

    def fwd_mix(rows, bv, vecs, res):
        (yp, ya, xv), (g1v,), (gso, gao), (wg, wo) = rows, bv, vecs, res
        z = _dot(_gelu(yp).astype(BF16), wg[...])
        y_ssm = z[:, :D_SSM] * _sigmoid(z[:, D_SSM:])
        _, sh = _rms_stats(y_ssm)
        _, ah = _rms_stats(ya)
        o = _dot((sh * gso).astype(BF16), wo[0:D_SSM, :]) + _dot((ah * gao).astype(BF16), wo[D_SSM:, :])
        return [z, o, xv + g1v * o], [], []

    (z, o, x2), _, _ = row_call("fwd_mix", fwd_mix, B, S, tm, [y_pre, y_attn, x2d], [gate1],
                                [ssm_out_g, attn_out_g], [w_glu_f, w_out_f],
                                [(2 * D_SSM, F32), (D, BF16), (D, F32)], [], [])

    def final_out(x3, gf, fsc, fsh):
        r3, xh3 = _rms_stats(x3)
        n3 = xh3 * gf
        return r3, xh3, n3, n3 * (1.0 + fsc) + fsh

    def mlp_norm(xv, g2, sc, sh):
        r2, xh2 = _rms_stats(xv)
        n2 = xh2 * g2
        return r2, xh2, n2, n2 * (1.0 + sc) + sh

    def fwd_mlp(rows, bv, vecs, res):
        (xv, tg), (sc, sh, g2v, fsc, fsh), (g2, gf), (w1, w2) = rows, bv, vecs, res
        _, _, _, h2 = mlp_norm(xv, g2, sc, sh)
        ra = jnp.maximum(_dot(h2.astype(BF16), w1[...]), 0.0)
        ff = _dot((ra * ra).astype(BF16), w2[...])
        x3 = xv + g2v * ff
        _, _, _, out = final_out(x3, gf, fsc, fsh)
        err = out - tg
        loss = 0.5 * jnp.sum(jnp.mean(err * err, axis=-1, keepdims=True), axis=0, keepdims=True)
        return [ff, x3, ra], [], [jnp.broadcast_to(loss, (1, 128))]

    fnorm_g = final_norm_g[None, :]
    (ff, x3, ra_b), _, (loss_acc,) = row_call(
        "fwd_mlp", fwd_mlp, B, S, tm, [x2, tgt2d], [scale2, shift2, gate2, fscale, fshift], [norm2_g, fnorm_g],
        [w_ff1_f, w_ff2_f], [(D, BF16), (D, F32), (4 * D, BF16)], [], [(1, 128)])
    loss = lax.psum(loss_acc[0, 0], ("x", "y", "c"))

    def bwd_mlp(rows, bv, vecs, res):
        (x3v, tg, x2v, ffv, rav), (sc, sh, g2v, fsc, fsh), (g2, gf), (w1, w2) = rows, bv, vecs, res
        r3, xh3, n3, out = final_out(x3v, gf, fsc, fsh)
        dout = (out - tg) * (1.0 / D)
        dn3 = dout * (1.0 + fsc)
        dx3 = _rms_bwd(dn3 * gf, xh3, r3)
        dff = dx3 * g2v
        r2, xh2, n2, h2 = mlp_norm(x2v, g2, sc, sh)
        ra = rav.astype(F32)
        dffb = dff.astype(BF16)
        da = _dot_nt(dffb, w2[...]) * (2.0 * ra)
        dab = da.astype(BF16)
        dh2 = _dot_nt(dab, w1[...])
        dn2 = dh2 * (1.0 + sc)
        dx2 = dx3 + _rms_bwd(dn2 * g2, xh2, r2)
        return ([dx2, t_val(h2), t_val(ra * ra), dab, dffb],
                [_colsum(dout), _colsum(dout * n3), _colsum(dx3 * ffv), _colsum(dh2), _colsum(dh2 * n2)],
                [_colsum(dn3 * xh3), _colsum(dn2 * xh2)])

    ((dx2, h2b, fb, dab, dffb), (d_fshift, d_fscale, d_gate2, d_shift2, d_scale2), (d_gf, d_g2)) = row_call(
        "bwd_mlp", bwd_mlp, B, S, tm, [x3, tgt2d, x2, ff, ra_b], [scale2, shift2, gate2, fscale, fshift],
        [norm2_g, fnorm_g], [w_ff1_f, w_ff2_f],
        [(D, F32), t_out(D), t_out(4 * D), (4 * D, BF16), (D, BF16)], [D] * 5, [(1, D), (1, D)])
    g_w_ff1 = matmul_nn("grad_w_ff1", h2b, dab)
    g_w_ff2 = matmul_nn("grad_w_ff2", fb, dffb)

    def bwd_mix(rows, bv, vecs, res):
        (dxv, ov, zv, ya, yp), (g1v,), (gso, gao), (wo, wg) = rows, bv, vecs, res
        do = (dxv * g1v).astype(BF16)
        dyn = _dot_nt(do, wo[...])
        z1, sg = zv[:, :D_SSM], _sigmoid(zv[:, D_SSM:])
        y_ssm = z1 * sg
        rs, sh = _rms_stats(y_ssm)
        ra, ah = _rms_stats(ya)
        dyn_s, dyn_a = dyn[:, :D_SSM], dyn[:, D_SSM:]
        dy_ssm = _rms_bwd(dyn_s * gso, sh, rs)
        dy_attn = _rms_bwd(dyn_a * gao, ah, ra)
        dz = jnp.concatenate([dy_ssm * sg, dy_ssm * z1 * sg * (1.0 - sg)], axis=1).astype(BF16)
        dy_pre = _dot_nt(dz, wg[...]) * _gelu_grad(yp)
        yn = jnp.concatenate([sh * gso, ah * gao], axis=1)
        delta = jnp.sum((dy_attn * ya).T.reshape(N_HEADS, V_HEAD, tm), axis=1, keepdims=True)
        return ([do, t_val(yn), dz, t_val(_gelu(yp)), dy_attn, dy_pre, delta], [_colsum(dxv * ov)],
                [_colsum(dyn_s * sh), _colsum(dyn_a * ah)])

    ((dob, ynb, dzb, ygb, dy_attn, dy_pre, delta), (d_gate1,), (d_gso, d_gao)) = row_call(
        "bwd_mix", bwd_mix, B, S, tm, [dx2, o, z, y_attn, y_pre], [gate1], [ssm_out_g, attn_out_g],
        [w_out_f, w_glu_f],
        [(D, BF16), t_out(D), (2 * D_SSM, BF16), t_out(D_SSM), (V_W, F32), (D_SSM, F32),
         ((N_HEADS, 1, T), F32, (N_HEADS, 1, tm), lambda b, s: (0, 0, b * ns + s))],
        [D], [(1, D_SSM), (1, V_W)])
    g_w_out = matmul_nn("grad_w_out", ynb, dob)
    g_w_glu = matmul_nn("grad_w_glu", ygb, dzb)

    grads_a = Exchange([g_w_glu, g_w_out.reshape(N_DEV, -1, D), g_w_ff1, g_w_ff2.reshape(N_DEV, -1, D)],
                       [w_glu.shape[2], None, w_ff1.shape[2], None])
    dq_t, dk_r, dv_r, *parts_a = attention_bwd(q_r, k_r, k_t, v_r, dy_attn, lse, delta, B, S, tm, grads_a)
    parts = dict(zip(late, parts_a))
    dq_t = dq_t.reshape(B, QK_W, S)

    def bwd_qkv(rows, bv, vecs, res):
        (dqt, dkv, dvv, pr, cs, s_lo, s_hi), (gq, gkv, rmask), (wq, wkv) = rows, vecs, res
        dqv = dqt[0].T
        dq_t = []
        dk_rope = jnp.zeros((dkv.shape[0], HEAD_PAD), F32)
        for h in range(N_HEADS):
            hs = slice(h * HEAD_PAD, (h + 1) * HEAD_PAD)
            dq_t.append(_rope_tile(dqv[:, hs], cs, s_lo, s_hi, -1.0))
            dk_rope = dk_rope + dkv[:, hs]
        dk_rope = _rope_tile(dk_rope * rmask, cs, s_lo, s_hi, -1.0)
        dqb = jnp.concatenate(dq_t, axis=1).astype(BF16)
        dkvb = jnp.concatenate([dkv, dvv], axis=1).astype(BF16)
        rq, qh = _rms_stats(pr[:, D_SSM:D_SSM + Q_LORA])
        rk, kvh = _rms_stats(pr[:, D_SSM + Q_LORA:D_SSM + Q_LORA + KV_LORA])
        dqn = _dot_nt(dqb, wq[...])
        dkvn = _dot_nt(dkvb, wkv[...])
        dpa = jnp.concatenate([_rms_bwd(dqn * gq, qh, rq), _rms_bwd(dkvn * gkv, kvh, rk), dk_rope], axis=1)
        return [dpa, t_val(qh * gq), t_val(kvh * gkv), dqb, dkvb], [], [_colsum(dqn * qh), _colsum(dkvn * kvh)]

    ((dpa, qnb, kvnb, dqb, dkvb), _, (d_gq, d_gkv)) = row_call(
        "bwd_qkv", bwd_qkv, B, S, tm,
        [(dq_t, (1, QK_W, tm), lambda b, s: (b, 0, s)), dk_r, dv_r, proj, rope_cs, rope_lo, rope_hi], [],
        [q_norm_g, kv_norm_g, rope_mask], [w_uq_p, w_ukv_p],
        [(Q_LORA + KV_LORA + HEAD_PAD, F32), t_out(Q_LORA), t_out(KV_LORA), (QK_W, BF16), (QK_W + V_W, BF16)],
        [], [(1, Q_LORA), (1, KV_LORA)])
    g_w_uq = _unpad_w_uq(matmul_nn("grad_w_uq", qnb, dqb))
    g_w_ukv = _unpad_w_ukv(matmul_nn("grad_w_ukv", kvnb, dkvb))

    du, dc_re_d, dc_im_d, db_re_d, db_im_d, da_re, da_im, d_d = ssm_bwd(
        dy_pre, proj, st_re, st_im, bbd_re, bbd_im, cbd_re, cbd_im, d_vec, a_re, a_im, B, S, tm)
    place = lambda a: jnp.zeros((N_GROUPS, GROUP, N_STATE), F32).at[:, 0, :].set(
        a.reshape(N_GROUPS, N_STATE)).reshape(D_SSM, N_STATE)
    cots = [place(da_re), place(da_im),
            _block_diag_take(db_re_d, GROUP, N_STATE).reshape(D_SSM, N_STATE),
            _block_diag_take(db_im_d, GROUP, N_STATE).reshape(D_SSM, N_STATE)]
    g_lam_re, g_lam_im, g_log_dt, g_bt_re, g_bt_im = ssm_params_bwd(reps, cots)
    unbt = lambda a: jnp.transpose(a.reshape(N_GROUPS, GROUP, N_STATE), (0, 2, 1))
    g_c_re = jnp.transpose(_block_diag_take(dc_re_d, N_STATE, GROUP), (0, 2, 1))
    g_c_im = jnp.transpose(_block_diag_take(dc_im_d, N_STATE, GROUP), (0, 2, 1))

    def bwd_in(rows, bv, vecs, res):
        (duv, dpav, xv, dx2v), (sc, sh), (g1,), (w,) = rows, bv, vecs, res
        dproj = jnp.concatenate([duv, dpav], axis=1).astype(BF16)
        dh = _dot_nt(dproj, w[...])
        r, xh = _rms_stats(xv)
        n1 = xh * g1
        dn1 = dh * (1.0 + sc)
        dx = dx2v + _rms_bwd(dn1 * g1, xh, r)
        return [dx, t_val(n1 * (1.0 + sc) + sh), dproj], [_colsum(dh), _colsum(dh * n1)], [_colsum(dn1 * xh)]

    small_part = {
        "ssm_lambda_re": g_lam_re,
        "ssm_lambda_im": g_lam_im, "ssm_b_re": unbt(g_bt_re), "ssm_b_im": unbt(g_bt_im), "ssm_c_re": g_c_re,
        "ssm_c_im": g_c_im, "ssm_d": d_d, "ssm_log_dt": g_log_dt, "q_norm_g": d_gq, "kv_norm_g": d_gkv,
        "ssm_out_g": d_gso, "attn_out_g": d_gao, "norm2_g": d_g2, "final_norm_g": d_gf}
    dense_bf16 = lambda n, g: g.reshape(_dense_2d(given[n].shape)).astype(BF16)
    ready = [n for n in SMALL if n in small_part]
    grads_b = Exchange([dense_bf16(n, small_part[n]) for n in ready] + [_cols_to_shards(g_w_uq), g_w_ukv],
                       [GATHER] * len(ready) + [None, w_ukv.shape[2]])
    ((grad_x, h1b, dprojb), (d_shift1, d_scale1), (d_g1,)) = row_call(
        "bwd_in", bwd_in, B, S, tm, [du, dpa, x2d, dx2], [scale1, shift1], [norm1_g], [w_in_p],
        [(D, F32), t_out(D), (IN_PAD, BF16)], [D, D], [(1, D)])
    g_w_in_p, *side_b = matmul_nn("grad_w_in", h1b, dprojb, side=grads_b)
    small_parts = dict(zip(ready, side_b[:len(ready)]))
    parts["w_uq"], parts["w_ukv"] = side_b[len(ready):]
    g_w_in = _unpad_w_in(g_w_in_p)

    dmod = jnp.concatenate([d_shift1, d_scale1, d_gate1, d_shift2, d_scale2, d_gate2, d_fshift, d_fscale],
                           axis=2).reshape(B, 8 * D)
    dmod_g, small_parts["norm1_g"], parts["w_in"] = exchange(
        "exchange_last_grads", [dmod.astype(BF16), dense_bf16("norm1_g", d_g1), _cols_to_shards(g_w_in)],
        [GATHER, GATHER, None])
    small_g = [small_parts[n] for n in SMALL]
    dmod_all = dmod_g.reshape(N_DEV * B, 8 * D)
    dm_sh = lax.dynamic_slice_in_dim(dmod_all[:, :6 * D], me * n_mod, n_mod, axis=1)
    dfm_sh = lax.dynamic_slice_in_dim(dmod_all[:, 6 * D:], me * n_fmod, n_fmod, axis=1)
    g_ada_w, g_fada_w, g_biases = modulation_bwd(cond_all, dm_sh, dfm_sh, dmod_all)
    parts["ada_w"] = g_ada_w[None]
    parts["final_ada_w"] = g_fada_w[None]

    res_g, res_d, res_m, res_v = {}, {}, {}, {}
    for n in ["ada_w"] + big + ["final_ada_w"]:
        w_full = given[n]
        w2 = w_full.reshape(w_full.shape[-2], w_full.shape[-1])
        out = adamw("adamw_" + n, parts[n], w2, given["m_" + n].reshape(w2.shape), given["v_" + n].reshape(w2.shape))
        res_g[n], res_d[n], res_m[n], res_v[n] = [a.reshape(w_full.shape) for a in out]
    bias_g = {"ada_b": g_biases[None, :, :6 * D], "final_ada_b": g_biases[None, :, 6 * D:]}
    names = SMALL + BIASES
    items = []
    for n, g_parts in zip(names, small_g + [bias_g[b] for b in BIASES]):
        dense = _dense_2d(given[n].shape)
        items.append((g_parts, given[n].reshape(dense), given["m_" + n].reshape(dense), given["v_" + n].reshape(dense)))
    for n, out in zip(names, adamw_many("adamw_small", items)):
        res_g[n], res_d[n], res_m[n], res_v[n] = [a.reshape(given[n].shape) for a in out]

    order = ["ada_w", "ada_b", "norm1_g", "w_in", "ssm_lambda_re", "ssm_lambda_im", "ssm_b_re", "ssm_b_im",
             "ssm_c_re", "ssm_c_im", "ssm_d", "ssm_log_dt", "w_glu", "q_norm_g", "w_uq", "kv_norm_g", "w_ukv",
             "ssm_out_g", "attn_out_g", "w_out", "norm2_g", "w_ff1", "w_ff2", "final_ada_w", "final_ada_b",
             "final_norm_g"]
    return (loss, grad_x.reshape(B, S, D), *[res_g[n] for n in order], *[res_d[n] for n in order],
            *[res_m[n] for n in order], *[res_v[n] for n in order])
```

```python
import functools
import math

import numpy as np
import jax
import jax.numpy as jnp
from jax import lax
from jax.experimental import pallas as pl
from jax.experimental.pallas import tpu as pltpu

F32 = jnp.float32
BF16 = jnp.bfloat16

N_DEV = 8
EPS = 1e-6
D_SSM = 512
N_GROUPS = 32
GROUP = 16
N_STATE = 64
N_ST = N_GROUPS * N_STATE
Q_LORA = 384
KV_LORA = 256
QK_NOPE = 64
QK_ROPE = 32
V_HEAD = 64
N_HEADS = 8
HEAD_PAD = 128
QK_W = N_HEADS * HEAD_PAD
V_W = N_HEADS * V_HEAD
IN_COLS = D_SSM + Q_LORA + KV_LORA + QK_ROPE
IN_PAD = D_SSM + Q_LORA + KV_LORA + HEAD_PAD
ROPE_BASE = 10000.0
ATTN_SCALE = (QK_NOPE + QK_ROPE) ** -0.5
NEG = -1e30

ADAM_LR = 0.001
ADAM_B1 = 0.9
ADAM_B2 = 0.999
ADAM_EPS = 1e-08
ADAM_WD = 0.01
ADAM_STEP = 10

VMEM_LIMIT = 56 * 1024 * 1024
MESH_ID = pl.DeviceIdType.MESH


def _dot(a, b):
    return jnp.dot(a, b, preferred_element_type=F32)


def _dot_nt(a, b):
    return lax.dot_general(a, b, (((1,), (1,)), ((), ())), preferred_element_type=F32)


def _dot_tn(a, b):
    return lax.dot_general(a, b, (((0,), (0,)), ((), ())), preferred_element_type=F32)


def _rms_stats(x):
    r = lax.rsqrt(jnp.mean(x * x, axis=-1, keepdims=True) + EPS)
    return r, x * r


def _rms_bwd(dy, xh, r):
    return r * (dy - xh * jnp.mean(dy * xh, axis=-1, keepdims=True))


def _sigmoid(x):
    return 1.0 / (1.0 + jnp.exp(-x))


_GELU_C = math.sqrt(2.0 / math.pi)


def _gelu(x):
    t = jnp.tanh(_GELU_C * (x + 0.044715 * x * x * x))
    return 0.5 * x * (1.0 + t)


def _gelu_grad(x):
    t = jnp.tanh(_GELU_C * (x + 0.044715 * x * x * x))
    return 0.5 * (1.0 + t) + 0.5 * x * (1.0 - t * t) * _GELU_C * (1.0 + 3.0 * 0.044715 * x * x)


def _colsum(a):
    return jnp.sum(a, axis=0, keepdims=True)


def _params(sem=None):
    return pltpu.CompilerParams(dimension_semantics=sem, vmem_limit_bytes=VMEM_LIMIT)


def _mesh_pos():
    x, y, c = lax.axis_index("x"), lax.axis_index("y"), lax.axis_index("c")
    return x, y, c, 4 * x + 2 * y + c


def _peer(x, y, c, k):
    px = 1 - x if k & 4 else x
    py = 1 - y if k & 2 else y
    pc = 1 - c if k & 1 else c
    return (px, py, pc), 4 * px + 2 * py + pc


GATHER = "gather"
ANY_SPEC = pl.BlockSpec(memory_space=pl.ANY)


class Exchange:
    def __init__(self, arrays, kinds):
        self.arrays, self.kinds, self.n = list(arrays), list(kinds), len(arrays)

    def _shard(self, i):
        a, kind = self.arrays[i], self.kinds[i]
        if kind == GATHER:
            return a.shape
        return a.shape[1:] if kind is None else (a.shape[0], kind)

    def out_shapes(self):
        return [jax.ShapeDtypeStruct((N_DEV,) + tuple(self._shard(i)), self.arrays[i].dtype) for i in range(self.n)]

    def scratch(self):
        return [pltpu.SemaphoreType.DMA((self.n, N_DEV - 1)), pltpu.SemaphoreType.DMA((self.n, N_DEV - 1)),
                pltpu.SemaphoreType.DMA((self.n,))]

    def _src(self, ins, i, j):
        kind = self.kinds[i]
        if kind == GATHER:
            return ins[i]
        if kind is None:
            return ins[i].at[j]
        return ins[i].at[:, pl.ds(pl.multiple_of(j * kind, 128), kind)]

    def _copies(self, ins, outs, sems, with_received):
        send_sems, recv_sems, loc_sems = sems
        x, y, c, me = _mesh_pos()
        local, sent, received = [], [], []
        for i in range(self.n):
            local.append(pltpu.make_async_copy(self._src(ins, i, me), outs[i].at[me], loc_sems.at[i]))
            for k in range(1, N_DEV):
                peer, pidx = _peer(x, y, c, k)
                pair = dict(send_sem=send_sems.at[i, k - 1], recv_sem=recv_sems.at[i, k - 1], device_id=peer,
                            device_id_type=MESH_ID)
                sent.append(pltpu.make_async_remote_copy(
                    src_ref=self._src(ins, i, pidx), dst_ref=outs[i].at[me], **pair))
                if with_received:
                    received.append(pltpu.make_async_remote_copy(
                        src_ref=self._src(ins, i, pidx), dst_ref=outs[i].at[pidx], **pair))
        return local, sent, received

    def start(self, ins, outs, sems):
        local, sent, _ = self._copies(ins, outs, sems, False)
        for cp in local + sent:
            cp.start()

    def wait(self, ins, outs, sems):
        local, sent, received = self._copies(ins, outs, sems, True)
        for cp in received:
            cp.wait_recv()
        for cp in sent:
            cp.wait_send()
        for cp in local:
            cp.wait()


def exchange(name, arrays, kinds):
    ex = Exchange(arrays, kinds)
    n = ex.n

    def body(*refs):
        ins, outs, sems = refs[:n], refs[n:2 * n], refs[2 * n:]
        ex.start(ins, outs, sems)
        ex.wait(ins, outs, sems)

    return pl.pallas_call(
        body, name=name, out_shape=ex.out_shapes(), in_specs=[ANY_SPEC] * n, out_specs=[ANY_SPEC] * n,
        scratch_shapes=ex.scratch())(*arrays)


def all_gather(name, arrays):
    return exchange(name, arrays, [GATHER] * len(arrays))


def row_call(name, body, B, S, tm, rows, bvecs, vecs, res, row_outs, bacc, gacc, side=None):
    ns = S // tm
    T = B * S
    n_r, n_b, n_v, n_w = len(rows), len(bvecs), len(vecs), len(res)
    n_ro, n_ba, n_ga = len(row_outs), len(bacc), len(gacc)
    n_x = side.n if side is not None else 0

    def kern(*refs):
        i = 0
        r_refs = refs[i:i + n_r]; i += n_r
        b_refs = refs[i:i + n_b]; i += n_b
        v_refs = refs[i:i + n_v]; i += n_v
        w_hbm = refs[i:i + n_w]; i += n_w
        x_in = refs[i:i + n_x]; i += n_x
        ro_refs = refs[i:i + n_ro]; i += n_ro
        ba_refs = refs[i:i + n_ba]; i += n_ba
        ga_refs = refs[i:i + n_ga]; i += n_ga
        x_out = refs[i:i + n_x]; i += n_x
        w_vmem = refs[i:i + n_w]; i += n_w
        x_sems = refs[i:]
        b = pl.program_id(0)
        s = pl.program_id(1)
        first = jnp.logical_and(b == 0, s == 0)

        if n_x:
            @pl.when(first)
            def _start_side():
                side.start(x_in, x_out, x_sems)

        if n_w:
            @pl.when(first)
            def _load_weights():
                for h, v in zip(w_hbm, w_vmem):
                    pltpu.sync_copy(h, v)

        ro, ba, ga = body([r[...] for r in r_refs], [r[0] for r in b_refs], [r[...] for r in v_refs],
                          list(w_vmem))
        for ref, val in zip(ro_refs, ro):
            ref[...] = val.astype(ref.dtype)

        def accumulate(ref, val, is_first, idx):
            @pl.when(is_first)
            def _set():
                ref[idx] = val

            @pl.when(jnp.logical_not(is_first))
            def _add():
                ref[idx] = ref[idx] + val

        for ref, val in zip(ba_refs, ba):
            accumulate(ref, val, s == 0, 0)
        for ref, val in zip(ga_refs, ga):
            accumulate(ref, val, first, Ellipsis)

        if n_x:
            @pl.when(jnp.logical_and(b == B - 1, s == ns - 1))
            def _wait_side():
                side.wait(x_in, x_out, x_sems)

    row_arrays = [r[0] if isinstance(r, tuple) else r for r in rows]
    row_in_specs = [pl.BlockSpec(r[1], r[2]) if isinstance(r, tuple)
                    else pl.BlockSpec((tm, r.shape[1]), lambda b, s: (b * ns + s, 0)) for r in rows]
    ro_shapes = [jax.ShapeDtypeStruct(tuple(o[0]), o[1]) if len(o) == 4 else jax.ShapeDtypeStruct((T, o[0]), o[1])
                 for o in row_outs]
    ro_specs = [pl.BlockSpec(o[2], o[3]) if len(o) == 4 else pl.BlockSpec((tm, o[0]), lambda b, s: (b * ns + s, 0))
                for o in row_outs]
    in_specs = (row_in_specs
                + [pl.BlockSpec((1, 1, v.shape[2]), lambda b, s: (b, 0, 0)) for v in bvecs]
                + [pl.BlockSpec(v.shape, lambda b, s, nd=v.ndim: (0,) * nd) for v in vecs]
                + [ANY_SPEC] * (n_w + n_x))
    out_shape = (ro_shapes
                 + [jax.ShapeDtypeStruct((B, 1, w), F32) for w in bacc]
                 + [jax.ShapeDtypeStruct(tuple(sh), F32) for sh in gacc]
                 + (side.out_shapes() if n_x else []))
    out_specs = (ro_specs
                 + [pl.BlockSpec((1, 1, w), lambda b, s: (b, 0, 0)) for w in bacc]
                 + [pl.BlockSpec(tuple(sh), lambda b, s, nd=len(sh): (0,) * nd) for sh in gacc]
                 + [ANY_SPEC] * n_x)
    outs = pl.pallas_call(
        kern, name=name, grid=(B, ns), in_specs=in_specs, out_specs=out_specs, out_shape=out_shape,
        scratch_shapes=[pltpu.VMEM(w.shape, w.dtype) for w in res] + (side.scratch() if n_x else []),
        compiler_params=_params(("arbitrary", "arbitrary")),
    )(*row_arrays, *bvecs, *vecs, *res, *(side.arrays if n_x else []))
    n_acc = n_ro + n_ba + n_ga
    if n_x:
        return outs[:n_ro], outs[n_ro:n_ro + n_ba], outs[n_ro + n_ba:n_acc], outs[n_acc:]
    return outs[:n_ro], outs[n_ro:n_ro + n_ba], outs[n_ro + n_ba:]


def matmul_nn(name, at, b, side=None):
    K, T = at.shape
    N = b.shape[1]
    tk = 1024 if K % 1024 == 0 else min(K, 512)
    tn = N if N <= 1280 else (2048 if N % 2048 == 0 else 512)
    assert K % tk == 0 and N % tn == 0 and T % min(T, 2048) == 0, (K, N, T)
    tt = min(T, 2048)
    nt = T // tt
    n_x = side.n if side is not None else 0
    grid = (K // tk, N // tn, nt)

    def kern(*refs):
        a_ref, b_ref = refs[:2]
        x_in = refs[2:2 + n_x]
        o_ref = refs[2 + n_x]
        x_out = refs[3 + n_x:3 + 2 * n_x]
        acc_ref = refs[3 + 2 * n_x]
        x_sems = refs[4 + 2 * n_x:]
        i, j, t = pl.program_id(0), pl.program_id(1), pl.program_id(2)
        if n_x:
            @pl.when(jnp.logical_and(i == 0, jnp.logical_and(j == 0, t == 0)))
            def _start_side():
                side.start(x_in, x_out, x_sems)

        part = _dot(a_ref[...], b_ref[...])

        @pl.when(t == 0)
        def _set():
            acc_ref[...] = part

        @pl.when(t > 0)
        def _add():
            acc_ref[...] = acc_ref[...] + part

        @pl.when(t == nt - 1)
        def _write():
            o_ref[...] = acc_ref[...].astype(o_ref.dtype)

        if n_x:
            @pl.when(jnp.logical_and(i == grid[0] - 1, jnp.logical_and(j == grid[1] - 1, t == nt - 1)))
            def _wait_side():
                side.wait(x_in, x_out, x_sems)

    outs = pl.pallas_call(
        kern, name=name, grid=grid,
        in_specs=[pl.BlockSpec((tk, tt), lambda i, j, t: (i, t)), pl.BlockSpec((tt, tn), lambda i, j, t: (t, j))]
        + [ANY_SPEC] * n_x,
        out_specs=[pl.BlockSpec((tk, tn), lambda i, j, t: (i, j))] + [ANY_SPEC] * n_x,
        out_shape=[jax.ShapeDtypeStruct((K, N), BF16)] + (side.out_shapes() if n_x else []),
        scratch_shapes=[pltpu.VMEM((tk, tn), F32)] + (side.scratch() if n_x else []),
        compiler_params=_params(("arbitrary",) * 3 if n_x else ("parallel", "parallel", "arbitrary")),
    )(at, b, *(side.arrays if n_x else []))
    return outs if n_x else outs[0]


def _rope_consts():
    inv_freq = ROPE_BASE ** (-jnp.arange(0, QK_ROPE, 2, dtype=F32) / QK_ROPE)
    zeros64 = jnp.zeros((64,), F32)
    zeros32 = jnp.zeros((32,), F32)
    zeros16 = jnp.zeros((16,), F32)
    ones16 = jnp.ones((16,), F32)
    invf = jnp.concatenate([zeros64, inv_freq, inv_freq, zeros32])[None, :]
    m_lo = jnp.concatenate([zeros64, ones16, zeros16, zeros32])[None, :]
    m_hi = jnp.concatenate([zeros64, zeros16, ones16, zeros32])[None, :]
    return invf, m_lo, m_hi


def _rope_tables(pos, invf, m_lo, m_hi):
    ang = pos * invf
    return jnp.cos(ang), jnp.sin(ang) * m_lo, jnp.sin(ang) * m_hi


def _rope_tile(t, cs, s_lo, s_hi, sign):
    up = pltpu.roll(t, HEAD_PAD - 16, axis=1)
    down = pltpu.roll(t, 16, axis=1)
    return t * cs + sign * (down * s_hi - up * s_lo)


def _heads(ref, width):
    if len(ref.shape) == 2:
        return jnp.stack([ref[:, h * width:(h + 1) * width] for h in range(N_HEADS)], axis=0)
    return jnp.stack([ref[b, :, h * width:(h + 1) * width] for b in range(ref.shape[0]) for h in range(N_HEADS)],
                     axis=0)


def _bmm(a, b, ca, cb):
    return lax.dot_general(a, b, (((ca,), (cb,)), ((0,), (0,))), preferred_element_type=F32)


LOG2E = math.log2(math.e)
EXP2_SCALE = ATTN_SCALE * LOG2E


def _scores_t(k_ref, q_ref, masked, tq, key_offset=0):
    st = _bmm(_heads(k_ref, HEAD_PAD), _heads(q_ref, HEAD_PAD), 2, 2)
    if masked:
        tk = st.shape[1]
        key = lax.broadcasted_iota(jnp.int32, (tk, tq), 0) + key_offset
        qry = lax.broadcasted_iota(jnp.int32, (tk, tq), 1)
        st = jnp.where((key <= qry)[None], st, NEG)
    return st


def attention_fwd(q, k, vt, B, S, tq, side):
    tk = tq
    tq = min(2 * tk, S)
    r = tq // tk
    nq = S // tq
    ns = side.n
    nbh = B * N_HEADS
    pairs = [(i, j) for i in range(nq) for j in range(r * (i + 1))]
    n_pairs = len(pairs)
    q_tab = jnp.asarray([p[0] for p in pairs], jnp.int32)
    k_tab = jnp.asarray([p[1] for p in pairs], jnp.int32)

    def kern(q_tab_ref, k_tab_ref, *refs):
        q_ref, k_ref, vt_ref = refs[:3]
        side_in = refs[3:3 + ns]
        o_ref, lse_ref = refs[3 + ns:5 + ns]
        side_out = refs[5 + ns:5 + 2 * ns]
        m_scr, l_scr, acc_scr = refs[5 + 2 * ns:8 + 2 * ns]
        sems = refs[8 + 2 * ns:]
        t = pl.program_id(0)
        qi, ki = q_tab_ref[t], k_tab_ref[t]

        @pl.when(t == 0)
        def _start_side():
            side.start(side_in, side_out, sems)

        @pl.when(ki == 0)
        def _init():
            m_scr[...] = jnp.full(m_scr.shape, NEG, F32)
            l_scr[...] = jnp.zeros(l_scr.shape, F32)
            acc_scr[...] = jnp.zeros(acc_scr.shape, F32)

        def block(masked):
            st = _scores_t(k_ref, q_ref, masked, tq, ki * tk - qi * tq)
            m_prev = m_scr[...]
            m_new = jnp.maximum(m_prev, jnp.max(st, axis=1, keepdims=True))
            alpha = jnp.exp2((m_prev - m_new) * EXP2_SCALE)
            p = jnp.exp2((st - m_new) * EXP2_SCALE)
            l_scr[...] = alpha * l_scr[...] + jnp.sum(p, axis=1, keepdims=True)
            vt3 = vt_ref[...].reshape(nbh, V_HEAD, tk)
            acc_scr[...] = alpha * acc_scr[...] + _bmm(vt3, p.astype(BF16), 2, 1)
            m_scr[...] = m_new

        @pl.when(ki < r * qi)
        def _full():
            block(False)

        @pl.when(ki >= r * qi)
        def _diagonal():
            block(True)

        @pl.when(ki == r * (qi + 1) - 1)
        def _finish():
            ot = acc_scr[...] / l_scr[...]
            for b in range(B):
                for h in range(N_HEADS):
                    o_ref[b, :, h * V_HEAD:(h + 1) * V_HEAD] = ot[b * N_HEADS + h].T
            lse_ref[...] = m_scr[...] * ATTN_SCALE + jnp.log(l_scr[...])

        @pl.when(t == n_pairs - 1)
        def _wait_side():
            side.wait(side_in, side_out, sems)

    grid_spec = pltpu.PrefetchScalarGridSpec(
        num_scalar_prefetch=2, grid=(n_pairs,),
        in_specs=[pl.BlockSpec((B, tq, QK_W), lambda t, qt, kt: (0, qt[t], 0)),
                  pl.BlockSpec((B, tk, QK_W), lambda t, qt, kt: (0, kt[t], 0)),
                  pl.BlockSpec((B, V_W, tk), lambda t, qt, kt: (0, 0, kt[t]))] + [ANY_SPEC] * ns,
        out_specs=[pl.BlockSpec((B, tq, V_W), lambda t, qt, kt: (0, qt[t], 0)),
                   pl.BlockSpec((nbh, 1, tq), lambda t, qt, kt: (0, 0, qt[t]))] + [ANY_SPEC] * ns,
        scratch_shapes=[pltpu.VMEM((nbh, 1, tq), F32), pltpu.VMEM((nbh, 1, tq), F32),
                        pltpu.VMEM((nbh, V_HEAD, tq), F32)] + side.scratch())
    return pl.pallas_call(
        kern, name="attention_fwd", grid_spec=grid_spec,
        out_shape=[jax.ShapeDtypeStruct((B, S, V_W), F32), jax.ShapeDtypeStruct((nbh, 1, S), F32)]
        + side.out_shapes(),
        compiler_params=_params(("arbitrary",)),
    )(q_tab, k_tab, q, k, vt, *side.arrays)


def attention_bwd(q, k, kt, v, do, lse, delta, B, S, tq, side):
    tk = tq
    tq = min(2 * tk, S)
    r = tq // tk
    nk, nq = S // tk, S // tq
    ns = side.n
    pairs = [(j, i) for j in range(nk) for i in range(j // r, nq)]
    n_pairs = len(pairs)
    k_tab = jnp.asarray([p[0] for p in pairs], jnp.int32)
    q_tab = jnp.asarray([p[1] for p in pairs], jnp.int32)

    def kern(k_tab_ref, q_tab_ref, *refs):
        q_ref, k_ref, kt_ref, v_ref, do_ref, lse_ref, delta_ref = refs[:7]
        side_in = refs[7:7 + ns]
        dq_hbm, dk_ref, dv_ref = refs[7 + ns:10 + ns]
        side_out = refs[10 + ns:10 + 2 * ns]
        dq_acc, dk_acc, dv_acc = refs[10 + 2 * ns:13 + 2 * ns]
        sems = refs[13 + 2 * ns:]
        b, t = pl.program_id(0), pl.program_id(1)
        kj, qi = k_tab_ref[t], q_tab_ref[t]
        on_diagonal = qi == kj // r

        @pl.when(jnp.logical_and(b == 0, t == 0))
        def _start_side():
            side.start(side_in, side_out, sems)

        @pl.when(t == 0)
        def _zero_dq():
            dq_acc[...] = jnp.zeros(dq_acc.shape, F32)

        @pl.when(on_diagonal)
        def _zero_dkv():
            dk_acc[...] = jnp.zeros(dk_acc.shape, F32)
            dv_acc[...] = jnp.zeros(dv_acc.shape, F32)

        def block(masked):
            st = _scores_t(k_ref, q_ref, masked, tq, kj * tk - qi * tq)
            pt = jnp.exp2(st * EXP2_SCALE - lse_ref[...] * LOG2E)
            do3 = _heads(do_ref, V_HEAD).astype(BF16)
            dv_acc[...] = dv_acc[...] + _bmm(pt.astype(BF16), do3, 2, 1)
            dpt = _bmm(_heads(v_ref, V_HEAD), do3, 2, 2)
            dst = (pt * (dpt - delta_ref[...]) * ATTN_SCALE).astype(BF16)
            dk_acc[...] = dk_acc[...] + _bmm(dst, _heads(q_ref, HEAD_PAD), 2, 1)
            q_cols = pl.ds(pl.multiple_of(qi * tq, tq), tq)
            kt3 = kt_ref[...].reshape(N_HEADS, HEAD_PAD, tk)
            dq_acc[:, :, q_cols] = dq_acc[:, :, q_cols] + _bmm(kt3, dst, 2, 1)

        @pl.when(jnp.logical_not(on_diagonal))
        def _full():
            block(False)

        @pl.when(on_diagonal)
        def _diagonal():
            block(True)

        @pl.when(qi == nq - 1)
        def _write_dkv():
            for h in range(N_HEADS):
                dk_ref[:, h * HEAD_PAD:(h + 1) * HEAD_PAD] = dk_acc[h]
                dv_ref[:, h * V_HEAD:(h + 1) * V_HEAD] = dv_acc[h]

        @pl.when(t == n_pairs - 1)
        def _write_dq():
            pltpu.sync_copy(dq_acc, dq_hbm.at[b])

        @pl.when(jnp.logical_and(b == B - 1, t == n_pairs - 1))
        def _wait_side():
            side.wait(side_in, side_out, sems)

    T = B * S
    qrow = lambda b, t, kt, qt: (b * nq + qt[t], 0)
    qcol3 = lambda b, t, kt, qt: (0, 0, b * nq + qt[t])
    krow = lambda b, t, kt, qt: (b * nk + kt[t], 0)
    grid_spec = pltpu.PrefetchScalarGridSpec(
        num_scalar_prefetch=2, grid=(B, n_pairs),
        in_specs=[pl.BlockSpec((tq, QK_W), qrow), pl.BlockSpec((tk, QK_W), krow),
                  pl.BlockSpec((QK_W, tk), lambda b, t, kt, qt: (0, b * nk + kt[t])), pl.BlockSpec((tk, V_W), krow),
                  pl.BlockSpec((tq, V_W), qrow), pl.BlockSpec((N_HEADS, 1, tq), qcol3),
                  pl.BlockSpec((N_HEADS, 1, tq), qcol3)] + [ANY_SPEC] * ns,
        out_specs=[ANY_SPEC, pl.BlockSpec((tk, QK_W), krow), pl.BlockSpec((tk, V_W), krow)] + [ANY_SPEC] * ns,
        scratch_shapes=[pltpu.VMEM((N_HEADS, HEAD_PAD, S), F32), pltpu.VMEM((N_HEADS, tk, HEAD_PAD), F32),
                        pltpu.VMEM((N_HEADS, tk, V_HEAD), F32)] + side.scratch())
    return pl.pallas_call(
        kern, name="attention_bwd", grid_spec=grid_spec,
        out_shape=[jax.ShapeDtypeStruct((B, N_HEADS, HEAD_PAD, S), F32), jax.ShapeDtypeStruct((T, QK_W), F32),
                   jax.ShapeDtypeStruct((T, V_W), F32)] + side.out_shapes(),
        compiler_params=_params(("arbitrary", "arbitrary")),
    )(k_tab, q_tab, q, k, kt, v, do, lse, delta, *side.arrays)


GB = 8
N_GB = N_GROUPS // GB
GB_U = GB * GROUP
GB_ST = GB * N_STATE


def _gb_u(j):
    return slice(j * GB_U, (j + 1) * GB_U)


def _gb_s(j):
    return slice(j * GB_ST, (j + 1) * GB_ST)


def ssm_param_fn(lr, li, ld, br, bi):
    dt = jnp.exp(ld)
    er = jnp.exp(lr * dt)
    ar = er * jnp.cos(li * dt)
    ai = er * jnp.sin(li * dt)
    nr = ar - 1.0
    den = lr * lr + li * li
    cr = (nr * lr + ai * li) / den
    ci = (ai * lr - nr * li) / den
    return ar, ai, cr * br - ci * bi, cr * bi + ci * br


def ssm_params_fwd(reps):
    def kern(lr, li, ld, br, bi, ar_o, ai_o, bbr_o, bbi_o):
        ar, ai, bbr, bbi = ssm_param_fn(lr[...], li[...], ld[...], br[...], bi[...])
        ar_o[...] = ar
        ai_o[...] = ai
        bbr_o[...] = bbr
        bbi_o[...] = bbi

    shp = jax.ShapeDtypeStruct(reps[0].shape, F32)
    return pl.pallas_call(kern, name="ssm_params_fwd", out_shape=[shp] * 4)(*reps)


def ssm_params_bwd(reps, cots):
    rows = reps[0].shape[0]

    def kern(lr, li, ld, br, bi, d_ar, d_ai, d_bbr, d_bbi, dlr_o, dli_o, dld_o, dbr_o, dbi_o):
        _, vjp = jax.vjp(ssm_param_fn, lr[...], li[...], ld[...], br[...], bi[...])
        dlr, dli, dld, dbr, dbi = vjp((d_ar[...], d_ai[...], d_bbr[...], d_bbi[...]))
        dlr_o[...] = jnp.sum(dlr.reshape(N_GROUPS, GROUP, N_STATE), axis=1)
        dli_o[...] = jnp.sum(dli.reshape(N_GROUPS, GROUP, N_STATE), axis=1)
        dld_o[...] = jnp.sum(jnp.sum(dld.reshape(N_GROUPS, GROUP, N_STATE), axis=1), axis=-1, keepdims=True)
        dbr_o[...] = dbr
        dbi_o[...] = dbi

    g = jax.ShapeDtypeStruct((N_GROUPS, N_STATE), F32)
    full = jax.ShapeDtypeStruct((rows, N_STATE), F32)
    return pl.pallas_call(
        kern, name="ssm_params_bwd",
        out_shape=[g, g, jax.ShapeDtypeStruct((N_GROUPS, 1), F32), full, full])(*reps, *cots)


def ssm_fwd(proj, b_re, b_im, c_re, c_im, d_vec, a_re, a_im, B, S, ts):
    ns = S // ts
    T = B * S

    def kern(u_ref, bre_ref, bim_ref, cre_ref, cim_ref, d_ref, are_ref, aim_ref,
             sre_ref, sim_ref, y_ref, car_re, car_im, bu_re, bu_im):
        s = pl.program_id(1)

        @pl.when(s == 0)
        def _reset():
            car_re[...] = jnp.zeros(car_re.shape, F32)
            car_im[...] = jnp.zeros(car_im.shape, F32)

        u = u_ref[...]
        ub = u.astype(BF16)
        for j in range(N_GB):
            uj, bj, sj = ub[:, _gb_u(j)], _gb_u(j), _gb_s(j)
            bu_re[:, sj] = _dot(uj, bre_ref[bj, :])
            bu_im[:, sj] = _dot(uj, bim_ref[bj, :])
        ar, ai = are_ref[...], aim_ref[...]

        def step(t, carry):
            xr, xi = carry
            row = pl.ds(t, 1)
            nr = ar * xr - ai * xi + bu_re[row, :]
            ni = ar * xi + ai * xr + bu_im[row, :]
            sre_ref[row, :] = nr
            sim_ref[row, :] = ni
            return nr, ni

        xr, xi = lax.fori_loop(0, ts, step, (car_re[0:1, :], car_im[0:1, :]))
        car_re[0:1, :] = xr
        car_im[0:1, :] = xi
        y = [_dot(sre_ref[:, _gb_s(j)].astype(BF16), cre_ref[_gb_s(j), :])
             - _dot(sim_ref[:, _gb_s(j)].astype(BF16), cim_ref[_gb_s(j), :]) for j in range(N_GB)]
        y_ref[...] = jnp.concatenate(y, axis=1) + d_ref[...] * u

    row = lambda b, s: (b * ns + s, 0)
    whole = lambda b, s: (0, 0)
    return pl.pallas_call(
        kern, name="ssm_fwd", grid=(B, ns),
        in_specs=[pl.BlockSpec((ts, D_SSM), row),
                  pl.BlockSpec((D_SSM, GB_ST), whole), pl.BlockSpec((D_SSM, GB_ST), whole),
                  pl.BlockSpec((N_ST, GB_U), whole), pl.BlockSpec((N_ST, GB_U), whole),
                  pl.BlockSpec((1, D_SSM), whole), pl.BlockSpec((1, N_ST), whole), pl.BlockSpec((1, N_ST), whole)],
        out_specs=[pl.BlockSpec((ts, N_ST), row), pl.BlockSpec((ts, N_ST), row), pl.BlockSpec((ts, D_SSM), row)],
        out_shape=[jax.ShapeDtypeStruct((T, N_ST), F32), jax.ShapeDtypeStruct((T, N_ST), F32),
                   jax.ShapeDtypeStruct((T, D_SSM), F32)],
        scratch_shapes=[pltpu.VMEM((8, N_ST), F32), pltpu.VMEM((8, N_ST), F32),
                        pltpu.VMEM((ts, N_ST), F32), pltpu.VMEM((ts, N_ST), F32)],
        compiler_params=_params(("arbitrary", "arbitrary")),
    )(proj, b_re, b_im, c_re, c_im, d_vec, a_re, a_im)


def ssm_bwd(dy, proj, s_re, s_im, b_re, b_im, c_re, c_im, d_vec, a_re, a_im, B, S, ts):
    ns = S // ts
    T = B * S
    t8 = ts // 8

    def kern(dy_ref, u_ref, sre_ref, sim_ref, pre_ref, pim_ref, bre_ref, bim_ref, cre_ref, cim_ref,
             d_ref, are_ref, aim_ref,
             du_ref, dcre_ref, dcim_ref, dbre_ref, dbim_ref, dare_ref, daim_ref, dd_ref,
             car_re, car_im, g_re, g_im):
        b, s = pl.program_id(0), pl.program_id(1)
        first = jnp.logical_and(b == 0, s == 0)

        @pl.when(s == 0)
        def _reset():
            car_re[...] = jnp.zeros(car_re.shape, F32)
            car_im[...] = jnp.zeros(car_im.shape, F32)

        dyv = dy_ref[...]
        dyb = dyv.astype(BF16)
        u = u_ref[...]
        for j in range(N_GB):
            g_re[:, _gb_s(j)] = _dot_nt(dyb[:, _gb_u(j)], cre_ref[_gb_s(j), :])
            g_im[:, _gb_s(j)] = -_dot_nt(dyb[:, _gb_u(j)], cim_ref[_gb_s(j), :])
        ar, ai = are_ref[...], aim_ref[...]

        def step(i, carry):
            gr_next, gi_next = carry
            row = pl.ds(ts - 1 - i, 1)
            gr = g_re[row, :] + ar * gr_next + ai * gi_next
            gi = g_im[row, :] + ar * gi_next - ai * gr_next
            g_re[row, :] = gr
            g_im[row, :] = gi
            return gr, gi

        gr0, gi0 = lax.fori_loop(0, ts, step, (car_re[0:1, :], car_im[0:1, :]))
        car_re[0:1, :] = gr0
        car_im[0:1, :] = gi0

        gr, gi = g_re[...], g_im[...]
        sr, si = sre_ref[...], sim_ref[...]
        has_prev = (s < ns - 1).astype(F32)
        row_id = lax.broadcasted_iota(jnp.int32, (ts, N_ST), 0)
        spr = jnp.where(row_id == 0, pre_ref[7:8, :] * has_prev, pltpu.roll(sr, 1, axis=0))
        spi = jnp.where(row_id == 0, pim_ref[7:8, :] * has_prev, pltpu.roll(si, 1, axis=0))
        grb, gib, ub = gr.astype(BF16), gi.astype(BF16), u.astype(BF16)
        srb, sib = sr.astype(BF16), si.astype(BF16)
        cat0 = lambda f: jnp.concatenate([f(j) for j in range(N_GB)], axis=0)
        cat1 = lambda f: jnp.concatenate([f(j) for j in range(N_GB)], axis=1)
        parts = [
            (dcre_ref, cat0(lambda j: _dot_tn(srb[:, _gb_s(j)], dyb[:, _gb_u(j)]))),
            (dcim_ref, cat0(lambda j: -_dot_tn(sib[:, _gb_s(j)], dyb[:, _gb_u(j)]))),
            (dbre_ref, cat0(lambda j: _dot_tn(ub[:, _gb_u(j)], grb[:, _gb_s(j)]))),
            (dbim_ref, cat0(lambda j: _dot_tn(ub[:, _gb_u(j)], gib[:, _gb_s(j)]))),
            (dare_ref, _colsum(gr * spr + gi * spi)),
            (daim_ref, _colsum(gi * spr - gr * spi)),
            (dd_ref, _colsum(dyv * u)),
        ]
        du = cat1(lambda j: _dot_nt(grb[:, _gb_s(j)], bre_ref[_gb_u(j), :])
                  + _dot_nt(gib[:, _gb_s(j)], bim_ref[_gb_u(j), :]))
        du_ref[...] = du + d_ref[...] * dyv

        @pl.when(first)
        def _set():
            for ref, val in parts:
                ref[...] = val

        @pl.when(jnp.logical_not(first))
        def _add():
            for ref, val in parts:
                ref[...] = ref[...] + val

    row = lambda b, s: (b * ns + (ns - 1 - s), 0)
    prev = lambda b, s: (jnp.maximum((b * ns + (ns - 1 - s)) * t8 - 1, 0), 0)
    whole = lambda b, s: (0, 0)
    acc_shapes = [(N_ST, GB_U), (N_ST, GB_U), (D_SSM, GB_ST), (D_SSM, GB_ST), (1, N_ST), (1, N_ST), (1, D_SSM)]
    return pl.pallas_call(
        kern, name="ssm_bwd", grid=(B, ns),
        in_specs=[pl.BlockSpec((ts, D_SSM), row), pl.BlockSpec((ts, D_SSM), row),
                  pl.BlockSpec((ts, N_ST), row), pl.BlockSpec((ts, N_ST), row),
                  pl.BlockSpec((8, N_ST), prev), pl.BlockSpec((8, N_ST), prev),
                  pl.BlockSpec((D_SSM, GB_ST), whole), pl.BlockSpec((D_SSM, GB_ST), whole),
                  pl.BlockSpec((N_ST, GB_U), whole), pl.BlockSpec((N_ST, GB_U), whole),
                  pl.BlockSpec((1, D_SSM), whole), pl.BlockSpec((1, N_ST), whole), pl.BlockSpec((1, N_ST), whole)],
        out_specs=[pl.BlockSpec((ts, D_SSM), row)] + [pl.BlockSpec(sh, whole) for sh in acc_shapes],
        out_shape=[jax.ShapeDtypeStruct((T, D_SSM), F32)] + [jax.ShapeDtypeStruct(sh, F32) for sh in acc_shapes],
        scratch_shapes=[pltpu.VMEM((8, N_ST), F32), pltpu.VMEM((8, N_ST), F32),
                        pltpu.VMEM((ts, N_ST), F32), pltpu.VMEM((ts, N_ST), F32)],
        compiler_params=_params(("arbitrary", "arbitrary")),
    )(dy, proj, s_re, s_im, s_re, s_im, b_re, b_im, c_re, c_im, d_vec, a_re, a_im)


def modulation_fwd(c_all, ada_w, ada_b, fada_w, fada_b):
    def kern(c_ref, w_ref, b_ref, fw_ref, fb_ref, cond_ref, mod_ref, fmod_ref):
        cv = c_ref[...]
        cond = (cv * _sigmoid(cv)).astype(BF16)
        cond_ref[...] = cond
        mod_ref[...] = _dot(cond, w_ref[...].astype(BF16)) + b_ref[...]
        fmod_ref[...] = _dot(cond, fw_ref[...].astype(BF16)) + fb_ref[...]

    n = c_all.shape[0]
    return pl.pallas_call(
        kern, name="modulation_fwd",
        out_shape=[jax.ShapeDtypeStruct(c_all.shape, BF16), jax.ShapeDtypeStruct((n, ada_w.shape[1]), F32),
                   jax.ShapeDtypeStruct((n, fada_w.shape[1]), F32)],
        compiler_params=_params(),
    )(c_all, ada_w, ada_b, fada_w, fada_b)


def modulation_bwd(cond_all, dmod, dfmod, dmod_all):
    def kern(c_ref, dm_ref, dfm_ref, all_ref, gw_ref, gfw_ref, gb_ref):
        cond = c_ref[...]
        gw_ref[...] = _dot_tn(cond, dm_ref[...].astype(BF16))
        gfw_ref[...] = _dot_tn(cond, dfm_ref[...].astype(BF16))
        gb_ref[...] = _colsum(all_ref[...].astype(F32))

    d = cond_all.shape[1]
    return pl.pallas_call(
        kern, name="modulation_bwd",
        out_shape=[jax.ShapeDtypeStruct((d, dmod.shape[1]), F32), jax.ShapeDtypeStruct((d, dfmod.shape[1]), F32),
                   jax.ShapeDtypeStruct((1, dmod_all.shape[1]), F32)],
        compiler_params=_params(),
    )(cond_all, dmod, dfmod, dmod_all)


def _adamw_update(p_ref, w_ref, m_ref, v_ref, g_o, d_o, m_o, v_o):
    g = p_ref[0].astype(F32)
    for i in range(1, p_ref.shape[0]):
        g = g + p_ref[i].astype(F32)
    m_new = ADAM_B1 * m_ref[...] + (1.0 - ADAM_B1) * g
    v_new = ADAM_B2 * v_ref[...] + (1.0 - ADAM_B2) * (g * g)
    m_hat = m_new / (1.0 - ADAM_B1 ** ADAM_STEP)
    v_hat = v_new / (1.0 - ADAM_B2 ** ADAM_STEP)
    g_o[...] = g
    d_o[...] = -ADAM_LR * (m_hat / (jnp.sqrt(v_hat) + ADAM_EPS) + ADAM_WD * w_ref[...])
    m_o[...] = m_new
    v_o[...] = v_new


def adamw_many(name, items):
    k = len(items)

    def kern(*refs):
        ins, outs = refs[:4 * k], refs[4 * k:]
        for i in range(k):
            _adamw_update(*ins[4 * i:4 * i + 4], *outs[4 * i:4 * i + 4])

    flat = [a for item in items for a in item]
    out_shape = [jax.ShapeDtypeStruct(item[1].shape, F32) for item in items for _ in range(4)]
    outs = pl.pallas_call(kern, name=name, out_shape=out_shape, compiler_params=_params())(*flat)
    return [tuple(outs[4 * i:4 * i + 4]) for i in range(k)]


def adamw(name, parts, w, m, v):
    n, R, C = parts.shape
    tr = R
    while n * tr * C * 4 > 4 * 1024 * 1024 and tr % 32 == 0:
        tr //= 2

    def kern(p_ref, w_ref, m_ref, v_ref, g_o, d_o, m_o, v_o):
        _adamw_update(p_ref, w_ref, m_ref, v_ref, g_o, d_o, m_o, v_o)

    blk = pl.BlockSpec((tr, C), lambda i: (i, 0))
    shp = jax.ShapeDtypeStruct((R, C), F32)
    return pl.pallas_call(
        kern, name=name, grid=(R // tr,),
        in_specs=[pl.BlockSpec((n, tr, C), lambda i: (0, i, 0)), blk, blk, blk],
        out_specs=[blk] * 4, out_shape=[shp] * 4,
        compiler_params=_params(("parallel",)),
    )(parts, w, m, v)


def _cols_from_gathered(g):
    return jnp.transpose(g, (1, 0, 2)).reshape(g.shape[1], N_DEV * g.shape[2])


def _cols_to_shards(w):
    k, n8 = w.shape
    return jnp.transpose(w.reshape(k, N_DEV, n8 // N_DEV), (1, 0, 2))


def _pad_w_in(w):
    z = functools.partial(jnp.zeros, dtype=w.dtype)
    k = w.shape[0]
    cut = D_SSM + Q_LORA + KV_LORA
    return jnp.concatenate([w[:, :cut], z((k, 64)), w[:, cut:], z((k, 32))], axis=1)


def _unpad_w_in(w):
    cut = D_SSM + Q_LORA + KV_LORA
    return jnp.concatenate([w[:, :cut], w[:, cut + 64:cut + 96]], axis=1)


def _pad_w_uq(w):
    k = w.shape[0]
    w3 = w.reshape(k, N_HEADS, QK_NOPE + QK_ROPE)
    return jnp.pad(w3, ((0, 0), (0, 0), (0, HEAD_PAD - QK_NOPE - QK_ROPE))).reshape(k, QK_W)


def _unpad_w_uq(w):
    k = w.shape[0]
    return w.reshape(k, N_HEADS, HEAD_PAD)[:, :, :QK_NOPE + QK_ROPE].reshape(k, N_HEADS * (QK_NOPE + QK_ROPE))


def _pad_w_ukv(w):
    k = w.shape[0]
    w3 = w.reshape(k, N_HEADS, QK_NOPE + V_HEAD)
    kpart = jnp.pad(w3[:, :, :QK_NOPE], ((0, 0), (0, 0), (0, HEAD_PAD - QK_NOPE))).reshape(k, QK_W)
    vpart = w3[:, :, QK_NOPE:].reshape(k, V_W)
    return jnp.concatenate([kpart, vpart], axis=1)


def _unpad_w_ukv(w):
    k = w.shape[0]
    kpart = w[:, :QK_W].reshape(k, N_HEADS, HEAD_PAD)[:, :, :QK_NOPE]
    vpart = w[:, QK_W:].reshape(k, N_HEADS, V_HEAD)
    return jnp.concatenate([kpart, vpart], axis=2).reshape(k, N_HEADS * (QK_NOPE + V_HEAD))


def _block_diag(blocks):
    g, r, c = blocks.shape
    b4 = blocks.reshape(N_GB, GB, r, c)
    keep = jnp.eye(GB, dtype=bool)[None, :, None, :, None]
    return jnp.where(keep, b4[:, :, :, None, :], 0).reshape(g * r, GB * c)


def _block_diag_take(stacked, r, c):
    d5 = stacked.reshape(N_GB, GB, r, GB, c)
    idx = jnp.arange(GB)
    return jnp.transpose(d5[:, idx, :, idx, :], (1, 0, 2, 3)).reshape(N_GROUPS, r, c)


SMALL = ["norm1_g", "ssm_lambda_re", "ssm_lambda_im", "ssm_b_re", "ssm_b_im", "ssm_c_re", "ssm_c_im",
         "ssm_d", "ssm_log_dt", "q_norm_g", "kv_norm_g", "ssm_out_g", "attn_out_g", "norm2_g", "final_norm_g"]
BIASES = ["ada_b", "final_ada_b"]


def _dense_2d(shape):
    dims = [d for d in shape[1:]] if len(shape) > 1 and shape[0] == 1 else list(shape)
    if len(dims) == 1:
        return (1, dims[0])
    return (dims[0], int(np.prod(dims[1:])))


def kernel(x, c, positions, ada_w, ada_b, norm1_g, w_in, ssm_lambda_re, ssm_lambda_im, ssm_b_re, ssm_b_im, ssm_c_re, ssm_c_im, ssm_d, ssm_log_dt, w_glu, q_norm_g, w_uq, kv_norm_g, w_ukv, ssm_out_g, attn_out_g, w_out, norm2_g, w_ff1, w_ff2, final_ada_w, final_ada_b, final_norm_g, loss_target, m_ada_w, m_ada_b, m_norm1_g, m_w_in, m_ssm_lambda_re, m_ssm_lambda_im, m_ssm_b_re, m_ssm_b_im, m_ssm_c_re, m_ssm_c_im, m_ssm_d, m_ssm_log_dt, m_w_glu, m_q_norm_g, m_w_uq, m_kv_norm_g, m_w_ukv, m_ssm_out_g, m_attn_out_g, m_w_out, m_norm2_g, m_w_ff1, m_w_ff2, m_final_ada_w, m_final_ada_b, m_final_norm_g, v_ada_w, v_ada_b, v_norm1_g, v_w_in, v_ssm_lambda_re, v_ssm_lambda_im, v_ssm_b_re, v_ssm_b_im, v_ssm_c_re, v_ssm_c_im, v_ssm_d, v_ssm_log_dt, v_w_glu, v_q_norm_g, v_w_uq, v_kv_norm_g, v_w_ukv, v_ssm_out_g, v_attn_out_g, v_w_out, v_norm2_g, v_w_ff1, v_w_ff2, v_final_ada_w, v_final_ada_b, v_final_norm_g):
    given = dict(locals())
    B, S, D = x.shape
    T = B * S
    tm = min(256, S)
    me = 4 * lax.axis_index("x") + 2 * lax.axis_index("y") + lax.axis_index("c")

    big = ["w_in", "w_glu", "w_uq", "w_ukv", "w_out", "w_ff1", "w_ff2"]
    shards = {n: given[n][0] for n in big}
    early, late = ["w_in", "w_uq", "w_ukv"], ["w_glu", "w_out", "w_ff1", "w_ff2"]
    c_g, w_in_g = all_gather("gather_first_weights", [c, shards["w_in"].astype(BF16)])
    c_all = c_g.reshape(N_DEV * B, D)
    w_in_p = _pad_w_in(_cols_from_gathered(w_in_g))

    n_mod = ada_w.shape[2]
    n_fmod = final_ada_w.shape[1]
    ada_b_sh = lax.dynamic_slice_in_dim(ada_b, me * n_mod, n_mod, axis=1)
    fada_b_sh = lax.dynamic_slice_in_dim(final_ada_b[None, :], me * n_fmod, n_fmod, axis=1)
    cond_all, mod_sh, fmod_sh = modulation_fwd(c_all, ada_w[0], ada_b_sh, final_ada_w, fada_b_sh)
    mod_g, fmod_g = all_gather("gather_modulation", [mod_sh, fmod_sh])
    mod_mine = lax.dynamic_slice_in_dim(_cols_from_gathered(mod_g), me * B, B, axis=0)
    fmod_mine = lax.dynamic_slice_in_dim(_cols_from_gathered(fmod_g), me * B, B, axis=0)
    shift1, scale1, gate1, shift2, scale2, gate2 = [mod_mine[:, None, i * D:(i + 1) * D] for i in range(6)]
    fshift, fscale = fmod_mine[:, None, :D], fmod_mine[:, None, D:]

    x2d = x.reshape(T, D)
    tgt2d = loss_target.reshape(T, D)
    pos = positions.astype(F32).reshape(T, 1)
    invf, m_lo, m_hi = _rope_consts()
    rope_mask = m_lo + m_hi

    def fwd_in(rows, bv, vecs, res):
        (xv,), (sc, sh), (g1,), (w,) = rows, bv, vecs, res
        r, xh = _rms_stats(xv)
        h = xh * g1 * (1.0 + sc) + sh
        return [_dot(h.astype(BF16), w[...])], [], []

    qkv_gather = Exchange([shards["w_uq"].astype(BF16), shards["w_ukv"].astype(BF16)], [GATHER, GATHER])
    (proj,), _, _, (w_uq_g, w_ukv_g) = row_call(
        "fwd_in", fwd_in, B, S, tm, [x2d], [scale1, shift1], [norm1_g], [w_in_p], [(IN_PAD, F32)], [], [],
        side=qkv_gather)
    w_uq_p = _pad_w_uq(_cols_from_gathered(w_uq_g))
    w_ukv_p = _pad_w_ukv(_cols_from_gathered(w_ukv_g))

    def fwd_qkv(rows, bv, vecs, res):
        (pr, ps), (gq, gkv, fr, lo, hi), (wq, wkv) = rows, vecs, res
        cs, s_lo, s_hi = _rope_tables(ps, fr, lo, hi)
        _, qh = _rms_stats(pr[:, D_SSM:D_SSM + Q_LORA])
        _, kvh = _rms_stats(pr[:, D_SSM + Q_LORA:D_SSM + Q_LORA + KV_LORA])
        qf = _dot((qh * gq).astype(BF16), wq[...])
        kvf = _dot((kvh * gkv).astype(BF16), wkv[...])
        k_rope = _rope_tile(pr[:, IN_PAD - HEAD_PAD:], cs, s_lo, s_hi, 1.0)
        q_t, k_t = [], []
        for h in range(N_HEADS):
            hs = slice(h * HEAD_PAD, (h + 1) * HEAD_PAD)
            q_t.append(_rope_tile(qf[:, hs], cs, s_lo, s_hi, 1.0))
            k_t.append(kvf[:, hs] + k_rope)
        kf, vf = jnp.concatenate(k_t, axis=1), kvf[:, QK_W:]
        return [jnp.concatenate(q_t, axis=1), kf, vf, t_val(kf), t_val(vf)[None], cs, s_lo, s_hi], [], []

    ns = S // tm
    col_tile = lambda b, s: (0, b * ns + s)
    t_out = lambda w: ((w, T), BF16, (w, tm), col_tile)
    t_val = lambda v: v.astype(BF16).T
    (q_r, k_r, v_r, k_t, v_t, rope_cs, rope_lo, rope_hi), _, _ = row_call(
        "fwd_qkv", fwd_qkv, B, S, tm, [proj, pos], [], [q_norm_g, kv_norm_g, invf, m_lo, m_hi],
        [w_uq_p, w_ukv_p],
        [(QK_W, BF16), (QK_W, BF16), (V_W, BF16), ((QK_W, T), BF16, (QK_W, tm), col_tile),
         ((B, V_W, S), BF16, (1, V_W, tm), lambda b, s: (b, 0, s)), (HEAD_PAD, F32), (HEAD_PAD, F32),
         (HEAD_PAD, F32)], [], [])
    late_gather = Exchange([shards[n].astype(BF16) for n in late], [GATHER] * len(late))
    y_attn, lse, *late_g = attention_fwd(q_r.reshape(B, S, QK_W), k_r.reshape(B, S, QK_W), v_t, B, S, tm,
                                         late_gather)
    y_attn = y_attn.reshape(T, V_W)
    lse = jnp.transpose(lse.reshape(B, N_HEADS, 1, S), (1, 2, 0, 3)).reshape(N_HEADS, 1, T)
    gw = dict(zip(late, late_g))
    w_glu_f = _cols_from_gathered(gw["w_glu"])
    w_out_f = gw["w_out"].reshape(-1, D)
    w_ff1_f = _cols_from_gathered(gw["w_ff1"])
    w_ff2_f = gw["w_ff2"].reshape(-1, D)

    rep = lambda a: jnp.repeat(a, GROUP, axis=0)
    lam_rep = [rep(ssm_lambda_re[0]), rep(ssm_lambda_im[0]),
               rep(jnp.broadcast_to(ssm_log_dt[0][:, None], (N_GROUPS, N_STATE)))]
    bt = lambda a: jnp.transpose(a, (0, 2, 1)).reshape(D_SSM, N_STATE)
    reps = lam_rep + [bt(ssm_b_re[0]), bt(ssm_b_im[0])]
    a_re_rep, a_im_rep, bb_re, bb_im = ssm_params_fwd(reps)
    a_re = a_re_rep.reshape(N_GROUPS, GROUP, N_STATE)[:, 0, :].reshape(1, N_ST)
    a_im = a_im_rep.reshape(N_GROUPS, GROUP, N_STATE)[:, 0, :].reshape(1, N_ST)
    bbd_re = _block_diag(bb_re.reshape(N_GROUPS, GROUP, N_STATE)).astype(BF16)
    bbd_im = _block_diag(bb_im.reshape(N_GROUPS, GROUP, N_STATE)).astype(BF16)
    cbd_re = _block_diag(jnp.transpose(ssm_c_re[0], (0, 2, 1))).astype(BF16)
    cbd_im = _block_diag(jnp.transpose(ssm_c_im[0], (0, 2, 1))).astype(BF16)
    d_vec = ssm_d.reshape(1, D_SSM)
    st_re, st_im, y_pre = ssm_fwd(proj, bbd_re, bbd_im, cbd_re, cbd_im, d_vec, a_re, a_im, B, S, tm)

    def fwd_mix(rows, bv, vecs, res):
        (yp, ya, xv), (g1v,), (gso, gao), (wg, wo) = rows, bv, vecs, res
        z = _dot(_gelu(yp).astype(BF16), wg[...])
        y_ssm = z[:, :D_SSM] * _sigmoid(z[:, D_SSM:])
        _, sh = _rms_stats(y_ssm)
        _, ah = _rms_stats(ya)
        o = _dot((sh * gso).astype(BF16), wo[0:D_SSM, :]) + _dot((ah * gao).astype(BF16), wo[D_SSM:, :])
        return [z, o, xv + g1v * o], [], []

    (z, o, x2), _, _ = row_call("fwd_mix", fwd_mix, B, S, tm, [y_pre, y_attn, x2d], [gate1],
                                [ssm_out_g, attn_out_g], [w_glu_f, w_out_f],
                                [(2 * D_SSM, F32), (D, BF16), (D, F32)], [], [])

    def final_out(x3, gf, fsc, fsh):
        r3, xh3 = _rms_stats(x3)
        n3 = xh3 * gf
        return r3, xh3, n3, n3 * (1.0 + fsc) + fsh

    def mlp_norm(xv, g2, sc, sh):
        r2, xh2 = _rms_stats(xv)
        n2 = xh2 * g2
        return r2, xh2, n2, n2 * (1.0 + sc) + sh

    def fwd_mlp(rows, bv, vecs, res):
        (xv, tg), (sc, sh, g2v, fsc, fsh), (g2, gf), (w1, w2) = rows, bv, vecs, res
        _, _, _, h2 = mlp_norm(xv, g2, sc, sh)
        ra = jnp.maximum(_dot(h2.astype(BF16), w1[...]), 0.0)
        ff = _dot((ra * ra).astype(BF16), w2[...])
        x3 = xv + g2v * ff
        _, _, _, out = final_out(x3, gf, fsc, fsh)
        err = out - tg
        loss = 0.5 * jnp.sum(jnp.mean(err * err, axis=-1, keepdims=True), axis=0, keepdims=True)
        return [ff, x3, ra], [], [jnp.broadcast_to(loss, (1, 128))]

    fnorm_g = final_norm_g[None, :]
    (ff, x3, ra_b), _, (loss_acc,) = row_call(
        "fwd_mlp", fwd_mlp, B, S, tm, [x2, tgt2d], [scale2, shift2, gate2, fscale, fshift], [norm2_g, fnorm_g],
        [w_ff1_f, w_ff2_f], [(D, BF16), (D, F32), (4 * D, BF16)], [], [(1, 128)])
    loss = lax.psum(loss_acc[0, 0], ("x", "y", "c"))

    def bwd_mlp(rows, bv, vecs, res):
        (x3v, tg, x2v, ffv, rav), (sc, sh, g2v, fsc, fsh), (g2, gf), (w1, w2) = rows, bv, vecs, res
        r3, xh3, n3, out = final_out(x3v, gf, fsc, fsh)
        dout = (out - tg) * (1.0 / D)
        dn3 = dout * (1.0 + fsc)
        dx3 = _rms_bwd(dn3 * gf, xh3, r3)
        dff = dx3 * g2v
        r2, xh2, n2, h2 = mlp_norm(x2v, g2, sc, sh)
        ra = rav.astype(F32)
        dffb = dff.astype(BF16)
        da = _dot_nt(dffb, w2[...]) * (2.0 * ra)
        dab = da.astype(BF16)
        dh2 = _dot_nt(dab, w1[...])
        dn2 = dh2 * (1.0 + sc)
        dx2 = dx3 + _rms_bwd(dn2 * g2, xh2, r2)
        return ([dx2, t_val(h2), t_val(ra * ra), dab, dffb],
                [_colsum(dout), _colsum(dout * n3), _colsum(dx3 * ffv), _colsum(dh2), _colsum(dh2 * n2)],
                [_colsum(dn3 * xh3), _colsum(dn2 * xh2)])

    ((dx2, h2b, fb, dab, dffb), (d_fshift, d_fscale, d_gate2, d_shift2, d_scale2), (d_gf, d_g2)) = row_call(
        "bwd_mlp", bwd_mlp, B, S, tm, [x3, tgt2d, x2, ff, ra_b], [scale2, shift2, gate2, fscale, fshift],
        [norm2_g, fnorm_g], [w_ff1_f, w_ff2_f],
        [(D, F32), t_out(D), t_out(4 * D), (4 * D, BF16), (D, BF16)], [D] * 5, [(1, D), (1, D)])
    g_w_ff1 = matmul_nn("grad_w_ff1", h2b, dab)
    g_w_ff2 = matmul_nn("grad_w_ff2", fb, dffb)

    def bwd_mix(rows, bv, vecs, res):
        (dxv, ov, zv, ya, yp), (g1v,), (gso, gao), (wo, wg) = rows, bv, vecs, res
        do = (dxv * g1v).astype(BF16)
        dyn = _dot_nt(do, wo[...])
        z1, sg = zv[:, :D_SSM], _sigmoid(zv[:, D_SSM:])
        y_ssm = z1 * sg
        rs, sh = _rms_stats(y_ssm)
        ra, ah = _rms_stats(ya)
        dyn_s, dyn_a = dyn[:, :D_SSM], dyn[:, D_SSM:]
        dy_ssm = _rms_bwd(dyn_s * gso, sh, rs)
        dy_attn = _rms_bwd(dyn_a * gao, ah, ra)
        dz = jnp.concatenate([dy_ssm * sg, dy_ssm * z1 * sg * (1.0 - sg)], axis=1).astype(BF16)
        dy_pre = _dot_nt(dz, wg[...]) * _gelu_grad(yp)
        yn = jnp.concatenate([sh * gso, ah * gao], axis=1)
        delta = jnp.sum((dy_attn * ya).T.reshape(N_HEADS, V_HEAD, tm), axis=1, keepdims=True)
        return ([do, t_val(yn), dz, t_val(_gelu(yp)), dy_attn, dy_pre, delta], [_colsum(dxv * ov)],
                [_colsum(dyn_s * sh), _colsum(dyn_a * ah)])

    ((dob, ynb, dzb, ygb, dy_attn, dy_pre, delta), (d_gate1,), (d_gso, d_gao)) = row_call(
        "bwd_mix", bwd_mix, B, S, tm, [dx2, o, z, y_attn, y_pre], [gate1], [ssm_out_g, attn_out_g],
        [w_out_f, w_glu_f],
        [(D, BF16), t_out(D), (2 * D_SSM, BF16), t_out(D_SSM), (V_W, F32), (D_SSM, F32),
         ((N_HEADS, 1, T), F32, (N_HEADS, 1, tm), lambda b, s: (0, 0, b * ns + s))],
        [D], [(1, D_SSM), (1, V_W)])
    g_w_out = matmul_nn("grad_w_out", ynb, dob)
    g_w_glu = matmul_nn("grad_w_glu", ygb, dzb)

    grads_a = Exchange([g_w_glu, g_w_out.reshape(N_DEV, -1, D), g_w_ff1, g_w_ff2.reshape(N_DEV, -1, D)],
                       [w_glu.shape[2], None, w_ff1.shape[2], None])
    dq_t, dk_r, dv_r, *parts_a = attention_bwd(q_r, k_r, k_t, v_r, dy_attn, lse, delta, B, S, tm, grads_a)
    parts = dict(zip(late, parts_a))
    dq_t = dq_t.reshape(B, QK_W, S)

    def bwd_qkv(rows, bv, vecs, res):
        (dqt, dkv, dvv, pr, cs, s_lo, s_hi), (gq, gkv, rmask), (wq, wkv) = rows, vecs, res
        dqv = dqt[0].T
        dq_t = []
        dk_rope = jnp.zeros((dkv.shape[0], HEAD_PAD), F32)
        for h in range(N_HEADS):
            hs = slice(h * HEAD_PAD, (h + 1) * HEAD_PAD)
            dq_t.append(_rope_tile(dqv[:, hs], cs, s_lo, s_hi, -1.0))
            dk_rope = dk_rope + dkv[:, hs]
        dk_rope = _rope_tile(dk_rope * rmask, cs, s_lo, s_hi, -1.0)
        dqb = jnp.concatenate(dq_t, axis=1).astype(BF16)
        dkvb = jnp.concatenate([dkv, dvv], axis=1).astype(BF16)
        rq, qh = _rms_stats(pr[:, D_SSM:D_SSM + Q_LORA])
        rk, kvh = _rms_stats(pr[:, D_SSM + Q_LORA:D_SSM + Q_LORA + KV_LORA])
        dqn = _dot_nt(dqb, wq[...])
        dkvn = _dot_nt(dkvb, wkv[...])
        dpa = jnp.concatenate([_rms_bwd(dqn * gq, qh, rq), _rms_bwd(dkvn * gkv, kvh, rk), dk_rope], axis=1)
        return [dpa, t_val(qh * gq), t_val(kvh * gkv), dqb, dkvb], [], [_colsum(dqn * qh), _colsum(dkvn * kvh)]

    ((dpa, qnb, kvnb, dqb, dkvb), _, (d_gq, d_gkv)) = row_call(
        "bwd_qkv", bwd_qkv, B, S, tm,
        [(dq_t, (1, QK_W, tm), lambda b, s: (b, 0, s)), dk_r, dv_r, proj, rope_cs, rope_lo, rope_hi], [],
        [q_norm_g, kv_norm_g, rope_mask], [w_uq_p, w_ukv_p],
        [(Q_LORA + KV_LORA + HEAD_PAD, F32), t_out(Q_LORA), t_out(KV_LORA), (QK_W, BF16), (QK_W + V_W, BF16)],
        [], [(1, Q_LORA), (1, KV_LORA)])
    g_w_uq = _unpad_w_uq(matmul_nn("grad_w_uq", qnb, dqb))
    g_w_ukv = _unpad_w_ukv(matmul_nn("grad_w_ukv", kvnb, dkvb))

    du, dc_re_d, dc_im_d, db_re_d, db_im_d, da_re, da_im, d_d = ssm_bwd(
        dy_pre, proj, st_re, st_im, bbd_re, bbd_im, cbd_re, cbd_im, d_vec, a_re, a_im, B, S, tm)
    place = lambda a: jnp.zeros((N_GROUPS, GROUP, N_STATE), F32).at[:, 0, :].set(
        a.reshape(N_GROUPS, N_STATE)).reshape(D_SSM, N_STATE)
    cots = [place(da_re), place(da_im),
            _block_diag_take(db_re_d, GROUP, N_STATE).reshape(D_SSM, N_STATE),
            _block_diag_take(db_im_d, GROUP, N_STATE).reshape(D_SSM, N_STATE)]
    g_lam_re, g_lam_im, g_log_dt, g_bt_re, g_bt_im = ssm_params_bwd(reps, cots)
    unbt = lambda a: jnp.transpose(a.reshape(N_GROUPS, GROUP, N_STATE), (0, 2, 1))
    g_c_re = jnp.transpose(_block_diag_take(dc_re_d, N_STATE, GROUP), (0, 2, 1))
    g_c_im = jnp.transpose(_block_diag_take(dc_im_d, N_STATE, GROUP), (0, 2, 1))

    def bwd_in(rows, bv, vecs, res):
        (duv, dpav, xv, dx2v), (sc, sh), (g1,), (w,) = rows, bv, vecs, res
        dproj = jnp.concatenate([duv, dpav], axis=1).astype(BF16)
        dh = _dot_nt(dproj, w[...])
        r, xh = _rms_stats(xv)
        n1 = xh * g1
        dn1 = dh * (1.0 + sc)
        dx = dx2v + _rms_bwd(dn1 * g1, xh, r)
        return [dx, t_val(n1 * (1.0 + sc) + sh), dproj], [_colsum(dh), _colsum(dh * n1)], [_colsum(dn1 * xh)]

    small_part = {
        "ssm_lambda_re": g_lam_re,
        "ssm_lambda_im": g_lam_im, "ssm_b_re": unbt(g_bt_re), "ssm_b_im": unbt(g_bt_im), "ssm_c_re": g_c_re,
        "ssm_c_im": g_c_im, "ssm_d": d_d, "ssm_log_dt": g_log_dt, "q_norm_g": d_gq, "kv_norm_g": d_gkv,
        "ssm_out_g": d_gso, "attn_out_g": d_gao, "norm2_g": d_g2, "final_norm_g": d_gf}
    dense_bf16 = lambda n, g: g.reshape(_dense_2d(given[n].shape)).astype(BF16)
    ready = [n for n in SMALL if n in small_part]
    grads_b = Exchange([dense_bf16(n, small_part[n]) for n in ready] + [_cols_to_shards(g_w_uq), g_w_ukv],
                       [GATHER] * len(ready) + [None, w_ukv.shape[2]])
    ((grad_x, h1b, dprojb), (d_shift1, d_scale1), (d_g1,)) = row_call(
        "bwd_in", bwd_in, B, S, tm, [du, dpa, x2d, dx2], [scale1, shift1], [norm1_g], [w_in_p],
        [(D, F32), t_out(D), (IN_PAD, BF16)], [D, D], [(1, D)])
    g_w_in_p, *side_b = matmul_nn("grad_w_in", h1b, dprojb, side=grads_b)
    small_parts = dict(zip(ready, side_b[:len(ready)]))
    parts["w_uq"], parts["w_ukv"] = side_b[len(ready):]
    g_w_in = _unpad_w_in(g_w_in_p)

    dmod = jnp.concatenate([d_shift1, d_scale1, d_gate1, d_shift2, d_scale2, d_gate2, d_fshift, d_fscale],
                           axis=2).reshape(B, 8 * D)
    dmod_g, small_parts["norm1_g"], parts["w_in"] = exchange(
        "exchange_last_grads", [dmod.astype(BF16), dense_bf16("norm1_g", d_g1), _cols_to_shards(g_w_in)],
        [GATHER, GATHER, None])
    small_g = [small_parts[n] for n in SMALL]
    dmod_all = dmod_g.reshape(N_DEV * B, 8 * D)
    dm_sh = lax.dynamic_slice_in_dim(dmod_all[:, :6 * D], me * n_mod, n_mod, axis=1)
    dfm_sh = lax.dynamic_slice_in_dim(dmod_all[:, 6 * D:], me * n_fmod, n_fmod, axis=1)
    g_ada_w, g_fada_w, g_biases = modulation_bwd(cond_all, dm_sh, dfm_sh, dmod_all)
    parts["ada_w"] = g_ada_w[None]
    parts["final_ada_w"] = g_fada_w[None]

    res_g, res_d, res_m, res_v = {}, {}, {}, {}
    for n in ["ada_w"] + big + ["final_ada_w"]:
        w_full = given[n]
        w2 = w_full.reshape(w_full.shape[-2], w_full.shape[-1])
        out = adamw("adamw_" + n, parts[n], w2, given["m_" + n].reshape(w2.shape), given["v_" + n].reshape(w2.shape))
        res_g[n], res_d[n], res_m[n], res_v[n] = [a.reshape(w_full.shape) for a in out]
    bias_g = {"ada_b": g_biases[None, :, :6 * D], "final_ada_b": g_biases[None, :, 6 * D:]}
    names = SMALL + BIASES
    items = []
    for n, g_parts in zip(names, small_g + [bias_g[b] for b in BIASES]):
        dense = _dense_2d(given[n].shape)
        items.append((g_parts, given[n].reshape(dense), given["m_" + n].reshape(dense), given["v_" + n].reshape(dense)))
    for n, out in zip(names, adamw_many("adamw_small", items)):
        res_g[n], res_d[n], res_m[n], res_v[n] = [a.reshape(given[n].shape) for a in out]

    order = ["ada_w", "ada_b", "norm1_g", "w_in", "ssm_lambda_re", "ssm_lambda_im", "ssm_b_re", "ssm_b_im",
             "ssm_c_re", "ssm_c_im", "ssm_d", "ssm_log_dt", "w_glu", "q_norm_g", "w_uq", "kv_norm_g", "w_ukv",
             "ssm_out_g", "attn_out_g", "w_out", "norm2_g", "w_ff1", "w_ff2", "final_ada_w", "final_ada_b",
             "final_norm_g"]
    return (loss, grad_x.reshape(B, S, D), *[res_g[n] for n in order], *[res_d[n] for n in order],
            *[res_m[n] for n in order], *[res_v[n] for n in order])
```

```python
import functools
import math

import numpy as np
import jax
import jax.numpy as jnp
from jax import lax
from jax.experimental import pallas as pl
from jax.experimental.pallas import tpu as pltpu

F32 = jnp.float32
BF16 = jnp.bfloat16

N_DEV = 8
EPS = 1e-6
D_SSM = 512
N_GROUPS = 32
GROUP = 16
N_STATE = 64
N_ST = N_GROUPS * N_STATE
Q_LORA = 384
KV_LORA = 256
QK_NOPE = 64
QK_ROPE = 32
V_HEAD = 64
N_HEADS = 8
HEAD_PAD = 128
QK_W = N_HEADS * HEAD_PAD
V_W = N_HEADS * V_HEAD
IN_COLS = D_SSM + Q_LORA + KV_LORA + QK_ROPE
IN_PAD = D_SSM + Q_LORA + KV_LORA + HEAD_PAD
ROPE_BASE = 10000.0
ATTN_SCALE = (QK_NOPE + QK_ROPE) ** -0.5
NEG = -1e30

ADAM_LR = 0.001
ADAM_B1 = 0.9
ADAM_B2 = 0.999
ADAM_EPS = 1e-08
ADAM_WD = 0.01
ADAM_STEP = 10

VMEM_LIMIT = 56 * 1024 * 1024
MESH_ID = pl.DeviceIdType.MESH


def _dot(a, b):
    return jnp.dot(a, b, preferred_element_type=F32)


def _dot_nt(a, b):
    return lax.dot_general(a, b, (((1,), (1,)), ((), ())), preferred_element_type=F32)


def _dot_tn(a, b):
    return lax.dot_general(a, b, (((0,), (0,)), ((), ())), preferred_element_type=F32)


def _rms_stats(x):
    r = lax.rsqrt(jnp.mean(x * x, axis=-1, keepdims=True) + EPS)
    return r, x * r


def _rms_bwd(dy, xh, r):
    return r * (dy - xh * jnp.mean(dy * xh, axis=-1, keepdims=True))


def _sigmoid(x):
    return 1.0 / (1.0 + jnp.exp(-x))


_GELU_C = math.sqrt(2.0 / math.pi)


def _gelu(x):
    t = jnp.tanh(_GELU_C * (x + 0.044715 * x * x * x))
    return 0.5 * x * (1.0 + t)


def _gelu_grad(x):
    t = jnp.tanh(_GELU_C * (x + 0.044715 * x * x * x))
    return 0.5 * (1.0 + t) + 0.5 * x * (1.0 - t * t) * _GELU_C * (1.0 + 3.0 * 0.044715 * x * x)


def _colsum(a):
    return jnp.sum(a, axis=0, keepdims=True)


def _params(sem=None):
    return pltpu.CompilerParams(dimension_semantics=sem, vmem_limit_bytes=VMEM_LIMIT)


def _mesh_pos():
    x, y, c = lax.axis_index("x"), lax.axis_index("y"), lax.axis_index("c")
    return x, y, c, 4 * x + 2 * y + c


def _peer(x, y, c, k):
    px = 1 - x if k & 4 else x
    py = 1 - y if k & 2 else y
    pc = 1 - c if k & 1 else c
    return (px, py, pc), 4 * px + 2 * py + pc


GATHER = "gather"
ANY_SPEC = pl.BlockSpec(memory_space=pl.ANY)


class Exchange:
    def __init__(self, arrays, kinds):
        self.arrays, self.kinds, self.n = list(arrays), list(kinds), len(arrays)

    def _shard(self, i):
        a, kind = self.arrays[i], self.kinds[i]
        if kind == GATHER:
            return a.shape
        return a.shape[1:] if kind is None else (a.shape[0], kind)

    def out_shapes(self):
        return [jax.ShapeDtypeStruct((N_DEV,) + tuple(self._shard(i)), self.arrays[i].dtype) for i in range(self.n)]

    def scratch(self):
        return [pltpu.SemaphoreType.DMA((self.n, N_DEV - 1)), pltpu.SemaphoreType.DMA((self.n, N_DEV - 1)),
                pltpu.SemaphoreType.DMA((self.n,))]

    def _src(self, ins, i, j):
        kind = self.kinds[i]
        if kind == GATHER:
            return ins[i]
        if kind is None:
            return ins[i].at[j]
        return ins[i].at[:, pl.ds(pl.multiple_of(j * kind, 128), kind)]

    def _copies(self, ins, outs, sems, with_received):
        send_sems, recv_sems, loc_sems = sems
        x, y, c, me = _mesh_pos()
        local, sent, received = [], [], []
        for i in range(self.n):
            local.append(pltpu.make_async_copy(self._src(ins, i, me), outs[i].at[me], loc_sems.at[i]))
            for k in range(1, N_DEV):
                peer, pidx = _peer(x, y, c, k)
                pair = dict(send_sem=send_sems.at[i, k - 1], recv_sem=recv_sems.at[i, k - 1], device_id=peer,
                            device_id_type=MESH_ID)
                sent.append(pltpu.make_async_remote_copy(
                    src_ref=self._src(ins, i, pidx), dst_ref=outs[i].at[me], **pair))
                if with_received:
                    received.append(pltpu.make_async_remote_copy(
                        src_ref=self._src(ins, i, pidx), dst_ref=outs[i].at[pidx], **pair))
        return local, sent, received

    def start(self, ins, outs, sems):
        local, sent, _ = self._copies(ins, outs, sems, False)
        for cp in local + sent:
            cp.start()

    def wait(self, ins, outs, sems):
        local, sent, received = self._copies(ins, outs, sems, True)
        for cp in received:
            cp.wait_recv()
        for cp in sent:
            cp.wait_send()
        for cp in local:
            cp.wait()


def exchange(name, arrays, kinds):
    ex = Exchange(arrays, kinds)
    n = ex.n

    def body(*refs):
        ins, outs, sems = refs[:n], refs[n:2 * n], refs[2 * n:]
        ex.start(ins, outs, sems)
        ex.wait(ins, outs, sems)

    return pl.pallas_call(
        body, name=name, out_shape=ex.out_shapes(), in_specs=[ANY_SPEC] * n, out_specs=[ANY_SPEC] * n,
        scratch_shapes=ex.scratch())(*arrays)


def all_gather(name, arrays):
    return exchange(name, arrays, [GATHER] * len(arrays))


def row_call(name, body, B, S, tm, rows, bvecs, vecs, res, row_outs, bacc, gacc, side=None):
    ns = S // tm
    T = B * S
    n_r, n_b, n_v, n_w = len(rows), len(bvecs), len(vecs), len(res)
    n_ro, n_ba, n_ga = len(row_outs), len(bacc), len(gacc)
    n_x = side.n if side is not None else 0

    def kern(*refs):
        i = 0
        r_refs = refs[i:i + n_r]; i += n_r
        b_refs = refs[i:i + n_b]; i += n_b
        v_refs = refs[i:i + n_v]; i += n_v
        w_hbm = refs[i:i + n_w]; i += n_w
        x_in = refs[i:i + n_x]; i += n_x
        ro_refs = refs[i:i + n_ro]; i += n_ro
        ba_refs = refs[i:i + n_ba]; i += n_ba
        ga_refs = refs[i:i + n_ga]; i += n_ga
        x_out = refs[i:i + n_x]; i += n_x
        w_vmem = refs[i:i + n_w]; i += n_w
        x_sems = refs[i:]
        b = pl.program_id(0)
        s = pl.program_id(1)
        first = jnp.logical_and(b == 0, s == 0)

        if n_x:
            @pl.when(first)
            def _start_side():
                side.start(x_in, x_out, x_sems)

        if n_w:
            @pl.when(first)
            def _load_weights():
                for h, v in zip(w_hbm, w_vmem):
                    pltpu.sync_copy(h, v)

        ro, ba, ga = body([r[...] for r in r_refs], [r[0] for r in b_refs], [r[...] for r in v_refs],
                          list(w_vmem))
        for ref, val in zip(ro_refs, ro):
            ref[...] = val.astype(ref.dtype)

        def accumulate(ref, val, is_first, idx):
            @pl.when(is_first)
            def _set():
                ref[idx] = val

            @pl.when(jnp.logical_not(is_first))
            def _add():
                ref[idx] = ref[idx] + val

        for ref, val in zip(ba_refs, ba):
            accumulate(ref, val, s == 0, 0)
        for ref, val in zip(ga_refs, ga):
            accumulate(ref, val, first, Ellipsis)

        if n_x:
            @pl.when(jnp.logical_and(b == B - 1, s == ns - 1))
            def _wait_side():
                side.wait(x_in, x_out, x_sems)

    row_arrays = [r[0] if isinstance(r, tuple) else r for r in rows]
    row_in_specs = [pl.BlockSpec(r[1], r[2]) if isinstance(r, tuple)
                    else pl.BlockSpec((tm, r.shape[1]), lambda b, s: (b * ns + s, 0)) for r in rows]
    ro_shapes = [jax.ShapeDtypeStruct(tuple(o[0]), o[1]) if len(o) == 4 else jax.ShapeDtypeStruct((T, o[0]), o[1])
                 for o in row_outs]
    ro_specs = [pl.BlockSpec(o[2], o[3]) if len(o) == 4 else pl.BlockSpec((tm, o[0]), lambda b, s: (b * ns + s, 0))
                for o in row_outs]
    in_specs = (row_in_specs
                + [pl.BlockSpec((1, 1, v.shape[2]), lambda b, s: (b, 0, 0)) for v in bvecs]
                + [pl.BlockSpec(v.shape, lambda b, s, nd=v.ndim: (0,) * nd) for v in vecs]
                + [ANY_SPEC] * (n_w + n_x))
    out_shape = (ro_shapes
                 + [jax.ShapeDtypeStruct((B, 1, w), F32) for w in bacc]
                 + [jax.ShapeDtypeStruct(tuple(sh), F32) for sh in gacc]
                 + (side.out_shapes() if n_x else []))
    out_specs = (ro_specs
                 + [pl.BlockSpec((1, 1, w), lambda b, s: (b, 0, 0)) for w in bacc]
                 + [pl.BlockSpec(tuple(sh), lambda b, s, nd=len(sh): (0,) * nd) for sh in gacc]
                 + [ANY_SPEC] * n_x)
    outs = pl.pallas_call(
        kern, name=name, grid=(B, ns), in_specs=in_specs, out_specs=out_specs, out_shape=out_shape,
        scratch_shapes=[pltpu.VMEM(w.shape, w.dtype) for w in res] + (side.scratch() if n_x else []),
        compiler_params=_params(("arbitrary", "arbitrary")),
    )(*row_arrays, *bvecs, *vecs, *res, *(side.arrays if n_x else []))
    n_acc = n_ro + n_ba + n_ga
    if n_x:
        return outs[:n_ro], outs[n_ro:n_ro + n_ba], outs[n_ro + n_ba:n_acc], outs[n_acc:]
    return outs[:n_ro], outs[n_ro:n_ro + n_ba], outs[n_ro + n_ba:]


def matmul_nn(name, at, b, side=None):
    K, T = at.shape
    N = b.shape[1]
    tk = 1024 if K % 1024 == 0 else min(K, 512)
    tn = N if N <= 1280 else (2048 if N % 2048 == 0 else 512)
    assert K % tk == 0 and N % tn == 0 and T % min(T, 2048) == 0, (K, N, T)
    tt = min(T, 2048)
    nt = T // tt
    n_x = side.n if side is not None else 0
    grid = (K // tk, N // tn, nt)

    def kern(*refs):
        a_ref, b_ref = refs[:2]
        x_in = refs[2:2 + n_x]
        o_ref = refs[2 + n_x]
        x_out = refs[3 + n_x:3 + 2 * n_x]
        acc_ref = refs[3 + 2 * n_x]
        x_sems = refs[4 + 2 * n_x:]
        i, j, t = pl.program_id(0), pl.program_id(1), pl.program_id(2)
        if n_x:
            @pl.when(jnp.logical_and(i == 0, jnp.logical_and(j == 0, t == 0)))
            def _start_side():
                side.start(x_in, x_out, x_sems)

        part = _dot(a_ref[...], b_ref[...])

        @pl.when(t == 0)
        def _set():
            acc_ref[...] = part

        @pl.when(t > 0)
        def _add():
            acc_ref[...] = acc_ref[...] + part

        @pl.when(t == nt - 1)
        def _write():
            o_ref[...] = acc_ref[...].astype(o_ref.dtype)

        if n_x:
            @pl.when(jnp.logical_and(i == grid[0] - 1, jnp.logical_and(j == grid[1] - 1, t == nt - 1)))
            def _wait_side():
                side.wait(x_in, x_out, x_sems)

    outs = pl.pallas_call(
        kern, name=name, grid=grid,
        in_specs=[pl.BlockSpec((tk, tt), lambda i, j, t: (i, t)), pl.BlockSpec((tt, tn), lambda i, j, t: (t, j))]
        + [ANY_SPEC] * n_x,
        out_specs=[pl.BlockSpec((tk, tn), lambda i, j, t: (i, j))] + [ANY_SPEC] * n_x,
        out_shape=[jax.ShapeDtypeStruct((K, N), BF16)] + (side.out_shapes() if n_x else []),
        scratch_shapes=[pltpu.VMEM((tk, tn), F32)] + (side.scratch() if n_x else []),
        compiler_params=_params(("arbitrary",) * 3 if n_x else ("parallel", "parallel", "arbitrary")),
    )(at, b, *(side.arrays if n_x else []))
    return outs if n_x else outs[0]


def _rope_consts():
    inv_freq = ROPE_BASE ** (-jnp.arange(0, QK_ROPE, 2, dtype=F32) / QK_ROPE)
    zeros64 = jnp.zeros((64,), F32)
    zeros32 = jnp.zeros((32,), F32)
    zeros16 = jnp.zeros((16,), F32)
    ones16 = jnp.ones((16,), F32)
    invf = jnp.concatenate([zeros64, inv_freq, inv_freq, zeros32])[None, :]
    m_lo = jnp.concatenate([zeros64, ones16, zeros16, zeros32])[None, :]
    m_hi = jnp.concatenate([zeros64, zeros16, ones16, zeros32])[None, :]
    return invf, m_lo, m_hi


def _rope_tables(pos, invf, m_lo, m_hi):
    ang = pos * invf
    return jnp.cos(ang), jnp.sin(ang) * m_lo, jnp.sin(ang) * m_hi


def _rope_tile(t, cs, s_lo, s_hi, sign):
    up = pltpu.roll(t, HEAD_PAD - 16, axis=1)
    down = pltpu.roll(t, 16, axis=1)
    return t * cs + sign * (down * s_hi - up * s_lo)


def _heads(ref, width):
    if len(ref.shape) == 2:
        return jnp.stack([ref[:, h * width:(h + 1) * width] for h in range(N_HEADS)], axis=0)
    return jnp.stack([ref[b, :, h * width:(h + 1) * width] for b in range(ref.shape[0]) for h in range(N_HEADS)],
                     axis=0)


def _bmm(a, b, ca, cb):
    return lax.dot_general(a, b, (((ca,), (cb,)), ((0,), (0,))), preferred_element_type=F32)


LOG2E = math.log2(math.e)
EXP2_SCALE = ATTN_SCALE * LOG2E


def _scores_t(k_ref, q_ref, masked, tq, key_offset=0):
    st = _bmm(_heads(k_ref, HEAD_PAD), _heads(q_ref, HEAD_PAD), 2, 2)
    if masked:
        tk = st.shape[1]
        key = lax.broadcasted_iota(jnp.int32, (tk, tq), 0) + key_offset
        qry = lax.broadcasted_iota(jnp.int32, (tk, tq), 1)
        st = jnp.where((key <= qry)[None], st, NEG)
    return st


def attention_fwd(q, k, vt, B, S, tq, side):
    tk = tq
    tq = min(2 * tk, S)
    r = tq // tk
    nq = S // tq
    ns = side.n
    nbh = B * N_HEADS
    pairs = [(i, j) for i in range(nq) for j in range(r * (i + 1))]
    n_pairs = len(pairs)
    q_tab = jnp.asarray([p[0] for p in pairs], jnp.int32)
    k_tab = jnp.asarray([p[1] for p in pairs], jnp.int32)

    def kern(q_tab_ref, k_tab_ref, *refs):
        q_ref, k_ref, vt_ref = refs[:3]
        side_in = refs[3:3 + ns]
        o_ref, lse_ref = refs[3 + ns:5 + ns]
        side_out = refs[5 + ns:5 + 2 * ns]
        m_scr, l_scr, acc_scr = refs[5 + 2 * ns:8 + 2 * ns]
        sems = refs[8 + 2 * ns:]
        t = pl.program_id(0)
        qi, ki = q_tab_ref[t], k_tab_ref[t]

        @pl.when(t == 0)
        def _start_side():
            side.start(side_in, side_out, sems)

        @pl.when(ki == 0)
        def _init():
            m_scr[...] = jnp.full(m_scr.shape, NEG, F32)
            l_scr[...] = jnp.zeros(l_scr.shape, F32)
            acc_scr[...] = jnp.zeros(acc_scr.shape, F32)

        def block(masked):
            st = _scores_t(k_ref, q_ref, masked, tq, ki * tk - qi * tq)
            m_prev = m_scr[...]
            m_new = jnp.maximum(m_prev, jnp.max(st, axis=1, keepdims=True))
            alpha = jnp.exp2((m_prev - m_new) * EXP2_SCALE)
            p = jnp.exp2((st - m_new) * EXP2_SCALE)
            l_scr[...] = alpha * l_scr[...] + jnp.sum(p, axis=1, keepdims=True)
            vt3 = vt_ref[...].reshape(nbh, V_HEAD, tk)
            acc_scr[...] = alpha * acc_scr[...] + _bmm(vt3, p.astype(BF16), 2, 1)
            m_scr[...] = m_new

        @pl.when(ki < r * qi)
        def _full():
            block(False)

        @pl.when(ki >= r * qi)
        def _diagonal():
            block(True)

        @pl.when(ki == r * (qi + 1) - 1)
        def _finish():
            ot = acc_scr[...] / l_scr[...]
            for b in range(B):
                for h in range(N_HEADS):
                    o_ref[b, :, h * V_HEAD:(h + 1) * V_HEAD] = ot[b * N_HEADS + h].T
            lse_ref[...] = m_scr[...] * ATTN_SCALE + jnp.log(l_scr[...])

        @pl.when(t == n_pairs - 1)
        def _wait_side():
            side.wait(side_in, side_out, sems)

    grid_spec = pltpu.PrefetchScalarGridSpec(
        num_scalar_prefetch=2, grid=(n_pairs,),
        in_specs=[pl.BlockSpec((B, tq, QK_W), lambda t, qt, kt: (0, qt[t], 0)),
                  pl.BlockSpec((B, tk, QK_W), lambda t, qt, kt: (0, kt[t], 0)),
                  pl.BlockSpec((B, V_W, tk), lambda t, qt, kt: (0, 0, kt[t]))] + [ANY_SPEC] * ns,
        out_specs=[pl.BlockSpec((B, tq, V_W), lambda t, qt, kt: (0, qt[t], 0)),
                   pl.BlockSpec((nbh, 1, tq), lambda t, qt, kt: (0, 0, qt[t]))] + [ANY_SPEC] * ns,
        scratch_shapes=[pltpu.VMEM((nbh, 1, tq), F32), pltpu.VMEM((nbh, 1, tq), F32),
                        pltpu.VMEM((nbh, V_HEAD, tq), F32)] + side.scratch())
    return pl.pallas_call(
        kern, name="attention_fwd", grid_spec=grid_spec,
        out_shape=[jax.ShapeDtypeStruct((B, S, V_W), F32), jax.ShapeDtypeStruct((nbh, 1, S), F32)]
        + side.out_shapes(),
        compiler_params=_params(("arbitrary",)),
    )(q_tab, k_tab, q, k, vt, *side.arrays)


def attention_bwd(q, k, kt, v, do, lse, delta, B, S, tq, side):
    tk = tq
    tq = min(2 * tk, S)
    r = tq // tk
    nk, nq = S // tk, S // tq
    ns = side.n
    pairs = [(j, i) for j in range(nk) for i in range(j // r, nq)]
    n_pairs = len(pairs)
    k_tab = jnp.asarray([p[0] for p in pairs], jnp.int32)
    q_tab = jnp.asarray([p[1] for p in pairs], jnp.int32)

    def kern(k_tab_ref, q_tab_ref, *refs):
        q_ref, k_ref, kt_ref, v_ref, do_ref, lse_ref, delta_ref = refs[:7]
        side_in = refs[7:7 + ns]
        dq_hbm, dk_ref, dv_ref = refs[7 + ns:10 + ns]
        side_out = refs[10 + ns:10 + 2 * ns]
        dq_acc, dk_acc, dv_acc = refs[10 + 2 * ns:13 + 2 * ns]
        sems = refs[13 + 2 * ns:]
        b, t = pl.program_id(0), pl.program_id(1)
        kj, qi = k_tab_ref[t], q_tab_ref[t]
        on_diagonal = qi == kj // r

        @pl.when(jnp.logical_and(b == 0, t == 0))
        def _start_side():
            side.start(side_in, side_out, sems)

        @pl.when(t == 0)
        def _zero_dq():
            dq_acc[...] = jnp.zeros(dq_acc.shape, F32)

        @pl.when(on_diagonal)
        def _zero_dkv():
            dk_acc[...] = jnp.zeros(dk_acc.shape, F32)
            dv_acc[...] = jnp.zeros(dv_acc.shape, F32)

        def block(masked):
            st = _scores_t(k_ref, q_ref, masked, tq, kj * tk - qi * tq)
            pt = jnp.exp2(st * EXP2_SCALE - lse_ref[...] * LOG2E)
            do3 = _heads(do_ref, V_HEAD).astype(BF16)
            dv_acc[...] = dv_acc[...] + _bmm(pt.astype(BF16), do3, 2, 1)
            dpt = _bmm(_heads(v_ref, V_HEAD), do3, 2, 2)
            dst = (pt * (dpt - delta_ref[...]) * ATTN_SCALE).astype(BF16)
            dk_acc[...] = dk_acc[...] + _bmm(dst, _heads(q_ref, HEAD_PAD), 2, 1)
            q_cols = pl.ds(pl.multiple_of(qi * tq, tq), tq)
            kt3 = kt_ref[...].reshape(N_HEADS, HEAD_PAD, tk)
            dq_acc[:, :, q_cols] = dq_acc[:, :, q_cols] + _bmm(kt3, dst, 2, 1)

        @pl.when(jnp.logical_not(on_diagonal))
        def _full():
            block(False)

        @pl.when(on_diagonal)
        def _diagonal():
            block(True)

        @pl.when(qi == nq - 1)
        def _write_dkv():
            for h in range(N_HEADS):
                dk_ref[:, h * HEAD_PAD:(h + 1) * HEAD_PAD] = dk_acc[h]
                dv_ref[:, h * V_HEAD:(h + 1) * V_HEAD] = dv_acc[h]

        @pl.when(t == n_pairs - 1)
        def _write_dq():
            pltpu.sync_copy(dq_acc, dq_hbm.at[b])

        @pl.when(jnp.logical_and(b == B - 1, t == n_pairs - 1))
        def _wait_side():
            side.wait(side_in, side_out, sems)

    T = B * S
    qrow = lambda b, t, kt, qt: (b * nq + qt[t], 0)
    qcol3 = lambda b, t, kt, qt: (0, 0, b * nq + qt[t])
    krow = lambda b, t, kt, qt: (b * nk + kt[t], 0)
    grid_spec = pltpu.PrefetchScalarGridSpec(
        num_scalar_prefetch=2, grid=(B, n_pairs),
        in_specs=[pl.BlockSpec((tq, QK_W), qrow), pl.BlockSpec((tk, QK_W), krow),
                  pl.BlockSpec((QK_W, tk), lambda b, t, kt, qt: (0, b * nk + kt[t])), pl.BlockSpec((tk, V_W), krow),
                  pl.BlockSpec((tq, V_W), qrow), pl.BlockSpec((N_HEADS, 1, tq), qcol3),
                  pl.BlockSpec((N_HEADS, 1, tq), qcol3)] + [ANY_SPEC] * ns,
        out_specs=[ANY_SPEC, pl.BlockSpec((tk, QK_W), krow), pl.BlockSpec((tk, V_W), krow)] + [ANY_SPEC] * ns,
        scratch_shapes=[pltpu.VMEM((N_HEADS, HEAD_PAD, S), F32), pltpu.VMEM((N_HEADS, tk, HEAD_PAD), F32),
                        pltpu.VMEM((N_HEADS, tk, V_HEAD), F32)] + side.scratch())
    return pl.pallas_call(
        kern, name="attention_bwd", grid_spec=grid_spec,
        out_shape=[jax.ShapeDtypeStruct((B, N_HEADS, HEAD_PAD, S), F32), jax.ShapeDtypeStruct((T, QK_W), F32),
                   jax.ShapeDtypeStruct((T, V_W), F32)] + side.out_shapes(),
        compiler_params=_params(("arbitrary", "arbitrary")),
    )(k_tab, q_tab, q, k, kt, v, do, lse, delta, *side.arrays)


GB = 8
N_GB = N_GROUPS // GB
GB_U = GB * GROUP
GB_ST = GB * N_STATE


def _gb_u(j):
    return slice(j * GB_U, (j + 1) * GB_U)


def _gb_s(j):
    return slice(j * GB_ST, (j + 1) * GB_ST)


def ssm_param_fn(lr, li, ld, br, bi):
    dt = jnp.exp(ld)
    er = jnp.exp(lr * dt)
    ar = er * jnp.cos(li * dt)
    ai = er * jnp.sin(li * dt)
    nr = ar - 1.0
    den = lr * lr + li * li
    cr = (nr * lr + ai * li) / den
    ci = (ai * lr - nr * li) / den
    return ar, ai, cr * br - ci * bi, cr * bi + ci * br


def ssm_params_fwd(reps):
    def kern(lr, li, ld, br, bi, ar_o, ai_o, bbr_o, bbi_o):
        ar, ai, bbr, bbi = ssm_param_fn(lr[...], li[...], ld[...], br[...], bi[...])
        ar_o[...] = ar
        ai_o[...] = ai
        bbr_o[...] = bbr
        bbi_o[...] = bbi

    shp = jax.ShapeDtypeStruct(reps[0].shape, F32)
    return pl.pallas_call(kern, name="ssm_params_fwd", out_shape=[shp] * 4)(*reps)


def ssm_params_bwd(reps, cots):
    rows = reps[0].shape[0]

    def kern(lr, li, ld, br, bi, d_ar, d_ai, d_bbr, d_bbi, dlr_o, dli_o, dld_o, dbr_o, dbi_o):
        _, vjp = jax.vjp(ssm_param_fn, lr[...], li[...], ld[...], br[...], bi[...])
        dlr, dli, dld, dbr, dbi = vjp((d_ar[...], d_ai[...], d_bbr[...], d_bbi[...]))
        dlr_o[...] = jnp.sum(dlr.reshape(N_GROUPS, GROUP, N_STATE), axis=1)
        dli_o[...] = jnp.sum(dli.reshape(N_GROUPS, GROUP, N_STATE), axis=1)
        dld_o[...] = jnp.sum(jnp.sum(dld.reshape(N_GROUPS, GROUP, N_STATE), axis=1), axis=-1, keepdims=True)
        dbr_o[...] = dbr
        dbi_o[...] = dbi

    g = jax.ShapeDtypeStruct((N_GROUPS, N_STATE), F32)
    full = jax.ShapeDtypeStruct((rows, N_STATE), F32)
    return pl.pallas_call(
        kern, name="ssm_params_bwd",
        out_shape=[g, g, jax.ShapeDtypeStruct((N_GROUPS, 1), F32), full, full])(*reps, *cots)


def ssm_fwd(proj, b_re, b_im, c_re, c_im, d_vec, a_re, a_im, B, S, ts):
    ns = S // ts
    T = B * S

    def kern(u_ref, bre_ref, bim_ref, cre_ref, cim_ref, d_ref, are_ref, aim_ref,
             sre_ref, sim_ref, y_ref, car_re, car_im, bu_re, bu_im):
        s = pl.program_id(0)

        @pl.when(s == 0)
        def _reset():
            car_re[...] = jnp.zeros(car_re.shape, F32)
            car_im[...] = jnp.zeros(car_im.shape, F32)

        for b in range(B):
            ub = u_ref[b].astype(BF16)
            for j in range(N_GB):
                uj, bj, sj = ub[:, _gb_u(j)], _gb_u(j), _gb_s(j)
                bu_re[b, :, sj] = _dot(uj, bre_ref[bj, :])
                bu_im[b, :, sj] = _dot(uj, bim_ref[bj, :])
        ar, ai = are_ref[...], aim_ref[...]

        def step(t, carry):
            row = pl.ds(t, 1)
            out = []
            for b in range(B):
                xr, xi = carry[2 * b], carry[2 * b + 1]
                nr = ar * xr - ai * xi + bu_re[b, row, :]
                ni = ar * xi + ai * xr + bu_im[b, row, :]
                sre_ref[b, row, :] = nr
                sim_ref[b, row, :] = ni
                out += [nr, ni]
            return tuple(out)

        init = tuple(c[b, 0:1, :] for b in range(B) for c in (car_re, car_im))
        fin = lax.fori_loop(0, ts, step, init)
        for b in range(B):
            car_re[b, 0:1, :] = fin[2 * b]
            car_im[b, 0:1, :] = fin[2 * b + 1]
            y = [_dot(sre_ref[b, :, _gb_s(j)].astype(BF16), cre_ref[_gb_s(j), :])
                 - _dot(sim_ref[b, :, _gb_s(j)].astype(BF16), cim_ref[_gb_s(j), :]) for j in range(N_GB)]
            y_ref[b] = jnp.concatenate(y, axis=1) + d_ref[...] * u_ref[b]

    tile = lambda s: (0, s, 0)
    whole = lambda s: (0, 0)
    st_re, st_im, y = pl.pallas_call(
        kern, name="ssm_fwd", grid=(ns,),
        in_specs=[pl.BlockSpec((B, ts, D_SSM), tile),
                  pl.BlockSpec((D_SSM, GB_ST), whole), pl.BlockSpec((D_SSM, GB_ST), whole),
                  pl.BlockSpec((N_ST, GB_U), whole), pl.BlockSpec((N_ST, GB_U), whole),
                  pl.BlockSpec((1, D_SSM), whole), pl.BlockSpec((1, N_ST), whole), pl.BlockSpec((1, N_ST), whole)],
        out_specs=[pl.BlockSpec((B, ts, N_ST), tile), pl.BlockSpec((B, ts, N_ST), tile),
                   pl.BlockSpec((B, ts, D_SSM), tile)],
        out_shape=[jax.ShapeDtypeStruct((B, S, N_ST), F32), jax.ShapeDtypeStruct((B, S, N_ST), F32),
                   jax.ShapeDtypeStruct((B, S, D_SSM), F32)],
        scratch_shapes=[pltpu.VMEM((B, 8, N_ST), F32), pltpu.VMEM((B, 8, N_ST), F32),
                        pltpu.VMEM((B, ts, N_ST), F32), pltpu.VMEM((B, ts, N_ST), F32)],
        compiler_params=_params(("arbitrary",)),
    )(proj.reshape(B, S, proj.shape[1]), b_re, b_im, c_re, c_im, d_vec, a_re, a_im)
    return st_re.reshape(T, N_ST), st_im.reshape(T, N_ST), y.reshape(T, D_SSM)


def ssm_bwd(dy, proj, s_re, s_im, b_re, b_im, c_re, c_im, d_vec, a_re, a_im, B, S, ts):
    ns = S // ts
    T = B * S
    t8 = ts // 8

    def kern(dy_ref, u_ref, sre_ref, sim_ref, pre_ref, pim_ref, bre_ref, bim_ref, cre_ref, cim_ref,
             d_ref, are_ref, aim_ref,
             du_ref, dcre_ref, dcim_ref, dbre_ref, dbim_ref, dare_ref, daim_ref, dd_ref,
             car_re, car_im, g_re, g_im):
        b, s = pl.program_id(0), pl.program_id(1)
        first = jnp.logical_and(b == 0, s == 0)

        @pl.when(s == 0)
        def _reset():
            car_re[...] = jnp.zeros(car_re.shape, F32)
            car_im[...] = jnp.zeros(car_im.shape, F32)

        dyv = dy_ref[...]
        dyb = dyv.astype(BF16)
        u = u_ref[...]
        for j in range(N_GB):
            g_re[:, _gb_s(j)] = _dot_nt(dyb[:, _gb_u(j)], cre_ref[_gb_s(j), :])
            g_im[:, _gb_s(j)] = -_dot_nt(dyb[:, _gb_u(j)], cim_ref[_gb_s(j), :])
        ar, ai = are_ref[...], aim_ref[...]

        def step(i, carry):
            gr_next, gi_next = carry
            row = pl.ds(ts - 1 - i, 1)
            gr = g_re[row, :] + ar * gr_next + ai * gi_next
            gi = g_im[row, :] + ar * gi_next - ai * gr_next
            g_re[row, :] = gr
            g_im[row, :] = gi
            return gr, gi

        gr0, gi0 = lax.fori_loop(0, ts, step, (car_re[0:1, :], car_im[0:1, :]))
        car_re[0:1, :] = gr0
        car_im[0:1, :] = gi0

        gr, gi = g_re[...], g_im[...]
        sr, si = sre_ref[...], sim_ref[...]
        has_prev = (s < ns - 1).astype(F32)
        row_id = lax.broadcasted_iota(jnp.int32, (ts, N_ST), 0)
        spr = jnp.where(row_id == 0, pre_ref[7:8, :] * has_prev, pltpu.roll(sr, 1, axis=0))
        spi = jnp.where(row_id == 0, pim_ref[7:8, :] * has_prev, pltpu.roll(si, 1, axis=0))
        grb, gib, ub = gr.astype(BF16), gi.astype(BF16), u.astype(BF16)
        srb, sib = sr.astype(BF16), si.astype(BF16)
        cat0 = lambda f: jnp.concatenate([f(j) for j in range(N_GB)], axis=0)
        cat1 = lambda f: jnp.concatenate([f(j) for j in range(N_GB)], axis=1)
        parts = [
            (dcre_ref, cat0(lambda j: _dot_tn(srb[:, _gb_s(j)], dyb[:, _gb_u(j)]))),
            (dcim_ref, cat0(lambda j: -_dot_tn(sib[:, _gb_s(j)], dyb[:, _gb_u(j)]))),
            (dbre_ref, cat0(lambda j: _dot_tn(ub[:, _gb_u(j)], grb[:, _gb_s(j)]))),
            (dbim_ref, cat0(lambda j: _dot_tn(ub[:, _gb_u(j)], gib[:, _gb_s(j)]))),
            (dare_ref, _colsum(gr * spr + gi * spi)),
            (daim_ref, _colsum(gi * spr - gr * spi)),
            (dd_ref, _colsum(dyv * u)),
        ]
        du = cat1(lambda j: _dot_nt(grb[:, _gb_s(j)], bre_ref[_gb_u(j), :])
                  + _dot_nt(gib[:, _gb_s(j)], bim_ref[_gb_u(j), :]))
        du_ref[...] = du + d_ref[...] * dyv

        @pl.when(first)
        def _set():
            for ref, val in parts:
                ref[...] = val

        @pl.when(jnp.logical_not(first))
        def _add():
            for ref, val in parts:
                ref[...] = ref[...] + val

    row = lambda b, s: (b * ns + (ns - 1 - s), 0)
    prev = lambda b, s: (jnp.maximum((b * ns + (ns - 1 - s)) * t8 - 1, 0), 0)
    whole = lambda b, s: (0, 0)
    acc_shapes = [(N_ST, GB_U), (N_ST, GB_U), (D_SSM, GB_ST), (D_SSM, GB_ST), (1, N_ST), (1, N_ST), (1, D_SSM)]
    return pl.pallas_call(
        kern, name="ssm_bwd", grid=(B, ns),
        in_specs=[pl.BlockSpec((ts, D_SSM), row), pl.BlockSpec((ts, D_SSM), row),
                  pl.BlockSpec((ts, N_ST), row), pl.BlockSpec((ts, N_ST), row),
                  pl.BlockSpec((8, N_ST), prev), pl.BlockSpec((8, N_ST), prev),
                  pl.BlockSpec((D_SSM, GB_ST), whole), pl.BlockSpec((D_SSM, GB_ST), whole),
                  pl.BlockSpec((N_ST, GB_U), whole), pl.BlockSpec((N_ST, GB_U), whole),
                  pl.BlockSpec((1, D_SSM), whole), pl.BlockSpec((1, N_ST), whole), pl.BlockSpec((1, N_ST), whole)],
        out_specs=[pl.BlockSpec((ts, D_SSM), row)] + [pl.BlockSpec(sh, whole) for sh in acc_shapes],
        out_shape=[jax.ShapeDtypeStruct((T, D_SSM), F32)] + [jax.ShapeDtypeStruct(sh, F32) for sh in acc_shapes],
        scratch_shapes=[pltpu.VMEM((8, N_ST), F32), pltpu.VMEM((8, N_ST), F32),
                        pltpu.VMEM((ts, N_ST), F32), pltpu.VMEM((ts, N_ST), F32)],
        compiler_params=_params(("arbitrary", "arbitrary")),
    )(dy, proj, s_re, s_im, s_re, s_im, b_re, b_im, c_re, c_im, d_vec, a_re, a_im)


def modulation_fwd(c_all, ada_w, ada_b, fada_w, fada_b):
    def kern(c_ref, w_ref, b_ref, fw_ref, fb_ref, cond_ref, mod_ref, fmod_ref):
        cv = c_ref[...]
        cond = (cv * _sigmoid(cv)).astype(BF16)
        cond_ref[...] = cond
        mod_ref[...] = _dot(cond, w_ref[...].astype(BF16)) + b_ref[...]
        fmod_ref[...] = _dot(cond, fw_ref[...].astype(BF16)) + fb_ref[...]

    n = c_all.shape[0]
    return pl.pallas_call(
        kern, name="modulation_fwd",
        out_shape=[jax.ShapeDtypeStruct(c_all.shape, BF16), jax.ShapeDtypeStruct((n, ada_w.shape[1]), F32),
                   jax.ShapeDtypeStruct((n, fada_w.shape[1]), F32)],
        compiler_params=_params(),
    )(c_all, ada_w, ada_b, fada_w, fada_b)


def modulation_bwd(cond_all, dmod, dfmod, dmod_all):
    def kern(c_ref, dm_ref, dfm_ref, all_ref, gw_ref, gfw_ref, gb_ref):
        cond = c_ref[...]
        gw_ref[...] = _dot_tn(cond, dm_ref[...].astype(BF16))
        gfw_ref[...] = _dot_tn(cond, dfm_ref[...].astype(BF16))
        gb_ref[...] = _colsum(all_ref[...].astype(F32))

    d = cond_all.shape[1]
    return pl.pallas_call(
        kern, name="modulation_bwd",
        out_shape=[jax.ShapeDtypeStruct((d, dmod.shape[1]), F32), jax.ShapeDtypeStruct((d, dfmod.shape[1]), F32),
                   jax.ShapeDtypeStruct((1, dmod_all.shape[1]), F32)],
        compiler_params=_params(),
    )(cond_all, dmod, dfmod, dmod_all)


def _adamw_update(p_ref, w_ref, m_ref, v_ref, g_o, d_o, m_o, v_o):
    g = p_ref[0].astype(F32)
    for i in range(1, p_ref.shape[0]):
        g = g + p_ref[i].astype(F32)
    m_new = ADAM_B1 * m_ref[...] + (1.0 - ADAM_B1) * g
    v_new = ADAM_B2 * v_ref[...] + (1.0 - ADAM_B2) * (g * g)
    m_hat = m_new / (1.0 - ADAM_B1 ** ADAM_STEP)
    v_hat = v_new / (1.0 - ADAM_B2 ** ADAM_STEP)
    g_o[...] = g
    d_o[...] = -ADAM_LR * (m_hat / (jnp.sqrt(v_hat) + ADAM_EPS) + ADAM_WD * w_ref[...])
    m_o[...] = m_new
    v_o[...] = v_new


def adamw_many(name, items):
    k = len(items)

    def kern(*refs):
        ins, outs = refs[:4 * k], refs[4 * k:]
        for i in range(k):
            _adamw_update(*ins[4 * i:4 * i + 4], *outs[4 * i:4 * i + 4])

    flat = [a for item in items for a in item]
    out_shape = [jax.ShapeDtypeStruct(item[1].shape, F32) for item in items for _ in range(4)]
    outs = pl.pallas_call(kern, name=name, out_shape=out_shape, compiler_params=_params())(*flat)
    return [tuple(outs[4 * i:4 * i + 4]) for i in range(k)]


def adamw(name, parts, w, m, v):
    n, R, C = parts.shape
    tr = R
    while n * tr * C * 4 > 4 * 1024 * 1024 and tr % 32 == 0:
        tr //= 2

    def kern(p_ref, w_ref, m_ref, v_ref, g_o, d_o, m_o, v_o):
        _adamw_update(p_ref, w_ref, m_ref, v_ref, g_o, d_o, m_o, v_o)

    blk = pl.BlockSpec((tr, C), lambda i: (i, 0))
    shp = jax.ShapeDtypeStruct((R, C), F32)
    return pl.pallas_call(
        kern, name=name, grid=(R // tr,),
        in_specs=[pl.BlockSpec((n, tr, C), lambda i: (0, i, 0)), blk, blk, blk],
        out_specs=[blk] * 4, out_shape=[shp] * 4,
        compiler_params=_params(("parallel",)),
    )(parts, w, m, v)


def _cols_from_gathered(g):
    return jnp.transpose(g, (1, 0, 2)).reshape(g.shape[1], N_DEV * g.shape[2])


def _cols_to_shards(w):
    k, n8 = w.shape
    return jnp.transpose(w.reshape(k, N_DEV, n8 // N_DEV), (1, 0, 2))


def _pad_w_in(w):
    z = functools.partial(jnp.zeros, dtype=w.dtype)
    k = w.shape[0]
    cut = D_SSM + Q_LORA + KV_LORA
    return jnp.concatenate([w[:, :cut], z((k, 64)), w[:, cut:], z((k, 32))], axis=1)


def _unpad_w_in(w):
    cut = D_SSM + Q_LORA + KV_LORA
    return jnp.concatenate([w[:, :cut], w[:, cut + 64:cut + 96]], axis=1)


def _pad_w_uq(w):
    k = w.shape[0]
    w3 = w.reshape(k, N_HEADS, QK_NOPE + QK_ROPE)
    return jnp.pad(w3, ((0, 0), (0, 0), (0, HEAD_PAD - QK_NOPE - QK_ROPE))).reshape(k, QK_W)


def _unpad_w_uq(w):
    k = w.shape[0]
    return w.reshape(k, N_HEADS, HEAD_PAD)[:, :, :QK_NOPE + QK_ROPE].reshape(k, N_HEADS * (QK_NOPE + QK_ROPE))


def _pad_w_ukv(w):
    k = w.shape[0]
    w3 = w.reshape(k, N_HEADS, QK_NOPE + V_HEAD)
    kpart = jnp.pad(w3[:, :, :QK_NOPE], ((0, 0), (0, 0), (0, HEAD_PAD - QK_NOPE))).reshape(k, QK_W)
    vpart = w3[:, :, QK_NOPE:].reshape(k, V_W)
    return jnp.concatenate([kpart, vpart], axis=1)


def _unpad_w_ukv(w):
    k = w.shape[0]
    kpart = w[:, :QK_W].reshape(k, N_HEADS, HEAD_PAD)[:, :, :QK_NOPE]
    vpart = w[:, QK_W:].reshape(k, N_HEADS, V_HEAD)
    return jnp.concatenate([kpart, vpart], axis=2).reshape(k, N_HEADS * (QK_NOPE + V_HEAD))


def _block_diag(blocks):
    g, r, c = blocks.shape
    b4 = blocks.reshape(N_GB, GB, r, c)
    keep = jnp.eye(GB, dtype=bool)[None, :, None, :, None]
    return jnp.where(keep, b4[:, :, :, None, :], 0).reshape(g * r, GB * c)


def _block_diag_take(stacked, r, c):
    d5 = stacked.reshape(N_GB, GB, r, GB, c)
    idx = jnp.arange(GB)
    return jnp.transpose(d5[:, idx, :, idx, :], (1, 0, 2, 3)).reshape(N_GROUPS, r, c)


SMALL = ["norm1_g", "ssm_lambda_re", "ssm_lambda_im", "ssm_b_re", "ssm_b_im", "ssm_c_re", "ssm_c_im",
         "ssm_d", "ssm_log_dt", "q_norm_g", "kv_norm_g", "ssm_out_g", "attn_out_g", "norm2_g", "final_norm_g"]
BIASES = ["ada_b", "final_ada_b"]


def _dense_2d(shape):
    dims = [d for d in shape[1:]] if len(shape) > 1 and shape[0] == 1 else list(shape)
    if len(dims) == 1:
        return (1, dims[0])
    return (dims[0], int(np.prod(dims[1:])))


def kernel(x, c, positions, ada_w, ada_b, norm1_g, w_in, ssm_lambda_re, ssm_lambda_im, ssm_b_re, ssm_b_im, ssm_c_re, ssm_c_im, ssm_d, ssm_log_dt, w_glu, q_norm_g, w_uq, kv_norm_g, w_ukv, ssm_out_g, attn_out_g, w_out, norm2_g, w_ff1, w_ff2, final_ada_w, final_ada_b, final_norm_g, loss_target, m_ada_w, m_ada_b, m_norm1_g, m_w_in, m_ssm_lambda_re, m_ssm_lambda_im, m_ssm_b_re, m_ssm_b_im, m_ssm_c_re, m_ssm_c_im, m_ssm_d, m_ssm_log_dt, m_w_glu, m_q_norm_g, m_w_uq, m_kv_norm_g, m_w_ukv, m_ssm_out_g, m_attn_out_g, m_w_out, m_norm2_g, m_w_ff1, m_w_ff2, m_final_ada_w, m_final_ada_b, m_final_norm_g, v_ada_w, v_ada_b, v_norm1_g, v_w_in, v_ssm_lambda_re, v_ssm_lambda_im, v_ssm_b_re, v_ssm_b_im, v_ssm_c_re, v_ssm_c_im, v_ssm_d, v_ssm_log_dt, v_w_glu, v_q_norm_g, v_w_uq, v_kv_norm_g, v_w_ukv, v_ssm_out_g, v_attn_out_g, v_w_out, v_norm2_g, v_w_ff1, v_w_ff2, v_final_ada_w, v_final_ada_b, v_final_norm_g):
    given = dict(locals())
    B, S, D = x.shape
    T = B * S
    tm = min(256, S)
    me = 4 * lax.axis_index("x") + 2 * lax.axis_index("y") + lax.axis_index("c")

    big = ["w_in", "w_glu", "w_uq", "w_ukv", "w_out", "w_ff1", "w_ff2"]
    shards = {n: given[n][0] for n in big}
    early, late = ["w_in", "w_uq", "w_ukv"], ["w_glu", "w_out", "w_ff1", "w_ff2"]
    c_g, w_in_g = all_gather("gather_first_weights", [c, shards["w_in"].astype(BF16)])
    c_all = c_g.reshape(N_DEV * B, D)
    w_in_p = _pad_w_in(_cols_from_gathered(w_in_g))

    n_mod = ada_w.shape[2]
    n_fmod = final_ada_w.shape[1]
    ada_b_sh = lax.dynamic_slice_in_dim(ada_b, me * n_mod, n_mod, axis=1)
    fada_b_sh = lax.dynamic_slice_in_dim(final_ada_b[None, :], me * n_fmod, n_fmod, axis=1)
    cond_all, mod_sh, fmod_sh = modulation_fwd(c_all, ada_w[0], ada_b_sh, final_ada_w, fada_b_sh)
    mod_g, fmod_g = all_gather("gather_modulation", [mod_sh, fmod_sh])
    mod_mine = lax.dynamic_slice_in_dim(_cols_from_gathered(mod_g), me * B, B, axis=0)
    fmod_mine = lax.dynamic_slice_in_dim(_cols_from_gathered(fmod_g), me * B, B, axis=0)
    shift1, scale1, gate1, shift2, scale2, gate2 = [mod_mine[:, None, i * D:(i + 1) * D] for i in range(6)]
    fshift, fscale = fmod_mine[:, None, :D], fmod_mine[:, None, D:]

    x2d = x.reshape(T, D)
    tgt2d = loss_target.reshape(T, D)
    pos = positions.astype(F32).reshape(T, 1)
    invf, m_lo, m_hi = _rope_consts()
    rope_mask = m_lo + m_hi

    def fwd_in(rows, bv, vecs, res):
        (xv,), (sc, sh), (g1,), (w,) = rows, bv, vecs, res
        r, xh = _rms_stats(xv)
        h = xh * g1 * (1.0 + sc) + sh
        return [_dot(h.astype(BF16), w[...])], [], []

    qkv_gather = Exchange([shards["w_uq"].astype(BF16), shards["w_ukv"].astype(BF16)], [GATHER, GATHER])
    (proj,), _, _, (w_uq_g, w_ukv_g) = row_call(
        "fwd_in", fwd_in, B, S, tm, [x2d], [scale1, shift1], [norm1_g], [w_in_p], [(IN_PAD, F32)], [], [],
        side=qkv_gather)
    w_uq_p = _pad_w_uq(_cols_from_gathered(w_uq_g))
    w_ukv_p = _pad_w_ukv(_cols_from_gathered(w_ukv_g))

    def fwd_qkv(rows, bv, vecs, res):
        (pr, ps), (gq, gkv, fr, lo, hi), (wq, wkv) = rows, vecs, res
        cs, s_lo, s_hi = _rope_tables(ps, fr, lo, hi)
        _, qh = _rms_stats(pr[:, D_SSM:D_SSM + Q_LORA])
        _, kvh = _rms_stats(pr[:, D_SSM + Q_LORA:D_SSM + Q_LORA + KV_LORA])
        qf = _dot((qh * gq).astype(BF16), wq[...])
        kvf = _dot((kvh * gkv).astype(BF16), wkv[...])
        k_rope = _rope_tile(pr[:, IN_PAD - HEAD_PAD:], cs, s_lo, s_hi, 1.0)
        q_t, k_t = [], []
        for h in range(N_HEADS):
            hs = slice(h * HEAD_PAD, (h + 1) * HEAD_PAD)
            q_t.append(_rope_tile(qf[:, hs], cs, s_lo, s_hi, 1.0))
            k_t.append(kvf[:, hs] + k_rope)
        kf, vf = jnp.concatenate(k_t, axis=1), kvf[:, QK_W:]
        return [jnp.concatenate(q_t, axis=1), kf, vf, t_val(kf), t_val(vf)[None], cs, s_lo, s_hi], [], []

    ns = S // tm
    col_tile = lambda b, s: (0, b * ns + s)
    t_out = lambda w: ((w, T), BF16, (w, tm), col_tile)
    t_val = lambda v: v.astype(BF16).T
    (q_r, k_r, v_r, k_t, v_t, rope_cs, rope_lo, rope_hi), _, _ = row_call(
        "fwd_qkv", fwd_qkv, B, S, tm, [proj, pos], [], [q_norm_g, kv_norm_g, invf, m_lo, m_hi],
        [w_uq_p, w_ukv_p],
        [(QK_W, BF16), (QK_W, BF16), (V_W, BF16), ((QK_W, T), BF16, (QK_W, tm), col_tile),
         ((B, V_W, S), BF16, (1, V_W, tm), lambda b, s: (b, 0, s)), (HEAD_PAD, F32), (HEAD_PAD, F32),
         (HEAD_PAD, F32)], [], [])
    late_gather = Exchange([shards[n].astype(BF16) for n in late], [GATHER] * len(late))
    y_attn, lse, *late_g = attention_fwd(q_r.reshape(B, S, QK_W), k_r.reshape(B, S, QK_W), v_t, B, S, tm,
                                         late_gather)
    y_attn = y_attn.reshape(T, V_W)
    lse = jnp.transpose(lse.reshape(B, N_HEADS, 1, S), (1, 2, 0, 3)).reshape(N_HEADS, 1, T)
    gw = dict(zip(late, late_g))
    w_glu_f = _cols_from_gathered(gw["w_glu"])
    w_out_f = gw["w_out"].reshape(-1, D)
    w_ff1_f = _cols_from_gathered(gw["w_ff1"])
    w_ff2_f = gw["w_ff2"].reshape(-1, D)

    rep = lambda a: jnp.repeat(a, GROUP, axis=0)
    lam_rep = [rep(ssm_lambda_re[0]), rep(ssm_lambda_im[0]),
               rep(jnp.broadcast_to(ssm_log_dt[0][:, None], (N_GROUPS, N_STATE)))]
    bt = lambda a: jnp.transpose(a, (0, 2, 1)).reshape(D_SSM, N_STATE)
    reps = lam_rep + [bt(ssm_b_re[0]), bt(ssm_b_im[0])]
    a_re_rep, a_im_rep, bb_re, bb_im = ssm_params_fwd(reps)
    a_re = a_re_rep.reshape(N_GROUPS, GROUP, N_STATE)[:, 0, :].reshape(1, N_ST)
    a_im = a_im_rep.reshape(N_GROUPS, GROUP, N_STATE)[:, 0, :].reshape(1, N_ST)
    bbd_re = _block_diag(bb_re.reshape(N_GROUPS, GROUP, N_STATE)).astype(BF16)
    bbd_im = _block_diag(bb_im.reshape(N_GROUPS, GROUP, N_STATE)).astype(BF16)
    cbd_re = _block_diag(jnp.transpose(ssm_c_re[0], (0, 2, 1))).astype(BF16)
    cbd_im = _block_diag(jnp.transpose(ssm_c_im[0], (0, 2, 1))).astype(BF16)
    d_vec = ssm_d.reshape(1, D_SSM)
    st_re, st_im, y_pre = ssm_fwd(proj, bbd_re, bbd_im, cbd_re, cbd_im, d_vec, a_re, a_im, B, S, tm)

    def fwd_mix(rows, bv, vecs, res):
        (yp, ya, xv), (g1v,), (gso, gao), (wg, wo) = rows, bv, vecs, res
        z = _dot(_gelu(yp).astype(BF16), wg[...])
        y_ssm = z[:, :D_SSM] * _sigmoid(z[:, D_SSM:])
        _, sh = _rms_stats(y_ssm)
        _, ah = _rms_stats(ya)
        o = _dot((sh * gso).astype(BF16), wo[0:D_SSM, :]) + _dot((ah * gao).astype(BF16), wo[D_SSM:, :])
        return [z, o, xv + g1v * o], [], []

    (z, o, x2), _, _ = row_call("fwd_mix", fwd_mix, B, S, tm, [y_pre, y_attn, x2d], [gate1],
                                [ssm_out_g, attn_out_g], [w_glu_f, w_out_f],
                                [(2 * D_SSM, F32), (D, BF16), (D, F32)], [], [])

    def final_out(x3, gf, fsc, fsh):
        r3, xh3 = _rms_stats(x3)
        n3 = xh3 * gf
        return r3, xh3, n3, n3 * (1.0 + fsc) + fsh

    def mlp_norm(xv, g2, sc, sh):
        r2, xh2 = _rms_stats(xv)
        n2 = xh2 * g2
        return r2, xh2, n2, n2 * (1.0 + sc) + sh

    def fwd_mlp(rows, bv, vecs, res):
        (xv, tg), (sc, sh, g2v, fsc, fsh), (g2, gf), (w1, w2) = rows, bv, vecs, res
        _, _, _, h2 = mlp_norm(xv, g2, sc, sh)
        ra = jnp.maximum(_dot(h2.astype(BF16), w1[...]), 0.0)
        ff = _dot((ra * ra).astype(BF16), w2[...])
        x3 = xv + g2v * ff
        _, _, _, out = final_out(x3, gf, fsc, fsh)
        err = out - tg
        loss = 0.5 * jnp.sum(jnp.mean(err * err, axis=-1, keepdims=True), axis=0, keepdims=True)
        return [ff, x3, ra], [], [jnp.broadcast_to(loss, (1, 128))]

    fnorm_g = final_norm_g[None, :]
    (ff, x3, ra_b), _, (loss_acc,) = row_call(
        "fwd_mlp", fwd_mlp, B, S, tm, [x2, tgt2d], [scale2, shift2, gate2, fscale, fshift], [norm2_g, fnorm_g],
        [w_ff1_f, w_ff2_f], [(D, BF16), (D, F32), (4 * D, BF16)], [], [(1, 128)])
    loss = lax.psum(loss_acc[0, 0], ("x", "y", "c"))

    def bwd_mlp(rows, bv, vecs, res):
        (x3v, tg, x2v, ffv, rav), (sc, sh, g2v, fsc, fsh), (g2, gf), (w1, w2) = rows, bv, vecs, res
        r3, xh3, n3, out = final_out(x3v, gf, fsc, fsh)
        dout = (out - tg) * (1.0 / D)
        dn3 = dout * (1.0 + fsc)
        dx3 = _rms_bwd(dn3 * gf, xh3, r3)
        dff = dx3 * g2v
        r2, xh2, n2, h2 = mlp_norm(x2v, g2, sc, sh)
        ra = rav.astype(F32)
        dffb = dff.astype(BF16)
        da = _dot_nt(dffb, w2[...]) * (2.0 * ra)
        dab = da.astype(BF16)
        dh2 = _dot_nt(dab, w1[...])
        dn2 = dh2 * (1.0 + sc)
        dx2 = dx3 + _rms_bwd(dn2 * g2, xh2, r2)
        return ([dx2, t_val(h2), t_val(ra * ra), dab, dffb],
                [_colsum(dout), _colsum(dout * n3), _colsum(dx3 * ffv), _colsum(dh2), _colsum(dh2 * n2)],
                [_colsum(dn3 * xh3), _colsum(dn2 * xh2)])

    ((dx2, h2b, fb, dab, dffb), (d_fshift, d_fscale, d_gate2, d_shift2, d_scale2), (d_gf, d_g2)) = row_call(
        "bwd_mlp", bwd_mlp, B, S, tm, [x3, tgt2d, x2, ff, ra_b], [scale2, shift2, gate2, fscale, fshift],
        [norm2_g, fnorm_g], [w_ff1_f, w_ff2_f],
        [(D, F32), t_out(D), t_out(4 * D), (4 * D, BF16), (D, BF16)], [D] * 5, [(1, D), (1, D)])
    g_w_ff1 = matmul_nn("grad_w_ff1", h2b, dab)
    g_w_ff2 = matmul_nn("grad_w_ff2", fb, dffb)

    def bwd_mix(rows, bv, vecs, res):
        (dxv, ov, zv, ya, yp), (g1v,), (gso, gao), (wo, wg) = rows, bv, vecs, res
        do = (dxv * g1v).astype(BF16)
        dyn = _dot_nt(do, wo[...])
        z1, sg = zv[:, :D_SSM], _sigmoid(zv[:, D_SSM:])
        y_ssm = z1 * sg
        rs, sh = _rms_stats(y_ssm)
        ra, ah = _rms_stats(ya)
        dyn_s, dyn_a = dyn[:, :D_SSM], dyn[:, D_SSM:]
        dy_ssm = _rms_bwd(dyn_s * gso, sh, rs)
        dy_attn = _rms_bwd(dyn_a * gao, ah, ra)
        dz = jnp.concatenate([dy_ssm * sg, dy_ssm * z1 * sg * (1.0 - sg)], axis=1).astype(BF16)
        dy_pre = _dot_nt(dz, wg[...]) * _gelu_grad(yp)
        yn = jnp.concatenate([sh * gso, ah * gao], axis=1)
        delta = jnp.sum((dy_attn * ya).T.reshape(N_HEADS, V_HEAD, tm), axis=1, keepdims=True)
        return ([do, t_val(yn), dz, t_val(_gelu(yp)), dy_attn, dy_pre, delta], [_colsum(dxv * ov)],
                [_colsum(dyn_s * sh), _colsum(dyn_a * ah)])

    ((dob, ynb, dzb, ygb, dy_attn, dy_pre, delta), (d_gate1,), (d_gso, d_gao)) = row_call(
        "bwd_mix", bwd_mix, B, S, tm, [dx2, o, z, y_attn, y_pre], [gate1], [ssm_out_g, attn_out_g],
        [w_out_f, w_glu_f],
        [(D, BF16), t_out(D), (2 * D_SSM, BF16), t_out(D_SSM), (V_W, F32), (D_SSM, F32),
         ((N_HEADS, 1, T), F32, (N_HEADS, 1, tm), lambda b, s: (0, 0, b * ns + s))],
        [D], [(1, D_SSM), (1, V_W)])
    g_w_out = matmul_nn("grad_w_out", ynb, dob)
    g_w_glu = matmul_nn("grad_w_glu", ygb, dzb)

    grads_a = Exchange([g_w_glu, g_w_out.reshape(N_DEV, -1, D), g_w_ff1, g_w_ff2.reshape(N_DEV, -1, D)],
                       [w_glu.shape[2], None, w_ff1.shape[2], None])
    dq_t, dk_r, dv_r, *parts_a = attention_bwd(q_r, k_r, k_t, v_r, dy_attn, lse, delta, B, S, tm, grads_a)
    parts = dict(zip(late, parts_a))
    dq_t = dq_t.reshape(B, QK_W, S)

    def bwd_qkv(rows, bv, vecs, res):
        (dqt, dkv, dvv, pr, cs, s_lo, s_hi), (gq, gkv, rmask), (wq, wkv) = rows, vecs, res
        dqv = dqt[0].T
        dq_t = []
        dk_rope = jnp.zeros((dkv.shape[0], HEAD_PAD), F32)
        for h in range(N_HEADS):
            hs = slice(h * HEAD_PAD, (h + 1) * HEAD_PAD)
            dq_t.append(_rope_tile(dqv[:, hs], cs, s_lo, s_hi, -1.0))
            dk_rope = dk_rope + dkv[:, hs]
        dk_rope = _rope_tile(dk_rope * rmask, cs, s_lo, s_hi, -1.0)
        dqb = jnp.concatenate(dq_t, axis=1).astype(BF16)
        dkvb = jnp.concatenate([dkv, dvv], axis=1).astype(BF16)
        rq, qh = _rms_stats(pr[:, D_SSM:D_SSM + Q_LORA])
        rk, kvh = _rms_stats(pr[:, D_SSM + Q_LORA:D_SSM + Q_LORA + KV_LORA])
        dqn = _dot_nt(dqb, wq[...])
        dkvn = _dot_nt(dkvb, wkv[...])
        dpa = jnp.concatenate([_rms_bwd(dqn * gq, qh, rq), _rms_bwd(dkvn * gkv, kvh, rk), dk_rope], axis=1)
        return [dpa, t_val(qh * gq), t_val(kvh * gkv), dqb, dkvb], [], [_colsum(dqn * qh), _colsum(dkvn * kvh)]

    ((dpa, qnb, kvnb, dqb, dkvb), _, (d_gq, d_gkv)) = row_call(
        "bwd_qkv", bwd_qkv, B, S, tm,
        [(dq_t, (1, QK_W, tm), lambda b, s: (b, 0, s)), dk_r, dv_r, proj, rope_cs, rope_lo, rope_hi], [],
        [q_norm_g, kv_norm_g, rope_mask], [w_uq_p, w_ukv_p],
        [(Q_LORA + KV_LORA + HEAD_PAD, F32), t_out(Q_LORA), t_out(KV_LORA), (QK_W, BF16), (QK_W + V_W, BF16)],
        [], [(1, Q_LORA), (1, KV_LORA)])
    g_w_uq = _unpad_w_uq(matmul_nn("grad_w_uq", qnb, dqb))
    g_w_ukv = _unpad_w_ukv(matmul_nn("grad_w_ukv", kvnb, dkvb))

    du, dc_re_d, dc_im_d, db_re_d, db_im_d, da_re, da_im, d_d = ssm_bwd(
        dy_pre, proj, st_re, st_im, bbd_re, bbd_im, cbd_re, cbd_im, d_vec, a_re, a_im, B, S, tm)
    place = lambda a: jnp.zeros((N_GROUPS, GROUP, N_STATE), F32).at[:, 0, :].set(
        a.reshape(N_GROUPS, N_STATE)).reshape(D_SSM, N_STATE)
    cots = [place(da_re), place(da_im),
            _block_diag_take(db_re_d, GROUP, N_STATE).reshape(D_SSM, N_STATE),
            _block_diag_take(db_im_d, GROUP, N_STATE).reshape(D_SSM, N_STATE)]
    g_lam_re, g_lam_im, g_log_dt, g_bt_re, g_bt_im = ssm_params_bwd(reps, cots)
    unbt = lambda a: jnp.transpose(a.reshape(N_GROUPS, GROUP, N_STATE), (0, 2, 1))
    g_c_re = jnp.transpose(_block_diag_take(dc_re_d, N_STATE, GROUP), (0, 2, 1))
    g_c_im = jnp.transpose(_block_diag_take(dc_im_d, N_STATE, GROUP), (0, 2, 1))

    def bwd_in(rows, bv, vecs, res):
        (duv, dpav, xv, dx2v), (sc, sh), (g1,), (w,) = rows, bv, vecs, res
        dproj = jnp.concatenate([duv, dpav], axis=1).astype(BF16)
        dh = _dot_nt(dproj, w[...])
        r, xh = _rms_stats(xv)
        n1 = xh * g1
        dn1 = dh * (1.0 + sc)
        dx = dx2v + _rms_bwd(dn1 * g1, xh, r)
        return [dx, t_val(n1 * (1.0 + sc) + sh), dproj], [_colsum(dh), _colsum(dh * n1)], [_colsum(dn1 * xh)]

    small_part = {
        "ssm_lambda_re": g_lam_re,
        "ssm_lambda_im": g_lam_im, "ssm_b_re": unbt(g_bt_re), "ssm_b_im": unbt(g_bt_im), "ssm_c_re": g_c_re,
        "ssm_c_im": g_c_im, "ssm_d": d_d, "ssm_log_dt": g_log_dt, "q_norm_g": d_gq, "kv_norm_g": d_gkv,
        "ssm_out_g": d_gso, "attn_out_g": d_gao, "norm2_g": d_g2, "final_norm_g": d_gf}
    dense_bf16 = lambda n, g: g.reshape(_dense_2d(given[n].shape)).astype(BF16)
    ready = [n for n in SMALL if n in small_part]
    grads_b = Exchange([dense_bf16(n, small_part[n]) for n in ready] + [_cols_to_shards(g_w_uq), g_w_ukv],
                       [GATHER] * len(ready) + [None, w_ukv.shape[2]])
    ((grad_x, h1b, dprojb), (d_shift1, d_scale1), (d_g1,)) = row_call(
        "bwd_in", bwd_in, B, S, tm, [du, dpa, x2d, dx2], [scale1, shift1], [norm1_g], [w_in_p],
        [(D, F32), t_out(D), (IN_PAD, BF16)], [D, D], [(1, D)])
    g_w_in_p, *side_b = matmul_nn("grad_w_in", h1b, dprojb, side=grads_b)
    small_parts = dict(zip(ready, side_b[:len(ready)]))
    parts["w_uq"], parts["w_ukv"] = side_b[len(ready):]
    g_w_in = _unpad_w_in(g_w_in_p)

    dmod = jnp.concatenate([d_shift1, d_scale1, d_gate1, d_shift2, d_scale2, d_gate2, d_fshift, d_fscale],
                           axis=2).reshape(B, 8 * D)
    dmod_g, small_parts["norm1_g"], parts["w_in"] = exchange(
        "exchange_last_grads", [dmod.astype(BF16), dense_bf16("norm1_g", d_g1), _cols_to_shards(g_w_in)],
        [GATHER, GATHER, None])
    small_g = [small_parts[n] for n in SMALL]
    dmod_all = dmod_g.reshape(N_DEV * B, 8 * D)
    dm_sh = lax.dynamic_slice_in_dim(dmod_all[:, :6 * D], me * n_mod, n_mod, axis=1)
    dfm_sh = lax.dynamic_slice_in_dim(dmod_all[:, 6 * D:], me * n_fmod, n_fmod, axis=1)
    g_ada_w, g_fada_w, g_biases = modulation_bwd(cond_all, dm_sh, dfm_sh, dmod_all)
    parts["ada_w"] = g_ada_w[None]
    parts["final_ada_w"] = g_fada_w[None]

    res_g, res_d, res_m, res_v = {}, {}, {}, {}
    for n in ["ada_w"] + big + ["final_ada_w"]:
        w_full = given[n]
        w2 = w_full.reshape(w_full.shape[-2], w_full.shape[-1])
        out = adamw("adamw_" + n, parts[n], w2, given["m_" + n].reshape(w2.shape), given["v_" + n].reshape(w2.shape))
        res_g[n], res_d[n], res_m[n], res_v[n] = [a.reshape(w_full.shape) for a in out]
    bias_g = {"ada_b": g_biases[None, :, :6 * D], "final_ada_b": g_biases[None, :, 6 * D:]}
    names = SMALL + BIASES
    items = []
    for n, g_parts in zip(names, small_g + [bias_g[b] for b in BIASES]):
        dense = _dense_2d(given[n].shape)
        items.append((g_parts, given[n].reshape(dense), given["m_" + n].reshape(dense), given["v_" + n].reshape(dense)))
    for n, out in zip(names, adamw_many("adamw_small", items)):
        res_g[n], res_d[n], res_m[n], res_v[n] = [a.reshape(given[n].shape) for a in out]

    order = ["ada_w", "ada_b", "norm1_g", "w_in", "ssm_lambda_re", "ssm_lambda_im", "ssm_b_re", "ssm_b_im",
             "ssm_c_re", "ssm_c_im", "ssm_d", "ssm_log_dt", "w_glu", "q_norm_g", "w_uq", "kv_norm_g", "w_ukv",
             "ssm_out_g", "attn_out_g", "w_out", "norm2_g", "w_ff1", "w_ff2", "final_ada_w", "final_ada_b",
             "final_norm_g"]
    return (loss, grad_x.reshape(B, S, D), *[res_g[n] for n in order], *[res_d[n] for n in order],
            *[res_m[n] for n in order], *[res_v[n] for n in order])
```
